```python
import math
import jax
import jax.numpy as jnp
from jax import lax
import numpy as np

D_MODEL = 1024
BATCH = 2
SEQ = 16384
DEPTH = 4

CTX_LEN = 256
GRID_W = 64
N_MIXERS = 3
N_A = (DEPTH + 2) // 3
N_B = (DEPTH + 1) // 3
N_C = DEPTH // 3
EPS = 1e-6
N_MOD = 6

DN_HEADS = 8
DN_DK = D_MODEL // DN_HEADS
DN_DV = D_MODEL // DN_HEADS
DN_CHUNK = 64
DN_IN = 4 * D_MODEL + 4 * DN_HEADS

HY_ORDER = 2
HY_BANDS = 16
HY_EMB = 1 + 2 * HY_BANDS
HY_HIDDEN = 64
HY_TARGET = 1e-2
HY_FAST = 0.3
HY_SLOW = 1.5

N_EXPERTS = 32
N_GROUPS = 8
EXPERTS_PER_GROUP = N_EXPERTS // N_GROUPS
GROUP_SCORE_K = 2
TOP_K = 2
D_EXPERT = 512
MOE_BLOCK = 512

kernel_name = 'hybrid_deltanet_hyena_shortconv_grouped_moe_trunk'


def rmsnorm(x, gain):
    x32 = x.astype(jnp.float32)
    y = x32 * lax.rsqrt(jnp.mean(x32 * x32, axis=-1, keepdims=True) + EPS)
    return (y * gain.astype(jnp.float32)).astype(x.dtype)


def modulate(x, gain, shift, scale):
    return rmsnorm(x, gain) * (1 + scale) + shift


def l2norm(t):
    return t * lax.rsqrt(jnp.sum(t * t, axis=-1, keepdims=True) + EPS)


def short_conv(x, w, on_grid):
    b, l, ch = x.shape
    xs = x.reshape(b, l // GRID_W, GRID_W, ch) if on_grid else x.reshape(b, 1, l, ch)
    n = xs.shape[2]
    xp = jnp.pad(xs, ((0, 0), (0, 0), (1, 1), (0, 0)))
    y = w[0] * xp[:, :, 0:n] + w[1] * xp[:, :, 1:n + 1] + w[2] * xp[:, :, 2:n + 2]
    return y.reshape(b, l, ch)


def gated_delta_chunked(q, k, v, g, beta, s0):
    b, h, l, dk = q.shape
    dv = v.shape[-1]
    c = DN_CHUNK
    n = l // c
    q = q.reshape(b, h, n, c, dk)
    k = k.reshape(b, h, n, c, dk)
    v = v.reshape(b, h, n, c, dv)
    g = jnp.cumsum(g.reshape(b, h, n, c), axis=-1)
    beta = beta.reshape(b, h, n, c, 1)
    pos = jnp.arange(c)
    incl = pos[:, None] >= pos[None, :]
    strict = pos[:, None] > pos[None, :]
    decay = jnp.exp(jnp.where(incl, g[..., :, None] - g[..., None, :], -jnp.inf))
    kb = k * beta
    a_mat = jnp.einsum('bhnid,bhnjd->bhnij', kb, k) * jnp.where(strict, decay, 0.0)
    rhs = jnp.concatenate([v * beta, kb * jnp.exp(g)[..., None]], axis=-1)
    sol = lax.linalg.triangular_solve(a_mat + jnp.eye(c, dtype=a_mat.dtype), rhs,
                                      left_side=True, lower=True, unit_diagonal=True)
    u, w = sol[..., :dv], sol[..., dv:]
    attn = jnp.einsum('bhnid,bhnjd->bhnij', q, k) * decay
    g_last = g[..., -1:]
    q_dec = q * jnp.exp(g)[..., None]
    k_dec = k * jnp.exp(g_last - g)[..., None]

    def step(s, inp):
        qd, kd, uu, ww, at, gl = inp
        v_new = uu - jnp.einsum('bhck,bhkv->bhcv', ww, s)
        o = jnp.einsum('bhck,bhkv->bhcv', qd, s) + jnp.einsum('bhcs,bhsv->bhcv', at, v_new)
        s = s * jnp.exp(gl)[..., None] + jnp.einsum('bhck,bhcv->bhkv', kd, v_new)
        return s, o

    xs = tuple(jnp.moveaxis(t, 2, 0) for t in (q_dec, k_dec, u, w, attn, g_last))
    s_final, o = lax.scan(step, s0, xs)
    o = jnp.moveaxis(o, 0, 2).reshape(b, h, l, dv)
    return o, s_final


def deltanet_mixer(h_ctx, h_lat, w_in, conv_w, a_log, dt_bias, out_norm, w_out, need_ctx_out):
    d = D_MODEL
    nh = DN_HEADS
    f32 = jnp.float32

    def project(h, on_grid):
        b, l, _ = h.shape
        p = h @ w_in
        qkv = jax.nn.silu(short_conv(p[..., :3 * d], conv_w, on_grid))
        z = p[..., 3 * d:4 * d]
        a = p[..., 4 * d:4 * d + 2 * nh].reshape(b, l, 2, nh).astype(f32)
        bb = p[..., 4 * d + 2 * nh:].reshape(b, l, 2, nh).astype(f32)

        def heads(t):
            return jnp.transpose(t.reshape(b, l, nh, -1), (0, 2, 1, 3)).astype(f32)

        q, k, v = (heads(t) for t in jnp.split(qkv, 3, axis=-1))
        q = l2norm(q) * DN_DK ** -0.5
        k = l2norm(k)
        g = -jnp.exp(a_log.astype(f32)) * jax.nn.softplus(a + dt_bias.astype(f32))
        g = jnp.transpose(g, (2, 0, 3, 1))
        beta = jnp.transpose(jax.nn.sigmoid(bb), (2, 0, 3, 1))
        return q, k, v, g, beta, z

    def scan_both(q, k, v, g, beta, s_f, s_b):
        o_f, s_f = gated_delta_chunked(q, k, v, g[0], beta[0], s_f)
        rev = lambda t: jnp.flip(t, axis=2)
        o_b, s_b = gated_delta_chunked(rev(q), rev(k), rev(v), rev(g[1]), rev(beta[1]), s_b)
        return o_f + rev(o_b), s_f, s_b

    def finish(o, z, dtype):
        b, _, l, _ = o.shape
        o = jnp.transpose(o, (0, 2, 1, 3))
        o = o * lax.rsqrt(jnp.mean(o * o, axis=-1, keepdims=True) + EPS) * out_norm.astype(f32)
        o = o * jax.nn.silu(z.reshape(b, l, nh, DN_DV).astype(f32))
        return o.reshape(b, l, d).astype(dtype) @ w_out

    qc, kc, vc, gc, bc, zc = project(h_ctx, False)
    s0 = jnp.zeros((h_ctx.shape[0], nh, DN_DK, DN_DV), f32)
    o_c, s_f, s_b = scan_both(qc, kc, vc, gc, bc, s0, s0)
    ql, kl, vl, gla, bl, zl = project(h_lat, True)
    o_l, _, _ = scan_both(ql, kl, vl, gla, bl, s_f, s_b)
    y_lat = finish(o_l, zl, h_lat.dtype)
    y_ctx = finish(o_c, zc, h_ctx.dtype) if need_ctx_out else None
    return y_ctx, y_lat


def hyena_filters(l, w1, b1, freq, w2, b2, w3):
    f32 = jnp.float32
    pos = jnp.arange(l, dtype=f32)[:, None]
    t = pos / max(l - 1, 1)
    bands = jnp.linspace(1e-4, HY_BANDS - 1, HY_BANDS, dtype=f32)[None, :]
    ang = (2 * math.pi / l) * pos * bands
    feat = jnp.concatenate([t, jnp.cos(ang), -jnp.sin(ang)], axis=-1)
    fr = freq.astype(f32)
    hdn = jnp.sin(fr * (feat @ w1.astype(f32) + b1.astype(f32)))
    hdn = jnp.sin(fr * (hdn @ w2.astype(f32) + b2.astype(f32)))
    filt = (hdn @ w3.astype(f32)).reshape(l, HY_ORDER, 2, D_MODEL)
    deltas = jnp.abs(jnp.linspace(math.log(HY_TARGET) / HY_SLOW, math.log(HY_TARGET) / HY_FAST,
                                  D_MODEL, dtype=f32))
    window = jnp.exp(-t * deltas[None, :])
    return filt * window[:, None, None, :]


def two_sided_fftconv(u, h_fwd, h_bwd):
    l = u.shape[1]
    k = jnp.concatenate([h_fwd, jnp.zeros_like(h_fwd[:1]), jnp.flip(h_bwd[1:], axis=0)], axis=0)
    kf = jnp.fft.rfft(k, axis=0)
    uf = jnp.fft.rfft(u, n=2 * l, axis=1)
    return jnp.fft.irfft(uf * kf[None], n=2 * l, axis=1)[:, :l]


def hyena_stream(h, on_grid, w_in, conv_w, f_w1, f_b1, f_freq, f_w2, f_b2, f_w3, bias, w_out):
    l = h.shape[1]
    p = short_conv(h @ w_in, conv_w, on_grid)
    v, x1, x2 = jnp.split(p, 3, axis=-1)
    filt = hyena_filters(l, f_w1, f_b1, f_freq, f_w2, f_b2, f_w3)
    z = v.astype(jnp.float32)
    for n, gate in enumerate((x1, x2)):
        conv = two_sided_fftconv(z, filt[:, n, 0], filt[:, n, 1])
        z = gate.astype(jnp.float32) * (conv + bias[n].astype(jnp.float32) * z)
    return z.astype(h.dtype) @ w_out


def shortconv_stream(h, on_grid, w_in, conv_w, w_out):
    bg, cg, xin = jnp.split(h @ w_in, 3, axis=-1)
    return (bg * short_conv(cg * xin, conv_w, on_grid)) @ w_out


def route(h, w_router, router_bias):
    t = h.shape[0]
    scores = jax.nn.sigmoid(h.astype(jnp.float32) @ w_router.astype(jnp.float32))
    choice = (scores + router_bias.astype(jnp.float32)).reshape(t, N_GROUPS, EXPERTS_PER_GROUP)
    group_score = lax.top_k(choice, GROUP_SCORE_K)[0].sum(-1)
    group = jnp.argmax(group_score, axis=-1)
    in_group = jnp.take_along_axis(choice, group[:, None, None], axis=1)[:, 0]
    local = lax.top_k(in_group, TOP_K)[1]
    expert = group[:, None] * EXPERTS_PER_GROUP + local
    weight = jnp.take_along_axis(scores, expert, axis=1)
    weight = weight / jnp.sum(weight, axis=-1, keepdims=True)
    return expert.astype(jnp.int32), weight


def moe_ffn(x, w_router, router_bias, w_gate, w_up, w_down):
    t, d = x.shape
    expert, weight = route(x, w_router, router_bias)
    a = t * TOP_K
    e_flat = expert.reshape(-1)
    order = jnp.argsort(e_flat)
    e_sorted = e_flat[order]
    tok_sorted = (order // TOP_K).astype(jnp.int32)
    w_sorted = weight.reshape(-1)[order]
    counts = jnp.zeros((N_EXPERTS,), jnp.int32).at[e_flat].add(1)
    start = jnp.cumsum(counts) - counts
    padded = (counts + MOE_BLOCK - 1) // MOE_BLOCK * MOE_BLOCK
    pend = jnp.cumsum(padded)
    pstart = pend - padded
    dest = pstart[e_sorted] + (jnp.arange(a, dtype=jnp.int32) - start[e_sorted])
    n_blocks = -(-a // MOE_BLOCK) + N_EXPERTS
    n_slots = n_blocks * MOE_BLOCK
    slot_tok = jnp.full((n_slots,), t, jnp.int32).at[dest].set(tok_sorted)
    slot_w = jnp.zeros((n_slots,), jnp.float32).at[dest].set(w_sorted)
    block_start = jnp.arange(n_blocks, dtype=jnp.int32) * MOE_BLOCK
    block_expert = jnp.minimum(jnp.searchsorted(pend, block_start, side='right'), N_EXPERTS - 1)
    x_pad = jnp.concatenate([x, jnp.zeros((1, d), x.dtype)], axis=0)
    xs = x_pad[slot_tok].reshape(n_blocks, MOE_BLOCK, d)

    def expert_block(args):
        e, xb = args
        hid = jax.nn.silu(xb @ w_gate[e]) * (xb @ w_up[e])
        return hid @ w_down[e]

    ys = lax.map(expert_block, (block_expert, xs)).reshape(n_slots, d)
    ys = ys * slot_w[:, None].astype(ys.dtype)
    return jnp.zeros((t + 1, d), ys.dtype).at[slot_tok].add(ys)[:t]


def setup_inputs(seed: int = 0) -> dict:
    key = jax.random.key(seed)
    ks = iter(jax.random.split(key, 48))
    f32 = jnp.float32

    def nrm(shape, s):
        return jax.random.normal(next(ks), shape, f32) * s

    d = D_MODEL
    inv = d ** -0.5
    dt = jnp.exp(jax.random.uniform(next(ks), (N_A, 2, DN_HEADS), f32, math.log(1e-3), math.log(1e-1)))
    dn_dt_bias = dt + jnp.log(-jnp.expm1(-dt))
    dn_a_log = jnp.log(jax.random.uniform(next(ks), (N_A, 2, DN_HEADS), f32, 1.0, 16.0))
    return {
        'x': nrm((BATCH, SEQ, d), 1.0),
        'c': nrm((BATCH, d), 1.0),
        'ctx': nrm((BATCH, CTX_LEN, d), 1.0),
        'c_ctx': nrm((d,), 1.0),
        'ada_w': nrm((DEPTH, d, N_MOD * d), 0.5 * inv),
        'ada_b': nrm((DEPTH, N_MOD * d), 0.02),
        'norm_mix': 1.0 + nrm((DEPTH, d), 0.05),
        'norm_ffn': 1.0 + nrm((DEPTH, d), 0.05),
        'norm_final': 1.0 + nrm((d,), 0.05),
        'dn_w_in': nrm((N_A, d, DN_IN), inv),
        'dn_conv': nrm((N_A, 3, 3 * d), 3 ** -0.5),
        'dn_a_log': dn_a_log,
        'dn_dt_bias': dn_dt_bias,
        'dn_out_norm': 1.0 + nrm((N_A, DN_DV), 0.05),
        'dn_w_out': nrm((N_A, d, d), inv),
        'hy_w_in': nrm((N_B, d, 3 * d), inv),
        'hy_conv': nrm((N_B, 3, 3 * d), 3 ** -0.5),
        'hy_f_w1': nrm((N_B, HY_EMB, HY_HIDDEN), HY_EMB ** -0.5),
        'hy_f_b1': nrm((N_B, HY_HIDDEN), 0.2),
        'hy_f_freq': 1.0 + nrm((N_B, HY_HIDDEN), 0.1),
        'hy_f_w2': nrm((N_B, HY_HIDDEN, HY_HIDDEN), HY_HIDDEN ** -0.5),
        'hy_f_b2': nrm((N_B, HY_HIDDEN), 0.2),
        'hy_f_w3': nrm((N_B, HY_HIDDEN, HY_ORDER * 2 * d), 0.02 * HY_HIDDEN ** -0.5),
        'hy_bias': nrm((N_B, HY_ORDER, d), 0.5),
        'hy_w_out': nrm((N_B, d, d), inv),
        'sc_w_in': nrm((N_C, d, 3 * d), inv),
        'sc_conv': nrm((N_C, 3, d), 3 ** -0.5),
        'sc_w_out': nrm((N_C, d, d), inv),
        'w_router': nrm((d, N_EXPERTS), inv),
        'router_bias': nrm((N_EXPERTS,), 0.01),
        'moe_w_gate': nrm((DEPTH, N_EXPERTS, d, D_EXPERT), inv),
        'moe_w_up': nrm((DEPTH, N_EXPERTS, d, D_EXPERT), inv),
        'moe_w_down': nrm((DEPTH, N_EXPERTS, D_EXPERT, d), D_EXPERT ** -0.5),
    }


def reference(x, c, ctx, c_ctx, ada_w, ada_b, norm_mix, norm_ffn, norm_final,
              dn_w_in, dn_conv, dn_a_log, dn_dt_bias, dn_out_norm, dn_w_out,
              hy_w_in, hy_conv, hy_f_w1, hy_f_b1, hy_f_freq, hy_f_w2, hy_f_b2, hy_f_w3, hy_bias, hy_w_out,
              sc_w_in, sc_conv, sc_w_out,
              w_router, router_bias, moe_w_gate, moe_w_up, moe_w_down):
    d = D_MODEL
    silu_c = jax.nn.silu(c)
    silu_cc = jax.nn.silu(c_ctx)
    for i in range(DEPTH):
        last = i == DEPTH - 1
        kind, j = i % N_MIXERS, i // N_MIXERS
        ml = jnp.split((silu_c @ ada_w[i] + ada_b[i])[:, None, :], N_MOD, axis=-1)
        mc = jnp.split(silu_cc @ ada_w[i] + ada_b[i], N_MOD, axis=-1)
        h_lat = modulate(x, norm_mix[i], ml[0], ml[1])
        h_ctx = modulate(ctx, norm_mix[i], mc[0], mc[1]) if (kind == 0 or not last) else None
        if kind == 0:
            y_ctx, y_lat = deltanet_mixer(h_ctx, h_lat, dn_w_in[j], dn_conv[j], dn_a_log[j], dn_dt_bias[j],
                                          dn_out_norm[j], dn_w_out[j], not last)
        elif kind == 1:
            hy = (hy_w_in[j], hy_conv[j], hy_f_w1[j], hy_f_b1[j], hy_f_freq[j], hy_f_w2[j], hy_f_b2[j],
                  hy_f_w3[j], hy_bias[j], hy_w_out[j])
            y_lat = hyena_stream(h_lat, True, *hy)
            y_ctx = None if last else hyena_stream(h_ctx, False, *hy)
        else:
            scw = (sc_w_in[j], sc_conv[j], sc_w_out[j])
            y_lat = shortconv_stream(h_lat, True, *scw)
            y_ctx = None if last else shortconv_stream(h_ctx, False, *scw)
        x = x + ml[2] * y_lat
        f_lat = modulate(x, norm_ffn[i], ml[3], ml[4])
        if last:
            out = moe_ffn(f_lat.reshape(-1, d), w_router, router_bias, moe_w_gate[i], moe_w_up[i], moe_w_down[i])
            x = x + ml[5] * out.reshape(x.shape)
        else:
            ctx = ctx + mc[2] * y_ctx
            f_ctx = modulate(ctx, norm_ffn[i], mc[3], mc[4])
            n_ctx = ctx.shape[0] * ctx.shape[1]
            tokens = jnp.concatenate([f_ctx.reshape(-1, d), f_lat.reshape(-1, d)], axis=0)
            out = moe_ffn(tokens, w_router, router_bias, moe_w_gate[i], moe_w_up[i], moe_w_down[i])
            ctx = ctx + mc[5] * out[:n_ctx].reshape(ctx.shape)
            x = x + ml[5] * out[n_ctx:].reshape(x.shape)
    return rmsnorm(x, norm_final)
```

```python
import functools
import math

import jax
import jax.numpy as jnp
from jax import lax
from jax.experimental import pallas as pl
from jax.experimental.pallas import tpu as pltpu

D_MODEL = 1024
DEPTH = 4
CTX_LEN = 256
GRID_W = 64
N_MIXERS = 3
EPS = 1e-6
N_MOD = 6

DN_HEADS = 8
DN_DK = D_MODEL // DN_HEADS
DN_DV = D_MODEL // DN_HEADS
DN_CHUNK = 64

HY_ORDER = 2
HY_BANDS = 16
HY_TARGET = 1e-2
HY_FAST = 0.3
HY_SLOW = 1.5

N_EXPERTS = 32
N_GROUPS = 8
EXPERTS_PER_GROUP = N_EXPERTS // N_GROUPS
GROUP_SCORE_K = 2
TOP_K = 2
D_EXPERT = 512
MOE_BLOCK = 512

F32 = jnp.float32
BF16 = jnp.bfloat16

ROW_TILE = 512
VMEM_LIMIT_BYTES = 48 * 1024 * 1024


def _mm_kernel(x_ref, w_ref, o_ref):
    o_ref[...] = jnp.dot(x_ref[...].astype(BF16), w_ref[...], preferred_element_type=F32)


def _mm(x, w, tn=None):
    m, k = x.shape
    n = w.shape[1]
    tm = min(ROW_TILE, m)
    tn = n if tn is None else tn
    assert m % tm == 0 and n % tn == 0
    return pl.pallas_call(
        _mm_kernel,
        out_shape=jax.ShapeDtypeStruct((m, n), F32),
        grid=(m // tm, n // tn),
        in_specs=[pl.BlockSpec((tm, k), lambda i, j: (i, 0)),
                  pl.BlockSpec((k, tn), lambda i, j: (0, j))],
        out_specs=pl.BlockSpec((tm, tn), lambda i, j: (i, j)),
        compiler_params=pltpu.CompilerParams(
            dimension_semantics=("arbitrary", "arbitrary"), vmem_limit_bytes=VMEM_LIMIT_BYTES),
        name="dense_mm",
    )(x, w.astype(BF16))


def _expert_kernel(be_ref, x_ref, wg_ref, wu_ref, wd_ref, sw_ref, o_ref):
    del be_ref
    xb = x_ref[...]
    g = jnp.dot(xb, wg_ref[0], preferred_element_type=F32)
    u = jnp.dot(xb, wu_ref[0], preferred_element_type=F32)
    hid = (g * jax.nn.sigmoid(g)) * u
    y = jnp.dot(hid.astype(BF16), wd_ref[0], preferred_element_type=F32)
    o_ref[...] = y * sw_ref[...]


def _expert_ffn(xs, block_expert, slot_w, w_gate, w_up, w_down):
    n_slots, d = xs.shape
    n_blocks = n_slots // MOE_BLOCK
    grid_spec = pltpu.PrefetchScalarGridSpec(
        num_scalar_prefetch=1,
        grid=(n_blocks,),
        in_specs=[
            pl.BlockSpec((MOE_BLOCK, d), lambda i, be: (i, 0)),
            pl.BlockSpec((1, d, D_EXPERT), lambda i, be: (be[i], 0, 0)),
            pl.BlockSpec((1, d, D_EXPERT), lambda i, be: (be[i], 0, 0)),
            pl.BlockSpec((1, D_EXPERT, d), lambda i, be: (be[i], 0, 0)),
            pl.BlockSpec((MOE_BLOCK, 1), lambda i, be: (i, 0)),
        ],
        out_specs=pl.BlockSpec((MOE_BLOCK, d), lambda i, be: (i, 0)),
    )
    return pl.pallas_call(
        _expert_kernel,
        out_shape=jax.ShapeDtypeStruct((n_slots, d), F32),
        grid_spec=grid_spec,
        compiler_params=pltpu.CompilerParams(
            dimension_semantics=("arbitrary",), vmem_limit_bytes=VMEM_LIMIT_BYTES),
        name="expert_ffn",
    )(block_expert, xs, w_gate.astype(BF16), w_up.astype(BF16), w_down.astype(BF16),
      slot_w.reshape(n_slots, 1))


def _rmsnorm(x, gain):
    y = x * lax.rsqrt(jnp.mean(x * x, axis=-1, keepdims=True) + EPS)
    return y * gain


def _modulate(x, gain, shift, scale):
    return _rmsnorm(x, gain) * (1 + scale) + shift


def _l2norm(t):
    return t * lax.rsqrt(jnp.sum(t * t, axis=-1, keepdims=True) + EPS)


def _short_conv(x, w, on_grid):
    b, l, ch = x.shape
    xs = x.reshape(b, l // GRID_W, GRID_W, ch) if on_grid else x.reshape(b, 1, l, ch)
    n = xs.shape[2]
    xp = jnp.pad(xs, ((0, 0), (0, 0), (1, 1), (0, 0)))
    y = w[0] * xp[:, :, 0:n] + w[1] * xp[:, :, 1:n + 1] + w[2] * xp[:, :, 2:n + 2]
    return y.reshape(b, l, ch)


def _gated_delta_chunked(q, k, v, g, beta, s0):
    b, h, l, dk = q.shape
    dv = v.shape[-1]
    c = DN_CHUNK
    n = l // c
    q = q.reshape(b, h, n, c, dk)
    k = k.reshape(b, h, n, c, dk)
    v = v.reshape(b, h, n, c, dv)
    g = jnp.cumsum(g.reshape(b, h, n, c), axis=-1)
    beta = beta.reshape(b, h, n, c, 1)
    pos = jnp.arange(c)
    incl = pos[:, None] >= pos[None, :]
    strict = pos[:, None] > pos[None, :]
    decay = jnp.exp(jnp.where(incl, g[..., :, None] - g[..., None, :], -jnp.inf))
    kb = k * beta
    a_mat = jnp.einsum('bhnid,bhnjd->bhnij', kb, k) * jnp.where(strict, decay, 0.0)
    rhs = jnp.concatenate([v * beta, kb * jnp.exp(g)[..., None]], axis=-1)
    sol = lax.linalg.triangular_solve(a_mat + jnp.eye(c, dtype=a_mat.dtype), rhs,
                                      left_side=True, lower=True, unit_diagonal=True)
    u, w = sol[..., :dv], sol[..., dv:]
    attn = jnp.einsum('bhnid,bhnjd->bhnij', q, k) * decay
    g_last = g[..., -1:]
    q_dec = q * jnp.exp(g)[..., None]
    k_dec = k * jnp.exp(g_last - g)[..., None]

    def step(s, inp):
        qd, kd, uu, ww, at, gl = inp
        v_new = uu - jnp.einsum('bhck,bhkv->bhcv', ww, s)
        o = jnp.einsum('bhck,bhkv->bhcv', qd, s) + jnp.einsum('bhcs,bhsv->bhcv', at, v_new)
        s = s * jnp.exp(gl)[..., None] + jnp.einsum('bhck,bhcv->bhkv', kd, v_new)
        return s, o

    xs = tuple(jnp.moveaxis(t, 2, 0) for t in (q_dec, k_dec, u, w, attn, g_last))
    s_final, o = lax.scan(step, s0, xs)
    o = jnp.moveaxis(o, 0, 2).reshape(b, h, l, dv)
    return o, s_final


def _deltanet_mixer(p_ctx, p_lat, conv_w, a_log, dt_bias, out_norm):
    d = D_MODEL
    nh = DN_HEADS

    def project(p, on_grid):
        b, l, _ = p.shape
        qkv = jax.nn.silu(_short_conv(p[..., :3 * d], conv_w, on_grid))
        z = p[..., 3 * d:4 * d]
        a = p[..., 4 * d:4 * d + 2 * nh].reshape(b, l, 2, nh)
        bb = p[..., 4 * d + 2 * nh:].reshape(b, l, 2, nh)

        def heads(t):
            return jnp.transpose(t.reshape(b, l, nh, -1), (0, 2, 1, 3))

        q, k, v = (heads(t) for t in jnp.split(qkv, 3, axis=-1))
        q = _l2norm(q) * DN_DK ** -0.5
        k = _l2norm(k)
        g = -jnp.exp(a_log) * jax.nn.softplus(a + dt_bias)
        g = jnp.transpose(g, (2, 0, 3, 1))
        beta = jnp.transpose(jax.nn.sigmoid(bb), (2, 0, 3, 1))
        return q, k, v, g, beta, z

    def scan_both(q, k, v, g, beta, s_f, s_b):
        o_f, s_f = _gated_delta_chunked(q, k, v, g[0], beta[0], s_f)
        rev = lambda t: jnp.flip(t, axis=2)
        o_b, s_b = _gated_delta_chunked(rev(q), rev(k), rev(v), rev(g[1]), rev(beta[1]), s_b)
        return o_f + rev(o_b), s_f, s_b

    def finish(o, z):
        b, _, l, _ = o.shape
        o = jnp.transpose(o, (0, 2, 1, 3))
        o = o * lax.rsqrt(jnp.mean(o * o, axis=-1, keepdims=True) + EPS) * out_norm
        o = o * jax.nn.silu(z.reshape(b, l, nh, DN_DV))
        return o.reshape(b, l, d)

    qc, kc, vc, gc, bc, zc = project(p_ctx, False)
    s0 = jnp.zeros((p_ctx.shape[0], nh, DN_DK, DN_DV), F32)
    o_c, s_f, s_b = scan_both(qc, kc, vc, gc, bc, s0, s0)
    ql, kl, vl, gla, bl, zl = project(p_lat, True)
    o_l, _, _ = scan_both(ql, kl, vl, gla, bl, s_f, s_b)
    return finish(o_c, zc), finish(o_l, zl)


def _hyena_filters(l, w1, b1, freq, w2, b2, w3):
    pos = jnp.arange(l, dtype=F32)[:, None]
    t = pos / max(l - 1, 1)
    bands = jnp.linspace(1e-4, HY_BANDS - 1, HY_BANDS, dtype=F32)[None, :]
    ang = (2 * math.pi / l) * pos * bands
    feat = jnp.concatenate([t, jnp.cos(ang), -jnp.sin(ang)], axis=-1)
    hp = lax.Precision.HIGHEST
    hdn = jnp.sin(freq * (jnp.dot(feat, w1, precision=hp) + b1))
    hdn = jnp.sin(freq * (jnp.dot(hdn, w2, precision=hp) + b2))
    filt = jnp.dot(hdn, w3, precision=hp).reshape(l, HY_ORDER, 2, D_MODEL)
    deltas = jnp.abs(jnp.linspace(math.log(HY_TARGET) / HY_SLOW, math.log(HY_TARGET) / HY_FAST,
                                  D_MODEL, dtype=F32))
    window = jnp.exp(-t * deltas[None, :])
    return filt * window[:, None, None, :]


def _two_sided_fftconv(u, h_fwd, h_bwd):
    l = u.shape[1]
    k = jnp.concatenate([h_fwd, jnp.zeros_like(h_fwd[:1]), jnp.flip(h_bwd[1:], axis=0)], axis=0)
    kf = jnp.fft.rfft(k, axis=0)
    uf = jnp.fft.rfft(u, n=2 * l, axis=1)
    return jnp.fft.irfft(uf * kf[None], n=2 * l, axis=1)[:, :l]


def _hyena_stream(p, on_grid, conv_w, f_w1, f_b1, f_freq, f_w2, f_b2, f_w3, bias):
    l = p.shape[1]
    p = _short_conv(p, conv_w, on_grid)
    v, x1, x2 = jnp.split(p, 3, axis=-1)
    filt = _hyena_filters(l, f_w1, f_b1, f_freq, f_w2, f_b2, f_w3)
    z = v
    for n, gate in enumerate((x1, x2)):
        conv = _two_sided_fftconv(z, filt[:, n, 0], filt[:, n, 1])
        z = gate * (conv + bias[n] * z)
    return z


def _shortconv_stream(p, on_grid, conv_w):
    bg, cg, xin = jnp.split(p, 3, axis=-1)
    return bg * _short_conv(cg * xin, conv_w, on_grid)


def _route(h, w_router, router_bias):
    t = h.shape[0]
    scores = jax.nn.sigmoid(jnp.dot(h, w_router, precision=lax.Precision.HIGHEST))
    choice = (scores + router_bias).reshape(t, N_GROUPS, EXPERTS_PER_GROUP)
    group_score = lax.top_k(choice, GROUP_SCORE_K)[0].sum(-1)
    group = jnp.argmax(group_score, axis=-1)
    in_group = jnp.take_along_axis(choice, group[:, None, None], axis=1)[:, 0]
    local = lax.top_k(in_group, TOP_K)[1]
    expert = group[:, None] * EXPERTS_PER_GROUP + local
    weight = jnp.take_along_axis(scores, expert, axis=1)
    weight = weight / jnp.sum(weight, axis=-1, keepdims=True)
    return expert.astype(jnp.int32), weight


def _moe_ffn(x, w_router, router_bias, w_gate, w_up, w_down):
    t, d = x.shape
    expert, weight = _route(x, w_router, router_bias)
    a = t * TOP_K
    e_flat = expert.reshape(-1)
    order = jnp.argsort(e_flat)
    e_sorted = e_flat[order]
    tok_sorted = (order // TOP_K).astype(jnp.int32)
    counts = jnp.zeros((N_EXPERTS,), jnp.int32).at[e_flat].add(1)
    start = jnp.cumsum(counts) - counts
    padded = (counts + MOE_BLOCK - 1) // MOE_BLOCK * MOE_BLOCK
    pend = jnp.cumsum(padded)
    pstart = pend - padded
    dest = pstart[e_sorted] + (jnp.arange(a, dtype=jnp.int32) - start[e_sorted])
    n_blocks = -(-a // MOE_BLOCK) + N_EXPERTS
    n_slots = n_blocks * MOE_BLOCK
    slot_tok = jnp.full((n_slots,), t, jnp.int32).at[dest].set(tok_sorted)
    block_start = jnp.arange(n_blocks, dtype=jnp.int32) * MOE_BLOCK
    block_expert = jnp.minimum(jnp.searchsorted(pend, block_start, side='right'),
                               N_EXPERTS - 1).astype(jnp.int32)
    x_pad = jnp.concatenate([x.astype(BF16), jnp.zeros((1, d), BF16)], axis=0)
    xs = x_pad[slot_tok]
    ys = _expert_ffn(xs, block_expert, jnp.ones((n_slots,), F32), w_gate, w_up, w_down)
    slot_of = jnp.zeros((a,), jnp.int32).at[order].set(dest).reshape(t, TOP_K)
    out = weight[:, 0:1] * ys[slot_of[:, 0]] + weight[:, 1:2] * ys[slot_of[:, 1]]
    return out


def kernel(x, c, ctx, c_ctx, ada_w, ada_b, norm_mix, norm_ffn, norm_final, dn_w_in, dn_conv, dn_a_log,
           dn_dt_bias, dn_out_norm, dn_w_out, hy_w_in, hy_conv, hy_f_w1, hy_f_b1, hy_f_freq, hy_f_w2,
           hy_f_b2, hy_f_w3, hy_bias, hy_w_out, sc_w_in, sc_conv, sc_w_out, w_router, router_bias,
           moe_w_gate, moe_w_up, moe_w_down):
    d = D_MODEL
    bsz, seq, _ = x.shape
    n_ctx = bsz * CTX_LEN
    silu_c = jax.nn.silu(c)
    silu_cc = jax.nn.silu(c_ctx)
    hp = lax.Precision.HIGHEST

    xs = jnp.concatenate([ctx.reshape(n_ctx, d), x.reshape(bsz * seq, d)], axis=0)

    def rows(ctx_vec, lat_vecs):
        return jnp.concatenate([jnp.broadcast_to(ctx_vec[None], (n_ctx, d)),
                                jnp.repeat(lat_vecs, seq, axis=0)], axis=0)

    for i in range(DEPTH):
        kind, j = i % N_MIXERS, i // N_MIXERS
        ml = jnp.split(jnp.dot(silu_c, ada_w[i], precision=hp) + ada_b[i], N_MOD, axis=-1)
        mc = jnp.split(jnp.dot(silu_cc, ada_w[i], precision=hp) + ada_b[i], N_MOD, axis=-1)
        mod = [rows(mc[m], ml[m]) for m in range(N_MOD)]

        h = _modulate(xs, norm_mix[i], mod[0], mod[1])
        if kind == 0:
            w_in = dn_w_in[j]
            p_main = _mm(h, w_in[:, :4 * d], tn=1024)
            p_ab = jnp.dot(h, w_in[:, 4 * d:], precision=hp)
            p = jnp.concatenate([p_main, p_ab], axis=-1)
            y_ctx, y_lat = _deltanet_mixer(p[:n_ctx].reshape(bsz, CTX_LEN, -1),
                                           p[n_ctx:].reshape(bsz, seq, -1),
                                           dn_conv[j], dn_a_log[j], dn_dt_bias[j], dn_out_norm[j])
            w_out = dn_w_out[j]
        elif kind == 1:
            p = _mm(h, hy_w_in[j], tn=1024)
            hy = (hy_conv[j], hy_f_w1[j], hy_f_b1[j], hy_f_freq[j], hy_f_w2[j], hy_f_b2[j],
                  hy_f_w3[j], hy_bias[j])
            y_ctx = _hyena_stream(p[:n_ctx].reshape(bsz, CTX_LEN, -1), False, *hy)
            y_lat = _hyena_stream(p[n_ctx:].reshape(bsz, seq, -1), True, *hy)
            w_out = hy_w_out[j]
        else:
            p = _mm(h, sc_w_in[j], tn=1024)
            y_ctx = _shortconv_stream(p[:n_ctx].reshape(bsz, CTX_LEN, -1), False, sc_conv[j])
            y_lat = _shortconv_stream(p[n_ctx:].reshape(bsz, seq, -1), True, sc_conv[j])
            w_out = sc_w_out[j]
        y = jnp.concatenate([y_ctx.reshape(n_ctx, d), y_lat.reshape(bsz * seq, d)], axis=0)
        xs = xs + mod[2] * _mm(y, w_out)
        f = _modulate(xs, norm_ffn[i], mod[3], mod[4])
        out = _moe_ffn(f, w_router, router_bias, moe_w_gate[i], moe_w_up[i], moe_w_down[i])
        xs = xs + mod[5] * out
    x_out = xs[n_ctx:].reshape(bsz, seq, d)
    return _rmsnorm(x_out, norm_final)
```

```python
import functools
import math

import jax
import jax.numpy as jnp
from jax import lax
from jax.experimental import pallas as pl
from jax.experimental.pallas import tpu as pltpu

D_MODEL = 1024
DEPTH = 4
CTX_LEN = 256
GRID_W = 64
N_MIXERS = 3
EPS = 1e-6
N_MOD = 6

DN_HEADS = 8
DN_DK = D_MODEL // DN_HEADS
DN_DV = D_MODEL // DN_HEADS
DN_CHUNK = 64

HY_ORDER = 2
HY_BANDS = 16
HY_TARGET = 1e-2
HY_FAST = 0.3
HY_SLOW = 1.5

N_EXPERTS = 32
N_GROUPS = 8
EXPERTS_PER_GROUP = N_EXPERTS // N_GROUPS
GROUP_SCORE_K = 2
TOP_K = 2
D_EXPERT = 512
MOE_BLOCK = 512

F32 = jnp.float32
BF16 = jnp.bfloat16

ROW_TILE = 512
VMEM_LIMIT_BYTES = 48 * 1024 * 1024


def _mm_kernel(x_ref, w_ref, o_ref):
    o_ref[...] = jnp.dot(x_ref[...].astype(BF16), w_ref[...], preferred_element_type=F32)


def _mm(x, w, tn=None):
    m, k = x.shape
    n = w.shape[1]
    tm = min(ROW_TILE, m)
    tn = n if tn is None else tn
    assert m % tm == 0 and n % tn == 0
    return pl.pallas_call(
        _mm_kernel,
        out_shape=jax.ShapeDtypeStruct((m, n), F32),
        grid=(m // tm, n // tn),
        in_specs=[pl.BlockSpec((tm, k), lambda i, j: (i, 0)),
                  pl.BlockSpec((k, tn), lambda i, j: (0, j))],
        out_specs=pl.BlockSpec((tm, tn), lambda i, j: (i, j)),
        compiler_params=pltpu.CompilerParams(
            dimension_semantics=("arbitrary", "arbitrary"), vmem_limit_bytes=VMEM_LIMIT_BYTES),
        name="dense_mm",
    )(x, w.astype(BF16))


def _expert_kernel(be_ref, x_ref, wg_ref, wu_ref, wd_ref, sw_ref, o_ref):
    del be_ref
    xb = x_ref[...]
    g = jnp.dot(xb, wg_ref[0], preferred_element_type=F32)
    u = jnp.dot(xb, wu_ref[0], preferred_element_type=F32)
    hid = (g * jax.nn.sigmoid(g)) * u
    y = jnp.dot(hid.astype(BF16), wd_ref[0], preferred_element_type=F32)
    o_ref[...] = y * sw_ref[...]


def _expert_ffn(xs, block_expert, slot_w, w_gate, w_up, w_down):
    n_slots, d = xs.shape
    n_blocks = n_slots // MOE_BLOCK
    grid_spec = pltpu.PrefetchScalarGridSpec(
        num_scalar_prefetch=1,
        grid=(n_blocks,),
        in_specs=[
            pl.BlockSpec((MOE_BLOCK, d), lambda i, be: (i, 0)),
            pl.BlockSpec((1, d, D_EXPERT), lambda i, be: (be[i], 0, 0)),
            pl.BlockSpec((1, d, D_EXPERT), lambda i, be: (be[i], 0, 0)),
            pl.BlockSpec((1, D_EXPERT, d), lambda i, be: (be[i], 0, 0)),
            pl.BlockSpec((MOE_BLOCK, 1), lambda i, be: (i, 0)),
        ],
        out_specs=pl.BlockSpec((MOE_BLOCK, d), lambda i, be: (i, 0)),
    )
    return pl.pallas_call(
        _expert_kernel,
        out_shape=jax.ShapeDtypeStruct((n_slots, d), F32),
        grid_spec=grid_spec,
        compiler_params=pltpu.CompilerParams(
            dimension_semantics=("arbitrary",), vmem_limit_bytes=VMEM_LIMIT_BYTES),
        name="expert_ffn",
    )(block_expert, xs, w_gate.astype(BF16), w_up.astype(BF16), w_down.astype(BF16),
      slot_w.reshape(n_slots, 1))


DN_BLOCK = CTX_LEN
DN_HB = 2
N_CHUNKS_PER_BLOCK = DN_BLOCK // DN_CHUNK


def _group_of_block(i, n_ctx_blocks, blocks_per_batch):
    return jnp.where(i < n_ctx_blocks, 0, 1 + (i - n_ctx_blocks) // blocks_per_batch)


def _dn_prep_kernel(p_ref, ab_ref, cw_ref, alog_ref, dtb_ref, q_ref, k_ref, v_ref, gate_ref, *,
                    n_ctx_blocks):
    i = pl.program_id(0)
    nrow = DN_BLOCK
    seg = jnp.where(i < n_ctx_blocks, CTX_LEN, GRID_W)
    r = lax.broadcasted_iota(jnp.int32, (nrow, 1), 0)
    pos = r & (seg - 1)
    not_first = pos != 0
    not_last = pos != seg - 1
    outs = (q_ref, k_ref, v_ref)
    for part in range(3):
        for h in range(DN_HEADS):
            col = part * D_MODEL + h * DN_DK
            x = p_ref[:, col:col + DN_DK]
            cw = cw_ref[:, col:col + DN_DK]
            xp = jnp.where(not_first, pltpu.roll(x, 1, axis=0), 0.0)
            xn = jnp.where(not_last, pltpu.roll(x, nrow - 1, axis=0), 0.0)
            y = cw[0:1] * xp + cw[1:2] * x + cw[2:3] * xn
            y = y * jax.nn.sigmoid(y)
            if part < 2:
                y = y * lax.rsqrt(jnp.sum(y * y, axis=-1, keepdims=True) + EPS)
            if part == 0:
                y = y * DN_DK ** -0.5
            outs[part][:, h * DN_DK:(h + 1) * DN_DK] = y

    ab = ab_ref[...]
    nd = 2 * DN_HEADS
    a = ab[:, :nd] + dtb_ref[...]
    softplus = jnp.maximum(a, 0.0) + jnp.log(1.0 + jnp.exp(-jnp.abs(a)))
    g = -jnp.exp(alog_ref[...]) * softplus
    beta = jax.nn.sigmoid(ab[:, nd:])
    cpos = r & (DN_CHUNK - 1)
    gp, gs = g, g
    sh = 1
    while sh < DN_CHUNK:
        gp = gp + jnp.where(cpos >= sh, pltpu.roll(gp, sh, axis=0), 0.0)
        gs = gs + jnp.where(cpos < DN_CHUNK - sh, pltpu.roll(gs, nrow - sh, axis=0), 0.0)
        sh *= 2
    colid = lax.broadcasted_iota(jnp.int32, (1, nd), 1)
    gate_ref[:, :nd] = jnp.where(colid < DN_HEADS, gp, gs)
    gate_ref[:, nd:] = beta


def _dn_prep(p, p_ab, conv_w, a_log, dt_bias, n_ctx_blocks):
    t = p.shape[0]
    d = D_MODEL
    nd = 2 * DN_HEADS
    nblk = t // DN_BLOCK
    row = lambda i: (i, 0)
    fixed = lambda i: (0, 0)
    return pl.pallas_call(
        functools.partial(_dn_prep_kernel, n_ctx_blocks=n_ctx_blocks),
        out_shape=[jax.ShapeDtypeStruct((t, d), F32)] * 3 + [jax.ShapeDtypeStruct((t, 2 * nd), F32)],
        grid=(nblk,),
        in_specs=[pl.BlockSpec((DN_BLOCK, 3 * d), row),
                  pl.BlockSpec((DN_BLOCK, 2 * nd), row),
                  pl.BlockSpec((3, 3 * d), fixed),
                  pl.BlockSpec((1, nd), fixed),
                  pl.BlockSpec((1, nd), fixed)],
        out_specs=[pl.BlockSpec((DN_BLOCK, d), row)] * 3 + [pl.BlockSpec((DN_BLOCK, 2 * nd), row)],
        compiler_params=pltpu.CompilerParams(
            dimension_semantics=("arbitrary",), vmem_limit_bytes=VMEM_LIMIT_BYTES),
        name="dn_prep",
    )(p, p_ab, conv_w, a_log.reshape(1, nd), dt_bias.reshape(1, nd))


def _dotb(a, b):
    return jnp.dot(a.astype(BF16), b.astype(BF16), preferred_element_type=F32)


def _dotb_nt(a, b):
    return lax.dot_general(a.astype(BF16), b.astype(BF16), (((1,), (1,)), ((), ())),
                           preferred_element_type=F32)


def _dotb_tn(a, b):
    return lax.dot_general(a.astype(BF16), b.astype(BF16), (((0,), (0,)), ((), ())),
                           preferred_element_type=F32)


def _unit_tri_inverses(mats, ii, jj):
    eye = (ii == jj).astype(F32)
    diag8 = (ii >> 3) == (jj >> 3)
    n = [-jnp.where(diag8, a, 0.0) for a in mats]
    n2 = [_dotb(x, x) for x in n]
    m = [eye + x for x in n]
    m = [x + _dotb(x, y) for x, y in zip(m, n2)]
    n4 = [_dotb(x, x) for x in n2]
    m = [x + _dotb(x, y) for x, y in zip(m, n4)]
    sh = 3
    while (1 << sh) < DN_CHUNK:
        off = ((ii >> (sh + 1)) == (jj >> (sh + 1))) & ((ii >> sh) != (jj >> sh))
        cm = [_dotb(jnp.where(off, a, 0.0), x) for a, x in zip(mats, m)]
        m = [x - _dotb(x, y) for x, y in zip(m, cm)]
        sh += 1
    return m


def _dn_scan_kernel(qf_ref, kf_ref, vf_ref, gcf_ref, grf_ref, qb_ref, kb_ref, vb_ref, gcb_ref, grb_ref,
                    of_ref, ob_ref, s_ref):
    @pl.when(pl.program_id(2) == 0)
    def _():
        s_ref[...] = jnp.zeros_like(s_ref)

    c = DN_CHUNK
    ncb = N_CHUNKS_PER_BLOCK
    ii = lax.broadcasted_iota(jnp.int32, (c, c), 0)
    jj = lax.broadcasted_iota(jnp.int32, (c, c), 1)
    incl = (ii >= jj, ii <= jj)
    strict = (ii > jj, ii < jj)
    dirs = ((qf_ref, kf_ref, vf_ref, gcf_ref, grf_ref, of_ref),
            (qb_ref, kb_ref, vb_ref, gcb_ref, grb_ref, ob_ref))
    items = [(d, hh, ci) for d in range(2) for hh in range(DN_HB) for ci in range(ncb)]

    def rows(ci):
        return slice(ci * c, (ci + 1) * c)

    def cols(hh):
        return slice(hh * DN_DK, (hh + 1) * DN_DK)

    q = [dirs[d][0][rows(ci), cols(hh)] for d, hh, ci in items]
    k = [dirs[d][1][rows(ci), cols(hh)] for d, hh, ci in items]
    v = [dirs[d][2][rows(ci), cols(hh)] for d, hh, ci in items]
    gc = [dirs[d][3][hh, rows(ci), d:d + 1] for d, hh, ci in items]
    gr = [dirs[d][4][hh, d:d + 1, rows(ci)] for d, hh, ci in items]
    beta = [dirs[d][3][hh, rows(ci), 2 + d:3 + d] for d, hh, ci in items]

    decay = [jnp.where(incl[it[0]], jnp.exp(jnp.where(incl[it[0]], x - y, 0.0)), 0.0)
             for it, x, y in zip(items, gc, gr)]
    kb = [x * y for x, y in zip(k, beta)]
    a = [_dotb_nt(x, y) * jnp.where(strict[it[0]], z, 0.0) for it, x, y, z in zip(items, kb, k, decay)]
    attn = [_dotb_nt(x, y) * z for x, y, z in zip(q, k, decay)]
    tinv = _unit_tri_inverses(a, ii, jj)
    eg = [jnp.exp(x) for x in gc]
    uw = [_dotb(t, jnp.concatenate([x * y, z * e], axis=-1))
          for t, x, y, z, e in zip(tinv, v, beta, kb, eg)]
    g_last = [x[0:1] if it[0] else x[c - 1:c] for it, x in zip(items, gc)]
    wq = [jnp.concatenate([x[:, DN_DV:], y * e], axis=0) for x, y, e in zip(uw, q, eg)]
    k_dec = [x * jnp.exp(y - z) for x, y, z in zip(k, g_last, gc)]
    s_dec = [jnp.exp(x) for x in g_last]

    chains = [(d, hh) for d in range(2) for hh in range(DN_HB)]
    state = [s_ref[d, hh] for d, hh in chains]
    for step in range(ncb):
        cur = [items.index((d, hh, ncb - 1 - step if d else step)) for d, hh in chains]
        ws = [_dotb(wq[n], s) for n, s in zip(cur, state)]
        v_new = [uw[n][:, :DN_DV] - x[:c] for n, x in zip(cur, ws)]
        o = [x[c:] + _dotb(attn[n], y) for n, x, y in zip(cur, ws, v_new)]
        state = [s * s_dec[n] + _dotb_tn(k_dec[n], y) for n, s, y in zip(cur, state, v_new)]
        for n, x in zip(cur, o):
            d, hh, ci = items[n]
            dirs[d][5][rows(ci), cols(hh)] = x
    for (d, hh), s in zip(chains, state):
        s_ref[d, hh] = s


def _dn_scan(q, k, v, gates, n_ctx_blocks, blocks_per_batch):
    t, d = q.shape
    bsz = n_ctx_blocks
    g4 = gates.reshape(t, 4, DN_HEADS)
    gcol = jnp.transpose(g4, (2, 0, 1))
    grow = jnp.transpose(g4, (2, 1, 0))

    def blk_f(b, s):
        return jnp.where(s == 0, b, n_ctx_blocks + b * blocks_per_batch + s - 1)

    def blk_b(b, s):
        return jnp.where(s == 0, b, n_ctx_blocks + b * blocks_per_batch + blocks_per_batch - s)

    hw = DN_HB * DN_DK

    def specs(blk):
        return [pl.BlockSpec((DN_BLOCK, hw), lambda b, hg, s: (blk(b, s), hg))] * 3 + [
            pl.BlockSpec((DN_HB, DN_BLOCK, 4), lambda b, hg, s: (hg, blk(b, s), 0)),
            pl.BlockSpec((DN_HB, 4, DN_BLOCK), lambda b, hg, s: (hg, 0, blk(b, s)))]

    return pl.pallas_call(
        _dn_scan_kernel,
        out_shape=[jax.ShapeDtypeStruct((t, d), F32)] * 2,
        grid=(bsz, DN_HEADS // DN_HB, 1 + blocks_per_batch),
        in_specs=specs(blk_f) + specs(blk_b),
        out_specs=[pl.BlockSpec((DN_BLOCK, hw), lambda b, hg, s: (blk_f(b, s), hg)),
                   pl.BlockSpec((DN_BLOCK, hw), lambda b, hg, s: (blk_b(b, s), hg))],
        scratch_shapes=[pltpu.VMEM((2, DN_HB, DN_DK, DN_DV), F32)],
        compiler_params=pltpu.CompilerParams(
            dimension_semantics=("arbitrary", "arbitrary", "arbitrary"),
            vmem_limit_bytes=VMEM_LIMIT_BYTES),
        name="dn_scan",
    )(q, k, v, gcol, grow, q, k, v, gcol, grow)


def _dn_out_kernel(of_ref, ob_ref, z_ref, on_ref, w_ref, x_ref, gate_ref, o_ref):
    z = z_ref[...]
    zs = z * jax.nn.sigmoid(z)
    parts = []
    for h in range(DN_HEADS):
        cols = slice(h * DN_DV, (h + 1) * DN_DV)
        o = of_ref[:, cols] + ob_ref[:, cols]
        o = o * lax.rsqrt(jnp.mean(o * o, axis=-1, keepdims=True) + EPS)
        parts.append(o)
    y = jnp.concatenate(parts, axis=-1) * on_ref[...] * zs
    o_ref[...] = x_ref[...] + gate_ref[0] * jnp.dot(y.astype(BF16), w_ref[...],
                                                    preferred_element_type=F32)


def _dn_out(o_f, o_b, p, out_norm, w_out, xs, gate3, n_ctx_blocks, blocks_per_batch):
    t, d = xs.shape
    row = lambda i: (i, 0)
    fixed = lambda i: (0, 0)
    grp = lambda i: (_group_of_block(i, n_ctx_blocks, blocks_per_batch), 0, 0)
    return pl.pallas_call(
        _dn_out_kernel,
        out_shape=jax.ShapeDtypeStruct((t, d), F32),
        grid=(t // DN_BLOCK,),
        in_specs=[pl.BlockSpec((DN_BLOCK, d), row),
                  pl.BlockSpec((DN_BLOCK, d), row),
                  pl.BlockSpec((DN_BLOCK, d), lambda i: (i, 3)),
                  pl.BlockSpec((1, d), fixed),
                  pl.BlockSpec((d, d), fixed),
                  pl.BlockSpec((DN_BLOCK, d), row),
                  pl.BlockSpec((1, 1, d), grp)],
        out_specs=pl.BlockSpec((DN_BLOCK, d), row),
        compiler_params=pltpu.CompilerParams(
            dimension_semantics=("arbitrary",), vmem_limit_bytes=VMEM_LIMIT_BYTES),
        name="dn_out",
    )(o_f, o_b, p, jnp.tile(out_norm, DN_HEADS).reshape(1, d), w_out.astype(BF16), xs, gate3)


def _deltanet_layer(xs, h, w_in, conv_w, a_log, dt_bias, out_norm, w_out, gate3, n_ctx_blocks,
                    blocks_per_batch):
    d = D_MODEL
    p = _mm(h, w_in[:, :4 * d], tn=1024)
    p_ab = jnp.dot(h, w_in[:, 4 * d:], precision=lax.Precision.HIGHEST)
    q, k, v, gates = _dn_prep(p, p_ab, conv_w, a_log, dt_bias, n_ctx_blocks)
    o_f, o_b = _dn_scan(q, k, v, gates, n_ctx_blocks, blocks_per_batch)
    return _dn_out(o_f, o_b, p, out_norm, w_out, xs, gate3, n_ctx_blocks, blocks_per_batch)


def _rmsnorm(x, gain):
    y = x * lax.rsqrt(jnp.mean(x * x, axis=-1, keepdims=True) + EPS)
    return y * gain


def _modulate(x, gain, shift, scale):
    return _rmsnorm(x, gain) * (1 + scale) + shift


def _l2norm(t):
    return t * lax.rsqrt(jnp.sum(t * t, axis=-1, keepdims=True) + EPS)


def _short_conv(x, w, on_grid):
    b, l, ch = x.shape
    xs = x.reshape(b, l // GRID_W, GRID_W, ch) if on_grid else x.reshape(b, 1, l, ch)
    n = xs.shape[2]
    xp = jnp.pad(xs, ((0, 0), (0, 0), (1, 1), (0, 0)))
    y = w[0] * xp[:, :, 0:n] + w[1] * xp[:, :, 1:n + 1] + w[2] * xp[:, :, 2:n + 2]
    return y.reshape(b, l, ch)


def _gated_delta_chunked(q, k, v, g, beta, s0):
    b, h, l, dk = q.shape
    dv = v.shape[-1]
    c = DN_CHUNK
    n = l // c
    q = q.reshape(b, h, n, c, dk)
    k = k.reshape(b, h, n, c, dk)
    v = v.reshape(b, h, n, c, dv)
    g = jnp.cumsum(g.reshape(b, h, n, c), axis=-1)
    beta = beta.reshape(b, h, n, c, 1)
    pos = jnp.arange(c)
    incl = pos[:, None] >= pos[None, :]
    strict = pos[:, None] > pos[None, :]
    decay = jnp.exp(jnp.where(incl, g[..., :, None] - g[..., None, :], -jnp.inf))
    kb = k * beta
    a_mat = jnp.einsum('bhnid,bhnjd->bhnij', kb, k) * jnp.where(strict, decay, 0.0)
    rhs = jnp.concatenate([v * beta, kb * jnp.exp(g)[..., None]], axis=-1)
    sol = lax.linalg.triangular_solve(a_mat + jnp.eye(c, dtype=a_mat.dtype), rhs,
                                      left_side=True, lower=True, unit_diagonal=True)
    u, w = sol[..., :dv], sol[..., dv:]
    attn = jnp.einsum('bhnid,bhnjd->bhnij', q, k) * decay
    g_last = g[..., -1:]
    q_dec = q * jnp.exp(g)[..., None]
    k_dec = k * jnp.exp(g_last - g)[..., None]

    def step(s, inp):
        qd, kd, uu, ww, at, gl = inp
        v_new = uu - jnp.einsum('bhck,bhkv->bhcv', ww, s)
        o = jnp.einsum('bhck,bhkv->bhcv', qd, s) + jnp.einsum('bhcs,bhsv->bhcv', at, v_new)
        s = s * jnp.exp(gl)[..., None] + jnp.einsum('bhck,bhcv->bhkv', kd, v_new)
        return s, o

    xs = tuple(jnp.moveaxis(t, 2, 0) for t in (q_dec, k_dec, u, w, attn, g_last))
    s_final, o = lax.scan(step, s0, xs)
    o = jnp.moveaxis(o, 0, 2).reshape(b, h, l, dv)
    return o, s_final


def _deltanet_mixer(p_ctx, p_lat, conv_w, a_log, dt_bias, out_norm):
    d = D_MODEL
    nh = DN_HEADS

    def project(p, on_grid):
        b, l, _ = p.shape
        qkv = jax.nn.silu(_short_conv(p[..., :3 * d], conv_w, on_grid))
        z = p[..., 3 * d:4 * d]
        a = p[..., 4 * d:4 * d + 2 * nh].reshape(b, l, 2, nh)
        bb = p[..., 4 * d + 2 * nh:].reshape(b, l, 2, nh)

        def heads(t):
            return jnp.transpose(t.reshape(b, l, nh, -1), (0, 2, 1, 3))

        q, k, v = (heads(t) for t in jnp.split(qkv, 3, axis=-1))
        q = _l2norm(q) * DN_DK ** -0.5
        k = _l2norm(k)
        g = -jnp.exp(a_log) * jax.nn.softplus(a + dt_bias)
        g = jnp.transpose(g, (2, 0, 3, 1))
        beta = jnp.transpose(jax.nn.sigmoid(bb), (2, 0, 3, 1))
        return q, k, v, g, beta, z

    def scan_both(q, k, v, g, beta, s_f, s_b):
        o_f, s_f = _gated_delta_chunked(q, k, v, g[0], beta[0], s_f)
        rev = lambda t: jnp.flip(t, axis=2)
        o_b, s_b = _gated_delta_chunked(rev(q), rev(k), rev(v), rev(g[1]), rev(beta[1]), s_b)
        return o_f + rev(o_b), s_f, s_b

    def finish(o, z):
        b, _, l, _ = o.shape
        o = jnp.transpose(o, (0, 2, 1, 3))
        o = o * lax.rsqrt(jnp.mean(o * o, axis=-1, keepdims=True) + EPS) * out_norm
        o = o * jax.nn.silu(z.reshape(b, l, nh, DN_DV))
        return o.reshape(b, l, d)

    qc, kc, vc, gc, bc, zc = project(p_ctx, False)
    s0 = jnp.zeros((p_ctx.shape[0], nh, DN_DK, DN_DV), F32)
    o_c, s_f, s_b = scan_both(qc, kc, vc, gc, bc, s0, s0)
    ql, kl, vl, gla, bl, zl = project(p_lat, True)
    o_l, _, _ = scan_both(ql, kl, vl, gla, bl, s_f, s_b)
    return finish(o_c, zc), finish(o_l, zl)


def _hyena_filters(l, w1, b1, freq, w2, b2, w3):
    pos = jnp.arange(l, dtype=F32)[:, None]
    t = pos / max(l - 1, 1)
    bands = jnp.linspace(1e-4, HY_BANDS - 1, HY_BANDS, dtype=F32)[None, :]
    ang = (2 * math.pi / l) * pos * bands
    feat = jnp.concatenate([t, jnp.cos(ang), -jnp.sin(ang)], axis=-1)
    hp = lax.Precision.HIGHEST
    hdn = jnp.sin(freq * (jnp.dot(feat, w1, precision=hp) + b1))
    hdn = jnp.sin(freq * (jnp.dot(hdn, w2, precision=hp) + b2))
    filt = jnp.dot(hdn, w3, precision=hp).reshape(l, HY_ORDER, 2, D_MODEL)
    deltas = jnp.abs(jnp.linspace(math.log(HY_TARGET) / HY_SLOW, math.log(HY_TARGET) / HY_FAST,
                                  D_MODEL, dtype=F32))
    window = jnp.exp(-t * deltas[None, :])
    return filt * window[:, None, None, :]


def _two_sided_fftconv(u, h_fwd, h_bwd):
    l = u.shape[1]
    k = jnp.concatenate([h_fwd, jnp.zeros_like(h_fwd[:1]), jnp.flip(h_bwd[1:], axis=0)], axis=0)
    kf = jnp.fft.rfft(k, axis=0)
    uf = jnp.fft.rfft(u, n=2 * l, axis=1)
    return jnp.fft.irfft(uf * kf[None], n=2 * l, axis=1)[:, :l]


def _hyena_stream(p, on_grid, conv_w, f_w1, f_b1, f_freq, f_w2, f_b2, f_w3, bias):
    l = p.shape[1]
    p = _short_conv(p, conv_w, on_grid)
    v, x1, x2 = jnp.split(p, 3, axis=-1)
    filt = _hyena_filters(l, f_w1, f_b1, f_freq, f_w2, f_b2, f_w3)
    z = v
    for n, gate in enumerate((x1, x2)):
        conv = _two_sided_fftconv(z, filt[:, n, 0], filt[:, n, 1])
        z = gate * (conv + bias[n] * z)
    return z


def _shortconv_stream(p, on_grid, conv_w):
    bg, cg, xin = jnp.split(p, 3, axis=-1)
    return bg * _short_conv(cg * xin, conv_w, on_grid)


def _route(h, w_router, router_bias):
    t = h.shape[0]
    scores = jax.nn.sigmoid(jnp.dot(h, w_router, precision=lax.Precision.HIGHEST))
    choice = (scores + router_bias).reshape(t, N_GROUPS, EXPERTS_PER_GROUP)
    group_score = lax.top_k(choice, GROUP_SCORE_K)[0].sum(-1)
    group = jnp.argmax(group_score, axis=-1)
    in_group = jnp.take_along_axis(choice, group[:, None, None], axis=1)[:, 0]
    local = lax.top_k(in_group, TOP_K)[1]
    expert = group[:, None] * EXPERTS_PER_GROUP + local
    weight = jnp.take_along_axis(scores, expert, axis=1)
    weight = weight / jnp.sum(weight, axis=-1, keepdims=True)
    return expert.astype(jnp.int32), weight


def _moe_ffn(x, w_router, router_bias, w_gate, w_up, w_down):
    t, d = x.shape
    expert, weight = _route(x, w_router, router_bias)
    a = t * TOP_K
    e_flat = expert.reshape(-1)
    order = jnp.argsort(e_flat)
    e_sorted = e_flat[order]
    tok_sorted = (order // TOP_K).astype(jnp.int32)
    counts = jnp.zeros((N_EXPERTS,), jnp.int32).at[e_flat].add(1)
    start = jnp.cumsum(counts) - counts
    padded = (counts + MOE_BLOCK - 1) // MOE_BLOCK * MOE_BLOCK
    pend = jnp.cumsum(padded)
    pstart = pend - padded
    dest = pstart[e_sorted] + (jnp.arange(a, dtype=jnp.int32) - start[e_sorted])
    n_blocks = -(-a // MOE_BLOCK) + N_EXPERTS
    n_slots = n_blocks * MOE_BLOCK
    slot_tok = jnp.full((n_slots,), t, jnp.int32).at[dest].set(tok_sorted)
    block_start = jnp.arange(n_blocks, dtype=jnp.int32) * MOE_BLOCK
    block_expert = jnp.minimum(jnp.searchsorted(pend, block_start, side='right'),
                               N_EXPERTS - 1).astype(jnp.int32)
    x_pad = jnp.concatenate([x.astype(BF16), jnp.zeros((1, d), BF16)], axis=0)
    xs = x_pad[slot_tok]
    ys = _expert_ffn(xs, block_expert, jnp.ones((n_slots,), F32), w_gate, w_up, w_down)
    slot_of = jnp.zeros((a,), jnp.int32).at[order].set(dest).reshape(t, TOP_K)
    out = weight[:, 0:1] * ys[slot_of[:, 0]] + weight[:, 1:2] * ys[slot_of[:, 1]]
    return out


def kernel(x, c, ctx, c_ctx, ada_w, ada_b, norm_mix, norm_ffn, norm_final, dn_w_in, dn_conv, dn_a_log,
           dn_dt_bias, dn_out_norm, dn_w_out, hy_w_in, hy_conv, hy_f_w1, hy_f_b1, hy_f_freq, hy_f_w2,
           hy_f_b2, hy_f_w3, hy_bias, hy_w_out, sc_w_in, sc_conv, sc_w_out, w_router, router_bias,
           moe_w_gate, moe_w_up, moe_w_down):
    d = D_MODEL
    bsz, seq, _ = x.shape
    n_ctx = bsz * CTX_LEN
    silu_c = jax.nn.silu(c)
    silu_cc = jax.nn.silu(c_ctx)
    hp = lax.Precision.HIGHEST

    xs = jnp.concatenate([ctx.reshape(n_ctx, d), x.reshape(bsz * seq, d)], axis=0)

    def rows(ctx_vec, lat_vecs):
        return jnp.concatenate([jnp.broadcast_to(ctx_vec[None], (n_ctx, d)),
                                jnp.repeat(lat_vecs, seq, axis=0)], axis=0)

    for i in range(DEPTH):
        kind, j = i % N_MIXERS, i // N_MIXERS
        ml = jnp.split(jnp.dot(silu_c, ada_w[i], precision=hp) + ada_b[i], N_MOD, axis=-1)
        mc = jnp.split(jnp.dot(silu_cc, ada_w[i], precision=hp) + ada_b[i], N_MOD, axis=-1)
        mod = [rows(mc[m], ml[m]) for m in range(N_MOD)]

        h = _modulate(xs, norm_mix[i], mod[0], mod[1])
        if kind == 0:
            gate3 = jnp.concatenate([mc[2][None], ml[2]], axis=0)[:, None, :]
            xs = _deltanet_layer(xs, h, dn_w_in[j], dn_conv[j], dn_a_log[j], dn_dt_bias[j],
                                 dn_out_norm[j], dn_w_out[j], gate3, bsz, seq // DN_BLOCK)
        elif kind == 1:
            p = _mm(h, hy_w_in[j], tn=1024)
            hy = (hy_conv[j], hy_f_w1[j], hy_f_b1[j], hy_f_freq[j], hy_f_w2[j], hy_f_b2[j],
                  hy_f_w3[j], hy_bias[j])
            y_ctx = _hyena_stream(p[:n_ctx].reshape(bsz, CTX_LEN, -1), False, *hy)
            y_lat = _hyena_stream(p[n_ctx:].reshape(bsz, seq, -1), True, *hy)
            w_out = hy_w_out[j]
        else:
            p = _mm(h, sc_w_in[j], tn=1024)
            y_ctx = _shortconv_stream(p[:n_ctx].reshape(bsz, CTX_LEN, -1), False, sc_conv[j])
            y_lat = _shortconv_stream(p[n_ctx:].reshape(bsz, seq, -1), True, sc_conv[j])
            w_out = sc_w_out[j]
        if kind != 0:
            y = jnp.concatenate([y_ctx.reshape(n_ctx, d), y_lat.reshape(bsz * seq, d)], axis=0)
            xs = xs + mod[2] * _mm(y, w_out)
        f = _modulate(xs, norm_ffn[i], mod[3], mod[4])
        out = _moe_ffn(f, w_router, router_bias, moe_w_gate[i], moe_w_up[i], moe_w_down[i])
        xs = xs + mod[5] * out
    x_out = xs[n_ctx:].reshape(bsz, seq, d)
    return _rmsnorm(x_out, norm_final)
```

```python
import functools
import math

import numpy as np
import jax
import jax.numpy as jnp
from jax import lax
from jax.experimental import pallas as pl
from jax.experimental.pallas import tpu as pltpu

D_MODEL = 1024
DEPTH = 4
CTX_LEN = 256
GRID_W = 64
N_MIXERS = 3
EPS = 1e-6
N_MOD = 6

DN_HEADS = 8
DN_DK = D_MODEL // DN_HEADS
DN_DV = D_MODEL // DN_HEADS
DN_CHUNK = 64

HY_ORDER = 2
HY_BANDS = 16
HY_TARGET = 1e-2
HY_FAST = 0.3
HY_SLOW = 1.5

N_EXPERTS = 32
N_GROUPS = 8
EXPERTS_PER_GROUP = N_EXPERTS // N_GROUPS
GROUP_SCORE_K = 2
TOP_K = 2
D_EXPERT = 512
MOE_BLOCK = 512

F32 = jnp.float32
BF16 = jnp.bfloat16

ROW_TILE = 512
VMEM_LIMIT_BYTES = 48 * 1024 * 1024


def _mm_kernel(x_ref, w_ref, o_ref):
    o_ref[...] = jnp.dot(x_ref[...].astype(BF16), w_ref[...], preferred_element_type=F32)


def _mm(x, w, tn=None):
    m, k = x.shape
    n = w.shape[1]
    tm = min(ROW_TILE, m)
    tn = n if tn is None else tn
    assert m % tm == 0 and n % tn == 0
    return pl.pallas_call(
        _mm_kernel,
        out_shape=jax.ShapeDtypeStruct((m, n), F32),
        grid=(m // tm, n // tn),
        in_specs=[pl.BlockSpec((tm, k), lambda i, j: (i, 0)),
                  pl.BlockSpec((k, tn), lambda i, j: (0, j))],
        out_specs=pl.BlockSpec((tm, tn), lambda i, j: (i, j)),
        compiler_params=pltpu.CompilerParams(
            dimension_semantics=("arbitrary", "arbitrary"), vmem_limit_bytes=VMEM_LIMIT_BYTES),
        name="dense_mm",
    )(x, w.astype(BF16))


def _expert_kernel(be_ref, x_ref, wg_ref, wu_ref, wd_ref, sw_ref, o_ref):
    del be_ref
    xb = x_ref[...]
    g = jnp.dot(xb, wg_ref[0], preferred_element_type=F32)
    u = jnp.dot(xb, wu_ref[0], preferred_element_type=F32)
    hid = (g * jax.nn.sigmoid(g)) * u
    y = jnp.dot(hid.astype(BF16), wd_ref[0], preferred_element_type=F32)
    o_ref[...] = y * sw_ref[...]


def _expert_ffn(xs, block_expert, slot_w, w_gate, w_up, w_down):
    n_slots, d = xs.shape
    n_blocks = n_slots // MOE_BLOCK
    grid_spec = pltpu.PrefetchScalarGridSpec(
        num_scalar_prefetch=1,
        grid=(n_blocks,),
        in_specs=[
            pl.BlockSpec((MOE_BLOCK, d), lambda i, be: (i, 0)),
            pl.BlockSpec((1, d, D_EXPERT), lambda i, be: (be[i], 0, 0)),
            pl.BlockSpec((1, d, D_EXPERT), lambda i, be: (be[i], 0, 0)),
            pl.BlockSpec((1, D_EXPERT, d), lambda i, be: (be[i], 0, 0)),
            pl.BlockSpec((MOE_BLOCK, 1), lambda i, be: (i, 0)),
        ],
        out_specs=pl.BlockSpec((MOE_BLOCK, d), lambda i, be: (i, 0)),
    )
    return pl.pallas_call(
        _expert_kernel,
        out_shape=jax.ShapeDtypeStruct((n_slots, d), F32),
        grid_spec=grid_spec,
        compiler_params=pltpu.CompilerParams(
            dimension_semantics=("arbitrary",), vmem_limit_bytes=VMEM_LIMIT_BYTES),
        name="expert_ffn",
    )(block_expert, xs, w_gate.astype(BF16), w_up.astype(BF16), w_down.astype(BF16),
      slot_w.reshape(n_slots, 1))


DN_BLOCK = CTX_LEN
DN_HB = 2
N_CHUNKS_PER_BLOCK = DN_BLOCK // DN_CHUNK


def _group_of_block(i, n_lat_blocks, blocks_per_batch):
    return jnp.where(i >= n_lat_blocks, 0, 1 + i // blocks_per_batch)


def _dn_prep_kernel(p_ref, ab_ref, cw_ref, alog_ref, dtb_ref, q_ref, k_ref, v_ref, gate_ref, *,
                    n_lat_blocks):
    i = pl.program_id(0)
    nrow = DN_BLOCK
    seg = jnp.where(i >= n_lat_blocks, CTX_LEN, GRID_W)
    r = lax.broadcasted_iota(jnp.int32, (nrow, 1), 0)
    pos = r & (seg - 1)
    not_first = pos != 0
    not_last = pos != seg - 1
    outs = (q_ref, k_ref, v_ref)
    for part in range(3):
        for h in range(DN_HEADS):
            col = part * D_MODEL + h * DN_DK
            x = p_ref[:, col:col + DN_DK]
            cw = cw_ref[:, col:col + DN_DK]
            xp = jnp.where(not_first, pltpu.roll(x, 1, axis=0), 0.0)
            xn = jnp.where(not_last, pltpu.roll(x, nrow - 1, axis=0), 0.0)
            y = cw[0:1] * xp + cw[1:2] * x + cw[2:3] * xn
            y = y * jax.nn.sigmoid(y)
            if part < 2:
                y = y * lax.rsqrt(jnp.sum(y * y, axis=-1, keepdims=True) + EPS)
            if part == 0:
                y = y * DN_DK ** -0.5
            outs[part][:, h * DN_DK:(h + 1) * DN_DK] = y

    ab = ab_ref[...]
    nd = 2 * DN_HEADS
    a = ab[:, :nd] + dtb_ref[...]
    softplus = jnp.maximum(a, 0.0) + jnp.log(1.0 + jnp.exp(-jnp.abs(a)))
    g = -jnp.exp(alog_ref[...]) * softplus
    beta = jax.nn.sigmoid(ab[:, nd:])
    cpos = r & (DN_CHUNK - 1)
    gp, gs = g, g
    sh = 1
    while sh < DN_CHUNK:
        gp = gp + jnp.where(cpos >= sh, pltpu.roll(gp, sh, axis=0), 0.0)
        gs = gs + jnp.where(cpos < DN_CHUNK - sh, pltpu.roll(gs, nrow - sh, axis=0), 0.0)
        sh *= 2
    colid = lax.broadcasted_iota(jnp.int32, (1, nd), 1)
    gate_ref[:, :nd] = jnp.where(colid < DN_HEADS, gp, gs)
    gate_ref[:, nd:] = beta


def _dn_prep(p, p_ab, conv_w, a_log, dt_bias, n_lat_blocks):
    t = p.shape[0]
    d = D_MODEL
    nd = 2 * DN_HEADS
    nblk = t // DN_BLOCK
    row = lambda i: (i, 0)
    fixed = lambda i: (0, 0)
    return pl.pallas_call(
        functools.partial(_dn_prep_kernel, n_lat_blocks=n_lat_blocks),
        out_shape=[jax.ShapeDtypeStruct((t, d), F32)] * 3 + [jax.ShapeDtypeStruct((t, 2 * nd), F32)],
        grid=(nblk,),
        in_specs=[pl.BlockSpec((DN_BLOCK, 3 * d), row),
                  pl.BlockSpec((DN_BLOCK, 2 * nd), row),
                  pl.BlockSpec((3, 3 * d), fixed),
                  pl.BlockSpec((1, nd), fixed),
                  pl.BlockSpec((1, nd), fixed)],
        out_specs=[pl.BlockSpec((DN_BLOCK, d), row)] * 3 + [pl.BlockSpec((DN_BLOCK, 2 * nd), row)],
        compiler_params=pltpu.CompilerParams(
            dimension_semantics=("arbitrary",), vmem_limit_bytes=VMEM_LIMIT_BYTES),
        name="dn_prep",
    )(p, p_ab, conv_w, a_log.reshape(1, nd), dt_bias.reshape(1, nd))


def _dotb(a, b):
    return jnp.dot(a.astype(BF16), b.astype(BF16), preferred_element_type=F32)


def _dotb_nt(a, b):
    return lax.dot_general(a.astype(BF16), b.astype(BF16), (((1,), (1,)), ((), ())),
                           preferred_element_type=F32)


def _dotb_tn(a, b):
    return lax.dot_general(a.astype(BF16), b.astype(BF16), (((0,), (0,)), ((), ())),
                           preferred_element_type=F32)


def _unit_tri_inverses(mats, ii, jj):
    eye = (ii == jj).astype(F32)
    diag8 = (ii >> 3) == (jj >> 3)
    n = [-jnp.where(diag8, a, 0.0) for a in mats]
    n2 = [_dotb(x, x) for x in n]
    m = [eye + x for x in n]
    m = [x + _dotb(x, y) for x, y in zip(m, n2)]
    n4 = [_dotb(x, x) for x in n2]
    m = [x + _dotb(x, y) for x, y in zip(m, n4)]
    sh = 3
    while (1 << sh) < DN_CHUNK:
        off = ((ii >> (sh + 1)) == (jj >> (sh + 1))) & ((ii >> sh) != (jj >> sh))
        cm = [_dotb(jnp.where(off, a, 0.0), x) for a, x in zip(mats, m)]
        m = [x - _dotb(x, y) for x, y in zip(m, cm)]
        sh += 1
    return m


def _dn_scan_kernel(qf_ref, kf_ref, vf_ref, gcf_ref, grf_ref, qb_ref, kb_ref, vb_ref, gcb_ref, grb_ref,
                    of_ref, ob_ref, s_ref):
    @pl.when(pl.program_id(2) == 0)
    def _():
        s_ref[...] = jnp.zeros_like(s_ref)

    c = DN_CHUNK
    ncb = N_CHUNKS_PER_BLOCK
    ii = lax.broadcasted_iota(jnp.int32, (c, c), 0)
    jj = lax.broadcasted_iota(jnp.int32, (c, c), 1)
    incl = (ii >= jj, ii <= jj)
    strict = (ii > jj, ii < jj)
    dirs = ((qf_ref, kf_ref, vf_ref, gcf_ref, grf_ref, of_ref),
            (qb_ref, kb_ref, vb_ref, gcb_ref, grb_ref, ob_ref))
    items = [(d, hh, ci) for d in range(2) for hh in range(DN_HB) for ci in range(ncb)]

    def rows(ci):
        return slice(ci * c, (ci + 1) * c)

    def cols(hh):
        return slice(hh * DN_DK, (hh + 1) * DN_DK)

    q = [dirs[d][0][rows(ci), cols(hh)] for d, hh, ci in items]
    k = [dirs[d][1][rows(ci), cols(hh)] for d, hh, ci in items]
    v = [dirs[d][2][rows(ci), cols(hh)] for d, hh, ci in items]
    gc = [dirs[d][3][hh, rows(ci), d:d + 1] for d, hh, ci in items]
    gr = [dirs[d][4][hh, d:d + 1, rows(ci)] for d, hh, ci in items]
    beta = [dirs[d][3][hh, rows(ci), 2 + d:3 + d] for d, hh, ci in items]

    decay = [jnp.where(incl[it[0]], jnp.exp(jnp.where(incl[it[0]], x - y, 0.0)), 0.0)
             for it, x, y in zip(items, gc, gr)]
    kb = [x * y for x, y in zip(k, beta)]
    a = [_dotb_nt(x, y) * jnp.where(strict[it[0]], z, 0.0) for it, x, y, z in zip(items, kb, k, decay)]
    attn = [_dotb_nt(x, y) * z for x, y, z in zip(q, k, decay)]
    tinv = _unit_tri_inverses(a, ii, jj)
    eg = [jnp.exp(x) for x in gc]
    uw = [_dotb(t, jnp.concatenate([x * y, z * e], axis=-1))
          for t, x, y, z, e in zip(tinv, v, beta, kb, eg)]
    g_last = [x[0:1] if it[0] else x[c - 1:c] for it, x in zip(items, gc)]
    wq = [jnp.concatenate([x[:, DN_DV:], y * e], axis=0) for x, y, e in zip(uw, q, eg)]
    k_dec = [x * jnp.exp(y - z) for x, y, z in zip(k, g_last, gc)]
    s_dec = [jnp.exp(x) for x in g_last]

    chains = [(d, hh) for d in range(2) for hh in range(DN_HB)]
    state = [s_ref[d, hh] for d, hh in chains]
    for step in range(ncb):
        cur = [items.index((d, hh, ncb - 1 - step if d else step)) for d, hh in chains]
        ws = [_dotb(wq[n], s) for n, s in zip(cur, state)]
        v_new = [uw[n][:, :DN_DV] - x[:c] for n, x in zip(cur, ws)]
        o = [x[c:] + _dotb(attn[n], y) for n, x, y in zip(cur, ws, v_new)]
        state = [s * s_dec[n] + _dotb_tn(k_dec[n], y) for n, s, y in zip(cur, state, v_new)]
        for n, x in zip(cur, o):
            d, hh, ci = items[n]
            dirs[d][5][rows(ci), cols(hh)] = x
    for (d, hh), s in zip(chains, state):
        s_ref[d, hh] = s


def _dn_scan(q, k, v, gates, n_lat_blocks, blocks_per_batch):
    t, d = q.shape
    bsz = n_lat_blocks // blocks_per_batch
    g4 = gates.reshape(t, 4, DN_HEADS)
    gcol = jnp.transpose(g4, (2, 0, 1))
    grow = jnp.transpose(g4, (2, 1, 0))

    def blk_f(b, s):
        return jnp.where(s == 0, n_lat_blocks + b, b * blocks_per_batch + s - 1)

    def blk_b(b, s):
        return jnp.where(s == 0, n_lat_blocks + b, b * blocks_per_batch + blocks_per_batch - s)

    hw = DN_HB * DN_DK

    def specs(blk):
        return [pl.BlockSpec((DN_BLOCK, hw), lambda b, hg, s: (blk(b, s), hg))] * 3 + [
            pl.BlockSpec((DN_HB, DN_BLOCK, 4), lambda b, hg, s: (hg, blk(b, s), 0)),
            pl.BlockSpec((DN_HB, 4, DN_BLOCK), lambda b, hg, s: (hg, 0, blk(b, s)))]

    return pl.pallas_call(
        _dn_scan_kernel,
        out_shape=[jax.ShapeDtypeStruct((t, d), F32)] * 2,
        grid=(bsz, DN_HEADS // DN_HB, 1 + blocks_per_batch),
        in_specs=specs(blk_f) + specs(blk_b),
        out_specs=[pl.BlockSpec((DN_BLOCK, hw), lambda b, hg, s: (blk_f(b, s), hg)),
                   pl.BlockSpec((DN_BLOCK, hw), lambda b, hg, s: (blk_b(b, s), hg))],
        scratch_shapes=[pltpu.VMEM((2, DN_HB, DN_DK, DN_DV), F32)],
        compiler_params=pltpu.CompilerParams(
            dimension_semantics=("arbitrary", "arbitrary", "arbitrary"),
            vmem_limit_bytes=VMEM_LIMIT_BYTES),
        name="dn_scan",
    )(q, k, v, gcol, grow, q, k, v, gcol, grow)


def _dn_out_kernel(of_ref, ob_ref, z_ref, on_ref, w_ref, x_ref, gate_ref, o_ref):
    z = z_ref[...]
    zs = z * jax.nn.sigmoid(z)
    parts = []
    for h in range(DN_HEADS):
        cols = slice(h * DN_DV, (h + 1) * DN_DV)
        o = of_ref[:, cols] + ob_ref[:, cols]
        o = o * lax.rsqrt(jnp.mean(o * o, axis=-1, keepdims=True) + EPS)
        parts.append(o)
    y = jnp.concatenate(parts, axis=-1) * on_ref[...] * zs
    o_ref[...] = x_ref[...] + gate_ref[0] * jnp.dot(y.astype(BF16), w_ref[...],
                                                    preferred_element_type=F32)


def _dn_out(o_f, o_b, p, out_norm, w_out, xs, gate3, n_lat_blocks, blocks_per_batch):
    t, d = xs.shape
    row = lambda i: (i, 0)
    fixed = lambda i: (0, 0)
    grp = lambda i: (_group_of_block(i, n_lat_blocks, blocks_per_batch), 0, 0)
    return pl.pallas_call(
        _dn_out_kernel,
        out_shape=jax.ShapeDtypeStruct((t, d), F32),
        grid=(t // DN_BLOCK,),
        in_specs=[pl.BlockSpec((DN_BLOCK, d), row),
                  pl.BlockSpec((DN_BLOCK, d), row),
                  pl.BlockSpec((DN_BLOCK, d), lambda i: (i, 3)),
                  pl.BlockSpec((1, d), fixed),
                  pl.BlockSpec((d, d), fixed),
                  pl.BlockSpec((DN_BLOCK, d), row),
                  pl.BlockSpec((1, 1, d), grp)],
        out_specs=pl.BlockSpec((DN_BLOCK, d), row),
        compiler_params=pltpu.CompilerParams(
            dimension_semantics=("arbitrary",), vmem_limit_bytes=VMEM_LIMIT_BYTES),
        name="dn_out",
    )(o_f, o_b, p, jnp.tile(out_norm, DN_HEADS).reshape(1, d), w_out.astype(BF16), xs, gate3)


def _deltanet_layer(xs, h, w_in, conv_w, a_log, dt_bias, out_norm, w_out, gate3, n_lat_blocks,
                    blocks_per_batch):
    d = D_MODEL
    p = _mm(h, w_in[:, :4 * d], tn=1024)
    p_ab = jnp.dot(h, w_in[:, 4 * d:], precision=lax.Precision.HIGHEST)
    q, k, v, gates = _dn_prep(p, p_ab, conv_w, a_log, dt_bias, n_lat_blocks)
    o_f, o_b = _dn_scan(q, k, v, gates, n_lat_blocks, blocks_per_batch)
    return _dn_out(o_f, o_b, p, out_norm, w_out, xs, gate3, n_lat_blocks, blocks_per_batch)


HY_N2 = 256
HY_CB = 8
HY_TOK_TILE = 512
HY_FILT_TILE = 512


def _dft_constants(nr):
    n1, n2 = 2 * nr, HY_N2
    n = n1 * n2
    a1 = np.arange(n1)
    f1 = np.exp(-2j * np.pi * np.outer(a1, a1) / n1)
    lhs_fwd = np.concatenate([f1.real[:, :nr], f1.imag[:, :nr]], axis=0)
    lhs_inv = np.concatenate([f1.real[:nr, :], f1.imag[:nr, :]], axis=1) / n
    tw = np.exp(-2j * np.pi * np.outer(a1, np.arange(n2)) / n)
    a2 = np.arange(n2)
    f2 = np.exp(-2j * np.pi * np.outer(a2, a2) / n2)
    w_fwd = np.block([[f2.real, f2.imag], [-f2.imag, f2.real]])
    w_inv = np.block([[f2.real, -f2.imag], [f2.imag, f2.real]])
    return (jnp.asarray(lhs_fwd, BF16), jnp.asarray(lhs_inv, BF16), jnp.asarray(tw.real, F32),
            jnp.asarray(tw.imag, F32), jnp.asarray(w_fwd, BF16), jnp.asarray(w_inv, BF16))


def _hy_dft(x3, lhs_fwd, twr, twi, w_fwd):
    n1 = twr.shape[0]
    a = [jnp.dot(lhs_fwd, x3[c].astype(BF16), preferred_element_type=F32) for c in range(x3.shape[0])]
    br = jnp.concatenate([t[:n1] * twr - t[n1:] * twi for t in a], axis=0)
    bi = jnp.concatenate([t[:n1] * twi + t[n1:] * twr for t in a], axis=0)
    b = jnp.concatenate([br, bi], axis=1)
    return jnp.dot(b.astype(BF16), w_fwd, preferred_element_type=F32)


def _hy_idft(p, cb, lhs_inv, twr, twi, w_inv):
    n1, n2 = twr.shape
    c = jnp.dot(p.astype(BF16), w_inv, preferred_element_type=F32)
    out = []
    for ch in range(cb):
        cr = c[ch * n1:(ch + 1) * n1, :n2]
        ci = c[ch * n1:(ch + 1) * n1, n2:]
        d = jnp.concatenate([cr * twr + ci * twi, ci * twr - cr * twi], axis=0)
        out.append(jnp.dot(lhs_inv, d.astype(BF16), preferred_element_type=F32))
    return out


def _hy_spectrum_kernel(hf_ref, hb_ref, lf_ref, twr_ref, twi_ref, wf_ref, o_ref):
    cb, nr, n2 = hf_ref.shape
    first = ((lax.broadcasted_iota(jnp.int32, (nr, n2), 0) == 0)
             & (lax.broadcasted_iota(jnp.int32, (nr, n2), 1) == 0))
    hb = jnp.where(first, 0.0, hb_ref[...])
    consts = (lf_ref[...], twr_ref[...], twi_ref[...], wf_ref[...])
    xf = _hy_dft(hf_ref[...], *consts)
    xb = _hy_dft(hb, *consts)
    n1 = 2 * nr
    o_ref[...] = jnp.concatenate([xf[:, :n2] + xb[:, :n2], xf[:, n2:] - xb[:, n2:]],
                                 axis=1).reshape(cb, n1, 2 * n2)


def _hy_spectrum(filt, consts):
    d = D_MODEL
    l = filt.shape[1]
    nr = l // HY_N2
    n1 = 2 * nr
    lhs_fwd, _, twr, twi, w_fwd, _ = consts
    cpo = d // HY_CB
    fixed2 = lambda o, c: (0, 0)
    return pl.pallas_call(
        _hy_spectrum_kernel,
        out_shape=jax.ShapeDtypeStruct((HY_ORDER * d, n1, 2 * HY_N2), F32),
        grid=(HY_ORDER, cpo),
        in_specs=[pl.BlockSpec((HY_CB, nr, HY_N2), lambda o, c: (2 * o * cpo + c, 0, 0)),
                  pl.BlockSpec((HY_CB, nr, HY_N2), lambda o, c: ((2 * o + 1) * cpo + c, 0, 0)),
                  pl.BlockSpec(lhs_fwd.shape, fixed2),
                  pl.BlockSpec(twr.shape, fixed2),
                  pl.BlockSpec(twi.shape, fixed2),
                  pl.BlockSpec(w_fwd.shape, fixed2)],
        out_specs=pl.BlockSpec((HY_CB, n1, 2 * HY_N2), lambda o, c: (o * cpo + c, 0, 0)),
        compiler_params=pltpu.CompilerParams(
            dimension_semantics=("arbitrary", "arbitrary"), vmem_limit_bytes=VMEM_LIMIT_BYTES),
        name="hy_spectrum",
    )(filt.reshape(-1, nr, HY_N2), filt.reshape(-1, nr, HY_N2), lhs_fwd, twr, twi, w_fwd)


def _hy_conv_kernel(z_ref, g_ref, k_ref, bias_ref, lf_ref, li_ref, twr_ref, twi_ref, wf_ref, wi_ref,
                    o_ref):
    cb, nr, n2 = z_ref.shape
    twr, twi = twr_ref[...], twi_ref[...]
    z = z_ref[...]
    x = _hy_dft(z, lf_ref[...], twr, twi, wf_ref[...])
    kk = k_ref[...].reshape(x.shape)
    xr, xi, kr, ki = x[:, :n2], x[:, n2:], kk[:, :n2], kk[:, n2:]
    p = jnp.concatenate([xr * kr - xi * ki, xr * ki + xi * kr], axis=1)
    conv = _hy_idft(p, cb, li_ref[...], twr, twi, wi_ref[...])
    for c in range(cb):
        o_ref[c] = g_ref[c] * (conv[c] + bias_ref[c] * z[c])


def _hy_conv(z, z_part, gate, gate_part, khat, order, bias, consts, bsz):
    d = D_MODEL
    l = z.shape[1] // bsz
    nr = l // HY_N2
    n1 = 2 * nr
    cpo = d // HY_CB
    lhs_fwd, lhs_inv, twr, twi, w_fwd, w_inv = consts
    fixed2 = lambda c, b: (0, 0)
    out = pl.pallas_call(
        _hy_conv_kernel,
        out_shape=jax.ShapeDtypeStruct((d, bsz * nr, HY_N2), F32),
        grid=(cpo, bsz),
        in_specs=[pl.BlockSpec((HY_CB, nr, HY_N2), lambda c, b: (z_part * cpo + c, b, 0)),
                  pl.BlockSpec((HY_CB, nr, HY_N2), lambda c, b: (gate_part * cpo + c, b, 0)),
                  pl.BlockSpec((HY_CB, n1, 2 * HY_N2), lambda c, b: (order * cpo + c, 0, 0)),
                  pl.BlockSpec((HY_CB, 1, 1), lambda c, b: (c, 0, 0)),
                  pl.BlockSpec(lhs_fwd.shape, fixed2),
                  pl.BlockSpec(lhs_inv.shape, fixed2),
                  pl.BlockSpec(twr.shape, fixed2),
                  pl.BlockSpec(twi.shape, fixed2),
                  pl.BlockSpec(w_fwd.shape, fixed2),
                  pl.BlockSpec(w_inv.shape, fixed2)],
        out_specs=pl.BlockSpec((HY_CB, nr, HY_N2), lambda c, b: (c, b, 0)),
        compiler_params=pltpu.CompilerParams(
            dimension_semantics=("arbitrary", "arbitrary"), vmem_limit_bytes=VMEM_LIMIT_BYTES),
        name="hy_conv",
    )(z.reshape(-1, bsz * nr, HY_N2), gate.reshape(-1, bsz * nr, HY_N2), khat,
      bias.reshape(d, 1, 1), lhs_fwd, lhs_inv, twr, twi, w_fwd, w_inv)
    return out.reshape(d, bsz * l)


def _hy_ctx_kernel(z_ref, g_ref, hf_ref, hb_ref, bias_ref, wf_ref, wi_ref, o_ref, *, bsz):
    l = hf_ref.shape[1]
    wf, wi = wf_ref[...], wi_ref[...]
    hb = jnp.where(lax.broadcasted_iota(jnp.int32, (1, l), 1) == 0, 0.0, hb_ref[...])
    kf = jnp.dot(hf_ref[...].astype(BF16), wf, preferred_element_type=F32)
    kb = jnp.dot(hb.astype(BF16), wf, preferred_element_type=F32)
    n = 2 * l
    kr, ki = kf[:, :n] + kb[:, :n], kf[:, n:] - kb[:, n:]
    bias = bias_ref[...]
    for b in range(bsz):
        z = z_ref[:, b * l:(b + 1) * l]
        x = jnp.dot(z.astype(BF16), wf, preferred_element_type=F32)
        xr, xi = x[:, :n], x[:, n:]
        p = jnp.concatenate([xr * kr - xi * ki, xr * ki + xi * kr], axis=1)
        conv = jnp.dot(p.astype(BF16), wi, preferred_element_type=F32)
        o_ref[:, b * l:(b + 1) * l] = g_ref[:, b * l:(b + 1) * l] * (conv + bias * z)


def _hy_ctx(z, z_part, gate, gate_part, filt, order, bias, bsz):
    d = D_MODEL
    l = filt.shape[1]
    n = 2 * l
    ang = 2 * np.pi * np.outer(np.arange(l), np.arange(n)) / n
    w_fwd = jnp.asarray(np.concatenate([np.cos(ang), -np.sin(ang)], axis=1), BF16)
    w_inv = jnp.asarray(np.concatenate([np.cos(ang.T), -np.sin(ang.T)], axis=0) / n, BF16)
    cb = 256
    nblk = d // cb
    fixed = lambda c: (0, 0)
    return pl.pallas_call(
        functools.partial(_hy_ctx_kernel, bsz=bsz),
        out_shape=jax.ShapeDtypeStruct((d, bsz * l), F32),
        grid=(nblk,),
        in_specs=[pl.BlockSpec((cb, bsz * l), lambda c: (z_part * nblk + c, 0)),
                  pl.BlockSpec((cb, bsz * l), lambda c: (gate_part * nblk + c, 0)),
                  pl.BlockSpec((cb, l), lambda c: (2 * order * nblk + c, 0)),
                  pl.BlockSpec((cb, l), lambda c: ((2 * order + 1) * nblk + c, 0)),
                  pl.BlockSpec((cb, 1), lambda c: (c, 0)),
                  pl.BlockSpec(w_fwd.shape, fixed),
                  pl.BlockSpec(w_inv.shape, fixed)],
        out_specs=pl.BlockSpec((cb, bsz * l), lambda c: (c, 0)),
        compiler_params=pltpu.CompilerParams(
            dimension_semantics=("arbitrary",), vmem_limit_bytes=VMEM_LIMIT_BYTES),
        name="hy_ctx_conv",
    )(z, gate, filt, filt, bias.reshape(d, 1), w_fwd, w_inv)


def _hy_filter_kernel(band_ref, w1t_ref, w1c_ref, w1s_ref, b1_ref, fr_ref, w2_ref, b2_ref, w3_ref,
                      delta_ref, o_ref, *, l):
    tl = o_ref.shape[1]
    d = D_MODEL
    hp = lax.Precision.HIGHEST
    pos = (lax.broadcasted_iota(jnp.int32, (1, tl), 1) + pl.program_id(0) * tl).astype(F32)
    t = pos / max(l - 1, 1)
    ang = ((2 * math.pi / l) * pos) * band_ref[...]
    fr = fr_ref[...]
    pre = (w1t_ref[...] * t + jnp.dot(w1c_ref[...], jnp.cos(ang), precision=hp)
           + jnp.dot(w1s_ref[...], -jnp.sin(ang), precision=hp) + b1_ref[...])
    hdn = jnp.sin(fr * pre)
    hdn = jnp.sin(fr * (jnp.dot(w2_ref[...], hdn, precision=hp) + b2_ref[...]))
    window = jnp.exp(-t * delta_ref[...])
    for part in range(2 * HY_ORDER):
        rows = slice(part * d, (part + 1) * d)
        o_ref[rows, :] = jnp.dot(w3_ref[rows, :], hdn, precision=hp) * window


def _hy_filter(l, w1, b1, freq, w2, b2, w3):
    d = D_MODEL
    nb = HY_BANDS
    tl = min(HY_FILT_TILE, l)
    col = lambda v: v.reshape(-1, 1)
    bands = jnp.linspace(1e-4, nb - 1, nb, dtype=F32)
    deltas = jnp.abs(jnp.linspace(math.log(HY_TARGET) / HY_SLOW, math.log(HY_TARGET) / HY_FAST, d, dtype=F32))
    w1t = w1.T
    args = (col(bands), w1t[:, 0:1], w1t[:, 1:1 + nb], w1t[:, 1 + nb:], col(b1), col(freq), w2.T, col(b2),
            w3.T, col(deltas))
    return pl.pallas_call(
        functools.partial(_hy_filter_kernel, l=l),
        out_shape=jax.ShapeDtypeStruct((2 * HY_ORDER * d, l), F32),
        grid=(l // tl,),
        in_specs=[pl.BlockSpec(a.shape, lambda j: (0, 0)) for a in args],
        out_specs=pl.BlockSpec((2 * HY_ORDER * d, tl), lambda j: (0, j)),
        compiler_params=pltpu.CompilerParams(
            dimension_semantics=("arbitrary",), vmem_limit_bytes=VMEM_LIMIT_BYTES),
        name="hy_filter",
    )(*args)


def _hy_inproj_kernel(h_ref, wt_ref, cw_ref, o_ref, *, seg):
    nch = wt_ref.shape[0]
    tm = h_ref.shape[0]
    hb = h_ref[...].astype(BF16)
    pos = lax.broadcasted_iota(jnp.int32, (1, tm), 1) & (seg - 1)
    not_first = pos != 0
    not_last = pos != seg - 1
    sub = 512
    for j in range(nch // sub):
        rows = slice(j * sub, (j + 1) * sub)
        p = _dotb_nt(wt_ref[rows, :], hb)
        cw = cw_ref[rows, :]
        prev = jnp.where(not_first, pltpu.roll(p, 1, axis=1), 0.0)
        nxt = jnp.where(not_last, pltpu.roll(p, tm - 1, axis=1), 0.0)
        o_ref[rows, :] = cw[:, 0:1] * prev + cw[:, 1:2] * p + cw[:, 2:3] * nxt


def _hy_inproj(h, w_in, conv_w, first_tile, n_tiles, seg):
    k = h.shape[1]
    nch = w_in.shape[1]
    tm = HY_TOK_TILE
    return pl.pallas_call(
        functools.partial(_hy_inproj_kernel, seg=seg),
        out_shape=jax.ShapeDtypeStruct((nch, n_tiles * tm), F32),
        grid=(n_tiles,),
        in_specs=[pl.BlockSpec((tm, k), lambda i: (first_tile + i, 0)),
                  pl.BlockSpec((nch, k), lambda i: (0, 0)),
                  pl.BlockSpec((nch, 3), lambda i: (0, 0))],
        out_specs=pl.BlockSpec((nch, tm), lambda i: (0, i)),
        compiler_params=pltpu.CompilerParams(
            dimension_semantics=("arbitrary",), vmem_limit_bytes=VMEM_LIMIT_BYTES),
        name="hy_inproj",
    )(h, w_in.T.astype(BF16), conv_w.T)


def _hy_out_kernel(zl_ref, zc_ref, w_ref, x_ref, gate_ref, o_ref, *, n_lat_tiles):
    z = jnp.where(pl.program_id(0) >= n_lat_tiles, zc_ref[...], zl_ref[...])
    o_ref[...] = x_ref[...] + gate_ref[0] * _dotb_tn(z, w_ref[...])


def _hy_out(z_lat, z_ctx, w_out, xs, gate3, blocks_per_batch):
    t, d = xs.shape
    tm = HY_TOK_TILE
    n_lat_tiles = z_lat.shape[1] // tm
    per_tile = tm // DN_BLOCK
    grp = lambda i: (_group_of_block(i * per_tile, n_lat_tiles * per_tile, blocks_per_batch), 0, 0)
    return pl.pallas_call(
        functools.partial(_hy_out_kernel, n_lat_tiles=n_lat_tiles),
        out_shape=jax.ShapeDtypeStruct((t, d), F32),
        grid=(t // tm,),
        in_specs=[pl.BlockSpec((d, tm), lambda i: (0, jnp.minimum(i, n_lat_tiles - 1))),
                  pl.BlockSpec((d, tm), lambda i: (0, 0)),
                  pl.BlockSpec((d, d), lambda i: (0, 0)),
                  pl.BlockSpec((tm, d), lambda i: (i, 0)),
                  pl.BlockSpec((1, 1, d), grp)],
        out_specs=pl.BlockSpec((tm, d), lambda i: (i, 0)),
        compiler_params=pltpu.CompilerParams(
            dimension_semantics=("arbitrary",), vmem_limit_bytes=VMEM_LIMIT_BYTES),
        name="hy_out",
    )(z_lat, z_ctx, w_out.astype(BF16), xs, gate3)


def _hyena_layer(xs, h, w_in, conv_w, f_w1, f_b1, f_freq, f_w2, f_b2, f_w3, bias, w_out, gate3, bsz, seq):
    n_lat_tiles = bsz * seq // HY_TOK_TILE
    assert bsz * CTX_LEN == HY_TOK_TILE
    p_lat = _hy_inproj(h, w_in, conv_w, 0, n_lat_tiles, GRID_W)
    p_ctx = _hy_inproj(h, w_in, conv_w, n_lat_tiles, 1, CTX_LEN)
    fargs = (f_w1, f_b1, f_freq, f_w2, f_b2, f_w3)
    consts = _dft_constants(seq // HY_N2)
    khat = _hy_spectrum(_hy_filter(seq, *fargs), consts)
    filt_ctx = _hy_filter(CTX_LEN, *fargs)
    z_lat, z_ctx = p_lat, p_ctx
    for n in range(HY_ORDER):
        z_lat = _hy_conv(z_lat, 0, p_lat, n + 1, khat, n, bias[n], consts, bsz)
        z_ctx = _hy_ctx(z_ctx, 0, p_ctx, n + 1, filt_ctx, n, bias[n], bsz)
    return _hy_out(z_lat, z_ctx, w_out, xs, gate3, seq // DN_BLOCK)


def _rmsnorm(x, gain):
    y = x * lax.rsqrt(jnp.mean(x * x, axis=-1, keepdims=True) + EPS)
    return y * gain


def _modulate(x, gain, shift, scale):
    return _rmsnorm(x, gain) * (1 + scale) + shift


def _l2norm(t):
    return t * lax.rsqrt(jnp.sum(t * t, axis=-1, keepdims=True) + EPS)


def _short_conv(x, w, on_grid):
    b, l, ch = x.shape
    xs = x.reshape(b, l // GRID_W, GRID_W, ch) if on_grid else x.reshape(b, 1, l, ch)
    n = xs.shape[2]
    xp = jnp.pad(xs, ((0, 0), (0, 0), (1, 1), (0, 0)))
    y = w[0] * xp[:, :, 0:n] + w[1] * xp[:, :, 1:n + 1] + w[2] * xp[:, :, 2:n + 2]
    return y.reshape(b, l, ch)


def _gated_delta_chunked(q, k, v, g, beta, s0):
    b, h, l, dk = q.shape
    dv = v.shape[-1]
    c = DN_CHUNK
    n = l // c
    q = q.reshape(b, h, n, c, dk)
    k = k.reshape(b, h, n, c, dk)
    v = v.reshape(b, h, n, c, dv)
    g = jnp.cumsum(g.reshape(b, h, n, c), axis=-1)
    beta = beta.reshape(b, h, n, c, 1)
    pos = jnp.arange(c)
    incl = pos[:, None] >= pos[None, :]
    strict = pos[:, None] > pos[None, :]
    decay = jnp.exp(jnp.where(incl, g[..., :, None] - g[..., None, :], -jnp.inf))
    kb = k * beta
    a_mat = jnp.einsum('bhnid,bhnjd->bhnij', kb, k) * jnp.where(strict, decay, 0.0)
    rhs = jnp.concatenate([v * beta, kb * jnp.exp(g)[..., None]], axis=-1)
    sol = lax.linalg.triangular_solve(a_mat + jnp.eye(c, dtype=a_mat.dtype), rhs,
                                      left_side=True, lower=True, unit_diagonal=True)
    u, w = sol[..., :dv], sol[..., dv:]
    attn = jnp.einsum('bhnid,bhnjd->bhnij', q, k) * decay
    g_last = g[..., -1:]
    q_dec = q * jnp.exp(g)[..., None]
    k_dec = k * jnp.exp(g_last - g)[..., None]

    def step(s, inp):
        qd, kd, uu, ww, at, gl = inp
        v_new = uu - jnp.einsum('bhck,bhkv->bhcv', ww, s)
        o = jnp.einsum('bhck,bhkv->bhcv', qd, s) + jnp.einsum('bhcs,bhsv->bhcv', at, v_new)
        s = s * jnp.exp(gl)[..., None] + jnp.einsum('bhck,bhcv->bhkv', kd, v_new)
        return s, o

    xs = tuple(jnp.moveaxis(t, 2, 0) for t in (q_dec, k_dec, u, w, attn, g_last))
    s_final, o = lax.scan(step, s0, xs)
    o = jnp.moveaxis(o, 0, 2).reshape(b, h, l, dv)
    return o, s_final


def _deltanet_mixer(p_ctx, p_lat, conv_w, a_log, dt_bias, out_norm):
    d = D_MODEL
    nh = DN_HEADS

    def project(p, on_grid):
        b, l, _ = p.shape
        qkv = jax.nn.silu(_short_conv(p[..., :3 * d], conv_w, on_grid))
        z = p[..., 3 * d:4 * d]
        a = p[..., 4 * d:4 * d + 2 * nh].reshape(b, l, 2, nh)
        bb = p[..., 4 * d + 2 * nh:].reshape(b, l, 2, nh)

        def heads(t):
            return jnp.transpose(t.reshape(b, l, nh, -1), (0, 2, 1, 3))

        q, k, v = (heads(t) for t in jnp.split(qkv, 3, axis=-1))
        q = _l2norm(q) * DN_DK ** -0.5
        k = _l2norm(k)
        g = -jnp.exp(a_log) * jax.nn.softplus(a + dt_bias)
        g = jnp.transpose(g, (2, 0, 3, 1))
        beta = jnp.transpose(jax.nn.sigmoid(bb), (2, 0, 3, 1))
        return q, k, v, g, beta, z

    def scan_both(q, k, v, g, beta, s_f, s_b):
        o_f, s_f = _gated_delta_chunked(q, k, v, g[0], beta[0], s_f)
        rev = lambda t: jnp.flip(t, axis=2)
        o_b, s_b = _gated_delta_chunked(rev(q), rev(k), rev(v), rev(g[1]), rev(beta[1]), s_b)
        return o_f + rev(o_b), s_f, s_b

    def finish(o, z):
        b, _, l, _ = o.shape
        o = jnp.transpose(o, (0, 2, 1, 3))
        o = o * lax.rsqrt(jnp.mean(o * o, axis=-1, keepdims=True) + EPS) * out_norm
        o = o * jax.nn.silu(z.reshape(b, l, nh, DN_DV))
        return o.reshape(b, l, d)

    qc, kc, vc, gc, bc, zc = project(p_ctx, False)
    s0 = jnp.zeros((p_ctx.shape[0], nh, DN_DK, DN_DV), F32)
    o_c, s_f, s_b = scan_both(qc, kc, vc, gc, bc, s0, s0)
    ql, kl, vl, gla, bl, zl = project(p_lat, True)
    o_l, _, _ = scan_both(ql, kl, vl, gla, bl, s_f, s_b)
    return finish(o_c, zc), finish(o_l, zl)


def _hyena_filters(l, w1, b1, freq, w2, b2, w3):
    pos = jnp.arange(l, dtype=F32)[:, None]
    t = pos / max(l - 1, 1)
    bands = jnp.linspace(1e-4, HY_BANDS - 1, HY_BANDS, dtype=F32)[None, :]
    ang = (2 * math.pi / l) * pos * bands
    feat = jnp.concatenate([t, jnp.cos(ang), -jnp.sin(ang)], axis=-1)
    hp = lax.Precision.HIGHEST
    hdn = jnp.sin(freq * (jnp.dot(feat, w1, precision=hp) + b1))
    hdn = jnp.sin(freq * (jnp.dot(hdn, w2, precision=hp) + b2))
    filt = jnp.dot(hdn, w3, precision=hp).reshape(l, HY_ORDER, 2, D_MODEL)
    deltas = jnp.abs(jnp.linspace(math.log(HY_TARGET) / HY_SLOW, math.log(HY_TARGET) / HY_FAST,
                                  D_MODEL, dtype=F32))
    window = jnp.exp(-t * deltas[None, :])
    return filt * window[:, None, None, :]


def _two_sided_fftconv(u, h_fwd, h_bwd):
    l = u.shape[1]
    k = jnp.concatenate([h_fwd, jnp.zeros_like(h_fwd[:1]), jnp.flip(h_bwd[1:], axis=0)], axis=0)
    kf = jnp.fft.rfft(k, axis=0)
    uf = jnp.fft.rfft(u, n=2 * l, axis=1)
    return jnp.fft.irfft(uf * kf[None], n=2 * l, axis=1)[:, :l]


def _hyena_stream(p, on_grid, conv_w, f_w1, f_b1, f_freq, f_w2, f_b2, f_w3, bias):
    l = p.shape[1]
    p = _short_conv(p, conv_w, on_grid)
    v, x1, x2 = jnp.split(p, 3, axis=-1)
    filt = _hyena_filters(l, f_w1, f_b1, f_freq, f_w2, f_b2, f_w3)
    z = v
    for n, gate in enumerate((x1, x2)):
        conv = _two_sided_fftconv(z, filt[:, n, 0], filt[:, n, 1])
        z = gate * (conv + bias[n] * z)
    return z


def _shortconv_stream(p, on_grid, conv_w):
    bg, cg, xin = jnp.split(p, 3, axis=-1)
    return bg * _short_conv(cg * xin, conv_w, on_grid)


def _route(h, w_router, router_bias):
    t = h.shape[0]
    scores = jax.nn.sigmoid(jnp.dot(h, w_router, precision=lax.Precision.HIGHEST))
    choice = (scores + router_bias).reshape(t, N_GROUPS, EXPERTS_PER_GROUP)
    group_score = lax.top_k(choice, GROUP_SCORE_K)[0].sum(-1)
    group = jnp.argmax(group_score, axis=-1)
    in_group = jnp.take_along_axis(choice, group[:, None, None], axis=1)[:, 0]
    local = lax.top_k(in_group, TOP_K)[1]
    expert = group[:, None] * EXPERTS_PER_GROUP + local
    weight = jnp.take_along_axis(scores, expert, axis=1)
    weight = weight / jnp.sum(weight, axis=-1, keepdims=True)
    return expert.astype(jnp.int32), weight


def _moe_ffn(x, w_router, router_bias, w_gate, w_up, w_down):
    t, d = x.shape
    expert, weight = _route(x, w_router, router_bias)
    a = t * TOP_K
    e_flat = expert.reshape(-1)
    order = jnp.argsort(e_flat)
    e_sorted = e_flat[order]
    tok_sorted = (order // TOP_K).astype(jnp.int32)
    counts = jnp.zeros((N_EXPERTS,), jnp.int32).at[e_flat].add(1)
    start = jnp.cumsum(counts) - counts
    padded = (counts + MOE_BLOCK - 1) // MOE_BLOCK * MOE_BLOCK
    pend = jnp.cumsum(padded)
    pstart = pend - padded
    dest = pstart[e_sorted] + (jnp.arange(a, dtype=jnp.int32) - start[e_sorted])
    n_blocks = -(-a // MOE_BLOCK) + N_EXPERTS
    n_slots = n_blocks * MOE_BLOCK
    slot_tok = jnp.full((n_slots,), t, jnp.int32).at[dest].set(tok_sorted)
    block_start = jnp.arange(n_blocks, dtype=jnp.int32) * MOE_BLOCK
    block_expert = jnp.minimum(jnp.searchsorted(pend, block_start, side='right'),
                               N_EXPERTS - 1).astype(jnp.int32)
    x_pad = jnp.concatenate([x.astype(BF16), jnp.zeros((1, d), BF16)], axis=0)
    xs = x_pad[slot_tok]
    ys = _expert_ffn(xs, block_expert, jnp.ones((n_slots,), F32), w_gate, w_up, w_down)
    slot_of = jnp.zeros((a,), jnp.int32).at[order].set(dest).reshape(t, TOP_K)
    out = weight[:, 0:1] * ys[slot_of[:, 0]] + weight[:, 1:2] * ys[slot_of[:, 1]]
    return out


def _sc_layer_kernel(h_ref, win_ref, cw_ref, wout_ref, x_ref, gate_ref, o_ref, *, n_lat_tiles):
    d = D_MODEL
    tm = h_ref.shape[0]
    seg = jnp.where(pl.program_id(0) >= n_lat_tiles, CTX_LEN, GRID_W)
    pos = lax.broadcasted_iota(jnp.int32, (tm, 1), 0) & (seg - 1)
    hb = h_ref[...].astype(BF16)
    u = (jnp.dot(hb, win_ref[:, d:2 * d], preferred_element_type=F32)
         * jnp.dot(hb, win_ref[:, 2 * d:], preferred_element_type=F32))
    prev = jnp.where(pos != 0, pltpu.roll(u, 1, axis=0), 0.0)
    nxt = jnp.where(pos != seg - 1, pltpu.roll(u, tm - 1, axis=0), 0.0)
    cw = cw_ref[...]
    y = jnp.dot(hb, win_ref[:, :d], preferred_element_type=F32) * (
        cw[0:1] * prev + cw[1:2] * u + cw[2:3] * nxt)
    o_ref[...] = x_ref[...] + gate_ref[0] * jnp.dot(y.astype(BF16), wout_ref[...],
                                                    preferred_element_type=F32)


def _shortconv_layer(xs, h, w_in, conv_w, w_out, gate3, n_lat_tiles, blocks_per_batch):
    t, d = xs.shape
    tm = HY_TOK_TILE
    per_tile = tm // DN_BLOCK
    row = lambda i: (i, 0)
    fixed = lambda i: (0, 0)
    grp = lambda i: (_group_of_block(i * per_tile, n_lat_tiles * per_tile, blocks_per_batch), 0, 0)
    return pl.pallas_call(
        functools.partial(_sc_layer_kernel, n_lat_tiles=n_lat_tiles),
        out_shape=jax.ShapeDtypeStruct((t, d), F32),
        grid=(t // tm,),
        in_specs=[pl.BlockSpec((tm, d), row),
                  pl.BlockSpec((d, 3 * d), fixed),
                  pl.BlockSpec((3, d), fixed),
                  pl.BlockSpec((d, d), fixed),
                  pl.BlockSpec((tm, d), row),
                  pl.BlockSpec((1, 1, d), grp)],
        out_specs=pl.BlockSpec((tm, d), row),
        compiler_params=pltpu.CompilerParams(
            dimension_semantics=("arbitrary",), vmem_limit_bytes=VMEM_LIMIT_BYTES),
        name="shortconv_layer",
    )(h, w_in.astype(BF16), conv_w, w_out.astype(BF16), xs, gate3)


def kernel(x, c, ctx, c_ctx, ada_w, ada_b, norm_mix, norm_ffn, norm_final, dn_w_in, dn_conv, dn_a_log,
           dn_dt_bias, dn_out_norm, dn_w_out, hy_w_in, hy_conv, hy_f_w1, hy_f_b1, hy_f_freq, hy_f_w2,
           hy_f_b2, hy_f_w3, hy_bias, hy_w_out, sc_w_in, sc_conv, sc_w_out, w_router, router_bias,
           moe_w_gate, moe_w_up, moe_w_down):
    d = D_MODEL
    bsz, seq, _ = x.shape
    n_ctx = bsz * CTX_LEN
    n_lat = bsz * seq
    silu_c = jax.nn.silu(c)
    silu_cc = jax.nn.silu(c_ctx)
    hp = lax.Precision.HIGHEST

    xs = jnp.concatenate([x.reshape(n_lat, d), ctx.reshape(n_ctx, d)], axis=0)

    def rows(ctx_vec, lat_vecs):
        return jnp.concatenate([jnp.repeat(lat_vecs, seq, axis=0),
                                jnp.broadcast_to(ctx_vec[None], (n_ctx, d))], axis=0)

    for i in range(DEPTH):
        kind, j = i % N_MIXERS, i // N_MIXERS
        ml = jnp.split(jnp.dot(silu_c, ada_w[i], precision=hp) + ada_b[i], N_MOD, axis=-1)
        mc = jnp.split(jnp.dot(silu_cc, ada_w[i], precision=hp) + ada_b[i], N_MOD, axis=-1)
        mod = [rows(mc[m], ml[m]) for m in range(N_MOD)]
        gate3 = jnp.concatenate([mc[2][None], ml[2]], axis=0)[:, None, :]

        h = _modulate(xs, norm_mix[i], mod[0], mod[1])
        if kind == 0:
            xs = _deltanet_layer(xs, h, dn_w_in[j], dn_conv[j], dn_a_log[j], dn_dt_bias[j],
                                 dn_out_norm[j], dn_w_out[j], gate3, n_lat // DN_BLOCK, seq // DN_BLOCK)
        elif kind == 1:
            xs = _hyena_layer(xs, h, hy_w_in[j], hy_conv[j], hy_f_w1[j], hy_f_b1[j], hy_f_freq[j],
                              hy_f_w2[j], hy_f_b2[j], hy_f_w3[j], hy_bias[j], hy_w_out[j], gate3, bsz, seq)
        else:
            xs = _shortconv_layer(xs, h, sc_w_in[j], sc_conv[j], sc_w_out[j], gate3,
                                  n_lat // HY_TOK_TILE, seq // DN_BLOCK)
        f = _modulate(xs, norm_ffn[i], mod[3], mod[4])
        out = _moe_ffn(f, w_router, router_bias, moe_w_gate[i], moe_w_up[i], moe_w_down[i])
        xs = xs + mod[5] * out
    x_out = xs[:n_lat].reshape(bsz, seq, d)
    return _rmsnorm(x_out, norm_final)
```

```python
import functools
import math

import numpy as np
import jax
import jax.numpy as jnp
from jax import lax
from jax.experimental import pallas as pl
from jax.experimental.pallas import tpu as pltpu

D_MODEL = 1024
DEPTH = 4
CTX_LEN = 256
GRID_W = 64
N_MIXERS = 3
EPS = 1e-6
N_MOD = 6

DN_HEADS = 8
DN_DK = D_MODEL // DN_HEADS
DN_DV = D_MODEL // DN_HEADS
DN_CHUNK = 64

HY_ORDER = 2
HY_BANDS = 16
HY_TARGET = 1e-2
HY_FAST = 0.3
HY_SLOW = 1.5

N_EXPERTS = 32
N_GROUPS = 8
EXPERTS_PER_GROUP = N_EXPERTS // N_GROUPS
GROUP_SCORE_K = 2
TOP_K = 2
D_EXPERT = 512
MOE_BLOCK = 512

F32 = jnp.float32
BF16 = jnp.bfloat16

ROW_TILE = 512
VMEM_LIMIT_BYTES = 48 * 1024 * 1024


def _mm_kernel(x_ref, w_ref, o_ref):
    o_ref[...] = jnp.dot(x_ref[...].astype(BF16), w_ref[...], preferred_element_type=F32)


def _mm(x, w, tn=None):
    m, k = x.shape
    n = w.shape[1]
    tm = min(ROW_TILE, m)
    tn = n if tn is None else tn
    assert m % tm == 0 and n % tn == 0
    return pl.pallas_call(
        _mm_kernel,
        out_shape=jax.ShapeDtypeStruct((m, n), F32),
        grid=(m // tm, n // tn),
        in_specs=[pl.BlockSpec((tm, k), lambda i, j: (i, 0)),
                  pl.BlockSpec((k, tn), lambda i, j: (0, j))],
        out_specs=pl.BlockSpec((tm, tn), lambda i, j: (i, j)),
        compiler_params=pltpu.CompilerParams(
            dimension_semantics=("arbitrary", "arbitrary"), vmem_limit_bytes=VMEM_LIMIT_BYTES),
        name="dense_mm",
    )(x, w.astype(BF16))


def _expert_kernel(be_ref, x_ref, wg_ref, wu_ref, wd_ref, sw_ref, o_ref):
    del be_ref
    xb = x_ref[...]
    g = jnp.dot(xb, wg_ref[0], preferred_element_type=F32)
    u = jnp.dot(xb, wu_ref[0], preferred_element_type=F32)
    hid = (g * jax.nn.sigmoid(g)) * u
    y = jnp.dot(hid.astype(BF16), wd_ref[0], preferred_element_type=F32)
    o_ref[...] = y * sw_ref[...]


def _expert_ffn(xs, block_expert, slot_w, w_gate, w_up, w_down):
    n_slots, d = xs.shape
    n_blocks = n_slots // MOE_BLOCK
    grid_spec = pltpu.PrefetchScalarGridSpec(
        num_scalar_prefetch=1,
        grid=(n_blocks,),
        in_specs=[
            pl.BlockSpec((MOE_BLOCK, d), lambda i, be: (i, 0)),
            pl.BlockSpec((1, d, D_EXPERT), lambda i, be: (be[i], 0, 0)),
            pl.BlockSpec((1, d, D_EXPERT), lambda i, be: (be[i], 0, 0)),
            pl.BlockSpec((1, D_EXPERT, d), lambda i, be: (be[i], 0, 0)),
            pl.BlockSpec((MOE_BLOCK, 1), lambda i, be: (i, 0)),
        ],
        out_specs=pl.BlockSpec((MOE_BLOCK, d), lambda i, be: (i, 0)),
    )
    return pl.pallas_call(
        _expert_kernel,
        out_shape=jax.ShapeDtypeStruct((n_slots, d), F32),
        grid_spec=grid_spec,
        compiler_params=pltpu.CompilerParams(
            dimension_semantics=("arbitrary",), vmem_limit_bytes=VMEM_LIMIT_BYTES),
        name="expert_ffn",
    )(block_expert, xs, w_gate.astype(BF16), w_up.astype(BF16), w_down.astype(BF16),
      slot_w.reshape(n_slots, 1))


DN_BLOCK = CTX_LEN
DN_HB = 2
N_CHUNKS_PER_BLOCK = DN_BLOCK // DN_CHUNK


def _group_of_block(i, n_lat_blocks, blocks_per_batch):
    return jnp.where(i >= n_lat_blocks, 0, 1 + i // blocks_per_batch)


def _dn_prep_kernel(p_ref, ab_ref, cw_ref, alog_ref, dtb_ref, q_ref, k_ref, v_ref, gate_ref, *,
                    n_lat_blocks):
    i = pl.program_id(0)
    nrow = DN_BLOCK
    seg = jnp.where(i >= n_lat_blocks, CTX_LEN, GRID_W)
    r = lax.broadcasted_iota(jnp.int32, (nrow, 1), 0)
    pos = r & (seg - 1)
    not_first = pos != 0
    not_last = pos != seg - 1
    outs = (q_ref, k_ref, v_ref)
    for part in range(3):
        for h in range(DN_HEADS):
            col = part * D_MODEL + h * DN_DK
            x = p_ref[:, col:col + DN_DK]
            cw = cw_ref[:, col:col + DN_DK]
            xp = jnp.where(not_first, pltpu.roll(x, 1, axis=0), 0.0)
            xn = jnp.where(not_last, pltpu.roll(x, nrow - 1, axis=0), 0.0)
            y = cw[0:1] * xp + cw[1:2] * x + cw[2:3] * xn
            y = y * jax.nn.sigmoid(y)
            if part < 2:
                y = y * lax.rsqrt(jnp.sum(y * y, axis=-1, keepdims=True) + EPS)
            if part == 0:
                y = y * DN_DK ** -0.5
            outs[part][:, h * DN_DK:(h + 1) * DN_DK] = y

    ab = ab_ref[...]
    nd = 2 * DN_HEADS
    a = ab[:, :nd] + dtb_ref[...]
    softplus = jnp.maximum(a, 0.0) + jnp.log(1.0 + jnp.exp(-jnp.abs(a)))
    g = -jnp.exp(alog_ref[...]) * softplus
    beta = jax.nn.sigmoid(ab[:, nd:])
    cpos = r & (DN_CHUNK - 1)
    gp, gs = g, g
    sh = 1
    while sh < DN_CHUNK:
        gp = gp + jnp.where(cpos >= sh, pltpu.roll(gp, sh, axis=0), 0.0)
        gs = gs + jnp.where(cpos < DN_CHUNK - sh, pltpu.roll(gs, nrow - sh, axis=0), 0.0)
        sh *= 2
    colid = lax.broadcasted_iota(jnp.int32, (1, nd), 1)
    gate_ref[:, :nd] = jnp.where(colid < DN_HEADS, gp, gs)
    gate_ref[:, nd:] = beta


def _dn_prep(p, p_ab, conv_w, a_log, dt_bias, n_lat_blocks):
    t = p.shape[0]
    d = D_MODEL
    nd = 2 * DN_HEADS
    nblk = t // DN_BLOCK
    row = lambda i: (i, 0)
    fixed = lambda i: (0, 0)
    return pl.pallas_call(
        functools.partial(_dn_prep_kernel, n_lat_blocks=n_lat_blocks),
        out_shape=[jax.ShapeDtypeStruct((t, d), F32)] * 3 + [jax.ShapeDtypeStruct((t, 2 * nd), F32)],
        grid=(nblk,),
        in_specs=[pl.BlockSpec((DN_BLOCK, 3 * d), row),
                  pl.BlockSpec((DN_BLOCK, 2 * nd), row),
                  pl.BlockSpec((3, 3 * d), fixed),
                  pl.BlockSpec((1, nd), fixed),
                  pl.BlockSpec((1, nd), fixed)],
        out_specs=[pl.BlockSpec((DN_BLOCK, d), row)] * 3 + [pl.BlockSpec((DN_BLOCK, 2 * nd), row)],
        compiler_params=pltpu.CompilerParams(
            dimension_semantics=("arbitrary",), vmem_limit_bytes=VMEM_LIMIT_BYTES),
        name="dn_prep",
    )(p, p_ab, conv_w, a_log.reshape(1, nd), dt_bias.reshape(1, nd))


def _dotb(a, b):
    return jnp.dot(a.astype(BF16), b.astype(BF16), preferred_element_type=F32)


def _dotb_nt(a, b):
    return lax.dot_general(a.astype(BF16), b.astype(BF16), (((1,), (1,)), ((), ())),
                           preferred_element_type=F32)


def _dotb_tn(a, b):
    return lax.dot_general(a.astype(BF16), b.astype(BF16), (((0,), (0,)), ((), ())),
                           preferred_element_type=F32)


def _unit_tri_inverses(mats, ii, jj):
    eye = (ii == jj).astype(F32)
    diag8 = (ii >> 3) == (jj >> 3)
    n = [-jnp.where(diag8, a, 0.0) for a in mats]
    n2 = [_dotb(x, x) for x in n]
    m = [eye + x for x in n]
    m = [x + _dotb(x, y) for x, y in zip(m, n2)]
    n4 = [_dotb(x, x) for x in n2]
    m = [x + _dotb(x, y) for x, y in zip(m, n4)]
    sh = 3
    while (1 << sh) < DN_CHUNK:
        off = ((ii >> (sh + 1)) == (jj >> (sh + 1))) & ((ii >> sh) != (jj >> sh))
        cm = [_dotb(jnp.where(off, a, 0.0), x) for a, x in zip(mats, m)]
        m = [x - _dotb(x, y) for x, y in zip(m, cm)]
        sh += 1
    return m


def _dn_scan_kernel(qf_ref, kf_ref, vf_ref, gcf_ref, grf_ref, qb_ref, kb_ref, vb_ref, gcb_ref, grb_ref,
                    of_ref, ob_ref, s_ref):
    @pl.when(pl.program_id(2) == 0)
    def _():
        s_ref[...] = jnp.zeros_like(s_ref)

    c = DN_CHUNK
    ncb = N_CHUNKS_PER_BLOCK
    ii = lax.broadcasted_iota(jnp.int32, (c, c), 0)
    jj = lax.broadcasted_iota(jnp.int32, (c, c), 1)
    incl = (ii >= jj, ii <= jj)
    strict = (ii > jj, ii < jj)
    dirs = ((qf_ref, kf_ref, vf_ref, gcf_ref, grf_ref, of_ref),
            (qb_ref, kb_ref, vb_ref, gcb_ref, grb_ref, ob_ref))
    items = [(d, hh, ci) for d in range(2) for hh in range(DN_HB) for ci in range(ncb)]

    def rows(ci):
        return slice(ci * c, (ci + 1) * c)

    def cols(hh):
        return slice(hh * DN_DK, (hh + 1) * DN_DK)

    q = [dirs[d][0][rows(ci), cols(hh)] for d, hh, ci in items]
    k = [dirs[d][1][rows(ci), cols(hh)] for d, hh, ci in items]
    v = [dirs[d][2][rows(ci), cols(hh)] for d, hh, ci in items]
    gc = [dirs[d][3][hh, rows(ci), d:d + 1] for d, hh, ci in items]
    gr = [dirs[d][4][hh, d:d + 1, rows(ci)] for d, hh, ci in items]
    beta = [dirs[d][3][hh, rows(ci), 2 + d:3 + d] for d, hh, ci in items]

    decay = [jnp.where(incl[it[0]], jnp.exp(jnp.where(incl[it[0]], x - y, 0.0)), 0.0)
             for it, x, y in zip(items, gc, gr)]
    kb = [x * y for x, y in zip(k, beta)]
    a = [_dotb_nt(x, y) * jnp.where(strict[it[0]], z, 0.0) for it, x, y, z in zip(items, kb, k, decay)]
    attn = [_dotb_nt(x, y) * z for x, y, z in zip(q, k, decay)]
    tinv = _unit_tri_inverses(a, ii, jj)
    eg = [jnp.exp(x) for x in gc]
    uw = [_dotb(t, jnp.concatenate([x * y, z * e], axis=-1))
          for t, x, y, z, e in zip(tinv, v, beta, kb, eg)]
    g_last = [x[0:1] if it[0] else x[c - 1:c] for it, x in zip(items, gc)]
    wq = [jnp.concatenate([x[:, DN_DV:], y * e], axis=0) for x, y, e in zip(uw, q, eg)]
    k_dec = [x * jnp.exp(y - z) for x, y, z in zip(k, g_last, gc)]
    s_dec = [jnp.exp(x) for x in g_last]

    chains = [(d, hh) for d in range(2) for hh in range(DN_HB)]
    state = [s_ref[d, hh] for d, hh in chains]
    for step in range(ncb):
        cur = [items.index((d, hh, ncb - 1 - step if d else step)) for d, hh in chains]
        ws = [_dotb(wq[n], s) for n, s in zip(cur, state)]
        v_new = [uw[n][:, :DN_DV] - x[:c] for n, x in zip(cur, ws)]
        o = [x[c:] + _dotb(attn[n], y) for n, x, y in zip(cur, ws, v_new)]
        state = [s * s_dec[n] + _dotb_tn(k_dec[n], y) for n, s, y in zip(cur, state, v_new)]
        for n, x in zip(cur, o):
            d, hh, ci = items[n]
            dirs[d][5][rows(ci), cols(hh)] = x
    for (d, hh), s in zip(chains, state):
        s_ref[d, hh] = s


def _dn_scan(q, k, v, gates, n_lat_blocks, blocks_per_batch):
    t, d = q.shape
    bsz = n_lat_blocks // blocks_per_batch
    g4 = gates.reshape(t, 4, DN_HEADS)
    gcol = jnp.transpose(g4, (2, 0, 1))
    grow = jnp.transpose(g4, (2, 1, 0))

    def blk_f(b, s):
        return jnp.where(s == 0, n_lat_blocks + b, b * blocks_per_batch + s - 1)

    def blk_b(b, s):
        return jnp.where(s == 0, n_lat_blocks + b, b * blocks_per_batch + blocks_per_batch - s)

    hw = DN_HB * DN_DK

    def specs(blk):
        return [pl.BlockSpec((DN_BLOCK, hw), lambda b, hg, s: (blk(b, s), hg))] * 3 + [
            pl.BlockSpec((DN_HB, DN_BLOCK, 4), lambda b, hg, s: (hg, blk(b, s), 0)),
            pl.BlockSpec((DN_HB, 4, DN_BLOCK), lambda b, hg, s: (hg, 0, blk(b, s)))]

    return pl.pallas_call(
        _dn_scan_kernel,
        out_shape=[jax.ShapeDtypeStruct((t, d), F32)] * 2,
        grid=(bsz, DN_HEADS // DN_HB, 1 + blocks_per_batch),
        in_specs=specs(blk_f) + specs(blk_b),
        out_specs=[pl.BlockSpec((DN_BLOCK, hw), lambda b, hg, s: (blk_f(b, s), hg)),
                   pl.BlockSpec((DN_BLOCK, hw), lambda b, hg, s: (blk_b(b, s), hg))],
        scratch_shapes=[pltpu.VMEM((2, DN_HB, DN_DK, DN_DV), F32)],
        compiler_params=pltpu.CompilerParams(
            dimension_semantics=("arbitrary", "arbitrary", "arbitrary"),
            vmem_limit_bytes=VMEM_LIMIT_BYTES),
        name="dn_scan",
    )(q, k, v, gcol, grow, q, k, v, gcol, grow)


def _dn_out_kernel(of_ref, ob_ref, z_ref, on_ref, w_ref, x_ref, gate_ref, o_ref):
    z = z_ref[...]
    zs = z * jax.nn.sigmoid(z)
    parts = []
    for h in range(DN_HEADS):
        cols = slice(h * DN_DV, (h + 1) * DN_DV)
        o = of_ref[:, cols] + ob_ref[:, cols]
        o = o * lax.rsqrt(jnp.mean(o * o, axis=-1, keepdims=True) + EPS)
        parts.append(o)
    y = jnp.concatenate(parts, axis=-1) * on_ref[...] * zs
    o_ref[...] = x_ref[...] + gate_ref[0] * jnp.dot(y.astype(BF16), w_ref[...],
                                                    preferred_element_type=F32)


def _dn_out(o_f, o_b, p, out_norm, w_out, xs, gate3, n_lat_blocks, blocks_per_batch):
    t, d = xs.shape
    row = lambda i: (i, 0)
    fixed = lambda i: (0, 0)
    grp = lambda i: (_group_of_block(i, n_lat_blocks, blocks_per_batch), 0, 0)
    return pl.pallas_call(
        _dn_out_kernel,
        out_shape=jax.ShapeDtypeStruct((t, d), F32),
        grid=(t // DN_BLOCK,),
        in_specs=[pl.BlockSpec((DN_BLOCK, d), row),
                  pl.BlockSpec((DN_BLOCK, d), row),
                  pl.BlockSpec((DN_BLOCK, d), lambda i: (i, 3)),
                  pl.BlockSpec((1, d), fixed),
                  pl.BlockSpec((d, d), fixed),
                  pl.BlockSpec((DN_BLOCK, d), row),
                  pl.BlockSpec((1, 1, d), grp)],
        out_specs=pl.BlockSpec((DN_BLOCK, d), row),
        compiler_params=pltpu.CompilerParams(
            dimension_semantics=("arbitrary",), vmem_limit_bytes=VMEM_LIMIT_BYTES),
        name="dn_out",
    )(o_f, o_b, p, jnp.tile(out_norm, DN_HEADS).reshape(1, d), w_out.astype(BF16), xs, gate3)


def _deltanet_layer(xs, h, w_in, conv_w, a_log, dt_bias, out_norm, w_out, gate3, n_lat_blocks,
                    blocks_per_batch):
    d = D_MODEL
    p = _mm(h, w_in[:, :4 * d], tn=1024)
    p_ab = jnp.dot(h, w_in[:, 4 * d:], precision=lax.Precision.HIGHEST)
    q, k, v, gates = _dn_prep(p, p_ab, conv_w, a_log, dt_bias, n_lat_blocks)
    o_f, o_b = _dn_scan(q, k, v, gates, n_lat_blocks, blocks_per_batch)
    return _dn_out(o_f, o_b, p, out_norm, w_out, xs, gate3, n_lat_blocks, blocks_per_batch)


HY_N2 = 256
HY_CB = 8
HY_TOK_TILE = 512
HY_FILT_TILE = 512


def _dft_constants(nr):
    n1, n2 = 2 * nr, HY_N2
    n = n1 * n2
    a1 = np.arange(n1)
    f1 = np.exp(-2j * np.pi * np.outer(a1, a1) / n1)
    lhs_fwd = np.concatenate([f1.real[:, :nr], f1.imag[:, :nr]], axis=0)
    lhs_inv = np.concatenate([f1.real[:nr, :], f1.imag[:nr, :]], axis=1) / n
    tw = np.exp(-2j * np.pi * np.outer(a1, np.arange(n2)) / n)
    a2 = np.arange(n2)
    f2 = np.exp(-2j * np.pi * np.outer(a2, a2) / n2)
    w_fwd = np.block([[f2.real, f2.imag], [-f2.imag, f2.real]])
    w_inv = np.block([[f2.real, -f2.imag], [f2.imag, f2.real]])
    return (jnp.asarray(lhs_fwd, BF16), jnp.asarray(lhs_inv, BF16), jnp.asarray(tw.real, F32),
            jnp.asarray(tw.imag, F32), jnp.asarray(w_fwd, BF16), jnp.asarray(w_inv, BF16))


def _hy_dft(x3, lhs_fwd, twr, twi, w_fwd):
    n1 = twr.shape[0]
    a = [jnp.dot(lhs_fwd, x3[c].astype(BF16), preferred_element_type=F32) for c in range(x3.shape[0])]
    br = jnp.concatenate([t[:n1] * twr - t[n1:] * twi for t in a], axis=0)
    bi = jnp.concatenate([t[:n1] * twi + t[n1:] * twr for t in a], axis=0)
    b = jnp.concatenate([br, bi], axis=1)
    return jnp.dot(b.astype(BF16), w_fwd, preferred_element_type=F32)


def _hy_idft(p, cb, lhs_inv, twr, twi, w_inv):
    n1, n2 = twr.shape
    c = jnp.dot(p.astype(BF16), w_inv, preferred_element_type=F32)
    out = []
    for ch in range(cb):
        cr = c[ch * n1:(ch + 1) * n1, :n2]
        ci = c[ch * n1:(ch + 1) * n1, n2:]
        d = jnp.concatenate([cr * twr + ci * twi, ci * twr - cr * twi], axis=0)
        out.append(jnp.dot(lhs_inv, d.astype(BF16), preferred_element_type=F32))
    return out


def _hy_spectrum_kernel(hf_ref, hb_ref, lf_ref, twr_ref, twi_ref, wf_ref, o_ref):
    cb, nr, n2 = hf_ref.shape
    first = ((lax.broadcasted_iota(jnp.int32, (nr, n2), 0) == 0)
             & (lax.broadcasted_iota(jnp.int32, (nr, n2), 1) == 0))
    hb = jnp.where(first, 0.0, hb_ref[...])
    consts = (lf_ref[...], twr_ref[...], twi_ref[...], wf_ref[...])
    xf = _hy_dft(hf_ref[...], *consts)
    xb = _hy_dft(hb, *consts)
    n1 = 2 * nr
    o_ref[...] = jnp.concatenate([xf[:, :n2] + xb[:, :n2], xf[:, n2:] - xb[:, n2:]],
                                 axis=1).reshape(cb, n1, 2 * n2)


def _hy_spectrum(filt, consts):
    d = D_MODEL
    l = filt.shape[1]
    nr = l // HY_N2
    n1 = 2 * nr
    lhs_fwd, _, twr, twi, w_fwd, _ = consts
    cpo = d // HY_CB
    fixed2 = lambda o, c: (0, 0)
    return pl.pallas_call(
        _hy_spectrum_kernel,
        out_shape=jax.ShapeDtypeStruct((HY_ORDER * d, n1, 2 * HY_N2), F32),
        grid=(HY_ORDER, cpo),
        in_specs=[pl.BlockSpec((HY_CB, nr, HY_N2), lambda o, c: (2 * o * cpo + c, 0, 0)),
                  pl.BlockSpec((HY_CB, nr, HY_N2), lambda o, c: ((2 * o + 1) * cpo + c, 0, 0)),
                  pl.BlockSpec(lhs_fwd.shape, fixed2),
                  pl.BlockSpec(twr.shape, fixed2),
                  pl.BlockSpec(twi.shape, fixed2),
                  pl.BlockSpec(w_fwd.shape, fixed2)],
        out_specs=pl.BlockSpec((HY_CB, n1, 2 * HY_N2), lambda o, c: (o * cpo + c, 0, 0)),
        compiler_params=pltpu.CompilerParams(
            dimension_semantics=("arbitrary", "arbitrary"), vmem_limit_bytes=VMEM_LIMIT_BYTES),
        name="hy_spectrum",
    )(filt.reshape(-1, nr, HY_N2), filt.reshape(-1, nr, HY_N2), lhs_fwd, twr, twi, w_fwd)


def _hy_conv_kernel(z_ref, g_ref, k_ref, bias_ref, lf_ref, li_ref, twr_ref, twi_ref, wf_ref, wi_ref,
                    o_ref):
    cb, nr, n2 = z_ref.shape
    twr, twi = twr_ref[...], twi_ref[...]
    z = z_ref[...]
    x = _hy_dft(z, lf_ref[...], twr, twi, wf_ref[...])
    kk = k_ref[...].reshape(x.shape)
    xr, xi, kr, ki = x[:, :n2], x[:, n2:], kk[:, :n2], kk[:, n2:]
    p = jnp.concatenate([xr * kr - xi * ki, xr * ki + xi * kr], axis=1)
    conv = _hy_idft(p, cb, li_ref[...], twr, twi, wi_ref[...])
    for c in range(cb):
        o_ref[c] = g_ref[c] * (conv[c] + bias_ref[c] * z[c])


def _hy_conv(z, z_part, gate, gate_part, khat, order, bias, consts, bsz):
    d = D_MODEL
    l = z.shape[1] // bsz
    nr = l // HY_N2
    n1 = 2 * nr
    cpo = d // HY_CB
    lhs_fwd, lhs_inv, twr, twi, w_fwd, w_inv = consts
    fixed2 = lambda c, b: (0, 0)
    out = pl.pallas_call(
        _hy_conv_kernel,
        out_shape=jax.ShapeDtypeStruct((d, bsz * nr, HY_N2), F32),
        grid=(cpo, bsz),
        in_specs=[pl.BlockSpec((HY_CB, nr, HY_N2), lambda c, b: (z_part * cpo + c, b, 0)),
                  pl.BlockSpec((HY_CB, nr, HY_N2), lambda c, b: (gate_part * cpo + c, b, 0)),
                  pl.BlockSpec((HY_CB, n1, 2 * HY_N2), lambda c, b: (order * cpo + c, 0, 0)),
                  pl.BlockSpec((HY_CB, 1, 1), lambda c, b: (c, 0, 0)),
                  pl.BlockSpec(lhs_fwd.shape, fixed2),
                  pl.BlockSpec(lhs_inv.shape, fixed2),
                  pl.BlockSpec(twr.shape, fixed2),
                  pl.BlockSpec(twi.shape, fixed2),
                  pl.BlockSpec(w_fwd.shape, fixed2),
                  pl.BlockSpec(w_inv.shape, fixed2)],
        out_specs=pl.BlockSpec((HY_CB, nr, HY_N2), lambda c, b: (c, b, 0)),
        compiler_params=pltpu.CompilerParams(
            dimension_semantics=("arbitrary", "arbitrary"), vmem_limit_bytes=VMEM_LIMIT_BYTES),
        name="hy_conv",
    )(z.reshape(-1, bsz * nr, HY_N2), gate.reshape(-1, bsz * nr, HY_N2), khat,
      bias.reshape(d, 1, 1), lhs_fwd, lhs_inv, twr, twi, w_fwd, w_inv)
    return out.reshape(d, bsz * l)


def _hy_ctx_kernel(z_ref, g_ref, hf_ref, hb_ref, bias_ref, wf_ref, wi_ref, o_ref, *, bsz):
    l = hf_ref.shape[1]
    wf, wi = wf_ref[...], wi_ref[...]
    hb = jnp.where(lax.broadcasted_iota(jnp.int32, (1, l), 1) == 0, 0.0, hb_ref[...])
    kf = jnp.dot(hf_ref[...].astype(BF16), wf, preferred_element_type=F32)
    kb = jnp.dot(hb.astype(BF16), wf, preferred_element_type=F32)
    n = 2 * l
    kr, ki = kf[:, :n] + kb[:, :n], kf[:, n:] - kb[:, n:]
    bias = bias_ref[...]
    for b in range(bsz):
        z = z_ref[:, b * l:(b + 1) * l]
        x = jnp.dot(z.astype(BF16), wf, preferred_element_type=F32)
        xr, xi = x[:, :n], x[:, n:]
        p = jnp.concatenate([xr * kr - xi * ki, xr * ki + xi * kr], axis=1)
        conv = jnp.dot(p.astype(BF16), wi, preferred_element_type=F32)
        o_ref[:, b * l:(b + 1) * l] = g_ref[:, b * l:(b + 1) * l] * (conv + bias * z)


def _hy_ctx(z, z_part, gate, gate_part, filt, order, bias, bsz):
    d = D_MODEL
    l = filt.shape[1]
    n = 2 * l
    ang = 2 * np.pi * np.outer(np.arange(l), np.arange(n)) / n
    w_fwd = jnp.asarray(np.concatenate([np.cos(ang), -np.sin(ang)], axis=1), BF16)
    w_inv = jnp.asarray(np.concatenate([np.cos(ang.T), -np.sin(ang.T)], axis=0) / n, BF16)
    cb = 256
    nblk = d // cb
    fixed = lambda c: (0, 0)
    return pl.pallas_call(
        functools.partial(_hy_ctx_kernel, bsz=bsz),
        out_shape=jax.ShapeDtypeStruct((d, bsz * l), F32),
        grid=(nblk,),
        in_specs=[pl.BlockSpec((cb, bsz * l), lambda c: (z_part * nblk + c, 0)),
                  pl.BlockSpec((cb, bsz * l), lambda c: (gate_part * nblk + c, 0)),
                  pl.BlockSpec((cb, l), lambda c: (2 * order * nblk + c, 0)),
                  pl.BlockSpec((cb, l), lambda c: ((2 * order + 1) * nblk + c, 0)),
                  pl.BlockSpec((cb, 1), lambda c: (c, 0)),
                  pl.BlockSpec(w_fwd.shape, fixed),
                  pl.BlockSpec(w_inv.shape, fixed)],
        out_specs=pl.BlockSpec((cb, bsz * l), lambda c: (c, 0)),
        compiler_params=pltpu.CompilerParams(
            dimension_semantics=("arbitrary",), vmem_limit_bytes=VMEM_LIMIT_BYTES),
        name="hy_ctx_conv",
    )(z, gate, filt, filt, bias.reshape(d, 1), w_fwd, w_inv)


def _hy_filter_kernel(band_ref, w1t_ref, w1c_ref, w1s_ref, b1_ref, fr_ref, w2_ref, b2_ref, w3_ref,
                      delta_ref, o_ref, *, l):
    tl = o_ref.shape[1]
    d = D_MODEL
    hp = lax.Precision.HIGHEST
    pos = (lax.broadcasted_iota(jnp.int32, (1, tl), 1) + pl.program_id(0) * tl).astype(F32)
    t = pos / max(l - 1, 1)
    ang = ((2 * math.pi / l) * pos) * band_ref[...]
    fr = fr_ref[...]
    pre = (w1t_ref[...] * t + jnp.dot(w1c_ref[...], jnp.cos(ang), precision=hp)
           + jnp.dot(w1s_ref[...], -jnp.sin(ang), precision=hp) + b1_ref[...])
    hdn = jnp.sin(fr * pre)
    hdn = jnp.sin(fr * (jnp.dot(w2_ref[...], hdn, precision=hp) + b2_ref[...]))
    window = jnp.exp(-t * delta_ref[...])
    for part in range(2 * HY_ORDER):
        rows = slice(part * d, (part + 1) * d)
        o_ref[rows, :] = jnp.dot(w3_ref[rows, :], hdn, precision=hp) * window


def _hy_filter(l, w1, b1, freq, w2, b2, w3):
    d = D_MODEL
    nb = HY_BANDS
    tl = min(HY_FILT_TILE, l)
    col = lambda v: v.reshape(-1, 1)
    bands = jnp.linspace(1e-4, nb - 1, nb, dtype=F32)
    deltas = jnp.abs(jnp.linspace(math.log(HY_TARGET) / HY_SLOW, math.log(HY_TARGET) / HY_FAST, d, dtype=F32))
    w1t = w1.T
    args = (col(bands), w1t[:, 0:1], w1t[:, 1:1 + nb], w1t[:, 1 + nb:], col(b1), col(freq), w2.T, col(b2),
            w3.T, col(deltas))
    return pl.pallas_call(
        functools.partial(_hy_filter_kernel, l=l),
        out_shape=jax.ShapeDtypeStruct((2 * HY_ORDER * d, l), F32),
        grid=(l // tl,),
        in_specs=[pl.BlockSpec(a.shape, lambda j: (0, 0)) for a in args],
        out_specs=pl.BlockSpec((2 * HY_ORDER * d, tl), lambda j: (0, j)),
        compiler_params=pltpu.CompilerParams(
            dimension_semantics=("arbitrary",), vmem_limit_bytes=VMEM_LIMIT_BYTES),
        name="hy_filter",
    )(*args)


def _hy_inproj_kernel(h_ref, wt_ref, cw_ref, o_ref, *, seg):
    nch = wt_ref.shape[0]
    tm = h_ref.shape[0]
    hb = h_ref[...].astype(BF16)
    pos = lax.broadcasted_iota(jnp.int32, (1, tm), 1) & (seg - 1)
    not_first = pos != 0
    not_last = pos != seg - 1
    sub = 512
    for j in range(nch // sub):
        rows = slice(j * sub, (j + 1) * sub)
        p = _dotb_nt(wt_ref[rows, :], hb)
        cw = cw_ref[rows, :]
        prev = jnp.where(not_first, pltpu.roll(p, 1, axis=1), 0.0)
        nxt = jnp.where(not_last, pltpu.roll(p, tm - 1, axis=1), 0.0)
        o_ref[rows, :] = cw[:, 0:1] * prev + cw[:, 1:2] * p + cw[:, 2:3] * nxt


def _hy_inproj(h, w_in, conv_w, first_tile, n_tiles, seg):
    k = h.shape[1]
    nch = w_in.shape[1]
    tm = HY_TOK_TILE
    return pl.pallas_call(
        functools.partial(_hy_inproj_kernel, seg=seg),
        out_shape=jax.ShapeDtypeStruct((nch, n_tiles * tm), F32),
        grid=(n_tiles,),
        in_specs=[pl.BlockSpec((tm, k), lambda i: (first_tile + i, 0)),
                  pl.BlockSpec((nch, k), lambda i: (0, 0)),
                  pl.BlockSpec((nch, 3), lambda i: (0, 0))],
        out_specs=pl.BlockSpec((nch, tm), lambda i: (0, i)),
        compiler_params=pltpu.CompilerParams(
            dimension_semantics=("arbitrary",), vmem_limit_bytes=VMEM_LIMIT_BYTES),
        name="hy_inproj",
    )(h, w_in.T.astype(BF16), conv_w.T)


def _hy_out_kernel(zl_ref, zc_ref, w_ref, x_ref, gate_ref, o_ref, *, n_lat_tiles):
    z = jnp.where(pl.program_id(0) >= n_lat_tiles, zc_ref[...], zl_ref[...])
    o_ref[...] = x_ref[...] + gate_ref[0] * _dotb_tn(z, w_ref[...])


def _hy_out(z_lat, z_ctx, w_out, xs, gate3, blocks_per_batch):
    t, d = xs.shape
    tm = HY_TOK_TILE
    n_lat_tiles = z_lat.shape[1] // tm
    per_tile = tm // DN_BLOCK
    grp = lambda i: (_group_of_block(i * per_tile, n_lat_tiles * per_tile, blocks_per_batch), 0, 0)
    return pl.pallas_call(
        functools.partial(_hy_out_kernel, n_lat_tiles=n_lat_tiles),
        out_shape=jax.ShapeDtypeStruct((t, d), F32),
        grid=(t // tm,),
        in_specs=[pl.BlockSpec((d, tm), lambda i: (0, jnp.minimum(i, n_lat_tiles - 1))),
                  pl.BlockSpec((d, tm), lambda i: (0, 0)),
                  pl.BlockSpec((d, d), lambda i: (0, 0)),
                  pl.BlockSpec((tm, d), lambda i: (i, 0)),
                  pl.BlockSpec((1, 1, d), grp)],
        out_specs=pl.BlockSpec((tm, d), lambda i: (i, 0)),
        compiler_params=pltpu.CompilerParams(
            dimension_semantics=("arbitrary",), vmem_limit_bytes=VMEM_LIMIT_BYTES),
        name="hy_out",
    )(z_lat, z_ctx, w_out.astype(BF16), xs, gate3)


def _hyena_layer(xs, h, w_in, conv_w, f_w1, f_b1, f_freq, f_w2, f_b2, f_w3, bias, w_out, gate3, bsz, seq):
    n_lat_tiles = bsz * seq // HY_TOK_TILE
    assert bsz * CTX_LEN == HY_TOK_TILE
    p_lat = _hy_inproj(h, w_in, conv_w, 0, n_lat_tiles, GRID_W)
    p_ctx = _hy_inproj(h, w_in, conv_w, n_lat_tiles, 1, CTX_LEN)
    fargs = (f_w1, f_b1, f_freq, f_w2, f_b2, f_w3)
    consts = _dft_constants(seq // HY_N2)
    khat = _hy_spectrum(_hy_filter(seq, *fargs), consts)
    filt_ctx = _hy_filter(CTX_LEN, *fargs)
    z_lat, z_ctx = p_lat, p_ctx
    for n in range(HY_ORDER):
        z_lat = _hy_conv(z_lat, 0, p_lat, n + 1, khat, n, bias[n], consts, bsz)
        z_ctx = _hy_ctx(z_ctx, 0, p_ctx, n + 1, filt_ctx, n, bias[n], bsz)
    return _hy_out(z_lat, z_ctx, w_out, xs, gate3, seq // DN_BLOCK)


MOE_TILE = 512
PACK = 2


def _route_kernel(x_ref, gain_ref, shift_ref, scale_ref, wr_ref, rb_ref, tri_ref,
                  f_ref, e_ref, w_ref, r_ref, cnt_ref, carry_ref):
    tm, d = x_ref.shape
    ne, epg, ng = N_EXPERTS, EXPERTS_PER_GROUP, N_GROUPS

    @pl.when(pl.program_id(0) == 0)
    def _():
        carry_ref[...] = jnp.zeros_like(carry_ref)

    x = x_ref[...]
    y = x * lax.rsqrt(jnp.mean(x * x, axis=-1, keepdims=True) + EPS) * gain_ref[...]
    f = y * (1 + scale_ref[0]) + shift_ref[0]
    bits = pltpu.bitcast(f.astype(BF16).astype(F32), jnp.uint32)
    half = d // PACK
    f_ref[...] = (bits[:, :half] >> 16) | (bits[:, half:] & jnp.uint32(0xFFFF0000))

    logits = lax.dot_general(wr_ref[...], f, (((1,), (1,)), ((), ())),
                             precision=lax.Precision.HIGHEST, preferred_element_type=F32)
    scores = jax.nn.sigmoid(logits)
    biased = scores + rb_ref[...]
    s = [scores[j * ng:(j + 1) * ng] for j in range(epg)]
    c = [biased[j * ng:(j + 1) * ng] for j in range(epg)]
    hi01, lo01 = jnp.maximum(c[0], c[1]), jnp.minimum(c[0], c[1])
    hi23, lo23 = jnp.maximum(c[2], c[3]), jnp.minimum(c[2], c[3])
    gscore = jnp.maximum(hi01, hi23) + jnp.maximum(jnp.minimum(hi01, hi23), jnp.maximum(lo01, lo23))
    gi = lax.broadcasted_iota(jnp.int32, (ng, tm), 0)
    gmax = jnp.max(gscore, axis=0, keepdims=True)
    grp = jnp.min(jnp.where(gscore == gmax, gi, ng), axis=0, keepdims=True)
    sel = gi == grp
    cv = [jnp.sum(jnp.where(sel, t, 0.0), axis=0, keepdims=True) for t in c]
    sv = [jnp.sum(jnp.where(sel, t, 0.0), axis=0, keepdims=True) for t in s]

    def pick(excluded):
        best = jnp.full((1, tm), -jnp.inf, F32)
        idx = jnp.zeros((1, tm), jnp.int32)
        val = jnp.zeros((1, tm), F32)
        for j in range(epg):
            cand = cv[j] if excluded is None else jnp.where(excluded == j, -jnp.inf, cv[j])
            take = cand > best
            best = jnp.where(take, cand, best)
            idx = jnp.where(take, j, idx)
            val = jnp.where(take, sv[j], val)
        return idx, val

    i1, v1 = pick(None)
    i2, v2 = pick(i1)
    e1 = grp * epg + i1
    e2 = grp * epg + i2
    wsum = v1 + v2
    e_ref[0:1, :] = e1
    e_ref[1:2, :] = e2
    w_ref[0:1, :] = v1 / wsum
    w_ref[1:2, :] = v2 / wsum

    ei = lax.broadcasted_iota(jnp.int32, (ne, tm), 0)
    oh1 = ei == e1
    oh2 = ei == e2
    tri = tri_ref[...]
    pre1 = jnp.dot(oh1.astype(BF16), tri, preferred_element_type=F32)
    pre2 = jnp.dot(oh2.astype(BF16), tri, preferred_element_type=F32)
    tot1 = pre1[:, tm - 1:tm]
    tot2 = pre2[:, tm - 1:tm]
    carry = carry_ref[:, 0:1]
    r1 = jnp.sum(jnp.where(oh1, carry + pre1 - 1.0, 0.0), axis=0, keepdims=True)
    r2 = jnp.sum(jnp.where(oh2, carry + tot1 + pre2 - 1.0, 0.0), axis=0, keepdims=True)
    r_ref[0:1, :] = r1.astype(jnp.int32)
    r_ref[1:2, :] = r2.astype(jnp.int32)
    carry_ref[...] = carry_ref[...] + (tot1 + tot2)
    cnt_ref[...] = carry_ref[...]


def _moe_route(xs, gain, shift3, scale3, w_router, router_bias, n_lat_tiles, blocks_per_batch):
    t, d = xs.shape
    tm = MOE_TILE
    per_tile = tm // DN_BLOCK
    row = lambda i: (i, 0)
    col = lambda i: (0, i)
    fixed = lambda i: (0, 0)
    grp = lambda i: (_group_of_block(i * per_tile, n_lat_tiles * per_tile, blocks_per_batch), 0, 0)
    tri = jnp.asarray(np.triu(np.ones((tm, tm), np.float32)), BF16)
    perm = np.arange(N_EXPERTS).reshape(N_GROUPS, EXPERTS_PER_GROUP).T.reshape(-1)
    return pl.pallas_call(
        _route_kernel,
        out_shape=[jax.ShapeDtypeStruct((t, d // PACK), jnp.uint32),
                   jax.ShapeDtypeStruct((TOP_K, t), jnp.int32),
                   jax.ShapeDtypeStruct((TOP_K, t), F32),
                   jax.ShapeDtypeStruct((TOP_K, t), jnp.int32),
                   jax.ShapeDtypeStruct((N_EXPERTS, 128), F32)],
        grid=(t // tm,),
        in_specs=[pl.BlockSpec((tm, d), row),
                  pl.BlockSpec((1, d), fixed),
                  pl.BlockSpec((1, 1, d), grp),
                  pl.BlockSpec((1, 1, d), grp),
                  pl.BlockSpec((N_EXPERTS, d), fixed),
                  pl.BlockSpec((N_EXPERTS, 1), fixed),
                  pl.BlockSpec((tm, tm), fixed)],
        out_specs=[pl.BlockSpec((tm, d // PACK), row),
                   pl.BlockSpec((TOP_K, tm), col),
                   pl.BlockSpec((TOP_K, tm), col),
                   pl.BlockSpec((TOP_K, tm), col),
                   pl.BlockSpec((N_EXPERTS, 128), fixed)],
        scratch_shapes=[pltpu.VMEM((N_EXPERTS, 128), F32)],
        compiler_params=pltpu.CompilerParams(
            dimension_semantics=("arbitrary",), vmem_limit_bytes=VMEM_LIMIT_BYTES),
        name="moe_route",
    )(xs, gain.reshape(1, d), shift3, scale3, w_router.T[perm], router_bias[perm].reshape(N_EXPERTS, 1), tri)


def _row_copy(src, src_row, dst, dst_row, sem):
    return pltpu.make_async_copy(src.at[pl.ds(src_row, 1)], dst.at[pl.ds(dst_row, 1)], sem)


def _dispatch_kernel(dest_ref, f_ref, xs_in_ref, xs_ref, dest_smem, sem, idx_sem):
    del xs_in_ref
    tm = f_ref.shape[0]
    idx_copy = pltpu.make_async_copy(dest_ref, dest_smem, idx_sem)
    idx_copy.start()
    idx_copy.wait()

    def issue(tok, carry):
        for k in range(TOP_K):
            _row_copy(f_ref, tok, xs_ref, dest_smem[k, tok], sem).start()
        return carry

    def drain(tok, carry):
        for k in range(TOP_K):
            _row_copy(f_ref, 0, xs_ref, 0, sem).wait()
        return carry

    lax.fori_loop(0, tm, issue, 0, unroll=8)
    lax.fori_loop(0, tm, drain, 0, unroll=8)


def _dispatch(f_packed, dest, n_slots):
    t, wd = f_packed.shape
    tm = MOE_TILE
    return pl.pallas_call(
        _dispatch_kernel,
        out_shape=jax.ShapeDtypeStruct((n_slots, wd), jnp.uint32),
        grid=(t // tm,),
        in_specs=[pl.BlockSpec((TOP_K, tm), lambda i: (0, i)),
                  pl.BlockSpec((tm, wd), lambda i: (i, 0)),
                  pl.BlockSpec(memory_space=pl.ANY)],
        out_specs=pl.BlockSpec(memory_space=pl.ANY),
        scratch_shapes=[pltpu.SMEM((TOP_K, tm), jnp.int32),
                        pltpu.SemaphoreType.DMA, pltpu.SemaphoreType.DMA],
        input_output_aliases={2: 0},
        compiler_params=pltpu.CompilerParams(
            dimension_semantics=("arbitrary",), vmem_limit_bytes=VMEM_LIMIT_BYTES),
        name="moe_dispatch",
    )(dest, f_packed, jnp.zeros((n_slots, wd), jnp.uint32))


def _experts_kernel(be_ref, na_ref, x_ref, wg_ref, wu_ref, wd_ref, o_ref, wgb_ref, wub_ref, wdb_ref):
    i = pl.program_id(0)
    prev = be_ref[jnp.maximum(i - 1, 0)]

    @pl.when((i == 0) | (be_ref[i] != prev))
    def _():
        wgb_ref[...] = wg_ref[0].astype(BF16)
        wub_ref[...] = wu_ref[0].astype(BF16)
        wdb_ref[...] = wd_ref[0].astype(BF16)

    @pl.when(i < na_ref[0])
    def _():
        packed = x_ref[...]
        lo = pltpu.bitcast(packed << 16, F32)
        hi = pltpu.bitcast(packed & jnp.uint32(0xFFFF0000), F32)
        xb = jnp.concatenate([lo, hi], axis=-1).astype(BF16)
        g = jnp.dot(xb, wgb_ref[...], preferred_element_type=F32)
        u = jnp.dot(xb, wub_ref[...], preferred_element_type=F32)
        hid = (g * jax.nn.sigmoid(g)) * u
        o_ref[...] = jnp.dot(hid.astype(BF16), wdb_ref[...], preferred_element_type=F32)

    @pl.when(i >= na_ref[0])
    def _():
        o_ref[...] = jnp.zeros_like(o_ref)


def _experts(xs_sorted, block_expert, n_active, w_gate, w_up, w_down):
    n_slots, wd = xs_sorted.shape
    d = wd * PACK
    n_blocks = n_slots // MOE_BLOCK
    blk = lambda i, be, na: (jnp.minimum(i, na[0] - 1), 0)
    wsel = lambda i, be, na: (be[jnp.minimum(i, na[0] - 1)], 0, 0)
    grid_spec = pltpu.PrefetchScalarGridSpec(
        num_scalar_prefetch=2,
        grid=(n_blocks,),
        in_specs=[pl.BlockSpec((MOE_BLOCK, wd), blk),
                  pl.BlockSpec((1, d, D_EXPERT), wsel),
                  pl.BlockSpec((1, d, D_EXPERT), wsel),
                  pl.BlockSpec((1, D_EXPERT, d), wsel)],
        out_specs=pl.BlockSpec((MOE_BLOCK, d), lambda i, be, na: (i, 0)),
        scratch_shapes=[pltpu.VMEM((d, D_EXPERT), BF16), pltpu.VMEM((d, D_EXPERT), BF16),
                        pltpu.VMEM((D_EXPERT, d), BF16)],
    )
    return pl.pallas_call(
        _experts_kernel,
        out_shape=jax.ShapeDtypeStruct((n_slots, d), F32),
        grid_spec=grid_spec,
        compiler_params=pltpu.CompilerParams(
            dimension_semantics=("arbitrary",), vmem_limit_bytes=VMEM_LIMIT_BYTES),
        name="moe_experts",
    )(block_expert, n_active, xs_sorted, w_gate, w_up, w_down)


def _combine_kernel(dest_ref, y_ref, x_ref, w_ref, gate_ref, o_ref, dest_smem, ya_ref, yb_ref, sem, idx_sem):
    tm = x_ref.shape[0]
    idx_copy = pltpu.make_async_copy(dest_ref, dest_smem, idx_sem)
    idx_copy.start()
    idx_copy.wait()
    bufs = (ya_ref, yb_ref)

    def issue(tok, carry):
        for k in range(TOP_K):
            _row_copy(y_ref, dest_smem[k, tok], bufs[k], tok, sem).start()
        return carry

    def drain(tok, carry):
        for k in range(TOP_K):
            _row_copy(y_ref, 0, bufs[k], 0, sem).wait()
        return carry

    lax.fori_loop(0, tm, issue, 0, unroll=8)
    lax.fori_loop(0, tm, drain, 0, unroll=8)
    w = w_ref[...]
    o_ref[...] = x_ref[...] + gate_ref[0] * (w[:, 0:1] * ya_ref[...] + w[:, 1:2] * yb_ref[...])


def _combine(ys, dest, weight_cols, xs, gate3, n_lat_tiles, blocks_per_batch):
    t, d = xs.shape
    tm = MOE_TILE
    per_tile = tm // DN_BLOCK
    row = lambda i: (i, 0)
    grp = lambda i: (_group_of_block(i * per_tile, n_lat_tiles * per_tile, blocks_per_batch), 0, 0)
    return pl.pallas_call(
        _combine_kernel,
        out_shape=jax.ShapeDtypeStruct((t, d), F32),
        grid=(t // tm,),
        in_specs=[pl.BlockSpec((TOP_K, tm), lambda i: (0, i)),
                  pl.BlockSpec(memory_space=pl.ANY),
                  pl.BlockSpec((tm, d), row),
                  pl.BlockSpec((tm, TOP_K), row),
                  pl.BlockSpec((1, 1, d), grp)],
        out_specs=pl.BlockSpec((tm, d), row),
        scratch_shapes=[pltpu.SMEM((TOP_K, tm), jnp.int32),
                        pltpu.VMEM((tm, d), F32), pltpu.VMEM((tm, d), F32),
                        pltpu.SemaphoreType.DMA, pltpu.SemaphoreType.DMA],
        compiler_params=pltpu.CompilerParams(
            dimension_semantics=("arbitrary",), vmem_limit_bytes=VMEM_LIMIT_BYTES),
        name="moe_combine",
    )(dest, ys, xs, weight_cols, gate3)


def _moe_layer(xs, gain, shift3, scale3, gate3, w_router, router_bias, w_gate, w_up, w_down,
               n_lat_tiles, blocks_per_batch):
    t = xs.shape[0]
    f_packed, expert, weight, rank, counts = _moe_route(xs, gain, shift3, scale3, w_router, router_bias,
                                                        n_lat_tiles, blocks_per_batch)
    counts = counts[:, 0].astype(jnp.int32)
    padded = (counts + MOE_BLOCK - 1) // MOE_BLOCK * MOE_BLOCK
    pend = jnp.cumsum(padded)
    pstart = pend - padded
    n_blocks = -(-(t * TOP_K) // MOE_BLOCK) + N_EXPERTS
    block_start = jnp.arange(n_blocks, dtype=jnp.int32) * MOE_BLOCK
    block_expert = jnp.minimum(jnp.searchsorted(pend, block_start, side='right'),
                               N_EXPERTS - 1).astype(jnp.int32)
    n_active = (pend[-1:] // MOE_BLOCK).astype(jnp.int32)
    dest = pstart[expert] + rank
    xs_sorted = _dispatch(f_packed, dest, n_blocks * MOE_BLOCK)
    ys = _experts(xs_sorted, block_expert, n_active, w_gate, w_up, w_down)
    return _combine(ys, dest, weight.T, xs, gate3, n_lat_tiles, blocks_per_batch)


def _rmsnorm(x, gain):
    y = x * lax.rsqrt(jnp.mean(x * x, axis=-1, keepdims=True) + EPS)
    return y * gain


def _modulate(x, gain, shift, scale):
    return _rmsnorm(x, gain) * (1 + scale) + shift


def _l2norm(t):
    return t * lax.rsqrt(jnp.sum(t * t, axis=-1, keepdims=True) + EPS)


def _short_conv(x, w, on_grid):
    b, l, ch = x.shape
    xs = x.reshape(b, l // GRID_W, GRID_W, ch) if on_grid else x.reshape(b, 1, l, ch)
    n = xs.shape[2]
    xp = jnp.pad(xs, ((0, 0), (0, 0), (1, 1), (0, 0)))
    y = w[0] * xp[:, :, 0:n] + w[1] * xp[:, :, 1:n + 1] + w[2] * xp[:, :, 2:n + 2]
    return y.reshape(b, l, ch)


def _gated_delta_chunked(q, k, v, g, beta, s0):
    b, h, l, dk = q.shape
    dv = v.shape[-1]
    c = DN_CHUNK
    n = l // c
    q = q.reshape(b, h, n, c, dk)
    k = k.reshape(b, h, n, c, dk)
    v = v.reshape(b, h, n, c, dv)
    g = jnp.cumsum(g.reshape(b, h, n, c), axis=-1)
    beta = beta.reshape(b, h, n, c, 1)
    pos = jnp.arange(c)
    incl = pos[:, None] >= pos[None, :]
    strict = pos[:, None] > pos[None, :]
    decay = jnp.exp(jnp.where(incl, g[..., :, None] - g[..., None, :], -jnp.inf))
    kb = k * beta
    a_mat = jnp.einsum('bhnid,bhnjd->bhnij', kb, k) * jnp.where(strict, decay, 0.0)
    rhs = jnp.concatenate([v * beta, kb * jnp.exp(g)[..., None]], axis=-1)
    sol = lax.linalg.triangular_solve(a_mat + jnp.eye(c, dtype=a_mat.dtype), rhs,
                                      left_side=True, lower=True, unit_diagonal=True)
    u, w = sol[..., :dv], sol[..., dv:]
    attn = jnp.einsum('bhnid,bhnjd->bhnij', q, k) * decay
    g_last = g[..., -1:]
    q_dec = q * jnp.exp(g)[..., None]
    k_dec = k * jnp.exp(g_last - g)[..., None]

    def step(s, inp):
        qd, kd, uu, ww, at, gl = inp
        v_new = uu - jnp.einsum('bhck,bhkv->bhcv', ww, s)
        o = jnp.einsum('bhck,bhkv->bhcv', qd, s) + jnp.einsum('bhcs,bhsv->bhcv', at, v_new)
        s = s * jnp.exp(gl)[..., None] + jnp.einsum('bhck,bhcv->bhkv', kd, v_new)
        return s, o

    xs = tuple(jnp.moveaxis(t, 2, 0) for t in (q_dec, k_dec, u, w, attn, g_last))
    s_final, o = lax.scan(step, s0, xs)
    o = jnp.moveaxis(o, 0, 2).reshape(b, h, l, dv)
    return o, s_final


def _deltanet_mixer(p_ctx, p_lat, conv_w, a_log, dt_bias, out_norm):
    d = D_MODEL
    nh = DN_HEADS

    def project(p, on_grid):
        b, l, _ = p.shape
        qkv = jax.nn.silu(_short_conv(p[..., :3 * d], conv_w, on_grid))
        z = p[..., 3 * d:4 * d]
        a = p[..., 4 * d:4 * d + 2 * nh].reshape(b, l, 2, nh)
        bb = p[..., 4 * d + 2 * nh:].reshape(b, l, 2, nh)

        def heads(t):
            return jnp.transpose(t.reshape(b, l, nh, -1), (0, 2, 1, 3))

        q, k, v = (heads(t) for t in jnp.split(qkv, 3, axis=-1))
        q = _l2norm(q) * DN_DK ** -0.5
        k = _l2norm(k)
        g = -jnp.exp(a_log) * jax.nn.softplus(a + dt_bias)
        g = jnp.transpose(g, (2, 0, 3, 1))
        beta = jnp.transpose(jax.nn.sigmoid(bb), (2, 0, 3, 1))
        return q, k, v, g, beta, z

    def scan_both(q, k, v, g, beta, s_f, s_b):
        o_f, s_f = _gated_delta_chunked(q, k, v, g[0], beta[0], s_f)
        rev = lambda t: jnp.flip(t, axis=2)
        o_b, s_b = _gated_delta_chunked(rev(q), rev(k), rev(v), rev(g[1]), rev(beta[1]), s_b)
        return o_f + rev(o_b), s_f, s_b

    def finish(o, z):
        b, _, l, _ = o.shape
        o = jnp.transpose(o, (0, 2, 1, 3))
        o = o * lax.rsqrt(jnp.mean(o * o, axis=-1, keepdims=True) + EPS) * out_norm
        o = o * jax.nn.silu(z.reshape(b, l, nh, DN_DV))
        return o.reshape(b, l, d)

    qc, kc, vc, gc, bc, zc = project(p_ctx, False)
    s0 = jnp.zeros((p_ctx.shape[0], nh, DN_DK, DN_DV), F32)
    o_c, s_f, s_b = scan_both(qc, kc, vc, gc, bc, s0, s0)
    ql, kl, vl, gla, bl, zl = project(p_lat, True)
    o_l, _, _ = scan_both(ql, kl, vl, gla, bl, s_f, s_b)
    return finish(o_c, zc), finish(o_l, zl)


def _hyena_filters(l, w1, b1, freq, w2, b2, w3):
    pos = jnp.arange(l, dtype=F32)[:, None]
    t = pos / max(l - 1, 1)
    bands = jnp.linspace(1e-4, HY_BANDS - 1, HY_BANDS, dtype=F32)[None, :]
    ang = (2 * math.pi / l) * pos * bands
    feat = jnp.concatenate([t, jnp.cos(ang), -jnp.sin(ang)], axis=-1)
    hp = lax.Precision.HIGHEST
    hdn = jnp.sin(freq * (jnp.dot(feat, w1, precision=hp) + b1))
    hdn = jnp.sin(freq * (jnp.dot(hdn, w2, precision=hp) + b2))
    filt = jnp.dot(hdn, w3, precision=hp).reshape(l, HY_ORDER, 2, D_MODEL)
    deltas = jnp.abs(jnp.linspace(math.log(HY_TARGET) / HY_SLOW, math.log(HY_TARGET) / HY_FAST,
                                  D_MODEL, dtype=F32))
    window = jnp.exp(-t * deltas[None, :])
    return filt * window[:, None, None, :]


def _two_sided_fftconv(u, h_fwd, h_bwd):
    l = u.shape[1]
    k = jnp.concatenate([h_fwd, jnp.zeros_like(h_fwd[:1]), jnp.flip(h_bwd[1:], axis=0)], axis=0)
    kf = jnp.fft.rfft(k, axis=0)
    uf = jnp.fft.rfft(u, n=2 * l, axis=1)
    return jnp.fft.irfft(uf * kf[None], n=2 * l, axis=1)[:, :l]


def _hyena_stream(p, on_grid, conv_w, f_w1, f_b1, f_freq, f_w2, f_b2, f_w3, bias):
    l = p.shape[1]
    p = _short_conv(p, conv_w, on_grid)
    v, x1, x2 = jnp.split(p, 3, axis=-1)
    filt = _hyena_filters(l, f_w1, f_b1, f_freq, f_w2, f_b2, f_w3)
    z = v
    for n, gate in enumerate((x1, x2)):
        conv = _two_sided_fftconv(z, filt[:, n, 0], filt[:, n, 1])
        z = gate * (conv + bias[n] * z)
    return z


def _shortconv_stream(p, on_grid, conv_w):
    bg, cg, xin = jnp.split(p, 3, axis=-1)
    return bg * _short_conv(cg * xin, conv_w, on_grid)


def _route(h, w_router, router_bias):
    t = h.shape[0]
    scores = jax.nn.sigmoid(jnp.dot(h, w_router, precision=lax.Precision.HIGHEST))
    choice = (scores + router_bias).reshape(t, N_GROUPS, EXPERTS_PER_GROUP)
    group_score = lax.top_k(choice, GROUP_SCORE_K)[0].sum(-1)
    group = jnp.argmax(group_score, axis=-1)
    in_group = jnp.take_along_axis(choice, group[:, None, None], axis=1)[:, 0]
    local = lax.top_k(in_group, TOP_K)[1]
    expert = group[:, None] * EXPERTS_PER_GROUP + local
    weight = jnp.take_along_axis(scores, expert, axis=1)
    weight = weight / jnp.sum(weight, axis=-1, keepdims=True)
    return expert.astype(jnp.int32), weight


def _moe_ffn(x, w_router, router_bias, w_gate, w_up, w_down):
    t, d = x.shape
    expert, weight = _route(x, w_router, router_bias)
    a = t * TOP_K
    e_flat = expert.reshape(-1)
    order = jnp.argsort(e_flat)
    e_sorted = e_flat[order]
    tok_sorted = (order // TOP_K).astype(jnp.int32)
    counts = jnp.zeros((N_EXPERTS,), jnp.int32).at[e_flat].add(1)
    start = jnp.cumsum(counts) - counts
    padded = (counts + MOE_BLOCK - 1) // MOE_BLOCK * MOE_BLOCK
    pend = jnp.cumsum(padded)
    pstart = pend - padded
    dest = pstart[e_sorted] + (jnp.arange(a, dtype=jnp.int32) - start[e_sorted])
    n_blocks = -(-a // MOE_BLOCK) + N_EXPERTS
    n_slots = n_blocks * MOE_BLOCK
    slot_tok = jnp.full((n_slots,), t, jnp.int32).at[dest].set(tok_sorted)
    block_start = jnp.arange(n_blocks, dtype=jnp.int32) * MOE_BLOCK
    block_expert = jnp.minimum(jnp.searchsorted(pend, block_start, side='right'),
                               N_EXPERTS - 1).astype(jnp.int32)
    x_pad = jnp.concatenate([x.astype(BF16), jnp.zeros((1, d), BF16)], axis=0)
    xs = x_pad[slot_tok]
    ys = _expert_ffn(xs, block_expert, jnp.ones((n_slots,), F32), w_gate, w_up, w_down)
    slot_of = jnp.zeros((a,), jnp.int32).at[order].set(dest).reshape(t, TOP_K)
    out = weight[:, 0:1] * ys[slot_of[:, 0]] + weight[:, 1:2] * ys[slot_of[:, 1]]
    return out


def _sc_layer_kernel(h_ref, win_ref, cw_ref, wout_ref, x_ref, gate_ref, o_ref, *, n_lat_tiles):
    d = D_MODEL
    tm = h_ref.shape[0]
    seg = jnp.where(pl.program_id(0) >= n_lat_tiles, CTX_LEN, GRID_W)
    pos = lax.broadcasted_iota(jnp.int32, (tm, 1), 0) & (seg - 1)
    hb = h_ref[...].astype(BF16)
    u = (jnp.dot(hb, win_ref[:, d:2 * d], preferred_element_type=F32)
         * jnp.dot(hb, win_ref[:, 2 * d:], preferred_element_type=F32))
    prev = jnp.where(pos != 0, pltpu.roll(u, 1, axis=0), 0.0)
    nxt = jnp.where(pos != seg - 1, pltpu.roll(u, tm - 1, axis=0), 0.0)
    cw = cw_ref[...]
    y = jnp.dot(hb, win_ref[:, :d], preferred_element_type=F32) * (
        cw[0:1] * prev + cw[1:2] * u + cw[2:3] * nxt)
    o_ref[...] = x_ref[...] + gate_ref[0] * jnp.dot(y.astype(BF16), wout_ref[...],
                                                    preferred_element_type=F32)


def _shortconv_layer(xs, h, w_in, conv_w, w_out, gate3, n_lat_tiles, blocks_per_batch):
    t, d = xs.shape
    tm = HY_TOK_TILE
    per_tile = tm // DN_BLOCK
    row = lambda i: (i, 0)
    fixed = lambda i: (0, 0)
    grp = lambda i: (_group_of_block(i * per_tile, n_lat_tiles * per_tile, blocks_per_batch), 0, 0)
    return pl.pallas_call(
        functools.partial(_sc_layer_kernel, n_lat_tiles=n_lat_tiles),
        out_shape=jax.ShapeDtypeStruct((t, d), F32),
        grid=(t // tm,),
        in_specs=[pl.BlockSpec((tm, d), row),
                  pl.BlockSpec((d, 3 * d), fixed),
                  pl.BlockSpec((3, d), fixed),
                  pl.BlockSpec((d, d), fixed),
                  pl.BlockSpec((tm, d), row),
                  pl.BlockSpec((1, 1, d), grp)],
        out_specs=pl.BlockSpec((tm, d), row),
        compiler_params=pltpu.CompilerParams(
            dimension_semantics=("arbitrary",), vmem_limit_bytes=VMEM_LIMIT_BYTES),
        name="shortconv_layer",
    )(h, w_in.astype(BF16), conv_w, w_out.astype(BF16), xs, gate3)


def kernel(x, c, ctx, c_ctx, ada_w, ada_b, norm_mix, norm_ffn, norm_final, dn_w_in, dn_conv, dn_a_log,
           dn_dt_bias, dn_out_norm, dn_w_out, hy_w_in, hy_conv, hy_f_w1, hy_f_b1, hy_f_freq, hy_f_w2,
           hy_f_b2, hy_f_w3, hy_bias, hy_w_out, sc_w_in, sc_conv, sc_w_out, w_router, router_bias,
           moe_w_gate, moe_w_up, moe_w_down):
    d = D_MODEL
    bsz, seq, _ = x.shape
    n_ctx = bsz * CTX_LEN
    n_lat = bsz * seq
    silu_c = jax.nn.silu(c)
    silu_cc = jax.nn.silu(c_ctx)
    hp = lax.Precision.HIGHEST

    xs = jnp.concatenate([x.reshape(n_lat, d), ctx.reshape(n_ctx, d)], axis=0)

    def rows(ctx_vec, lat_vecs):
        return jnp.concatenate([jnp.repeat(lat_vecs, seq, axis=0),
                                jnp.broadcast_to(ctx_vec[None], (n_ctx, d))], axis=0)

    for i in range(DEPTH):
        kind, j = i % N_MIXERS, i // N_MIXERS
        ml = jnp.split(jnp.dot(silu_c, ada_w[i], precision=hp) + ada_b[i], N_MOD, axis=-1)
        mc = jnp.split(jnp.dot(silu_cc, ada_w[i], precision=hp) + ada_b[i], N_MOD, axis=-1)
        mod = [rows(mc[m], ml[m]) for m in range(N_MOD)]
        gate3 = jnp.concatenate([mc[2][None], ml[2]], axis=0)[:, None, :]

        h = _modulate(xs, norm_mix[i], mod[0], mod[1])
        if kind == 0:
            xs = _deltanet_layer(xs, h, dn_w_in[j], dn_conv[j], dn_a_log[j], dn_dt_bias[j],
                                 dn_out_norm[j], dn_w_out[j], gate3, n_lat // DN_BLOCK, seq // DN_BLOCK)
        elif kind == 1:
            xs = _hyena_layer(xs, h, hy_w_in[j], hy_conv[j], hy_f_w1[j], hy_f_b1[j], hy_f_freq[j],
                              hy_f_w2[j], hy_f_b2[j], hy_f_w3[j], hy_bias[j], hy_w_out[j], gate3, bsz, seq)
        else:
            xs = _shortconv_layer(xs, h, sc_w_in[j], sc_conv[j], sc_w_out[j], gate3,
                                  n_lat // HY_TOK_TILE, seq // DN_BLOCK)
        per_group = lambda m: jnp.concatenate([mc[m][None], ml[m]], axis=0)[:, None, :]
        xs = _moe_layer(xs, norm_ffn[i], per_group(3), per_group(4), per_group(5), w_router, router_bias,
                        moe_w_gate[i], moe_w_up[i], moe_w_down[i], n_lat // MOE_TILE, seq // DN_BLOCK)
    x_out = xs[:n_lat].reshape(bsz, seq, d)
    return _rmsnorm(x_out, norm_final)
```

```python
import functools
import math

import numpy as np
import jax
import jax.numpy as jnp
from jax import lax
from jax.experimental import pallas as pl
from jax.experimental.pallas import tpu as pltpu

D_MODEL = 1024
DEPTH = 4
CTX_LEN = 256
GRID_W = 64
N_MIXERS = 3
EPS = 1e-6
N_MOD = 6

DN_HEADS = 8
DN_DK = D_MODEL // DN_HEADS
DN_DV = D_MODEL // DN_HEADS
DN_CHUNK = 64

HY_ORDER = 2
HY_BANDS = 16
HY_TARGET = 1e-2
HY_FAST = 0.3
HY_SLOW = 1.5

N_EXPERTS = 32
N_GROUPS = 8
EXPERTS_PER_GROUP = N_EXPERTS // N_GROUPS
GROUP_SCORE_K = 2
TOP_K = 2
D_EXPERT = 512
MOE_BLOCK = 512

F32 = jnp.float32
BF16 = jnp.bfloat16

ROW_TILE = 512
VMEM_LIMIT_BYTES = 48 * 1024 * 1024


def _mm_kernel(x_ref, w_ref, o_ref):
    o_ref[...] = jnp.dot(x_ref[...].astype(BF16), w_ref[...], preferred_element_type=F32)


def _mm(x, w, tn=None):
    m, k = x.shape
    n = w.shape[1]
    tm = min(ROW_TILE, m)
    tn = n if tn is None else tn
    assert m % tm == 0 and n % tn == 0
    return pl.pallas_call(
        _mm_kernel,
        out_shape=jax.ShapeDtypeStruct((m, n), F32),
        grid=(m // tm, n // tn),
        in_specs=[pl.BlockSpec((tm, k), lambda i, j: (i, 0)),
                  pl.BlockSpec((k, tn), lambda i, j: (0, j))],
        out_specs=pl.BlockSpec((tm, tn), lambda i, j: (i, j)),
        compiler_params=pltpu.CompilerParams(
            dimension_semantics=("arbitrary", "arbitrary"), vmem_limit_bytes=VMEM_LIMIT_BYTES),
        name="dense_mm",
    )(x, w.astype(BF16))


def _expert_kernel(be_ref, x_ref, wg_ref, wu_ref, wd_ref, sw_ref, o_ref):
    del be_ref
    xb = x_ref[...]
    g = jnp.dot(xb, wg_ref[0], preferred_element_type=F32)
    u = jnp.dot(xb, wu_ref[0], preferred_element_type=F32)
    hid = (g * jax.nn.sigmoid(g)) * u
    y = jnp.dot(hid.astype(BF16), wd_ref[0], preferred_element_type=F32)
    o_ref[...] = y * sw_ref[...]


def _expert_ffn(xs, block_expert, slot_w, w_gate, w_up, w_down):
    n_slots, d = xs.shape
    n_blocks = n_slots // MOE_BLOCK
    grid_spec = pltpu.PrefetchScalarGridSpec(
        num_scalar_prefetch=1,
        grid=(n_blocks,),
        in_specs=[
            pl.BlockSpec((MOE_BLOCK, d), lambda i, be: (i, 0)),
            pl.BlockSpec((1, d, D_EXPERT), lambda i, be: (be[i], 0, 0)),
            pl.BlockSpec((1, d, D_EXPERT), lambda i, be: (be[i], 0, 0)),
            pl.BlockSpec((1, D_EXPERT, d), lambda i, be: (be[i], 0, 0)),
            pl.BlockSpec((MOE_BLOCK, 1), lambda i, be: (i, 0)),
        ],
        out_specs=pl.BlockSpec((MOE_BLOCK, d), lambda i, be: (i, 0)),
    )
    return pl.pallas_call(
        _expert_kernel,
        out_shape=jax.ShapeDtypeStruct((n_slots, d), F32),
        grid_spec=grid_spec,
        compiler_params=pltpu.CompilerParams(
            dimension_semantics=("arbitrary",), vmem_limit_bytes=VMEM_LIMIT_BYTES),
        name="expert_ffn",
    )(block_expert, xs, w_gate.astype(BF16), w_up.astype(BF16), w_down.astype(BF16),
      slot_w.reshape(n_slots, 1))


DN_BLOCK = CTX_LEN
DN_HB = 2
N_CHUNKS_PER_BLOCK = DN_BLOCK // DN_CHUNK


def _group_of_block(i, n_lat_blocks, blocks_per_batch):
    return jnp.where(i >= n_lat_blocks, 0, 1 + i // blocks_per_batch)


def _modulated(x, gain, shift, scale):
    y = x * lax.rsqrt(jnp.mean(x * x, axis=-1, keepdims=True) + EPS) * gain
    return y * (1 + scale) + shift


def _dn_inproj_kernel(x_ref, gain_ref, shift_ref, scale_ref, w_ref, wab_ref, cw_ref, alog_ref, dtb_ref,
                      q_ref, k_ref, v_ref, z_ref, gate_ref, *, n_lat_blocks):
    i = pl.program_id(0)
    nrow = DN_BLOCK
    d = D_MODEL
    pair = 2 * DN_DK
    seg = jnp.where(i >= n_lat_blocks, CTX_LEN, GRID_W)
    r = lax.broadcasted_iota(jnp.int32, (nrow, 1), 0)
    pos = r & (seg - 1)
    not_first = pos != 0
    not_last = pos != seg - 1
    h = _modulated(x_ref[...], gain_ref[...], shift_ref[0], scale_ref[0])
    hb = h.astype(BF16)
    outs = (q_ref, k_ref, v_ref)
    for part in range(3):
        for hp in range(d // pair):
            col = part * d + hp * pair
            x = jnp.dot(hb, w_ref[:, col:col + pair], preferred_element_type=F32)
            cw = cw_ref[:, col:col + pair]
            xp = jnp.where(not_first, pltpu.roll(x, 1, axis=0), 0.0)
            xn = jnp.where(not_last, pltpu.roll(x, nrow - 1, axis=0), 0.0)
            y = cw[0:1] * xp + cw[1:2] * x + cw[2:3] * xn
            y = y * jax.nn.sigmoid(y)
            for hh in range(2):
                yh = y[:, hh * DN_DK:(hh + 1) * DN_DK]
                if part < 2:
                    yh = yh * lax.rsqrt(jnp.sum(yh * yh, axis=-1, keepdims=True) + EPS)
                if part == 0:
                    yh = yh * DN_DK ** -0.5
                outs[part][:, hp * pair + hh * DN_DK:hp * pair + (hh + 1) * DN_DK] = yh
    for j in range(d // pair):
        z_ref[:, j * pair:(j + 1) * pair] = jnp.dot(hb, w_ref[:, 3 * d + j * pair:3 * d + (j + 1) * pair],
                                                    preferred_element_type=F32)

    ab = jnp.dot(h, wab_ref[...], precision=lax.Precision.HIGHEST, preferred_element_type=F32)
    nd = 2 * DN_HEADS
    a = ab[:, :nd] + dtb_ref[...]
    softplus = jnp.maximum(a, 0.0) + jnp.log(1.0 + jnp.exp(-jnp.abs(a)))
    g = -jnp.exp(alog_ref[...]) * softplus
    beta = jax.nn.sigmoid(ab[:, nd:])
    cpos = r & (DN_CHUNK - 1)
    gp, gs = g, g
    sh = 1
    while sh < DN_CHUNK:
        gp = gp + jnp.where(cpos >= sh, pltpu.roll(gp, sh, axis=0), 0.0)
        gs = gs + jnp.where(cpos < DN_CHUNK - sh, pltpu.roll(gs, nrow - sh, axis=0), 0.0)
        sh *= 2
    colid = lax.broadcasted_iota(jnp.int32, (1, nd), 1)
    gate_ref[:, :nd] = jnp.where(colid < DN_HEADS, gp, gs)
    gate_ref[:, nd:] = beta


def _dn_inproj(xs, gain, shift3, scale3, w_in, conv_w, a_log, dt_bias, n_lat_blocks, blocks_per_batch):
    t, d = xs.shape
    nd = 2 * DN_HEADS
    row = lambda i: (i, 0)
    fixed = lambda i: (0, 0)
    grp = lambda i: (_group_of_block(i, n_lat_blocks, blocks_per_batch), 0, 0)
    return pl.pallas_call(
        functools.partial(_dn_inproj_kernel, n_lat_blocks=n_lat_blocks),
        out_shape=[jax.ShapeDtypeStruct((t, d), F32)] * 4 + [jax.ShapeDtypeStruct((t, 2 * nd), F32)],
        grid=(t // DN_BLOCK,),
        in_specs=[pl.BlockSpec((DN_BLOCK, d), row),
                  pl.BlockSpec((1, d), fixed),
                  pl.BlockSpec((1, 1, d), grp),
                  pl.BlockSpec((1, 1, d), grp),
                  pl.BlockSpec((d, 4 * d), fixed),
                  pl.BlockSpec((d, 2 * nd), fixed),
                  pl.BlockSpec((3, 3 * d), fixed),
                  pl.BlockSpec((1, nd), fixed),
                  pl.BlockSpec((1, nd), fixed)],
        out_specs=[pl.BlockSpec((DN_BLOCK, d), row)] * 4 + [pl.BlockSpec((DN_BLOCK, 2 * nd), row)],
        compiler_params=pltpu.CompilerParams(
            dimension_semantics=("arbitrary",), vmem_limit_bytes=VMEM_LIMIT_BYTES),
        name="dn_inproj",
    )(xs, gain.reshape(1, d), shift3, scale3, w_in[:, :4 * d].astype(BF16), w_in[:, 4 * d:], conv_w,
      a_log.reshape(1, nd), dt_bias.reshape(1, nd))


def _dotb(a, b):
    return jnp.dot(a.astype(BF16), b.astype(BF16), preferred_element_type=F32)


def _dotb_nt(a, b):
    return lax.dot_general(a.astype(BF16), b.astype(BF16), (((1,), (1,)), ((), ())),
                           preferred_element_type=F32)


def _dotb_tn(a, b):
    return lax.dot_general(a.astype(BF16), b.astype(BF16), (((0,), (0,)), ((), ())),
                           preferred_element_type=F32)


def _unit_tri_inverses(mats, ii, jj):
    eye = (ii == jj).astype(F32)
    diag8 = (ii >> 3) == (jj >> 3)
    n = [-jnp.where(diag8, a, 0.0) for a in mats]
    n2 = [_dotb(x, x) for x in n]
    m = [eye + x for x in n]
    m = [x + _dotb(x, y) for x, y in zip(m, n2)]
    n4 = [_dotb(x, x) for x in n2]
    m = [x + _dotb(x, y) for x, y in zip(m, n4)]
    sh = 3
    while (1 << sh) < DN_CHUNK:
        off = ((ii >> (sh + 1)) == (jj >> (sh + 1))) & ((ii >> sh) != (jj >> sh))
        cm = [_dotb(jnp.where(off, a, 0.0), x) for a, x in zip(mats, m)]
        m = [x - _dotb(x, y) for x, y in zip(m, cm)]
        sh += 1
    return m


def _dn_scan_kernel(qf_ref, kf_ref, vf_ref, gcf_ref, grf_ref, qb_ref, kb_ref, vb_ref, gcb_ref, grb_ref,
                    of_ref, ob_ref, s_ref):
    @pl.when(pl.program_id(2) == 0)
    def _():
        s_ref[...] = jnp.zeros_like(s_ref)

    c = DN_CHUNK
    ncb = N_CHUNKS_PER_BLOCK
    ii = lax.broadcasted_iota(jnp.int32, (c, c), 0)
    jj = lax.broadcasted_iota(jnp.int32, (c, c), 1)
    incl = (ii >= jj, ii <= jj)
    strict = (ii > jj, ii < jj)
    dirs = ((qf_ref, kf_ref, vf_ref, gcf_ref, grf_ref, of_ref),
            (qb_ref, kb_ref, vb_ref, gcb_ref, grb_ref, ob_ref))
    items = [(d, hh, ci) for d in range(2) for hh in range(DN_HB) for ci in range(ncb)]

    def rows(ci):
        return slice(ci * c, (ci + 1) * c)

    def cols(hh):
        return slice(hh * DN_DK, (hh + 1) * DN_DK)

    q = [dirs[d][0][rows(ci), cols(hh)] for d, hh, ci in items]
    k = [dirs[d][1][rows(ci), cols(hh)] for d, hh, ci in items]
    v = [dirs[d][2][rows(ci), cols(hh)] for d, hh, ci in items]
    gc = [dirs[d][3][hh, rows(ci), d:d + 1] for d, hh, ci in items]
    gr = [dirs[d][4][hh, d:d + 1, rows(ci)] for d, hh, ci in items]
    beta = [dirs[d][3][hh, rows(ci), 2 + d:3 + d] for d, hh, ci in items]

    decay = [jnp.where(incl[it[0]], jnp.exp(jnp.where(incl[it[0]], x - y, 0.0)), 0.0)
             for it, x, y in zip(items, gc, gr)]
    kb = [x * y for x, y in zip(k, beta)]
    a = [_dotb_nt(x, y) * jnp.where(strict[it[0]], z, 0.0) for it, x, y, z in zip(items, kb, k, decay)]
    attn = [_dotb_nt(x, y) * z for x, y, z in zip(q, k, decay)]
    tinv = _unit_tri_inverses(a, ii, jj)
    eg = [jnp.exp(x) for x in gc]
    uw = [_dotb(t, jnp.concatenate([x * y, z * e], axis=-1))
          for t, x, y, z, e in zip(tinv, v, beta, kb, eg)]
    g_last = [x[0:1] if it[0] else x[c - 1:c] for it, x in zip(items, gc)]
    wq = [jnp.concatenate([x[:, DN_DV:], y * e], axis=0) for x, y, e in zip(uw, q, eg)]
    k_dec = [x * jnp.exp(y - z) for x, y, z in zip(k, g_last, gc)]
    s_dec = [jnp.exp(x) for x in g_last]

    chains = [(d, hh) for d in range(2) for hh in range(DN_HB)]
    state = [s_ref[d, hh] for d, hh in chains]
    for step in range(ncb):
        cur = [items.index((d, hh, ncb - 1 - step if d else step)) for d, hh in chains]
        ws = [_dotb(wq[n], s) for n, s in zip(cur, state)]
        v_new = [uw[n][:, :DN_DV] - x[:c] for n, x in zip(cur, ws)]
        o = [x[c:] + _dotb(attn[n], y) for n, x, y in zip(cur, ws, v_new)]
        state = [s * s_dec[n] + _dotb_tn(k_dec[n], y) for n, s, y in zip(cur, state, v_new)]
        for n, x in zip(cur, o):
            d, hh, ci = items[n]
            dirs[d][5][rows(ci), cols(hh)] = x
    for (d, hh), s in zip(chains, state):
        s_ref[d, hh] = s


def _dn_scan(q, k, v, gates, n_lat_blocks, blocks_per_batch):
    t, d = q.shape
    bsz = n_lat_blocks // blocks_per_batch
    g4 = gates.reshape(t, 4, DN_HEADS)
    gcol = jnp.transpose(g4, (2, 0, 1))
    grow = jnp.transpose(g4, (2, 1, 0))

    def blk_f(b, s):
        return jnp.where(s == 0, n_lat_blocks + b, b * blocks_per_batch + s - 1)

    def blk_b(b, s):
        return jnp.where(s == 0, n_lat_blocks + b, b * blocks_per_batch + blocks_per_batch - s)

    hw = DN_HB * DN_DK

    def specs(blk):
        return [pl.BlockSpec((DN_BLOCK, hw), lambda b, hg, s: (blk(b, s), hg))] * 3 + [
            pl.BlockSpec((DN_HB, DN_BLOCK, 4), lambda b, hg, s: (hg, blk(b, s), 0)),
            pl.BlockSpec((DN_HB, 4, DN_BLOCK), lambda b, hg, s: (hg, 0, blk(b, s)))]

    return pl.pallas_call(
        _dn_scan_kernel,
        out_shape=[jax.ShapeDtypeStruct((t, d), F32)] * 2,
        grid=(bsz, DN_HEADS // DN_HB, 1 + blocks_per_batch),
        in_specs=specs(blk_f) + specs(blk_b),
        out_specs=[pl.BlockSpec((DN_BLOCK, hw), lambda b, hg, s: (blk_f(b, s), hg)),
                   pl.BlockSpec((DN_BLOCK, hw), lambda b, hg, s: (blk_b(b, s), hg))],
        scratch_shapes=[pltpu.VMEM((2, DN_HB, DN_DK, DN_DV), F32)],
        compiler_params=pltpu.CompilerParams(
            dimension_semantics=("arbitrary", "arbitrary", "arbitrary"),
            vmem_limit_bytes=VMEM_LIMIT_BYTES),
        name="dn_scan",
    )(q, k, v, gcol, grow, q, k, v, gcol, grow)


def _dn_out_kernel(of_ref, ob_ref, z_ref, on_ref, w_ref, x_ref, gate_ref, o_ref):
    z = z_ref[...]
    zs = z * jax.nn.sigmoid(z)
    parts = []
    for h in range(DN_HEADS):
        cols = slice(h * DN_DV, (h + 1) * DN_DV)
        o = of_ref[:, cols] + ob_ref[:, cols]
        o = o * lax.rsqrt(jnp.mean(o * o, axis=-1, keepdims=True) + EPS)
        parts.append(o)
    y = jnp.concatenate(parts, axis=-1) * on_ref[...] * zs
    o_ref[...] = x_ref[...] + gate_ref[0] * jnp.dot(y.astype(BF16), w_ref[...],
                                                    preferred_element_type=F32)


def _dn_out(o_f, o_b, z, out_norm, w_out, xs, gate3, n_lat_blocks, blocks_per_batch):
    t, d = xs.shape
    row = lambda i: (i, 0)
    fixed = lambda i: (0, 0)
    grp = lambda i: (_group_of_block(i, n_lat_blocks, blocks_per_batch), 0, 0)
    return pl.pallas_call(
        _dn_out_kernel,
        out_shape=jax.ShapeDtypeStruct((t, d), F32),
        grid=(t // DN_BLOCK,),
        in_specs=[pl.BlockSpec((DN_BLOCK, d), row),
                  pl.BlockSpec((DN_BLOCK, d), row),
                  pl.BlockSpec((DN_BLOCK, d), row),
                  pl.BlockSpec((1, d), fixed),
                  pl.BlockSpec((d, d), fixed),
                  pl.BlockSpec((DN_BLOCK, d), row),
                  pl.BlockSpec((1, 1, d), grp)],
        out_specs=pl.BlockSpec((DN_BLOCK, d), row),
        compiler_params=pltpu.CompilerParams(
            dimension_semantics=("arbitrary",), vmem_limit_bytes=VMEM_LIMIT_BYTES),
        name="dn_out",
    )(o_f, o_b, z, jnp.tile(out_norm, DN_HEADS).reshape(1, d), w_out.astype(BF16), xs, gate3)


def _deltanet_layer(xs, gain, shift3, scale3, w_in, conv_w, a_log, dt_bias, out_norm, w_out, gate3,
                    n_lat_blocks, blocks_per_batch):
    q, k, v, z, gates = _dn_inproj(xs, gain, shift3, scale3, w_in, conv_w, a_log, dt_bias, n_lat_blocks,
                                   blocks_per_batch)
    o_f, o_b = _dn_scan(q, k, v, gates, n_lat_blocks, blocks_per_batch)
    return _dn_out(o_f, o_b, z, out_norm, w_out, xs, gate3, n_lat_blocks, blocks_per_batch)


HY_N2 = 256
HY_CB = 8
HY_TOK_TILE = 512
HY_FILT_TILE = 512


def _dft_constants(nr):
    n1, n2 = 2 * nr, HY_N2
    n = n1 * n2
    a1 = np.arange(n1)
    f1 = np.exp(-2j * np.pi * np.outer(a1, a1) / n1)
    lhs_fwd = np.concatenate([f1.real[:, :nr], f1.imag[:, :nr]], axis=0)
    lhs_inv = np.concatenate([f1.real[:nr, :], f1.imag[:nr, :]], axis=1) / n
    tw = np.exp(-2j * np.pi * np.outer(a1, np.arange(n2)) / n)
    a2 = np.arange(n2)
    f2 = np.exp(-2j * np.pi * np.outer(a2, a2) / n2)
    w_fwd = np.block([[f2.real, f2.imag], [-f2.imag, f2.real]])
    w_inv = np.block([[f2.real, -f2.imag], [f2.imag, f2.real]])
    return (jnp.asarray(lhs_fwd, BF16), jnp.asarray(lhs_inv, BF16), jnp.asarray(tw.real, F32),
            jnp.asarray(tw.imag, F32), jnp.asarray(w_fwd, BF16), jnp.asarray(w_inv, BF16))


def _hy_dft(x3, lhs_fwd, twr, twi, w_fwd):
    n1 = twr.shape[0]
    a = [jnp.dot(lhs_fwd, x3[c].astype(BF16), preferred_element_type=F32) for c in range(x3.shape[0])]
    br = jnp.concatenate([t[:n1] * twr - t[n1:] * twi for t in a], axis=0)
    bi = jnp.concatenate([t[:n1] * twi + t[n1:] * twr for t in a], axis=0)
    b = jnp.concatenate([br, bi], axis=1)
    return jnp.dot(b.astype(BF16), w_fwd, preferred_element_type=F32)


def _hy_idft(p, cb, lhs_inv, twr, twi, w_inv):
    n1, n2 = twr.shape
    c = jnp.dot(p.astype(BF16), w_inv, preferred_element_type=F32)
    out = []
    for ch in range(cb):
        cr = c[ch * n1:(ch + 1) * n1, :n2]
        ci = c[ch * n1:(ch + 1) * n1, n2:]
        d = jnp.concatenate([cr * twr + ci * twi, ci * twr - cr * twi], axis=0)
        out.append(jnp.dot(lhs_inv, d.astype(BF16), preferred_element_type=F32))
    return out


def _hy_spectrum_kernel(hf_ref, hb_ref, lf_ref, twr_ref, twi_ref, wf_ref, o_ref):
    cb, nr, n2 = hf_ref.shape
    first = ((lax.broadcasted_iota(jnp.int32, (nr, n2), 0) == 0)
             & (lax.broadcasted_iota(jnp.int32, (nr, n2), 1) == 0))
    hb = jnp.where(first, 0.0, hb_ref[...])
    consts = (lf_ref[...], twr_ref[...], twi_ref[...], wf_ref[...])
    xf = _hy_dft(hf_ref[...], *consts)
    xb = _hy_dft(hb, *consts)
    n1 = 2 * nr
    o_ref[...] = jnp.concatenate([xf[:, :n2] + xb[:, :n2], xf[:, n2:] - xb[:, n2:]],
                                 axis=1).reshape(cb, n1, 2 * n2)


def _hy_spectrum(filt, consts):
    d = D_MODEL
    l = filt.shape[1]
    nr = l // HY_N2
    n1 = 2 * nr
    lhs_fwd, _, twr, twi, w_fwd, _ = consts
    cpo = d // HY_CB
    fixed2 = lambda o, c: (0, 0)
    return pl.pallas_call(
        _hy_spectrum_kernel,
        out_shape=jax.ShapeDtypeStruct((HY_ORDER * d, n1, 2 * HY_N2), F32),
        grid=(HY_ORDER, cpo),
        in_specs=[pl.BlockSpec((HY_CB, nr, HY_N2), lambda o, c: (2 * o * cpo + c, 0, 0)),
                  pl.BlockSpec((HY_CB, nr, HY_N2), lambda o, c: ((2 * o + 1) * cpo + c, 0, 0)),
                  pl.BlockSpec(lhs_fwd.shape, fixed2),
                  pl.BlockSpec(twr.shape, fixed2),
                  pl.BlockSpec(twi.shape, fixed2),
                  pl.BlockSpec(w_fwd.shape, fixed2)],
        out_specs=pl.BlockSpec((HY_CB, n1, 2 * HY_N2), lambda o, c: (o * cpo + c, 0, 0)),
        compiler_params=pltpu.CompilerParams(
            dimension_semantics=("arbitrary", "arbitrary"), vmem_limit_bytes=VMEM_LIMIT_BYTES),
        name="hy_spectrum",
    )(filt.reshape(-1, nr, HY_N2), filt.reshape(-1, nr, HY_N2), lhs_fwd, twr, twi, w_fwd)


def _hy_conv_kernel(z_ref, g_ref, k_ref, bias_ref, lf_ref, li_ref, twr_ref, twi_ref, wf_ref, wi_ref,
                    o_ref):
    cb, nr, n2 = z_ref.shape
    twr, twi = twr_ref[...], twi_ref[...]
    z = z_ref[...]
    x = _hy_dft(z, lf_ref[...], twr, twi, wf_ref[...])
    kk = k_ref[...].reshape(x.shape)
    xr, xi, kr, ki = x[:, :n2], x[:, n2:], kk[:, :n2], kk[:, n2:]
    p = jnp.concatenate([xr * kr - xi * ki, xr * ki + xi * kr], axis=1)
    conv = _hy_idft(p, cb, li_ref[...], twr, twi, wi_ref[...])
    for c in range(cb):
        o_ref[c] = g_ref[c] * (conv[c] + bias_ref[c] * z[c])


def _hy_conv(z, z_part, gate, gate_part, khat, order, bias, consts, bsz):
    d = D_MODEL
    l = z.shape[1] // bsz
    nr = l // HY_N2
    n1 = 2 * nr
    cpo = d // HY_CB
    lhs_fwd, lhs_inv, twr, twi, w_fwd, w_inv = consts
    fixed2 = lambda c, b: (0, 0)
    out = pl.pallas_call(
        _hy_conv_kernel,
        out_shape=jax.ShapeDtypeStruct((d, bsz * nr, HY_N2), F32),
        grid=(cpo, bsz),
        in_specs=[pl.BlockSpec((HY_CB, nr, HY_N2), lambda c, b: (z_part * cpo + c, b, 0)),
                  pl.BlockSpec((HY_CB, nr, HY_N2), lambda c, b: (gate_part * cpo + c, b, 0)),
                  pl.BlockSpec((HY_CB, n1, 2 * HY_N2), lambda c, b: (order * cpo + c, 0, 0)),
                  pl.BlockSpec((HY_CB, 1, 1), lambda c, b: (c, 0, 0)),
                  pl.BlockSpec(lhs_fwd.shape, fixed2),
                  pl.BlockSpec(lhs_inv.shape, fixed2),
                  pl.BlockSpec(twr.shape, fixed2),
                  pl.BlockSpec(twi.shape, fixed2),
                  pl.BlockSpec(w_fwd.shape, fixed2),
                  pl.BlockSpec(w_inv.shape, fixed2)],
        out_specs=pl.BlockSpec((HY_CB, nr, HY_N2), lambda c, b: (c, b, 0)),
        compiler_params=pltpu.CompilerParams(
            dimension_semantics=("arbitrary", "arbitrary"), vmem_limit_bytes=VMEM_LIMIT_BYTES),
        name="hy_conv",
    )(z.reshape(-1, bsz * nr, HY_N2), gate.reshape(-1, bsz * nr, HY_N2), khat,
      bias.reshape(d, 1, 1), lhs_fwd, lhs_inv, twr, twi, w_fwd, w_inv)
    return out.reshape(d, bsz * l)


def _hy_ctx_kernel(z_ref, g_ref, hf_ref, hb_ref, bias_ref, wf_ref, wi_ref, o_ref, *, bsz):
    l = hf_ref.shape[1]
    wf, wi = wf_ref[...], wi_ref[...]
    hb = jnp.where(lax.broadcasted_iota(jnp.int32, (1, l), 1) == 0, 0.0, hb_ref[...])
    kf = jnp.dot(hf_ref[...].astype(BF16), wf, preferred_element_type=F32)
    kb = jnp.dot(hb.astype(BF16), wf, preferred_element_type=F32)
    n = 2 * l
    kr, ki = kf[:, :n] + kb[:, :n], kf[:, n:] - kb[:, n:]
    bias = bias_ref[...]
    for b in range(bsz):
        z = z_ref[:, b * l:(b + 1) * l]
        x = jnp.dot(z.astype(BF16), wf, preferred_element_type=F32)
        xr, xi = x[:, :n], x[:, n:]
        p = jnp.concatenate([xr * kr - xi * ki, xr * ki + xi * kr], axis=1)
        conv = jnp.dot(p.astype(BF16), wi, preferred_element_type=F32)
        o_ref[:, b * l:(b + 1) * l] = g_ref[:, b * l:(b + 1) * l] * (conv + bias * z)


def _hy_ctx(z, z_part, gate, gate_part, filt, order, bias, bsz):
    d = D_MODEL
    l = filt.shape[1]
    n = 2 * l
    ang = 2 * np.pi * np.outer(np.arange(l), np.arange(n)) / n
    w_fwd = jnp.asarray(np.concatenate([np.cos(ang), -np.sin(ang)], axis=1), BF16)
    w_inv = jnp.asarray(np.concatenate([np.cos(ang.T), -np.sin(ang.T)], axis=0) / n, BF16)
    cb = 256
    nblk = d // cb
    fixed = lambda c: (0, 0)
    return pl.pallas_call(
        functools.partial(_hy_ctx_kernel, bsz=bsz),
        out_shape=jax.ShapeDtypeStruct((d, bsz * l), F32),
        grid=(nblk,),
        in_specs=[pl.BlockSpec((cb, bsz * l), lambda c: (z_part * nblk + c, 0)),
                  pl.BlockSpec((cb, bsz * l), lambda c: (gate_part * nblk + c, 0)),
                  pl.BlockSpec((cb, l), lambda c: (2 * order * nblk + c, 0)),
                  pl.BlockSpec((cb, l), lambda c: ((2 * order + 1) * nblk + c, 0)),
                  pl.BlockSpec((cb, 1), lambda c: (c, 0)),
                  pl.BlockSpec(w_fwd.shape, fixed),
                  pl.BlockSpec(w_inv.shape, fixed)],
        out_specs=pl.BlockSpec((cb, bsz * l), lambda c: (c, 0)),
        compiler_params=pltpu.CompilerParams(
            dimension_semantics=("arbitrary",), vmem_limit_bytes=VMEM_LIMIT_BYTES),
        name="hy_ctx_conv",
    )(z, gate, filt, filt, bias.reshape(d, 1), w_fwd, w_inv)


def _hy_filter_kernel(band_ref, w1t_ref, w1c_ref, w1s_ref, b1_ref, fr_ref, w2_ref, b2_ref, w3_ref,
                      delta_ref, o_ref, *, l):
    tl = o_ref.shape[1]
    d = D_MODEL
    hp = lax.Precision.HIGHEST
    pos = (lax.broadcasted_iota(jnp.int32, (1, tl), 1) + pl.program_id(0) * tl).astype(F32)
    t = pos / max(l - 1, 1)
    ang = ((2 * math.pi / l) * pos) * band_ref[...]
    fr = fr_ref[...]
    pre = (w1t_ref[...] * t + jnp.dot(w1c_ref[...], jnp.cos(ang), precision=hp)
           + jnp.dot(w1s_ref[...], -jnp.sin(ang), precision=hp) + b1_ref[...])
    hdn = jnp.sin(fr * pre)
    hdn = jnp.sin(fr * (jnp.dot(w2_ref[...], hdn, precision=hp) + b2_ref[...]))
    window = jnp.exp(-t * delta_ref[...])
    for part in range(2 * HY_ORDER):
        rows = slice(part * d, (part + 1) * d)
        o_ref[rows, :] = jnp.dot(w3_ref[rows, :], hdn, precision=hp) * window


def _hy_filter(l, w1, b1, freq, w2, b2, w3):
    d = D_MODEL
    nb = HY_BANDS
    tl = min(HY_FILT_TILE, l)
    col = lambda v: v.reshape(-1, 1)
    bands = jnp.linspace(1e-4, nb - 1, nb, dtype=F32)
    deltas = jnp.abs(jnp.linspace(math.log(HY_TARGET) / HY_SLOW, math.log(HY_TARGET) / HY_FAST, d, dtype=F32))
    w1t = w1.T
    args = (col(bands), w1t[:, 0:1], w1t[:, 1:1 + nb], w1t[:, 1 + nb:], col(b1), col(freq), w2.T, col(b2),
            w3.T, col(deltas))
    return pl.pallas_call(
        functools.partial(_hy_filter_kernel, l=l),
        out_shape=jax.ShapeDtypeStruct((2 * HY_ORDER * d, l), F32),
        grid=(l // tl,),
        in_specs=[pl.BlockSpec(a.shape, lambda j: (0, 0)) for a in args],
        out_specs=pl.BlockSpec((2 * HY_ORDER * d, tl), lambda j: (0, j)),
        compiler_params=pltpu.CompilerParams(
            dimension_semantics=("arbitrary",), vmem_limit_bytes=VMEM_LIMIT_BYTES),
        name="hy_filter",
    )(*args)


def _hy_inproj_kernel(x_ref, gain_ref, shift_ref, scale_ref, wt_ref, cw_ref, o_ref, *, seg):
    nch = wt_ref.shape[0]
    tm = x_ref.shape[0]
    hb = _modulated(x_ref[...], gain_ref[...], shift_ref[0], scale_ref[0]).astype(BF16)
    pos = lax.broadcasted_iota(jnp.int32, (1, tm), 1) & (seg - 1)
    not_first = pos != 0
    not_last = pos != seg - 1
    sub = 512
    for j in range(nch // sub):
        rows = slice(j * sub, (j + 1) * sub)
        p = _dotb_nt(wt_ref[rows, :], hb)
        cw = cw_ref[rows, :]
        prev = jnp.where(not_first, pltpu.roll(p, 1, axis=1), 0.0)
        nxt = jnp.where(not_last, pltpu.roll(p, tm - 1, axis=1), 0.0)
        o_ref[rows, :] = cw[:, 0:1] * prev + cw[:, 1:2] * p + cw[:, 2:3] * nxt


def _hy_inproj(xs, gain, shift3, scale3, w_in, conv_w, first_tile, n_tiles, seg, n_lat_blocks,
               blocks_per_batch):
    k = xs.shape[1]
    nch = w_in.shape[1]
    tm = HY_TOK_TILE
    per_tile = tm // DN_BLOCK
    grp = lambda i: (_group_of_block((first_tile + i) * per_tile, n_lat_blocks, blocks_per_batch), 0, 0)
    return pl.pallas_call(
        functools.partial(_hy_inproj_kernel, seg=seg),
        out_shape=jax.ShapeDtypeStruct((nch, n_tiles * tm), F32),
        grid=(n_tiles,),
        in_specs=[pl.BlockSpec((tm, k), lambda i: (first_tile + i, 0)),
                  pl.BlockSpec((1, k), lambda i: (0, 0)),
                  pl.BlockSpec((1, 1, k), grp),
                  pl.BlockSpec((1, 1, k), grp),
                  pl.BlockSpec((nch, k), lambda i: (0, 0)),
                  pl.BlockSpec((nch, 3), lambda i: (0, 0))],
        out_specs=pl.BlockSpec((nch, tm), lambda i: (0, i)),
        compiler_params=pltpu.CompilerParams(
            dimension_semantics=("arbitrary",), vmem_limit_bytes=VMEM_LIMIT_BYTES),
        name="hy_inproj",
    )(xs, gain.reshape(1, k), shift3, scale3, w_in.T.astype(BF16), conv_w.T)


def _hy_out_kernel(zl_ref, zc_ref, w_ref, x_ref, gate_ref, o_ref, *, n_lat_tiles):
    z = jnp.where(pl.program_id(0) >= n_lat_tiles, zc_ref[...], zl_ref[...])
    o_ref[...] = x_ref[...] + gate_ref[0] * _dotb_tn(z, w_ref[...])


def _hy_out(z_lat, z_ctx, w_out, xs, gate3, blocks_per_batch):
    t, d = xs.shape
    tm = HY_TOK_TILE
    n_lat_tiles = z_lat.shape[1] // tm
    per_tile = tm // DN_BLOCK
    grp = lambda i: (_group_of_block(i * per_tile, n_lat_tiles * per_tile, blocks_per_batch), 0, 0)
    return pl.pallas_call(
        functools.partial(_hy_out_kernel, n_lat_tiles=n_lat_tiles),
        out_shape=jax.ShapeDtypeStruct((t, d), F32),
        grid=(t // tm,),
        in_specs=[pl.BlockSpec((d, tm), lambda i: (0, jnp.minimum(i, n_lat_tiles - 1))),
                  pl.BlockSpec((d, tm), lambda i: (0, 0)),
                  pl.BlockSpec((d, d), lambda i: (0, 0)),
                  pl.BlockSpec((tm, d), lambda i: (i, 0)),
                  pl.BlockSpec((1, 1, d), grp)],
        out_specs=pl.BlockSpec((tm, d), lambda i: (i, 0)),
        compiler_params=pltpu.CompilerParams(
            dimension_semantics=("arbitrary",), vmem_limit_bytes=VMEM_LIMIT_BYTES),
        name="hy_out",
    )(z_lat, z_ctx, w_out.astype(BF16), xs, gate3)


def _hyena_layer(xs, gain, shift3, scale3, w_in, conv_w, f_w1, f_b1, f_freq, f_w2, f_b2, f_w3, bias, w_out,
                 gate3, bsz, seq):
    n_lat_tiles = bsz * seq // HY_TOK_TILE
    assert bsz * CTX_LEN == HY_TOK_TILE
    margs = (xs, gain, shift3, scale3, w_in, conv_w)
    blocks = (bsz * seq // DN_BLOCK, seq // DN_BLOCK)
    p_lat = _hy_inproj(*margs, 0, n_lat_tiles, GRID_W, *blocks)
    p_ctx = _hy_inproj(*margs, n_lat_tiles, 1, CTX_LEN, *blocks)
    fargs = (f_w1, f_b1, f_freq, f_w2, f_b2, f_w3)
    consts = _dft_constants(seq // HY_N2)
    khat = _hy_spectrum(_hy_filter(seq, *fargs), consts)
    filt_ctx = _hy_filter(CTX_LEN, *fargs)
    z_lat, z_ctx = p_lat, p_ctx
    for n in range(HY_ORDER):
        z_lat = _hy_conv(z_lat, 0, p_lat, n + 1, khat, n, bias[n], consts, bsz)
        z_ctx = _hy_ctx(z_ctx, 0, p_ctx, n + 1, filt_ctx, n, bias[n], bsz)
    return _hy_out(z_lat, z_ctx, w_out, xs, gate3, seq // DN_BLOCK)


MOE_TILE = 512
PACK = 2


def _route_kernel(x_ref, gain_ref, shift_ref, scale_ref, wr_ref, rb_ref, tri_ref,
                  f_ref, e_ref, w_ref, r_ref, cnt_ref, carry_ref):
    tm, d = x_ref.shape
    ne, epg, ng = N_EXPERTS, EXPERTS_PER_GROUP, N_GROUPS

    @pl.when(pl.program_id(0) == 0)
    def _():
        carry_ref[...] = jnp.zeros_like(carry_ref)

    x = x_ref[...]
    y = x * lax.rsqrt(jnp.mean(x * x, axis=-1, keepdims=True) + EPS) * gain_ref[...]
    f = y * (1 + scale_ref[0]) + shift_ref[0]
    bits = pltpu.bitcast(f.astype(BF16).astype(F32), jnp.uint32)
    half = d // PACK
    f_ref[...] = (bits[:, :half] >> 16) | (bits[:, half:] & jnp.uint32(0xFFFF0000))

    logits = lax.dot_general(wr_ref[...], f, (((1,), (1,)), ((), ())),
                             precision=lax.Precision.HIGHEST, preferred_element_type=F32)
    scores = jax.nn.sigmoid(logits)
    biased = scores + rb_ref[...]
    s = [scores[j * ng:(j + 1) * ng] for j in range(epg)]
    c = [biased[j * ng:(j + 1) * ng] for j in range(epg)]
    hi01, lo01 = jnp.maximum(c[0], c[1]), jnp.minimum(c[0], c[1])
    hi23, lo23 = jnp.maximum(c[2], c[3]), jnp.minimum(c[2], c[3])
    gscore = jnp.maximum(hi01, hi23) + jnp.maximum(jnp.minimum(hi01, hi23), jnp.maximum(lo01, lo23))
    gi = lax.broadcasted_iota(jnp.int32, (ng, tm), 0)
    gmax = jnp.max(gscore, axis=0, keepdims=True)
    grp = jnp.min(jnp.where(gscore == gmax, gi, ng), axis=0, keepdims=True)
    sel = gi == grp
    cv = [jnp.sum(jnp.where(sel, t, 0.0), axis=0, keepdims=True) for t in c]
    sv = [jnp.sum(jnp.where(sel, t, 0.0), axis=0, keepdims=True) for t in s]

    def pick(excluded):
        best = jnp.full((1, tm), -jnp.inf, F32)
        idx = jnp.zeros((1, tm), jnp.int32)
        val = jnp.zeros((1, tm), F32)
        for j in range(epg):
            cand = cv[j] if excluded is None else jnp.where(excluded == j, -jnp.inf, cv[j])
            take = cand > best
            best = jnp.where(take, cand, best)
            idx = jnp.where(take, j, idx)
            val = jnp.where(take, sv[j], val)
        return idx, val

    i1, v1 = pick(None)
    i2, v2 = pick(i1)
    e1 = grp * epg + i1
    e2 = grp * epg + i2
    wsum = v1 + v2
    e_ref[0:1, :] = e1
    e_ref[1:2, :] = e2
    w_ref[0:1, :] = v1 / wsum
    w_ref[1:2, :] = v2 / wsum

    ei = lax.broadcasted_iota(jnp.int32, (ne, tm), 0)
    oh1 = ei == e1
    oh2 = ei == e2
    tri = tri_ref[...]
    pre1 = jnp.dot(oh1.astype(BF16), tri, preferred_element_type=F32)
    pre2 = jnp.dot(oh2.astype(BF16), tri, preferred_element_type=F32)
    tot1 = pre1[:, tm - 1:tm]
    tot2 = pre2[:, tm - 1:tm]
    carry = carry_ref[:, 0:1]
    r1 = jnp.sum(jnp.where(oh1, carry + pre1 - 1.0, 0.0), axis=0, keepdims=True)
    r2 = jnp.sum(jnp.where(oh2, carry + tot1 + pre2 - 1.0, 0.0), axis=0, keepdims=True)
    r_ref[0:1, :] = r1.astype(jnp.int32)
    r_ref[1:2, :] = r2.astype(jnp.int32)
    carry_ref[...] = carry_ref[...] + (tot1 + tot2)
    cnt_ref[...] = carry_ref[...]


def _moe_route(xs, gain, shift3, scale3, w_router, router_bias, n_lat_tiles, blocks_per_batch):
    t, d = xs.shape
    tm = MOE_TILE
    per_tile = tm // DN_BLOCK
    row = lambda i: (i, 0)
    col = lambda i: (0, i)
    fixed = lambda i: (0, 0)
    grp = lambda i: (_group_of_block(i * per_tile, n_lat_tiles * per_tile, blocks_per_batch), 0, 0)
    tri = jnp.asarray(np.triu(np.ones((tm, tm), np.float32)), BF16)
    perm = np.arange(N_EXPERTS).reshape(N_GROUPS, EXPERTS_PER_GROUP).T.reshape(-1)
    return pl.pallas_call(
        _route_kernel,
        out_shape=[jax.ShapeDtypeStruct((t, d // PACK), jnp.uint32),
                   jax.ShapeDtypeStruct((TOP_K, t), jnp.int32),
                   jax.ShapeDtypeStruct((TOP_K, t), F32),
                   jax.ShapeDtypeStruct((TOP_K, t), jnp.int32),
                   jax.ShapeDtypeStruct((N_EXPERTS, 128), F32)],
        grid=(t // tm,),
        in_specs=[pl.BlockSpec((tm, d), row),
                  pl.BlockSpec((1, d), fixed),
                  pl.BlockSpec((1, 1, d), grp),
                  pl.BlockSpec((1, 1, d), grp),
                  pl.BlockSpec((N_EXPERTS, d), fixed),
                  pl.BlockSpec((N_EXPERTS, 1), fixed),
                  pl.BlockSpec((tm, tm), fixed)],
        out_specs=[pl.BlockSpec((tm, d // PACK), row),
                   pl.BlockSpec((TOP_K, tm), col),
                   pl.BlockSpec((TOP_K, tm), col),
                   pl.BlockSpec((TOP_K, tm), col),
                   pl.BlockSpec((N_EXPERTS, 128), fixed)],
        scratch_shapes=[pltpu.VMEM((N_EXPERTS, 128), F32)],
        compiler_params=pltpu.CompilerParams(
            dimension_semantics=("arbitrary",), vmem_limit_bytes=VMEM_LIMIT_BYTES),
        name="moe_route",
    )(xs, gain.reshape(1, d), shift3, scale3, w_router.T[perm], router_bias[perm].reshape(N_EXPERTS, 1), tri)


def _row_copy(src, src_row, dst, dst_row, sem):
    return pltpu.make_async_copy(src.at[pl.ds(src_row, 1)], dst.at[pl.ds(dst_row, 1)], sem)


def _dispatch_kernel(dest_ref, f_ref, xs_in_ref, xs_ref, dest_smem, sem, idx_sem):
    del xs_in_ref
    tm = f_ref.shape[0]
    idx_copy = pltpu.make_async_copy(dest_ref, dest_smem, idx_sem)
    idx_copy.start()
    idx_copy.wait()

    def issue(tok, carry):
        for k in range(TOP_K):
            _row_copy(f_ref, tok, xs_ref, dest_smem[k, tok], sem).start(priority=k)
        return carry

    def drain(tok, carry):
        for k in range(TOP_K):
            _row_copy(f_ref, 0, xs_ref, 0, sem).wait()
        return carry

    lax.fori_loop(0, tm, issue, 0, unroll=8)
    lax.fori_loop(0, tm, drain, 0, unroll=8)


def _dispatch(f_packed, dest, n_slots):
    t, wd = f_packed.shape
    tm = MOE_TILE
    return pl.pallas_call(
        _dispatch_kernel,
        out_shape=jax.ShapeDtypeStruct((n_slots, wd), jnp.uint32),
        grid=(t // tm,),
        in_specs=[pl.BlockSpec((TOP_K, tm), lambda i: (0, i)),
                  pl.BlockSpec((tm, wd), lambda i: (i, 0)),
                  pl.BlockSpec(memory_space=pl.ANY)],
        out_specs=pl.BlockSpec(memory_space=pl.ANY),
        scratch_shapes=[pltpu.SMEM((TOP_K, tm), jnp.int32),
                        pltpu.SemaphoreType.DMA, pltpu.SemaphoreType.DMA],
        input_output_aliases={2: 0},
        compiler_params=pltpu.CompilerParams(
            dimension_semantics=("arbitrary",), vmem_limit_bytes=VMEM_LIMIT_BYTES),
        name="moe_dispatch",
    )(dest, f_packed, jnp.zeros((n_slots, wd), jnp.uint32))


def _experts_kernel(be_ref, na_ref, x_ref, wg_ref, wu_ref, wd_ref, o_ref, wgb_ref, wub_ref, wdb_ref):
    i = pl.program_id(0)
    prev = be_ref[jnp.maximum(i - 1, 0)]

    @pl.when((i == 0) | (be_ref[i] != prev))
    def _():
        wgb_ref[...] = wg_ref[0].astype(BF16)
        wub_ref[...] = wu_ref[0].astype(BF16)
        wdb_ref[...] = wd_ref[0].astype(BF16)

    @pl.when(i < na_ref[0])
    def _():
        packed = x_ref[...]
        lo = pltpu.bitcast(packed << 16, F32)
        hi = pltpu.bitcast(packed & jnp.uint32(0xFFFF0000), F32)
        xb = jnp.concatenate([lo, hi], axis=-1).astype(BF16)
        g = jnp.dot(xb, wgb_ref[...], preferred_element_type=F32)
        u = jnp.dot(xb, wub_ref[...], preferred_element_type=F32)
        hid = (g * jax.nn.sigmoid(g)) * u
        o_ref[...] = jnp.dot(hid.astype(BF16), wdb_ref[...], preferred_element_type=F32)

    @pl.when(i >= na_ref[0])
    def _():
        o_ref[...] = jnp.zeros_like(o_ref)


def _experts(xs_sorted, block_expert, n_active, w_gate, w_up, w_down):
    n_slots, wd = xs_sorted.shape
    d = wd * PACK
    n_blocks = n_slots // MOE_BLOCK
    blk = lambda i, be, na: (jnp.minimum(i, na[0] - 1), 0)
    wsel = lambda i, be, na: (be[jnp.minimum(i, na[0] - 1)], 0, 0)
    grid_spec = pltpu.PrefetchScalarGridSpec(
        num_scalar_prefetch=2,
        grid=(n_blocks,),
        in_specs=[pl.BlockSpec((MOE_BLOCK, wd), blk),
                  pl.BlockSpec((1, d, D_EXPERT), wsel),
                  pl.BlockSpec((1, d, D_EXPERT), wsel),
                  pl.BlockSpec((1, D_EXPERT, d), wsel)],
        out_specs=pl.BlockSpec((MOE_BLOCK, d), lambda i, be, na: (i, 0)),
        scratch_shapes=[pltpu.VMEM((d, D_EXPERT), BF16), pltpu.VMEM((d, D_EXPERT), BF16),
                        pltpu.VMEM((D_EXPERT, d), BF16)],
    )
    return pl.pallas_call(
        _experts_kernel,
        out_shape=jax.ShapeDtypeStruct((n_slots, d), F32),
        grid_spec=grid_spec,
        compiler_params=pltpu.CompilerParams(
            dimension_semantics=("arbitrary",), vmem_limit_bytes=VMEM_LIMIT_BYTES),
        name="moe_experts",
    )(block_expert, n_active, xs_sorted, w_gate, w_up, w_down)


def _combine_kernel(dest_ref, y_ref, x_ref, w_ref, gate_ref, o_ref, dest_smem, ya_ref, yb_ref, sem, idx_sem):
    tm = x_ref.shape[0]
    idx_copy = pltpu.make_async_copy(dest_ref, dest_smem, idx_sem)
    idx_copy.start()
    idx_copy.wait()
    bufs = (ya_ref, yb_ref)

    def issue(tok, carry):
        for k in range(TOP_K):
            _row_copy(y_ref, dest_smem[k, tok], bufs[k], tok, sem).start(priority=k)
        return carry

    def drain(tok, carry):
        for k in range(TOP_K):
            _row_copy(y_ref, 0, bufs[k], 0, sem).wait()
        return carry

    lax.fori_loop(0, tm, issue, 0, unroll=8)
    lax.fori_loop(0, tm, drain, 0, unroll=8)
    w = w_ref[...]
    o_ref[...] = x_ref[...] + gate_ref[0] * (w[:, 0:1] * ya_ref[...] + w[:, 1:2] * yb_ref[...])


def _combine(ys, dest, weight_cols, xs, gate3, n_lat_tiles, blocks_per_batch):
    t, d = xs.shape
    tm = MOE_TILE
    per_tile = tm // DN_BLOCK
    row = lambda i: (i, 0)
    grp = lambda i: (_group_of_block(i * per_tile, n_lat_tiles * per_tile, blocks_per_batch), 0, 0)
    return pl.pallas_call(
        _combine_kernel,
        out_shape=jax.ShapeDtypeStruct((t, d), F32),
        grid=(t // tm,),
        in_specs=[pl.BlockSpec((TOP_K, tm), lambda i: (0, i)),
                  pl.BlockSpec(memory_space=pl.ANY),
                  pl.BlockSpec((tm, d), row),
                  pl.BlockSpec((tm, TOP_K), row),
                  pl.BlockSpec((1, 1, d), grp)],
        out_specs=pl.BlockSpec((tm, d), row),
        scratch_shapes=[pltpu.SMEM((TOP_K, tm), jnp.int32),
                        pltpu.VMEM((tm, d), F32), pltpu.VMEM((tm, d), F32),
                        pltpu.SemaphoreType.DMA, pltpu.SemaphoreType.DMA],
        compiler_params=pltpu.CompilerParams(
            dimension_semantics=("arbitrary",), vmem_limit_bytes=VMEM_LIMIT_BYTES),
        name="moe_combine",
    )(dest, ys, xs, weight_cols, gate3)


def _moe_layer(xs, gain, shift3, scale3, gate3, w_router, router_bias, w_gate, w_up, w_down,
               n_lat_tiles, blocks_per_batch):
    t = xs.shape[0]
    f_packed, expert, weight, rank, counts = _moe_route(xs, gain, shift3, scale3, w_router, router_bias,
                                                        n_lat_tiles, blocks_per_batch)
    counts = counts[:, 0].astype(jnp.int32)
    padded = (counts + MOE_BLOCK - 1) // MOE_BLOCK * MOE_BLOCK
    pend = jnp.cumsum(padded)
    pstart = pend - padded
    n_blocks = -(-(t * TOP_K) // MOE_BLOCK) + N_EXPERTS
    block_start = jnp.arange(n_blocks, dtype=jnp.int32) * MOE_BLOCK
    block_expert = jnp.minimum(jnp.sum(pend[None, :] <= block_start[:, None], axis=1),
                               N_EXPERTS - 1).astype(jnp.int32)
    n_active = (pend[-1:] // MOE_BLOCK).astype(jnp.int32)
    is_e = expert[..., None] == jnp.arange(N_EXPERTS, dtype=jnp.int32)
    dest = rank + jnp.sum(jnp.where(is_e, pstart, 0), axis=-1)
    xs_sorted = _dispatch(f_packed, dest, n_blocks * MOE_BLOCK)
    ys = _experts(xs_sorted, block_expert, n_active, w_gate, w_up, w_down)
    return _combine(ys, dest, weight.T, xs, gate3, n_lat_tiles, blocks_per_batch)


def _rmsnorm(x, gain):
    y = x * lax.rsqrt(jnp.mean(x * x, axis=-1, keepdims=True) + EPS)
    return y * gain


def _modulate(x, gain, shift, scale):
    return _rmsnorm(x, gain) * (1 + scale) + shift


def _l2norm(t):
    return t * lax.rsqrt(jnp.sum(t * t, axis=-1, keepdims=True) + EPS)


def _short_conv(x, w, on_grid):
    b, l, ch = x.shape
    xs = x.reshape(b, l // GRID_W, GRID_W, ch) if on_grid else x.reshape(b, 1, l, ch)
    n = xs.shape[2]
    xp = jnp.pad(xs, ((0, 0), (0, 0), (1, 1), (0, 0)))
    y = w[0] * xp[:, :, 0:n] + w[1] * xp[:, :, 1:n + 1] + w[2] * xp[:, :, 2:n + 2]
    return y.reshape(b, l, ch)


def _gated_delta_chunked(q, k, v, g, beta, s0):
    b, h, l, dk = q.shape
    dv = v.shape[-1]
    c = DN_CHUNK
    n = l // c
    q = q.reshape(b, h, n, c, dk)
    k = k.reshape(b, h, n, c, dk)
    v = v.reshape(b, h, n, c, dv)
    g = jnp.cumsum(g.reshape(b, h, n, c), axis=-1)
    beta = beta.reshape(b, h, n, c, 1)
    pos = jnp.arange(c)
    incl = pos[:, None] >= pos[None, :]
    strict = pos[:, None] > pos[None, :]
    decay = jnp.exp(jnp.where(incl, g[..., :, None] - g[..., None, :], -jnp.inf))
    kb = k * beta
    a_mat = jnp.einsum('bhnid,bhnjd->bhnij', kb, k) * jnp.where(strict, decay, 0.0)
    rhs = jnp.concatenate([v * beta, kb * jnp.exp(g)[..., None]], axis=-1)
    sol = lax.linalg.triangular_solve(a_mat + jnp.eye(c, dtype=a_mat.dtype), rhs,
                                      left_side=True, lower=True, unit_diagonal=True)
    u, w = sol[..., :dv], sol[..., dv:]
    attn = jnp.einsum('bhnid,bhnjd->bhnij', q, k) * decay
    g_last = g[..., -1:]
    q_dec = q * jnp.exp(g)[..., None]
    k_dec = k * jnp.exp(g_last - g)[..., None]

    def step(s, inp):
        qd, kd, uu, ww, at, gl = inp
        v_new = uu - jnp.einsum('bhck,bhkv->bhcv', ww, s)
        o = jnp.einsum('bhck,bhkv->bhcv', qd, s) + jnp.einsum('bhcs,bhsv->bhcv', at, v_new)
        s = s * jnp.exp(gl)[..., None] + jnp.einsum('bhck,bhcv->bhkv', kd, v_new)
        return s, o

    xs = tuple(jnp.moveaxis(t, 2, 0) for t in (q_dec, k_dec, u, w, attn, g_last))
    s_final, o = lax.scan(step, s0, xs)
    o = jnp.moveaxis(o, 0, 2).reshape(b, h, l, dv)
    return o, s_final


def _deltanet_mixer(p_ctx, p_lat, conv_w, a_log, dt_bias, out_norm):
    d = D_MODEL
    nh = DN_HEADS

    def project(p, on_grid):
        b, l, _ = p.shape
        qkv = jax.nn.silu(_short_conv(p[..., :3 * d], conv_w, on_grid))
        z = p[..., 3 * d:4 * d]
        a = p[..., 4 * d:4 * d + 2 * nh].reshape(b, l, 2, nh)
        bb = p[..., 4 * d + 2 * nh:].reshape(b, l, 2, nh)

        def heads(t):
            return jnp.transpose(t.reshape(b, l, nh, -1), (0, 2, 1, 3))

        q, k, v = (heads(t) for t in jnp.split(qkv, 3, axis=-1))
        q = _l2norm(q) * DN_DK ** -0.5
        k = _l2norm(k)
        g = -jnp.exp(a_log) * jax.nn.softplus(a + dt_bias)
        g = jnp.transpose(g, (2, 0, 3, 1))
        beta = jnp.transpose(jax.nn.sigmoid(bb), (2, 0, 3, 1))
        return q, k, v, g, beta, z

    def scan_both(q, k, v, g, beta, s_f, s_b):
        o_f, s_f = _gated_delta_chunked(q, k, v, g[0], beta[0], s_f)
        rev = lambda t: jnp.flip(t, axis=2)
        o_b, s_b = _gated_delta_chunked(rev(q), rev(k), rev(v), rev(g[1]), rev(beta[1]), s_b)
        return o_f + rev(o_b), s_f, s_b

    def finish(o, z):
        b, _, l, _ = o.shape
        o = jnp.transpose(o, (0, 2, 1, 3))
        o = o * lax.rsqrt(jnp.mean(o * o, axis=-1, keepdims=True) + EPS) * out_norm
        o = o * jax.nn.silu(z.reshape(b, l, nh, DN_DV))
        return o.reshape(b, l, d)

    qc, kc, vc, gc, bc, zc = project(p_ctx, False)
    s0 = jnp.zeros((p_ctx.shape[0], nh, DN_DK, DN_DV), F32)
    o_c, s_f, s_b = scan_both(qc, kc, vc, gc, bc, s0, s0)
    ql, kl, vl, gla, bl, zl = project(p_lat, True)
    o_l, _, _ = scan_both(ql, kl, vl, gla, bl, s_f, s_b)
    return finish(o_c, zc), finish(o_l, zl)


def _hyena_filters(l, w1, b1, freq, w2, b2, w3):
    pos = jnp.arange(l, dtype=F32)[:, None]
    t = pos / max(l - 1, 1)
    bands = jnp.linspace(1e-4, HY_BANDS - 1, HY_BANDS, dtype=F32)[None, :]
    ang = (2 * math.pi / l) * pos * bands
    feat = jnp.concatenate([t, jnp.cos(ang), -jnp.sin(ang)], axis=-1)
    hp = lax.Precision.HIGHEST
    hdn = jnp.sin(freq * (jnp.dot(feat, w1, precision=hp) + b1))
    hdn = jnp.sin(freq * (jnp.dot(hdn, w2, precision=hp) + b2))
    filt = jnp.dot(hdn, w3, precision=hp).reshape(l, HY_ORDER, 2, D_MODEL)
    deltas = jnp.abs(jnp.linspace(math.log(HY_TARGET) / HY_SLOW, math.log(HY_TARGET) / HY_FAST,
                                  D_MODEL, dtype=F32))
    window = jnp.exp(-t * deltas[None, :])
    return filt * window[:, None, None, :]


def _two_sided_fftconv(u, h_fwd, h_bwd):
    l = u.shape[1]
    k = jnp.concatenate([h_fwd, jnp.zeros_like(h_fwd[:1]), jnp.flip(h_bwd[1:], axis=0)], axis=0)
    kf = jnp.fft.rfft(k, axis=0)
    uf = jnp.fft.rfft(u, n=2 * l, axis=1)
    return jnp.fft.irfft(uf * kf[None], n=2 * l, axis=1)[:, :l]


def _hyena_stream(p, on_grid, conv_w, f_w1, f_b1, f_freq, f_w2, f_b2, f_w3, bias):
    l = p.shape[1]
    p = _short_conv(p, conv_w, on_grid)
    v, x1, x2 = jnp.split(p, 3, axis=-1)
    filt = _hyena_filters(l, f_w1, f_b1, f_freq, f_w2, f_b2, f_w3)
    z = v
    for n, gate in enumerate((x1, x2)):
        conv = _two_sided_fftconv(z, filt[:, n, 0], filt[:, n, 1])
        z = gate * (conv + bias[n] * z)
    return z


def _shortconv_stream(p, on_grid, conv_w):
    bg, cg, xin = jnp.split(p, 3, axis=-1)
    return bg * _short_conv(cg * xin, conv_w, on_grid)


def _route(h, w_router, router_bias):
    t = h.shape[0]
    scores = jax.nn.sigmoid(jnp.dot(h, w_router, precision=lax.Precision.HIGHEST))
    choice = (scores + router_bias).reshape(t, N_GROUPS, EXPERTS_PER_GROUP)
    group_score = lax.top_k(choice, GROUP_SCORE_K)[0].sum(-1)
    group = jnp.argmax(group_score, axis=-1)
    in_group = jnp.take_along_axis(choice, group[:, None, None], axis=1)[:, 0]
    local = lax.top_k(in_group, TOP_K)[1]
    expert = group[:, None] * EXPERTS_PER_GROUP + local
    weight = jnp.take_along_axis(scores, expert, axis=1)
    weight = weight / jnp.sum(weight, axis=-1, keepdims=True)
    return expert.astype(jnp.int32), weight


def _moe_ffn(x, w_router, router_bias, w_gate, w_up, w_down):
    t, d = x.shape
    expert, weight = _route(x, w_router, router_bias)
    a = t * TOP_K
    e_flat = expert.reshape(-1)
    order = jnp.argsort(e_flat)
    e_sorted = e_flat[order]
    tok_sorted = (order // TOP_K).astype(jnp.int32)
    counts = jnp.zeros((N_EXPERTS,), jnp.int32).at[e_flat].add(1)
    start = jnp.cumsum(counts) - counts
    padded = (counts + MOE_BLOCK - 1) // MOE_BLOCK * MOE_BLOCK
    pend = jnp.cumsum(padded)
    pstart = pend - padded
    dest = pstart[e_sorted] + (jnp.arange(a, dtype=jnp.int32) - start[e_sorted])
    n_blocks = -(-a // MOE_BLOCK) + N_EXPERTS
    n_slots = n_blocks * MOE_BLOCK
    slot_tok = jnp.full((n_slots,), t, jnp.int32).at[dest].set(tok_sorted)
    block_start = jnp.arange(n_blocks, dtype=jnp.int32) * MOE_BLOCK
    block_expert = jnp.minimum(jnp.searchsorted(pend, block_start, side='right'),
                               N_EXPERTS - 1).astype(jnp.int32)
    x_pad = jnp.concatenate([x.astype(BF16), jnp.zeros((1, d), BF16)], axis=0)
    xs = x_pad[slot_tok]
    ys = _expert_ffn(xs, block_expert, jnp.ones((n_slots,), F32), w_gate, w_up, w_down)
    slot_of = jnp.zeros((a,), jnp.int32).at[order].set(dest).reshape(t, TOP_K)
    out = weight[:, 0:1] * ys[slot_of[:, 0]] + weight[:, 1:2] * ys[slot_of[:, 1]]
    return out


def _sc_layer_kernel(x_ref, gain_ref, shift_ref, scale_ref, win_ref, cw_ref, wout_ref, gate_ref, o_ref, *,
                     n_lat_tiles):
    d = D_MODEL
    tm = x_ref.shape[0]
    seg = jnp.where(pl.program_id(0) >= n_lat_tiles, CTX_LEN, GRID_W)
    pos = lax.broadcasted_iota(jnp.int32, (tm, 1), 0) & (seg - 1)
    hb = _modulated(x_ref[...], gain_ref[...], shift_ref[0], scale_ref[0]).astype(BF16)
    u = (jnp.dot(hb, win_ref[:, d:2 * d], preferred_element_type=F32)
         * jnp.dot(hb, win_ref[:, 2 * d:], preferred_element_type=F32))
    prev = jnp.where(pos != 0, pltpu.roll(u, 1, axis=0), 0.0)
    nxt = jnp.where(pos != seg - 1, pltpu.roll(u, tm - 1, axis=0), 0.0)
    cw = cw_ref[...]
    y = jnp.dot(hb, win_ref[:, :d], preferred_element_type=F32) * (
        cw[0:1] * prev + cw[1:2] * u + cw[2:3] * nxt)
    o_ref[...] = x_ref[...] + gate_ref[0] * jnp.dot(y.astype(BF16), wout_ref[...],
                                                    preferred_element_type=F32)


def _shortconv_layer(xs, gain, shift3, scale3, w_in, conv_w, w_out, gate3, n_lat_tiles, blocks_per_batch):
    t, d = xs.shape
    tm = HY_TOK_TILE
    per_tile = tm // DN_BLOCK
    row = lambda i: (i, 0)
    fixed = lambda i: (0, 0)
    grp = lambda i: (_group_of_block(i * per_tile, n_lat_tiles * per_tile, blocks_per_batch), 0, 0)
    return pl.pallas_call(
        functools.partial(_sc_layer_kernel, n_lat_tiles=n_lat_tiles),
        out_shape=jax.ShapeDtypeStruct((t, d), F32),
        grid=(t // tm,),
        in_specs=[pl.BlockSpec((tm, d), row),
                  pl.BlockSpec((1, d), fixed),
                  pl.BlockSpec((1, 1, d), grp),
                  pl.BlockSpec((1, 1, d), grp),
                  pl.BlockSpec((d, 3 * d), fixed),
                  pl.BlockSpec((3, d), fixed),
                  pl.BlockSpec((d, d), fixed),
                  pl.BlockSpec((1, 1, d), grp)],
        out_specs=pl.BlockSpec((tm, d), row),
        compiler_params=pltpu.CompilerParams(
            dimension_semantics=("arbitrary",), vmem_limit_bytes=VMEM_LIMIT_BYTES),
        name="shortconv_layer",
    )(xs, gain.reshape(1, d), shift3, scale3, w_in.astype(BF16), conv_w, w_out.astype(BF16), gate3)


def kernel(x, c, ctx, c_ctx, ada_w, ada_b, norm_mix, norm_ffn, norm_final, dn_w_in, dn_conv, dn_a_log,
           dn_dt_bias, dn_out_norm, dn_w_out, hy_w_in, hy_conv, hy_f_w1, hy_f_b1, hy_f_freq, hy_f_w2,
           hy_f_b2, hy_f_w3, hy_bias, hy_w_out, sc_w_in, sc_conv, sc_w_out, w_router, router_bias,
           moe_w_gate, moe_w_up, moe_w_down):
    d = D_MODEL
    bsz, seq, _ = x.shape
    n_ctx = bsz * CTX_LEN
    n_lat = bsz * seq
    silu_c = jax.nn.silu(c)
    silu_cc = jax.nn.silu(c_ctx)
    hp = lax.Precision.HIGHEST

    xs = jnp.concatenate([x.reshape(n_lat, d), ctx.reshape(n_ctx, d)], axis=0)
    n_lat_blocks, blocks_per_batch = n_lat // DN_BLOCK, seq // DN_BLOCK

    for i in range(DEPTH):
        kind, j = i % N_MIXERS, i // N_MIXERS
        ml = jnp.split(jnp.dot(silu_c, ada_w[i], precision=hp) + ada_b[i], N_MOD, axis=-1)
        mc = jnp.split(jnp.dot(silu_cc, ada_w[i], precision=hp) + ada_b[i], N_MOD, axis=-1)
        mod = [jnp.concatenate([mc[m][None], ml[m]], axis=0)[:, None, :] for m in range(N_MOD)]
        if kind == 0:
            xs = _deltanet_layer(xs, norm_mix[i], mod[0], mod[1], dn_w_in[j], dn_conv[j], dn_a_log[j],
                                 dn_dt_bias[j], dn_out_norm[j], dn_w_out[j], mod[2], n_lat_blocks,
                                 blocks_per_batch)
        elif kind == 1:
            xs = _hyena_layer(xs, norm_mix[i], mod[0], mod[1], hy_w_in[j], hy_conv[j], hy_f_w1[j], hy_f_b1[j],
                              hy_f_freq[j], hy_f_w2[j], hy_f_b2[j], hy_f_w3[j], hy_bias[j], hy_w_out[j],
                              mod[2], bsz, seq)
        else:
            xs = _shortconv_layer(xs, norm_mix[i], mod[0], mod[1], sc_w_in[j], sc_conv[j], sc_w_out[j],
                                  mod[2], n_lat // HY_TOK_TILE, blocks_per_batch)
        xs = _moe_layer(xs, norm_ffn[i], mod[3], mod[4], mod[5], w_router, router_bias,
                        moe_w_gate[i], moe_w_up[i], moe_w_down[i], n_lat // MOE_TILE, blocks_per_batch)
    x_out = xs[:n_lat].reshape(bsz, seq, d)
    return _rmsnorm(x_out, norm_final)
```

```python
import functools
import math

import numpy as np
import jax
import jax.numpy as jnp
from jax import lax
from jax.experimental import pallas as pl
from jax.experimental.pallas import tpu as pltpu

D_MODEL = 1024
DEPTH = 4
CTX_LEN = 256
GRID_W = 64
N_MIXERS = 3
EPS = 1e-6
N_MOD = 6

DN_HEADS = 8
DN_DK = D_MODEL // DN_HEADS
DN_DV = D_MODEL // DN_HEADS
DN_CHUNK = 64

HY_ORDER = 2
HY_BANDS = 16
HY_TARGET = 1e-2
HY_FAST = 0.3
HY_SLOW = 1.5

N_EXPERTS = 32
N_GROUPS = 8
EXPERTS_PER_GROUP = N_EXPERTS // N_GROUPS
GROUP_SCORE_K = 2
TOP_K = 2
D_EXPERT = 512
MOE_BLOCK = 512

F32 = jnp.float32
BF16 = jnp.bfloat16

ROW_TILE = 512
VMEM_LIMIT_BYTES = 48 * 1024 * 1024


def _mm_kernel(x_ref, w_ref, o_ref):
    o_ref[...] = jnp.dot(x_ref[...].astype(BF16), w_ref[...], preferred_element_type=F32)


def _mm(x, w, tn=None):
    m, k = x.shape
    n = w.shape[1]
    tm = min(ROW_TILE, m)
    tn = n if tn is None else tn
    assert m % tm == 0 and n % tn == 0
    return pl.pallas_call(
        _mm_kernel,
        out_shape=jax.ShapeDtypeStruct((m, n), F32),
        grid=(m // tm, n // tn),
        in_specs=[pl.BlockSpec((tm, k), lambda i, j: (i, 0)),
                  pl.BlockSpec((k, tn), lambda i, j: (0, j))],
        out_specs=pl.BlockSpec((tm, tn), lambda i, j: (i, j)),
        compiler_params=pltpu.CompilerParams(
            dimension_semantics=("arbitrary", "arbitrary"), vmem_limit_bytes=VMEM_LIMIT_BYTES),
        name="dense_mm",
    )(x, w.astype(BF16))


def _expert_kernel(be_ref, x_ref, wg_ref, wu_ref, wd_ref, sw_ref, o_ref):
    del be_ref
    xb = x_ref[...]
    g = jnp.dot(xb, wg_ref[0], preferred_element_type=F32)
    u = jnp.dot(xb, wu_ref[0], preferred_element_type=F32)
    hid = (g * jax.nn.sigmoid(g)) * u
    y = jnp.dot(hid.astype(BF16), wd_ref[0], preferred_element_type=F32)
    o_ref[...] = y * sw_ref[...]


def _expert_ffn(xs, block_expert, slot_w, w_gate, w_up, w_down):
    n_slots, d = xs.shape
    n_blocks = n_slots // MOE_BLOCK
    grid_spec = pltpu.PrefetchScalarGridSpec(
        num_scalar_prefetch=1,
        grid=(n_blocks,),
        in_specs=[
            pl.BlockSpec((MOE_BLOCK, d), lambda i, be: (i, 0)),
            pl.BlockSpec((1, d, D_EXPERT), lambda i, be: (be[i], 0, 0)),
            pl.BlockSpec((1, d, D_EXPERT), lambda i, be: (be[i], 0, 0)),
            pl.BlockSpec((1, D_EXPERT, d), lambda i, be: (be[i], 0, 0)),
            pl.BlockSpec((MOE_BLOCK, 1), lambda i, be: (i, 0)),
        ],
        out_specs=pl.BlockSpec((MOE_BLOCK, d), lambda i, be: (i, 0)),
    )
    return pl.pallas_call(
        _expert_kernel,
        out_shape=jax.ShapeDtypeStruct((n_slots, d), F32),
        grid_spec=grid_spec,
        compiler_params=pltpu.CompilerParams(
            dimension_semantics=("arbitrary",), vmem_limit_bytes=VMEM_LIMIT_BYTES),
        name="expert_ffn",
    )(block_expert, xs, w_gate.astype(BF16), w_up.astype(BF16), w_down.astype(BF16),
      slot_w.reshape(n_slots, 1))


DN_BLOCK = CTX_LEN
DN_HB = DN_HEADS
N_CHUNKS_PER_BLOCK = DN_BLOCK // DN_CHUNK


def _group_of_block(i, n_lat_blocks, blocks_per_batch):
    return jnp.where(i >= n_lat_blocks, 0, 1 + i // blocks_per_batch)


def _modulated(x, gain, shift, scale):
    y = x * lax.rsqrt(jnp.mean(x * x, axis=-1, keepdims=True) + EPS) * gain
    return y * (1 + scale) + shift


def _dn_inproj_kernel(x_ref, gain_ref, shift_ref, scale_ref, w_ref, wab_ref, cw_ref, alog_ref, dtb_ref,
                      q_ref, k_ref, v_ref, z_ref, gate_ref, *, n_lat_blocks):
    i = pl.program_id(0)
    nrow = DN_BLOCK
    d = D_MODEL
    pair = 2 * DN_DK
    seg = jnp.where(i >= n_lat_blocks, CTX_LEN, GRID_W)
    r = lax.broadcasted_iota(jnp.int32, (nrow, 1), 0)
    pos = r & (seg - 1)
    not_first = pos != 0
    not_last = pos != seg - 1
    h = _modulated(x_ref[...], gain_ref[...], shift_ref[0], scale_ref[0])
    hb = h.astype(BF16)
    outs = (q_ref, k_ref, v_ref)
    for part in range(3):
        for hp in range(d // pair):
            col = part * d + hp * pair
            x = jnp.dot(hb, w_ref[:, col:col + pair], preferred_element_type=F32)
            cw = cw_ref[:, col:col + pair]
            xp = jnp.where(not_first, pltpu.roll(x, 1, axis=0), 0.0)
            xn = jnp.where(not_last, pltpu.roll(x, nrow - 1, axis=0), 0.0)
            y = cw[0:1] * xp + cw[1:2] * x + cw[2:3] * xn
            y = y * jax.nn.sigmoid(y)
            for hh in range(2):
                yh = y[:, hh * DN_DK:(hh + 1) * DN_DK]
                if part < 2:
                    yh = yh * lax.rsqrt(jnp.sum(yh * yh, axis=-1, keepdims=True) + EPS)
                if part == 0:
                    yh = yh * DN_DK ** -0.5
                outs[part][:, hp * pair + hh * DN_DK:hp * pair + (hh + 1) * DN_DK] = yh
    for j in range(d // pair):
        z_ref[:, j * pair:(j + 1) * pair] = jnp.dot(hb, w_ref[:, 3 * d + j * pair:3 * d + (j + 1) * pair],
                                                    preferred_element_type=F32)

    ab = jnp.dot(h, wab_ref[...], precision=lax.Precision.HIGHEST, preferred_element_type=F32)
    nd = 2 * DN_HEADS
    a = ab[:, :nd] + dtb_ref[...]
    softplus = jnp.maximum(a, 0.0) + jnp.log(1.0 + jnp.exp(-jnp.abs(a)))
    g = -jnp.exp(alog_ref[...]) * softplus
    beta = jax.nn.sigmoid(ab[:, nd:])
    cpos = r & (DN_CHUNK - 1)
    gp, gs = g, g
    sh = 1
    while sh < DN_CHUNK:
        gp = gp + jnp.where(cpos >= sh, pltpu.roll(gp, sh, axis=0), 0.0)
        gs = gs + jnp.where(cpos < DN_CHUNK - sh, pltpu.roll(gs, nrow - sh, axis=0), 0.0)
        sh *= 2
    colid = lax.broadcasted_iota(jnp.int32, (1, nd), 1)
    gate_ref[:, :nd] = jnp.where(colid < DN_HEADS, gp, gs)
    gate_ref[:, nd:] = beta


def _dn_inproj(xs, gain, shift3, scale3, w_in, conv_w, a_log, dt_bias, n_lat_blocks, blocks_per_batch):
    t, d = xs.shape
    nd = 2 * DN_HEADS
    row = lambda i: (i, 0)
    fixed = lambda i: (0, 0)
    grp = lambda i: (_group_of_block(i, n_lat_blocks, blocks_per_batch), 0, 0)
    return pl.pallas_call(
        functools.partial(_dn_inproj_kernel, n_lat_blocks=n_lat_blocks),
        out_shape=[jax.ShapeDtypeStruct((t, d), F32)] * 4 + [jax.ShapeDtypeStruct((t, 2 * nd), F32)],
        grid=(t // DN_BLOCK,),
        in_specs=[pl.BlockSpec((DN_BLOCK, d), row),
                  pl.BlockSpec((1, d), fixed),
                  pl.BlockSpec((1, 1, d), grp),
                  pl.BlockSpec((1, 1, d), grp),
                  pl.BlockSpec((d, 4 * d), fixed),
                  pl.BlockSpec((d, 2 * nd), fixed),
                  pl.BlockSpec((3, 3 * d), fixed),
                  pl.BlockSpec((1, nd), fixed),
                  pl.BlockSpec((1, nd), fixed)],
        out_specs=[pl.BlockSpec((DN_BLOCK, d), row)] * 4 + [pl.BlockSpec((DN_BLOCK, 2 * nd), row)],
        compiler_params=pltpu.CompilerParams(
            dimension_semantics=("arbitrary",), vmem_limit_bytes=VMEM_LIMIT_BYTES),
        name="dn_inproj",
    )(xs, gain.reshape(1, d), shift3, scale3, w_in[:, :4 * d].astype(BF16), w_in[:, 4 * d:], conv_w,
      a_log.reshape(1, nd), dt_bias.reshape(1, nd))


def _dotb(a, b):
    return jnp.dot(a.astype(BF16), b.astype(BF16), preferred_element_type=F32)


def _dotb_nt(a, b):
    return lax.dot_general(a.astype(BF16), b.astype(BF16), (((1,), (1,)), ((), ())),
                           preferred_element_type=F32)


def _dotb_tn(a, b):
    return lax.dot_general(a.astype(BF16), b.astype(BF16), (((0,), (0,)), ((), ())),
                           preferred_element_type=F32)


def _unit_tri_inverses(mats, ii, jj):
    eye = (ii == jj).astype(F32)
    diag8 = (ii >> 3) == (jj >> 3)
    n = [-jnp.where(diag8, a, 0.0) for a in mats]
    n2 = [_dotb(x, x) for x in n]
    m = [eye + x for x in n]
    m = [x + _dotb(x, y) for x, y in zip(m, n2)]
    n4 = [_dotb(x, x) for x in n2]
    m = [x + _dotb(x, y) for x, y in zip(m, n4)]
    sh = 3
    while (1 << sh) < DN_CHUNK:
        off = ((ii >> (sh + 1)) == (jj >> (sh + 1))) & ((ii >> sh) != (jj >> sh))
        cm = [_dotb(jnp.where(off, a, 0.0), x) for a, x in zip(mats, m)]
        m = [x - _dotb(x, y) for x, y in zip(m, cm)]
        sh += 1
    return m


def _dn_scan_kernel(qf_ref, kf_ref, vf_ref, gcf_ref, grf_ref, qb_ref, kb_ref, vb_ref, gcb_ref, grb_ref,
                    of_ref, ob_ref, s_ref):
    @pl.when(pl.program_id(2) == 0)
    def _():
        s_ref[...] = jnp.zeros_like(s_ref)

    c = DN_CHUNK
    ncb = N_CHUNKS_PER_BLOCK
    ii = lax.broadcasted_iota(jnp.int32, (c, c), 0)
    jj = lax.broadcasted_iota(jnp.int32, (c, c), 1)
    incl = (ii >= jj, ii <= jj)
    strict = (ii > jj, ii < jj)
    dirs = ((qf_ref, kf_ref, vf_ref, gcf_ref, grf_ref, of_ref),
            (qb_ref, kb_ref, vb_ref, gcb_ref, grb_ref, ob_ref))
    items = [(d, hh, ci) for d in range(2) for hh in range(DN_HB) for ci in range(ncb)]

    def rows(ci):
        return slice(ci * c, (ci + 1) * c)

    def cols(hh):
        return slice(hh * DN_DK, (hh + 1) * DN_DK)

    q = [dirs[d][0][rows(ci), cols(hh)] for d, hh, ci in items]
    k = [dirs[d][1][rows(ci), cols(hh)] for d, hh, ci in items]
    v = [dirs[d][2][rows(ci), cols(hh)] for d, hh, ci in items]
    gc = [dirs[d][3][hh, rows(ci), d:d + 1] for d, hh, ci in items]
    gr = [dirs[d][4][hh, d:d + 1, rows(ci)] for d, hh, ci in items]
    beta = [dirs[d][3][hh, rows(ci), 2 + d:3 + d] for d, hh, ci in items]

    decay = [jnp.where(incl[it[0]], jnp.exp(jnp.where(incl[it[0]], x - y, 0.0)), 0.0)
             for it, x, y in zip(items, gc, gr)]
    kb = [x * y for x, y in zip(k, beta)]
    a = [_dotb_nt(x, y) * jnp.where(strict[it[0]], z, 0.0) for it, x, y, z in zip(items, kb, k, decay)]
    attn = [_dotb_nt(x, y) * z for x, y, z in zip(q, k, decay)]
    tinv = _unit_tri_inverses(a, ii, jj)
    eg = [jnp.exp(x) for x in gc]
    uw = [_dotb(t, jnp.concatenate([x * y, z * e], axis=-1))
          for t, x, y, z, e in zip(tinv, v, beta, kb, eg)]
    g_last = [x[0:1] if it[0] else x[c - 1:c] for it, x in zip(items, gc)]
    wq = [jnp.concatenate([x[:, DN_DV:], y * e], axis=0) for x, y, e in zip(uw, q, eg)]
    k_dec = [x * jnp.exp(y - z) for x, y, z in zip(k, g_last, gc)]
    s_dec = [jnp.exp(x) for x in g_last]

    chains = [(d, hh) for d in range(2) for hh in range(DN_HB)]
    state = [s_ref[d, hh] for d, hh in chains]
    for step in range(ncb):
        cur = [items.index((d, hh, ncb - 1 - step if d else step)) for d, hh in chains]
        ws = [_dotb(wq[n], s) for n, s in zip(cur, state)]
        v_new = [uw[n][:, :DN_DV] - x[:c] for n, x in zip(cur, ws)]
        o = [x[c:] + _dotb(attn[n], y) for n, x, y in zip(cur, ws, v_new)]
        state = [s * s_dec[n] + _dotb_tn(k_dec[n], y) for n, s, y in zip(cur, state, v_new)]
        for n, x in zip(cur, o):
            d, hh, ci = items[n]
            dirs[d][5][rows(ci), cols(hh)] = x
    for (d, hh), s in zip(chains, state):
        s_ref[d, hh] = s


def _dn_scan(q, k, v, gates, n_lat_blocks, blocks_per_batch):
    t, d = q.shape
    bsz = n_lat_blocks // blocks_per_batch
    g4 = gates.reshape(t, 4, DN_HEADS)
    gcol = jnp.transpose(g4, (2, 0, 1))
    grow = jnp.transpose(g4, (2, 1, 0))

    def blk_f(b, s):
        return jnp.where(s == 0, n_lat_blocks + b, b * blocks_per_batch + s - 1)

    def blk_b(b, s):
        return jnp.where(s == 0, n_lat_blocks + b, b * blocks_per_batch + blocks_per_batch - s)

    hw = DN_HB * DN_DK

    def specs(blk):
        return [pl.BlockSpec((DN_BLOCK, hw), lambda b, hg, s: (blk(b, s), hg))] * 3 + [
            pl.BlockSpec((DN_HB, DN_BLOCK, 4), lambda b, hg, s: (hg, blk(b, s), 0)),
            pl.BlockSpec((DN_HB, 4, DN_BLOCK), lambda b, hg, s: (hg, 0, blk(b, s)))]

    return pl.pallas_call(
        _dn_scan_kernel,
        out_shape=[jax.ShapeDtypeStruct((t, d), F32)] * 2,
        grid=(bsz, DN_HEADS // DN_HB, 1 + blocks_per_batch),
        in_specs=specs(blk_f) + specs(blk_b),
        out_specs=[pl.BlockSpec((DN_BLOCK, hw), lambda b, hg, s: (blk_f(b, s), hg)),
                   pl.BlockSpec((DN_BLOCK, hw), lambda b, hg, s: (blk_b(b, s), hg))],
        scratch_shapes=[pltpu.VMEM((2, DN_HB, DN_DK, DN_DV), F32)],
        compiler_params=pltpu.CompilerParams(
            dimension_semantics=("arbitrary", "arbitrary", "arbitrary"),
            vmem_limit_bytes=VMEM_LIMIT_BYTES),
        name="dn_scan",
    )(q, k, v, gcol, grow, q, k, v, gcol, grow)


def _dn_out_kernel(of_ref, ob_ref, z_ref, on_ref, w_ref, x_ref, gate_ref, o_ref):
    z = z_ref[...]
    zs = z * jax.nn.sigmoid(z)
    parts = []
    for h in range(DN_HEADS):
        cols = slice(h * DN_DV, (h + 1) * DN_DV)
        o = of_ref[:, cols] + ob_ref[:, cols]
        o = o * lax.rsqrt(jnp.mean(o * o, axis=-1, keepdims=True) + EPS)
        parts.append(o)
    y = jnp.concatenate(parts, axis=-1) * on_ref[...] * zs
    o_ref[...] = x_ref[...] + gate_ref[0] * jnp.dot(y.astype(BF16), w_ref[...],
                                                    preferred_element_type=F32)


def _dn_out(o_f, o_b, z, out_norm, w_out, xs, gate3, n_lat_blocks, blocks_per_batch):
    t, d = xs.shape
    row = lambda i: (i, 0)
    fixed = lambda i: (0, 0)
    grp = lambda i: (_group_of_block(i, n_lat_blocks, blocks_per_batch), 0, 0)
    return pl.pallas_call(
        _dn_out_kernel,
        out_shape=jax.ShapeDtypeStruct((t, d), F32),
        grid=(t // DN_BLOCK,),
        in_specs=[pl.BlockSpec((DN_BLOCK, d), row),
                  pl.BlockSpec((DN_BLOCK, d), row),
                  pl.BlockSpec((DN_BLOCK, d), row),
                  pl.BlockSpec((1, d), fixed),
                  pl.BlockSpec((d, d), fixed),
                  pl.BlockSpec((DN_BLOCK, d), row),
                  pl.BlockSpec((1, 1, d), grp)],
        out_specs=pl.BlockSpec((DN_BLOCK, d), row),
        compiler_params=pltpu.CompilerParams(
            dimension_semantics=("arbitrary",), vmem_limit_bytes=VMEM_LIMIT_BYTES),
        name="dn_out",
    )(o_f, o_b, z, jnp.tile(out_norm, DN_HEADS).reshape(1, d), w_out.astype(BF16), xs, gate3)


def _deltanet_layer(xs, gain, shift3, scale3, w_in, conv_w, a_log, dt_bias, out_norm, w_out, gate3,
                    n_lat_blocks, blocks_per_batch):
    q, k, v, z, gates = _dn_inproj(xs, gain, shift3, scale3, w_in, conv_w, a_log, dt_bias, n_lat_blocks,
                                   blocks_per_batch)
    o_f, o_b = _dn_scan(q, k, v, gates, n_lat_blocks, blocks_per_batch)
    return _dn_out(o_f, o_b, z, out_norm, w_out, xs, gate3, n_lat_blocks, blocks_per_batch)


HY_N2 = 256
HY_CB = 8
HY_TOK_TILE = 512
HY_FILT_TILE = 512


def _dft_constants(nr):
    n1, n2 = 2 * nr, HY_N2
    n = n1 * n2
    a1 = np.arange(n1)
    f1 = np.exp(-2j * np.pi * np.outer(a1, a1) / n1)
    lhs_fwd = np.concatenate([f1.real[:, :nr], f1.imag[:, :nr]], axis=0)
    lhs_inv = np.concatenate([f1.real[:nr, :], f1.imag[:nr, :]], axis=1) / n
    tw = np.exp(-2j * np.pi * np.outer(a1, np.arange(n2)) / n)
    a2 = np.arange(n2)
    f2 = np.exp(-2j * np.pi * np.outer(a2, a2) / n2)
    w_fwd = np.block([[f2.real, f2.imag], [-f2.imag, f2.real]])
    w_inv = np.block([[f2.real, -f2.imag], [f2.imag, f2.real]])
    return (jnp.asarray(lhs_fwd, BF16), jnp.asarray(lhs_inv, BF16), jnp.asarray(tw.real, F32),
            jnp.asarray(tw.imag, F32), jnp.asarray(w_fwd, BF16), jnp.asarray(w_inv, BF16))


def _hy_dft(x3, lhs_fwd, twr, twi, w_fwd):
    n1 = twr.shape[0]
    a = [jnp.dot(lhs_fwd, x3[c].astype(BF16), preferred_element_type=F32) for c in range(x3.shape[0])]
    br = jnp.concatenate([t[:n1] * twr - t[n1:] * twi for t in a], axis=0)
    bi = jnp.concatenate([t[:n1] * twi + t[n1:] * twr for t in a], axis=0)
    b = jnp.concatenate([br, bi], axis=1)
    return jnp.dot(b.astype(BF16), w_fwd, preferred_element_type=F32)


def _hy_idft(p, cb, lhs_inv, twr, twi, w_inv):
    n1, n2 = twr.shape
    c = jnp.dot(p.astype(BF16), w_inv, preferred_element_type=F32)
    out = []
    for ch in range(cb):
        cr = c[ch * n1:(ch + 1) * n1, :n2]
        ci = c[ch * n1:(ch + 1) * n1, n2:]
        d = jnp.concatenate([cr * twr + ci * twi, ci * twr - cr * twi], axis=0)
        out.append(jnp.dot(lhs_inv, d.astype(BF16), preferred_element_type=F32))
    return out


def _hy_spectrum_kernel(hf_ref, hb_ref, lf_ref, twr_ref, twi_ref, wf_ref, o_ref):
    cb, nr, n2 = hf_ref.shape
    first = ((lax.broadcasted_iota(jnp.int32, (nr, n2), 0) == 0)
             & (lax.broadcasted_iota(jnp.int32, (nr, n2), 1) == 0))
    hb = jnp.where(first, 0.0, hb_ref[...])
    consts = (lf_ref[...], twr_ref[...], twi_ref[...], wf_ref[...])
    xf = _hy_dft(hf_ref[...], *consts)
    xb = _hy_dft(hb, *consts)
    n1 = 2 * nr
    o_ref[...] = jnp.concatenate([xf[:, :n2] + xb[:, :n2], xf[:, n2:] - xb[:, n2:]],
                                 axis=1).reshape(cb, n1, 2 * n2)


def _hy_spectrum(filt, consts):
    d = D_MODEL
    l = filt.shape[1]
    nr = l // HY_N2
    n1 = 2 * nr
    lhs_fwd, _, twr, twi, w_fwd, _ = consts
    cpo = d // HY_CB
    fixed2 = lambda o, c: (0, 0)
    return pl.pallas_call(
        _hy_spectrum_kernel,
        out_shape=jax.ShapeDtypeStruct((HY_ORDER * d, n1, 2 * HY_N2), F32),
        grid=(HY_ORDER, cpo),
        in_specs=[pl.BlockSpec((HY_CB, nr, HY_N2), lambda o, c: (2 * o * cpo + c, 0, 0)),
                  pl.BlockSpec((HY_CB, nr, HY_N2), lambda o, c: ((2 * o + 1) * cpo + c, 0, 0)),
                  pl.BlockSpec(lhs_fwd.shape, fixed2),
                  pl.BlockSpec(twr.shape, fixed2),
                  pl.BlockSpec(twi.shape, fixed2),
                  pl.BlockSpec(w_fwd.shape, fixed2)],
        out_specs=pl.BlockSpec((HY_CB, n1, 2 * HY_N2), lambda o, c: (o * cpo + c, 0, 0)),
        compiler_params=pltpu.CompilerParams(
            dimension_semantics=("arbitrary", "arbitrary"), vmem_limit_bytes=VMEM_LIMIT_BYTES),
        name="hy_spectrum",
    )(filt.reshape(-1, nr, HY_N2), filt.reshape(-1, nr, HY_N2), lhs_fwd, twr, twi, w_fwd)


def _hy_conv_kernel(z_ref, g_ref, k_ref, bias_ref, lf_ref, li_ref, twr_ref, twi_ref, wf_ref, wi_ref,
                    o_ref):
    cb, nr, n2 = z_ref.shape
    twr, twi = twr_ref[...], twi_ref[...]
    z = z_ref[...]
    x = _hy_dft(z, lf_ref[...], twr, twi, wf_ref[...])
    kk = k_ref[...].reshape(x.shape)
    xr, xi, kr, ki = x[:, :n2], x[:, n2:], kk[:, :n2], kk[:, n2:]
    p = jnp.concatenate([xr * kr - xi * ki, xr * ki + xi * kr], axis=1)
    conv = _hy_idft(p, cb, li_ref[...], twr, twi, wi_ref[...])
    for c in range(cb):
        o_ref[c] = g_ref[c] * (conv[c] + bias_ref[c] * z[c])


def _hy_conv(z, z_part, gate, gate_part, khat, order, bias, consts, bsz):
    d = D_MODEL
    l = z.shape[1] // bsz
    nr = l // HY_N2
    n1 = 2 * nr
    cpo = d // HY_CB
    lhs_fwd, lhs_inv, twr, twi, w_fwd, w_inv = consts
    fixed2 = lambda c, b: (0, 0)
    out = pl.pallas_call(
        _hy_conv_kernel,
        out_shape=jax.ShapeDtypeStruct((d, bsz * nr, HY_N2), F32),
        grid=(cpo, bsz),
        in_specs=[pl.BlockSpec((HY_CB, nr, HY_N2), lambda c, b: (z_part * cpo + c, b, 0)),
                  pl.BlockSpec((HY_CB, nr, HY_N2), lambda c, b: (gate_part * cpo + c, b, 0)),
                  pl.BlockSpec((HY_CB, n1, 2 * HY_N2), lambda c, b: (order * cpo + c, 0, 0)),
                  pl.BlockSpec((HY_CB, 1, 1), lambda c, b: (c, 0, 0)),
                  pl.BlockSpec(lhs_fwd.shape, fixed2),
                  pl.BlockSpec(lhs_inv.shape, fixed2),
                  pl.BlockSpec(twr.shape, fixed2),
                  pl.BlockSpec(twi.shape, fixed2),
                  pl.BlockSpec(w_fwd.shape, fixed2),
                  pl.BlockSpec(w_inv.shape, fixed2)],
        out_specs=pl.BlockSpec((HY_CB, nr, HY_N2), lambda c, b: (c, b, 0)),
        compiler_params=pltpu.CompilerParams(
            dimension_semantics=("arbitrary", "arbitrary"), vmem_limit_bytes=VMEM_LIMIT_BYTES),
        name="hy_conv",
    )(z.reshape(-1, bsz * nr, HY_N2), gate.reshape(-1, bsz * nr, HY_N2), khat,
      bias.reshape(d, 1, 1), lhs_fwd, lhs_inv, twr, twi, w_fwd, w_inv)
    return out.reshape(d, bsz * l)


def _hy_ctx_kernel(z_ref, g_ref, hf_ref, hb_ref, bias_ref, wf_ref, wi_ref, o_ref, *, bsz):
    l = hf_ref.shape[1]
    wf, wi = wf_ref[...], wi_ref[...]
    hb = jnp.where(lax.broadcasted_iota(jnp.int32, (1, l), 1) == 0, 0.0, hb_ref[...])
    kf = jnp.dot(hf_ref[...].astype(BF16), wf, preferred_element_type=F32)
    kb = jnp.dot(hb.astype(BF16), wf, preferred_element_type=F32)
    n = 2 * l
    kr, ki = kf[:, :n] + kb[:, :n], kf[:, n:] - kb[:, n:]
    bias = bias_ref[...]
    for b in range(bsz):
        z = z_ref[:, b * l:(b + 1) * l]
        x = jnp.dot(z.astype(BF16), wf, preferred_element_type=F32)
        xr, xi = x[:, :n], x[:, n:]
        p = jnp.concatenate([xr * kr - xi * ki, xr * ki + xi * kr], axis=1)
        conv = jnp.dot(p.astype(BF16), wi, preferred_element_type=F32)
        o_ref[:, b * l:(b + 1) * l] = g_ref[:, b * l:(b + 1) * l] * (conv + bias * z)


def _hy_ctx(z, z_part, gate, gate_part, filt, order, bias, bsz):
    d = D_MODEL
    l = filt.shape[1]
    n = 2 * l
    ang = 2 * np.pi * np.outer(np.arange(l), np.arange(n)) / n
    w_fwd = jnp.asarray(np.concatenate([np.cos(ang), -np.sin(ang)], axis=1), BF16)
    w_inv = jnp.asarray(np.concatenate([np.cos(ang.T), -np.sin(ang.T)], axis=0) / n, BF16)
    cb = 256
    nblk = d // cb
    fixed = lambda c: (0, 0)
    return pl.pallas_call(
        functools.partial(_hy_ctx_kernel, bsz=bsz),
        out_shape=jax.ShapeDtypeStruct((d, bsz * l), F32),
        grid=(nblk,),
        in_specs=[pl.BlockSpec((cb, bsz * l), lambda c: (z_part * nblk + c, 0)),
                  pl.BlockSpec((cb, bsz * l), lambda c: (gate_part * nblk + c, 0)),
                  pl.BlockSpec((cb, l), lambda c: (2 * order * nblk + c, 0)),
                  pl.BlockSpec((cb, l), lambda c: ((2 * order + 1) * nblk + c, 0)),
                  pl.BlockSpec((cb, 1), lambda c: (c, 0)),
                  pl.BlockSpec(w_fwd.shape, fixed),
                  pl.BlockSpec(w_inv.shape, fixed)],
        out_specs=pl.BlockSpec((cb, bsz * l), lambda c: (c, 0)),
        compiler_params=pltpu.CompilerParams(
            dimension_semantics=("arbitrary",), vmem_limit_bytes=VMEM_LIMIT_BYTES),
        name="hy_ctx_conv",
    )(z, gate, filt, filt, bias.reshape(d, 1), w_fwd, w_inv)


def _hy_filter_kernel(band_ref, w1t_ref, w1c_ref, w1s_ref, b1_ref, fr_ref, w2_ref, b2_ref, w3_ref,
                      delta_ref, o_ref, *, l):
    tl = o_ref.shape[1]
    d = D_MODEL
    hp = lax.Precision.HIGHEST
    pos = (lax.broadcasted_iota(jnp.int32, (1, tl), 1) + pl.program_id(0) * tl).astype(F32)
    t = pos / max(l - 1, 1)
    ang = ((2 * math.pi / l) * pos) * band_ref[...]
    fr = fr_ref[...]
    pre = (w1t_ref[...] * t + jnp.dot(w1c_ref[...], jnp.cos(ang), precision=hp)
           + jnp.dot(w1s_ref[...], -jnp.sin(ang), precision=hp) + b1_ref[...])
    hdn = jnp.sin(fr * pre)
    hdn = jnp.sin(fr * (jnp.dot(w2_ref[...], hdn, precision=hp) + b2_ref[...]))
    window = jnp.exp(-t * delta_ref[...])
    for part in range(2 * HY_ORDER):
        rows = slice(part * d, (part + 1) * d)
        o_ref[rows, :] = jnp.dot(w3_ref[rows, :], hdn, precision=hp) * window


def _hy_filter(l, w1, b1, freq, w2, b2, w3):
    d = D_MODEL
    nb = HY_BANDS
    tl = min(HY_FILT_TILE, l)
    col = lambda v: v.reshape(-1, 1)
    bands = jnp.linspace(1e-4, nb - 1, nb, dtype=F32)
    deltas = jnp.abs(jnp.linspace(math.log(HY_TARGET) / HY_SLOW, math.log(HY_TARGET) / HY_FAST, d, dtype=F32))
    w1t = w1.T
    args = (col(bands), w1t[:, 0:1], w1t[:, 1:1 + nb], w1t[:, 1 + nb:], col(b1), col(freq), w2.T, col(b2),
            w3.T, col(deltas))
    return pl.pallas_call(
        functools.partial(_hy_filter_kernel, l=l),
        out_shape=jax.ShapeDtypeStruct((2 * HY_ORDER * d, l), F32),
        grid=(l // tl,),
        in_specs=[pl.BlockSpec(a.shape, lambda j: (0, 0)) for a in args],
        out_specs=pl.BlockSpec((2 * HY_ORDER * d, tl), lambda j: (0, j)),
        compiler_params=pltpu.CompilerParams(
            dimension_semantics=("arbitrary",), vmem_limit_bytes=VMEM_LIMIT_BYTES),
        name="hy_filter",
    )(*args)


def _hy_inproj_kernel(x_ref, gain_ref, shift_ref, scale_ref, wt_ref, cw_ref, o_ref, *, seg):
    nch = wt_ref.shape[0]
    tm = x_ref.shape[0]
    hb = _modulated(x_ref[...], gain_ref[...], shift_ref[0], scale_ref[0]).astype(BF16)
    pos = lax.broadcasted_iota(jnp.int32, (1, tm), 1) & (seg - 1)
    not_first = pos != 0
    not_last = pos != seg - 1
    sub = 512
    for j in range(nch // sub):
        rows = slice(j * sub, (j + 1) * sub)
        p = _dotb_nt(wt_ref[rows, :], hb)
        cw = cw_ref[rows, :]
        prev = jnp.where(not_first, pltpu.roll(p, 1, axis=1), 0.0)
        nxt = jnp.where(not_last, pltpu.roll(p, tm - 1, axis=1), 0.0)
        o_ref[rows, :] = cw[:, 0:1] * prev + cw[:, 1:2] * p + cw[:, 2:3] * nxt


def _hy_inproj(xs, gain, shift3, scale3, w_in, conv_w, first_tile, n_tiles, seg, n_lat_blocks,
               blocks_per_batch):
    k = xs.shape[1]
    nch = w_in.shape[1]
    tm = HY_TOK_TILE
    per_tile = tm // DN_BLOCK
    grp = lambda i: (_group_of_block((first_tile + i) * per_tile, n_lat_blocks, blocks_per_batch), 0, 0)
    return pl.pallas_call(
        functools.partial(_hy_inproj_kernel, seg=seg),
        out_shape=jax.ShapeDtypeStruct((nch, n_tiles * tm), F32),
        grid=(n_tiles,),
        in_specs=[pl.BlockSpec((tm, k), lambda i: (first_tile + i, 0)),
                  pl.BlockSpec((1, k), lambda i: (0, 0)),
                  pl.BlockSpec((1, 1, k), grp),
                  pl.BlockSpec((1, 1, k), grp),
                  pl.BlockSpec((nch, k), lambda i: (0, 0)),
                  pl.BlockSpec((nch, 3), lambda i: (0, 0))],
        out_specs=pl.BlockSpec((nch, tm), lambda i: (0, i)),
        compiler_params=pltpu.CompilerParams(
            dimension_semantics=("arbitrary",), vmem_limit_bytes=VMEM_LIMIT_BYTES),
        name="hy_inproj",
    )(xs, gain.reshape(1, k), shift3, scale3, w_in.T.astype(BF16), conv_w.T)


def _hy_out_kernel(zl_ref, zc_ref, w_ref, x_ref, gate_ref, o_ref, *, n_lat_tiles):
    z = jnp.where(pl.program_id(0) >= n_lat_tiles, zc_ref[...], zl_ref[...])
    o_ref[...] = x_ref[...] + gate_ref[0] * _dotb_tn(z, w_ref[...])


def _hy_out(z_lat, z_ctx, w_out, xs, gate3, blocks_per_batch):
    t, d = xs.shape
    tm = HY_TOK_TILE
    n_lat_tiles = z_lat.shape[1] // tm
    per_tile = tm // DN_BLOCK
    grp = lambda i: (_group_of_block(i * per_tile, n_lat_tiles * per_tile, blocks_per_batch), 0, 0)
    return pl.pallas_call(
        functools.partial(_hy_out_kernel, n_lat_tiles=n_lat_tiles),
        out_shape=jax.ShapeDtypeStruct((t, d), F32),
        grid=(t // tm,),
        in_specs=[pl.BlockSpec((d, tm), lambda i: (0, jnp.minimum(i, n_lat_tiles - 1))),
                  pl.BlockSpec((d, tm), lambda i: (0, 0)),
                  pl.BlockSpec((d, d), lambda i: (0, 0)),
                  pl.BlockSpec((tm, d), lambda i: (i, 0)),
                  pl.BlockSpec((1, 1, d), grp)],
        out_specs=pl.BlockSpec((tm, d), lambda i: (i, 0)),
        compiler_params=pltpu.CompilerParams(
            dimension_semantics=("arbitrary",), vmem_limit_bytes=VMEM_LIMIT_BYTES),
        name="hy_out",
    )(z_lat, z_ctx, w_out.astype(BF16), xs, gate3)


def _hyena_layer(xs, gain, shift3, scale3, w_in, conv_w, f_w1, f_b1, f_freq, f_w2, f_b2, f_w3, bias, w_out,
                 gate3, bsz, seq):
    n_lat_tiles = bsz * seq // HY_TOK_TILE
    assert bsz * CTX_LEN == HY_TOK_TILE
    margs = (xs, gain, shift3, scale3, w_in, conv_w)
    blocks = (bsz * seq // DN_BLOCK, seq // DN_BLOCK)
    p_lat = _hy_inproj(*margs, 0, n_lat_tiles, GRID_W, *blocks)
    p_ctx = _hy_inproj(*margs, n_lat_tiles, 1, CTX_LEN, *blocks)
    fargs = (f_w1, f_b1, f_freq, f_w2, f_b2, f_w3)
    consts = _dft_constants(seq // HY_N2)
    khat = _hy_spectrum(_hy_filter(seq, *fargs), consts)
    filt_ctx = _hy_filter(CTX_LEN, *fargs)
    z_lat, z_ctx = p_lat, p_ctx
    for n in range(HY_ORDER):
        z_lat = _hy_conv(z_lat, 0, p_lat, n + 1, khat, n, bias[n], consts, bsz)
        z_ctx = _hy_ctx(z_ctx, 0, p_ctx, n + 1, filt_ctx, n, bias[n], bsz)
    return _hy_out(z_lat, z_ctx, w_out, xs, gate3, seq // DN_BLOCK)


MOE_TILE = 512
PACK = 2


def _route_kernel(x_ref, gain_ref, shift_ref, scale_ref, wr_ref, rb_ref, tri_ref,
                  f_ref, e_ref, w_ref, r_ref, cnt_ref, carry_ref):
    tm, d = x_ref.shape
    ne, epg, ng = N_EXPERTS, EXPERTS_PER_GROUP, N_GROUPS

    @pl.when(pl.program_id(0) == 0)
    def _():
        carry_ref[...] = jnp.zeros_like(carry_ref)

    x = x_ref[...]
    y = x * lax.rsqrt(jnp.mean(x * x, axis=-1, keepdims=True) + EPS) * gain_ref[...]
    f = y * (1 + scale_ref[0]) + shift_ref[0]
    bits = pltpu.bitcast(f.astype(BF16).astype(F32), jnp.uint32)
    half = d // PACK
    f_ref[...] = (bits[:, :half] >> 16) | (bits[:, half:] & jnp.uint32(0xFFFF0000))

    logits = lax.dot_general(wr_ref[...], f, (((1,), (1,)), ((), ())),
                             precision=lax.Precision.HIGHEST, preferred_element_type=F32)
    scores = jax.nn.sigmoid(logits)
    biased = scores + rb_ref[...]
    s = [scores[j * ng:(j + 1) * ng] for j in range(epg)]
    c = [biased[j * ng:(j + 1) * ng] for j in range(epg)]
    hi01, lo01 = jnp.maximum(c[0], c[1]), jnp.minimum(c[0], c[1])
    hi23, lo23 = jnp.maximum(c[2], c[3]), jnp.minimum(c[2], c[3])
    gscore = jnp.maximum(hi01, hi23) + jnp.maximum(jnp.minimum(hi01, hi23), jnp.maximum(lo01, lo23))
    gi = lax.broadcasted_iota(jnp.int32, (ng, tm), 0)
    gmax = jnp.max(gscore, axis=0, keepdims=True)
    grp = jnp.min(jnp.where(gscore == gmax, gi, ng), axis=0, keepdims=True)
    sel = gi == grp
    cv = [jnp.sum(jnp.where(sel, t, 0.0), axis=0, keepdims=True) for t in c]
    sv = [jnp.sum(jnp.where(sel, t, 0.0), axis=0, keepdims=True) for t in s]

    def pick(excluded):
        best = jnp.full((1, tm), -jnp.inf, F32)
        idx = jnp.zeros((1, tm), jnp.int32)
        val = jnp.zeros((1, tm), F32)
        for j in range(epg):
            cand = cv[j] if excluded is None else jnp.where(excluded == j, -jnp.inf, cv[j])
            take = cand > best
            best = jnp.where(take, cand, best)
            idx = jnp.where(take, j, idx)
            val = jnp.where(take, sv[j], val)
        return idx, val

    i1, v1 = pick(None)
    i2, v2 = pick(i1)
    e1 = grp * epg + i1
    e2 = grp * epg + i2
    wsum = v1 + v2
    e_ref[0:1, :] = e1
    e_ref[1:2, :] = e2
    w_ref[0:1, :] = v1 / wsum
    w_ref[1:2, :] = v2 / wsum

    ei = lax.broadcasted_iota(jnp.int32, (ne, tm), 0)
    oh1 = ei == e1
    oh2 = ei == e2
    tri = tri_ref[...]
    pre1 = jnp.dot(oh1.astype(BF16), tri, preferred_element_type=F32)
    pre2 = jnp.dot(oh2.astype(BF16), tri, preferred_element_type=F32)
    tot1 = pre1[:, tm - 1:tm]
    tot2 = pre2[:, tm - 1:tm]
    carry = carry_ref[:, 0:1]
    r1 = jnp.sum(jnp.where(oh1, carry + pre1 - 1.0, 0.0), axis=0, keepdims=True)
    r2 = jnp.sum(jnp.where(oh2, carry + tot1 + pre2 - 1.0, 0.0), axis=0, keepdims=True)
    r_ref[0:1, :] = r1.astype(jnp.int32)
    r_ref[1:2, :] = r2.astype(jnp.int32)
    carry_ref[...] = carry_ref[...] + (tot1 + tot2)
    cnt_ref[...] = carry_ref[...]


def _moe_route(xs, gain, shift3, scale3, w_router, router_bias, n_lat_tiles, blocks_per_batch):
    t, d = xs.shape
    tm = MOE_TILE
    per_tile = tm // DN_BLOCK
    row = lambda i: (i, 0)
    col = lambda i: (0, i)
    fixed = lambda i: (0, 0)
    grp = lambda i: (_group_of_block(i * per_tile, n_lat_tiles * per_tile, blocks_per_batch), 0, 0)
    tri = jnp.asarray(np.triu(np.ones((tm, tm), np.float32)), BF16)
    perm = np.arange(N_EXPERTS).reshape(N_GROUPS, EXPERTS_PER_GROUP).T.reshape(-1)
    return pl.pallas_call(
        _route_kernel,
        out_shape=[jax.ShapeDtypeStruct((t, d // PACK), jnp.uint32),
                   jax.ShapeDtypeStruct((TOP_K, t), jnp.int32),
                   jax.ShapeDtypeStruct((TOP_K, t), F32),
                   jax.ShapeDtypeStruct((TOP_K, t), jnp.int32),
                   jax.ShapeDtypeStruct((N_EXPERTS, 128), F32)],
        grid=(t // tm,),
        in_specs=[pl.BlockSpec((tm, d), row),
                  pl.BlockSpec((1, d), fixed),
                  pl.BlockSpec((1, 1, d), grp),
                  pl.BlockSpec((1, 1, d), grp),
                  pl.BlockSpec((N_EXPERTS, d), fixed),
                  pl.BlockSpec((N_EXPERTS, 1), fixed),
                  pl.BlockSpec((tm, tm), fixed)],
        out_specs=[pl.BlockSpec((tm, d // PACK), row),
                   pl.BlockSpec((TOP_K, tm), col),
                   pl.BlockSpec((TOP_K, tm), col),
                   pl.BlockSpec((TOP_K, tm), col),
                   pl.BlockSpec((N_EXPERTS, 128), fixed)],
        scratch_shapes=[pltpu.VMEM((N_EXPERTS, 128), F32)],
        compiler_params=pltpu.CompilerParams(
            dimension_semantics=("arbitrary",), vmem_limit_bytes=VMEM_LIMIT_BYTES),
        name="moe_route",
    )(xs, gain.reshape(1, d), shift3, scale3, w_router.T[perm], router_bias[perm].reshape(N_EXPERTS, 1), tri)


def _row_copy(src, src_row, dst, dst_row, sem):
    return pltpu.make_async_copy(src.at[pl.ds(src_row, 1)], dst.at[pl.ds(dst_row, 1)], sem)


def _dispatch_kernel(dest_ref, f_ref, xs_in_ref, xs_ref, dest_smem, sem, idx_sem):
    del xs_in_ref
    tm = f_ref.shape[0]
    idx_copy = pltpu.make_async_copy(dest_ref, dest_smem, idx_sem)
    idx_copy.start()
    idx_copy.wait()

    def issue(tok, carry):
        for k in range(TOP_K):
            _row_copy(f_ref, tok, xs_ref, dest_smem[k, tok], sem).start(priority=k)
        return carry

    def drain(tok, carry):
        for k in range(TOP_K):
            _row_copy(f_ref, 0, xs_ref, 0, sem).wait()
        return carry

    lax.fori_loop(0, tm, issue, 0, unroll=8)
    lax.fori_loop(0, tm, drain, 0, unroll=8)


def _dispatch(f_packed, dest, n_slots):
    t, wd = f_packed.shape
    tm = MOE_TILE
    return pl.pallas_call(
        _dispatch_kernel,
        out_shape=jax.ShapeDtypeStruct((n_slots, wd), jnp.uint32),
        grid=(t // tm,),
        in_specs=[pl.BlockSpec((TOP_K, tm), lambda i: (0, i)),
                  pl.BlockSpec((tm, wd), lambda i: (i, 0)),
                  pl.BlockSpec(memory_space=pl.ANY)],
        out_specs=pl.BlockSpec(memory_space=pl.ANY),
        scratch_shapes=[pltpu.SMEM((TOP_K, tm), jnp.int32),
                        pltpu.SemaphoreType.DMA, pltpu.SemaphoreType.DMA],
        input_output_aliases={2: 0},
        compiler_params=pltpu.CompilerParams(
            dimension_semantics=("arbitrary",), vmem_limit_bytes=VMEM_LIMIT_BYTES),
        name="moe_dispatch",
    )(dest, f_packed, jnp.zeros((n_slots, wd), jnp.uint32))


def _experts_kernel(be_ref, na_ref, x_ref, wg_ref, wu_ref, wd_ref, o_ref, wgb_ref, wub_ref, wdb_ref):
    i = pl.program_id(0)
    prev = be_ref[jnp.maximum(i - 1, 0)]

    @pl.when((i == 0) | (be_ref[i] != prev))
    def _():
        wgb_ref[...] = wg_ref[0, 0].astype(BF16)
        wub_ref[...] = wu_ref[0, 0].astype(BF16)
        wdb_ref[...] = wd_ref[0, 0].astype(BF16)

    @pl.when(i < na_ref[0])
    def _():
        packed = x_ref[...]
        lo = pltpu.bitcast(packed << 16, F32)
        hi = pltpu.bitcast(packed & jnp.uint32(0xFFFF0000), F32)
        xb = jnp.concatenate([lo, hi], axis=-1).astype(BF16)
        g = jnp.dot(xb, wgb_ref[...], preferred_element_type=F32)
        u = jnp.dot(xb, wub_ref[...], preferred_element_type=F32)
        hid = (g * jax.nn.sigmoid(g)) * u
        o_ref[...] = jnp.dot(hid.astype(BF16), wdb_ref[...], preferred_element_type=F32)

    @pl.when(i >= na_ref[0])
    def _():
        o_ref[...] = jnp.zeros_like(o_ref)


def _experts(xs_sorted, block_expert, n_active, w_gate, w_up, w_down, layer):
    n_slots, wd = xs_sorted.shape
    d = wd * PACK
    n_blocks = n_slots // MOE_BLOCK
    blk = lambda i, be, na: (jnp.minimum(i, na[0] - 1), 0)
    wsel = lambda i, be, na: (layer, be[jnp.minimum(i, na[0] - 1)], 0, 0)
    grid_spec = pltpu.PrefetchScalarGridSpec(
        num_scalar_prefetch=2,
        grid=(n_blocks,),
        in_specs=[pl.BlockSpec((MOE_BLOCK, wd), blk),
                  pl.BlockSpec((1, 1, d, D_EXPERT), wsel),
                  pl.BlockSpec((1, 1, d, D_EXPERT), wsel),
                  pl.BlockSpec((1, 1, D_EXPERT, d), wsel)],
        out_specs=pl.BlockSpec((MOE_BLOCK, d), lambda i, be, na: (i, 0)),
        scratch_shapes=[pltpu.VMEM((d, D_EXPERT), BF16), pltpu.VMEM((d, D_EXPERT), BF16),
                        pltpu.VMEM((D_EXPERT, d), BF16)],
    )
    return pl.pallas_call(
        _experts_kernel,
        out_shape=jax.ShapeDtypeStruct((n_slots, d), F32),
        grid_spec=grid_spec,
        compiler_params=pltpu.CompilerParams(
            dimension_semantics=("arbitrary",), vmem_limit_bytes=VMEM_LIMIT_BYTES),
        name="moe_experts",
    )(block_expert, n_active, xs_sorted, w_gate, w_up, w_down)


def _combine_kernel(dest_ref, y_ref, x_ref, w_ref, gate_ref, fin_ref, o_ref, dest_smem, ya_ref, yb_ref, sem,
                    idx_sem, *, final_norm):
    tm = x_ref.shape[0]
    idx_copy = pltpu.make_async_copy(dest_ref, dest_smem, idx_sem)
    idx_copy.start()
    idx_copy.wait()
    bufs = (ya_ref, yb_ref)

    def issue(tok, carry):
        for k in range(TOP_K):
            _row_copy(y_ref, dest_smem[k, tok], bufs[k], tok, sem).start(priority=k)
        return carry

    def drain(tok, carry):
        for k in range(TOP_K):
            _row_copy(y_ref, 0, bufs[k], 0, sem).wait()
        return carry

    lax.fori_loop(0, tm, issue, 0, unroll=8)
    lax.fori_loop(0, tm, drain, 0, unroll=8)
    w = w_ref[...]
    out = x_ref[...] + gate_ref[0] * (w[:, 0:1] * ya_ref[...] + w[:, 1:2] * yb_ref[...])
    if final_norm:
        out = out * lax.rsqrt(jnp.mean(out * out, axis=-1, keepdims=True) + EPS) * fin_ref[...]
    o_ref[...] = out


def _combine(ys, dest, weight_cols, xs, gate3, n_lat_tiles, blocks_per_batch, final_gain=None):
    t, d = xs.shape
    tm = MOE_TILE
    per_tile = tm // DN_BLOCK
    n_tiles = t // tm if final_gain is None else n_lat_tiles
    fin = jnp.ones((1, d), F32) if final_gain is None else final_gain.reshape(1, d)
    row = lambda i: (i, 0)
    grp = lambda i: (_group_of_block(i * per_tile, n_lat_tiles * per_tile, blocks_per_batch), 0, 0)
    return pl.pallas_call(
        functools.partial(_combine_kernel, final_norm=final_gain is not None),
        out_shape=jax.ShapeDtypeStruct((n_tiles * tm, d), F32),
        grid=(n_tiles,),
        in_specs=[pl.BlockSpec((TOP_K, tm), lambda i: (0, i)),
                  pl.BlockSpec(memory_space=pl.ANY),
                  pl.BlockSpec((tm, d), row),
                  pl.BlockSpec((tm, TOP_K), row),
                  pl.BlockSpec((1, 1, d), grp),
                  pl.BlockSpec((1, d), lambda i: (0, 0))],
        out_specs=pl.BlockSpec((tm, d), row),
        scratch_shapes=[pltpu.SMEM((TOP_K, tm), jnp.int32),
                        pltpu.VMEM((tm, d), F32), pltpu.VMEM((tm, d), F32),
                        pltpu.SemaphoreType.DMA, pltpu.SemaphoreType.DMA],
        compiler_params=pltpu.CompilerParams(
            dimension_semantics=("arbitrary",), vmem_limit_bytes=VMEM_LIMIT_BYTES),
        name="moe_combine",
    )(dest, ys, xs, weight_cols, gate3, fin)


def _moe_layer(xs, gain, shift3, scale3, gate3, w_router, router_bias, w_gate, w_up, w_down, layer,
               n_lat_tiles, blocks_per_batch, final_gain=None):
    t = xs.shape[0]
    f_packed, expert, weight, rank, counts = _moe_route(xs, gain, shift3, scale3, w_router, router_bias,
                                                        n_lat_tiles, blocks_per_batch)
    counts = counts[:, 0].astype(jnp.int32)
    padded = (counts + MOE_BLOCK - 1) // MOE_BLOCK * MOE_BLOCK
    pend = jnp.cumsum(padded)
    pstart = pend - padded
    n_blocks = -(-(t * TOP_K) // MOE_BLOCK) + N_EXPERTS
    block_start = jnp.arange(n_blocks, dtype=jnp.int32) * MOE_BLOCK
    block_expert = jnp.minimum(jnp.sum(pend[None, :] <= block_start[:, None], axis=1),
                               N_EXPERTS - 1).astype(jnp.int32)
    n_active = (pend[-1:] // MOE_BLOCK).astype(jnp.int32)
    is_e = expert[..., None] == jnp.arange(N_EXPERTS, dtype=jnp.int32)
    dest = rank + jnp.sum(jnp.where(is_e, pstart, 0), axis=-1)
    xs_sorted = _dispatch(f_packed, dest, n_blocks * MOE_BLOCK)
    ys = _experts(xs_sorted, block_expert, n_active, w_gate, w_up, w_down, layer)
    return _combine(ys, dest, weight.T, xs, gate3, n_lat_tiles, blocks_per_batch, final_gain)


def _rmsnorm(x, gain):
    y = x * lax.rsqrt(jnp.mean(x * x, axis=-1, keepdims=True) + EPS)
    return y * gain


def _modulate(x, gain, shift, scale):
    return _rmsnorm(x, gain) * (1 + scale) + shift


def _l2norm(t):
    return t * lax.rsqrt(jnp.sum(t * t, axis=-1, keepdims=True) + EPS)


def _short_conv(x, w, on_grid):
    b, l, ch = x.shape
    xs = x.reshape(b, l // GRID_W, GRID_W, ch) if on_grid else x.reshape(b, 1, l, ch)
    n = xs.shape[2]
    xp = jnp.pad(xs, ((0, 0), (0, 0), (1, 1), (0, 0)))
    y = w[0] * xp[:, :, 0:n] + w[1] * xp[:, :, 1:n + 1] + w[2] * xp[:, :, 2:n + 2]
    return y.reshape(b, l, ch)


def _gated_delta_chunked(q, k, v, g, beta, s0):
    b, h, l, dk = q.shape
    dv = v.shape[-1]
    c = DN_CHUNK
    n = l // c
    q = q.reshape(b, h, n, c, dk)
    k = k.reshape(b, h, n, c, dk)
    v = v.reshape(b, h, n, c, dv)
    g = jnp.cumsum(g.reshape(b, h, n, c), axis=-1)
    beta = beta.reshape(b, h, n, c, 1)
    pos = jnp.arange(c)
    incl = pos[:, None] >= pos[None, :]
    strict = pos[:, None] > pos[None, :]
    decay = jnp.exp(jnp.where(incl, g[..., :, None] - g[..., None, :], -jnp.inf))
    kb = k * beta
    a_mat = jnp.einsum('bhnid,bhnjd->bhnij', kb, k) * jnp.where(strict, decay, 0.0)
    rhs = jnp.concatenate([v * beta, kb * jnp.exp(g)[..., None]], axis=-1)
    sol = lax.linalg.triangular_solve(a_mat + jnp.eye(c, dtype=a_mat.dtype), rhs,
                                      left_side=True, lower=True, unit_diagonal=True)
    u, w = sol[..., :dv], sol[..., dv:]
    attn = jnp.einsum('bhnid,bhnjd->bhnij', q, k) * decay
    g_last = g[..., -1:]
    q_dec = q * jnp.exp(g)[..., None]
    k_dec = k * jnp.exp(g_last - g)[..., None]

    def step(s, inp):
        qd, kd, uu, ww, at, gl = inp
        v_new = uu - jnp.einsum('bhck,bhkv->bhcv', ww, s)
        o = jnp.einsum('bhck,bhkv->bhcv', qd, s) + jnp.einsum('bhcs,bhsv->bhcv', at, v_new)
        s = s * jnp.exp(gl)[..., None] + jnp.einsum('bhck,bhcv->bhkv', kd, v_new)
        return s, o

    xs = tuple(jnp.moveaxis(t, 2, 0) for t in (q_dec, k_dec, u, w, attn, g_last))
    s_final, o = lax.scan(step, s0, xs)
    o = jnp.moveaxis(o, 0, 2).reshape(b, h, l, dv)
    return o, s_final


def _deltanet_mixer(p_ctx, p_lat, conv_w, a_log, dt_bias, out_norm):
    d = D_MODEL
    nh = DN_HEADS

    def project(p, on_grid):
        b, l, _ = p.shape
        qkv = jax.nn.silu(_short_conv(p[..., :3 * d], conv_w, on_grid))
        z = p[..., 3 * d:4 * d]
        a = p[..., 4 * d:4 * d + 2 * nh].reshape(b, l, 2, nh)
        bb = p[..., 4 * d + 2 * nh:].reshape(b, l, 2, nh)

        def heads(t):
            return jnp.transpose(t.reshape(b, l, nh, -1), (0, 2, 1, 3))

        q, k, v = (heads(t) for t in jnp.split(qkv, 3, axis=-1))
        q = _l2norm(q) * DN_DK ** -0.5
        k = _l2norm(k)
        g = -jnp.exp(a_log) * jax.nn.softplus(a + dt_bias)
        g = jnp.transpose(g, (2, 0, 3, 1))
        beta = jnp.transpose(jax.nn.sigmoid(bb), (2, 0, 3, 1))
        return q, k, v, g, beta, z

    def scan_both(q, k, v, g, beta, s_f, s_b):
        o_f, s_f = _gated_delta_chunked(q, k, v, g[0], beta[0], s_f)
        rev = lambda t: jnp.flip(t, axis=2)
        o_b, s_b = _gated_delta_chunked(rev(q), rev(k), rev(v), rev(g[1]), rev(beta[1]), s_b)
        return o_f + rev(o_b), s_f, s_b

    def finish(o, z):
        b, _, l, _ = o.shape
        o = jnp.transpose(o, (0, 2, 1, 3))
        o = o * lax.rsqrt(jnp.mean(o * o, axis=-1, keepdims=True) + EPS) * out_norm
        o = o * jax.nn.silu(z.reshape(b, l, nh, DN_DV))
        return o.reshape(b, l, d)

    qc, kc, vc, gc, bc, zc = project(p_ctx, False)
    s0 = jnp.zeros((p_ctx.shape[0], nh, DN_DK, DN_DV), F32)
    o_c, s_f, s_b = scan_both(qc, kc, vc, gc, bc, s0, s0)
    ql, kl, vl, gla, bl, zl = project(p_lat, True)
    o_l, _, _ = scan_both(ql, kl, vl, gla, bl, s_f, s_b)
    return finish(o_c, zc), finish(o_l, zl)


def _hyena_filters(l, w1, b1, freq, w2, b2, w3):
    pos = jnp.arange(l, dtype=F32)[:, None]
    t = pos / max(l - 1, 1)
    bands = jnp.linspace(1e-4, HY_BANDS - 1, HY_BANDS, dtype=F32)[None, :]
    ang = (2 * math.pi / l) * pos * bands
    feat = jnp.concatenate([t, jnp.cos(ang), -jnp.sin(ang)], axis=-1)
    hp = lax.Precision.HIGHEST
    hdn = jnp.sin(freq * (jnp.dot(feat, w1, precision=hp) + b1))
    hdn = jnp.sin(freq * (jnp.dot(hdn, w2, precision=hp) + b2))
    filt = jnp.dot(hdn, w3, precision=hp).reshape(l, HY_ORDER, 2, D_MODEL)
    deltas = jnp.abs(jnp.linspace(math.log(HY_TARGET) / HY_SLOW, math.log(HY_TARGET) / HY_FAST,
                                  D_MODEL, dtype=F32))
    window = jnp.exp(-t * deltas[None, :])
    return filt * window[:, None, None, :]


def _two_sided_fftconv(u, h_fwd, h_bwd):
    l = u.shape[1]
    k = jnp.concatenate([h_fwd, jnp.zeros_like(h_fwd[:1]), jnp.flip(h_bwd[1:], axis=0)], axis=0)
    kf = jnp.fft.rfft(k, axis=0)
    uf = jnp.fft.rfft(u, n=2 * l, axis=1)
    return jnp.fft.irfft(uf * kf[None], n=2 * l, axis=1)[:, :l]


def _hyena_stream(p, on_grid, conv_w, f_w1, f_b1, f_freq, f_w2, f_b2, f_w3, bias):
    l = p.shape[1]
    p = _short_conv(p, conv_w, on_grid)
    v, x1, x2 = jnp.split(p, 3, axis=-1)
    filt = _hyena_filters(l, f_w1, f_b1, f_freq, f_w2, f_b2, f_w3)
    z = v
    for n, gate in enumerate((x1, x2)):
        conv = _two_sided_fftconv(z, filt[:, n, 0], filt[:, n, 1])
        z = gate * (conv + bias[n] * z)
    return z


def _shortconv_stream(p, on_grid, conv_w):
    bg, cg, xin = jnp.split(p, 3, axis=-1)
    return bg * _short_conv(cg * xin, conv_w, on_grid)


def _route(h, w_router, router_bias):
    t = h.shape[0]
    scores = jax.nn.sigmoid(jnp.dot(h, w_router, precision=lax.Precision.HIGHEST))
    choice = (scores + router_bias).reshape(t, N_GROUPS, EXPERTS_PER_GROUP)
    group_score = lax.top_k(choice, GROUP_SCORE_K)[0].sum(-1)
    group = jnp.argmax(group_score, axis=-1)
    in_group = jnp.take_along_axis(choice, group[:, None, None], axis=1)[:, 0]
    local = lax.top_k(in_group, TOP_K)[1]
    expert = group[:, None] * EXPERTS_PER_GROUP + local
    weight = jnp.take_along_axis(scores, expert, axis=1)
    weight = weight / jnp.sum(weight, axis=-1, keepdims=True)
    return expert.astype(jnp.int32), weight


def _moe_ffn(x, w_router, router_bias, w_gate, w_up, w_down):
    t, d = x.shape
    expert, weight = _route(x, w_router, router_bias)
    a = t * TOP_K
    e_flat = expert.reshape(-1)
    order = jnp.argsort(e_flat)
    e_sorted = e_flat[order]
    tok_sorted = (order // TOP_K).astype(jnp.int32)
    counts = jnp.zeros((N_EXPERTS,), jnp.int32).at[e_flat].add(1)
    start = jnp.cumsum(counts) - counts
    padded = (counts + MOE_BLOCK - 1) // MOE_BLOCK * MOE_BLOCK
    pend = jnp.cumsum(padded)
    pstart = pend - padded
    dest = pstart[e_sorted] + (jnp.arange(a, dtype=jnp.int32) - start[e_sorted])
    n_blocks = -(-a // MOE_BLOCK) + N_EXPERTS
    n_slots = n_blocks * MOE_BLOCK
    slot_tok = jnp.full((n_slots,), t, jnp.int32).at[dest].set(tok_sorted)
    block_start = jnp.arange(n_blocks, dtype=jnp.int32) * MOE_BLOCK
    block_expert = jnp.minimum(jnp.searchsorted(pend, block_start, side='right'),
                               N_EXPERTS - 1).astype(jnp.int32)
    x_pad = jnp.concatenate([x.astype(BF16), jnp.zeros((1, d), BF16)], axis=0)
    xs = x_pad[slot_tok]
    ys = _expert_ffn(xs, block_expert, jnp.ones((n_slots,), F32), w_gate, w_up, w_down)
    slot_of = jnp.zeros((a,), jnp.int32).at[order].set(dest).reshape(t, TOP_K)
    out = weight[:, 0:1] * ys[slot_of[:, 0]] + weight[:, 1:2] * ys[slot_of[:, 1]]
    return out


def _sc_layer_kernel(x_ref, gain_ref, shift_ref, scale_ref, win_ref, cw_ref, wout_ref, gate_ref, o_ref, *,
                     n_lat_tiles):
    d = D_MODEL
    tm = x_ref.shape[0]
    seg = jnp.where(pl.program_id(0) >= n_lat_tiles, CTX_LEN, GRID_W)
    pos = lax.broadcasted_iota(jnp.int32, (tm, 1), 0) & (seg - 1)
    hb = _modulated(x_ref[...], gain_ref[...], shift_ref[0], scale_ref[0]).astype(BF16)
    u = (jnp.dot(hb, win_ref[:, d:2 * d], preferred_element_type=F32)
         * jnp.dot(hb, win_ref[:, 2 * d:], preferred_element_type=F32))
    prev = jnp.where(pos != 0, pltpu.roll(u, 1, axis=0), 0.0)
    nxt = jnp.where(pos != seg - 1, pltpu.roll(u, tm - 1, axis=0), 0.0)
    cw = cw_ref[...]
    y = jnp.dot(hb, win_ref[:, :d], preferred_element_type=F32) * (
        cw[0:1] * prev + cw[1:2] * u + cw[2:3] * nxt)
    o_ref[...] = x_ref[...] + gate_ref[0] * jnp.dot(y.astype(BF16), wout_ref[...],
                                                    preferred_element_type=F32)


def _shortconv_layer(xs, gain, shift3, scale3, w_in, conv_w, w_out, gate3, n_lat_tiles, blocks_per_batch):
    t, d = xs.shape
    tm = HY_TOK_TILE
    per_tile = tm // DN_BLOCK
    row = lambda i: (i, 0)
    fixed = lambda i: (0, 0)
    grp = lambda i: (_group_of_block(i * per_tile, n_lat_tiles * per_tile, blocks_per_batch), 0, 0)
    return pl.pallas_call(
        functools.partial(_sc_layer_kernel, n_lat_tiles=n_lat_tiles),
        out_shape=jax.ShapeDtypeStruct((t, d), F32),
        grid=(t // tm,),
        in_specs=[pl.BlockSpec((tm, d), row),
                  pl.BlockSpec((1, d), fixed),
                  pl.BlockSpec((1, 1, d), grp),
                  pl.BlockSpec((1, 1, d), grp),
                  pl.BlockSpec((d, 3 * d), fixed),
                  pl.BlockSpec((3, d), fixed),
                  pl.BlockSpec((d, d), fixed),
                  pl.BlockSpec((1, 1, d), grp)],
        out_specs=pl.BlockSpec((tm, d), row),
        compiler_params=pltpu.CompilerParams(
            dimension_semantics=("arbitrary",), vmem_limit_bytes=VMEM_LIMIT_BYTES),
        name="shortconv_layer",
    )(xs, gain.reshape(1, d), shift3, scale3, w_in.astype(BF16), conv_w, w_out.astype(BF16), gate3)


def kernel(x, c, ctx, c_ctx, ada_w, ada_b, norm_mix, norm_ffn, norm_final, dn_w_in, dn_conv, dn_a_log,
           dn_dt_bias, dn_out_norm, dn_w_out, hy_w_in, hy_conv, hy_f_w1, hy_f_b1, hy_f_freq, hy_f_w2,
           hy_f_b2, hy_f_w3, hy_bias, hy_w_out, sc_w_in, sc_conv, sc_w_out, w_router, router_bias,
           moe_w_gate, moe_w_up, moe_w_down):
    d = D_MODEL
    bsz, seq, _ = x.shape
    n_ctx = bsz * CTX_LEN
    n_lat = bsz * seq
    silu_c = jax.nn.silu(c)
    silu_cc = jax.nn.silu(c_ctx)
    hp = lax.Precision.HIGHEST

    xs = jnp.concatenate([x.reshape(n_lat, d), ctx.reshape(n_ctx, d)], axis=0)
    n_lat_blocks, blocks_per_batch = n_lat // DN_BLOCK, seq // DN_BLOCK

    for i in range(DEPTH):
        kind, j = i % N_MIXERS, i // N_MIXERS
        ml = jnp.split(jnp.dot(silu_c, ada_w[i], precision=hp) + ada_b[i], N_MOD, axis=-1)
        mc = jnp.split(jnp.dot(silu_cc, ada_w[i], precision=hp) + ada_b[i], N_MOD, axis=-1)
        mod = [jnp.concatenate([mc[m][None], ml[m]], axis=0)[:, None, :] for m in range(N_MOD)]
        if kind == 0:
            xs = _deltanet_layer(xs, norm_mix[i], mod[0], mod[1], dn_w_in[j], dn_conv[j], dn_a_log[j],
                                 dn_dt_bias[j], dn_out_norm[j], dn_w_out[j], mod[2], n_lat_blocks,
                                 blocks_per_batch)
        elif kind == 1:
            xs = _hyena_layer(xs, norm_mix[i], mod[0], mod[1], hy_w_in[j], hy_conv[j], hy_f_w1[j], hy_f_b1[j],
                              hy_f_freq[j], hy_f_w2[j], hy_f_b2[j], hy_f_w3[j], hy_bias[j], hy_w_out[j],
                              mod[2], bsz, seq)
        else:
            xs = _shortconv_layer(xs, norm_mix[i], mod[0], mod[1], sc_w_in[j], sc_conv[j], sc_w_out[j],
                                  mod[2], n_lat // HY_TOK_TILE, blocks_per_batch)
        xs = _moe_layer(xs, norm_ffn[i], mod[3], mod[4], mod[5], w_router, router_bias,
                        moe_w_gate, moe_w_up, moe_w_down, i, n_lat // MOE_TILE, blocks_per_batch,
                        norm_final if i == DEPTH - 1 else None)
    return xs.reshape(bsz, seq, d)
```

```python
import functools
import math

import numpy as np
import jax
import jax.numpy as jnp
from jax import lax
from jax.experimental import pallas as pl
from jax.experimental.pallas import tpu as pltpu

D_MODEL = 1024
DEPTH = 4
CTX_LEN = 256
GRID_W = 64
N_MIXERS = 3
EPS = 1e-6
N_MOD = 6

DN_HEADS = 8
DN_DK = D_MODEL // DN_HEADS
DN_DV = D_MODEL // DN_HEADS
DN_CHUNK = 64

HY_ORDER = 2
HY_BANDS = 16
HY_TARGET = 1e-2
HY_FAST = 0.3
HY_SLOW = 1.5

N_EXPERTS = 32
N_GROUPS = 8
EXPERTS_PER_GROUP = N_EXPERTS // N_GROUPS
GROUP_SCORE_K = 2
TOP_K = 2
D_EXPERT = 512
MOE_BLOCK = 512

F32 = jnp.float32
BF16 = jnp.bfloat16

ROW_TILE = 512
VMEM_LIMIT_BYTES = 48 * 1024 * 1024


def _mm_kernel(x_ref, w_ref, o_ref):
    o_ref[...] = jnp.dot(x_ref[...].astype(BF16), w_ref[...], preferred_element_type=F32)


def _mm(x, w, tn=None):
    m, k = x.shape
    n = w.shape[1]
    tm = min(ROW_TILE, m)
    tn = n if tn is None else tn
    assert m % tm == 0 and n % tn == 0
    return pl.pallas_call(
        _mm_kernel,
        out_shape=jax.ShapeDtypeStruct((m, n), F32),
        grid=(m // tm, n // tn),
        in_specs=[pl.BlockSpec((tm, k), lambda i, j: (i, 0)),
                  pl.BlockSpec((k, tn), lambda i, j: (0, j))],
        out_specs=pl.BlockSpec((tm, tn), lambda i, j: (i, j)),
        compiler_params=pltpu.CompilerParams(
            dimension_semantics=("arbitrary", "arbitrary"), vmem_limit_bytes=VMEM_LIMIT_BYTES),
        name="dense_mm",
    )(x, w.astype(BF16))


def _expert_kernel(be_ref, x_ref, wg_ref, wu_ref, wd_ref, sw_ref, o_ref):
    del be_ref
    xb = x_ref[...]
    g = jnp.dot(xb, wg_ref[0], preferred_element_type=F32)
    u = jnp.dot(xb, wu_ref[0], preferred_element_type=F32)
    hid = (g * jax.nn.sigmoid(g)) * u
    y = jnp.dot(hid.astype(BF16), wd_ref[0], preferred_element_type=F32)
    o_ref[...] = y * sw_ref[...]


def _expert_ffn(xs, block_expert, slot_w, w_gate, w_up, w_down):
    n_slots, d = xs.shape
    n_blocks = n_slots // MOE_BLOCK
    grid_spec = pltpu.PrefetchScalarGridSpec(
        num_scalar_prefetch=1,
        grid=(n_blocks,),
        in_specs=[
            pl.BlockSpec((MOE_BLOCK, d), lambda i, be: (i, 0)),
            pl.BlockSpec((1, d, D_EXPERT), lambda i, be: (be[i], 0, 0)),
            pl.BlockSpec((1, d, D_EXPERT), lambda i, be: (be[i], 0, 0)),
            pl.BlockSpec((1, D_EXPERT, d), lambda i, be: (be[i], 0, 0)),
            pl.BlockSpec((MOE_BLOCK, 1), lambda i, be: (i, 0)),
        ],
        out_specs=pl.BlockSpec((MOE_BLOCK, d), lambda i, be: (i, 0)),
    )
    return pl.pallas_call(
        _expert_kernel,
        out_shape=jax.ShapeDtypeStruct((n_slots, d), F32),
        grid_spec=grid_spec,
        compiler_params=pltpu.CompilerParams(
            dimension_semantics=("arbitrary",), vmem_limit_bytes=VMEM_LIMIT_BYTES),
        name="expert_ffn",
    )(block_expert, xs, w_gate.astype(BF16), w_up.astype(BF16), w_down.astype(BF16),
      slot_w.reshape(n_slots, 1))


DN_BLOCK = CTX_LEN
DN_HB = DN_HEADS
N_CHUNKS_PER_BLOCK = DN_BLOCK // DN_CHUNK


def _group_of_block(i, n_lat_blocks, blocks_per_batch):
    return jnp.where(i >= n_lat_blocks, 0, 1 + i // blocks_per_batch)


def _modulated(x, gain, shift, scale):
    y = x * lax.rsqrt(jnp.mean(x * x, axis=-1, keepdims=True) + EPS) * gain
    return y * (1 + scale) + shift


def _dn_inproj_kernel(x_ref, gain_ref, shift_ref, scale_ref, w_ref, wab_ref, cw_ref, alog_ref, dtb_ref,
                      q_ref, k_ref, v_ref, z_ref, gate_ref, *, n_lat_blocks):
    i = pl.program_id(0)
    nrow = DN_BLOCK
    d = D_MODEL
    pair = 2 * DN_DK
    seg = jnp.where(i >= n_lat_blocks, CTX_LEN, GRID_W)
    r = lax.broadcasted_iota(jnp.int32, (nrow, 1), 0)
    pos = r & (seg - 1)
    not_first = pos != 0
    not_last = pos != seg - 1
    h = _modulated(x_ref[...], gain_ref[...], shift_ref[0], scale_ref[0])
    hb = h.astype(BF16)
    outs = (q_ref, k_ref, v_ref)
    for part in range(3):
        for hp in range(d // pair):
            col = part * d + hp * pair
            x = jnp.dot(hb, w_ref[:, col:col + pair], preferred_element_type=F32)
            cw = cw_ref[:, col:col + pair]
            xp = jnp.where(not_first, pltpu.roll(x, 1, axis=0), 0.0)
            xn = jnp.where(not_last, pltpu.roll(x, nrow - 1, axis=0), 0.0)
            y = cw[0:1] * xp + cw[1:2] * x + cw[2:3] * xn
            y = y * jax.nn.sigmoid(y)
            for hh in range(2):
                yh = y[:, hh * DN_DK:(hh + 1) * DN_DK]
                if part < 2:
                    yh = yh * lax.rsqrt(jnp.sum(yh * yh, axis=-1, keepdims=True) + EPS)
                if part == 0:
                    yh = yh * DN_DK ** -0.5
                outs[part][:, hp * pair + hh * DN_DK:hp * pair + (hh + 1) * DN_DK] = yh
    for j in range(d // pair):
        z_ref[:, j * pair:(j + 1) * pair] = jnp.dot(hb, w_ref[:, 3 * d + j * pair:3 * d + (j + 1) * pair],
                                                    preferred_element_type=F32)

    ab = jnp.dot(h, wab_ref[...], precision=lax.Precision.HIGHEST, preferred_element_type=F32)
    nd = 2 * DN_HEADS
    a = ab[:, :nd] + dtb_ref[...]
    softplus = jnp.maximum(a, 0.0) + jnp.log(1.0 + jnp.exp(-jnp.abs(a)))
    g = -jnp.exp(alog_ref[...]) * softplus
    beta = jax.nn.sigmoid(ab[:, nd:])
    cpos = r & (DN_CHUNK - 1)
    gp, gs = g, g
    sh = 1
    while sh < DN_CHUNK:
        gp = gp + jnp.where(cpos >= sh, pltpu.roll(gp, sh, axis=0), 0.0)
        gs = gs + jnp.where(cpos < DN_CHUNK - sh, pltpu.roll(gs, nrow - sh, axis=0), 0.0)
        sh *= 2
    colid = lax.broadcasted_iota(jnp.int32, (1, nd), 1)
    gate_ref[:, :nd] = jnp.where(colid < DN_HEADS, gp, gs)
    gate_ref[:, nd:] = beta


def _dn_inproj(xs, gain, shift3, scale3, w_in, conv_w, a_log, dt_bias, n_lat_blocks, blocks_per_batch):
    t, d = xs.shape
    nd = 2 * DN_HEADS
    row = lambda i: (i, 0)
    fixed = lambda i: (0, 0)
    grp = lambda i: (_group_of_block(i, n_lat_blocks, blocks_per_batch), 0, 0)
    return pl.pallas_call(
        functools.partial(_dn_inproj_kernel, n_lat_blocks=n_lat_blocks),
        out_shape=[jax.ShapeDtypeStruct((t, d), F32)] * 4 + [jax.ShapeDtypeStruct((t, 2 * nd), F32)],
        grid=(t // DN_BLOCK,),
        in_specs=[pl.BlockSpec((DN_BLOCK, d), row),
                  pl.BlockSpec((1, d), fixed),
                  pl.BlockSpec((1, 1, d), grp),
                  pl.BlockSpec((1, 1, d), grp),
                  pl.BlockSpec((d, 4 * d), fixed),
                  pl.BlockSpec((d, 2 * nd), fixed),
                  pl.BlockSpec((3, 3 * d), fixed),
                  pl.BlockSpec((1, nd), fixed),
                  pl.BlockSpec((1, nd), fixed)],
        out_specs=[pl.BlockSpec((DN_BLOCK, d), row)] * 4 + [pl.BlockSpec((DN_BLOCK, 2 * nd), row)],
        compiler_params=pltpu.CompilerParams(
            dimension_semantics=("arbitrary",), vmem_limit_bytes=VMEM_LIMIT_BYTES),
        name="dn_inproj",
    )(xs, gain.reshape(1, d), shift3, scale3, w_in[:, :4 * d].astype(BF16), w_in[:, 4 * d:], conv_w,
      a_log.reshape(1, nd), dt_bias.reshape(1, nd))


def _dotb(a, b):
    return jnp.dot(a.astype(BF16), b.astype(BF16), preferred_element_type=F32)


def _dotb_nt(a, b):
    return lax.dot_general(a.astype(BF16), b.astype(BF16), (((1,), (1,)), ((), ())),
                           preferred_element_type=F32)


def _dotb_tn(a, b):
    return lax.dot_general(a.astype(BF16), b.astype(BF16), (((0,), (0,)), ((), ())),
                           preferred_element_type=F32)


def _unit_tri_inverses(mats, ii, jj):
    eye = (ii == jj).astype(F32)
    diag8 = (ii >> 3) == (jj >> 3)
    n = [-jnp.where(diag8, a, 0.0) for a in mats]
    n2 = [_dotb(x, x) for x in n]
    m = [eye + x for x in n]
    m = [x + _dotb(x, y) for x, y in zip(m, n2)]
    n4 = [_dotb(x, x) for x in n2]
    m = [x + _dotb(x, y) for x, y in zip(m, n4)]
    sh = 3
    while (1 << sh) < DN_CHUNK:
        off = ((ii >> (sh + 1)) == (jj >> (sh + 1))) & ((ii >> sh) != (jj >> sh))
        cm = [_dotb(jnp.where(off, a, 0.0), x) for a, x in zip(mats, m)]
        m = [x - _dotb(x, y) for x, y in zip(m, cm)]
        sh += 1
    return m


def _dn_scan_kernel(qf_ref, kf_ref, vf_ref, gcf_ref, grf_ref, qb_ref, kb_ref, vb_ref, gcb_ref, grb_ref,
                    of_ref, ob_ref, s_ref):
    @pl.when(pl.program_id(2) == 0)
    def _():
        s_ref[...] = jnp.zeros_like(s_ref)

    c = DN_CHUNK
    ncb = N_CHUNKS_PER_BLOCK
    ii = lax.broadcasted_iota(jnp.int32, (c, c), 0)
    jj = lax.broadcasted_iota(jnp.int32, (c, c), 1)
    incl = (ii >= jj, ii <= jj)
    strict = (ii > jj, ii < jj)
    dirs = ((qf_ref, kf_ref, vf_ref, gcf_ref, grf_ref, of_ref),
            (qb_ref, kb_ref, vb_ref, gcb_ref, grb_ref, ob_ref))
    items = [(d, hh, ci) for d in range(2) for hh in range(DN_HB) for ci in range(ncb)]

    def rows(ci):
        return slice(ci * c, (ci + 1) * c)

    def cols(hh):
        return slice(hh * DN_DK, (hh + 1) * DN_DK)

    q = [dirs[d][0][rows(ci), cols(hh)] for d, hh, ci in items]
    k = [dirs[d][1][rows(ci), cols(hh)] for d, hh, ci in items]
    v = [dirs[d][2][rows(ci), cols(hh)] for d, hh, ci in items]
    gc = [dirs[d][3][hh, rows(ci), d:d + 1] for d, hh, ci in items]
    gr = [dirs[d][4][hh, d:d + 1, rows(ci)] for d, hh, ci in items]
    beta = [dirs[d][3][hh, rows(ci), 2 + d:3 + d] for d, hh, ci in items]

    decay = [jnp.where(incl[it[0]], jnp.exp(jnp.where(incl[it[0]], x - y, 0.0)), 0.0)
             for it, x, y in zip(items, gc, gr)]
    kb = [x * y for x, y in zip(k, beta)]
    a = [_dotb_nt(x, y) * jnp.where(strict[it[0]], z, 0.0) for it, x, y, z in zip(items, kb, k, decay)]
    attn = [_dotb_nt(x, y) * z for x, y, z in zip(q, k, decay)]
    tinv = _unit_tri_inverses(a, ii, jj)
    eg = [jnp.exp(x) for x in gc]
    uw = [_dotb(t, jnp.concatenate([x * y, z * e], axis=-1))
          for t, x, y, z, e in zip(tinv, v, beta, kb, eg)]
    g_last = [x[0:1] if it[0] else x[c - 1:c] for it, x in zip(items, gc)]
    wq = [jnp.concatenate([x[:, DN_DV:], y * e], axis=0) for x, y, e in zip(uw, q, eg)]
    k_dec = [x * jnp.exp(y - z) for x, y, z in zip(k, g_last, gc)]
    s_dec = [jnp.exp(x) for x in g_last]

    chains = [(d, hh) for d in range(2) for hh in range(DN_HB)]
    state = [s_ref[d, hh] for d, hh in chains]
    for step in range(ncb):
        cur = [items.index((d, hh, ncb - 1 - step if d else step)) for d, hh in chains]
        ws = [_dotb(wq[n], s) for n, s in zip(cur, state)]
        v_new = [uw[n][:, :DN_DV] - x[:c] for n, x in zip(cur, ws)]
        o = [x[c:] + _dotb(attn[n], y) for n, x, y in zip(cur, ws, v_new)]
        state = [s * s_dec[n] + _dotb_tn(k_dec[n], y) for n, s, y in zip(cur, state, v_new)]
        for n, x in zip(cur, o):
            d, hh, ci = items[n]
            dirs[d][5][rows(ci), cols(hh)] = x
    for (d, hh), s in zip(chains, state):
        s_ref[d, hh] = s


def _dn_scan(q, k, v, gates, n_lat_blocks, blocks_per_batch):
    t, d = q.shape
    bsz = n_lat_blocks // blocks_per_batch
    g4 = gates.reshape(t, 4, DN_HEADS)
    gcol = jnp.transpose(g4, (2, 0, 1))
    grow = jnp.transpose(g4, (2, 1, 0))

    def blk_f(b, s):
        return jnp.where(s == 0, n_lat_blocks + b, b * blocks_per_batch + s - 1)

    def blk_b(b, s):
        return jnp.where(s == 0, n_lat_blocks + b, b * blocks_per_batch + blocks_per_batch - s)

    hw = DN_HB * DN_DK

    def specs(blk):
        return [pl.BlockSpec((DN_BLOCK, hw), lambda b, hg, s: (blk(b, s), hg))] * 3 + [
            pl.BlockSpec((DN_HB, DN_BLOCK, 4), lambda b, hg, s: (hg, blk(b, s), 0)),
            pl.BlockSpec((DN_HB, 4, DN_BLOCK), lambda b, hg, s: (hg, 0, blk(b, s)))]

    return pl.pallas_call(
        _dn_scan_kernel,
        out_shape=[jax.ShapeDtypeStruct((t, d), F32)] * 2,
        grid=(bsz, DN_HEADS // DN_HB, 1 + blocks_per_batch),
        in_specs=specs(blk_f) + specs(blk_b),
        out_specs=[pl.BlockSpec((DN_BLOCK, hw), lambda b, hg, s: (blk_f(b, s), hg)),
                   pl.BlockSpec((DN_BLOCK, hw), lambda b, hg, s: (blk_b(b, s), hg))],
        scratch_shapes=[pltpu.VMEM((2, DN_HB, DN_DK, DN_DV), F32)],
        compiler_params=pltpu.CompilerParams(
            dimension_semantics=("arbitrary", "arbitrary", "arbitrary"),
            vmem_limit_bytes=VMEM_LIMIT_BYTES),
        name="dn_scan",
    )(q, k, v, gcol, grow, q, k, v, gcol, grow)


def _dn_out_kernel(of_ref, ob_ref, z_ref, on_ref, w_ref, x_ref, gate_ref, o_ref):
    z = z_ref[...]
    zs = z * jax.nn.sigmoid(z)
    parts = []
    for h in range(DN_HEADS):
        cols = slice(h * DN_DV, (h + 1) * DN_DV)
        o = of_ref[:, cols] + ob_ref[:, cols]
        o = o * lax.rsqrt(jnp.mean(o * o, axis=-1, keepdims=True) + EPS)
        parts.append(o)
    y = jnp.concatenate(parts, axis=-1) * on_ref[...] * zs
    o_ref[...] = x_ref[...] + gate_ref[0] * jnp.dot(y.astype(BF16), w_ref[...],
                                                    preferred_element_type=F32)


def _dn_out(o_f, o_b, z, out_norm, w_out, xs, gate3, n_lat_blocks, blocks_per_batch):
    t, d = xs.shape
    row = lambda i: (i, 0)
    fixed = lambda i: (0, 0)
    grp = lambda i: (_group_of_block(i, n_lat_blocks, blocks_per_batch), 0, 0)
    return pl.pallas_call(
        _dn_out_kernel,
        out_shape=jax.ShapeDtypeStruct((t, d), F32),
        grid=(t // DN_BLOCK,),
        in_specs=[pl.BlockSpec((DN_BLOCK, d), row),
                  pl.BlockSpec((DN_BLOCK, d), row),
                  pl.BlockSpec((DN_BLOCK, d), row),
                  pl.BlockSpec((1, d), fixed),
                  pl.BlockSpec((d, d), fixed),
                  pl.BlockSpec((DN_BLOCK, d), row),
                  pl.BlockSpec((1, 1, d), grp)],
        out_specs=pl.BlockSpec((DN_BLOCK, d), row),
        compiler_params=pltpu.CompilerParams(
            dimension_semantics=("arbitrary",), vmem_limit_bytes=VMEM_LIMIT_BYTES),
        name="dn_out",
    )(o_f, o_b, z, jnp.tile(out_norm, DN_HEADS).reshape(1, d), w_out.astype(BF16), xs, gate3)


def _deltanet_layer(xs, gain, shift3, scale3, w_in, conv_w, a_log, dt_bias, out_norm, w_out, gate3,
                    n_lat_blocks, blocks_per_batch):
    q, k, v, z, gates = _dn_inproj(xs, gain, shift3, scale3, w_in, conv_w, a_log, dt_bias, n_lat_blocks,
                                   blocks_per_batch)
    o_f, o_b = _dn_scan(q, k, v, gates, n_lat_blocks, blocks_per_batch)
    return _dn_out(o_f, o_b, z, out_norm, w_out, xs, gate3, n_lat_blocks, blocks_per_batch)


HY_N2 = 256
HY_CB = 8
HY_TOK_TILE = 512
HY_FILT_TILE = 512


def _dft_constants(nr):
    n1, n2 = 2 * nr, HY_N2
    n = n1 * n2
    a1 = np.arange(n1)
    f1 = np.exp(-2j * np.pi * np.outer(a1, a1) / n1)
    lhs_fwd = np.concatenate([f1.real[:, :nr], f1.imag[:, :nr]], axis=0)
    lhs_inv = np.concatenate([f1.real[:nr, :], f1.imag[:nr, :]], axis=1) / n
    tw = np.exp(-2j * np.pi * np.outer(a1, np.arange(n2)) / n)
    a2 = np.arange(n2)
    f2 = np.exp(-2j * np.pi * np.outer(a2, a2) / n2)
    w_fwd = np.block([[f2.real, f2.imag], [-f2.imag, f2.real]])
    w_inv = np.block([[f2.real, -f2.imag], [f2.imag, f2.real]])
    return (jnp.asarray(lhs_fwd, BF16), jnp.asarray(lhs_inv, BF16), jnp.asarray(tw.real, F32),
            jnp.asarray(tw.imag, F32), jnp.asarray(w_fwd, BF16), jnp.asarray(w_inv, BF16))


def _hy_dft(x3, lhs_fwd, twr, twi, w_fwd):
    n1 = twr.shape[0]
    a = [jnp.dot(lhs_fwd, x3[c].astype(BF16), preferred_element_type=F32) for c in range(x3.shape[0])]
    br = jnp.concatenate([t[:n1] * twr - t[n1:] * twi for t in a], axis=0)
    bi = jnp.concatenate([t[:n1] * twi + t[n1:] * twr for t in a], axis=0)
    b = jnp.concatenate([br, bi], axis=1)
    return jnp.dot(b.astype(BF16), w_fwd, preferred_element_type=F32)


def _hy_idft(p, cb, lhs_inv, twr, twi, w_inv):
    n1, n2 = twr.shape
    c = jnp.dot(p.astype(BF16), w_inv, preferred_element_type=F32)
    out = []
    for ch in range(cb):
        cr = c[ch * n1:(ch + 1) * n1, :n2]
        ci = c[ch * n1:(ch + 1) * n1, n2:]
        d = jnp.concatenate([cr * twr + ci * twi, ci * twr - cr * twi], axis=0)
        out.append(jnp.dot(lhs_inv, d.astype(BF16), preferred_element_type=F32))
    return out


def _hy_spectrum_kernel(hf_ref, hb_ref, lf_ref, twr_ref, twi_ref, wf_ref, o_ref):
    cb, nr, n2 = hf_ref.shape
    first = ((lax.broadcasted_iota(jnp.int32, (nr, n2), 0) == 0)
             & (lax.broadcasted_iota(jnp.int32, (nr, n2), 1) == 0))
    hb = jnp.where(first, 0.0, hb_ref[...])
    consts = (lf_ref[...], twr_ref[...], twi_ref[...], wf_ref[...])
    xf = _hy_dft(hf_ref[...], *consts)
    xb = _hy_dft(hb, *consts)
    n1 = 2 * nr
    o_ref[...] = jnp.concatenate([xf[:, :n2] + xb[:, :n2], xf[:, n2:] - xb[:, n2:]],
                                 axis=1).reshape(cb, n1, 2 * n2)


def _hy_spectrum(filt, consts):
    d = D_MODEL
    l = filt.shape[1]
    nr = l // HY_N2
    n1 = 2 * nr
    lhs_fwd, _, twr, twi, w_fwd, _ = consts
    cpo = d // HY_CB
    fixed2 = lambda o, c: (0, 0)
    return pl.pallas_call(
        _hy_spectrum_kernel,
        out_shape=jax.ShapeDtypeStruct((HY_ORDER * d, n1, 2 * HY_N2), F32),
        grid=(HY_ORDER, cpo),
        in_specs=[pl.BlockSpec((HY_CB, nr, HY_N2), lambda o, c: (2 * o * cpo + c, 0, 0)),
                  pl.BlockSpec((HY_CB, nr, HY_N2), lambda o, c: ((2 * o + 1) * cpo + c, 0, 0)),
                  pl.BlockSpec(lhs_fwd.shape, fixed2),
                  pl.BlockSpec(twr.shape, fixed2),
                  pl.BlockSpec(twi.shape, fixed2),
                  pl.BlockSpec(w_fwd.shape, fixed2)],
        out_specs=pl.BlockSpec((HY_CB, n1, 2 * HY_N2), lambda o, c: (o * cpo + c, 0, 0)),
        compiler_params=pltpu.CompilerParams(
            dimension_semantics=("arbitrary", "arbitrary"), vmem_limit_bytes=VMEM_LIMIT_BYTES),
        name="hy_spectrum",
    )(filt.reshape(-1, nr, HY_N2), filt.reshape(-1, nr, HY_N2), lhs_fwd, twr, twi, w_fwd)


def _hy_conv_kernel(z_ref, g_ref, k_ref, bias_ref, lf_ref, li_ref, twr_ref, twi_ref, wf_ref, wi_ref,
                    o_ref):
    cb, nr, n2 = z_ref.shape
    twr, twi = twr_ref[...], twi_ref[...]
    z = z_ref[...]
    x = _hy_dft(z, lf_ref[...], twr, twi, wf_ref[...])
    kk = k_ref[...].reshape(x.shape)
    xr, xi, kr, ki = x[:, :n2], x[:, n2:], kk[:, :n2], kk[:, n2:]
    p = jnp.concatenate([xr * kr - xi * ki, xr * ki + xi * kr], axis=1)
    conv = _hy_idft(p, cb, li_ref[...], twr, twi, wi_ref[...])
    for c in range(cb):
        o_ref[c] = g_ref[c] * (conv[c] + bias_ref[c] * z[c])


def _hy_conv(z, z_part, gate, gate_part, khat, order, bias, consts, bsz):
    d = D_MODEL
    l = z.shape[1] // bsz
    nr = l // HY_N2
    n1 = 2 * nr
    cpo = d // HY_CB
    lhs_fwd, lhs_inv, twr, twi, w_fwd, w_inv = consts
    fixed2 = lambda c, b: (0, 0)
    out = pl.pallas_call(
        _hy_conv_kernel,
        out_shape=jax.ShapeDtypeStruct((d, bsz * nr, HY_N2), F32),
        grid=(cpo, bsz),
        in_specs=[pl.BlockSpec((HY_CB, nr, HY_N2), lambda c, b: (z_part * cpo + c, b, 0)),
                  pl.BlockSpec((HY_CB, nr, HY_N2), lambda c, b: (gate_part * cpo + c, b, 0)),
                  pl.BlockSpec((HY_CB, n1, 2 * HY_N2), lambda c, b: (order * cpo + c, 0, 0)),
                  pl.BlockSpec((HY_CB, 1, 1), lambda c, b: (c, 0, 0)),
                  pl.BlockSpec(lhs_fwd.shape, fixed2),
                  pl.BlockSpec(lhs_inv.shape, fixed2),
                  pl.BlockSpec(twr.shape, fixed2),
                  pl.BlockSpec(twi.shape, fixed2),
                  pl.BlockSpec(w_fwd.shape, fixed2),
                  pl.BlockSpec(w_inv.shape, fixed2)],
        out_specs=pl.BlockSpec((HY_CB, nr, HY_N2), lambda c, b: (c, b, 0)),
        compiler_params=pltpu.CompilerParams(
            dimension_semantics=("arbitrary", "arbitrary"), vmem_limit_bytes=VMEM_LIMIT_BYTES),
        name="hy_conv",
    )(z.reshape(-1, bsz * nr, HY_N2), gate.reshape(-1, bsz * nr, HY_N2), khat,
      bias.reshape(d, 1, 1), lhs_fwd, lhs_inv, twr, twi, w_fwd, w_inv)
    return out.reshape(d, bsz * l)


def _hy_ctx_kernel(z_ref, g_ref, hf_ref, hb_ref, bias_ref, wf_ref, wi_ref, o_ref, *, bsz):
    l = hf_ref.shape[1]
    wf, wi = wf_ref[...], wi_ref[...]
    hb = jnp.where(lax.broadcasted_iota(jnp.int32, (1, l), 1) == 0, 0.0, hb_ref[...])
    kf = jnp.dot(hf_ref[...].astype(BF16), wf, preferred_element_type=F32)
    kb = jnp.dot(hb.astype(BF16), wf, preferred_element_type=F32)
    n = 2 * l
    kr, ki = kf[:, :n] + kb[:, :n], kf[:, n:] - kb[:, n:]
    bias = bias_ref[...]
    for b in range(bsz):
        z = z_ref[:, b * l:(b + 1) * l]
        x = jnp.dot(z.astype(BF16), wf, preferred_element_type=F32)
        xr, xi = x[:, :n], x[:, n:]
        p = jnp.concatenate([xr * kr - xi * ki, xr * ki + xi * kr], axis=1)
        conv = jnp.dot(p.astype(BF16), wi, preferred_element_type=F32)
        o_ref[:, b * l:(b + 1) * l] = g_ref[:, b * l:(b + 1) * l] * (conv + bias * z)


def _hy_ctx(z, z_part, gate, gate_part, filt, order, bias, bsz):
    d = D_MODEL
    l = filt.shape[1]
    n = 2 * l
    ang = 2 * np.pi * np.outer(np.arange(l), np.arange(n)) / n
    w_fwd = jnp.asarray(np.concatenate([np.cos(ang), -np.sin(ang)], axis=1), BF16)
    w_inv = jnp.asarray(np.concatenate([np.cos(ang.T), -np.sin(ang.T)], axis=0) / n, BF16)
    cb = 256
    nblk = d // cb
    fixed = lambda c: (0, 0)
    return pl.pallas_call(
        functools.partial(_hy_ctx_kernel, bsz=bsz),
        out_shape=jax.ShapeDtypeStruct((d, bsz * l), F32),
        grid=(nblk,),
        in_specs=[pl.BlockSpec((cb, bsz * l), lambda c: (z_part * nblk + c, 0)),
                  pl.BlockSpec((cb, bsz * l), lambda c: (gate_part * nblk + c, 0)),
                  pl.BlockSpec((cb, l), lambda c: (2 * order * nblk + c, 0)),
                  pl.BlockSpec((cb, l), lambda c: ((2 * order + 1) * nblk + c, 0)),
                  pl.BlockSpec((cb, 1), lambda c: (c, 0)),
                  pl.BlockSpec(w_fwd.shape, fixed),
                  pl.BlockSpec(w_inv.shape, fixed)],
        out_specs=pl.BlockSpec((cb, bsz * l), lambda c: (c, 0)),
        compiler_params=pltpu.CompilerParams(
            dimension_semantics=("arbitrary",), vmem_limit_bytes=VMEM_LIMIT_BYTES),
        name="hy_ctx_conv",
    )(z, gate, filt, filt, bias.reshape(d, 1), w_fwd, w_inv)


def _hy_filter_kernel(band_ref, w1t_ref, w1c_ref, w1s_ref, b1_ref, fr_ref, w2_ref, b2_ref, w3_ref,
                      delta_ref, o_ref, *, l):
    tl = o_ref.shape[1]
    d = D_MODEL
    hp = lax.Precision.HIGHEST
    pos = (lax.broadcasted_iota(jnp.int32, (1, tl), 1) + pl.program_id(0) * tl).astype(F32)
    t = pos / max(l - 1, 1)
    ang = ((2 * math.pi / l) * pos) * band_ref[...]
    fr = fr_ref[...]
    pre = (w1t_ref[...] * t + jnp.dot(w1c_ref[...], jnp.cos(ang), precision=hp)
           + jnp.dot(w1s_ref[...], -jnp.sin(ang), precision=hp) + b1_ref[...])
    hdn = jnp.sin(fr * pre)
    hdn = jnp.sin(fr * (jnp.dot(w2_ref[...], hdn, precision=hp) + b2_ref[...]))
    window = jnp.exp(-t * delta_ref[...])
    for part in range(2 * HY_ORDER):
        rows = slice(part * d, (part + 1) * d)
        o_ref[rows, :] = jnp.dot(w3_ref[rows, :], hdn, precision=hp) * window


def _hy_filter(l, w1, b1, freq, w2, b2, w3):
    d = D_MODEL
    nb = HY_BANDS
    tl = min(HY_FILT_TILE, l)
    col = lambda v: v.reshape(-1, 1)
    bands = jnp.linspace(1e-4, nb - 1, nb, dtype=F32)
    deltas = jnp.abs(jnp.linspace(math.log(HY_TARGET) / HY_SLOW, math.log(HY_TARGET) / HY_FAST, d, dtype=F32))
    w1t = w1.T
    args = (col(bands), w1t[:, 0:1], w1t[:, 1:1 + nb], w1t[:, 1 + nb:], col(b1), col(freq), w2.T, col(b2),
            w3.T, col(deltas))
    return pl.pallas_call(
        functools.partial(_hy_filter_kernel, l=l),
        out_shape=jax.ShapeDtypeStruct((2 * HY_ORDER * d, l), F32),
        grid=(l // tl,),
        in_specs=[pl.BlockSpec(a.shape, lambda j: (0, 0)) for a in args],
        out_specs=pl.BlockSpec((2 * HY_ORDER * d, tl), lambda j: (0, j)),
        compiler_params=pltpu.CompilerParams(
            dimension_semantics=("arbitrary",), vmem_limit_bytes=VMEM_LIMIT_BYTES),
        name="hy_filter",
    )(*args)


def _hy_inproj_kernel(x_ref, gain_ref, shift_ref, scale_ref, wt_ref, cw_ref, o_ref, *, seg):
    nch = wt_ref.shape[0]
    tm = x_ref.shape[0]
    hb = _modulated(x_ref[...], gain_ref[...], shift_ref[0], scale_ref[0]).astype(BF16)
    pos = lax.broadcasted_iota(jnp.int32, (1, tm), 1) & (seg - 1)
    not_first = pos != 0
    not_last = pos != seg - 1
    sub = 512
    for j in range(nch // sub):
        rows = slice(j * sub, (j + 1) * sub)
        p = _dotb_nt(wt_ref[rows, :], hb)
        cw = cw_ref[rows, :]
        prev = jnp.where(not_first, pltpu.roll(p, 1, axis=1), 0.0)
        nxt = jnp.where(not_last, pltpu.roll(p, tm - 1, axis=1), 0.0)
        o_ref[rows, :] = cw[:, 0:1] * prev + cw[:, 1:2] * p + cw[:, 2:3] * nxt


def _hy_inproj(xs, gain, shift3, scale3, w_in, conv_w, first_tile, n_tiles, seg, n_lat_blocks,
               blocks_per_batch):
    k = xs.shape[1]
    nch = w_in.shape[1]
    tm = HY_TOK_TILE
    per_tile = tm // DN_BLOCK
    grp = lambda i: (_group_of_block((first_tile + i) * per_tile, n_lat_blocks, blocks_per_batch), 0, 0)
    return pl.pallas_call(
        functools.partial(_hy_inproj_kernel, seg=seg),
        out_shape=jax.ShapeDtypeStruct((nch, n_tiles * tm), F32),
        grid=(n_tiles,),
        in_specs=[pl.BlockSpec((tm, k), lambda i: (first_tile + i, 0)),
                  pl.BlockSpec((1, k), lambda i: (0, 0)),
                  pl.BlockSpec((1, 1, k), grp),
                  pl.BlockSpec((1, 1, k), grp),
                  pl.BlockSpec((nch, k), lambda i: (0, 0)),
                  pl.BlockSpec((nch, 3), lambda i: (0, 0))],
        out_specs=pl.BlockSpec((nch, tm), lambda i: (0, i)),
        compiler_params=pltpu.CompilerParams(
            dimension_semantics=("arbitrary",), vmem_limit_bytes=VMEM_LIMIT_BYTES),
        name="hy_inproj",
    )(xs, gain.reshape(1, k), shift3, scale3, w_in.T.astype(BF16), conv_w.T)


def _hy_out_kernel(zl_ref, zc_ref, w_ref, x_ref, gate_ref, o_ref, *, n_lat_tiles):
    z = jnp.where(pl.program_id(0) >= n_lat_tiles, zc_ref[...], zl_ref[...])
    o_ref[...] = x_ref[...] + gate_ref[0] * _dotb_tn(z, w_ref[...])


def _hy_out(z_lat, z_ctx, w_out, xs, gate3, blocks_per_batch):
    t, d = xs.shape
    tm = HY_TOK_TILE
    n_lat_tiles = z_lat.shape[1] // tm
    per_tile = tm // DN_BLOCK
    grp = lambda i: (_group_of_block(i * per_tile, n_lat_tiles * per_tile, blocks_per_batch), 0, 0)
    return pl.pallas_call(
        functools.partial(_hy_out_kernel, n_lat_tiles=n_lat_tiles),
        out_shape=jax.ShapeDtypeStruct((t, d), F32),
        grid=(t // tm,),
        in_specs=[pl.BlockSpec((d, tm), lambda i: (0, jnp.minimum(i, n_lat_tiles - 1))),
                  pl.BlockSpec((d, tm), lambda i: (0, 0)),
                  pl.BlockSpec((d, d), lambda i: (0, 0)),
                  pl.BlockSpec((tm, d), lambda i: (i, 0)),
                  pl.BlockSpec((1, 1, d), grp)],
        out_specs=pl.BlockSpec((tm, d), lambda i: (i, 0)),
        compiler_params=pltpu.CompilerParams(
            dimension_semantics=("arbitrary",), vmem_limit_bytes=VMEM_LIMIT_BYTES),
        name="hy_out",
    )(z_lat, z_ctx, w_out.astype(BF16), xs, gate3)


def _hyena_layer(xs, gain, shift3, scale3, w_in, conv_w, f_w1, f_b1, f_freq, f_w2, f_b2, f_w3, bias, w_out,
                 gate3, bsz, seq):
    n_lat_tiles = bsz * seq // HY_TOK_TILE
    assert bsz * CTX_LEN == HY_TOK_TILE
    margs = (xs, gain, shift3, scale3, w_in, conv_w)
    blocks = (bsz * seq // DN_BLOCK, seq // DN_BLOCK)
    p_lat = _hy_inproj(*margs, 0, n_lat_tiles, GRID_W, *blocks)
    p_ctx = _hy_inproj(*margs, n_lat_tiles, 1, CTX_LEN, *blocks)
    fargs = (f_w1, f_b1, f_freq, f_w2, f_b2, f_w3)
    consts = _dft_constants(seq // HY_N2)
    khat = _hy_spectrum(_hy_filter(seq, *fargs), consts)
    filt_ctx = _hy_filter(CTX_LEN, *fargs)
    z_lat, z_ctx = p_lat, p_ctx
    for n in range(HY_ORDER):
        z_lat = _hy_conv(z_lat, 0, p_lat, n + 1, khat, n, bias[n], consts, bsz)
        z_ctx = _hy_ctx(z_ctx, 0, p_ctx, n + 1, filt_ctx, n, bias[n], bsz)
    return _hy_out(z_lat, z_ctx, w_out, xs, gate3, seq // DN_BLOCK)


MOE_TILE = 512
PACK = 2


def _route_kernel(x_ref, gain_ref, shift_ref, scale_ref, wr_ref, rb_ref, tri_ref,
                  f_ref, e_ref, w_ref, r_ref, cnt_ref, carry_ref):
    tm, d = x_ref.shape
    ne, epg, ng = N_EXPERTS, EXPERTS_PER_GROUP, N_GROUPS

    @pl.when(pl.program_id(0) == 0)
    def _():
        carry_ref[...] = jnp.zeros_like(carry_ref)

    x = x_ref[...]
    y = x * lax.rsqrt(jnp.mean(x * x, axis=-1, keepdims=True) + EPS) * gain_ref[...]
    f = y * (1 + scale_ref[0]) + shift_ref[0]
    bits = pltpu.bitcast(f.astype(BF16).astype(F32), jnp.uint32)
    half = d // PACK
    f_ref[...] = (bits[:, :half] >> 16) | (bits[:, half:] & jnp.uint32(0xFFFF0000))

    logits = lax.dot_general(wr_ref[...], f, (((1,), (1,)), ((), ())),
                             precision=lax.Precision.HIGHEST, preferred_element_type=F32)
    scores = jax.nn.sigmoid(logits)
    biased = scores + rb_ref[...]
    s = [scores[j * ng:(j + 1) * ng] for j in range(epg)]
    c = [biased[j * ng:(j + 1) * ng] for j in range(epg)]
    hi01, lo01 = jnp.maximum(c[0], c[1]), jnp.minimum(c[0], c[1])
    hi23, lo23 = jnp.maximum(c[2], c[3]), jnp.minimum(c[2], c[3])
    gscore = jnp.maximum(hi01, hi23) + jnp.maximum(jnp.minimum(hi01, hi23), jnp.maximum(lo01, lo23))
    gi = lax.broadcasted_iota(jnp.int32, (ng, tm), 0)
    gmax = jnp.max(gscore, axis=0, keepdims=True)
    grp = jnp.min(jnp.where(gscore == gmax, gi, ng), axis=0, keepdims=True)
    sel = gi == grp
    cv = [jnp.sum(jnp.where(sel, t, 0.0), axis=0, keepdims=True) for t in c]
    sv = [jnp.sum(jnp.where(sel, t, 0.0), axis=0, keepdims=True) for t in s]

    def pick(excluded):
        best = jnp.full((1, tm), -jnp.inf, F32)
        idx = jnp.zeros((1, tm), jnp.int32)
        val = jnp.zeros((1, tm), F32)
        for j in range(epg):
            cand = cv[j] if excluded is None else jnp.where(excluded == j, -jnp.inf, cv[j])
            take = cand > best
            best = jnp.where(take, cand, best)
            idx = jnp.where(take, j, idx)
            val = jnp.where(take, sv[j], val)
        return idx, val

    i1, v1 = pick(None)
    i2, v2 = pick(i1)
    e1 = grp * epg + i1
    e2 = grp * epg + i2
    wsum = v1 + v2
    e_ref[0:1, :] = e1
    e_ref[1:2, :] = e2
    w_ref[0:1, :] = v1 / wsum
    w_ref[1:2, :] = v2 / wsum

    ei = lax.broadcasted_iota(jnp.int32, (ne, tm), 0)
    oh1 = ei == e1
    oh2 = ei == e2
    tri = tri_ref[...]
    pre1 = jnp.dot(oh1.astype(BF16), tri, preferred_element_type=F32)
    pre2 = jnp.dot(oh2.astype(BF16), tri, preferred_element_type=F32)
    tot1 = pre1[:, tm - 1:tm]
    tot2 = pre2[:, tm - 1:tm]
    carry = carry_ref[:, 0:1]
    r1 = jnp.sum(jnp.where(oh1, carry + pre1 - 1.0, 0.0), axis=0, keepdims=True)
    r2 = jnp.sum(jnp.where(oh2, carry + tot1 + pre2 - 1.0, 0.0), axis=0, keepdims=True)
    r_ref[0:1, :] = r1.astype(jnp.int32)
    r_ref[1:2, :] = r2.astype(jnp.int32)
    carry_ref[...] = carry_ref[...] + (tot1 + tot2)
    cnt_ref[...] = carry_ref[...]


def _moe_route(xs, gain, shift3, scale3, w_router, router_bias, n_lat_tiles, blocks_per_batch):
    t, d = xs.shape
    tm = MOE_TILE
    per_tile = tm // DN_BLOCK
    row = lambda i: (i, 0)
    col = lambda i: (0, i)
    fixed = lambda i: (0, 0)
    grp = lambda i: (_group_of_block(i * per_tile, n_lat_tiles * per_tile, blocks_per_batch), 0, 0)
    tri = jnp.asarray(np.triu(np.ones((tm, tm), np.float32)), BF16)
    perm = np.arange(N_EXPERTS).reshape(N_GROUPS, EXPERTS_PER_GROUP).T.reshape(-1)
    return pl.pallas_call(
        _route_kernel,
        out_shape=[jax.ShapeDtypeStruct((t, d // PACK), jnp.uint32),
                   jax.ShapeDtypeStruct((TOP_K, t), jnp.int32),
                   jax.ShapeDtypeStruct((TOP_K, t), F32),
                   jax.ShapeDtypeStruct((TOP_K, t), jnp.int32),
                   jax.ShapeDtypeStruct((N_EXPERTS, 128), F32)],
        grid=(t // tm,),
        in_specs=[pl.BlockSpec((tm, d), row),
                  pl.BlockSpec((1, d), fixed),
                  pl.BlockSpec((1, 1, d), grp),
                  pl.BlockSpec((1, 1, d), grp),
                  pl.BlockSpec((N_EXPERTS, d), fixed),
                  pl.BlockSpec((N_EXPERTS, 1), fixed),
                  pl.BlockSpec((tm, tm), fixed)],
        out_specs=[pl.BlockSpec((tm, d // PACK), row),
                   pl.BlockSpec((TOP_K, tm), col),
                   pl.BlockSpec((TOP_K, tm), col),
                   pl.BlockSpec((TOP_K, tm), col),
                   pl.BlockSpec((N_EXPERTS, 128), fixed)],
        scratch_shapes=[pltpu.VMEM((N_EXPERTS, 128), F32)],
        compiler_params=pltpu.CompilerParams(
            dimension_semantics=("arbitrary",), vmem_limit_bytes=VMEM_LIMIT_BYTES),
        name="moe_route",
    )(xs, gain.reshape(1, d), shift3, scale3, w_router.T[perm], router_bias[perm].reshape(N_EXPERTS, 1), tri)


def _row_copy(src, src_row, dst, dst_row, sem):
    return pltpu.make_async_copy(src.at[pl.ds(src_row, 1)], dst.at[pl.ds(dst_row, 1)], sem)


def _dispatch_kernel(dest_ref, f_ref, xs_in_ref, xs_ref, dest_smem, sem, idx_sem):
    del xs_in_ref
    tm = f_ref.shape[0]
    idx_copy = pltpu.make_async_copy(dest_ref, dest_smem, idx_sem)
    idx_copy.start()
    idx_copy.wait()

    def issue(tok, carry):
        for k in range(TOP_K):
            _row_copy(f_ref, tok, xs_ref, dest_smem[k, tok], sem).start(priority=k)
        return carry

    def drain(tok, carry):
        for k in range(TOP_K):
            _row_copy(f_ref, 0, xs_ref, 0, sem).wait()
        return carry

    lax.fori_loop(0, tm, issue, 0, unroll=8)
    lax.fori_loop(0, tm, drain, 0, unroll=8)


def _dispatch(f_packed, dest, n_slots):
    t, wd = f_packed.shape
    tm = MOE_TILE
    return pl.pallas_call(
        _dispatch_kernel,
        out_shape=jax.ShapeDtypeStruct((n_slots, wd), jnp.uint32),
        grid=(t // tm,),
        in_specs=[pl.BlockSpec((TOP_K, tm), lambda i: (0, i)),
                  pl.BlockSpec((tm, wd), lambda i: (i, 0)),
                  pl.BlockSpec(memory_space=pl.ANY)],
        out_specs=pl.BlockSpec(memory_space=pl.ANY),
        scratch_shapes=[pltpu.SMEM((TOP_K, tm), jnp.int32),
                        pltpu.SemaphoreType.DMA, pltpu.SemaphoreType.DMA],
        input_output_aliases={2: 0},
        compiler_params=pltpu.CompilerParams(
            dimension_semantics=("arbitrary",), vmem_limit_bytes=VMEM_LIMIT_BYTES),
        name="moe_dispatch",
    )(dest, f_packed, jnp.zeros((n_slots, wd), jnp.uint32))


def _experts_kernel(be_ref, na_ref, x_ref, wg_ref, wu_ref, wd_ref, o_ref, wgb_ref, wub_ref, wdb_ref):
    i = pl.program_id(0)
    prev = be_ref[jnp.maximum(i - 1, 0)]

    @pl.when((i == 0) | (be_ref[i] != prev))
    def _():
        wgb_ref[...] = wg_ref[0, 0].astype(BF16)
        wub_ref[...] = wu_ref[0, 0].astype(BF16)
        wdb_ref[...] = wd_ref[0, 0].astype(BF16)

    @pl.when(i < na_ref[0])
    def _():
        packed = x_ref[...]
        lo = pltpu.bitcast(packed << 16, F32)
        hi = pltpu.bitcast(packed & jnp.uint32(0xFFFF0000), F32)
        xb = jnp.concatenate([lo, hi], axis=-1).astype(BF16)
        g = jnp.dot(xb, wgb_ref[...], preferred_element_type=F32)
        u = jnp.dot(xb, wub_ref[...], preferred_element_type=F32)
        hid = (g * jax.nn.sigmoid(g)) * u
        o_ref[...] = jnp.dot(hid.astype(BF16), wdb_ref[...], preferred_element_type=F32)

    @pl.when(i >= na_ref[0])
    def _():
        o_ref[...] = jnp.zeros_like(o_ref)


def _experts(xs_sorted, block_expert, n_active, w_gate, w_up, w_down, layer):
    n_slots, wd = xs_sorted.shape
    d = wd * PACK
    n_blocks = n_slots // MOE_BLOCK
    blk = lambda i, be, na: (jnp.minimum(i, na[0] - 1), 0)
    wsel = lambda i, be, na: (layer, be[jnp.minimum(i, na[0] - 1)], 0, 0)
    grid_spec = pltpu.PrefetchScalarGridSpec(
        num_scalar_prefetch=2,
        grid=(n_blocks,),
        in_specs=[pl.BlockSpec((MOE_BLOCK, wd), blk),
                  pl.BlockSpec((1, 1, d, D_EXPERT), wsel),
                  pl.BlockSpec((1, 1, d, D_EXPERT), wsel),
                  pl.BlockSpec((1, 1, D_EXPERT, d), wsel)],
        out_specs=pl.BlockSpec((MOE_BLOCK, d), lambda i, be, na: (i, 0)),
        scratch_shapes=[pltpu.VMEM((d, D_EXPERT), BF16), pltpu.VMEM((d, D_EXPERT), BF16),
                        pltpu.VMEM((D_EXPERT, d), BF16)],
    )
    return pl.pallas_call(
        _experts_kernel,
        out_shape=jax.ShapeDtypeStruct((n_slots, d), F32),
        grid_spec=grid_spec,
        compiler_params=pltpu.CompilerParams(
            dimension_semantics=("arbitrary",), vmem_limit_bytes=VMEM_LIMIT_BYTES),
        name="moe_experts",
    )(block_expert, n_active, xs_sorted, w_gate, w_up, w_down)


def _combine_kernel(dest_ref, y_ref, x_ref, w_ref, gate_ref, fin_ref, o_ref, dest_smem, ya_ref, yb_ref, sem,
                    idx_sem, *, final_norm):
    tm = x_ref.shape[0]
    idx_copy = pltpu.make_async_copy(dest_ref, dest_smem, idx_sem)
    idx_copy.start()
    idx_copy.wait()
    bufs = (ya_ref, yb_ref)

    def issue(tok, carry):
        for k in range(TOP_K):
            _row_copy(y_ref, dest_smem[k, tok], bufs[k], tok, sem).start(priority=k)
        return carry

    def drain(tok, carry):
        for k in range(TOP_K):
            _row_copy(y_ref, 0, bufs[k], 0, sem).wait()
        return carry

    lax.fori_loop(0, tm, issue, 0, unroll=8)
    lax.fori_loop(0, tm, drain, 0, unroll=8)
    w = w_ref[...]
    out = x_ref[...] + gate_ref[0] * (w[:, 0:1] * ya_ref[...] + w[:, 1:2] * yb_ref[...])
    if final_norm:
        out = out * lax.rsqrt(jnp.mean(out * out, axis=-1, keepdims=True) + EPS) * fin_ref[...]
    o_ref[...] = out


def _combine(ys, dest, weight_cols, xs, gate3, n_lat_tiles, blocks_per_batch, final_gain=None):
    t, d = xs.shape
    tm = MOE_TILE
    per_tile = tm // DN_BLOCK
    n_tiles = t // tm if final_gain is None else n_lat_tiles
    fin = jnp.ones((1, d), F32) if final_gain is None else final_gain.reshape(1, d)
    row = lambda i: (i, 0)
    grp = lambda i: (_group_of_block(i * per_tile, n_lat_tiles * per_tile, blocks_per_batch), 0, 0)
    return pl.pallas_call(
        functools.partial(_combine_kernel, final_norm=final_gain is not None),
        out_shape=jax.ShapeDtypeStruct((n_tiles * tm, d), F32),
        grid=(n_tiles,),
        in_specs=[pl.BlockSpec((TOP_K, tm), lambda i: (0, i)),
                  pl.BlockSpec(memory_space=pl.ANY),
                  pl.BlockSpec((tm, d), row),
                  pl.BlockSpec((tm, TOP_K), row),
                  pl.BlockSpec((1, 1, d), grp),
                  pl.BlockSpec((1, d), lambda i: (0, 0))],
        out_specs=pl.BlockSpec((tm, d), row),
        scratch_shapes=[pltpu.SMEM((TOP_K, tm), jnp.int32),
                        pltpu.VMEM((tm, d), F32), pltpu.VMEM((tm, d), F32),
                        pltpu.SemaphoreType.DMA, pltpu.SemaphoreType.DMA],
        compiler_params=pltpu.CompilerParams(
            dimension_semantics=("arbitrary",), vmem_limit_bytes=VMEM_LIMIT_BYTES),
        name="moe_combine",
    )(dest, ys, xs, weight_cols, gate3, fin)


def _moe_layer(xs, gain, shift3, scale3, gate3, w_router, router_bias, w_gate, w_up, w_down, layer,
               n_lat_tiles, blocks_per_batch, final_gain=None):
    t = xs.shape[0]
    f_packed, expert, weight, rank, counts = _moe_route(xs, gain, shift3, scale3, w_router, router_bias,
                                                        n_lat_tiles, blocks_per_batch)
    counts = counts[:, 0].astype(jnp.int32)
    padded = (counts + MOE_BLOCK - 1) // MOE_BLOCK * MOE_BLOCK
    pend = jnp.cumsum(padded)
    pstart = pend - padded
    n_blocks = -(-(t * TOP_K) // MOE_BLOCK) + N_EXPERTS
    block_start = jnp.arange(n_blocks, dtype=jnp.int32) * MOE_BLOCK
    block_expert = jnp.minimum(jnp.sum(pend[None, :] <= block_start[:, None], axis=1),
                               N_EXPERTS - 1).astype(jnp.int32)
    n_active = (pend[-1:] // MOE_BLOCK).astype(jnp.int32)
    is_e = expert[..., None] == jnp.arange(N_EXPERTS, dtype=jnp.int32)
    dest = rank + jnp.sum(jnp.where(is_e, pstart, 0), axis=-1)
    xs_sorted = _dispatch(f_packed, dest, n_blocks * MOE_BLOCK)
    ys = _experts(xs_sorted, block_expert, n_active, w_gate, w_up, w_down, layer)
    return _combine(ys, dest, weight.T, xs, gate3, n_lat_tiles, blocks_per_batch, final_gain)


MOE_CHUNK = 8
MOE_LB = 1280
assert MOE_LB >= MOE_TILE * TOP_K + N_EXPERTS * (MOE_CHUNK - 1) and MOE_LB % 128 == 0
N_CHUNKS = MOE_LB // MOE_CHUNK
TAB_W = 256
assert TAB_W >= N_CHUNKS


def _moe_route_kernel(x_ref, gain_ref, shift_ref, scale_ref, wr_ref, rb_ref, tri_ref, lt_ref,
                      f_ref, pos_ref, w_ref, cnt_ref):
    tm, d = x_ref.shape
    ne, epg, ng = N_EXPERTS, EXPERTS_PER_GROUP, N_GROUPS
    f = _modulated(x_ref[...], gain_ref[...], shift_ref[0], scale_ref[0])
    f_ref[...] = f.astype(BF16)

    logits = lax.dot_general(wr_ref[...], f, (((1,), (1,)), ((), ())),
                             precision=lax.Precision.HIGHEST, preferred_element_type=F32)
    scores = jax.nn.sigmoid(logits)
    biased = scores + rb_ref[...]
    s = [scores[j * ng:(j + 1) * ng] for j in range(epg)]
    c = [biased[j * ng:(j + 1) * ng] for j in range(epg)]
    hi01, lo01 = jnp.maximum(c[0], c[1]), jnp.minimum(c[0], c[1])
    hi23, lo23 = jnp.maximum(c[2], c[3]), jnp.minimum(c[2], c[3])
    gscore = jnp.maximum(hi01, hi23) + jnp.maximum(jnp.minimum(hi01, hi23), jnp.maximum(lo01, lo23))
    gi = lax.broadcasted_iota(jnp.int32, (ng, tm), 0)
    gmax = jnp.max(gscore, axis=0, keepdims=True)
    grp = jnp.min(jnp.where(gscore == gmax, gi, ng), axis=0, keepdims=True)
    sel = gi == grp
    cv = [jnp.sum(jnp.where(sel, t, 0.0), axis=0, keepdims=True) for t in c]
    sv = [jnp.sum(jnp.where(sel, t, 0.0), axis=0, keepdims=True) for t in s]

    def pick(excluded):
        best = jnp.full((1, tm), -jnp.inf, F32)
        idx = jnp.zeros((1, tm), jnp.int32)
        val = jnp.zeros((1, tm), F32)
        for j in range(epg):
            cand = cv[j] if excluded is None else jnp.where(excluded == j, -jnp.inf, cv[j])
            take = cand > best
            best = jnp.where(take, cand, best)
            idx = jnp.where(take, j, idx)
            val = jnp.where(take, sv[j], val)
        return idx, val

    i1, v1 = pick(None)
    i2, v2 = pick(i1)
    wsum = v1 + v2
    w_ref[0:1, :] = v1 / wsum
    w_ref[1:2, :] = v2 / wsum

    ei = lax.broadcasted_iota(jnp.int32, (ne, tm), 0)
    oh1 = ei == grp * epg + i1
    oh2 = ei == grp * epg + i2
    tri = tri_ref[...]
    pre1 = jnp.dot(oh1.astype(BF16), tri, preferred_element_type=F32)
    pre2 = jnp.dot(oh2.astype(BF16), tri, preferred_element_type=F32)
    tot1 = pre1[:, tm - 1:tm]
    tot = tot1 + pre2[:, tm - 1:tm]
    seg = jnp.floor((tot + (MOE_CHUNK - 1)) * (1.0 / MOE_CHUNK)) * MOE_CHUNK
    off = jnp.dot(lt_ref[...], jnp.broadcast_to(seg, (ne, 128)).astype(BF16),
                  preferred_element_type=F32)[:, 0:1]
    p1 = jnp.sum(jnp.where(oh1, off + pre1 - 1.0, 0.0), axis=0, keepdims=True)
    p2 = jnp.sum(jnp.where(oh2, off + tot1 + pre2 - 1.0, 0.0), axis=0, keepdims=True)
    pos_ref[0:1, :] = p1.astype(jnp.int32)
    pos_ref[1:2, :] = p2.astype(jnp.int32)
    cnt_ref[0] = jnp.broadcast_to(tot, (ne, 128))


def _moe_route(xs, gain, shift3, scale3, w_router, router_bias, n_lat_tiles, blocks_per_batch):
    t, d = xs.shape
    tm = MOE_TILE
    ne = N_EXPERTS
    per_tile = tm // DN_BLOCK
    row = lambda i: (i, 0)
    col = lambda i: (0, i)
    fixed = lambda i: (0, 0)
    grp = lambda i: (_group_of_block(i * per_tile, n_lat_tiles * per_tile, blocks_per_batch), 0, 0)
    tri = jnp.asarray(np.triu(np.ones((tm, tm), np.float32)), BF16)
    lt = jnp.asarray(np.tril(np.ones((ne, ne), np.float32), -1), BF16)
    perm = np.arange(ne).reshape(N_GROUPS, EXPERTS_PER_GROUP).T.reshape(-1)
    return pl.pallas_call(
        _moe_route_kernel,
        out_shape=[jax.ShapeDtypeStruct((t, d), BF16),
                   jax.ShapeDtypeStruct((TOP_K, t), jnp.int32),
                   jax.ShapeDtypeStruct((TOP_K, t), F32),
                   jax.ShapeDtypeStruct((t // tm, ne, 128), F32)],
        grid=(t // tm,),
        in_specs=[pl.BlockSpec((tm, d), row),
                  pl.BlockSpec((1, d), fixed),
                  pl.BlockSpec((1, 1, d), grp),
                  pl.BlockSpec((1, 1, d), grp),
                  pl.BlockSpec((ne, d), fixed),
                  pl.BlockSpec((ne, 1), fixed),
                  pl.BlockSpec((tm, tm), fixed),
                  pl.BlockSpec((ne, ne), fixed)],
        out_specs=[pl.BlockSpec((tm, d), row),
                   pl.BlockSpec((TOP_K, tm), col),
                   pl.BlockSpec((TOP_K, tm), col),
                   pl.BlockSpec((1, ne, 128), lambda i: (i, 0, 0))],
        compiler_params=pltpu.CompilerParams(
            dimension_semantics=("arbitrary",), vmem_limit_bytes=VMEM_LIMIT_BYTES),
        name="moe_route",
    )(xs, gain.reshape(1, d), shift3, scale3, w_router.T[perm], router_bias[perm].reshape(ne, 1), tri, lt)


def _chunk_row(j):
    return j * MOE_CHUNK if isinstance(j, int) else pl.multiple_of(j * MOE_CHUNK, MOE_CHUNK)


def _chunk_copies(tab_smem, make_copy):
    def issue(j, n):
        dst = tab_smem[0, j]

        @pl.when(dst >= 0)
        def _():
            make_copy(j, pl.multiple_of(dst, MOE_CHUNK)).start()

        return n + (dst >= 0).astype(jnp.int32)

    n = lax.fori_loop(0, N_CHUNKS, issue, jnp.int32(0), unroll=8)

    def drain(j, carry):
        make_copy(0, 0).wait()
        return carry

    lax.fori_loop(0, n, drain, 0)


def _moe_dispatch_kernel(tab_ref, pos_ref, f_ref, xs_in_ref, xs_ref, tab_smem, loc_ref, sem, idx_sem):
    del xs_in_ref
    tm, d = f_ref.shape
    idx_copy = pltpu.make_async_copy(tab_ref.at[0], tab_smem, idx_sem)
    idx_copy.start()
    r = lax.broadcasted_iota(jnp.int32, (tm, MOE_LB), 1)
    p = pos_ref[...]
    onehot = ((p[:, 0:1] == r) | (p[:, 1:2] == r)).astype(BF16)
    loc = _dotb_tn(onehot, f_ref[...])
    bits = pltpu.bitcast(loc, jnp.uint32)
    half = d // PACK
    loc_ref[...] = (bits[:, :half] >> 16) | (bits[:, half:] & jnp.uint32(0xFFFF0000))
    idx_copy.wait()

    def make_copy(j, dst):
        return pltpu.make_async_copy(loc_ref.at[pl.ds(_chunk_row(j), MOE_CHUNK)],
                                     xs_ref.at[pl.ds(dst, MOE_CHUNK)], sem)

    _chunk_copies(tab_smem, make_copy)


def _moe_dispatch(f, pos_cols, table, n_slots):
    t, d = f.shape
    tm = MOE_TILE
    wd = d // PACK
    return pl.pallas_call(
        _moe_dispatch_kernel,
        out_shape=jax.ShapeDtypeStruct((n_slots, wd), jnp.uint32),
        grid=(t // tm,),
        in_specs=[pl.BlockSpec((1, 1, TAB_W), lambda i: (i, 0, 0)),
                  pl.BlockSpec((tm, TOP_K), lambda i: (i, 0)),
                  pl.BlockSpec((tm, d), lambda i: (i, 0)),
                  pl.BlockSpec(memory_space=pl.ANY)],
        out_specs=pl.BlockSpec(memory_space=pl.ANY),
        scratch_shapes=[pltpu.SMEM((1, TAB_W), jnp.int32),
                        pltpu.VMEM((MOE_LB, wd), jnp.uint32),
                        pltpu.SemaphoreType.DMA, pltpu.SemaphoreType.DMA],
        input_output_aliases={3: 0},
        compiler_params=pltpu.CompilerParams(
            dimension_semantics=("arbitrary",), vmem_limit_bytes=VMEM_LIMIT_BYTES),
        name="moe_dispatch",
    )(table, pos_cols, f, jnp.zeros((n_slots, wd), jnp.uint32))


def _moe_combine_kernel(tab_ref, pos_ref, w_ref, y_ref, x_ref, gate_ref, fin_ref, o_ref, tab_smem, yloc_ref,
                        sem, idx_sem, *, final_norm):
    tm = x_ref.shape[0]

    @pl.when(pl.program_id(0) == 0)
    def _():
        yloc_ref[...] = jnp.zeros_like(yloc_ref)

    idx_copy = pltpu.make_async_copy(tab_ref.at[0], tab_smem, idx_sem)
    idx_copy.start()
    idx_copy.wait()

    def make_copy(j, src):
        return pltpu.make_async_copy(y_ref.at[pl.ds(src, MOE_CHUNK)],
                                     yloc_ref.at[pl.ds(_chunk_row(j), MOE_CHUNK)],
                                     sem)

    _chunk_copies(tab_smem, make_copy)
    r = lax.broadcasted_iota(jnp.int32, (tm, MOE_LB), 1)
    p = pos_ref[...]
    w = w_ref[...]
    wmat = jnp.where(p[:, 0:1] == r, w[:, 0:1], 0.0) + jnp.where(p[:, 1:2] == r, w[:, 1:2], 0.0)
    out = x_ref[...] + gate_ref[0] * _dotb(wmat, yloc_ref[...])
    if final_norm:
        out = out * lax.rsqrt(jnp.mean(out * out, axis=-1, keepdims=True) + EPS) * fin_ref[...]
    o_ref[...] = out


def _moe_combine(ys, pos_cols, weight_cols, table, xs, gate3, n_lat_tiles, blocks_per_batch, final_gain=None):
    t, d = xs.shape
    tm = MOE_TILE
    per_tile = tm // DN_BLOCK
    n_tiles = t // tm if final_gain is None else n_lat_tiles
    fin = jnp.ones((1, d), F32) if final_gain is None else final_gain.reshape(1, d)
    row = lambda i: (i, 0)
    grp = lambda i: (_group_of_block(i * per_tile, n_lat_tiles * per_tile, blocks_per_batch), 0, 0)
    return pl.pallas_call(
        functools.partial(_moe_combine_kernel, final_norm=final_gain is not None),
        out_shape=jax.ShapeDtypeStruct((n_tiles * tm, d), F32),
        grid=(n_tiles,),
        in_specs=[pl.BlockSpec((1, 1, TAB_W), lambda i: (i, 0, 0)),
                  pl.BlockSpec((tm, TOP_K), row),
                  pl.BlockSpec((tm, TOP_K), row),
                  pl.BlockSpec(memory_space=pl.ANY),
                  pl.BlockSpec((tm, d), row),
                  pl.BlockSpec((1, 1, d), grp),
                  pl.BlockSpec((1, d), lambda i: (0, 0))],
        out_specs=pl.BlockSpec((tm, d), row),
        scratch_shapes=[pltpu.SMEM((1, TAB_W), jnp.int32),
                        pltpu.VMEM((MOE_LB, d), F32),
                        pltpu.SemaphoreType.DMA, pltpu.SemaphoreType.DMA],
        compiler_params=pltpu.CompilerParams(
            dimension_semantics=("arbitrary",), vmem_limit_bytes=VMEM_LIMIT_BYTES),
        name="moe_combine",
    )(table, pos_cols, weight_cols, ys, xs, gate3, fin)


def _moe_layer(xs, gain, shift3, scale3, gate3, w_router, router_bias, w_gate, w_up, w_down, layer,
               n_lat_tiles, blocks_per_batch, final_gain=None):
    t = xs.shape[0]
    n_tiles = t // MOE_TILE
    f, pos, weight, cnt = _moe_route(xs, gain, shift3, scale3, w_router, router_bias,
                                     n_lat_tiles, blocks_per_batch)
    seg = (cnt[:, :, 0].astype(jnp.int32) + MOE_CHUNK - 1) // MOE_CHUNK * MOE_CHUNK
    loc_end = jnp.cumsum(seg, axis=1)
    loc_off = loc_end - seg
    padded = (jnp.sum(seg, axis=0) + MOE_BLOCK - 1) // MOE_BLOCK * MOE_BLOCK
    pend = jnp.cumsum(padded)
    seg_start = (pend - padded)[None, :] + jnp.cumsum(seg, axis=0) - seg
    n_blocks = -(-(t * TOP_K + n_tiles * N_EXPERTS * (MOE_CHUNK - 1)) // MOE_BLOCK) + N_EXPERTS
    block_start = jnp.arange(n_blocks, dtype=jnp.int32) * MOE_BLOCK
    block_expert = jnp.minimum(jnp.sum(pend[None, :] <= block_start[:, None], axis=1),
                               N_EXPERTS - 1).astype(jnp.int32)
    n_active = (pend[-1:] // MOE_BLOCK).astype(jnp.int32)
    row0 = jnp.arange(N_CHUNKS, dtype=jnp.int32) * MOE_CHUNK
    e_of = jnp.sum(loc_end[:, None, :] <= row0[None, :, None], axis=-1)
    is_e = e_of[..., None] == jnp.arange(N_EXPERTS, dtype=jnp.int32)
    shift = jnp.sum(jnp.where(is_e, (seg_start - loc_off)[:, None, :], 0), axis=-1)
    table = jnp.where(e_of < N_EXPERTS, row0[None, :] + shift, -1)
    table = jnp.pad(table, ((0, 0), (0, TAB_W - N_CHUNKS)), constant_values=-1).reshape(n_tiles, 1, TAB_W)

    pos_cols = pos.T
    xs_sorted = _moe_dispatch(f, pos_cols, table, n_blocks * MOE_BLOCK)
    ys = _experts(xs_sorted, block_expert, n_active, w_gate, w_up, w_down, layer)
    return _moe_combine(ys, pos_cols, weight.T, table, xs, gate3, n_lat_tiles, blocks_per_batch, final_gain)


def _rmsnorm(x, gain):
    y = x * lax.rsqrt(jnp.mean(x * x, axis=-1, keepdims=True) + EPS)
    return y * gain


def _modulate(x, gain, shift, scale):
    return _rmsnorm(x, gain) * (1 + scale) + shift


def _l2norm(t):
    return t * lax.rsqrt(jnp.sum(t * t, axis=-1, keepdims=True) + EPS)


def _short_conv(x, w, on_grid):
    b, l, ch = x.shape
    xs = x.reshape(b, l // GRID_W, GRID_W, ch) if on_grid else x.reshape(b, 1, l, ch)
    n = xs.shape[2]
    xp = jnp.pad(xs, ((0, 0), (0, 0), (1, 1), (0, 0)))
    y = w[0] * xp[:, :, 0:n] + w[1] * xp[:, :, 1:n + 1] + w[2] * xp[:, :, 2:n + 2]
    return y.reshape(b, l, ch)


def _gated_delta_chunked(q, k, v, g, beta, s0):
    b, h, l, dk = q.shape
    dv = v.shape[-1]
    c = DN_CHUNK
    n = l // c
    q = q.reshape(b, h, n, c, dk)
    k = k.reshape(b, h, n, c, dk)
    v = v.reshape(b, h, n, c, dv)
    g = jnp.cumsum(g.reshape(b, h, n, c), axis=-1)
    beta = beta.reshape(b, h, n, c, 1)
    pos = jnp.arange(c)
    incl = pos[:, None] >= pos[None, :]
    strict = pos[:, None] > pos[None, :]
    decay = jnp.exp(jnp.where(incl, g[..., :, None] - g[..., None, :], -jnp.inf))
    kb = k * beta
    a_mat = jnp.einsum('bhnid,bhnjd->bhnij', kb, k) * jnp.where(strict, decay, 0.0)
    rhs = jnp.concatenate([v * beta, kb * jnp.exp(g)[..., None]], axis=-1)
    sol = lax.linalg.triangular_solve(a_mat + jnp.eye(c, dtype=a_mat.dtype), rhs,
                                      left_side=True, lower=True, unit_diagonal=True)
    u, w = sol[..., :dv], sol[..., dv:]
    attn = jnp.einsum('bhnid,bhnjd->bhnij', q, k) * decay
    g_last = g[..., -1:]
    q_dec = q * jnp.exp(g)[..., None]
    k_dec = k * jnp.exp(g_last - g)[..., None]

    def step(s, inp):
        qd, kd, uu, ww, at, gl = inp
        v_new = uu - jnp.einsum('bhck,bhkv->bhcv', ww, s)
        o = jnp.einsum('bhck,bhkv->bhcv', qd, s) + jnp.einsum('bhcs,bhsv->bhcv', at, v_new)
        s = s * jnp.exp(gl)[..., None] + jnp.einsum('bhck,bhcv->bhkv', kd, v_new)
        return s, o

    xs = tuple(jnp.moveaxis(t, 2, 0) for t in (q_dec, k_dec, u, w, attn, g_last))
    s_final, o = lax.scan(step, s0, xs)
    o = jnp.moveaxis(o, 0, 2).reshape(b, h, l, dv)
    return o, s_final


def _deltanet_mixer(p_ctx, p_lat, conv_w, a_log, dt_bias, out_norm):
    d = D_MODEL
    nh = DN_HEADS

    def project(p, on_grid):
        b, l, _ = p.shape
        qkv = jax.nn.silu(_short_conv(p[..., :3 * d], conv_w, on_grid))
        z = p[..., 3 * d:4 * d]
        a = p[..., 4 * d:4 * d + 2 * nh].reshape(b, l, 2, nh)
        bb = p[..., 4 * d + 2 * nh:].reshape(b, l, 2, nh)

        def heads(t):
            return jnp.transpose(t.reshape(b, l, nh, -1), (0, 2, 1, 3))

        q, k, v = (heads(t) for t in jnp.split(qkv, 3, axis=-1))
        q = _l2norm(q) * DN_DK ** -0.5
        k = _l2norm(k)
        g = -jnp.exp(a_log) * jax.nn.softplus(a + dt_bias)
        g = jnp.transpose(g, (2, 0, 3, 1))
        beta = jnp.transpose(jax.nn.sigmoid(bb), (2, 0, 3, 1))
        return q, k, v, g, beta, z

    def scan_both(q, k, v, g, beta, s_f, s_b):
        o_f, s_f = _gated_delta_chunked(q, k, v, g[0], beta[0], s_f)
        rev = lambda t: jnp.flip(t, axis=2)
        o_b, s_b = _gated_delta_chunked(rev(q), rev(k), rev(v), rev(g[1]), rev(beta[1]), s_b)
        return o_f + rev(o_b), s_f, s_b

    def finish(o, z):
        b, _, l, _ = o.shape
        o = jnp.transpose(o, (0, 2, 1, 3))
        o = o * lax.rsqrt(jnp.mean(o * o, axis=-1, keepdims=True) + EPS) * out_norm
        o = o * jax.nn.silu(z.reshape(b, l, nh, DN_DV))
        return o.reshape(b, l, d)

    qc, kc, vc, gc, bc, zc = project(p_ctx, False)
    s0 = jnp.zeros((p_ctx.shape[0], nh, DN_DK, DN_DV), F32)
    o_c, s_f, s_b = scan_both(qc, kc, vc, gc, bc, s0, s0)
    ql, kl, vl, gla, bl, zl = project(p_lat, True)
    o_l, _, _ = scan_both(ql, kl, vl, gla, bl, s_f, s_b)
    return finish(o_c, zc), finish(o_l, zl)


def _hyena_filters(l, w1, b1, freq, w2, b2, w3):
    pos = jnp.arange(l, dtype=F32)[:, None]
    t = pos / max(l - 1, 1)
    bands = jnp.linspace(1e-4, HY_BANDS - 1, HY_BANDS, dtype=F32)[None, :]
    ang = (2 * math.pi / l) * pos * bands
    feat = jnp.concatenate([t, jnp.cos(ang), -jnp.sin(ang)], axis=-1)
    hp = lax.Precision.HIGHEST
    hdn = jnp.sin(freq * (jnp.dot(feat, w1, precision=hp) + b1))
    hdn = jnp.sin(freq * (jnp.dot(hdn, w2, precision=hp) + b2))
    filt = jnp.dot(hdn, w3, precision=hp).reshape(l, HY_ORDER, 2, D_MODEL)
    deltas = jnp.abs(jnp.linspace(math.log(HY_TARGET) / HY_SLOW, math.log(HY_TARGET) / HY_FAST,
                                  D_MODEL, dtype=F32))
    window = jnp.exp(-t * deltas[None, :])
    return filt * window[:, None, None, :]


def _two_sided_fftconv(u, h_fwd, h_bwd):
    l = u.shape[1]
    k = jnp.concatenate([h_fwd, jnp.zeros_like(h_fwd[:1]), jnp.flip(h_bwd[1:], axis=0)], axis=0)
    kf = jnp.fft.rfft(k, axis=0)
    uf = jnp.fft.rfft(u, n=2 * l, axis=1)
    return jnp.fft.irfft(uf * kf[None], n=2 * l, axis=1)[:, :l]


def _hyena_stream(p, on_grid, conv_w, f_w1, f_b1, f_freq, f_w2, f_b2, f_w3, bias):
    l = p.shape[1]
    p = _short_conv(p, conv_w, on_grid)
    v, x1, x2 = jnp.split(p, 3, axis=-1)
    filt = _hyena_filters(l, f_w1, f_b1, f_freq, f_w2, f_b2, f_w3)
    z = v
    for n, gate in enumerate((x1, x2)):
        conv = _two_sided_fftconv(z, filt[:, n, 0], filt[:, n, 1])
        z = gate * (conv + bias[n] * z)
    return z


def _shortconv_stream(p, on_grid, conv_w):
    bg, cg, xin = jnp.split(p, 3, axis=-1)
    return bg * _short_conv(cg * xin, conv_w, on_grid)


def _route(h, w_router, router_bias):
    t = h.shape[0]
    scores = jax.nn.sigmoid(jnp.dot(h, w_router, precision=lax.Precision.HIGHEST))
    choice = (scores + router_bias).reshape(t, N_GROUPS, EXPERTS_PER_GROUP)
    group_score = lax.top_k(choice, GROUP_SCORE_K)[0].sum(-1)
    group = jnp.argmax(group_score, axis=-1)
    in_group = jnp.take_along_axis(choice, group[:, None, None], axis=1)[:, 0]
    local = lax.top_k(in_group, TOP_K)[1]
    expert = group[:, None] * EXPERTS_PER_GROUP + local
    weight = jnp.take_along_axis(scores, expert, axis=1)
    weight = weight / jnp.sum(weight, axis=-1, keepdims=True)
    return expert.astype(jnp.int32), weight


def _moe_ffn(x, w_router, router_bias, w_gate, w_up, w_down):
    t, d = x.shape
    expert, weight = _route(x, w_router, router_bias)
    a = t * TOP_K
    e_flat = expert.reshape(-1)
    order = jnp.argsort(e_flat)
    e_sorted = e_flat[order]
    tok_sorted = (order // TOP_K).astype(jnp.int32)
    counts = jnp.zeros((N_EXPERTS,), jnp.int32).at[e_flat].add(1)
    start = jnp.cumsum(counts) - counts
    padded = (counts + MOE_BLOCK - 1) // MOE_BLOCK * MOE_BLOCK
    pend = jnp.cumsum(padded)
    pstart = pend - padded
    dest = pstart[e_sorted] + (jnp.arange(a, dtype=jnp.int32) - start[e_sorted])
    n_blocks = -(-a // MOE_BLOCK) + N_EXPERTS
    n_slots = n_blocks * MOE_BLOCK
    slot_tok = jnp.full((n_slots,), t, jnp.int32).at[dest].set(tok_sorted)
    block_start = jnp.arange(n_blocks, dtype=jnp.int32) * MOE_BLOCK
    block_expert = jnp.minimum(jnp.searchsorted(pend, block_start, side='right'),
                               N_EXPERTS - 1).astype(jnp.int32)
    x_pad = jnp.concatenate([x.astype(BF16), jnp.zeros((1, d), BF16)], axis=0)
    xs = x_pad[slot_tok]
    ys = _expert_ffn(xs, block_expert, jnp.ones((n_slots,), F32), w_gate, w_up, w_down)
    slot_of = jnp.zeros((a,), jnp.int32).at[order].set(dest).reshape(t, TOP_K)
    out = weight[:, 0:1] * ys[slot_of[:, 0]] + weight[:, 1:2] * ys[slot_of[:, 1]]
    return out


def _sc_layer_kernel(x_ref, gain_ref, shift_ref, scale_ref, win_ref, cw_ref, wout_ref, gate_ref, o_ref, *,
                     n_lat_tiles):
    d = D_MODEL
    tm = x_ref.shape[0]
    seg = jnp.where(pl.program_id(0) >= n_lat_tiles, CTX_LEN, GRID_W)
    pos = lax.broadcasted_iota(jnp.int32, (tm, 1), 0) & (seg - 1)
    hb = _modulated(x_ref[...], gain_ref[...], shift_ref[0], scale_ref[0]).astype(BF16)
    u = (jnp.dot(hb, win_ref[:, d:2 * d], preferred_element_type=F32)
         * jnp.dot(hb, win_ref[:, 2 * d:], preferred_element_type=F32))
    prev = jnp.where(pos != 0, pltpu.roll(u, 1, axis=0), 0.0)
    nxt = jnp.where(pos != seg - 1, pltpu.roll(u, tm - 1, axis=0), 0.0)
    cw = cw_ref[...]
    y = jnp.dot(hb, win_ref[:, :d], preferred_element_type=F32) * (
        cw[0:1] * prev + cw[1:2] * u + cw[2:3] * nxt)
    o_ref[...] = x_ref[...] + gate_ref[0] * jnp.dot(y.astype(BF16), wout_ref[...],
                                                    preferred_element_type=F32)


def _shortconv_layer(xs, gain, shift3, scale3, w_in, conv_w, w_out, gate3, n_lat_tiles, blocks_per_batch):
    t, d = xs.shape
    tm = HY_TOK_TILE
    per_tile = tm // DN_BLOCK
    row = lambda i: (i, 0)
    fixed = lambda i: (0, 0)
    grp = lambda i: (_group_of_block(i * per_tile, n_lat_tiles * per_tile, blocks_per_batch), 0, 0)
    return pl.pallas_call(
        functools.partial(_sc_layer_kernel, n_lat_tiles=n_lat_tiles),
        out_shape=jax.ShapeDtypeStruct((t, d), F32),
        grid=(t // tm,),
        in_specs=[pl.BlockSpec((tm, d), row),
                  pl.BlockSpec((1, d), fixed),
                  pl.BlockSpec((1, 1, d), grp),
                  pl.BlockSpec((1, 1, d), grp),
                  pl.BlockSpec((d, 3 * d), fixed),
                  pl.BlockSpec((3, d), fixed),
                  pl.BlockSpec((d, d), fixed),
                  pl.BlockSpec((1, 1, d), grp)],
        out_specs=pl.BlockSpec((tm, d), row),
        compiler_params=pltpu.CompilerParams(
            dimension_semantics=("arbitrary",), vmem_limit_bytes=VMEM_LIMIT_BYTES),
        name="shortconv_layer",
    )(xs, gain.reshape(1, d), shift3, scale3, w_in.astype(BF16), conv_w, w_out.astype(BF16), gate3)


def kernel(x, c, ctx, c_ctx, ada_w, ada_b, norm_mix, norm_ffn, norm_final, dn_w_in, dn_conv, dn_a_log,
           dn_dt_bias, dn_out_norm, dn_w_out, hy_w_in, hy_conv, hy_f_w1, hy_f_b1, hy_f_freq, hy_f_w2,
           hy_f_b2, hy_f_w3, hy_bias, hy_w_out, sc_w_in, sc_conv, sc_w_out, w_router, router_bias,
           moe_w_gate, moe_w_up, moe_w_down):
    d = D_MODEL
    bsz, seq, _ = x.shape
    n_ctx = bsz * CTX_LEN
    n_lat = bsz * seq
    silu_c = jax.nn.silu(c)
    silu_cc = jax.nn.silu(c_ctx)
    hp = lax.Precision.HIGHEST

    xs = jnp.concatenate([x.reshape(n_lat, d), ctx.reshape(n_ctx, d)], axis=0)
    n_lat_blocks, blocks_per_batch = n_lat // DN_BLOCK, seq // DN_BLOCK

    for i in range(DEPTH):
        kind, j = i % N_MIXERS, i // N_MIXERS
        ml = jnp.split(jnp.dot(silu_c, ada_w[i], precision=hp) + ada_b[i], N_MOD, axis=-1)
        mc = jnp.split(jnp.dot(silu_cc, ada_w[i], precision=hp) + ada_b[i], N_MOD, axis=-1)
        mod = [jnp.concatenate([mc[m][None], ml[m]], axis=0)[:, None, :] for m in range(N_MOD)]
        if kind == 0:
            xs = _deltanet_layer(xs, norm_mix[i], mod[0], mod[1], dn_w_in[j], dn_conv[j], dn_a_log[j],
                                 dn_dt_bias[j], dn_out_norm[j], dn_w_out[j], mod[2], n_lat_blocks,
                                 blocks_per_batch)
        elif kind == 1:
            xs = _hyena_layer(xs, norm_mix[i], mod[0], mod[1], hy_w_in[j], hy_conv[j], hy_f_w1[j], hy_f_b1[j],
                              hy_f_freq[j], hy_f_w2[j], hy_f_b2[j], hy_f_w3[j], hy_bias[j], hy_w_out[j],
                              mod[2], bsz, seq)
        else:
            xs = _shortconv_layer(xs, norm_mix[i], mod[0], mod[1], sc_w_in[j], sc_conv[j], sc_w_out[j],
                                  mod[2], n_lat // HY_TOK_TILE, blocks_per_batch)
        xs = _moe_layer(xs, norm_ffn[i], mod[3], mod[4], mod[5], w_router, router_bias,
                        moe_w_gate, moe_w_up, moe_w_down, i, n_lat // MOE_TILE, blocks_per_batch,
                        norm_final if i == DEPTH - 1 else None)
    return xs.reshape(bsz, seq, d)
```

```python
import functools
import math

import numpy as np
import jax
import jax.numpy as jnp
from jax import lax
from jax.experimental import pallas as pl
from jax.experimental.pallas import tpu as pltpu

D_MODEL = 1024
DEPTH = 4
CTX_LEN = 256
GRID_W = 64
N_MIXERS = 3
EPS = 1e-6
N_MOD = 6

DN_HEADS = 8
DN_DK = D_MODEL // DN_HEADS
DN_DV = D_MODEL // DN_HEADS
DN_CHUNK = 64

HY_ORDER = 2
HY_BANDS = 16
HY_TARGET = 1e-2
HY_FAST = 0.3
HY_SLOW = 1.5

N_EXPERTS = 32
N_GROUPS = 8
EXPERTS_PER_GROUP = N_EXPERTS // N_GROUPS
GROUP_SCORE_K = 2
TOP_K = 2
D_EXPERT = 512
MOE_BLOCK = 512

F32 = jnp.float32
BF16 = jnp.bfloat16

ROW_TILE = 512
VMEM_LIMIT_BYTES = 48 * 1024 * 1024


def _mm_kernel(x_ref, w_ref, o_ref):
    o_ref[...] = jnp.dot(x_ref[...].astype(BF16), w_ref[...], preferred_element_type=F32)


def _mm(x, w, tn=None):
    m, k = x.shape
    n = w.shape[1]
    tm = min(ROW_TILE, m)
    tn = n if tn is None else tn
    assert m % tm == 0 and n % tn == 0
    return pl.pallas_call(
        _mm_kernel,
        out_shape=jax.ShapeDtypeStruct((m, n), F32),
        grid=(m // tm, n // tn),
        in_specs=[pl.BlockSpec((tm, k), lambda i, j: (i, 0)),
                  pl.BlockSpec((k, tn), lambda i, j: (0, j))],
        out_specs=pl.BlockSpec((tm, tn), lambda i, j: (i, j)),
        compiler_params=pltpu.CompilerParams(
            dimension_semantics=("arbitrary", "arbitrary"), vmem_limit_bytes=VMEM_LIMIT_BYTES),
        name="dense_mm",
    )(x, w.astype(BF16))


def _expert_kernel(be_ref, x_ref, wg_ref, wu_ref, wd_ref, sw_ref, o_ref):
    del be_ref
    xb = x_ref[...]
    g = jnp.dot(xb, wg_ref[0], preferred_element_type=F32)
    u = jnp.dot(xb, wu_ref[0], preferred_element_type=F32)
    hid = (g * jax.nn.sigmoid(g)) * u
    y = jnp.dot(hid.astype(BF16), wd_ref[0], preferred_element_type=F32)
    o_ref[...] = y * sw_ref[...]


def _expert_ffn(xs, block_expert, slot_w, w_gate, w_up, w_down):
    n_slots, d = xs.shape
    n_blocks = n_slots // MOE_BLOCK
    grid_spec = pltpu.PrefetchScalarGridSpec(
        num_scalar_prefetch=1,
        grid=(n_blocks,),
        in_specs=[
            pl.BlockSpec((MOE_BLOCK, d), lambda i, be: (i, 0)),
            pl.BlockSpec((1, d, D_EXPERT), lambda i, be: (be[i], 0, 0)),
            pl.BlockSpec((1, d, D_EXPERT), lambda i, be: (be[i], 0, 0)),
            pl.BlockSpec((1, D_EXPERT, d), lambda i, be: (be[i], 0, 0)),
            pl.BlockSpec((MOE_BLOCK, 1), lambda i, be: (i, 0)),
        ],
        out_specs=pl.BlockSpec((MOE_BLOCK, d), lambda i, be: (i, 0)),
    )
    return pl.pallas_call(
        _expert_kernel,
        out_shape=jax.ShapeDtypeStruct((n_slots, d), F32),
        grid_spec=grid_spec,
        compiler_params=pltpu.CompilerParams(
            dimension_semantics=("arbitrary",), vmem_limit_bytes=VMEM_LIMIT_BYTES),
        name="expert_ffn",
    )(block_expert, xs, w_gate.astype(BF16), w_up.astype(BF16), w_down.astype(BF16),
      slot_w.reshape(n_slots, 1))


DN_BLOCK = CTX_LEN
DN_HB = DN_HEADS
N_CHUNKS_PER_BLOCK = DN_BLOCK // DN_CHUNK


def _group_of_block(i, n_lat_blocks, blocks_per_batch):
    return jnp.where(i >= n_lat_blocks, 0, 1 + i // blocks_per_batch)


def _modulated(x, gain, shift, scale):
    y = x * lax.rsqrt(jnp.mean(x * x, axis=-1, keepdims=True) + EPS) * gain
    return y * (1 + scale) + shift


def _dn_inproj_kernel(x_ref, gain_ref, shift_ref, scale_ref, w_ref, wab_ref, cw_ref, alog_ref, dtb_ref,
                      q_ref, k_ref, v_ref, z_ref, gate_ref, *, n_lat_blocks):
    i = pl.program_id(0)
    nrow = DN_BLOCK
    d = D_MODEL
    pair = 2 * DN_DK
    seg = jnp.where(i >= n_lat_blocks, CTX_LEN, GRID_W)
    r = lax.broadcasted_iota(jnp.int32, (nrow, 1), 0)
    pos = r & (seg - 1)
    not_first = pos != 0
    not_last = pos != seg - 1
    h = _modulated(x_ref[...], gain_ref[...], shift_ref[0], scale_ref[0])
    hb = h.astype(BF16)
    outs = (q_ref, k_ref, v_ref)
    for part in range(3):
        for hp in range(d // pair):
            col = part * d + hp * pair
            x = jnp.dot(hb, w_ref[:, col:col + pair], preferred_element_type=F32)
            cw = cw_ref[:, col:col + pair]
            xp = jnp.where(not_first, pltpu.roll(x, 1, axis=0), 0.0)
            xn = jnp.where(not_last, pltpu.roll(x, nrow - 1, axis=0), 0.0)
            y = cw[0:1] * xp + cw[1:2] * x + cw[2:3] * xn
            y = y * jax.nn.sigmoid(y)
            for hh in range(2):
                yh = y[:, hh * DN_DK:(hh + 1) * DN_DK]
                if part < 2:
                    yh = yh * lax.rsqrt(jnp.sum(yh * yh, axis=-1, keepdims=True) + EPS)
                if part == 0:
                    yh = yh * DN_DK ** -0.5
                outs[part][:, hp * pair + hh * DN_DK:hp * pair + (hh + 1) * DN_DK] = yh
    for j in range(d // pair):
        z_ref[:, j * pair:(j + 1) * pair] = jnp.dot(hb, w_ref[:, 3 * d + j * pair:3 * d + (j + 1) * pair],
                                                    preferred_element_type=F32)

    ab = jnp.dot(h, wab_ref[...], precision=lax.Precision.HIGHEST, preferred_element_type=F32)
    nd = 2 * DN_HEADS
    a = ab[:, :nd] + dtb_ref[...]
    softplus = jnp.maximum(a, 0.0) + jnp.log(1.0 + jnp.exp(-jnp.abs(a)))
    g = -jnp.exp(alog_ref[...]) * softplus
    beta = jax.nn.sigmoid(ab[:, nd:])
    cpos = r & (DN_CHUNK - 1)
    gp, gs = g, g
    sh = 1
    while sh < DN_CHUNK:
        gp = gp + jnp.where(cpos >= sh, pltpu.roll(gp, sh, axis=0), 0.0)
        gs = gs + jnp.where(cpos < DN_CHUNK - sh, pltpu.roll(gs, nrow - sh, axis=0), 0.0)
        sh *= 2
    colid = lax.broadcasted_iota(jnp.int32, (1, nd), 1)
    gate_ref[:, :nd] = jnp.where(colid < DN_HEADS, gp, gs)
    gate_ref[:, nd:] = beta


def _dn_inproj(xs, gain, shift3, scale3, w_in, conv_w, a_log, dt_bias, n_lat_blocks, blocks_per_batch):
    t, d = xs.shape
    nd = 2 * DN_HEADS
    row = lambda i: (i, 0)
    fixed = lambda i: (0, 0)
    grp = lambda i: (_group_of_block(i, n_lat_blocks, blocks_per_batch), 0, 0)
    return pl.pallas_call(
        functools.partial(_dn_inproj_kernel, n_lat_blocks=n_lat_blocks),
        out_shape=[jax.ShapeDtypeStruct((t, d), F32)] * 4 + [jax.ShapeDtypeStruct((t, 2 * nd), F32)],
        grid=(t // DN_BLOCK,),
        in_specs=[pl.BlockSpec((DN_BLOCK, d), row),
                  pl.BlockSpec((1, d), fixed),
                  pl.BlockSpec((1, 1, d), grp),
                  pl.BlockSpec((1, 1, d), grp),
                  pl.BlockSpec((d, 4 * d), fixed),
                  pl.BlockSpec((d, 2 * nd), fixed),
                  pl.BlockSpec((3, 3 * d), fixed),
                  pl.BlockSpec((1, nd), fixed),
                  pl.BlockSpec((1, nd), fixed)],
        out_specs=[pl.BlockSpec((DN_BLOCK, d), row)] * 4 + [pl.BlockSpec((DN_BLOCK, 2 * nd), row)],
        compiler_params=pltpu.CompilerParams(
            dimension_semantics=("arbitrary",), vmem_limit_bytes=VMEM_LIMIT_BYTES),
        name="dn_inproj",
    )(xs, gain.reshape(1, d), shift3, scale3, w_in[:, :4 * d].astype(BF16), w_in[:, 4 * d:], conv_w,
      a_log.reshape(1, nd), dt_bias.reshape(1, nd))


def _dotb(a, b):
    return jnp.dot(a.astype(BF16), b.astype(BF16), preferred_element_type=F32)


def _dotb_nt(a, b):
    return lax.dot_general(a.astype(BF16), b.astype(BF16), (((1,), (1,)), ((), ())),
                           preferred_element_type=F32)


def _dotb_tn(a, b):
    return lax.dot_general(a.astype(BF16), b.astype(BF16), (((0,), (0,)), ((), ())),
                           preferred_element_type=F32)


def _unit_tri_inverses(mats, ii, jj):
    eye = (ii == jj).astype(F32)
    diag8 = (ii >> 3) == (jj >> 3)
    n = [-jnp.where(diag8, a, 0.0) for a in mats]
    n2 = [_dotb(x, x) for x in n]
    m = [eye + x for x in n]
    m = [x + _dotb(x, y) for x, y in zip(m, n2)]
    n4 = [_dotb(x, x) for x in n2]
    m = [x + _dotb(x, y) for x, y in zip(m, n4)]
    sh = 3
    while (1 << sh) < DN_CHUNK:
        off = ((ii >> (sh + 1)) == (jj >> (sh + 1))) & ((ii >> sh) != (jj >> sh))
        cm = [_dotb(jnp.where(off, a, 0.0), x) for a, x in zip(mats, m)]
        m = [x - _dotb(x, y) for x, y in zip(m, cm)]
        sh += 1
    return m


def _dn_scan_kernel(qf_ref, kf_ref, vf_ref, gcf_ref, grf_ref, qb_ref, kb_ref, vb_ref, gcb_ref, grb_ref,
                    of_ref, ob_ref, s_ref):
    @pl.when(pl.program_id(2) == 0)
    def _():
        s_ref[...] = jnp.zeros_like(s_ref)

    c = DN_CHUNK
    ncb = N_CHUNKS_PER_BLOCK
    ii = lax.broadcasted_iota(jnp.int32, (c, c), 0)
    jj = lax.broadcasted_iota(jnp.int32, (c, c), 1)
    incl = (ii >= jj, ii <= jj)
    strict = (ii > jj, ii < jj)
    dirs = ((qf_ref, kf_ref, vf_ref, gcf_ref, grf_ref, of_ref),
            (qb_ref, kb_ref, vb_ref, gcb_ref, grb_ref, ob_ref))
    items = [(d, hh, ci) for d in range(2) for hh in range(DN_HB) for ci in range(ncb)]

    def rows(ci):
        return slice(ci * c, (ci + 1) * c)

    def cols(hh):
        return slice(hh * DN_DK, (hh + 1) * DN_DK)

    q = [dirs[d][0][rows(ci), cols(hh)] for d, hh, ci in items]
    k = [dirs[d][1][rows(ci), cols(hh)] for d, hh, ci in items]
    v = [dirs[d][2][rows(ci), cols(hh)] for d, hh, ci in items]
    gc = [dirs[d][3][hh, rows(ci), d:d + 1] for d, hh, ci in items]
    gr = [dirs[d][4][hh, d:d + 1, rows(ci)] for d, hh, ci in items]
    beta = [dirs[d][3][hh, rows(ci), 2 + d:3 + d] for d, hh, ci in items]

    decay = [jnp.where(incl[it[0]], jnp.exp(jnp.where(incl[it[0]], x - y, 0.0)), 0.0)
             for it, x, y in zip(items, gc, gr)]
    kb = [x * y for x, y in zip(k, beta)]
    a = [_dotb_nt(x, y) * jnp.where(strict[it[0]], z, 0.0) for it, x, y, z in zip(items, kb, k, decay)]
    attn = [_dotb_nt(x, y) * z for x, y, z in zip(q, k, decay)]
    tinv = _unit_tri_inverses(a, ii, jj)
    eg = [jnp.exp(x) for x in gc]
    uw = [_dotb(t, jnp.concatenate([x * y, z * e], axis=-1))
          for t, x, y, z, e in zip(tinv, v, beta, kb, eg)]
    g_last = [x[0:1] if it[0] else x[c - 1:c] for it, x in zip(items, gc)]
    wq = [jnp.concatenate([x[:, DN_DV:], y * e], axis=0) for x, y, e in zip(uw, q, eg)]
    k_dec = [x * jnp.exp(y - z) for x, y, z in zip(k, g_last, gc)]
    s_dec = [jnp.exp(x) for x in g_last]

    chains = [(d, hh) for d in range(2) for hh in range(DN_HB)]
    state = [s_ref[d, hh] for d, hh in chains]
    for step in range(ncb):
        cur = [items.index((d, hh, ncb - 1 - step if d else step)) for d, hh in chains]
        ws = [_dotb(wq[n], s) for n, s in zip(cur, state)]
        v_new = [uw[n][:, :DN_DV] - x[:c] for n, x in zip(cur, ws)]
        o = [x[c:] + _dotb(attn[n], y) for n, x, y in zip(cur, ws, v_new)]
        state = [s * s_dec[n] + _dotb_tn(k_dec[n], y) for n, s, y in zip(cur, state, v_new)]
        for n, x in zip(cur, o):
            d, hh, ci = items[n]
            dirs[d][5][rows(ci), cols(hh)] = x
    for (d, hh), s in zip(chains, state):
        s_ref[d, hh] = s


def _dn_scan(q, k, v, gates, n_lat_blocks, blocks_per_batch):
    t, d = q.shape
    bsz = n_lat_blocks // blocks_per_batch
    g4 = gates.reshape(t, 4, DN_HEADS)
    gcol = jnp.transpose(g4, (2, 0, 1))
    grow = jnp.transpose(g4, (2, 1, 0))

    def blk_f(b, s):
        return jnp.where(s == 0, n_lat_blocks + b, b * blocks_per_batch + s - 1)

    def blk_b(b, s):
        return jnp.where(s == 0, n_lat_blocks + b, b * blocks_per_batch + blocks_per_batch - s)

    hw = DN_HB * DN_DK

    def specs(blk):
        return [pl.BlockSpec((DN_BLOCK, hw), lambda b, hg, s: (blk(b, s), hg))] * 3 + [
            pl.BlockSpec((DN_HB, DN_BLOCK, 4), lambda b, hg, s: (hg, blk(b, s), 0)),
            pl.BlockSpec((DN_HB, 4, DN_BLOCK), lambda b, hg, s: (hg, 0, blk(b, s)))]

    return pl.pallas_call(
        _dn_scan_kernel,
        out_shape=[jax.ShapeDtypeStruct((t, d), F32)] * 2,
        grid=(bsz, DN_HEADS // DN_HB, 1 + blocks_per_batch),
        in_specs=specs(blk_f) + specs(blk_b),
        out_specs=[pl.BlockSpec((DN_BLOCK, hw), lambda b, hg, s: (blk_f(b, s), hg)),
                   pl.BlockSpec((DN_BLOCK, hw), lambda b, hg, s: (blk_b(b, s), hg))],
        scratch_shapes=[pltpu.VMEM((2, DN_HB, DN_DK, DN_DV), F32)],
        compiler_params=pltpu.CompilerParams(
            dimension_semantics=("arbitrary", "arbitrary", "arbitrary"),
            vmem_limit_bytes=VMEM_LIMIT_BYTES),
        name="dn_scan",
    )(q, k, v, gcol, grow, q, k, v, gcol, grow)


def _dn_out_kernel(of_ref, ob_ref, z_ref, on_ref, w_ref, x_ref, gate_ref, o_ref):
    z = z_ref[...]
    zs = z * jax.nn.sigmoid(z)
    parts = []
    for h in range(DN_HEADS):
        cols = slice(h * DN_DV, (h + 1) * DN_DV)
        o = of_ref[:, cols] + ob_ref[:, cols]
        o = o * lax.rsqrt(jnp.mean(o * o, axis=-1, keepdims=True) + EPS)
        parts.append(o)
    y = jnp.concatenate(parts, axis=-1) * on_ref[...] * zs
    o_ref[...] = x_ref[...] + gate_ref[0] * jnp.dot(y.astype(BF16), w_ref[...],
                                                    preferred_element_type=F32)


def _dn_out(o_f, o_b, z, out_norm, w_out, xs, gate3, n_lat_blocks, blocks_per_batch):
    t, d = xs.shape
    row = lambda i: (i, 0)
    fixed = lambda i: (0, 0)
    grp = lambda i: (_group_of_block(i, n_lat_blocks, blocks_per_batch), 0, 0)
    return pl.pallas_call(
        _dn_out_kernel,
        out_shape=jax.ShapeDtypeStruct((t, d), F32),
        grid=(t // DN_BLOCK,),
        in_specs=[pl.BlockSpec((DN_BLOCK, d), row),
                  pl.BlockSpec((DN_BLOCK, d), row),
                  pl.BlockSpec((DN_BLOCK, d), row),
                  pl.BlockSpec((1, d), fixed),
                  pl.BlockSpec((d, d), fixed),
                  pl.BlockSpec((DN_BLOCK, d), row),
                  pl.BlockSpec((1, 1, d), grp)],
        out_specs=pl.BlockSpec((DN_BLOCK, d), row),
        compiler_params=pltpu.CompilerParams(
            dimension_semantics=("arbitrary",), vmem_limit_bytes=VMEM_LIMIT_BYTES),
        name="dn_out",
    )(o_f, o_b, z, jnp.tile(out_norm, DN_HEADS).reshape(1, d), w_out.astype(BF16), xs, gate3)


def _deltanet_layer(xs, gain, shift3, scale3, w_in, conv_w, a_log, dt_bias, out_norm, w_out, gate3,
                    n_lat_blocks, blocks_per_batch):
    q, k, v, z, gates = _dn_inproj(xs, gain, shift3, scale3, w_in, conv_w, a_log, dt_bias, n_lat_blocks,
                                   blocks_per_batch)
    o_f, o_b = _dn_scan(q, k, v, gates, n_lat_blocks, blocks_per_batch)
    return _dn_out(o_f, o_b, z, out_norm, w_out, xs, gate3, n_lat_blocks, blocks_per_batch)


HY_N2 = 256
HY_CB = 16
HY_TOK_TILE = 512
HY_FILT_TILE = 512


def _dft_constants(nr):
    n1, n2 = 2 * nr, HY_N2
    n = n1 * n2
    nk = -(-(nr + 1) // 8) * 8
    keep = np.arange(nk) <= nr
    k1 = np.where(keep, np.arange(nk), 0)
    f1 = np.exp(-2j * np.pi * np.outer(k1, np.arange(nr)) / n1) * keep[:, None]
    twice = np.where((k1 > 0) & (k1 < nr), 2.0, 1.0) * keep
    lhs_fwd = np.concatenate([f1.real, f1.imag], axis=0)
    lhs_inv = np.concatenate([f1.real.T * twice, f1.imag.T * twice], axis=1) / n
    tw = np.exp(-2j * np.pi * np.outer(k1, np.arange(n2)) / n)
    a2 = np.arange(n2)
    f2 = np.exp(-2j * np.pi * np.outer(a2, a2) / n2)
    w_fwd = np.block([[f2.real, f2.imag], [-f2.imag, f2.real]])
    w_inv = np.block([[f2.real, -f2.imag], [f2.imag, f2.real]])
    return (jnp.asarray(lhs_fwd, BF16), jnp.asarray(lhs_inv, BF16), jnp.asarray(tw.real, F32),
            jnp.asarray(tw.imag, F32), jnp.asarray(w_fwd, BF16), jnp.asarray(w_inv, BF16))


def _hy_dft(x3, lhs_fwd, twr, twi, w_fwd):
    n1 = twr.shape[0]
    a = [jnp.dot(lhs_fwd, x3[c].astype(BF16), preferred_element_type=F32) for c in range(x3.shape[0])]
    br = jnp.concatenate([t[:n1] * twr - t[n1:] * twi for t in a], axis=0)
    bi = jnp.concatenate([t[:n1] * twi + t[n1:] * twr for t in a], axis=0)
    b = jnp.concatenate([br, bi], axis=1)
    return jnp.dot(b.astype(BF16), w_fwd, preferred_element_type=F32)


def _hy_idft(p, cb, lhs_inv, twr, twi, w_inv):
    n1, n2 = twr.shape
    c = jnp.dot(p.astype(BF16), w_inv, preferred_element_type=F32)
    out = []
    for ch in range(cb):
        cr = c[ch * n1:(ch + 1) * n1, :n2]
        ci = c[ch * n1:(ch + 1) * n1, n2:]
        d = jnp.concatenate([cr * twr + ci * twi, ci * twr - cr * twi], axis=0)
        out.append(jnp.dot(lhs_inv, d.astype(BF16), preferred_element_type=F32))
    return out


def _hy_spectrum_kernel(hf_ref, hb_ref, lf_ref, twr_ref, twi_ref, wf_ref, o_ref):
    cb, nr, n2 = hf_ref.shape
    first = ((lax.broadcasted_iota(jnp.int32, (nr, n2), 0) == 0)
             & (lax.broadcasted_iota(jnp.int32, (nr, n2), 1) == 0))
    hb = jnp.where(first, 0.0, hb_ref[...])
    consts = (lf_ref[...], twr_ref[...], twi_ref[...], wf_ref[...])
    xf = _hy_dft(hf_ref[...], *consts)
    xb = _hy_dft(hb, *consts)
    o_ref[...] = jnp.concatenate([xf[:, :n2] + xb[:, :n2], xf[:, n2:] - xb[:, n2:]],
                                 axis=1).reshape(o_ref.shape)


def _hy_spectrum(filt, consts):
    d = D_MODEL
    l = filt.shape[1]
    nr = l // HY_N2
    lhs_fwd, _, twr, twi, w_fwd, _ = consts
    n1 = twr.shape[0]
    cpo = d // HY_CB
    fixed2 = lambda o, c: (0, 0)
    return pl.pallas_call(
        _hy_spectrum_kernel,
        out_shape=jax.ShapeDtypeStruct((HY_ORDER * d, n1, 2 * HY_N2), F32),
        grid=(HY_ORDER, cpo),
        in_specs=[pl.BlockSpec((HY_CB, nr, HY_N2), lambda o, c: (2 * o * cpo + c, 0, 0)),
                  pl.BlockSpec((HY_CB, nr, HY_N2), lambda o, c: ((2 * o + 1) * cpo + c, 0, 0)),
                  pl.BlockSpec(lhs_fwd.shape, fixed2),
                  pl.BlockSpec(twr.shape, fixed2),
                  pl.BlockSpec(twi.shape, fixed2),
                  pl.BlockSpec(w_fwd.shape, fixed2)],
        out_specs=pl.BlockSpec((HY_CB, n1, 2 * HY_N2), lambda o, c: (o * cpo + c, 0, 0)),
        compiler_params=pltpu.CompilerParams(
            dimension_semantics=("arbitrary", "arbitrary"), vmem_limit_bytes=VMEM_LIMIT_BYTES),
        name="hy_spectrum",
    )(filt.reshape(-1, nr, HY_N2), filt.reshape(-1, nr, HY_N2), lhs_fwd, twr, twi, w_fwd)


def _hy_conv_kernel(z_ref, g_ref, k_ref, bias_ref, lf_ref, li_ref, twr_ref, twi_ref, wf_ref, wi_ref,
                    o_ref):
    cb, nr, n2 = z_ref.shape
    twr, twi = twr_ref[...], twi_ref[...]
    z = z_ref[...]
    x = _hy_dft(z, lf_ref[...], twr, twi, wf_ref[...])
    kk = k_ref[...].reshape(x.shape)
    xr, xi, kr, ki = x[:, :n2], x[:, n2:], kk[:, :n2], kk[:, n2:]
    p = jnp.concatenate([xr * kr - xi * ki, xr * ki + xi * kr], axis=1)
    conv = _hy_idft(p, cb, li_ref[...], twr, twi, wi_ref[...])
    for c in range(cb):
        o_ref[c] = g_ref[c] * (conv[c] + bias_ref[c] * z[c])


def _hy_conv(z, z_part, gate, gate_part, khat, order, bias, consts, bsz):
    d = D_MODEL
    l = z.shape[1] // bsz
    nr = l // HY_N2
    cpo = d // HY_CB
    lhs_fwd, lhs_inv, twr, twi, w_fwd, w_inv = consts
    n1 = twr.shape[0]
    fixed2 = lambda c, b: (0, 0)
    out = pl.pallas_call(
        _hy_conv_kernel,
        out_shape=jax.ShapeDtypeStruct((d, bsz * nr, HY_N2), F32),
        grid=(cpo, bsz),
        in_specs=[pl.BlockSpec((HY_CB, nr, HY_N2), lambda c, b: (z_part * cpo + c, b, 0)),
                  pl.BlockSpec((HY_CB, nr, HY_N2), lambda c, b: (gate_part * cpo + c, b, 0)),
                  pl.BlockSpec((HY_CB, n1, 2 * HY_N2), lambda c, b: (order * cpo + c, 0, 0)),
                  pl.BlockSpec((HY_CB, 1, 1), lambda c, b: (c, 0, 0)),
                  pl.BlockSpec(lhs_fwd.shape, fixed2),
                  pl.BlockSpec(lhs_inv.shape, fixed2),
                  pl.BlockSpec(twr.shape, fixed2),
                  pl.BlockSpec(twi.shape, fixed2),
                  pl.BlockSpec(w_fwd.shape, fixed2),
                  pl.BlockSpec(w_inv.shape, fixed2)],
        out_specs=pl.BlockSpec((HY_CB, nr, HY_N2), lambda c, b: (c, b, 0)),
        compiler_params=pltpu.CompilerParams(
            dimension_semantics=("arbitrary", "arbitrary"), vmem_limit_bytes=VMEM_LIMIT_BYTES),
        name="hy_conv",
    )(z.reshape(-1, bsz * nr, HY_N2), gate.reshape(-1, bsz * nr, HY_N2), khat,
      bias.reshape(d, 1, 1), lhs_fwd, lhs_inv, twr, twi, w_fwd, w_inv)
    return out.reshape(d, bsz * l)


def _hy_ctx_kernel(z_ref, g_ref, hf_ref, hb_ref, bias_ref, wf_ref, wi_ref, o_ref, *, bsz):
    l = hf_ref.shape[1]
    wf, wi = wf_ref[...], wi_ref[...]
    hb = jnp.where(lax.broadcasted_iota(jnp.int32, (1, l), 1) == 0, 0.0, hb_ref[...])
    kf = jnp.dot(hf_ref[...].astype(BF16), wf, preferred_element_type=F32)
    kb = jnp.dot(hb.astype(BF16), wf, preferred_element_type=F32)
    n = 2 * l
    kr, ki = kf[:, :n] + kb[:, :n], kf[:, n:] - kb[:, n:]
    bias = bias_ref[...]
    for b in range(bsz):
        z = z_ref[:, b * l:(b + 1) * l]
        x = jnp.dot(z.astype(BF16), wf, preferred_element_type=F32)
        xr, xi = x[:, :n], x[:, n:]
        p = jnp.concatenate([xr * kr - xi * ki, xr * ki + xi * kr], axis=1)
        conv = jnp.dot(p.astype(BF16), wi, preferred_element_type=F32)
        o_ref[:, b * l:(b + 1) * l] = g_ref[:, b * l:(b + 1) * l] * (conv + bias * z)


def _hy_ctx(z, z_part, gate, gate_part, filt, order, bias, bsz):
    d = D_MODEL
    l = filt.shape[1]
    n = 2 * l
    ang = 2 * np.pi * np.outer(np.arange(l), np.arange(n)) / n
    w_fwd = jnp.asarray(np.concatenate([np.cos(ang), -np.sin(ang)], axis=1), BF16)
    w_inv = jnp.asarray(np.concatenate([np.cos(ang.T), -np.sin(ang.T)], axis=0) / n, BF16)
    cb = 256
    nblk = d // cb
    fixed = lambda c: (0, 0)
    return pl.pallas_call(
        functools.partial(_hy_ctx_kernel, bsz=bsz),
        out_shape=jax.ShapeDtypeStruct((d, bsz * l), F32),
        grid=(nblk,),
        in_specs=[pl.BlockSpec((cb, bsz * l), lambda c: (z_part * nblk + c, 0)),
                  pl.BlockSpec((cb, bsz * l), lambda c: (gate_part * nblk + c, 0)),
                  pl.BlockSpec((cb, l), lambda c: (2 * order * nblk + c, 0)),
                  pl.BlockSpec((cb, l), lambda c: ((2 * order + 1) * nblk + c, 0)),
                  pl.BlockSpec((cb, 1), lambda c: (c, 0)),
                  pl.BlockSpec(w_fwd.shape, fixed),
                  pl.BlockSpec(w_inv.shape, fixed)],
        out_specs=pl.BlockSpec((cb, bsz * l), lambda c: (c, 0)),
        compiler_params=pltpu.CompilerParams(
            dimension_semantics=("arbitrary",), vmem_limit_bytes=VMEM_LIMIT_BYTES),
        name="hy_ctx_conv",
    )(z, gate, filt, filt, bias.reshape(d, 1), w_fwd, w_inv)


def _hy_filter_kernel(band_ref, w1t_ref, w1c_ref, w1s_ref, b1_ref, fr_ref, w2_ref, b2_ref, w3_ref,
                      delta_ref, o_ref, *, l):
    tl = o_ref.shape[1]
    d = D_MODEL
    hp = lax.Precision.HIGHEST
    pos = (lax.broadcasted_iota(jnp.int32, (1, tl), 1) + pl.program_id(0) * tl).astype(F32)
    t = pos / max(l - 1, 1)
    ang = ((2 * math.pi / l) * pos) * band_ref[...]
    fr = fr_ref[...]
    pre = (w1t_ref[...] * t + jnp.dot(w1c_ref[...], jnp.cos(ang), precision=hp)
           + jnp.dot(w1s_ref[...], -jnp.sin(ang), precision=hp) + b1_ref[...])
    hdn = jnp.sin(fr * pre)
    hdn = jnp.sin(fr * (jnp.dot(w2_ref[...], hdn, precision=hp) + b2_ref[...]))
    window = jnp.exp(-t * delta_ref[...])
    for part in range(2 * HY_ORDER):
        rows = slice(part * d, (part + 1) * d)
        o_ref[rows, :] = jnp.dot(w3_ref[rows, :], hdn, precision=hp) * window


def _hy_filter(l, w1, b1, freq, w2, b2, w3):
    d = D_MODEL
    nb = HY_BANDS
    tl = min(HY_FILT_TILE, l)
    col = lambda v: v.reshape(-1, 1)
    bands = jnp.linspace(1e-4, nb - 1, nb, dtype=F32)
    deltas = jnp.abs(jnp.linspace(math.log(HY_TARGET) / HY_SLOW, math.log(HY_TARGET) / HY_FAST, d, dtype=F32))
    w1t = w1.T
    args = (col(bands), w1t[:, 0:1], w1t[:, 1:1 + nb], w1t[:, 1 + nb:], col(b1), col(freq), w2.T, col(b2),
            w3.T, col(deltas))
    return pl.pallas_call(
        functools.partial(_hy_filter_kernel, l=l),
        out_shape=jax.ShapeDtypeStruct((2 * HY_ORDER * d, l), F32),
        grid=(l // tl,),
        in_specs=[pl.BlockSpec(a.shape, lambda j: (0, 0)) for a in args],
        out_specs=pl.BlockSpec((2 * HY_ORDER * d, tl), lambda j: (0, j)),
        compiler_params=pltpu.CompilerParams(
            dimension_semantics=("arbitrary",), vmem_limit_bytes=VMEM_LIMIT_BYTES),
        name="hy_filter",
    )(*args)


def _hy_inproj_kernel(x_ref, gain_ref, shift_ref, scale_ref, wt_ref, cw_ref, o_ref, *, seg):
    nch = wt_ref.shape[0]
    tm = x_ref.shape[0]
    hb = _modulated(x_ref[...], gain_ref[...], shift_ref[0], scale_ref[0]).astype(BF16)
    pos = lax.broadcasted_iota(jnp.int32, (1, tm), 1) & (seg - 1)
    not_first = pos != 0
    not_last = pos != seg - 1
    sub = 512
    for j in range(nch // sub):
        rows = slice(j * sub, (j + 1) * sub)
        p = _dotb_nt(wt_ref[rows, :], hb)
        cw = cw_ref[rows, :]
        prev = jnp.where(not_first, pltpu.roll(p, 1, axis=1), 0.0)
        nxt = jnp.where(not_last, pltpu.roll(p, tm - 1, axis=1), 0.0)
        o_ref[rows, :] = cw[:, 0:1] * prev + cw[:, 1:2] * p + cw[:, 2:3] * nxt


def _hy_inproj(xs, gain, shift3, scale3, w_in, conv_w, first_tile, n_tiles, seg, n_lat_blocks,
               blocks_per_batch):
    k = xs.shape[1]
    nch = w_in.shape[1]
    tm = HY_TOK_TILE
    per_tile = tm // DN_BLOCK
    grp = lambda i: (_group_of_block((first_tile + i) * per_tile, n_lat_blocks, blocks_per_batch), 0, 0)
    return pl.pallas_call(
        functools.partial(_hy_inproj_kernel, seg=seg),
        out_shape=jax.ShapeDtypeStruct((nch, n_tiles * tm), F32),
        grid=(n_tiles,),
        in_specs=[pl.BlockSpec((tm, k), lambda i: (first_tile + i, 0)),
                  pl.BlockSpec((1, k), lambda i: (0, 0)),
                  pl.BlockSpec((1, 1, k), grp),
                  pl.BlockSpec((1, 1, k), grp),
                  pl.BlockSpec((nch, k), lambda i: (0, 0)),
                  pl.BlockSpec((nch, 3), lambda i: (0, 0))],
        out_specs=pl.BlockSpec((nch, tm), lambda i: (0, i)),
        compiler_params=pltpu.CompilerParams(
            dimension_semantics=("arbitrary",), vmem_limit_bytes=VMEM_LIMIT_BYTES),
        name="hy_inproj",
    )(xs, gain.reshape(1, k), shift3, scale3, w_in.T.astype(BF16), conv_w.T)


def _hy_out_kernel(zl_ref, zc_ref, w_ref, x_ref, gate_ref, o_ref, *, n_lat_tiles):
    z = jnp.where(pl.program_id(0) >= n_lat_tiles, zc_ref[...], zl_ref[...])
    o_ref[...] = x_ref[...] + gate_ref[0] * _dotb_tn(z, w_ref[...])


def _hy_out(z_lat, z_ctx, w_out, xs, gate3, blocks_per_batch):
    t, d = xs.shape
    tm = HY_TOK_TILE
    n_lat_tiles = z_lat.shape[1] // tm
    per_tile = tm // DN_BLOCK
    grp = lambda i: (_group_of_block(i * per_tile, n_lat_tiles * per_tile, blocks_per_batch), 0, 0)
    return pl.pallas_call(
        functools.partial(_hy_out_kernel, n_lat_tiles=n_lat_tiles),
        out_shape=jax.ShapeDtypeStruct((t, d), F32),
        grid=(t // tm,),
        in_specs=[pl.BlockSpec((d, tm), lambda i: (0, jnp.minimum(i, n_lat_tiles - 1))),
                  pl.BlockSpec((d, tm), lambda i: (0, 0)),
                  pl.BlockSpec((d, d), lambda i: (0, 0)),
                  pl.BlockSpec((tm, d), lambda i: (i, 0)),
                  pl.BlockSpec((1, 1, d), grp)],
        out_specs=pl.BlockSpec((tm, d), lambda i: (i, 0)),
        compiler_params=pltpu.CompilerParams(
            dimension_semantics=("arbitrary",), vmem_limit_bytes=VMEM_LIMIT_BYTES),
        name="hy_out",
    )(z_lat, z_ctx, w_out.astype(BF16), xs, gate3)


def _hyena_layer(xs, gain, shift3, scale3, w_in, conv_w, f_w1, f_b1, f_freq, f_w2, f_b2, f_w3, bias, w_out,
                 gate3, bsz, seq):
    n_lat_tiles = bsz * seq // HY_TOK_TILE
    assert bsz * CTX_LEN == HY_TOK_TILE
    margs = (xs, gain, shift3, scale3, w_in, conv_w)
    blocks = (bsz * seq // DN_BLOCK, seq // DN_BLOCK)
    p_lat = _hy_inproj(*margs, 0, n_lat_tiles, GRID_W, *blocks)
    p_ctx = _hy_inproj(*margs, n_lat_tiles, 1, CTX_LEN, *blocks)
    fargs = (f_w1, f_b1, f_freq, f_w2, f_b2, f_w3)
    consts = _dft_constants(seq // HY_N2)
    khat = _hy_spectrum(_hy_filter(seq, *fargs), consts)
    filt_ctx = _hy_filter(CTX_LEN, *fargs)
    z_lat, z_ctx = p_lat, p_ctx
    for n in range(HY_ORDER):
        z_lat = _hy_conv(z_lat, 0, p_lat, n + 1, khat, n, bias[n], consts, bsz)
        z_ctx = _hy_ctx(z_ctx, 0, p_ctx, n + 1, filt_ctx, n, bias[n], bsz)
    return _hy_out(z_lat, z_ctx, w_out, xs, gate3, seq // DN_BLOCK)


MOE_TILE = 512
PACK = 2


def _route_kernel(x_ref, gain_ref, shift_ref, scale_ref, wr_ref, rb_ref, tri_ref,
                  f_ref, e_ref, w_ref, r_ref, cnt_ref, carry_ref):
    tm, d = x_ref.shape
    ne, epg, ng = N_EXPERTS, EXPERTS_PER_GROUP, N_GROUPS

    @pl.when(pl.program_id(0) == 0)
    def _():
        carry_ref[...] = jnp.zeros_like(carry_ref)

    x = x_ref[...]
    y = x * lax.rsqrt(jnp.mean(x * x, axis=-1, keepdims=True) + EPS) * gain_ref[...]
    f = y * (1 + scale_ref[0]) + shift_ref[0]
    bits = pltpu.bitcast(f.astype(BF16).astype(F32), jnp.uint32)
    half = d // PACK
    f_ref[...] = (bits[:, :half] >> 16) | (bits[:, half:] & jnp.uint32(0xFFFF0000))

    logits = lax.dot_general(wr_ref[...], f, (((1,), (1,)), ((), ())),
                             precision=lax.Precision.HIGHEST, preferred_element_type=F32)
    scores = jax.nn.sigmoid(logits)
    biased = scores + rb_ref[...]
    s = [scores[j * ng:(j + 1) * ng] for j in range(epg)]
    c = [biased[j * ng:(j + 1) * ng] for j in range(epg)]
    hi01, lo01 = jnp.maximum(c[0], c[1]), jnp.minimum(c[0], c[1])
    hi23, lo23 = jnp.maximum(c[2], c[3]), jnp.minimum(c[2], c[3])
    gscore = jnp.maximum(hi01, hi23) + jnp.maximum(jnp.minimum(hi01, hi23), jnp.maximum(lo01, lo23))
    gi = lax.broadcasted_iota(jnp.int32, (ng, tm), 0)
    gmax = jnp.max(gscore, axis=0, keepdims=True)
    grp = jnp.min(jnp.where(gscore == gmax, gi, ng), axis=0, keepdims=True)
    sel = gi == grp
    cv = [jnp.sum(jnp.where(sel, t, 0.0), axis=0, keepdims=True) for t in c]
    sv = [jnp.sum(jnp.where(sel, t, 0.0), axis=0, keepdims=True) for t in s]

    def pick(excluded):
        best = jnp.full((1, tm), -jnp.inf, F32)
        idx = jnp.zeros((1, tm), jnp.int32)
        val = jnp.zeros((1, tm), F32)
        for j in range(epg):
            cand = cv[j] if excluded is None else jnp.where(excluded == j, -jnp.inf, cv[j])
            take = cand > best
            best = jnp.where(take, cand, best)
            idx = jnp.where(take, j, idx)
            val = jnp.where(take, sv[j], val)
        return idx, val

    i1, v1 = pick(None)
    i2, v2 = pick(i1)
    e1 = grp * epg + i1
    e2 = grp * epg + i2
    wsum = v1 + v2
    e_ref[0:1, :] = e1
    e_ref[1:2, :] = e2
    w_ref[0:1, :] = v1 / wsum
    w_ref[1:2, :] = v2 / wsum

    ei = lax.broadcasted_iota(jnp.int32, (ne, tm), 0)
    oh1 = ei == e1
    oh2 = ei == e2
    tri = tri_ref[...]
    pre1 = jnp.dot(oh1.astype(BF16), tri, preferred_element_type=F32)
    pre2 = jnp.dot(oh2.astype(BF16), tri, preferred_element_type=F32)
    tot1 = pre1[:, tm - 1:tm]
    tot2 = pre2[:, tm - 1:tm]
    carry = carry_ref[:, 0:1]
    r1 = jnp.sum(jnp.where(oh1, carry + pre1 - 1.0, 0.0), axis=0, keepdims=True)
    r2 = jnp.sum(jnp.where(oh2, carry + tot1 + pre2 - 1.0, 0.0), axis=0, keepdims=True)
    r_ref[0:1, :] = r1.astype(jnp.int32)
    r_ref[1:2, :] = r2.astype(jnp.int32)
    carry_ref[...] = carry_ref[...] + (tot1 + tot2)
    cnt_ref[...] = carry_ref[...]


def _moe_route(xs, gain, shift3, scale3, w_router, router_bias, n_lat_tiles, blocks_per_batch):
    t, d = xs.shape
    tm = MOE_TILE
    per_tile = tm // DN_BLOCK
    row = lambda i: (i, 0)
    col = lambda i: (0, i)
    fixed = lambda i: (0, 0)
    grp = lambda i: (_group_of_block(i * per_tile, n_lat_tiles * per_tile, blocks_per_batch), 0, 0)
    tri = jnp.asarray(np.triu(np.ones((tm, tm), np.float32)), BF16)
    perm = np.arange(N_EXPERTS).reshape(N_GROUPS, EXPERTS_PER_GROUP).T.reshape(-1)
    return pl.pallas_call(
        _route_kernel,
        out_shape=[jax.ShapeDtypeStruct((t, d // PACK), jnp.uint32),
                   jax.ShapeDtypeStruct((TOP_K, t), jnp.int32),
                   jax.ShapeDtypeStruct((TOP_K, t), F32),
                   jax.ShapeDtypeStruct((TOP_K, t), jnp.int32),
                   jax.ShapeDtypeStruct((N_EXPERTS, 128), F32)],
        grid=(t // tm,),
        in_specs=[pl.BlockSpec((tm, d), row),
                  pl.BlockSpec((1, d), fixed),
                  pl.BlockSpec((1, 1, d), grp),
                  pl.BlockSpec((1, 1, d), grp),
                  pl.BlockSpec((N_EXPERTS, d), fixed),
                  pl.BlockSpec((N_EXPERTS, 1), fixed),
                  pl.BlockSpec((tm, tm), fixed)],
        out_specs=[pl.BlockSpec((tm, d // PACK), row),
                   pl.BlockSpec((TOP_K, tm), col),
                   pl.BlockSpec((TOP_K, tm), col),
                   pl.BlockSpec((TOP_K, tm), col),
                   pl.BlockSpec((N_EXPERTS, 128), fixed)],
        scratch_shapes=[pltpu.VMEM((N_EXPERTS, 128), F32)],
        compiler_params=pltpu.CompilerParams(
            dimension_semantics=("arbitrary",), vmem_limit_bytes=VMEM_LIMIT_BYTES),
        name="moe_route",
    )(xs, gain.reshape(1, d), shift3, scale3, w_router.T[perm], router_bias[perm].reshape(N_EXPERTS, 1), tri)


def _row_copy(src, src_row, dst, dst_row, sem):
    return pltpu.make_async_copy(src.at[pl.ds(src_row, 1)], dst.at[pl.ds(dst_row, 1)], sem)


def _dispatch_kernel(dest_ref, f_ref, xs_in_ref, xs_ref, dest_smem, sem, idx_sem):
    del xs_in_ref
    tm = f_ref.shape[0]
    idx_copy = pltpu.make_async_copy(dest_ref, dest_smem, idx_sem)
    idx_copy.start()
    idx_copy.wait()

    def issue(tok, carry):
        for k in range(TOP_K):
            _row_copy(f_ref, tok, xs_ref, dest_smem[k, tok], sem).start(priority=k)
        return carry

    def drain(tok, carry):
        for k in range(TOP_K):
            _row_copy(f_ref, 0, xs_ref, 0, sem).wait()
        return carry

    lax.fori_loop(0, tm, issue, 0, unroll=8)
    lax.fori_loop(0, tm, drain, 0, unroll=8)


def _dispatch(f_packed, dest, n_slots):
    t, wd = f_packed.shape
    tm = MOE_TILE
    return pl.pallas_call(
        _dispatch_kernel,
        out_shape=jax.ShapeDtypeStruct((n_slots, wd), jnp.uint32),
        grid=(t // tm,),
        in_specs=[pl.BlockSpec((TOP_K, tm), lambda i: (0, i)),
                  pl.BlockSpec((tm, wd), lambda i: (i, 0)),
                  pl.BlockSpec(memory_space=pl.ANY)],
        out_specs=pl.BlockSpec(memory_space=pl.ANY),
        scratch_shapes=[pltpu.SMEM((TOP_K, tm), jnp.int32),
                        pltpu.SemaphoreType.DMA, pltpu.SemaphoreType.DMA],
        input_output_aliases={2: 0},
        compiler_params=pltpu.CompilerParams(
            dimension_semantics=("arbitrary",), vmem_limit_bytes=VMEM_LIMIT_BYTES),
        name="moe_dispatch",
    )(dest, f_packed, jnp.zeros((n_slots, wd), jnp.uint32))


def _experts_kernel(be_ref, na_ref, x_ref, wg_ref, wu_ref, wd_ref, o_ref, wgb_ref, wub_ref, wdb_ref):
    i = pl.program_id(0)
    prev = be_ref[jnp.maximum(i - 1, 0)]

    @pl.when((i == 0) | (be_ref[i] != prev))
    def _():
        wgb_ref[...] = wg_ref[0, 0].astype(BF16)
        wub_ref[...] = wu_ref[0, 0].astype(BF16)
        wdb_ref[...] = wd_ref[0, 0].astype(BF16)

    @pl.when(i < na_ref[0])
    def _():
        packed = x_ref[...]
        lo = pltpu.bitcast(packed << 16, F32)
        hi = pltpu.bitcast(packed & jnp.uint32(0xFFFF0000), F32)
        xb = jnp.concatenate([lo, hi], axis=-1).astype(BF16)
        g = jnp.dot(xb, wgb_ref[...], preferred_element_type=F32)
        u = jnp.dot(xb, wub_ref[...], preferred_element_type=F32)
        hid = (g * jax.nn.sigmoid(g)) * u
        y = jnp.dot(hid.astype(BF16), wdb_ref[...], preferred_element_type=F32)
        ybits = pltpu.bitcast(y.astype(BF16).astype(F32), jnp.uint32)
        half = y.shape[1] // PACK
        o_ref[...] = (ybits[:, :half] >> 16) | (ybits[:, half:] & jnp.uint32(0xFFFF0000))

    @pl.when(i >= na_ref[0])
    def _():
        o_ref[...] = jnp.zeros_like(o_ref)


def _experts(xs_sorted, block_expert, n_active, w_gate, w_up, w_down, layer):
    n_slots, wd = xs_sorted.shape
    d = wd * PACK
    n_blocks = n_slots // MOE_BLOCK
    blk = lambda i, be, na: (jnp.minimum(i, na[0] - 1), 0)
    wsel = lambda i, be, na: (layer, be[jnp.minimum(i, na[0] - 1)], 0, 0)
    grid_spec = pltpu.PrefetchScalarGridSpec(
        num_scalar_prefetch=2,
        grid=(n_blocks,),
        in_specs=[pl.BlockSpec((MOE_BLOCK, wd), blk),
                  pl.BlockSpec((1, 1, d, D_EXPERT), wsel),
                  pl.BlockSpec((1, 1, d, D_EXPERT), wsel),
                  pl.BlockSpec((1, 1, D_EXPERT, d), wsel)],
        out_specs=pl.BlockSpec((MOE_BLOCK, wd), lambda i, be, na: (i, 0)),
        scratch_shapes=[pltpu.VMEM((d, D_EXPERT), BF16), pltpu.VMEM((d, D_EXPERT), BF16),
                        pltpu.VMEM((D_EXPERT, d), BF16)],
    )
    return pl.pallas_call(
        _experts_kernel,
        out_shape=jax.ShapeDtypeStruct((n_slots, wd), jnp.uint32),
        grid_spec=grid_spec,
        compiler_params=pltpu.CompilerParams(
            dimension_semantics=("arbitrary",), vmem_limit_bytes=VMEM_LIMIT_BYTES),
        name="moe_experts",
    )(block_expert, n_active, xs_sorted, w_gate, w_up, w_down)


def _combine_kernel(dest_ref, y_ref, x_ref, w_ref, gate_ref, fin_ref, o_ref, dest_smem, ya_ref, yb_ref, sem,
                    idx_sem, *, final_norm):
    tm = x_ref.shape[0]
    idx_copy = pltpu.make_async_copy(dest_ref, dest_smem, idx_sem)
    idx_copy.start()
    idx_copy.wait()
    bufs = (ya_ref, yb_ref)

    def issue(tok, carry):
        for k in range(TOP_K):
            _row_copy(y_ref, dest_smem[k, tok], bufs[k], tok, sem).start(priority=k)
        return carry

    def drain(tok, carry):
        for k in range(TOP_K):
            _row_copy(y_ref, 0, bufs[k], 0, sem).wait()
        return carry

    lax.fori_loop(0, tm, issue, 0, unroll=8)
    lax.fori_loop(0, tm, drain, 0, unroll=8)
    w = w_ref[...]
    out = x_ref[...] + gate_ref[0] * (w[:, 0:1] * ya_ref[...] + w[:, 1:2] * yb_ref[...])
    if final_norm:
        out = out * lax.rsqrt(jnp.mean(out * out, axis=-1, keepdims=True) + EPS) * fin_ref[...]
    o_ref[...] = out


def _combine(ys, dest, weight_cols, xs, gate3, n_lat_tiles, blocks_per_batch, final_gain=None):
    t, d = xs.shape
    tm = MOE_TILE
    per_tile = tm // DN_BLOCK
    n_tiles = t // tm if final_gain is None else n_lat_tiles
    fin = jnp.ones((1, d), F32) if final_gain is None else final_gain.reshape(1, d)
    row = lambda i: (i, 0)
    grp = lambda i: (_group_of_block(i * per_tile, n_lat_tiles * per_tile, blocks_per_batch), 0, 0)
    return pl.pallas_call(
        functools.partial(_combine_kernel, final_norm=final_gain is not None),
        out_shape=jax.ShapeDtypeStruct((n_tiles * tm, d), F32),
        grid=(n_tiles,),
        in_specs=[pl.BlockSpec((TOP_K, tm), lambda i: (0, i)),
                  pl.BlockSpec(memory_space=pl.ANY),
                  pl.BlockSpec((tm, d), row),
                  pl.BlockSpec((tm, TOP_K), row),
                  pl.BlockSpec((1, 1, d), grp),
                  pl.BlockSpec((1, d), lambda i: (0, 0))],
        out_specs=pl.BlockSpec((tm, d), row),
        scratch_shapes=[pltpu.SMEM((TOP_K, tm), jnp.int32),
                        pltpu.VMEM((tm, d), F32), pltpu.VMEM((tm, d), F32),
                        pltpu.SemaphoreType.DMA, pltpu.SemaphoreType.DMA],
        compiler_params=pltpu.CompilerParams(
            dimension_semantics=("arbitrary",), vmem_limit_bytes=VMEM_LIMIT_BYTES),
        name="moe_combine",
    )(dest, ys, xs, weight_cols, gate3, fin)


def _moe_layer(xs, gain, shift3, scale3, gate3, w_router, router_bias, w_gate, w_up, w_down, layer,
               n_lat_tiles, blocks_per_batch, final_gain=None):
    t = xs.shape[0]
    f_packed, expert, weight, rank, counts = _moe_route(xs, gain, shift3, scale3, w_router, router_bias,
                                                        n_lat_tiles, blocks_per_batch)
    counts = counts[:, 0].astype(jnp.int32)
    padded = (counts + MOE_BLOCK - 1) // MOE_BLOCK * MOE_BLOCK
    pend = jnp.cumsum(padded)
    pstart = pend - padded
    n_blocks = -(-(t * TOP_K) // MOE_BLOCK) + N_EXPERTS
    block_start = jnp.arange(n_blocks, dtype=jnp.int32) * MOE_BLOCK
    block_expert = jnp.minimum(jnp.sum(pend[None, :] <= block_start[:, None], axis=1),
                               N_EXPERTS - 1).astype(jnp.int32)
    n_active = (pend[-1:] // MOE_BLOCK).astype(jnp.int32)
    is_e = expert[..., None] == jnp.arange(N_EXPERTS, dtype=jnp.int32)
    dest = rank + jnp.sum(jnp.where(is_e, pstart, 0), axis=-1)
    xs_sorted = _dispatch(f_packed, dest, n_blocks * MOE_BLOCK)
    ys = _experts(xs_sorted, block_expert, n_active, w_gate, w_up, w_down, layer)
    return _combine(ys, dest, weight.T, xs, gate3, n_lat_tiles, blocks_per_batch, final_gain)


MOE_CHUNK = 8
MOE_LB = 1280
assert MOE_LB >= MOE_TILE * TOP_K + N_EXPERTS * (MOE_CHUNK - 1) and MOE_LB % 128 == 0
N_CHUNKS = MOE_LB // MOE_CHUNK
TAB_W = 256
assert TAB_W >= N_CHUNKS


def _moe_route_kernel(x_ref, gain_ref, shift_ref, scale_ref, wr_ref, rb_ref, tri_ref, lt_ref,
                      f_ref, pos_ref, w_ref, cnt_ref):
    tm, d = x_ref.shape
    ne, epg, ng = N_EXPERTS, EXPERTS_PER_GROUP, N_GROUPS
    f = _modulated(x_ref[...], gain_ref[...], shift_ref[0], scale_ref[0])
    f_ref[...] = f.astype(BF16)

    logits = lax.dot_general(wr_ref[...], f, (((1,), (1,)), ((), ())),
                             precision=lax.Precision.HIGHEST, preferred_element_type=F32)
    scores = jax.nn.sigmoid(logits)
    biased = scores + rb_ref[...]
    s = [scores[j * ng:(j + 1) * ng] for j in range(epg)]
    c = [biased[j * ng:(j + 1) * ng] for j in range(epg)]
    hi01, lo01 = jnp.maximum(c[0], c[1]), jnp.minimum(c[0], c[1])
    hi23, lo23 = jnp.maximum(c[2], c[3]), jnp.minimum(c[2], c[3])
    gscore = jnp.maximum(hi01, hi23) + jnp.maximum(jnp.minimum(hi01, hi23), jnp.maximum(lo01, lo23))
    gi = lax.broadcasted_iota(jnp.int32, (ng, tm), 0)
    gmax = jnp.max(gscore, axis=0, keepdims=True)
    grp = jnp.min(jnp.where(gscore == gmax, gi, ng), axis=0, keepdims=True)
    sel = gi == grp
    cv = [jnp.sum(jnp.where(sel, t, 0.0), axis=0, keepdims=True) for t in c]
    sv = [jnp.sum(jnp.where(sel, t, 0.0), axis=0, keepdims=True) for t in s]

    def pick(excluded):
        best = jnp.full((1, tm), -jnp.inf, F32)
        idx = jnp.zeros((1, tm), jnp.int32)
        val = jnp.zeros((1, tm), F32)
        for j in range(epg):
            cand = cv[j] if excluded is None else jnp.where(excluded == j, -jnp.inf, cv[j])
            take = cand > best
            best = jnp.where(take, cand, best)
            idx = jnp.where(take, j, idx)
            val = jnp.where(take, sv[j], val)
        return idx, val

    i1, v1 = pick(None)
    i2, v2 = pick(i1)
    wsum = v1 + v2
    w_ref[0:1, :] = v1 / wsum
    w_ref[1:2, :] = v2 / wsum

    ei = lax.broadcasted_iota(jnp.int32, (ne, tm), 0)
    oh1 = ei == grp * epg + i1
    oh2 = ei == grp * epg + i2
    tri = tri_ref[...]
    pre1 = jnp.dot(oh1.astype(BF16), tri, preferred_element_type=F32)
    pre2 = jnp.dot(oh2.astype(BF16), tri, preferred_element_type=F32)
    tot1 = pre1[:, tm - 1:tm]
    tot = tot1 + pre2[:, tm - 1:tm]
    seg = jnp.floor((tot + (MOE_CHUNK - 1)) * (1.0 / MOE_CHUNK)) * MOE_CHUNK
    off = jnp.dot(lt_ref[...], jnp.broadcast_to(seg, (ne, 128)).astype(BF16),
                  preferred_element_type=F32)[:, 0:1]
    p1 = jnp.sum(jnp.where(oh1, off + pre1 - 1.0, 0.0), axis=0, keepdims=True)
    p2 = jnp.sum(jnp.where(oh2, off + tot1 + pre2 - 1.0, 0.0), axis=0, keepdims=True)
    pos_ref[0:1, :] = p1.astype(jnp.int32)
    pos_ref[1:2, :] = p2.astype(jnp.int32)
    cnt_ref[0] = jnp.broadcast_to(tot, (ne, 128))


def _moe_route(xs, gain, shift3, scale3, w_router, router_bias, n_lat_tiles, blocks_per_batch):
    t, d = xs.shape
    tm = MOE_TILE
    ne = N_EXPERTS
    per_tile = tm // DN_BLOCK
    row = lambda i: (i, 0)
    col = lambda i: (0, i)
    fixed = lambda i: (0, 0)
    grp = lambda i: (_group_of_block(i * per_tile, n_lat_tiles * per_tile, blocks_per_batch), 0, 0)
    tri = jnp.asarray(np.triu(np.ones((tm, tm), np.float32)), BF16)
    lt = jnp.asarray(np.tril(np.ones((ne, ne), np.float32), -1), BF16)
    perm = np.arange(ne).reshape(N_GROUPS, EXPERTS_PER_GROUP).T.reshape(-1)
    return pl.pallas_call(
        _moe_route_kernel,
        out_shape=[jax.ShapeDtypeStruct((t, d), BF16),
                   jax.ShapeDtypeStruct((TOP_K, t), jnp.int32),
                   jax.ShapeDtypeStruct((TOP_K, t), F32),
                   jax.ShapeDtypeStruct((t // tm, ne, 128), F32)],
        grid=(t // tm,),
        in_specs=[pl.BlockSpec((tm, d), row),
                  pl.BlockSpec((1, d), fixed),
                  pl.BlockSpec((1, 1, d), grp),
                  pl.BlockSpec((1, 1, d), grp),
                  pl.BlockSpec((ne, d), fixed),
                  pl.BlockSpec((ne, 1), fixed),
                  pl.BlockSpec((tm, tm), fixed),
                  pl.BlockSpec((ne, ne), fixed)],
        out_specs=[pl.BlockSpec((tm, d), row),
                   pl.BlockSpec((TOP_K, tm), col),
                   pl.BlockSpec((TOP_K, tm), col),
                   pl.BlockSpec((1, ne, 128), lambda i: (i, 0, 0))],
        compiler_params=pltpu.CompilerParams(
            dimension_semantics=("arbitrary",), vmem_limit_bytes=VMEM_LIMIT_BYTES),
        name="moe_route",
    )(xs, gain.reshape(1, d), shift3, scale3, w_router.T[perm], router_bias[perm].reshape(ne, 1), tri, lt)


def _chunk_row(j):
    return j * MOE_CHUNK if isinstance(j, int) else pl.multiple_of(j * MOE_CHUNK, MOE_CHUNK)


def _chunk_copies(tab_smem, make_copy):
    def issue(j, n):
        dst = tab_smem[0, j]

        @pl.when(dst >= 0)
        def _():
            make_copy(j, pl.multiple_of(dst, MOE_CHUNK)).start()

        return n + (dst >= 0).astype(jnp.int32)

    n = lax.fori_loop(0, N_CHUNKS, issue, jnp.int32(0), unroll=8)

    def drain(j, carry):
        make_copy(0, 0).wait()
        return carry

    lax.fori_loop(0, n, drain, 0)


def _moe_dispatch_kernel(tab_ref, pos_ref, f_ref, xs_in_ref, xs_ref, tab_smem, loc_ref, sem, idx_sem):
    del xs_in_ref
    tm, d = f_ref.shape
    idx_copy = pltpu.make_async_copy(tab_ref.at[0], tab_smem, idx_sem)
    idx_copy.start()
    r = lax.broadcasted_iota(jnp.int32, (tm, MOE_LB), 1)
    p = pos_ref[...]
    onehot = ((p[:, 0:1] == r) | (p[:, 1:2] == r)).astype(BF16)
    loc = _dotb_tn(onehot, f_ref[...])
    bits = pltpu.bitcast(loc, jnp.uint32)
    half = d // PACK
    loc_ref[...] = (bits[:, :half] >> 16) | (bits[:, half:] & jnp.uint32(0xFFFF0000))
    idx_copy.wait()

    def make_copy(j, dst):
        return pltpu.make_async_copy(loc_ref.at[pl.ds(_chunk_row(j), MOE_CHUNK)],
                                     xs_ref.at[pl.ds(dst, MOE_CHUNK)], sem)

    _chunk_copies(tab_smem, make_copy)


def _moe_dispatch(f, pos_cols, table, n_slots):
    t, d = f.shape
    tm = MOE_TILE
    wd = d // PACK
    return pl.pallas_call(
        _moe_dispatch_kernel,
        out_shape=jax.ShapeDtypeStruct((n_slots, wd), jnp.uint32),
        grid=(t // tm,),
        in_specs=[pl.BlockSpec((1, 1, TAB_W), lambda i: (i, 0, 0)),
                  pl.BlockSpec((tm, TOP_K), lambda i: (i, 0)),
                  pl.BlockSpec((tm, d), lambda i: (i, 0)),
                  pl.BlockSpec(memory_space=pl.ANY)],
        out_specs=pl.BlockSpec(memory_space=pl.ANY),
        scratch_shapes=[pltpu.SMEM((1, TAB_W), jnp.int32),
                        pltpu.VMEM((MOE_LB, wd), jnp.uint32),
                        pltpu.SemaphoreType.DMA, pltpu.SemaphoreType.DMA],
        input_output_aliases={3: 0},
        compiler_params=pltpu.CompilerParams(
            dimension_semantics=("arbitrary",), vmem_limit_bytes=VMEM_LIMIT_BYTES),
        name="moe_dispatch",
    )(table, pos_cols, f, jnp.zeros((n_slots, wd), jnp.uint32))


def _moe_combine_kernel(tab_ref, pos_ref, w_ref, y_ref, x_ref, gate_ref, fin_ref, o_ref, tab_smem, yloc_ref,
                        sem, idx_sem, *, final_norm):
    tm = x_ref.shape[0]

    @pl.when(pl.program_id(0) == 0)
    def _():
        yloc_ref[...] = jnp.zeros_like(yloc_ref)

    idx_copy = pltpu.make_async_copy(tab_ref.at[0], tab_smem, idx_sem)
    idx_copy.start()
    idx_copy.wait()

    def make_copy(j, src):
        return pltpu.make_async_copy(y_ref.at[pl.ds(src, MOE_CHUNK)],
                                     yloc_ref.at[pl.ds(_chunk_row(j), MOE_CHUNK)],
                                     sem)

    _chunk_copies(tab_smem, make_copy)
    r = lax.broadcasted_iota(jnp.int32, (tm, MOE_LB), 1)
    p = pos_ref[...]
    w = w_ref[...]
    wmat = jnp.where(p[:, 0:1] == r, w[:, 0:1], 0.0) + jnp.where(p[:, 1:2] == r, w[:, 1:2], 0.0)
    packed = yloc_ref[...]
    y_lo = pltpu.bitcast(packed << 16, F32)
    y_hi = pltpu.bitcast(packed & jnp.uint32(0xFFFF0000), F32)
    moe = jnp.concatenate([_dotb(wmat, y_lo), _dotb(wmat, y_hi)], axis=-1)
    out = x_ref[...] + gate_ref[0] * moe
    if final_norm:
        out = out * lax.rsqrt(jnp.mean(out * out, axis=-1, keepdims=True) + EPS) * fin_ref[...]
    o_ref[...] = out


def _moe_combine(ys, pos_cols, weight_cols, table, xs, gate3, n_lat_tiles, blocks_per_batch, final_gain=None):
    t, d = xs.shape
    tm = MOE_TILE
    per_tile = tm // DN_BLOCK
    n_tiles = t // tm if final_gain is None else n_lat_tiles
    fin = jnp.ones((1, d), F32) if final_gain is None else final_gain.reshape(1, d)
    row = lambda i: (i, 0)
    grp = lambda i: (_group_of_block(i * per_tile, n_lat_tiles * per_tile, blocks_per_batch), 0, 0)
    return pl.pallas_call(
        functools.partial(_moe_combine_kernel, final_norm=final_gain is not None),
        out_shape=jax.ShapeDtypeStruct((n_tiles * tm, d), F32),
        grid=(n_tiles,),
        in_specs=[pl.BlockSpec((1, 1, TAB_W), lambda i: (i, 0, 0)),
                  pl.BlockSpec((tm, TOP_K), row),
                  pl.BlockSpec((tm, TOP_K), row),
                  pl.BlockSpec(memory_space=pl.ANY),
                  pl.BlockSpec((tm, d), row),
                  pl.BlockSpec((1, 1, d), grp),
                  pl.BlockSpec((1, d), lambda i: (0, 0))],
        out_specs=pl.BlockSpec((tm, d), row),
        scratch_shapes=[pltpu.SMEM((1, TAB_W), jnp.int32),
                        pltpu.VMEM((MOE_LB, d // PACK), jnp.uint32),
                        pltpu.SemaphoreType.DMA, pltpu.SemaphoreType.DMA],
        compiler_params=pltpu.CompilerParams(
            dimension_semantics=("arbitrary",), vmem_limit_bytes=VMEM_LIMIT_BYTES),
        name="moe_combine",
    )(table, pos_cols, weight_cols, ys, xs, gate3, fin)


def _moe_layer(xs, gain, shift3, scale3, gate3, w_router, router_bias, w_gate, w_up, w_down, layer,
               n_lat_tiles, blocks_per_batch, final_gain=None):
    t = xs.shape[0]
    n_tiles = t // MOE_TILE
    f, pos, weight, cnt = _moe_route(xs, gain, shift3, scale3, w_router, router_bias,
                                     n_lat_tiles, blocks_per_batch)
    seg = (cnt[:, :, 0].astype(jnp.int32) + MOE_CHUNK - 1) // MOE_CHUNK * MOE_CHUNK
    loc_end = jnp.cumsum(seg, axis=1)
    loc_off = loc_end - seg
    padded = (jnp.sum(seg, axis=0) + MOE_BLOCK - 1) // MOE_BLOCK * MOE_BLOCK
    pend = jnp.cumsum(padded)
    seg_start = (pend - padded)[None, :] + jnp.cumsum(seg, axis=0) - seg
    n_blocks = -(-(t * TOP_K + n_tiles * N_EXPERTS * (MOE_CHUNK - 1)) // MOE_BLOCK) + N_EXPERTS
    block_start = jnp.arange(n_blocks, dtype=jnp.int32) * MOE_BLOCK
    block_expert = jnp.minimum(jnp.sum(pend[None, :] <= block_start[:, None], axis=1),
                               N_EXPERTS - 1).astype(jnp.int32)
    n_active = (pend[-1:] // MOE_BLOCK).astype(jnp.int32)
    row0 = jnp.arange(N_CHUNKS, dtype=jnp.int32) * MOE_CHUNK
    e_of = jnp.sum(loc_end[:, None, :] <= row0[None, :, None], axis=-1)
    is_e = e_of[..., None] == jnp.arange(N_EXPERTS, dtype=jnp.int32)
    shift = jnp.sum(jnp.where(is_e, (seg_start - loc_off)[:, None, :], 0), axis=-1)
    table = jnp.where(e_of < N_EXPERTS, row0[None, :] + shift, -1)
    table = jnp.pad(table, ((0, 0), (0, TAB_W - N_CHUNKS)), constant_values=-1).reshape(n_tiles, 1, TAB_W)

    pos_cols = pos.T
    xs_sorted = _moe_dispatch(f, pos_cols, table, n_blocks * MOE_BLOCK)
    ys = _experts(xs_sorted, block_expert, n_active, w_gate, w_up, w_down, layer)
    return _moe_combine(ys, pos_cols, weight.T, table, xs, gate3, n_lat_tiles, blocks_per_batch, final_gain)


def _rmsnorm(x, gain):
    y = x * lax.rsqrt(jnp.mean(x * x, axis=-1, keepdims=True) + EPS)
    return y * gain


def _modulate(x, gain, shift, scale):
    return _rmsnorm(x, gain) * (1 + scale) + shift


def _l2norm(t):
    return t * lax.rsqrt(jnp.sum(t * t, axis=-1, keepdims=True) + EPS)


def _short_conv(x, w, on_grid):
    b, l, ch = x.shape
    xs = x.reshape(b, l // GRID_W, GRID_W, ch) if on_grid else x.reshape(b, 1, l, ch)
    n = xs.shape[2]
    xp = jnp.pad(xs, ((0, 0), (0, 0), (1, 1), (0, 0)))
    y = w[0] * xp[:, :, 0:n] + w[1] * xp[:, :, 1:n + 1] + w[2] * xp[:, :, 2:n + 2]
    return y.reshape(b, l, ch)


def _gated_delta_chunked(q, k, v, g, beta, s0):
    b, h, l, dk = q.shape
    dv = v.shape[-1]
    c = DN_CHUNK
    n = l // c
    q = q.reshape(b, h, n, c, dk)
    k = k.reshape(b, h, n, c, dk)
    v = v.reshape(b, h, n, c, dv)
    g = jnp.cumsum(g.reshape(b, h, n, c), axis=-1)
    beta = beta.reshape(b, h, n, c, 1)
    pos = jnp.arange(c)
    incl = pos[:, None] >= pos[None, :]
    strict = pos[:, None] > pos[None, :]
    decay = jnp.exp(jnp.where(incl, g[..., :, None] - g[..., None, :], -jnp.inf))
    kb = k * beta
    a_mat = jnp.einsum('bhnid,bhnjd->bhnij', kb, k) * jnp.where(strict, decay, 0.0)
    rhs = jnp.concatenate([v * beta, kb * jnp.exp(g)[..., None]], axis=-1)
    sol = lax.linalg.triangular_solve(a_mat + jnp.eye(c, dtype=a_mat.dtype), rhs,
                                      left_side=True, lower=True, unit_diagonal=True)
    u, w = sol[..., :dv], sol[..., dv:]
    attn = jnp.einsum('bhnid,bhnjd->bhnij', q, k) * decay
    g_last = g[..., -1:]
    q_dec = q * jnp.exp(g)[..., None]
    k_dec = k * jnp.exp(g_last - g)[..., None]

    def step(s, inp):
        qd, kd, uu, ww, at, gl = inp
        v_new = uu - jnp.einsum('bhck,bhkv->bhcv', ww, s)
        o = jnp.einsum('bhck,bhkv->bhcv', qd, s) + jnp.einsum('bhcs,bhsv->bhcv', at, v_new)
        s = s * jnp.exp(gl)[..., None] + jnp.einsum('bhck,bhcv->bhkv', kd, v_new)
        return s, o

    xs = tuple(jnp.moveaxis(t, 2, 0) for t in (q_dec, k_dec, u, w, attn, g_last))
    s_final, o = lax.scan(step, s0, xs)
    o = jnp.moveaxis(o, 0, 2).reshape(b, h, l, dv)
    return o, s_final


def _deltanet_mixer(p_ctx, p_lat, conv_w, a_log, dt_bias, out_norm):
    d = D_MODEL
    nh = DN_HEADS

    def project(p, on_grid):
        b, l, _ = p.shape
        qkv = jax.nn.silu(_short_conv(p[..., :3 * d], conv_w, on_grid))
        z = p[..., 3 * d:4 * d]
        a = p[..., 4 * d:4 * d + 2 * nh].reshape(b, l, 2, nh)
        bb = p[..., 4 * d + 2 * nh:].reshape(b, l, 2, nh)

        def heads(t):
            return jnp.transpose(t.reshape(b, l, nh, -1), (0, 2, 1, 3))

        q, k, v = (heads(t) for t in jnp.split(qkv, 3, axis=-1))
        q = _l2norm(q) * DN_DK ** -0.5
        k = _l2norm(k)
        g = -jnp.exp(a_log) * jax.nn.softplus(a + dt_bias)
        g = jnp.transpose(g, (2, 0, 3, 1))
        beta = jnp.transpose(jax.nn.sigmoid(bb), (2, 0, 3, 1))
        return q, k, v, g, beta, z

    def scan_both(q, k, v, g, beta, s_f, s_b):
        o_f, s_f = _gated_delta_chunked(q, k, v, g[0], beta[0], s_f)
        rev = lambda t: jnp.flip(t, axis=2)
        o_b, s_b = _gated_delta_chunked(rev(q), rev(k), rev(v), rev(g[1]), rev(beta[1]), s_b)
        return o_f + rev(o_b), s_f, s_b

    def finish(o, z):
        b, _, l, _ = o.shape
        o = jnp.transpose(o, (0, 2, 1, 3))
        o = o * lax.rsqrt(jnp.mean(o * o, axis=-1, keepdims=True) + EPS) * out_norm
        o = o * jax.nn.silu(z.reshape(b, l, nh, DN_DV))
        return o.reshape(b, l, d)

    qc, kc, vc, gc, bc, zc = project(p_ctx, False)
    s0 = jnp.zeros((p_ctx.shape[0], nh, DN_DK, DN_DV), F32)
    o_c, s_f, s_b = scan_both(qc, kc, vc, gc, bc, s0, s0)
    ql, kl, vl, gla, bl, zl = project(p_lat, True)
    o_l, _, _ = scan_both(ql, kl, vl, gla, bl, s_f, s_b)
    return finish(o_c, zc), finish(o_l, zl)


def _hyena_filters(l, w1, b1, freq, w2, b2, w3):
    pos = jnp.arange(l, dtype=F32)[:, None]
    t = pos / max(l - 1, 1)
    bands = jnp.linspace(1e-4, HY_BANDS - 1, HY_BANDS, dtype=F32)[None, :]
    ang = (2 * math.pi / l) * pos * bands
    feat = jnp.concatenate([t, jnp.cos(ang), -jnp.sin(ang)], axis=-1)
    hp = lax.Precision.HIGHEST
    hdn = jnp.sin(freq * (jnp.dot(feat, w1, precision=hp) + b1))
    hdn = jnp.sin(freq * (jnp.dot(hdn, w2, precision=hp) + b2))
    filt = jnp.dot(hdn, w3, precision=hp).reshape(l, HY_ORDER, 2, D_MODEL)
    deltas = jnp.abs(jnp.linspace(math.log(HY_TARGET) / HY_SLOW, math.log(HY_TARGET) / HY_FAST,
                                  D_MODEL, dtype=F32))
    window = jnp.exp(-t * deltas[None, :])
    return filt * window[:, None, None, :]


def _two_sided_fftconv(u, h_fwd, h_bwd):
    l = u.shape[1]
    k = jnp.concatenate([h_fwd, jnp.zeros_like(h_fwd[:1]), jnp.flip(h_bwd[1:], axis=0)], axis=0)
    kf = jnp.fft.rfft(k, axis=0)
    uf = jnp.fft.rfft(u, n=2 * l, axis=1)
    return jnp.fft.irfft(uf * kf[None], n=2 * l, axis=1)[:, :l]


def _hyena_stream(p, on_grid, conv_w, f_w1, f_b1, f_freq, f_w2, f_b2, f_w3, bias):
    l = p.shape[1]
    p = _short_conv(p, conv_w, on_grid)
    v, x1, x2 = jnp.split(p, 3, axis=-1)
    filt = _hyena_filters(l, f_w1, f_b1, f_freq, f_w2, f_b2, f_w3)
    z = v
    for n, gate in enumerate((x1, x2)):
        conv = _two_sided_fftconv(z, filt[:, n, 0], filt[:, n, 1])
        z = gate * (conv + bias[n] * z)
    return z


def _shortconv_stream(p, on_grid, conv_w):
    bg, cg, xin = jnp.split(p, 3, axis=-1)
    return bg * _short_conv(cg * xin, conv_w, on_grid)


def _route(h, w_router, router_bias):
    t = h.shape[0]
    scores = jax.nn.sigmoid(jnp.dot(h, w_router, precision=lax.Precision.HIGHEST))
    choice = (scores + router_bias).reshape(t, N_GROUPS, EXPERTS_PER_GROUP)
    group_score = lax.top_k(choice, GROUP_SCORE_K)[0].sum(-1)
    group = jnp.argmax(group_score, axis=-1)
    in_group = jnp.take_along_axis(choice, group[:, None, None], axis=1)[:, 0]
    local = lax.top_k(in_group, TOP_K)[1]
    expert = group[:, None] * EXPERTS_PER_GROUP + local
    weight = jnp.take_along_axis(scores, expert, axis=1)
    weight = weight / jnp.sum(weight, axis=-1, keepdims=True)
    return expert.astype(jnp.int32), weight


def _moe_ffn(x, w_router, router_bias, w_gate, w_up, w_down):
    t, d = x.shape
    expert, weight = _route(x, w_router, router_bias)
    a = t * TOP_K
    e_flat = expert.reshape(-1)
    order = jnp.argsort(e_flat)
    e_sorted = e_flat[order]
    tok_sorted = (order // TOP_K).astype(jnp.int32)
    counts = jnp.zeros((N_EXPERTS,), jnp.int32).at[e_flat].add(1)
    start = jnp.cumsum(counts) - counts
    padded = (counts + MOE_BLOCK - 1) // MOE_BLOCK * MOE_BLOCK
    pend = jnp.cumsum(padded)
    pstart = pend - padded
    dest = pstart[e_sorted] + (jnp.arange(a, dtype=jnp.int32) - start[e_sorted])
    n_blocks = -(-a // MOE_BLOCK) + N_EXPERTS
    n_slots = n_blocks * MOE_BLOCK
    slot_tok = jnp.full((n_slots,), t, jnp.int32).at[dest].set(tok_sorted)
    block_start = jnp.arange(n_blocks, dtype=jnp.int32) * MOE_BLOCK
    block_expert = jnp.minimum(jnp.searchsorted(pend, block_start, side='right'),
                               N_EXPERTS - 1).astype(jnp.int32)
    x_pad = jnp.concatenate([x.astype(BF16), jnp.zeros((1, d), BF16)], axis=0)
    xs = x_pad[slot_tok]
    ys = _expert_ffn(xs, block_expert, jnp.ones((n_slots,), F32), w_gate, w_up, w_down)
    slot_of = jnp.zeros((a,), jnp.int32).at[order].set(dest).reshape(t, TOP_K)
    out = weight[:, 0:1] * ys[slot_of[:, 0]] + weight[:, 1:2] * ys[slot_of[:, 1]]
    return out


def _sc_layer_kernel(x_ref, gain_ref, shift_ref, scale_ref, win_ref, cw_ref, wout_ref, gate_ref, o_ref, *,
                     n_lat_tiles):
    d = D_MODEL
    tm = x_ref.shape[0]
    seg = jnp.where(pl.program_id(0) >= n_lat_tiles, CTX_LEN, GRID_W)
    pos = lax.broadcasted_iota(jnp.int32, (tm, 1), 0) & (seg - 1)
    hb = _modulated(x_ref[...], gain_ref[...], shift_ref[0], scale_ref[0]).astype(BF16)
    u = (jnp.dot(hb, win_ref[:, d:2 * d], preferred_element_type=F32)
         * jnp.dot(hb, win_ref[:, 2 * d:], preferred_element_type=F32))
    prev = jnp.where(pos != 0, pltpu.roll(u, 1, axis=0), 0.0)
    nxt = jnp.where(pos != seg - 1, pltpu.roll(u, tm - 1, axis=0), 0.0)
    cw = cw_ref[...]
    y = jnp.dot(hb, win_ref[:, :d], preferred_element_type=F32) * (
        cw[0:1] * prev + cw[1:2] * u + cw[2:3] * nxt)
    o_ref[...] = x_ref[...] + gate_ref[0] * jnp.dot(y.astype(BF16), wout_ref[...],
                                                    preferred_element_type=F32)


def _shortconv_layer(xs, gain, shift3, scale3, w_in, conv_w, w_out, gate3, n_lat_tiles, blocks_per_batch):
    t, d = xs.shape
    tm = HY_TOK_TILE
    per_tile = tm // DN_BLOCK
    row = lambda i: (i, 0)
    fixed = lambda i: (0, 0)
    grp = lambda i: (_group_of_block(i * per_tile, n_lat_tiles * per_tile, blocks_per_batch), 0, 0)
    return pl.pallas_call(
        functools.partial(_sc_layer_kernel, n_lat_tiles=n_lat_tiles),
        out_shape=jax.ShapeDtypeStruct((t, d), F32),
        grid=(t // tm,),
        in_specs=[pl.BlockSpec((tm, d), row),
                  pl.BlockSpec((1, d), fixed),
                  pl.BlockSpec((1, 1, d), grp),
                  pl.BlockSpec((1, 1, d), grp),
                  pl.BlockSpec((d, 3 * d), fixed),
                  pl.BlockSpec((3, d), fixed),
                  pl.BlockSpec((d, d), fixed),
                  pl.BlockSpec((1, 1, d), grp)],
        out_specs=pl.BlockSpec((tm, d), row),
        compiler_params=pltpu.CompilerParams(
            dimension_semantics=("arbitrary",), vmem_limit_bytes=VMEM_LIMIT_BYTES),
        name="shortconv_layer",
    )(xs, gain.reshape(1, d), shift3, scale3, w_in.astype(BF16), conv_w, w_out.astype(BF16), gate3)


def kernel(x, c, ctx, c_ctx, ada_w, ada_b, norm_mix, norm_ffn, norm_final, dn_w_in, dn_conv, dn_a_log,
           dn_dt_bias, dn_out_norm, dn_w_out, hy_w_in, hy_conv, hy_f_w1, hy_f_b1, hy_f_freq, hy_f_w2,
           hy_f_b2, hy_f_w3, hy_bias, hy_w_out, sc_w_in, sc_conv, sc_w_out, w_router, router_bias,
           moe_w_gate, moe_w_up, moe_w_down):
    d = D_MODEL
    bsz, seq, _ = x.shape
    n_ctx = bsz * CTX_LEN
    n_lat = bsz * seq
    silu_c = jax.nn.silu(c)
    silu_cc = jax.nn.silu(c_ctx)
    hp = lax.Precision.HIGHEST

    xs = jnp.concatenate([x.reshape(n_lat, d), ctx.reshape(n_ctx, d)], axis=0)
    n_lat_blocks, blocks_per_batch = n_lat // DN_BLOCK, seq // DN_BLOCK

    for i in range(DEPTH):
        kind, j = i % N_MIXERS, i // N_MIXERS
        ml = jnp.split(jnp.dot(silu_c, ada_w[i], precision=hp) + ada_b[i], N_MOD, axis=-1)
        mc = jnp.split(jnp.dot(silu_cc, ada_w[i], precision=hp) + ada_b[i], N_MOD, axis=-1)
        mod = [jnp.concatenate([mc[m][None], ml[m]], axis=0)[:, None, :] for m in range(N_MOD)]
        if kind == 0:
            xs = _deltanet_layer(xs, norm_mix[i], mod[0], mod[1], dn_w_in[j], dn_conv[j], dn_a_log[j],
                                 dn_dt_bias[j], dn_out_norm[j], dn_w_out[j], mod[2], n_lat_blocks,
                                 blocks_per_batch)
        elif kind == 1:
            xs = _hyena_layer(xs, norm_mix[i], mod[0], mod[1], hy_w_in[j], hy_conv[j], hy_f_w1[j], hy_f_b1[j],
                              hy_f_freq[j], hy_f_w2[j], hy_f_b2[j], hy_f_w3[j], hy_bias[j], hy_w_out[j],
                              mod[2], bsz, seq)
        else:
            xs = _shortconv_layer(xs, norm_mix[i], mod[0], mod[1], sc_w_in[j], sc_conv[j], sc_w_out[j],
                                  mod[2], n_lat // HY_TOK_TILE, blocks_per_batch)
        xs = _moe_layer(xs, norm_ffn[i], mod[3], mod[4], mod[5], w_router, router_bias,
                        moe_w_gate, moe_w_up, moe_w_down, i, n_lat // MOE_TILE, blocks_per_batch,
                        norm_final if i == DEPTH - 1 else None)
    return xs.reshape(bsz, seq, d)
```

```python
import functools
import math

import numpy as np
import jax
import jax.numpy as jnp
from jax import lax
from jax.experimental import pallas as pl
from jax.experimental.pallas import tpu as pltpu

D_MODEL = 1024
DEPTH = 4
CTX_LEN = 256
GRID_W = 64
N_MIXERS = 3
EPS = 1e-6
N_MOD = 6

DN_HEADS = 8
DN_DK = D_MODEL // DN_HEADS
DN_DV = D_MODEL // DN_HEADS
DN_CHUNK = 64

HY_ORDER = 2
HY_BANDS = 16
HY_TARGET = 1e-2
HY_FAST = 0.3
HY_SLOW = 1.5

N_EXPERTS = 32
N_GROUPS = 8
EXPERTS_PER_GROUP = N_EXPERTS // N_GROUPS
GROUP_SCORE_K = 2
TOP_K = 2
D_EXPERT = 512
MOE_BLOCK = 512

F32 = jnp.float32
BF16 = jnp.bfloat16

ROW_TILE = 512
VMEM_LIMIT_BYTES = 48 * 1024 * 1024


def _mm_kernel(x_ref, w_ref, o_ref):
    o_ref[...] = jnp.dot(x_ref[...].astype(BF16), w_ref[...], preferred_element_type=F32)


def _mm(x, w, tn=None):
    m, k = x.shape
    n = w.shape[1]
    tm = min(ROW_TILE, m)
    tn = n if tn is None else tn
    assert m % tm == 0 and n % tn == 0
    return pl.pallas_call(
        _mm_kernel,
        out_shape=jax.ShapeDtypeStruct((m, n), F32),
        grid=(m // tm, n // tn),
        in_specs=[pl.BlockSpec((tm, k), lambda i, j: (i, 0)),
                  pl.BlockSpec((k, tn), lambda i, j: (0, j))],
        out_specs=pl.BlockSpec((tm, tn), lambda i, j: (i, j)),
        compiler_params=pltpu.CompilerParams(
            dimension_semantics=("arbitrary", "arbitrary"), vmem_limit_bytes=VMEM_LIMIT_BYTES),
        name="dense_mm",
    )(x, w.astype(BF16))


def _expert_kernel(be_ref, x_ref, wg_ref, wu_ref, wd_ref, sw_ref, o_ref):
    del be_ref
    xb = x_ref[...]
    g = jnp.dot(xb, wg_ref[0], preferred_element_type=F32)
    u = jnp.dot(xb, wu_ref[0], preferred_element_type=F32)
    hid = (g * jax.nn.sigmoid(g)) * u
    y = jnp.dot(hid.astype(BF16), wd_ref[0], preferred_element_type=F32)
    o_ref[...] = y * sw_ref[...]


def _expert_ffn(xs, block_expert, slot_w, w_gate, w_up, w_down):
    n_slots, d = xs.shape
    n_blocks = n_slots // MOE_BLOCK
    grid_spec = pltpu.PrefetchScalarGridSpec(
        num_scalar_prefetch=1,
        grid=(n_blocks,),
        in_specs=[
            pl.BlockSpec((MOE_BLOCK, d), lambda i, be: (i, 0)),
            pl.BlockSpec((1, d, D_EXPERT), lambda i, be: (be[i], 0, 0)),
            pl.BlockSpec((1, d, D_EXPERT), lambda i, be: (be[i], 0, 0)),
            pl.BlockSpec((1, D_EXPERT, d), lambda i, be: (be[i], 0, 0)),
            pl.BlockSpec((MOE_BLOCK, 1), lambda i, be: (i, 0)),
        ],
        out_specs=pl.BlockSpec((MOE_BLOCK, d), lambda i, be: (i, 0)),
    )
    return pl.pallas_call(
        _expert_kernel,
        out_shape=jax.ShapeDtypeStruct((n_slots, d), F32),
        grid_spec=grid_spec,
        compiler_params=pltpu.CompilerParams(
            dimension_semantics=("arbitrary",), vmem_limit_bytes=VMEM_LIMIT_BYTES),
        name="expert_ffn",
    )(block_expert, xs, w_gate.astype(BF16), w_up.astype(BF16), w_down.astype(BF16),
      slot_w.reshape(n_slots, 1))


DN_BLOCK = CTX_LEN
DN_HB = DN_HEADS
N_CHUNKS_PER_BLOCK = DN_BLOCK // DN_CHUNK


def _group_of_block(i, n_lat_blocks, blocks_per_batch):
    return jnp.where(i >= n_lat_blocks, 0, 1 + i // blocks_per_batch)


def _modulated(x, gain, shift, scale):
    y = x * lax.rsqrt(jnp.mean(x * x, axis=-1, keepdims=True) + EPS) * gain
    return y * (1 + scale) + shift


def _dn_inproj_kernel(x_ref, gain_ref, shift_ref, scale_ref, w_ref, wab_ref, cw_ref, alog_ref, dtb_ref,
                      q_ref, k_ref, v_ref, z_ref, gate_ref, *, n_lat_blocks):
    i = pl.program_id(0)
    nrow = DN_BLOCK
    d = D_MODEL
    pair = 2 * DN_DK
    seg = jnp.where(i >= n_lat_blocks, CTX_LEN, GRID_W)
    r = lax.broadcasted_iota(jnp.int32, (nrow, 1), 0)
    pos = r & (seg - 1)
    not_first = pos != 0
    not_last = pos != seg - 1
    h = _modulated(x_ref[...], gain_ref[...], shift_ref[0], scale_ref[0])
    hb = h.astype(BF16)
    outs = (q_ref, k_ref, v_ref)
    for part in range(3):
        for hp in range(d // pair):
            col = part * d + hp * pair
            x = jnp.dot(hb, w_ref[:, col:col + pair], preferred_element_type=F32)
            cw = cw_ref[:, col:col + pair]
            xp = jnp.where(not_first, pltpu.roll(x, 1, axis=0), 0.0)
            xn = jnp.where(not_last, pltpu.roll(x, nrow - 1, axis=0), 0.0)
            y = cw[0:1] * xp + cw[1:2] * x + cw[2:3] * xn
            y = y * jax.nn.sigmoid(y)
            for hh in range(2):
                yh = y[:, hh * DN_DK:(hh + 1) * DN_DK]
                if part < 2:
                    yh = yh * lax.rsqrt(jnp.sum(yh * yh, axis=-1, keepdims=True) + EPS)
                if part == 0:
                    yh = yh * DN_DK ** -0.5
                outs[part][:, hp * pair + hh * DN_DK:hp * pair + (hh + 1) * DN_DK] = yh
    for j in range(d // pair):
        z_ref[:, j * pair:(j + 1) * pair] = jnp.dot(hb, w_ref[:, 3 * d + j * pair:3 * d + (j + 1) * pair],
                                                    preferred_element_type=F32)

    ab = jnp.dot(h, wab_ref[...], precision=lax.Precision.HIGHEST, preferred_element_type=F32)
    nd = 2 * DN_HEADS
    a = ab[:, :nd] + dtb_ref[...]
    softplus = jnp.maximum(a, 0.0) + jnp.log(1.0 + jnp.exp(-jnp.abs(a)))
    g = -jnp.exp(alog_ref[...]) * softplus
    beta = jax.nn.sigmoid(ab[:, nd:])
    cpos = r & (DN_CHUNK - 1)
    gp, gs = g, g
    sh = 1
    while sh < DN_CHUNK:
        gp = gp + jnp.where(cpos >= sh, pltpu.roll(gp, sh, axis=0), 0.0)
        gs = gs + jnp.where(cpos < DN_CHUNK - sh, pltpu.roll(gs, nrow - sh, axis=0), 0.0)
        sh *= 2
    colid = lax.broadcasted_iota(jnp.int32, (1, nd), 1)
    gate_ref[:, :nd] = jnp.where(colid < DN_HEADS, gp, gs)
    gate_ref[:, nd:] = beta


def _dn_inproj(xs, gain, shift3, scale3, w_in, conv_w, a_log, dt_bias, n_lat_blocks, blocks_per_batch):
    t, d = xs.shape
    nd = 2 * DN_HEADS
    row = lambda i: (i, 0)
    fixed = lambda i: (0, 0)
    grp = lambda i: (_group_of_block(i, n_lat_blocks, blocks_per_batch), 0, 0)
    return pl.pallas_call(
        functools.partial(_dn_inproj_kernel, n_lat_blocks=n_lat_blocks),
        out_shape=[jax.ShapeDtypeStruct((t, d), F32)] * 4 + [jax.ShapeDtypeStruct((t, 2 * nd), F32)],
        grid=(t // DN_BLOCK,),
        in_specs=[pl.BlockSpec((DN_BLOCK, d), row),
                  pl.BlockSpec((1, d), fixed),
                  pl.BlockSpec((1, 1, d), grp),
                  pl.BlockSpec((1, 1, d), grp),
                  pl.BlockSpec((d, 4 * d), fixed),
                  pl.BlockSpec((d, 2 * nd), fixed),
                  pl.BlockSpec((3, 3 * d), fixed),
                  pl.BlockSpec((1, nd), fixed),
                  pl.BlockSpec((1, nd), fixed)],
        out_specs=[pl.BlockSpec((DN_BLOCK, d), row)] * 4 + [pl.BlockSpec((DN_BLOCK, 2 * nd), row)],
        compiler_params=pltpu.CompilerParams(
            dimension_semantics=("arbitrary",), vmem_limit_bytes=VMEM_LIMIT_BYTES),
        name="dn_inproj",
    )(xs, gain.reshape(1, d), shift3, scale3, w_in[:, :4 * d].astype(BF16), w_in[:, 4 * d:], conv_w,
      a_log.reshape(1, nd), dt_bias.reshape(1, nd))


def _dotb(a, b):
    return jnp.dot(a.astype(BF16), b.astype(BF16), preferred_element_type=F32)


def _dotb_nt(a, b):
    return lax.dot_general(a.astype(BF16), b.astype(BF16), (((1,), (1,)), ((), ())),
                           preferred_element_type=F32)


def _dotb_tn(a, b):
    return lax.dot_general(a.astype(BF16), b.astype(BF16), (((0,), (0,)), ((), ())),
                           preferred_element_type=F32)


def _unit_tri_inverses(mats, ii, jj):
    eye = (ii == jj).astype(F32)
    diag8 = (ii >> 3) == (jj >> 3)
    n = [-jnp.where(diag8, a, 0.0) for a in mats]
    n2 = [_dotb(x, x) for x in n]
    m = [eye + x for x in n]
    m = [x + _dotb(x, y) for x, y in zip(m, n2)]
    n4 = [_dotb(x, x) for x in n2]
    m = [x + _dotb(x, y) for x, y in zip(m, n4)]
    sh = 3
    while (1 << sh) < DN_CHUNK:
        off = ((ii >> (sh + 1)) == (jj >> (sh + 1))) & ((ii >> sh) != (jj >> sh))
        cm = [_dotb(jnp.where(off, a, 0.0), x) for a, x in zip(mats, m)]
        m = [x - _dotb(x, y) for x, y in zip(m, cm)]
        sh += 1
    return m


def _dn_scan_kernel(qf_ref, kf_ref, vf_ref, gcf_ref, grf_ref, qb_ref, kb_ref, vb_ref, gcb_ref, grb_ref,
                    of_ref, ob_ref, s_ref):
    @pl.when(pl.program_id(2) == 0)
    def _():
        s_ref[...] = jnp.zeros_like(s_ref)

    c = DN_CHUNK
    ncb = N_CHUNKS_PER_BLOCK
    ii = lax.broadcasted_iota(jnp.int32, (c, c), 0)
    jj = lax.broadcasted_iota(jnp.int32, (c, c), 1)
    incl = (ii >= jj, ii <= jj)
    strict = (ii > jj, ii < jj)
    dirs = ((qf_ref, kf_ref, vf_ref, gcf_ref, grf_ref, of_ref),
            (qb_ref, kb_ref, vb_ref, gcb_ref, grb_ref, ob_ref))
    items = [(d, hh, ci) for d in range(2) for hh in range(DN_HB) for ci in range(ncb)]

    def rows(ci):
        return slice(ci * c, (ci + 1) * c)

    def cols(hh):
        return slice(hh * DN_DK, (hh + 1) * DN_DK)

    q = [dirs[d][0][rows(ci), cols(hh)] for d, hh, ci in items]
    k = [dirs[d][1][rows(ci), cols(hh)] for d, hh, ci in items]
    v = [dirs[d][2][rows(ci), cols(hh)] for d, hh, ci in items]
    gc = [dirs[d][3][hh, rows(ci), d:d + 1] for d, hh, ci in items]
    gr = [dirs[d][4][hh, d:d + 1, rows(ci)] for d, hh, ci in items]
    beta = [dirs[d][3][hh, rows(ci), 2 + d:3 + d] for d, hh, ci in items]

    decay = [jnp.where(incl[it[0]], jnp.exp(jnp.where(incl[it[0]], x - y, 0.0)), 0.0)
             for it, x, y in zip(items, gc, gr)]
    kb = [x * y for x, y in zip(k, beta)]
    a = [_dotb_nt(x, y) * jnp.where(strict[it[0]], z, 0.0) for it, x, y, z in zip(items, kb, k, decay)]
    attn = [_dotb_nt(x, y) * z for x, y, z in zip(q, k, decay)]
    tinv = _unit_tri_inverses(a, ii, jj)
    eg = [jnp.exp(x) for x in gc]
    uw = [_dotb(t, jnp.concatenate([x * y, z * e], axis=-1))
          for t, x, y, z, e in zip(tinv, v, beta, kb, eg)]
    g_last = [x[0:1] if it[0] else x[c - 1:c] for it, x in zip(items, gc)]
    wq = [jnp.concatenate([x[:, DN_DV:], y * e], axis=0) for x, y, e in zip(uw, q, eg)]
    k_dec = [x * jnp.exp(y - z) for x, y, z in zip(k, g_last, gc)]
    s_dec = [jnp.exp(x) for x in g_last]

    chains = [(d, hh) for d in range(2) for hh in range(DN_HB)]
    state = [s_ref[d, hh] for d, hh in chains]
    for step in range(ncb):
        cur = [items.index((d, hh, ncb - 1 - step if d else step)) for d, hh in chains]
        ws = [_dotb(wq[n], s) for n, s in zip(cur, state)]
        v_new = [uw[n][:, :DN_DV] - x[:c] for n, x in zip(cur, ws)]
        o = [x[c:] + _dotb(attn[n], y) for n, x, y in zip(cur, ws, v_new)]
        state = [s * s_dec[n] + _dotb_tn(k_dec[n], y) for n, s, y in zip(cur, state, v_new)]
        for n, x in zip(cur, o):
            d, hh, ci = items[n]
            dirs[d][5][rows(ci), cols(hh)] = x
    for (d, hh), s in zip(chains, state):
        s_ref[d, hh] = s


def _dn_scan(q, k, v, gates, n_lat_blocks, blocks_per_batch):
    t, d = q.shape
    bsz = n_lat_blocks // blocks_per_batch
    g4 = gates.reshape(t, 4, DN_HEADS)
    gcol = jnp.transpose(g4, (2, 0, 1))
    grow = jnp.transpose(g4, (2, 1, 0))

    def blk_f(b, s):
        return jnp.where(s == 0, n_lat_blocks + b, b * blocks_per_batch + s - 1)

    def blk_b(b, s):
        return jnp.where(s == 0, n_lat_blocks + b, b * blocks_per_batch + blocks_per_batch - s)

    hw = DN_HB * DN_DK

    def specs(blk):
        return [pl.BlockSpec((DN_BLOCK, hw), lambda b, hg, s: (blk(b, s), hg))] * 3 + [
            pl.BlockSpec((DN_HB, DN_BLOCK, 4), lambda b, hg, s: (hg, blk(b, s), 0)),
            pl.BlockSpec((DN_HB, 4, DN_BLOCK), lambda b, hg, s: (hg, 0, blk(b, s)))]

    return pl.pallas_call(
        _dn_scan_kernel,
        out_shape=[jax.ShapeDtypeStruct((t, d), F32)] * 2,
        grid=(bsz, DN_HEADS // DN_HB, 1 + blocks_per_batch),
        in_specs=specs(blk_f) + specs(blk_b),
        out_specs=[pl.BlockSpec((DN_BLOCK, hw), lambda b, hg, s: (blk_f(b, s), hg)),
                   pl.BlockSpec((DN_BLOCK, hw), lambda b, hg, s: (blk_b(b, s), hg))],
        scratch_shapes=[pltpu.VMEM((2, DN_HB, DN_DK, DN_DV), F32)],
        compiler_params=pltpu.CompilerParams(
            dimension_semantics=("arbitrary", "arbitrary", "arbitrary"),
            vmem_limit_bytes=VMEM_LIMIT_BYTES),
        name="dn_scan",
    )(q, k, v, gcol, grow, q, k, v, gcol, grow)


def _dn_out_kernel(of_ref, ob_ref, z_ref, on_ref, w_ref, x_ref, gate_ref, o_ref):
    z = z_ref[...]
    zs = z * jax.nn.sigmoid(z)
    parts = []
    for h in range(DN_HEADS):
        cols = slice(h * DN_DV, (h + 1) * DN_DV)
        o = of_ref[:, cols] + ob_ref[:, cols]
        o = o * lax.rsqrt(jnp.mean(o * o, axis=-1, keepdims=True) + EPS)
        parts.append(o)
    y = jnp.concatenate(parts, axis=-1) * on_ref[...] * zs
    o_ref[...] = x_ref[...] + gate_ref[0] * jnp.dot(y.astype(BF16), w_ref[...],
                                                    preferred_element_type=F32)


def _dn_out(o_f, o_b, z, out_norm, w_out, xs, gate3, n_lat_blocks, blocks_per_batch):
    t, d = xs.shape
    row = lambda i: (i, 0)
    fixed = lambda i: (0, 0)
    grp = lambda i: (_group_of_block(i, n_lat_blocks, blocks_per_batch), 0, 0)
    return pl.pallas_call(
        _dn_out_kernel,
        out_shape=jax.ShapeDtypeStruct((t, d), F32),
        grid=(t // DN_BLOCK,),
        in_specs=[pl.BlockSpec((DN_BLOCK, d), row),
                  pl.BlockSpec((DN_BLOCK, d), row),
                  pl.BlockSpec((DN_BLOCK, d), row),
                  pl.BlockSpec((1, d), fixed),
                  pl.BlockSpec((d, d), fixed),
                  pl.BlockSpec((DN_BLOCK, d), row),
                  pl.BlockSpec((1, 1, d), grp)],
        out_specs=pl.BlockSpec((DN_BLOCK, d), row),
        compiler_params=pltpu.CompilerParams(
            dimension_semantics=("arbitrary",), vmem_limit_bytes=VMEM_LIMIT_BYTES),
        name="dn_out",
    )(o_f, o_b, z, jnp.tile(out_norm, DN_HEADS).reshape(1, d), w_out.astype(BF16), xs, gate3)


def _deltanet_layer(xs, gain, shift3, scale3, w_in, conv_w, a_log, dt_bias, out_norm, w_out, gate3,
                    n_lat_blocks, blocks_per_batch):
    q, k, v, z, gates = _dn_inproj(xs, gain, shift3, scale3, w_in, conv_w, a_log, dt_bias, n_lat_blocks,
                                   blocks_per_batch)
    o_f, o_b = _dn_scan(q, k, v, gates, n_lat_blocks, blocks_per_batch)
    return _dn_out(o_f, o_b, z, out_norm, w_out, xs, gate3, n_lat_blocks, blocks_per_batch)


HY_N2 = 256
HY_CB = 16
HY_TOK_TILE = 512
HY_FILT_TILE = 512


def _dft_constants(nr):
    n1, n2 = 2 * nr, HY_N2
    n = n1 * n2
    nk = -(-(nr + 1) // 8) * 8
    keep = np.arange(nk) <= nr
    k1 = np.where(keep, np.arange(nk), 0)
    f1 = np.exp(-2j * np.pi * np.outer(k1, np.arange(nr)) / n1) * keep[:, None]
    twice = np.where((k1 > 0) & (k1 < nr), 2.0, 1.0) * keep
    lhs_fwd = np.concatenate([f1.real, f1.imag], axis=0)
    lhs_inv = np.concatenate([f1.real.T * twice, f1.imag.T * twice], axis=1) / n
    tw = np.exp(-2j * np.pi * np.outer(k1, np.arange(n2)) / n)
    a2 = np.arange(n2)
    f2 = np.exp(-2j * np.pi * np.outer(a2, a2) / n2)
    w_fwd = np.block([[f2.real, f2.imag], [-f2.imag, f2.real]])
    w_inv = np.block([[f2.real, -f2.imag], [f2.imag, f2.real]])
    return (jnp.asarray(lhs_fwd, BF16), jnp.asarray(lhs_inv, BF16), jnp.asarray(tw.real, F32),
            jnp.asarray(tw.imag, F32), jnp.asarray(w_fwd, BF16), jnp.asarray(w_inv, BF16))


def _hy_dft(x3, lhs_fwd, twr, twi, w_fwd):
    n1 = twr.shape[0]
    a = [jnp.dot(lhs_fwd, x3[c].astype(BF16), preferred_element_type=F32) for c in range(x3.shape[0])]
    br = jnp.concatenate([t[:n1] * twr - t[n1:] * twi for t in a], axis=0)
    bi = jnp.concatenate([t[:n1] * twi + t[n1:] * twr for t in a], axis=0)
    b = jnp.concatenate([br, bi], axis=1)
    return jnp.dot(b.astype(BF16), w_fwd, preferred_element_type=F32)


def _hy_idft(p, cb, lhs_inv, twr, twi, w_inv):
    n1, n2 = twr.shape
    c = jnp.dot(p.astype(BF16), w_inv, preferred_element_type=F32)
    out = []
    for ch in range(cb):
        cr = c[ch * n1:(ch + 1) * n1, :n2]
        ci = c[ch * n1:(ch + 1) * n1, n2:]
        d = jnp.concatenate([cr * twr + ci * twi, ci * twr - cr * twi], axis=0)
        out.append(jnp.dot(lhs_inv, d.astype(BF16), preferred_element_type=F32))
    return out


def _hy_spectrum_kernel(hf_ref, hb_ref, lf_ref, twr_ref, twi_ref, wf_ref, o_ref):
    cb, nr, n2 = hf_ref.shape
    first = ((lax.broadcasted_iota(jnp.int32, (nr, n2), 0) == 0)
             & (lax.broadcasted_iota(jnp.int32, (nr, n2), 1) == 0))
    hb = jnp.where(first, 0.0, hb_ref[...])
    consts = (lf_ref[...], twr_ref[...], twi_ref[...], wf_ref[...])
    xf = _hy_dft(hf_ref[...], *consts)
    xb = _hy_dft(hb, *consts)
    o_ref[...] = jnp.concatenate([xf[:, :n2] + xb[:, :n2], xf[:, n2:] - xb[:, n2:]],
                                 axis=1).reshape(o_ref.shape)


def _hy_spectrum(filt, consts):
    d = D_MODEL
    l = filt.shape[1]
    nr = l // HY_N2
    lhs_fwd, _, twr, twi, w_fwd, _ = consts
    n1 = twr.shape[0]
    cpo = d // HY_CB
    fixed2 = lambda o, c: (0, 0)
    return pl.pallas_call(
        _hy_spectrum_kernel,
        out_shape=jax.ShapeDtypeStruct((HY_ORDER * d, n1, 2 * HY_N2), F32),
        grid=(HY_ORDER, cpo),
        in_specs=[pl.BlockSpec((HY_CB, nr, HY_N2), lambda o, c: (2 * o * cpo + c, 0, 0)),
                  pl.BlockSpec((HY_CB, nr, HY_N2), lambda o, c: ((2 * o + 1) * cpo + c, 0, 0)),
                  pl.BlockSpec(lhs_fwd.shape, fixed2),
                  pl.BlockSpec(twr.shape, fixed2),
                  pl.BlockSpec(twi.shape, fixed2),
                  pl.BlockSpec(w_fwd.shape, fixed2)],
        out_specs=pl.BlockSpec((HY_CB, n1, 2 * HY_N2), lambda o, c: (o * cpo + c, 0, 0)),
        compiler_params=pltpu.CompilerParams(
            dimension_semantics=("arbitrary", "arbitrary"), vmem_limit_bytes=VMEM_LIMIT_BYTES),
        name="hy_spectrum",
    )(filt.reshape(-1, nr, HY_N2), filt.reshape(-1, nr, HY_N2), lhs_fwd, twr, twi, w_fwd)


def _hy_conv_kernel(z_ref, g_ref, k_ref, bias_ref, lf_ref, li_ref, twr_ref, twi_ref, wf_ref, wi_ref,
                    o_ref):
    cb, nr, n2 = z_ref.shape
    twr, twi = twr_ref[...], twi_ref[...]
    z = z_ref[...]
    x = _hy_dft(z, lf_ref[...], twr, twi, wf_ref[...])
    kk = k_ref[...].reshape(x.shape)
    xr, xi, kr, ki = x[:, :n2], x[:, n2:], kk[:, :n2], kk[:, n2:]
    p = jnp.concatenate([xr * kr - xi * ki, xr * ki + xi * kr], axis=1)
    conv = _hy_idft(p, cb, li_ref[...], twr, twi, wi_ref[...])
    for c in range(cb):
        o_ref[c] = g_ref[c] * (conv[c] + bias_ref[c] * z[c])


def _hy_conv(z, z_part, gate, gate_part, khat, order, bias, consts, bsz):
    d = D_MODEL
    l = z.shape[1] // bsz
    nr = l // HY_N2
    cpo = d // HY_CB
    lhs_fwd, lhs_inv, twr, twi, w_fwd, w_inv = consts
    n1 = twr.shape[0]
    fixed2 = lambda c, b: (0, 0)
    out = pl.pallas_call(
        _hy_conv_kernel,
        out_shape=jax.ShapeDtypeStruct((d, bsz * nr, HY_N2), F32),
        grid=(cpo, bsz),
        in_specs=[pl.BlockSpec((HY_CB, nr, HY_N2), lambda c, b: (z_part * cpo + c, b, 0)),
                  pl.BlockSpec((HY_CB, nr, HY_N2), lambda c, b: (gate_part * cpo + c, b, 0)),
                  pl.BlockSpec((HY_CB, n1, 2 * HY_N2), lambda c, b: (order * cpo + c, 0, 0)),
                  pl.BlockSpec((HY_CB, 1, 1), lambda c, b: (c, 0, 0)),
                  pl.BlockSpec(lhs_fwd.shape, fixed2),
                  pl.BlockSpec(lhs_inv.shape, fixed2),
                  pl.BlockSpec(twr.shape, fixed2),
                  pl.BlockSpec(twi.shape, fixed2),
                  pl.BlockSpec(w_fwd.shape, fixed2),
                  pl.BlockSpec(w_inv.shape, fixed2)],
        out_specs=pl.BlockSpec((HY_CB, nr, HY_N2), lambda c, b: (c, b, 0)),
        compiler_params=pltpu.CompilerParams(
            dimension_semantics=("arbitrary", "arbitrary"), vmem_limit_bytes=VMEM_LIMIT_BYTES),
        name="hy_conv",
    )(z.reshape(-1, bsz * nr, HY_N2), gate.reshape(-1, bsz * nr, HY_N2), khat,
      bias.reshape(d, 1, 1), lhs_fwd, lhs_inv, twr, twi, w_fwd, w_inv)
    return out.reshape(d, bsz * l)


def _hy_ctx_kernel(z_ref, g_ref, hf_ref, hb_ref, bias_ref, wf_ref, wi_ref, o_ref, *, bsz):
    l = hf_ref.shape[1]
    wf, wi = wf_ref[...], wi_ref[...]
    hb = jnp.where(lax.broadcasted_iota(jnp.int32, (1, l), 1) == 0, 0.0, hb_ref[...])
    kf = jnp.dot(hf_ref[...].astype(BF16), wf, preferred_element_type=F32)
    kb = jnp.dot(hb.astype(BF16), wf, preferred_element_type=F32)
    n = 2 * l
    kr, ki = kf[:, :n] + kb[:, :n], kf[:, n:] - kb[:, n:]
    bias = bias_ref[...]
    for b in range(bsz):
        z = z_ref[:, b * l:(b + 1) * l]
        x = jnp.dot(z.astype(BF16), wf, preferred_element_type=F32)
        xr, xi = x[:, :n], x[:, n:]
        p = jnp.concatenate([xr * kr - xi * ki, xr * ki + xi * kr], axis=1)
        conv = jnp.dot(p.astype(BF16), wi, preferred_element_type=F32)
        o_ref[:, b * l:(b + 1) * l] = g_ref[:, b * l:(b + 1) * l] * (conv + bias * z)


def _hy_ctx(z, z_part, gate, gate_part, filt, order, bias, bsz):
    d = D_MODEL
    l = filt.shape[1]
    n = 2 * l
    ang = 2 * np.pi * np.outer(np.arange(l), np.arange(n)) / n
    w_fwd = jnp.asarray(np.concatenate([np.cos(ang), -np.sin(ang)], axis=1), BF16)
    w_inv = jnp.asarray(np.concatenate([np.cos(ang.T), -np.sin(ang.T)], axis=0) / n, BF16)
    cb = 256
    nblk = d // cb
    fixed = lambda c: (0, 0)
    return pl.pallas_call(
        functools.partial(_hy_ctx_kernel, bsz=bsz),
        out_shape=jax.ShapeDtypeStruct((d, bsz * l), F32),
        grid=(nblk,),
        in_specs=[pl.BlockSpec((cb, bsz * l), lambda c: (z_part * nblk + c, 0)),
                  pl.BlockSpec((cb, bsz * l), lambda c: (gate_part * nblk + c, 0)),
                  pl.BlockSpec((cb, l), lambda c: (2 * order * nblk + c, 0)),
                  pl.BlockSpec((cb, l), lambda c: ((2 * order + 1) * nblk + c, 0)),
                  pl.BlockSpec((cb, 1), lambda c: (c, 0)),
                  pl.BlockSpec(w_fwd.shape, fixed),
                  pl.BlockSpec(w_inv.shape, fixed)],
        out_specs=pl.BlockSpec((cb, bsz * l), lambda c: (c, 0)),
        compiler_params=pltpu.CompilerParams(
            dimension_semantics=("arbitrary",), vmem_limit_bytes=VMEM_LIMIT_BYTES),
        name="hy_ctx_conv",
    )(z, gate, filt, filt, bias.reshape(d, 1), w_fwd, w_inv)


def _hy_filter_kernel(band_ref, w1t_ref, w1c_ref, w1s_ref, b1_ref, fr_ref, w2_ref, b2_ref, w3_ref,
                      delta_ref, o_ref, *, l):
    tl = o_ref.shape[1]
    d = D_MODEL
    hp = lax.Precision.HIGHEST
    pos = (lax.broadcasted_iota(jnp.int32, (1, tl), 1) + pl.program_id(0) * tl).astype(F32)
    t = pos / max(l - 1, 1)
    ang = ((2 * math.pi / l) * pos) * band_ref[...]
    fr = fr_ref[...]
    pre = (w1t_ref[...] * t + jnp.dot(w1c_ref[...], jnp.cos(ang), precision=hp)
           + jnp.dot(w1s_ref[...], -jnp.sin(ang), precision=hp) + b1_ref[...])
    hdn = jnp.sin(fr * pre)
    hdn = jnp.sin(fr * (jnp.dot(w2_ref[...], hdn, precision=hp) + b2_ref[...]))
    window = jnp.exp(-t * delta_ref[...])
    for part in range(2 * HY_ORDER):
        rows = slice(part * d, (part + 1) * d)
        o_ref[rows, :] = jnp.dot(w3_ref[rows, :], hdn, precision=hp) * window


def _hy_filter(l, w1, b1, freq, w2, b2, w3):
    d = D_MODEL
    nb = HY_BANDS
    tl = min(HY_FILT_TILE, l)
    col = lambda v: v.reshape(-1, 1)
    bands = jnp.linspace(1e-4, nb - 1, nb, dtype=F32)
    deltas = jnp.abs(jnp.linspace(math.log(HY_TARGET) / HY_SLOW, math.log(HY_TARGET) / HY_FAST, d, dtype=F32))
    w1t = w1.T
    args = (col(bands), w1t[:, 0:1], w1t[:, 1:1 + nb], w1t[:, 1 + nb:], col(b1), col(freq), w2.T, col(b2),
            w3.T, col(deltas))
    return pl.pallas_call(
        functools.partial(_hy_filter_kernel, l=l),
        out_shape=jax.ShapeDtypeStruct((2 * HY_ORDER * d, l), F32),
        grid=(l // tl,),
        in_specs=[pl.BlockSpec(a.shape, lambda j: (0, 0)) for a in args],
        out_specs=pl.BlockSpec((2 * HY_ORDER * d, tl), lambda j: (0, j)),
        compiler_params=pltpu.CompilerParams(
            dimension_semantics=("arbitrary",), vmem_limit_bytes=VMEM_LIMIT_BYTES),
        name="hy_filter",
    )(*args)


def _hy_inproj_kernel(x_ref, gain_ref, shift_ref, scale_ref, wt_ref, cw_ref, o_ref, *, seg):
    nch = wt_ref.shape[0]
    tm = x_ref.shape[0]
    hb = _modulated(x_ref[...], gain_ref[...], shift_ref[0], scale_ref[0]).astype(BF16)
    pos = lax.broadcasted_iota(jnp.int32, (1, tm), 1) & (seg - 1)
    not_first = pos != 0
    not_last = pos != seg - 1
    sub = 512
    for j in range(nch // sub):
        rows = slice(j * sub, (j + 1) * sub)
        p = _dotb_nt(wt_ref[rows, :], hb)
        cw = cw_ref[rows, :]
        prev = jnp.where(not_first, pltpu.roll(p, 1, axis=1), 0.0)
        nxt = jnp.where(not_last, pltpu.roll(p, tm - 1, axis=1), 0.0)
        o_ref[rows, :] = cw[:, 0:1] * prev + cw[:, 1:2] * p + cw[:, 2:3] * nxt


def _hy_inproj(xs, gain, shift3, scale3, w_in, conv_w, first_tile, n_tiles, seg, n_lat_blocks,
               blocks_per_batch):
    k = xs.shape[1]
    nch = w_in.shape[1]
    tm = HY_TOK_TILE
    per_tile = tm // DN_BLOCK
    grp = lambda i: (_group_of_block((first_tile + i) * per_tile, n_lat_blocks, blocks_per_batch), 0, 0)
    return pl.pallas_call(
        functools.partial(_hy_inproj_kernel, seg=seg),
        out_shape=jax.ShapeDtypeStruct((nch, n_tiles * tm), F32),
        grid=(n_tiles,),
        in_specs=[pl.BlockSpec((tm, k), lambda i: (first_tile + i, 0)),
                  pl.BlockSpec((1, k), lambda i: (0, 0)),
                  pl.BlockSpec((1, 1, k), grp),
                  pl.BlockSpec((1, 1, k), grp),
                  pl.BlockSpec((nch, k), lambda i: (0, 0)),
                  pl.BlockSpec((nch, 3), lambda i: (0, 0))],
        out_specs=pl.BlockSpec((nch, tm), lambda i: (0, i)),
        compiler_params=pltpu.CompilerParams(
            dimension_semantics=("arbitrary",), vmem_limit_bytes=VMEM_LIMIT_BYTES),
        name="hy_inproj",
    )(xs, gain.reshape(1, k), shift3, scale3, w_in.T.astype(BF16), conv_w.T)


def _hy_out_kernel(zl_ref, zc_ref, w_ref, x_ref, gate_ref, o_ref, *, n_lat_tiles):
    z = jnp.where(pl.program_id(0) >= n_lat_tiles, zc_ref[...], zl_ref[...])
    o_ref[...] = x_ref[...] + gate_ref[0] * _dotb_tn(z, w_ref[...])


def _hy_out(z_lat, z_ctx, w_out, xs, gate3, blocks_per_batch):
    t, d = xs.shape
    tm = HY_TOK_TILE
    n_lat_tiles = z_lat.shape[1] // tm
    per_tile = tm // DN_BLOCK
    grp = lambda i: (_group_of_block(i * per_tile, n_lat_tiles * per_tile, blocks_per_batch), 0, 0)
    return pl.pallas_call(
        functools.partial(_hy_out_kernel, n_lat_tiles=n_lat_tiles),
        out_shape=jax.ShapeDtypeStruct((t, d), F32),
        grid=(t // tm,),
        in_specs=[pl.BlockSpec((d, tm), lambda i: (0, jnp.minimum(i, n_lat_tiles - 1))),
                  pl.BlockSpec((d, tm), lambda i: (0, 0)),
                  pl.BlockSpec((d, d), lambda i: (0, 0)),
                  pl.BlockSpec((tm, d), lambda i: (i, 0)),
                  pl.BlockSpec((1, 1, d), grp)],
        out_specs=pl.BlockSpec((tm, d), lambda i: (i, 0)),
        compiler_params=pltpu.CompilerParams(
            dimension_semantics=("arbitrary",), vmem_limit_bytes=VMEM_LIMIT_BYTES),
        name="hy_out",
    )(z_lat, z_ctx, w_out.astype(BF16), xs, gate3)


def _hyena_layer(xs, gain, shift3, scale3, w_in, conv_w, f_w1, f_b1, f_freq, f_w2, f_b2, f_w3, bias, w_out,
                 gate3, bsz, seq):
    n_lat_tiles = bsz * seq // HY_TOK_TILE
    assert bsz * CTX_LEN == HY_TOK_TILE
    margs = (xs, gain, shift3, scale3, w_in, conv_w)
    blocks = (bsz * seq // DN_BLOCK, seq // DN_BLOCK)
    p_lat = _hy_inproj(*margs, 0, n_lat_tiles, GRID_W, *blocks)
    p_ctx = _hy_inproj(*margs, n_lat_tiles, 1, CTX_LEN, *blocks)
    fargs = (f_w1, f_b1, f_freq, f_w2, f_b2, f_w3)
    consts = _dft_constants(seq // HY_N2)
    khat = _hy_spectrum(_hy_filter(seq, *fargs), consts)
    filt_ctx = _hy_filter(CTX_LEN, *fargs)
    z_lat, z_ctx = p_lat, p_ctx
    for n in range(HY_ORDER):
        z_lat = _hy_conv(z_lat, 0, p_lat, n + 1, khat, n, bias[n], consts, bsz)
        z_ctx = _hy_ctx(z_ctx, 0, p_ctx, n + 1, filt_ctx, n, bias[n], bsz)
    return _hy_out(z_lat, z_ctx, w_out, xs, gate3, seq // DN_BLOCK)


MOE_TILE = 512
PACK = 2


def _route_kernel(x_ref, gain_ref, shift_ref, scale_ref, wr_ref, rb_ref, tri_ref,
                  f_ref, e_ref, w_ref, r_ref, cnt_ref, carry_ref):
    tm, d = x_ref.shape
    ne, epg, ng = N_EXPERTS, EXPERTS_PER_GROUP, N_GROUPS

    @pl.when(pl.program_id(0) == 0)
    def _():
        carry_ref[...] = jnp.zeros_like(carry_ref)

    x = x_ref[...]
    y = x * lax.rsqrt(jnp.mean(x * x, axis=-1, keepdims=True) + EPS) * gain_ref[...]
    f = y * (1 + scale_ref[0]) + shift_ref[0]
    bits = pltpu.bitcast(f.astype(BF16).astype(F32), jnp.uint32)
    half = d // PACK
    f_ref[...] = (bits[:, :half] >> 16) | (bits[:, half:] & jnp.uint32(0xFFFF0000))

    logits = lax.dot_general(wr_ref[...], f, (((1,), (1,)), ((), ())),
                             precision=lax.Precision.HIGHEST, preferred_element_type=F32)
    scores = jax.nn.sigmoid(logits)
    biased = scores + rb_ref[...]
    s = [scores[j * ng:(j + 1) * ng] for j in range(epg)]
    c = [biased[j * ng:(j + 1) * ng] for j in range(epg)]
    hi01, lo01 = jnp.maximum(c[0], c[1]), jnp.minimum(c[0], c[1])
    hi23, lo23 = jnp.maximum(c[2], c[3]), jnp.minimum(c[2], c[3])
    gscore = jnp.maximum(hi01, hi23) + jnp.maximum(jnp.minimum(hi01, hi23), jnp.maximum(lo01, lo23))
    gi = lax.broadcasted_iota(jnp.int32, (ng, tm), 0)
    gmax = jnp.max(gscore, axis=0, keepdims=True)
    grp = jnp.min(jnp.where(gscore == gmax, gi, ng), axis=0, keepdims=True)
    sel = gi == grp
    cv = [jnp.sum(jnp.where(sel, t, 0.0), axis=0, keepdims=True) for t in c]
    sv = [jnp.sum(jnp.where(sel, t, 0.0), axis=0, keepdims=True) for t in s]

    def pick(excluded):
        best = jnp.full((1, tm), -jnp.inf, F32)
        idx = jnp.zeros((1, tm), jnp.int32)
        val = jnp.zeros((1, tm), F32)
        for j in range(epg):
            cand = cv[j] if excluded is None else jnp.where(excluded == j, -jnp.inf, cv[j])
            take = cand > best
            best = jnp.where(take, cand, best)
            idx = jnp.where(take, j, idx)
            val = jnp.where(take, sv[j], val)
        return idx, val

    i1, v1 = pick(None)
    i2, v2 = pick(i1)
    e1 = grp * epg + i1
    e2 = grp * epg + i2
    wsum = v1 + v2
    e_ref[0:1, :] = e1
    e_ref[1:2, :] = e2
    w_ref[0:1, :] = v1 / wsum
    w_ref[1:2, :] = v2 / wsum

    ei = lax.broadcasted_iota(jnp.int32, (ne, tm), 0)
    oh1 = ei == e1
    oh2 = ei == e2
    tri = tri_ref[...]
    pre1 = jnp.dot(oh1.astype(BF16), tri, preferred_element_type=F32)
    pre2 = jnp.dot(oh2.astype(BF16), tri, preferred_element_type=F32)
    tot1 = pre1[:, tm - 1:tm]
    tot2 = pre2[:, tm - 1:tm]
    carry = carry_ref[:, 0:1]
    r1 = jnp.sum(jnp.where(oh1, carry + pre1 - 1.0, 0.0), axis=0, keepdims=True)
    r2 = jnp.sum(jnp.where(oh2, carry + tot1 + pre2 - 1.0, 0.0), axis=0, keepdims=True)
    r_ref[0:1, :] = r1.astype(jnp.int32)
    r_ref[1:2, :] = r2.astype(jnp.int32)
    carry_ref[...] = carry_ref[...] + (tot1 + tot2)
    cnt_ref[...] = carry_ref[...]


def _moe_route(xs, gain, shift3, scale3, w_router, router_bias, n_lat_tiles, blocks_per_batch):
    t, d = xs.shape
    tm = MOE_TILE
    per_tile = tm // DN_BLOCK
    row = lambda i: (i, 0)
    col = lambda i: (0, i)
    fixed = lambda i: (0, 0)
    grp = lambda i: (_group_of_block(i * per_tile, n_lat_tiles * per_tile, blocks_per_batch), 0, 0)
    tri = jnp.asarray(np.triu(np.ones((tm, tm), np.float32)), BF16)
    perm = np.arange(N_EXPERTS).reshape(N_GROUPS, EXPERTS_PER_GROUP).T.reshape(-1)
    return pl.pallas_call(
        _route_kernel,
        out_shape=[jax.ShapeDtypeStruct((t, d // PACK), jnp.uint32),
                   jax.ShapeDtypeStruct((TOP_K, t), jnp.int32),
                   jax.ShapeDtypeStruct((TOP_K, t), F32),
                   jax.ShapeDtypeStruct((TOP_K, t), jnp.int32),
                   jax.ShapeDtypeStruct((N_EXPERTS, 128), F32)],
        grid=(t // tm,),
        in_specs=[pl.BlockSpec((tm, d), row),
                  pl.BlockSpec((1, d), fixed),
                  pl.BlockSpec((1, 1, d), grp),
                  pl.BlockSpec((1, 1, d), grp),
                  pl.BlockSpec((N_EXPERTS, d), fixed),
                  pl.BlockSpec((N_EXPERTS, 1), fixed),
                  pl.BlockSpec((tm, tm), fixed)],
        out_specs=[pl.BlockSpec((tm, d // PACK), row),
                   pl.BlockSpec((TOP_K, tm), col),
                   pl.BlockSpec((TOP_K, tm), col),
                   pl.BlockSpec((TOP_K, tm), col),
                   pl.BlockSpec((N_EXPERTS, 128), fixed)],
        scratch_shapes=[pltpu.VMEM((N_EXPERTS, 128), F32)],
        compiler_params=pltpu.CompilerParams(
            dimension_semantics=("arbitrary",), vmem_limit_bytes=VMEM_LIMIT_BYTES),
        name="moe_route",
    )(xs, gain.reshape(1, d), shift3, scale3, w_router.T[perm], router_bias[perm].reshape(N_EXPERTS, 1), tri)


def _row_copy(src, src_row, dst, dst_row, sem):
    return pltpu.make_async_copy(src.at[pl.ds(src_row, 1)], dst.at[pl.ds(dst_row, 1)], sem)


def _dispatch_kernel(dest_ref, f_ref, xs_in_ref, xs_ref, dest_smem, sem, idx_sem):
    del xs_in_ref
    tm = f_ref.shape[0]
    idx_copy = pltpu.make_async_copy(dest_ref, dest_smem, idx_sem)
    idx_copy.start()
    idx_copy.wait()

    def issue(tok, carry):
        for k in range(TOP_K):
            _row_copy(f_ref, tok, xs_ref, dest_smem[k, tok], sem).start(priority=k)
        return carry

    def drain(tok, carry):
        for k in range(TOP_K):
            _row_copy(f_ref, 0, xs_ref, 0, sem).wait()
        return carry

    lax.fori_loop(0, tm, issue, 0, unroll=8)
    lax.fori_loop(0, tm, drain, 0, unroll=8)


def _dispatch(f_packed, dest, n_slots):
    t, wd = f_packed.shape
    tm = MOE_TILE
    return pl.pallas_call(
        _dispatch_kernel,
        out_shape=jax.ShapeDtypeStruct((n_slots, wd), jnp.uint32),
        grid=(t // tm,),
        in_specs=[pl.BlockSpec((TOP_K, tm), lambda i: (0, i)),
                  pl.BlockSpec((tm, wd), lambda i: (i, 0)),
                  pl.BlockSpec(memory_space=pl.ANY)],
        out_specs=pl.BlockSpec(memory_space=pl.ANY),
        scratch_shapes=[pltpu.SMEM((TOP_K, tm), jnp.int32),
                        pltpu.SemaphoreType.DMA, pltpu.SemaphoreType.DMA],
        input_output_aliases={2: 0},
        compiler_params=pltpu.CompilerParams(
            dimension_semantics=("arbitrary",), vmem_limit_bytes=VMEM_LIMIT_BYTES),
        name="moe_dispatch",
    )(dest, f_packed, jnp.zeros((n_slots, wd), jnp.uint32))


def _experts_kernel(be_ref, na_ref, x_ref, wg_ref, wu_ref, wd_ref, o_ref, wgb_ref, wub_ref, wdb_ref):
    i = pl.program_id(0)
    prev = be_ref[jnp.maximum(i - 1, 0)]

    @pl.when((i == 0) | (be_ref[i] != prev))
    def _():
        wgb_ref[...] = wg_ref[0, 0].astype(BF16)
        wub_ref[...] = wu_ref[0, 0].astype(BF16)
        wdb_ref[...] = wd_ref[0, 0].astype(BF16)

    @pl.when(i < na_ref[0])
    def _():
        packed = x_ref[...]
        lo = pltpu.bitcast(packed << 16, F32)
        hi = pltpu.bitcast(packed & jnp.uint32(0xFFFF0000), F32)
        xb = jnp.concatenate([lo, hi], axis=-1).astype(BF16)
        g = jnp.dot(xb, wgb_ref[...], preferred_element_type=F32)
        u = jnp.dot(xb, wub_ref[...], preferred_element_type=F32)
        hid = (g * jax.nn.sigmoid(g)) * u
        y = jnp.dot(hid.astype(BF16), wdb_ref[...], preferred_element_type=F32)
        ybits = pltpu.bitcast(y.astype(BF16).astype(F32), jnp.uint32)
        half = y.shape[1] // PACK
        o_ref[...] = (ybits[:, :half] >> 16) | (ybits[:, half:] & jnp.uint32(0xFFFF0000))

    @pl.when(i >= na_ref[0])
    def _():
        o_ref[...] = jnp.zeros_like(o_ref)


def _experts(xs_sorted, block_expert, n_active, w_gate, w_up, w_down, layer):
    n_slots, wd = xs_sorted.shape
    d = wd * PACK
    n_blocks = n_slots // MOE_BLOCK
    blk = lambda i, be, na: (jnp.minimum(i, na[0] - 1), 0)
    wsel = lambda i, be, na: (layer, be[jnp.minimum(i, na[0] - 1)], 0, 0)
    grid_spec = pltpu.PrefetchScalarGridSpec(
        num_scalar_prefetch=2,
        grid=(n_blocks,),
        in_specs=[pl.BlockSpec((MOE_BLOCK, wd), blk),
                  pl.BlockSpec((1, 1, d, D_EXPERT), wsel),
                  pl.BlockSpec((1, 1, d, D_EXPERT), wsel),
                  pl.BlockSpec((1, 1, D_EXPERT, d), wsel)],
        out_specs=pl.BlockSpec((MOE_BLOCK, wd), lambda i, be, na: (i, 0)),
        scratch_shapes=[pltpu.VMEM((d, D_EXPERT), BF16), pltpu.VMEM((d, D_EXPERT), BF16),
                        pltpu.VMEM((D_EXPERT, d), BF16)],
    )
    return pl.pallas_call(
        _experts_kernel,
        out_shape=jax.ShapeDtypeStruct((n_slots, wd), jnp.uint32),
        grid_spec=grid_spec,
        compiler_params=pltpu.CompilerParams(
            dimension_semantics=("arbitrary",), vmem_limit_bytes=VMEM_LIMIT_BYTES),
        name="moe_experts",
    )(block_expert, n_active, xs_sorted, w_gate, w_up, w_down)


def _combine_kernel(dest_ref, y_ref, x_ref, w_ref, gate_ref, fin_ref, o_ref, dest_smem, ya_ref, yb_ref, sem,
                    idx_sem, *, final_norm):
    tm = x_ref.shape[0]
    idx_copy = pltpu.make_async_copy(dest_ref, dest_smem, idx_sem)
    idx_copy.start()
    idx_copy.wait()
    bufs = (ya_ref, yb_ref)

    def issue(tok, carry):
        for k in range(TOP_K):
            _row_copy(y_ref, dest_smem[k, tok], bufs[k], tok, sem).start(priority=k)
        return carry

    def drain(tok, carry):
        for k in range(TOP_K):
            _row_copy(y_ref, 0, bufs[k], 0, sem).wait()
        return carry

    lax.fori_loop(0, tm, issue, 0, unroll=8)
    lax.fori_loop(0, tm, drain, 0, unroll=8)
    w = w_ref[...]
    out = x_ref[...] + gate_ref[0] * (w[:, 0:1] * ya_ref[...] + w[:, 1:2] * yb_ref[...])
    if final_norm:
        out = out * lax.rsqrt(jnp.mean(out * out, axis=-1, keepdims=True) + EPS) * fin_ref[...]
    o_ref[...] = out


def _combine(ys, dest, weight_cols, xs, gate3, n_lat_tiles, blocks_per_batch, final_gain=None):
    t, d = xs.shape
    tm = MOE_TILE
    per_tile = tm // DN_BLOCK
    n_tiles = t // tm if final_gain is None else n_lat_tiles
    fin = jnp.ones((1, d), F32) if final_gain is None else final_gain.reshape(1, d)
    row = lambda i: (i, 0)
    grp = lambda i: (_group_of_block(i * per_tile, n_lat_tiles * per_tile, blocks_per_batch), 0, 0)
    return pl.pallas_call(
        functools.partial(_combine_kernel, final_norm=final_gain is not None),
        out_shape=jax.ShapeDtypeStruct((n_tiles * tm, d), F32),
        grid=(n_tiles,),
        in_specs=[pl.BlockSpec((TOP_K, tm), lambda i: (0, i)),
                  pl.BlockSpec(memory_space=pl.ANY),
                  pl.BlockSpec((tm, d), row),
                  pl.BlockSpec((tm, TOP_K), row),
                  pl.BlockSpec((1, 1, d), grp),
                  pl.BlockSpec((1, d), lambda i: (0, 0))],
        out_specs=pl.BlockSpec((tm, d), row),
        scratch_shapes=[pltpu.SMEM((TOP_K, tm), jnp.int32),
                        pltpu.VMEM((tm, d), F32), pltpu.VMEM((tm, d), F32),
                        pltpu.SemaphoreType.DMA, pltpu.SemaphoreType.DMA],
        compiler_params=pltpu.CompilerParams(
            dimension_semantics=("arbitrary",), vmem_limit_bytes=VMEM_LIMIT_BYTES),
        name="moe_combine",
    )(dest, ys, xs, weight_cols, gate3, fin)


def _moe_layer(xs, gain, shift3, scale3, gate3, w_router, router_bias, w_gate, w_up, w_down, layer,
               n_lat_tiles, blocks_per_batch, final_gain=None):
    t = xs.shape[0]
    f_packed, expert, weight, rank, counts = _moe_route(xs, gain, shift3, scale3, w_router, router_bias,
                                                        n_lat_tiles, blocks_per_batch)
    counts = counts[:, 0].astype(jnp.int32)
    padded = (counts + MOE_BLOCK - 1) // MOE_BLOCK * MOE_BLOCK
    pend = jnp.cumsum(padded)
    pstart = pend - padded
    n_blocks = -(-(t * TOP_K) // MOE_BLOCK) + N_EXPERTS
    block_start = jnp.arange(n_blocks, dtype=jnp.int32) * MOE_BLOCK
    block_expert = jnp.minimum(jnp.sum(pend[None, :] <= block_start[:, None], axis=1),
                               N_EXPERTS - 1).astype(jnp.int32)
    n_active = (pend[-1:] // MOE_BLOCK).astype(jnp.int32)
    is_e = expert[..., None] == jnp.arange(N_EXPERTS, dtype=jnp.int32)
    dest = rank + jnp.sum(jnp.where(is_e, pstart, 0), axis=-1)
    xs_sorted = _dispatch(f_packed, dest, n_blocks * MOE_BLOCK)
    ys = _experts(xs_sorted, block_expert, n_active, w_gate, w_up, w_down, layer)
    return _combine(ys, dest, weight.T, xs, gate3, n_lat_tiles, blocks_per_batch, final_gain)


MOE_CHUNK = 8
MOE_LB = 1280
assert MOE_LB >= MOE_TILE * TOP_K + N_EXPERTS * (MOE_CHUNK - 1) and MOE_LB % 128 == 0
N_CHUNKS = MOE_LB // MOE_CHUNK
TAB_W = 256
assert TAB_W >= N_CHUNKS


def _moe_route_kernel(x_ref, gain_ref, shift_ref, scale_ref, wr_ref, rb_ref, tri_ref, lt_ref,
                      f_ref, pos_ref, w_ref, cnt_ref):
    tm, d = x_ref.shape
    ne, epg, ng = N_EXPERTS, EXPERTS_PER_GROUP, N_GROUPS
    f = _modulated(x_ref[...], gain_ref[...], shift_ref[0], scale_ref[0])
    f_ref[...] = f.astype(BF16)

    logits = lax.dot_general(wr_ref[...], f, (((1,), (1,)), ((), ())),
                             precision=lax.Precision.HIGHEST, preferred_element_type=F32)
    scores = jax.nn.sigmoid(logits)
    biased = scores + rb_ref[...]
    s = [scores[j * ng:(j + 1) * ng] for j in range(epg)]
    c = [biased[j * ng:(j + 1) * ng] for j in range(epg)]
    hi01, lo01 = jnp.maximum(c[0], c[1]), jnp.minimum(c[0], c[1])
    hi23, lo23 = jnp.maximum(c[2], c[3]), jnp.minimum(c[2], c[3])
    gscore = jnp.maximum(hi01, hi23) + jnp.maximum(jnp.minimum(hi01, hi23), jnp.maximum(lo01, lo23))
    gi = lax.broadcasted_iota(jnp.int32, (ng, tm), 0)
    gmax = jnp.max(gscore, axis=0, keepdims=True)
    grp = jnp.min(jnp.where(gscore == gmax, gi, ng), axis=0, keepdims=True)
    sel = gi == grp
    cv = [jnp.sum(jnp.where(sel, t, 0.0), axis=0, keepdims=True) for t in c]
    sv = [jnp.sum(jnp.where(sel, t, 0.0), axis=0, keepdims=True) for t in s]

    def pick(excluded):
        best = jnp.full((1, tm), -jnp.inf, F32)
        idx = jnp.zeros((1, tm), jnp.int32)
        val = jnp.zeros((1, tm), F32)
        for j in range(epg):
            cand = cv[j] if excluded is None else jnp.where(excluded == j, -jnp.inf, cv[j])
            take = cand > best
            best = jnp.where(take, cand, best)
            idx = jnp.where(take, j, idx)
            val = jnp.where(take, sv[j], val)
        return idx, val

    i1, v1 = pick(None)
    i2, v2 = pick(i1)
    wsum = v1 + v2
    w_ref[0:1, :] = v1 / wsum
    w_ref[1:2, :] = v2 / wsum

    ei = lax.broadcasted_iota(jnp.int32, (ne, tm), 0)
    oh1 = ei == grp * epg + i1
    oh2 = ei == grp * epg + i2
    tri = tri_ref[...]
    pre1 = jnp.dot(oh1.astype(BF16), tri, preferred_element_type=F32)
    pre2 = jnp.dot(oh2.astype(BF16), tri, preferred_element_type=F32)
    tot1 = pre1[:, tm - 1:tm]
    tot = tot1 + pre2[:, tm - 1:tm]
    seg = jnp.floor((tot + (MOE_CHUNK - 1)) * (1.0 / MOE_CHUNK)) * MOE_CHUNK
    off = jnp.dot(lt_ref[...], jnp.broadcast_to(seg, (ne, 128)).astype(BF16),
                  preferred_element_type=F32)[:, 0:1]
    p1 = jnp.sum(jnp.where(oh1, off + pre1 - 1.0, 0.0), axis=0, keepdims=True)
    p2 = jnp.sum(jnp.where(oh2, off + tot1 + pre2 - 1.0, 0.0), axis=0, keepdims=True)
    pos_ref[0:1, :] = p1.astype(jnp.int32)
    pos_ref[1:2, :] = p2.astype(jnp.int32)
    cnt_ref[0] = jnp.broadcast_to(tot, (ne, 128))


def _moe_route(xs, gain, shift3, scale3, w_router, router_bias, n_lat_tiles, blocks_per_batch):
    t, d = xs.shape
    tm = MOE_TILE
    ne = N_EXPERTS
    per_tile = tm // DN_BLOCK
    row = lambda i: (i, 0)
    col = lambda i: (0, i)
    fixed = lambda i: (0, 0)
    grp = lambda i: (_group_of_block(i * per_tile, n_lat_tiles * per_tile, blocks_per_batch), 0, 0)
    tri = jnp.asarray(np.triu(np.ones((tm, tm), np.float32)), BF16)
    lt = jnp.asarray(np.tril(np.ones((ne, ne), np.float32), -1), BF16)
    perm = np.arange(ne).reshape(N_GROUPS, EXPERTS_PER_GROUP).T.reshape(-1)
    return pl.pallas_call(
        _moe_route_kernel,
        out_shape=[jax.ShapeDtypeStruct((t, d), BF16),
                   jax.ShapeDtypeStruct((TOP_K, t), jnp.int32),
                   jax.ShapeDtypeStruct((TOP_K, t), F32),
                   jax.ShapeDtypeStruct((t // tm, ne, 128), F32)],
        grid=(t // tm,),
        in_specs=[pl.BlockSpec((tm, d), row),
                  pl.BlockSpec((1, d), fixed),
                  pl.BlockSpec((1, 1, d), grp),
                  pl.BlockSpec((1, 1, d), grp),
                  pl.BlockSpec((ne, d), fixed),
                  pl.BlockSpec((ne, 1), fixed),
                  pl.BlockSpec((tm, tm), fixed),
                  pl.BlockSpec((ne, ne), fixed)],
        out_specs=[pl.BlockSpec((tm, d), row),
                   pl.BlockSpec((TOP_K, tm), col),
                   pl.BlockSpec((TOP_K, tm), col),
                   pl.BlockSpec((1, ne, 128), lambda i: (i, 0, 0))],
        compiler_params=pltpu.CompilerParams(
            dimension_semantics=("arbitrary",), vmem_limit_bytes=VMEM_LIMIT_BYTES),
        name="moe_route",
    )(xs, gain.reshape(1, d), shift3, scale3, w_router.T[perm], router_bias[perm].reshape(ne, 1), tri, lt)


def _chunk_row(j):
    return j * MOE_CHUNK if isinstance(j, int) else pl.multiple_of(j * MOE_CHUNK, MOE_CHUNK)


def _chunk_issue(tab_smem, tab_row, make_copy):
    def issue(j, n):
        dst = tab_smem[tab_row, j]

        @pl.when(dst >= 0)
        def _():
            make_copy(j, pl.multiple_of(dst, MOE_CHUNK)).start()

        return n + (dst >= 0).astype(jnp.int32)

    return lax.fori_loop(0, N_CHUNKS, issue, jnp.int32(0), unroll=8)


def _chunk_drain(n, make_copy):
    def drain(j, carry):
        make_copy(0, 0).wait()
        return carry

    lax.fori_loop(0, n, drain, 0)


def _moe_dispatch_kernel(tab_ref, pos_ref, f_ref, xs_in_ref, xs_ref, tab_smem, cnt_smem, loc_ref, sem,
                         idx_sem, *, n_steps):
    del xs_in_ref
    tm, d = f_ref.shape
    i = pl.program_id(0)
    slot = i % 2
    idx_copy = pltpu.make_async_copy(tab_ref.at[0], tab_smem, idx_sem)
    idx_copy.start()
    r = lax.broadcasted_iota(jnp.int32, (tm, MOE_LB), 1)
    p = pos_ref[...]
    onehot = ((p[:, 0:1] == r) | (p[:, 1:2] == r)).astype(BF16)
    loc = _dotb_tn(onehot, f_ref[...])
    bits = pltpu.bitcast(loc, jnp.uint32)
    half = d // PACK
    loc_ref[slot] = (bits[:, :half] >> 16) | (bits[:, half:] & jnp.uint32(0xFFFF0000))
    idx_copy.wait()

    def copy_from(buf):
        def make_copy(j, dst):
            return pltpu.make_async_copy(loc_ref.at[buf, pl.ds(_chunk_row(j), MOE_CHUNK)],
                                         xs_ref.at[pl.ds(dst, MOE_CHUNK)], sem.at[buf])
        return make_copy

    n = _chunk_issue(tab_smem, 0, copy_from(slot))
    cnt_smem[slot] = n

    @pl.when(i > 0)
    def _():
        _chunk_drain(cnt_smem[1 - slot], copy_from(1 - slot))

    @pl.when(i == n_steps - 1)
    def _():
        _chunk_drain(n, copy_from(slot))


def _moe_dispatch(f, pos_cols, table, slots):
    t, d = f.shape
    tm = MOE_TILE
    n_slots, wd = slots.shape
    return pl.pallas_call(
        functools.partial(_moe_dispatch_kernel, n_steps=t // tm),
        out_shape=jax.ShapeDtypeStruct((n_slots, wd), jnp.uint32),
        grid=(t // tm,),
        in_specs=[pl.BlockSpec((1, 1, TAB_W), lambda i: (i, 0, 0)),
                  pl.BlockSpec((tm, TOP_K), lambda i: (i, 0)),
                  pl.BlockSpec((tm, d), lambda i: (i, 0)),
                  pl.BlockSpec(memory_space=pl.ANY)],
        out_specs=pl.BlockSpec(memory_space=pl.ANY),
        scratch_shapes=[pltpu.SMEM((1, TAB_W), jnp.int32),
                        pltpu.SMEM((2,), jnp.int32),
                        pltpu.VMEM((2, MOE_LB, wd), jnp.uint32),
                        pltpu.SemaphoreType.DMA((2,)), pltpu.SemaphoreType.DMA],
        input_output_aliases={3: 0},
        compiler_params=pltpu.CompilerParams(
            dimension_semantics=("arbitrary",), vmem_limit_bytes=VMEM_LIMIT_BYTES),
        name="moe_dispatch",
    )(table, pos_cols, f, slots)


def _moe_combine_kernel(tab_ref, nxt_ref, pos_ref, w_ref, y_ref, x_ref, gate_ref, fin_ref, o_ref, tab_smem,
                        cnt_smem, yloc_ref, sem, idx_sem, *, final_norm, n_steps):
    tm = x_ref.shape[0]
    i = pl.program_id(0)
    slot = i % 2

    def fetch(table_block, buf):
        idx_copy = pltpu.make_async_copy(table_block.at[0], tab_smem.at[pl.ds(buf, 1)], idx_sem)
        idx_copy.start()
        idx_copy.wait()
        cnt_smem[buf] = _chunk_issue(tab_smem, buf, copy_into(buf))

    def copy_into(buf):
        def make_copy(j, src):
            return pltpu.make_async_copy(y_ref.at[pl.ds(src, MOE_CHUNK)],
                                         yloc_ref.at[buf, pl.ds(_chunk_row(j), MOE_CHUNK)], sem.at[buf])
        return make_copy

    @pl.when(i == 0)
    def _():
        yloc_ref[...] = jnp.zeros_like(yloc_ref)
        fetch(tab_ref, 0)

    @pl.when(i + 1 < n_steps)
    def _():
        fetch(nxt_ref, 1 - slot)

    _chunk_drain(cnt_smem[slot], copy_into(slot))
    r = lax.broadcasted_iota(jnp.int32, (tm, MOE_LB), 1)
    p = pos_ref[...]
    w = w_ref[...]
    wmat = jnp.where(p[:, 0:1] == r, w[:, 0:1], 0.0) + jnp.where(p[:, 1:2] == r, w[:, 1:2], 0.0)
    packed = yloc_ref[slot]
    y_lo = pltpu.bitcast(packed << 16, F32)
    y_hi = pltpu.bitcast(packed & jnp.uint32(0xFFFF0000), F32)
    moe = jnp.concatenate([_dotb(wmat, y_lo), _dotb(wmat, y_hi)], axis=-1)
    out = x_ref[...] + gate_ref[0] * moe
    if final_norm:
        out = out * lax.rsqrt(jnp.mean(out * out, axis=-1, keepdims=True) + EPS) * fin_ref[...]
    o_ref[...] = out


def _moe_combine(ys, pos_cols, weight_cols, table, xs, gate3, n_lat_tiles, blocks_per_batch, final_gain=None):
    t, d = xs.shape
    tm = MOE_TILE
    per_tile = tm // DN_BLOCK
    n_tiles = t // tm if final_gain is None else n_lat_tiles
    fin = jnp.ones((1, d), F32) if final_gain is None else final_gain.reshape(1, d)
    row = lambda i: (i, 0)
    grp = lambda i: (_group_of_block(i * per_tile, n_lat_tiles * per_tile, blocks_per_batch), 0, 0)
    last = table.shape[0] - 1
    return pl.pallas_call(
        functools.partial(_moe_combine_kernel, final_norm=final_gain is not None, n_steps=n_tiles),
        out_shape=jax.ShapeDtypeStruct((n_tiles * tm, d), F32),
        grid=(n_tiles,),
        in_specs=[pl.BlockSpec((1, 1, TAB_W), lambda i: (i, 0, 0)),
                  pl.BlockSpec((1, 1, TAB_W), lambda i: (jnp.minimum(i + 1, last), 0, 0)),
                  pl.BlockSpec((tm, TOP_K), row),
                  pl.BlockSpec((tm, TOP_K), row),
                  pl.BlockSpec(memory_space=pl.ANY),
                  pl.BlockSpec((tm, d), row),
                  pl.BlockSpec((1, 1, d), grp),
                  pl.BlockSpec((1, d), lambda i: (0, 0))],
        out_specs=pl.BlockSpec((tm, d), row),
        scratch_shapes=[pltpu.SMEM((2, TAB_W), jnp.int32),
                        pltpu.SMEM((2,), jnp.int32),
                        pltpu.VMEM((2, MOE_LB, d // PACK), jnp.uint32),
                        pltpu.SemaphoreType.DMA((2,)), pltpu.SemaphoreType.DMA],
        compiler_params=pltpu.CompilerParams(
            dimension_semantics=("arbitrary",), vmem_limit_bytes=VMEM_LIMIT_BYTES),
        name="moe_combine",
    )(table, table, pos_cols, weight_cols, ys, xs, gate3, fin)


def _moe_layer(xs, gain, shift3, scale3, gate3, w_router, router_bias, w_gate, w_up, w_down, layer,
               n_lat_tiles, blocks_per_batch, final_gain=None, slots=None):
    t = xs.shape[0]
    n_tiles = t // MOE_TILE
    f, pos, weight, cnt = _moe_route(xs, gain, shift3, scale3, w_router, router_bias,
                                     n_lat_tiles, blocks_per_batch)
    seg = (cnt[:, :, 0].astype(jnp.int32) + MOE_CHUNK - 1) // MOE_CHUNK * MOE_CHUNK
    loc_end = jnp.cumsum(seg, axis=1)
    loc_off = loc_end - seg
    padded = (jnp.sum(seg, axis=0) + MOE_BLOCK - 1) // MOE_BLOCK * MOE_BLOCK
    pend = jnp.cumsum(padded)
    seg_start = (pend - padded)[None, :] + jnp.cumsum(seg, axis=0) - seg
    n_blocks = -(-(t * TOP_K + n_tiles * N_EXPERTS * (MOE_CHUNK - 1)) // MOE_BLOCK) + N_EXPERTS
    block_start = jnp.arange(n_blocks, dtype=jnp.int32) * MOE_BLOCK
    block_expert = jnp.minimum(jnp.sum(pend[None, :] <= block_start[:, None], axis=1),
                               N_EXPERTS - 1).astype(jnp.int32)
    n_active = (pend[-1:] // MOE_BLOCK).astype(jnp.int32)
    row0 = jnp.arange(N_CHUNKS, dtype=jnp.int32) * MOE_CHUNK
    e_of = jnp.sum(loc_end[:, None, :] <= row0[None, :, None], axis=-1)
    is_e = e_of[..., None] == jnp.arange(N_EXPERTS, dtype=jnp.int32)
    shift = jnp.sum(jnp.where(is_e, (seg_start - loc_off)[:, None, :], 0), axis=-1)
    table = jnp.where(e_of < N_EXPERTS, row0[None, :] + shift, -1)
    table = jnp.pad(table, ((0, 0), (0, TAB_W - N_CHUNKS)), constant_values=-1).reshape(n_tiles, 1, TAB_W)

    pos_cols = pos.T
    if slots is None:
        slots = jnp.zeros((n_blocks * MOE_BLOCK, f.shape[1] // PACK), jnp.uint32)
    xs_sorted = _moe_dispatch(f, pos_cols, table, slots)
    ys = _experts(xs_sorted, block_expert, n_active, w_gate, w_up, w_down, layer)
    out = _moe_combine(ys, pos_cols, weight.T, table, xs, gate3, n_lat_tiles, blocks_per_batch, final_gain)
    return out, xs_sorted


def _rmsnorm(x, gain):
    y = x * lax.rsqrt(jnp.mean(x * x, axis=-1, keepdims=True) + EPS)
    return y * gain


def _modulate(x, gain, shift, scale):
    return _rmsnorm(x, gain) * (1 + scale) + shift


def _l2norm(t):
    return t * lax.rsqrt(jnp.sum(t * t, axis=-1, keepdims=True) + EPS)


def _short_conv(x, w, on_grid):
    b, l, ch = x.shape
    xs = x.reshape(b, l // GRID_W, GRID_W, ch) if on_grid else x.reshape(b, 1, l, ch)
    n = xs.shape[2]
    xp = jnp.pad(xs, ((0, 0), (0, 0), (1, 1), (0, 0)))
    y = w[0] * xp[:, :, 0:n] + w[1] * xp[:, :, 1:n + 1] + w[2] * xp[:, :, 2:n + 2]
    return y.reshape(b, l, ch)


def _gated_delta_chunked(q, k, v, g, beta, s0):
    b, h, l, dk = q.shape
    dv = v.shape[-1]
    c = DN_CHUNK
    n = l // c
    q = q.reshape(b, h, n, c, dk)
    k = k.reshape(b, h, n, c, dk)
    v = v.reshape(b, h, n, c, dv)
    g = jnp.cumsum(g.reshape(b, h, n, c), axis=-1)
    beta = beta.reshape(b, h, n, c, 1)
    pos = jnp.arange(c)
    incl = pos[:, None] >= pos[None, :]
    strict = pos[:, None] > pos[None, :]
    decay = jnp.exp(jnp.where(incl, g[..., :, None] - g[..., None, :], -jnp.inf))
    kb = k * beta
    a_mat = jnp.einsum('bhnid,bhnjd->bhnij', kb, k) * jnp.where(strict, decay, 0.0)
    rhs = jnp.concatenate([v * beta, kb * jnp.exp(g)[..., None]], axis=-1)
    sol = lax.linalg.triangular_solve(a_mat + jnp.eye(c, dtype=a_mat.dtype), rhs,
                                      left_side=True, lower=True, unit_diagonal=True)
    u, w = sol[..., :dv], sol[..., dv:]
    attn = jnp.einsum('bhnid,bhnjd->bhnij', q, k) * decay
    g_last = g[..., -1:]
    q_dec = q * jnp.exp(g)[..., None]
    k_dec = k * jnp.exp(g_last - g)[..., None]

    def step(s, inp):
        qd, kd, uu, ww, at, gl = inp
        v_new = uu - jnp.einsum('bhck,bhkv->bhcv', ww, s)
        o = jnp.einsum('bhck,bhkv->bhcv', qd, s) + jnp.einsum('bhcs,bhsv->bhcv', at, v_new)
        s = s * jnp.exp(gl)[..., None] + jnp.einsum('bhck,bhcv->bhkv', kd, v_new)
        return s, o

    xs = tuple(jnp.moveaxis(t, 2, 0) for t in (q_dec, k_dec, u, w, attn, g_last))
    s_final, o = lax.scan(step, s0, xs)
    o = jnp.moveaxis(o, 0, 2).reshape(b, h, l, dv)
    return o, s_final


def _deltanet_mixer(p_ctx, p_lat, conv_w, a_log, dt_bias, out_norm):
    d = D_MODEL
    nh = DN_HEADS

    def project(p, on_grid):
        b, l, _ = p.shape
        qkv = jax.nn.silu(_short_conv(p[..., :3 * d], conv_w, on_grid))
        z = p[..., 3 * d:4 * d]
        a = p[..., 4 * d:4 * d + 2 * nh].reshape(b, l, 2, nh)
        bb = p[..., 4 * d + 2 * nh:].reshape(b, l, 2, nh)

        def heads(t):
            return jnp.transpose(t.reshape(b, l, nh, -1), (0, 2, 1, 3))

        q, k, v = (heads(t) for t in jnp.split(qkv, 3, axis=-1))
        q = _l2norm(q) * DN_DK ** -0.5
        k = _l2norm(k)
        g = -jnp.exp(a_log) * jax.nn.softplus(a + dt_bias)
        g = jnp.transpose(g, (2, 0, 3, 1))
        beta = jnp.transpose(jax.nn.sigmoid(bb), (2, 0, 3, 1))
        return q, k, v, g, beta, z

    def scan_both(q, k, v, g, beta, s_f, s_b):
        o_f, s_f = _gated_delta_chunked(q, k, v, g[0], beta[0], s_f)
        rev = lambda t: jnp.flip(t, axis=2)
        o_b, s_b = _gated_delta_chunked(rev(q), rev(k), rev(v), rev(g[1]), rev(beta[1]), s_b)
        return o_f + rev(o_b), s_f, s_b

    def finish(o, z):
        b, _, l, _ = o.shape
        o = jnp.transpose(o, (0, 2, 1, 3))
        o = o * lax.rsqrt(jnp.mean(o * o, axis=-1, keepdims=True) + EPS) * out_norm
        o = o * jax.nn.silu(z.reshape(b, l, nh, DN_DV))
        return o.reshape(b, l, d)

    qc, kc, vc, gc, bc, zc = project(p_ctx, False)
    s0 = jnp.zeros((p_ctx.shape[0], nh, DN_DK, DN_DV), F32)
    o_c, s_f, s_b = scan_both(qc, kc, vc, gc, bc, s0, s0)
    ql, kl, vl, gla, bl, zl = project(p_lat, True)
    o_l, _, _ = scan_both(ql, kl, vl, gla, bl, s_f, s_b)
    return finish(o_c, zc), finish(o_l, zl)


def _hyena_filters(l, w1, b1, freq, w2, b2, w3):
    pos = jnp.arange(l, dtype=F32)[:, None]
    t = pos / max(l - 1, 1)
    bands = jnp.linspace(1e-4, HY_BANDS - 1, HY_BANDS, dtype=F32)[None, :]
    ang = (2 * math.pi / l) * pos * bands
    feat = jnp.concatenate([t, jnp.cos(ang), -jnp.sin(ang)], axis=-1)
    hp = lax.Precision.HIGHEST
    hdn = jnp.sin(freq * (jnp.dot(feat, w1, precision=hp) + b1))
    hdn = jnp.sin(freq * (jnp.dot(hdn, w2, precision=hp) + b2))
    filt = jnp.dot(hdn, w3, precision=hp).reshape(l, HY_ORDER, 2, D_MODEL)
    deltas = jnp.abs(jnp.linspace(math.log(HY_TARGET) / HY_SLOW, math.log(HY_TARGET) / HY_FAST,
                                  D_MODEL, dtype=F32))
    window = jnp.exp(-t * deltas[None, :])
    return filt * window[:, None, None, :]


def _two_sided_fftconv(u, h_fwd, h_bwd):
    l = u.shape[1]
    k = jnp.concatenate([h_fwd, jnp.zeros_like(h_fwd[:1]), jnp.flip(h_bwd[1:], axis=0)], axis=0)
    kf = jnp.fft.rfft(k, axis=0)
    uf = jnp.fft.rfft(u, n=2 * l, axis=1)
    return jnp.fft.irfft(uf * kf[None], n=2 * l, axis=1)[:, :l]


def _hyena_stream(p, on_grid, conv_w, f_w1, f_b1, f_freq, f_w2, f_b2, f_w3, bias):
    l = p.shape[1]
    p = _short_conv(p, conv_w, on_grid)
    v, x1, x2 = jnp.split(p, 3, axis=-1)
    filt = _hyena_filters(l, f_w1, f_b1, f_freq, f_w2, f_b2, f_w3)
    z = v
    for n, gate in enumerate((x1, x2)):
        conv = _two_sided_fftconv(z, filt[:, n, 0], filt[:, n, 1])
        z = gate * (conv + bias[n] * z)
    return z


def _shortconv_stream(p, on_grid, conv_w):
    bg, cg, xin = jnp.split(p, 3, axis=-1)
    return bg * _short_conv(cg * xin, conv_w, on_grid)


def _route(h, w_router, router_bias):
    t = h.shape[0]
    scores = jax.nn.sigmoid(jnp.dot(h, w_router, precision=lax.Precision.HIGHEST))
    choice = (scores + router_bias).reshape(t, N_GROUPS, EXPERTS_PER_GROUP)
    group_score = lax.top_k(choice, GROUP_SCORE_K)[0].sum(-1)
    group = jnp.argmax(group_score, axis=-1)
    in_group = jnp.take_along_axis(choice, group[:, None, None], axis=1)[:, 0]
    local = lax.top_k(in_group, TOP_K)[1]
    expert = group[:, None] * EXPERTS_PER_GROUP + local
    weight = jnp.take_along_axis(scores, expert, axis=1)
    weight = weight / jnp.sum(weight, axis=-1, keepdims=True)
    return expert.astype(jnp.int32), weight


def _moe_ffn(x, w_router, router_bias, w_gate, w_up, w_down):
    t, d = x.shape
    expert, weight = _route(x, w_router, router_bias)
    a = t * TOP_K
    e_flat = expert.reshape(-1)
    order = jnp.argsort(e_flat)
    e_sorted = e_flat[order]
    tok_sorted = (order // TOP_K).astype(jnp.int32)
    counts = jnp.zeros((N_EXPERTS,), jnp.int32).at[e_flat].add(1)
    start = jnp.cumsum(counts) - counts
    padded = (counts + MOE_BLOCK - 1) // MOE_BLOCK * MOE_BLOCK
    pend = jnp.cumsum(padded)
    pstart = pend - padded
    dest = pstart[e_sorted] + (jnp.arange(a, dtype=jnp.int32) - start[e_sorted])
    n_blocks = -(-a // MOE_BLOCK) + N_EXPERTS
    n_slots = n_blocks * MOE_BLOCK
    slot_tok = jnp.full((n_slots,), t, jnp.int32).at[dest].set(tok_sorted)
    block_start = jnp.arange(n_blocks, dtype=jnp.int32) * MOE_BLOCK
    block_expert = jnp.minimum(jnp.searchsorted(pend, block_start, side='right'),
                               N_EXPERTS - 1).astype(jnp.int32)
    x_pad = jnp.concatenate([x.astype(BF16), jnp.zeros((1, d), BF16)], axis=0)
    xs = x_pad[slot_tok]
    ys = _expert_ffn(xs, block_expert, jnp.ones((n_slots,), F32), w_gate, w_up, w_down)
    slot_of = jnp.zeros((a,), jnp.int32).at[order].set(dest).reshape(t, TOP_K)
    out = weight[:, 0:1] * ys[slot_of[:, 0]] + weight[:, 1:2] * ys[slot_of[:, 1]]
    return out


def _sc_layer_kernel(x_ref, gain_ref, shift_ref, scale_ref, win_ref, cw_ref, wout_ref, gate_ref, o_ref, *,
                     n_lat_tiles):
    d = D_MODEL
    tm = x_ref.shape[0]
    seg = jnp.where(pl.program_id(0) >= n_lat_tiles, CTX_LEN, GRID_W)
    pos = lax.broadcasted_iota(jnp.int32, (tm, 1), 0) & (seg - 1)
    hb = _modulated(x_ref[...], gain_ref[...], shift_ref[0], scale_ref[0]).astype(BF16)
    u = (jnp.dot(hb, win_ref[:, d:2 * d], preferred_element_type=F32)
         * jnp.dot(hb, win_ref[:, 2 * d:], preferred_element_type=F32))
    prev = jnp.where(pos != 0, pltpu.roll(u, 1, axis=0), 0.0)
    nxt = jnp.where(pos != seg - 1, pltpu.roll(u, tm - 1, axis=0), 0.0)
    cw = cw_ref[...]
    y = jnp.dot(hb, win_ref[:, :d], preferred_element_type=F32) * (
        cw[0:1] * prev + cw[1:2] * u + cw[2:3] * nxt)
    o_ref[...] = x_ref[...] + gate_ref[0] * jnp.dot(y.astype(BF16), wout_ref[...],
                                                    preferred_element_type=F32)


def _shortconv_layer(xs, gain, shift3, scale3, w_in, conv_w, w_out, gate3, n_lat_tiles, blocks_per_batch):
    t, d = xs.shape
    tm = HY_TOK_TILE
    per_tile = tm // DN_BLOCK
    row = lambda i: (i, 0)
    fixed = lambda i: (0, 0)
    grp = lambda i: (_group_of_block(i * per_tile, n_lat_tiles * per_tile, blocks_per_batch), 0, 0)
    return pl.pallas_call(
        functools.partial(_sc_layer_kernel, n_lat_tiles=n_lat_tiles),
        out_shape=jax.ShapeDtypeStruct((t, d), F32),
        grid=(t // tm,),
        in_specs=[pl.BlockSpec((tm, d), row),
                  pl.BlockSpec((1, d), fixed),
                  pl.BlockSpec((1, 1, d), grp),
                  pl.BlockSpec((1, 1, d), grp),
                  pl.BlockSpec((d, 3 * d), fixed),
                  pl.BlockSpec((3, d), fixed),
                  pl.BlockSpec((d, d), fixed),
                  pl.BlockSpec((1, 1, d), grp)],
        out_specs=pl.BlockSpec((tm, d), row),
        compiler_params=pltpu.CompilerParams(
            dimension_semantics=("arbitrary",), vmem_limit_bytes=VMEM_LIMIT_BYTES),
        name="shortconv_layer",
    )(xs, gain.reshape(1, d), shift3, scale3, w_in.astype(BF16), conv_w, w_out.astype(BF16), gate3)


def kernel(x, c, ctx, c_ctx, ada_w, ada_b, norm_mix, norm_ffn, norm_final, dn_w_in, dn_conv, dn_a_log,
           dn_dt_bias, dn_out_norm, dn_w_out, hy_w_in, hy_conv, hy_f_w1, hy_f_b1, hy_f_freq, hy_f_w2,
           hy_f_b2, hy_f_w3, hy_bias, hy_w_out, sc_w_in, sc_conv, sc_w_out, w_router, router_bias,
           moe_w_gate, moe_w_up, moe_w_down):
    d = D_MODEL
    bsz, seq, _ = x.shape
    n_ctx = bsz * CTX_LEN
    n_lat = bsz * seq
    silu_c = jax.nn.silu(c)
    silu_cc = jax.nn.silu(c_ctx)
    hp = lax.Precision.HIGHEST

    xs = jnp.concatenate([x.reshape(n_lat, d), ctx.reshape(n_ctx, d)], axis=0)
    n_lat_blocks, blocks_per_batch = n_lat // DN_BLOCK, seq // DN_BLOCK
    slots = None

    for i in range(DEPTH):
        kind, j = i % N_MIXERS, i // N_MIXERS
        ml = jnp.split(jnp.dot(silu_c, ada_w[i], precision=hp) + ada_b[i], N_MOD, axis=-1)
        mc = jnp.split(jnp.dot(silu_cc, ada_w[i], precision=hp) + ada_b[i], N_MOD, axis=-1)
        mod = [jnp.concatenate([mc[m][None], ml[m]], axis=0)[:, None, :] for m in range(N_MOD)]
        if kind == 0:
            xs = _deltanet_layer(xs, norm_mix[i], mod[0], mod[1], dn_w_in[j], dn_conv[j], dn_a_log[j],
                                 dn_dt_bias[j], dn_out_norm[j], dn_w_out[j], mod[2], n_lat_blocks,
                                 blocks_per_batch)
        elif kind == 1:
            xs = _hyena_layer(xs, norm_mix[i], mod[0], mod[1], hy_w_in[j], hy_conv[j], hy_f_w1[j], hy_f_b1[j],
                              hy_f_freq[j], hy_f_w2[j], hy_f_b2[j], hy_f_w3[j], hy_bias[j], hy_w_out[j],
                              mod[2], bsz, seq)
        else:
            xs = _shortconv_layer(xs, norm_mix[i], mod[0], mod[1], sc_w_in[j], sc_conv[j], sc_w_out[j],
                                  mod[2], n_lat // HY_TOK_TILE, blocks_per_batch)
        xs, slots = _moe_layer(xs, norm_ffn[i], mod[3], mod[4], mod[5], w_router, router_bias,
                               moe_w_gate, moe_w_up, moe_w_down, i, n_lat // MOE_TILE, blocks_per_batch,
                               norm_final if i == DEPTH - 1 else None, slots)
    return xs.reshape(bsz, seq, d)
```

```python
import functools
import math

import numpy as np
import jax
import jax.numpy as jnp
from jax import lax
from jax.experimental import pallas as pl
from jax.experimental.pallas import tpu as pltpu

D_MODEL = 1024
DEPTH = 4
CTX_LEN = 256
GRID_W = 64
N_MIXERS = 3
EPS = 1e-6
N_MOD = 6

DN_HEADS = 8
DN_DK = D_MODEL // DN_HEADS
DN_DV = D_MODEL // DN_HEADS
DN_CHUNK = 64

HY_ORDER = 2
HY_BANDS = 16
HY_TARGET = 1e-2
HY_FAST = 0.3
HY_SLOW = 1.5

N_EXPERTS = 32
N_GROUPS = 8
EXPERTS_PER_GROUP = N_EXPERTS // N_GROUPS
GROUP_SCORE_K = 2
TOP_K = 2
D_EXPERT = 512
MOE_BLOCK = 512

F32 = jnp.float32
BF16 = jnp.bfloat16

ROW_TILE = 512
VMEM_LIMIT_BYTES = 48 * 1024 * 1024


def _mm_kernel(x_ref, w_ref, o_ref):
    o_ref[...] = jnp.dot(x_ref[...].astype(BF16), w_ref[...], preferred_element_type=F32)


def _mm(x, w, tn=None):
    m, k = x.shape
    n = w.shape[1]
    tm = min(ROW_TILE, m)
    tn = n if tn is None else tn
    assert m % tm == 0 and n % tn == 0
    return pl.pallas_call(
        _mm_kernel,
        out_shape=jax.ShapeDtypeStruct((m, n), F32),
        grid=(m // tm, n // tn),
        in_specs=[pl.BlockSpec((tm, k), lambda i, j: (i, 0)),
                  pl.BlockSpec((k, tn), lambda i, j: (0, j))],
        out_specs=pl.BlockSpec((tm, tn), lambda i, j: (i, j)),
        compiler_params=pltpu.CompilerParams(
            dimension_semantics=("arbitrary", "arbitrary"), vmem_limit_bytes=VMEM_LIMIT_BYTES),
        name="dense_mm",
    )(x, w.astype(BF16))


def _expert_kernel(be_ref, x_ref, wg_ref, wu_ref, wd_ref, sw_ref, o_ref):
    del be_ref
    xb = x_ref[...]
    g = jnp.dot(xb, wg_ref[0], preferred_element_type=F32)
    u = jnp.dot(xb, wu_ref[0], preferred_element_type=F32)
    hid = (g * jax.nn.sigmoid(g)) * u
    y = jnp.dot(hid.astype(BF16), wd_ref[0], preferred_element_type=F32)
    o_ref[...] = y * sw_ref[...]


def _expert_ffn(xs, block_expert, slot_w, w_gate, w_up, w_down):
    n_slots, d = xs.shape
    n_blocks = n_slots // MOE_BLOCK
    grid_spec = pltpu.PrefetchScalarGridSpec(
        num_scalar_prefetch=1,
        grid=(n_blocks,),
        in_specs=[
            pl.BlockSpec((MOE_BLOCK, d), lambda i, be: (i, 0)),
            pl.BlockSpec((1, d, D_EXPERT), lambda i, be: (be[i], 0, 0)),
            pl.BlockSpec((1, d, D_EXPERT), lambda i, be: (be[i], 0, 0)),
            pl.BlockSpec((1, D_EXPERT, d), lambda i, be: (be[i], 0, 0)),
            pl.BlockSpec((MOE_BLOCK, 1), lambda i, be: (i, 0)),
        ],
        out_specs=pl.BlockSpec((MOE_BLOCK, d), lambda i, be: (i, 0)),
    )
    return pl.pallas_call(
        _expert_kernel,
        out_shape=jax.ShapeDtypeStruct((n_slots, d), F32),
        grid_spec=grid_spec,
        compiler_params=pltpu.CompilerParams(
            dimension_semantics=("arbitrary",), vmem_limit_bytes=VMEM_LIMIT_BYTES),
        name="expert_ffn",
    )(block_expert, xs, w_gate.astype(BF16), w_up.astype(BF16), w_down.astype(BF16),
      slot_w.reshape(n_slots, 1))


DN_BLOCK = CTX_LEN
DN_HB = DN_HEADS
N_CHUNKS_PER_BLOCK = DN_BLOCK // DN_CHUNK


def _group_of_block(i, n_lat_blocks, blocks_per_batch):
    return jnp.where(i >= n_lat_blocks, 0, 1 + i // blocks_per_batch)


def _modulated(x, gain, shift, scale):
    y = x * lax.rsqrt(jnp.mean(x * x, axis=-1, keepdims=True) + EPS) * gain
    return y * (1 + scale) + shift


def _dn_inproj_kernel(x_ref, gain_ref, shift_ref, scale_ref, w_ref, wab_ref, cw_ref, alog_ref, dtb_ref,
                      q_ref, k_ref, v_ref, z_ref, gate_ref, *, n_lat_blocks):
    i = pl.program_id(0)
    nrow = DN_BLOCK
    d = D_MODEL
    pair = 2 * DN_DK
    seg = jnp.where(i >= n_lat_blocks, CTX_LEN, GRID_W)
    r = lax.broadcasted_iota(jnp.int32, (nrow, 1), 0)
    pos = r & (seg - 1)
    not_first = pos != 0
    not_last = pos != seg - 1
    h = _modulated(x_ref[...], gain_ref[...], shift_ref[0], scale_ref[0])
    hb = h.astype(BF16)
    outs = (q_ref, k_ref, v_ref)
    for part in range(3):
        for hp in range(d // pair):
            col = part * d + hp * pair
            x = jnp.dot(hb, w_ref[:, col:col + pair], preferred_element_type=F32)
            cw = cw_ref[:, col:col + pair]
            xp = jnp.where(not_first, pltpu.roll(x, 1, axis=0), 0.0)
            xn = jnp.where(not_last, pltpu.roll(x, nrow - 1, axis=0), 0.0)
            y = cw[0:1] * xp + cw[1:2] * x + cw[2:3] * xn
            y = y * jax.nn.sigmoid(y)
            for hh in range(2):
                yh = y[:, hh * DN_DK:(hh + 1) * DN_DK]
                if part < 2:
                    yh = yh * lax.rsqrt(jnp.sum(yh * yh, axis=-1, keepdims=True) + EPS)
                if part == 0:
                    yh = yh * DN_DK ** -0.5
                outs[part][:, hp * pair + hh * DN_DK:hp * pair + (hh + 1) * DN_DK] = yh
    for j in range(d // pair):
        z_ref[:, j * pair:(j + 1) * pair] = jnp.dot(hb, w_ref[:, 3 * d + j * pair:3 * d + (j + 1) * pair],
                                                    preferred_element_type=F32).astype(BF16)

    ab = jnp.dot(hb, wab_ref[...], preferred_element_type=F32)
    nd = 2 * DN_HEADS
    a = ab[:, :nd] + dtb_ref[...]
    softplus = jnp.maximum(a, 0.0) + jnp.log(1.0 + jnp.exp(-jnp.abs(a)))
    g = -jnp.exp(alog_ref[...]) * softplus
    beta = jax.nn.sigmoid(ab[:, nd:])
    cpos = r & (DN_CHUNK - 1)
    gp, gs = g, g
    sh = 1
    while sh < DN_CHUNK:
        gp = gp + jnp.where(cpos >= sh, pltpu.roll(gp, sh, axis=0), 0.0)
        gs = gs + jnp.where(cpos < DN_CHUNK - sh, pltpu.roll(gs, nrow - sh, axis=0), 0.0)
        sh *= 2
    colid = lax.broadcasted_iota(jnp.int32, (1, nd), 1)
    gate_ref[:, :nd] = jnp.where(colid < DN_HEADS, gp, gs)
    gate_ref[:, nd:] = beta


def _dn_inproj(xs, gain, shift3, scale3, w_in, conv_w, a_log, dt_bias, n_lat_blocks, blocks_per_batch):
    t, d = xs.shape
    nd = 2 * DN_HEADS
    row = lambda i: (i, 0)
    fixed = lambda i: (0, 0)
    grp = lambda i: (_group_of_block(i, n_lat_blocks, blocks_per_batch), 0, 0)
    return pl.pallas_call(
        functools.partial(_dn_inproj_kernel, n_lat_blocks=n_lat_blocks),
        out_shape=[jax.ShapeDtypeStruct((t, d), F32)] * 3 + [jax.ShapeDtypeStruct((t, d), BF16),
                                                              jax.ShapeDtypeStruct((t, 2 * nd), F32)],
        grid=(t // DN_BLOCK,),
        in_specs=[pl.BlockSpec((DN_BLOCK, d), row),
                  pl.BlockSpec((1, d), fixed),
                  pl.BlockSpec((1, 1, d), grp),
                  pl.BlockSpec((1, 1, d), grp),
                  pl.BlockSpec((d, 4 * d), fixed),
                  pl.BlockSpec((d, 2 * nd), fixed),
                  pl.BlockSpec((3, 3 * d), fixed),
                  pl.BlockSpec((1, nd), fixed),
                  pl.BlockSpec((1, nd), fixed)],
        out_specs=[pl.BlockSpec((DN_BLOCK, d), row)] * 4 + [pl.BlockSpec((DN_BLOCK, 2 * nd), row)],
        compiler_params=pltpu.CompilerParams(
            dimension_semantics=("arbitrary",), vmem_limit_bytes=VMEM_LIMIT_BYTES),
        name="dn_inproj",
    )(xs, gain.reshape(1, d), shift3, scale3, w_in[:, :4 * d].astype(BF16), w_in[:, 4 * d:].astype(BF16), conv_w,
      a_log.reshape(1, nd), dt_bias.reshape(1, nd))


def _dotb(a, b):
    return jnp.dot(a.astype(BF16), b.astype(BF16), preferred_element_type=F32)


def _dotb_nt(a, b):
    return lax.dot_general(a.astype(BF16), b.astype(BF16), (((1,), (1,)), ((), ())),
                           preferred_element_type=F32)


def _dotb_tn(a, b):
    return lax.dot_general(a.astype(BF16), b.astype(BF16), (((0,), (0,)), ((), ())),
                           preferred_element_type=F32)


def _unit_tri_inverses(mats, ii, jj):
    eye = (ii == jj).astype(F32)
    diag8 = (ii >> 3) == (jj >> 3)
    n = [-jnp.where(diag8, a, 0.0) for a in mats]
    n2 = [_dotb(x, x) for x in n]
    m = [eye + x for x in n]
    m = [x + _dotb(x, y) for x, y in zip(m, n2)]
    n4 = [_dotb(x, x) for x in n2]
    m = [x + _dotb(x, y) for x, y in zip(m, n4)]
    sh = 3
    while (1 << sh) < DN_CHUNK:
        off = ((ii >> (sh + 1)) == (jj >> (sh + 1))) & ((ii >> sh) != (jj >> sh))
        cm = [_dotb(jnp.where(off, a, 0.0), x) for a, x in zip(mats, m)]
        m = [x - _dotb(x, y) for x, y in zip(m, cm)]
        sh += 1
    return m


def _dn_scan_kernel(qf_ref, kf_ref, vf_ref, gcf_ref, grf_ref, qb_ref, kb_ref, vb_ref, gcb_ref, grb_ref,
                    of_ref, ob_ref, s_ref):
    @pl.when(pl.program_id(2) == 0)
    def _():
        s_ref[...] = jnp.zeros_like(s_ref)

    c = DN_CHUNK
    ncb = N_CHUNKS_PER_BLOCK
    ii = lax.broadcasted_iota(jnp.int32, (c, c), 0)
    jj = lax.broadcasted_iota(jnp.int32, (c, c), 1)
    incl = (ii >= jj, ii <= jj)
    strict = (ii > jj, ii < jj)
    dirs = ((qf_ref, kf_ref, vf_ref, gcf_ref, grf_ref, of_ref),
            (qb_ref, kb_ref, vb_ref, gcb_ref, grb_ref, ob_ref))
    items = [(d, hh, ci) for d in range(2) for hh in range(DN_HB) for ci in range(ncb)]

    def rows(ci):
        return slice(ci * c, (ci + 1) * c)

    def cols(hh):
        return slice(hh * DN_DK, (hh + 1) * DN_DK)

    q = [dirs[d][0][rows(ci), cols(hh)] for d, hh, ci in items]
    k = [dirs[d][1][rows(ci), cols(hh)] for d, hh, ci in items]
    v = [dirs[d][2][rows(ci), cols(hh)] for d, hh, ci in items]
    gc = [dirs[d][3][hh, rows(ci), d:d + 1] for d, hh, ci in items]
    gr = [dirs[d][4][hh, d:d + 1, rows(ci)] for d, hh, ci in items]
    beta = [dirs[d][3][hh, rows(ci), 2 + d:3 + d] for d, hh, ci in items]

    decay = [jnp.where(incl[it[0]], jnp.exp(jnp.where(incl[it[0]], x - y, 0.0)), 0.0)
             for it, x, y in zip(items, gc, gr)]
    kb = [x * y for x, y in zip(k, beta)]
    a = [_dotb_nt(x, y) * jnp.where(strict[it[0]], z, 0.0) for it, x, y, z in zip(items, kb, k, decay)]
    attn = [_dotb_nt(x, y) * z for x, y, z in zip(q, k, decay)]
    tinv = _unit_tri_inverses(a, ii, jj)
    eg = [jnp.exp(x) for x in gc]
    uw = [_dotb(t, jnp.concatenate([x * y, z * e], axis=-1))
          for t, x, y, z, e in zip(tinv, v, beta, kb, eg)]
    g_last = [x[0:1] if it[0] else x[c - 1:c] for it, x in zip(items, gc)]
    wq = [jnp.concatenate([x[:, DN_DV:], y * e], axis=0) for x, y, e in zip(uw, q, eg)]
    k_dec = [x * jnp.exp(y - z) for x, y, z in zip(k, g_last, gc)]
    s_dec = [jnp.exp(x) for x in g_last]

    chains = [(d, hh) for d in range(2) for hh in range(DN_HB)]
    state = [s_ref[d, hh] for d, hh in chains]
    for step in range(ncb):
        cur = [items.index((d, hh, ncb - 1 - step if d else step)) for d, hh in chains]
        ws = [_dotb(wq[n], s) for n, s in zip(cur, state)]
        v_new = [uw[n][:, :DN_DV] - x[:c] for n, x in zip(cur, ws)]
        o = [x[c:] + _dotb(attn[n], y) for n, x, y in zip(cur, ws, v_new)]
        state = [s * s_dec[n] + _dotb_tn(k_dec[n], y) for n, s, y in zip(cur, state, v_new)]
        for n, x in zip(cur, o):
            d, hh, ci = items[n]
            dirs[d][5][rows(ci), cols(hh)] = x.astype(BF16)
    for (d, hh), s in zip(chains, state):
        s_ref[d, hh] = s


def _dn_scan(q, k, v, gates, n_lat_blocks, blocks_per_batch):
    t, d = q.shape
    bsz = n_lat_blocks // blocks_per_batch
    g4 = gates.reshape(t, 4, DN_HEADS)
    gcol = jnp.transpose(g4, (2, 0, 1))
    grow = jnp.transpose(g4, (2, 1, 0))

    def blk_f(b, s):
        return jnp.where(s == 0, n_lat_blocks + b, b * blocks_per_batch + s - 1)

    def blk_b(b, s):
        return jnp.where(s == 0, n_lat_blocks + b, b * blocks_per_batch + blocks_per_batch - s)

    hw = DN_HB * DN_DK

    def specs(blk):
        return [pl.BlockSpec((DN_BLOCK, hw), lambda b, hg, s: (blk(b, s), hg))] * 3 + [
            pl.BlockSpec((DN_HB, DN_BLOCK, 4), lambda b, hg, s: (hg, blk(b, s), 0)),
            pl.BlockSpec((DN_HB, 4, DN_BLOCK), lambda b, hg, s: (hg, 0, blk(b, s)))]

    return pl.pallas_call(
        _dn_scan_kernel,
        out_shape=[jax.ShapeDtypeStruct((t, d), BF16)] * 2,
        grid=(bsz, DN_HEADS // DN_HB, 1 + blocks_per_batch),
        in_specs=specs(blk_f) + specs(blk_b),
        out_specs=[pl.BlockSpec((DN_BLOCK, hw), lambda b, hg, s: (blk_f(b, s), hg)),
                   pl.BlockSpec((DN_BLOCK, hw), lambda b, hg, s: (blk_b(b, s), hg))],
        scratch_shapes=[pltpu.VMEM((2, DN_HB, DN_DK, DN_DV), F32)],
        compiler_params=pltpu.CompilerParams(
            dimension_semantics=("arbitrary", "arbitrary", "arbitrary"),
            vmem_limit_bytes=VMEM_LIMIT_BYTES),
        name="dn_scan",
    )(q, k, v, gcol, grow, q, k, v, gcol, grow)


def _dn_out_kernel(of_ref, ob_ref, z_ref, on_ref, w_ref, x_ref, gate_ref, o_ref):
    z = z_ref[...].astype(F32)
    zs = z * jax.nn.sigmoid(z)
    parts = []
    for h in range(DN_HEADS):
        cols = slice(h * DN_DV, (h + 1) * DN_DV)
        o = of_ref[:, cols].astype(F32) + ob_ref[:, cols].astype(F32)
        o = o * lax.rsqrt(jnp.mean(o * o, axis=-1, keepdims=True) + EPS)
        parts.append(o)
    y = jnp.concatenate(parts, axis=-1) * on_ref[...] * zs
    o_ref[...] = x_ref[...] + gate_ref[0] * jnp.dot(y.astype(BF16), w_ref[...],
                                                    preferred_element_type=F32)


def _dn_out(o_f, o_b, z, out_norm, w_out, xs, gate3, n_lat_blocks, blocks_per_batch):
    t, d = xs.shape
    row = lambda i: (i, 0)
    fixed = lambda i: (0, 0)
    grp = lambda i: (_group_of_block(i, n_lat_blocks, blocks_per_batch), 0, 0)
    return pl.pallas_call(
        _dn_out_kernel,
        out_shape=jax.ShapeDtypeStruct((t, d), F32),
        grid=(t // DN_BLOCK,),
        in_specs=[pl.BlockSpec((DN_BLOCK, d), row),
                  pl.BlockSpec((DN_BLOCK, d), row),
                  pl.BlockSpec((DN_BLOCK, d), row),
                  pl.BlockSpec((1, d), fixed),
                  pl.BlockSpec((d, d), fixed),
                  pl.BlockSpec((DN_BLOCK, d), row),
                  pl.BlockSpec((1, 1, d), grp)],
        out_specs=pl.BlockSpec((DN_BLOCK, d), row),
        compiler_params=pltpu.CompilerParams(
            dimension_semantics=("arbitrary",), vmem_limit_bytes=VMEM_LIMIT_BYTES),
        name="dn_out",
    )(o_f, o_b, z, jnp.tile(out_norm, DN_HEADS).reshape(1, d), w_out.astype(BF16), xs, gate3)


def _deltanet_layer(xs, gain, shift3, scale3, w_in, conv_w, a_log, dt_bias, out_norm, w_out, gate3,
                    n_lat_blocks, blocks_per_batch):
    q, k, v, z, gates = _dn_inproj(xs, gain, shift3, scale3, w_in, conv_w, a_log, dt_bias, n_lat_blocks,
                                   blocks_per_batch)
    o_f, o_b = _dn_scan(q, k, v, gates, n_lat_blocks, blocks_per_batch)
    return _dn_out(o_f, o_b, z, out_norm, w_out, xs, gate3, n_lat_blocks, blocks_per_batch)


HY_N2 = 256
HY_CB = 16
HY_TOK_TILE = 512
HY_FILT_TILE = 512


def _dft_constants(nr):
    n1, n2 = 2 * nr, HY_N2
    n = n1 * n2
    nk = -(-(nr + 1) // 8) * 8
    keep = np.arange(nk) <= nr
    k1 = np.where(keep, np.arange(nk), 0)
    f1 = np.exp(-2j * np.pi * np.outer(k1, np.arange(nr)) / n1) * keep[:, None]
    twice = np.where((k1 > 0) & (k1 < nr), 2.0, 1.0) * keep
    lhs_fwd = np.concatenate([f1.real, f1.imag], axis=0)
    lhs_inv = np.concatenate([f1.real.T * twice, f1.imag.T * twice], axis=1) / n
    tw = np.exp(-2j * np.pi * np.outer(k1, np.arange(n2)) / n)
    a2 = np.arange(n2)
    f2 = np.exp(-2j * np.pi * np.outer(a2, a2) / n2)
    w_fwd = np.block([[f2.real, f2.imag], [-f2.imag, f2.real]])
    w_inv = np.block([[f2.real, -f2.imag], [f2.imag, f2.real]])
    return (jnp.asarray(lhs_fwd, BF16), jnp.asarray(lhs_inv, BF16), jnp.asarray(tw.real, F32),
            jnp.asarray(tw.imag, F32), jnp.asarray(w_fwd, BF16), jnp.asarray(w_inv, BF16))


def _hy_dft(x3, lhs_fwd, twr, twi, w_fwd):
    n1 = twr.shape[0]
    a = [jnp.dot(lhs_fwd, x3[c].astype(BF16), preferred_element_type=F32) for c in range(x3.shape[0])]
    br = jnp.concatenate([t[:n1] * twr - t[n1:] * twi for t in a], axis=0)
    bi = jnp.concatenate([t[:n1] * twi + t[n1:] * twr for t in a], axis=0)
    b = jnp.concatenate([br, bi], axis=1)
    return jnp.dot(b.astype(BF16), w_fwd, preferred_element_type=F32)


def _hy_idft(p, cb, lhs_inv, twr, twi, w_inv):
    n1, n2 = twr.shape
    c = jnp.dot(p.astype(BF16), w_inv, preferred_element_type=F32)
    out = []
    for ch in range(cb):
        cr = c[ch * n1:(ch + 1) * n1, :n2]
        ci = c[ch * n1:(ch + 1) * n1, n2:]
        d = jnp.concatenate([cr * twr + ci * twi, ci * twr - cr * twi], axis=0)
        out.append(jnp.dot(lhs_inv, d.astype(BF16), preferred_element_type=F32))
    return out


def _hy_spectrum_kernel(hf_ref, hb_ref, lf_ref, twr_ref, twi_ref, wf_ref, o_ref):
    cb, nr, n2 = hf_ref.shape
    first = ((lax.broadcasted_iota(jnp.int32, (nr, n2), 0) == 0)
             & (lax.broadcasted_iota(jnp.int32, (nr, n2), 1) == 0))
    hb = jnp.where(first, 0.0, hb_ref[...])
    consts = (lf_ref[...], twr_ref[...], twi_ref[...], wf_ref[...])
    xf = _hy_dft(hf_ref[...], *consts)
    xb = _hy_dft(hb, *consts)
    o_ref[...] = jnp.concatenate([xf[:, :n2] + xb[:, :n2], xf[:, n2:] - xb[:, n2:]],
                                 axis=1).reshape(o_ref.shape)


def _hy_spectrum(filt, consts):
    d = D_MODEL
    l = filt.shape[1]
    nr = l // HY_N2
    lhs_fwd, _, twr, twi, w_fwd, _ = consts
    n1 = twr.shape[0]
    cpo = d // HY_CB
    fixed2 = lambda o, c: (0, 0)
    return pl.pallas_call(
        _hy_spectrum_kernel,
        out_shape=jax.ShapeDtypeStruct((HY_ORDER * d, n1, 2 * HY_N2), F32),
        grid=(HY_ORDER, cpo),
        in_specs=[pl.BlockSpec((HY_CB, nr, HY_N2), lambda o, c: (2 * o * cpo + c, 0, 0)),
                  pl.BlockSpec((HY_CB, nr, HY_N2), lambda o, c: ((2 * o + 1) * cpo + c, 0, 0)),
                  pl.BlockSpec(lhs_fwd.shape, fixed2),
                  pl.BlockSpec(twr.shape, fixed2),
                  pl.BlockSpec(twi.shape, fixed2),
                  pl.BlockSpec(w_fwd.shape, fixed2)],
        out_specs=pl.BlockSpec((HY_CB, n1, 2 * HY_N2), lambda o, c: (o * cpo + c, 0, 0)),
        compiler_params=pltpu.CompilerParams(
            dimension_semantics=("arbitrary", "arbitrary"), vmem_limit_bytes=VMEM_LIMIT_BYTES),
        name="hy_spectrum",
    )(filt.reshape(-1, nr, HY_N2), filt.reshape(-1, nr, HY_N2), lhs_fwd, twr, twi, w_fwd)


def _hy_conv_kernel(z_ref, g_ref, k_ref, bias_ref, lf_ref, li_ref, twr_ref, twi_ref, wf_ref, wi_ref,
                    o_ref):
    cb, nr, n2 = z_ref.shape
    twr, twi = twr_ref[...], twi_ref[...]
    z = z_ref[...]
    x = _hy_dft(z, lf_ref[...], twr, twi, wf_ref[...])
    kk = k_ref[...].reshape(x.shape)
    xr, xi, kr, ki = x[:, :n2], x[:, n2:], kk[:, :n2], kk[:, n2:]
    p = jnp.concatenate([xr * kr - xi * ki, xr * ki + xi * kr], axis=1)
    conv = _hy_idft(p, cb, li_ref[...], twr, twi, wi_ref[...])
    for c in range(cb):
        o_ref[c] = g_ref[c] * (conv[c] + bias_ref[c] * z[c])


def _hy_conv(z, z_part, gate, gate_part, khat, order, bias, consts, bsz):
    d = D_MODEL
    l = z.shape[1] // bsz
    nr = l // HY_N2
    cpo = d // HY_CB
    lhs_fwd, lhs_inv, twr, twi, w_fwd, w_inv = consts
    n1 = twr.shape[0]
    fixed2 = lambda c, b: (0, 0)
    out = pl.pallas_call(
        _hy_conv_kernel,
        out_shape=jax.ShapeDtypeStruct((d, bsz * nr, HY_N2), F32),
        grid=(cpo, bsz),
        in_specs=[pl.BlockSpec((HY_CB, nr, HY_N2), lambda c, b: (z_part * cpo + c, b, 0)),
                  pl.BlockSpec((HY_CB, nr, HY_N2), lambda c, b: (gate_part * cpo + c, b, 0)),
                  pl.BlockSpec((HY_CB, n1, 2 * HY_N2), lambda c, b: (order * cpo + c, 0, 0)),
                  pl.BlockSpec((HY_CB, 1, 1), lambda c, b: (c, 0, 0)),
                  pl.BlockSpec(lhs_fwd.shape, fixed2),
                  pl.BlockSpec(lhs_inv.shape, fixed2),
                  pl.BlockSpec(twr.shape, fixed2),
                  pl.BlockSpec(twi.shape, fixed2),
                  pl.BlockSpec(w_fwd.shape, fixed2),
                  pl.BlockSpec(w_inv.shape, fixed2)],
        out_specs=pl.BlockSpec((HY_CB, nr, HY_N2), lambda c, b: (c, b, 0)),
        compiler_params=pltpu.CompilerParams(
            dimension_semantics=("arbitrary", "arbitrary"), vmem_limit_bytes=VMEM_LIMIT_BYTES),
        name="hy_conv",
    )(z.reshape(-1, bsz * nr, HY_N2), gate.reshape(-1, bsz * nr, HY_N2), khat,
      bias.reshape(d, 1, 1), lhs_fwd, lhs_inv, twr, twi, w_fwd, w_inv)
    return out.reshape(d, bsz * l)


def _hy_ctx_kernel(z_ref, g_ref, hf_ref, hb_ref, bias_ref, wf_ref, wi_ref, o_ref, *, bsz):
    l = hf_ref.shape[1]
    wf, wi = wf_ref[...], wi_ref[...]
    hb = jnp.where(lax.broadcasted_iota(jnp.int32, (1, l), 1) == 0, 0.0, hb_ref[...])
    kf = jnp.dot(hf_ref[...].astype(BF16), wf, preferred_element_type=F32)
    kb = jnp.dot(hb.astype(BF16), wf, preferred_element_type=F32)
    n = 2 * l
    kr, ki = kf[:, :n] + kb[:, :n], kf[:, n:] - kb[:, n:]
    bias = bias_ref[...]
    for b in range(bsz):
        z = z_ref[:, b * l:(b + 1) * l]
        x = jnp.dot(z.astype(BF16), wf, preferred_element_type=F32)
        xr, xi = x[:, :n], x[:, n:]
        p = jnp.concatenate([xr * kr - xi * ki, xr * ki + xi * kr], axis=1)
        conv = jnp.dot(p.astype(BF16), wi, preferred_element_type=F32)
        o_ref[:, b * l:(b + 1) * l] = g_ref[:, b * l:(b + 1) * l] * (conv + bias * z)


def _hy_ctx(z, z_part, gate, gate_part, filt, order, bias, bsz):
    d = D_MODEL
    l = filt.shape[1]
    n = 2 * l
    ang = 2 * np.pi * np.outer(np.arange(l), np.arange(n)) / n
    w_fwd = jnp.asarray(np.concatenate([np.cos(ang), -np.sin(ang)], axis=1), BF16)
    w_inv = jnp.asarray(np.concatenate([np.cos(ang.T), -np.sin(ang.T)], axis=0) / n, BF16)
    cb = 256
    nblk = d // cb
    fixed = lambda c: (0, 0)
    return pl.pallas_call(
        functools.partial(_hy_ctx_kernel, bsz=bsz),
        out_shape=jax.ShapeDtypeStruct((d, bsz * l), F32),
        grid=(nblk,),
        in_specs=[pl.BlockSpec((cb, bsz * l), lambda c: (z_part * nblk + c, 0)),
                  pl.BlockSpec((cb, bsz * l), lambda c: (gate_part * nblk + c, 0)),
                  pl.BlockSpec((cb, l), lambda c: (2 * order * nblk + c, 0)),
                  pl.BlockSpec((cb, l), lambda c: ((2 * order + 1) * nblk + c, 0)),
                  pl.BlockSpec((cb, 1), lambda c: (c, 0)),
                  pl.BlockSpec(w_fwd.shape, fixed),
                  pl.BlockSpec(w_inv.shape, fixed)],
        out_specs=pl.BlockSpec((cb, bsz * l), lambda c: (c, 0)),
        compiler_params=pltpu.CompilerParams(
            dimension_semantics=("arbitrary",), vmem_limit_bytes=VMEM_LIMIT_BYTES),
        name="hy_ctx_conv",
    )(z, gate, filt, filt, bias.reshape(d, 1), w_fwd, w_inv)


def _hy_filter_kernel(band_ref, w1t_ref, w1c_ref, w1s_ref, b1_ref, fr_ref, w2_ref, b2_ref, w3_ref,
                      delta_ref, o_ref, *, l):
    tl = o_ref.shape[1]
    d = D_MODEL
    hp = lax.Precision.HIGHEST
    pos = (lax.broadcasted_iota(jnp.int32, (1, tl), 1) + pl.program_id(0) * tl).astype(F32)
    t = pos / max(l - 1, 1)
    ang = ((2 * math.pi / l) * pos) * band_ref[...]
    fr = fr_ref[...]
    pre = (w1t_ref[...] * t + jnp.dot(w1c_ref[...], jnp.cos(ang), precision=hp)
           + jnp.dot(w1s_ref[...], -jnp.sin(ang), precision=hp) + b1_ref[...])
    hdn = jnp.sin(fr * pre)
    hdn = jnp.sin(fr * (jnp.dot(w2_ref[...], hdn, precision=hp) + b2_ref[...]))
    window = jnp.exp(-t * delta_ref[...])
    for part in range(2 * HY_ORDER):
        rows = slice(part * d, (part + 1) * d)
        o_ref[rows, :] = jnp.dot(w3_ref[rows, :], hdn.astype(BF16), preferred_element_type=F32) * window


def _hy_filter(l, w1, b1, freq, w2, b2, w3):
    d = D_MODEL
    nb = HY_BANDS
    tl = min(HY_FILT_TILE, l)
    col = lambda v: v.reshape(-1, 1)
    bands = jnp.linspace(1e-4, nb - 1, nb, dtype=F32)
    deltas = jnp.abs(jnp.linspace(math.log(HY_TARGET) / HY_SLOW, math.log(HY_TARGET) / HY_FAST, d, dtype=F32))
    w1t = w1.T
    args = (col(bands), w1t[:, 0:1], w1t[:, 1:1 + nb], w1t[:, 1 + nb:], col(b1), col(freq), w2.T, col(b2),
            w3.T.astype(BF16), col(deltas))
    return pl.pallas_call(
        functools.partial(_hy_filter_kernel, l=l),
        out_shape=jax.ShapeDtypeStruct((2 * HY_ORDER * d, l), F32),
        grid=(l // tl,),
        in_specs=[pl.BlockSpec(a.shape, lambda j: (0, 0)) for a in args],
        out_specs=pl.BlockSpec((2 * HY_ORDER * d, tl), lambda j: (0, j)),
        compiler_params=pltpu.CompilerParams(
            dimension_semantics=("arbitrary",), vmem_limit_bytes=VMEM_LIMIT_BYTES),
        name="hy_filter",
    )(*args)


def _hy_inproj_kernel(x_ref, gain_ref, shift_ref, scale_ref, wt_ref, cw_ref, o_ref, *, seg):
    nch = wt_ref.shape[0]
    tm = x_ref.shape[0]
    hb = _modulated(x_ref[...], gain_ref[...], shift_ref[0], scale_ref[0]).astype(BF16)
    pos = lax.broadcasted_iota(jnp.int32, (1, tm), 1) & (seg - 1)
    not_first = pos != 0
    not_last = pos != seg - 1
    sub = 512
    for j in range(nch // sub):
        rows = slice(j * sub, (j + 1) * sub)
        p = _dotb_nt(wt_ref[rows, :], hb)
        cw = cw_ref[rows, :]
        prev = jnp.where(not_first, pltpu.roll(p, 1, axis=1), 0.0)
        nxt = jnp.where(not_last, pltpu.roll(p, tm - 1, axis=1), 0.0)
        o_ref[rows, :] = cw[:, 0:1] * prev + cw[:, 1:2] * p + cw[:, 2:3] * nxt


def _hy_inproj(xs, gain, shift3, scale3, w_in, conv_w, first_tile, n_tiles, seg, n_lat_blocks,
               blocks_per_batch):
    k = xs.shape[1]
    nch = w_in.shape[1]
    tm = HY_TOK_TILE
    per_tile = tm // DN_BLOCK
    grp = lambda i: (_group_of_block((first_tile + i) * per_tile, n_lat_blocks, blocks_per_batch), 0, 0)
    return pl.pallas_call(
        functools.partial(_hy_inproj_kernel, seg=seg),
        out_shape=jax.ShapeDtypeStruct((nch, n_tiles * tm), F32),
        grid=(n_tiles,),
        in_specs=[pl.BlockSpec((tm, k), lambda i: (first_tile + i, 0)),
                  pl.BlockSpec((1, k), lambda i: (0, 0)),
                  pl.BlockSpec((1, 1, k), grp),
                  pl.BlockSpec((1, 1, k), grp),
                  pl.BlockSpec((nch, k), lambda i: (0, 0)),
                  pl.BlockSpec((nch, 3), lambda i: (0, 0))],
        out_specs=pl.BlockSpec((nch, tm), lambda i: (0, i)),
        compiler_params=pltpu.CompilerParams(
            dimension_semantics=("arbitrary",), vmem_limit_bytes=VMEM_LIMIT_BYTES),
        name="hy_inproj",
    )(xs, gain.reshape(1, k), shift3, scale3, w_in.T.astype(BF16), conv_w.T)


def _hy_out_kernel(zl_ref, zc_ref, w_ref, x_ref, gate_ref, o_ref, *, n_lat_tiles):
    z = jnp.where(pl.program_id(0) >= n_lat_tiles, zc_ref[...], zl_ref[...])
    o_ref[...] = x_ref[...] + gate_ref[0] * _dotb_tn(z, w_ref[...])


def _hy_out(z_lat, z_ctx, w_out, xs, gate3, blocks_per_batch):
    t, d = xs.shape
    tm = HY_TOK_TILE
    n_lat_tiles = z_lat.shape[1] // tm
    per_tile = tm // DN_BLOCK
    grp = lambda i: (_group_of_block(i * per_tile, n_lat_tiles * per_tile, blocks_per_batch), 0, 0)
    return pl.pallas_call(
        functools.partial(_hy_out_kernel, n_lat_tiles=n_lat_tiles),
        out_shape=jax.ShapeDtypeStruct((t, d), F32),
        grid=(t // tm,),
        in_specs=[pl.BlockSpec((d, tm), lambda i: (0, jnp.minimum(i, n_lat_tiles - 1))),
                  pl.BlockSpec((d, tm), lambda i: (0, 0)),
                  pl.BlockSpec((d, d), lambda i: (0, 0)),
                  pl.BlockSpec((tm, d), lambda i: (i, 0)),
                  pl.BlockSpec((1, 1, d), grp)],
        out_specs=pl.BlockSpec((tm, d), lambda i: (i, 0)),
        compiler_params=pltpu.CompilerParams(
            dimension_semantics=("arbitrary",), vmem_limit_bytes=VMEM_LIMIT_BYTES),
        name="hy_out",
    )(z_lat, z_ctx, w_out.astype(BF16), xs, gate3)


def _hyena_layer(xs, gain, shift3, scale3, w_in, conv_w, f_w1, f_b1, f_freq, f_w2, f_b2, f_w3, bias, w_out,
                 gate3, bsz, seq):
    n_lat_tiles = bsz * seq // HY_TOK_TILE
    assert bsz * CTX_LEN == HY_TOK_TILE
    margs = (xs, gain, shift3, scale3, w_in, conv_w)
    blocks = (bsz * seq // DN_BLOCK, seq // DN_BLOCK)
    p_lat = _hy_inproj(*margs, 0, n_lat_tiles, GRID_W, *blocks)
    p_ctx = _hy_inproj(*margs, n_lat_tiles, 1, CTX_LEN, *blocks)
    fargs = (f_w1, f_b1, f_freq, f_w2, f_b2, f_w3)
    consts = _dft_constants(seq // HY_N2)
    khat = _hy_spectrum(_hy_filter(seq, *fargs), consts)
    filt_ctx = _hy_filter(CTX_LEN, *fargs)
    z_lat, z_ctx = p_lat, p_ctx
    for n in range(HY_ORDER):
        z_lat = _hy_conv(z_lat, 0, p_lat, n + 1, khat, n, bias[n], consts, bsz)
        z_ctx = _hy_ctx(z_ctx, 0, p_ctx, n + 1, filt_ctx, n, bias[n], bsz)
    return _hy_out(z_lat, z_ctx, w_out, xs, gate3, seq // DN_BLOCK)


MOE_TILE = 512
PACK = 2


def _route_kernel(x_ref, gain_ref, shift_ref, scale_ref, wr_ref, rb_ref, tri_ref,
                  f_ref, e_ref, w_ref, r_ref, cnt_ref, carry_ref):
    tm, d = x_ref.shape
    ne, epg, ng = N_EXPERTS, EXPERTS_PER_GROUP, N_GROUPS

    @pl.when(pl.program_id(0) == 0)
    def _():
        carry_ref[...] = jnp.zeros_like(carry_ref)

    x = x_ref[...]
    y = x * lax.rsqrt(jnp.mean(x * x, axis=-1, keepdims=True) + EPS) * gain_ref[...]
    f = y * (1 + scale_ref[0]) + shift_ref[0]
    bits = pltpu.bitcast(f.astype(BF16).astype(F32), jnp.uint32)
    half = d // PACK
    f_ref[...] = (bits[:, :half] >> 16) | (bits[:, half:] & jnp.uint32(0xFFFF0000))

    logits = lax.dot_general(wr_ref[...], f, (((1,), (1,)), ((), ())),
                             precision=lax.Precision.HIGHEST, preferred_element_type=F32)
    scores = jax.nn.sigmoid(logits)
    biased = scores + rb_ref[...]
    s = [scores[j * ng:(j + 1) * ng] for j in range(epg)]
    c = [biased[j * ng:(j + 1) * ng] for j in range(epg)]
    hi01, lo01 = jnp.maximum(c[0], c[1]), jnp.minimum(c[0], c[1])
    hi23, lo23 = jnp.maximum(c[2], c[3]), jnp.minimum(c[2], c[3])
    gscore = jnp.maximum(hi01, hi23) + jnp.maximum(jnp.minimum(hi01, hi23), jnp.maximum(lo01, lo23))
    gi = lax.broadcasted_iota(jnp.int32, (ng, tm), 0)
    gmax = jnp.max(gscore, axis=0, keepdims=True)
    grp = jnp.min(jnp.where(gscore == gmax, gi, ng), axis=0, keepdims=True)
    sel = gi == grp
    cv = [jnp.sum(jnp.where(sel, t, 0.0), axis=0, keepdims=True) for t in c]
    sv = [jnp.sum(jnp.where(sel, t, 0.0), axis=0, keepdims=True) for t in s]

    def pick(excluded):
        best = jnp.full((1, tm), -jnp.inf, F32)
        idx = jnp.zeros((1, tm), jnp.int32)
        val = jnp.zeros((1, tm), F32)
        for j in range(epg):
            cand = cv[j] if excluded is None else jnp.where(excluded == j, -jnp.inf, cv[j])
            take = cand > best
            best = jnp.where(take, cand, best)
            idx = jnp.where(take, j, idx)
            val = jnp.where(take, sv[j], val)
        return idx, val

    i1, v1 = pick(None)
    i2, v2 = pick(i1)
    e1 = grp * epg + i1
    e2 = grp * epg + i2
    wsum = v1 + v2
    e_ref[0:1, :] = e1
    e_ref[1:2, :] = e2
    w_ref[0:1, :] = v1 / wsum
    w_ref[1:2, :] = v2 / wsum

    ei = lax.broadcasted_iota(jnp.int32, (ne, tm), 0)
    oh1 = ei == e1
    oh2 = ei == e2
    tri = tri_ref[...]
    pre1 = jnp.dot(oh1.astype(BF16), tri, preferred_element_type=F32)
    pre2 = jnp.dot(oh2.astype(BF16), tri, preferred_element_type=F32)
    tot1 = pre1[:, tm - 1:tm]
    tot2 = pre2[:, tm - 1:tm]
    carry = carry_ref[:, 0:1]
    r1 = jnp.sum(jnp.where(oh1, carry + pre1 - 1.0, 0.0), axis=0, keepdims=True)
    r2 = jnp.sum(jnp.where(oh2, carry + tot1 + pre2 - 1.0, 0.0), axis=0, keepdims=True)
    r_ref[0:1, :] = r1.astype(jnp.int32)
    r_ref[1:2, :] = r2.astype(jnp.int32)
    carry_ref[...] = carry_ref[...] + (tot1 + tot2)
    cnt_ref[...] = carry_ref[...]


def _moe_route(xs, gain, shift3, scale3, w_router, router_bias, n_lat_tiles, blocks_per_batch):
    t, d = xs.shape
    tm = MOE_TILE
    per_tile = tm // DN_BLOCK
    row = lambda i: (i, 0)
    col = lambda i: (0, i)
    fixed = lambda i: (0, 0)
    grp = lambda i: (_group_of_block(i * per_tile, n_lat_tiles * per_tile, blocks_per_batch), 0, 0)
    tri = jnp.asarray(np.triu(np.ones((tm, tm), np.float32)), BF16)
    perm = np.arange(N_EXPERTS).reshape(N_GROUPS, EXPERTS_PER_GROUP).T.reshape(-1)
    return pl.pallas_call(
        _route_kernel,
        out_shape=[jax.ShapeDtypeStruct((t, d // PACK), jnp.uint32),
                   jax.ShapeDtypeStruct((TOP_K, t), jnp.int32),
                   jax.ShapeDtypeStruct((TOP_K, t), F32),
                   jax.ShapeDtypeStruct((TOP_K, t), jnp.int32),
                   jax.ShapeDtypeStruct((N_EXPERTS, 128), F32)],
        grid=(t // tm,),
        in_specs=[pl.BlockSpec((tm, d), row),
                  pl.BlockSpec((1, d), fixed),
                  pl.BlockSpec((1, 1, d), grp),
                  pl.BlockSpec((1, 1, d), grp),
                  pl.BlockSpec((N_EXPERTS, d), fixed),
                  pl.BlockSpec((N_EXPERTS, 1), fixed),
                  pl.BlockSpec((tm, tm), fixed)],
        out_specs=[pl.BlockSpec((tm, d // PACK), row),
                   pl.BlockSpec((TOP_K, tm), col),
                   pl.BlockSpec((TOP_K, tm), col),
                   pl.BlockSpec((TOP_K, tm), col),
                   pl.BlockSpec((N_EXPERTS, 128), fixed)],
        scratch_shapes=[pltpu.VMEM((N_EXPERTS, 128), F32)],
        compiler_params=pltpu.CompilerParams(
            dimension_semantics=("arbitrary",), vmem_limit_bytes=VMEM_LIMIT_BYTES),
        name="moe_route",
    )(xs, gain.reshape(1, d), shift3, scale3, w_router.T[perm], router_bias[perm].reshape(N_EXPERTS, 1), tri)


def _row_copy(src, src_row, dst, dst_row, sem):
    return pltpu.make_async_copy(src.at[pl.ds(src_row, 1)], dst.at[pl.ds(dst_row, 1)], sem)


def _dispatch_kernel(dest_ref, f_ref, xs_in_ref, xs_ref, dest_smem, sem, idx_sem):
    del xs_in_ref
    tm = f_ref.shape[0]
    idx_copy = pltpu.make_async_copy(dest_ref, dest_smem, idx_sem)
    idx_copy.start()
    idx_copy.wait()

    def issue(tok, carry):
        for k in range(TOP_K):
            _row_copy(f_ref, tok, xs_ref, dest_smem[k, tok], sem).start(priority=k)
        return carry

    def drain(tok, carry):
        for k in range(TOP_K):
            _row_copy(f_ref, 0, xs_ref, 0, sem).wait()
        return carry

    lax.fori_loop(0, tm, issue, 0, unroll=8)
    lax.fori_loop(0, tm, drain, 0, unroll=8)


def _dispatch(f_packed, dest, n_slots):
    t, wd = f_packed.shape
    tm = MOE_TILE
    return pl.pallas_call(
        _dispatch_kernel,
        out_shape=jax.ShapeDtypeStruct((n_slots, wd), jnp.uint32),
        grid=(t // tm,),
        in_specs=[pl.BlockSpec((TOP_K, tm), lambda i: (0, i)),
                  pl.BlockSpec((tm, wd), lambda i: (i, 0)),
                  pl.BlockSpec(memory_space=pl.ANY)],
        out_specs=pl.BlockSpec(memory_space=pl.ANY),
        scratch_shapes=[pltpu.SMEM((TOP_K, tm), jnp.int32),
                        pltpu.SemaphoreType.DMA, pltpu.SemaphoreType.DMA],
        input_output_aliases={2: 0},
        compiler_params=pltpu.CompilerParams(
            dimension_semantics=("arbitrary",), vmem_limit_bytes=VMEM_LIMIT_BYTES),
        name="moe_dispatch",
    )(dest, f_packed, jnp.zeros((n_slots, wd), jnp.uint32))


def _experts_kernel(be_ref, na_ref, x_ref, wg_ref, wu_ref, wd_ref, o_ref, wgb_ref, wub_ref, wdb_ref):
    i = pl.program_id(0)
    prev = be_ref[jnp.maximum(i - 1, 0)]

    @pl.when((i == 0) | (be_ref[i] != prev))
    def _():
        wgb_ref[...] = wg_ref[0, 0].astype(BF16)
        wub_ref[...] = wu_ref[0, 0].astype(BF16)
        wdb_ref[...] = wd_ref[0, 0].astype(BF16)

    @pl.when(i < na_ref[0])
    def _():
        packed = x_ref[...]
        lo = pltpu.bitcast(packed << 16, F32)
        hi = pltpu.bitcast(packed & jnp.uint32(0xFFFF0000), F32)
        xb = jnp.concatenate([lo, hi], axis=-1).astype(BF16)
        g = jnp.dot(xb, wgb_ref[...], preferred_element_type=F32)
        u = jnp.dot(xb, wub_ref[...], preferred_element_type=F32)
        hid = (g * jax.nn.sigmoid(g)) * u
        y = jnp.dot(hid.astype(BF16), wdb_ref[...], preferred_element_type=F32)
        ybits = pltpu.bitcast(y.astype(BF16).astype(F32), jnp.uint32)
        half = y.shape[1] // PACK
        o_ref[...] = (ybits[:, :half] >> 16) | (ybits[:, half:] & jnp.uint32(0xFFFF0000))

    @pl.when(i >= na_ref[0])
    def _():
        o_ref[...] = jnp.zeros_like(o_ref)


def _experts(xs_sorted, block_expert, n_active, w_gate, w_up, w_down, layer):
    n_slots, wd = xs_sorted.shape
    d = wd * PACK
    n_blocks = n_slots // MOE_BLOCK
    blk = lambda i, be, na: (jnp.minimum(i, na[0] - 1), 0)
    wsel = lambda i, be, na: (layer, be[jnp.minimum(i, na[0] - 1)], 0, 0)
    grid_spec = pltpu.PrefetchScalarGridSpec(
        num_scalar_prefetch=2,
        grid=(n_blocks,),
        in_specs=[pl.BlockSpec((MOE_BLOCK, wd), blk),
                  pl.BlockSpec((1, 1, d, D_EXPERT), wsel),
                  pl.BlockSpec((1, 1, d, D_EXPERT), wsel),
                  pl.BlockSpec((1, 1, D_EXPERT, d), wsel)],
        out_specs=pl.BlockSpec((MOE_BLOCK, wd), lambda i, be, na: (i, 0)),
        scratch_shapes=[pltpu.VMEM((d, D_EXPERT), BF16), pltpu.VMEM((d, D_EXPERT), BF16),
                        pltpu.VMEM((D_EXPERT, d), BF16)],
    )
    return pl.pallas_call(
        _experts_kernel,
        out_shape=jax.ShapeDtypeStruct((n_slots, wd), jnp.uint32),
        grid_spec=grid_spec,
        compiler_params=pltpu.CompilerParams(
            dimension_semantics=("arbitrary",), vmem_limit_bytes=VMEM_LIMIT_BYTES),
        name="moe_experts",
    )(block_expert, n_active, xs_sorted, w_gate, w_up, w_down)


def _combine_kernel(dest_ref, y_ref, x_ref, w_ref, gate_ref, fin_ref, o_ref, dest_smem, ya_ref, yb_ref, sem,
                    idx_sem, *, final_norm):
    tm = x_ref.shape[0]
    idx_copy = pltpu.make_async_copy(dest_ref, dest_smem, idx_sem)
    idx_copy.start()
    idx_copy.wait()
    bufs = (ya_ref, yb_ref)

    def issue(tok, carry):
        for k in range(TOP_K):
            _row_copy(y_ref, dest_smem[k, tok], bufs[k], tok, sem).start(priority=k)
        return carry

    def drain(tok, carry):
        for k in range(TOP_K):
            _row_copy(y_ref, 0, bufs[k], 0, sem).wait()
        return carry

    lax.fori_loop(0, tm, issue, 0, unroll=8)
    lax.fori_loop(0, tm, drain, 0, unroll=8)
    w = w_ref[...]
    out = x_ref[...] + gate_ref[0] * (w[:, 0:1] * ya_ref[...] + w[:, 1:2] * yb_ref[...])
    if final_norm:
        out = out * lax.rsqrt(jnp.mean(out * out, axis=-1, keepdims=True) + EPS) * fin_ref[...]
    o_ref[...] = out


def _combine(ys, dest, weight_cols, xs, gate3, n_lat_tiles, blocks_per_batch, final_gain=None):
    t, d = xs.shape
    tm = MOE_TILE
    per_tile = tm // DN_BLOCK
    n_tiles = t // tm if final_gain is None else n_lat_tiles
    fin = jnp.ones((1, d), F32) if final_gain is None else final_gain.reshape(1, d)
    row = lambda i: (i, 0)
    grp = lambda i: (_group_of_block(i * per_tile, n_lat_tiles * per_tile, blocks_per_batch), 0, 0)
    return pl.pallas_call(
        functools.partial(_combine_kernel, final_norm=final_gain is not None),
        out_shape=jax.ShapeDtypeStruct((n_tiles * tm, d), F32),
        grid=(n_tiles,),
        in_specs=[pl.BlockSpec((TOP_K, tm), lambda i: (0, i)),
                  pl.BlockSpec(memory_space=pl.ANY),
                  pl.BlockSpec((tm, d), row),
                  pl.BlockSpec((tm, TOP_K), row),
                  pl.BlockSpec((1, 1, d), grp),
                  pl.BlockSpec((1, d), lambda i: (0, 0))],
        out_specs=pl.BlockSpec((tm, d), row),
        scratch_shapes=[pltpu.SMEM((TOP_K, tm), jnp.int32),
                        pltpu.VMEM((tm, d), F32), pltpu.VMEM((tm, d), F32),
                        pltpu.SemaphoreType.DMA, pltpu.SemaphoreType.DMA],
        compiler_params=pltpu.CompilerParams(
            dimension_semantics=("arbitrary",), vmem_limit_bytes=VMEM_LIMIT_BYTES),
        name="moe_combine",
    )(dest, ys, xs, weight_cols, gate3, fin)


def _moe_layer(xs, gain, shift3, scale3, gate3, w_router, router_bias, w_gate, w_up, w_down, layer,
               n_lat_tiles, blocks_per_batch, final_gain=None):
    t = xs.shape[0]
    f_packed, expert, weight, rank, counts = _moe_route(xs, gain, shift3, scale3, w_router, router_bias,
                                                        n_lat_tiles, blocks_per_batch)
    counts = counts[:, 0].astype(jnp.int32)
    padded = (counts + MOE_BLOCK - 1) // MOE_BLOCK * MOE_BLOCK
    pend = jnp.cumsum(padded)
    pstart = pend - padded
    n_blocks = -(-(t * TOP_K) // MOE_BLOCK) + N_EXPERTS
    block_start = jnp.arange(n_blocks, dtype=jnp.int32) * MOE_BLOCK
    block_expert = jnp.minimum(jnp.sum(pend[None, :] <= block_start[:, None], axis=1),
                               N_EXPERTS - 1).astype(jnp.int32)
    n_active = (pend[-1:] // MOE_BLOCK).astype(jnp.int32)
    is_e = expert[..., None] == jnp.arange(N_EXPERTS, dtype=jnp.int32)
    dest = rank + jnp.sum(jnp.where(is_e, pstart, 0), axis=-1)
    xs_sorted = _dispatch(f_packed, dest, n_blocks * MOE_BLOCK)
    ys = _experts(xs_sorted, block_expert, n_active, w_gate, w_up, w_down, layer)
    return _combine(ys, dest, weight.T, xs, gate3, n_lat_tiles, blocks_per_batch, final_gain)


MOE_CHUNK = 8
MOE_LB = 1280
assert MOE_LB >= MOE_TILE * TOP_K + N_EXPERTS * (MOE_CHUNK - 1) and MOE_LB % 128 == 0
N_CHUNKS = MOE_LB // MOE_CHUNK
TAB_W = 256
assert TAB_W >= N_CHUNKS


def _moe_route_kernel(x_ref, gain_ref, shift_ref, scale_ref, wr_ref, rb_ref, tri_ref, lt_ref,
                      f_ref, pos_ref, w_ref, cnt_ref):
    tm, d = x_ref.shape
    ne, epg, ng = N_EXPERTS, EXPERTS_PER_GROUP, N_GROUPS
    f = _modulated(x_ref[...], gain_ref[...], shift_ref[0], scale_ref[0])
    f_ref[...] = f.astype(BF16)

    logits = lax.dot_general(wr_ref[...], f, (((1,), (1,)), ((), ())),
                             precision=lax.Precision.HIGHEST, preferred_element_type=F32)
    scores = jax.nn.sigmoid(logits)
    biased = scores + rb_ref[...]
    s = [scores[j * ng:(j + 1) * ng] for j in range(epg)]
    c = [biased[j * ng:(j + 1) * ng] for j in range(epg)]
    hi01, lo01 = jnp.maximum(c[0], c[1]), jnp.minimum(c[0], c[1])
    hi23, lo23 = jnp.maximum(c[2], c[3]), jnp.minimum(c[2], c[3])
    gscore = jnp.maximum(hi01, hi23) + jnp.maximum(jnp.minimum(hi01, hi23), jnp.maximum(lo01, lo23))
    gi = lax.broadcasted_iota(jnp.int32, (ng, tm), 0)
    gmax = jnp.max(gscore, axis=0, keepdims=True)
    grp = jnp.min(jnp.where(gscore == gmax, gi, ng), axis=0, keepdims=True)
    sel = gi == grp
    cv = [jnp.sum(jnp.where(sel, t, 0.0), axis=0, keepdims=True) for t in c]
    sv = [jnp.sum(jnp.where(sel, t, 0.0), axis=0, keepdims=True) for t in s]

    def pick(excluded):
        best = jnp.full((1, tm), -jnp.inf, F32)
        idx = jnp.zeros((1, tm), jnp.int32)
        val = jnp.zeros((1, tm), F32)
        for j in range(epg):
            cand = cv[j] if excluded is None else jnp.where(excluded == j, -jnp.inf, cv[j])
            take = cand > best
            best = jnp.where(take, cand, best)
            idx = jnp.where(take, j, idx)
            val = jnp.where(take, sv[j], val)
        return idx, val

    i1, v1 = pick(None)
    i2, v2 = pick(i1)
    wsum = v1 + v2
    w_ref[0:1, :] = v1 / wsum
    w_ref[1:2, :] = v2 / wsum

    ei = lax.broadcasted_iota(jnp.int32, (ne, tm), 0)
    oh1 = ei == grp * epg + i1
    oh2 = ei == grp * epg + i2
    tri = tri_ref[...]
    pre1 = jnp.dot(oh1.astype(BF16), tri, preferred_element_type=F32)
    pre2 = jnp.dot(oh2.astype(BF16), tri, preferred_element_type=F32)
    tot1 = pre1[:, tm - 1:tm]
    tot = tot1 + pre2[:, tm - 1:tm]
    seg = jnp.floor((tot + (MOE_CHUNK - 1)) * (1.0 / MOE_CHUNK)) * MOE_CHUNK
    off = jnp.dot(lt_ref[...], jnp.broadcast_to(seg, (ne, 128)).astype(BF16),
                  preferred_element_type=F32)[:, 0:1]
    p1 = jnp.sum(jnp.where(oh1, off + pre1 - 1.0, 0.0), axis=0, keepdims=True)
    p2 = jnp.sum(jnp.where(oh2, off + tot1 + pre2 - 1.0, 0.0), axis=0, keepdims=True)
    pos_ref[0:1, :] = p1.astype(jnp.int32)
    pos_ref[1:2, :] = p2.astype(jnp.int32)
    cnt_ref[0] = jnp.broadcast_to(tot, (ne, 128))


def _moe_route(xs, gain, shift3, scale3, w_router, router_bias, n_lat_tiles, blocks_per_batch):
    t, d = xs.shape
    tm = MOE_TILE
    ne = N_EXPERTS
    per_tile = tm // DN_BLOCK
    row = lambda i: (i, 0)
    col = lambda i: (0, i)
    fixed = lambda i: (0, 0)
    grp = lambda i: (_group_of_block(i * per_tile, n_lat_tiles * per_tile, blocks_per_batch), 0, 0)
    tri = jnp.asarray(np.triu(np.ones((tm, tm), np.float32)), BF16)
    lt = jnp.asarray(np.tril(np.ones((ne, ne), np.float32), -1), BF16)
    perm = np.arange(ne).reshape(N_GROUPS, EXPERTS_PER_GROUP).T.reshape(-1)
    return pl.pallas_call(
        _moe_route_kernel,
        out_shape=[jax.ShapeDtypeStruct((t, d), BF16),
                   jax.ShapeDtypeStruct((TOP_K, t), jnp.int32),
                   jax.ShapeDtypeStruct((TOP_K, t), F32),
                   jax.ShapeDtypeStruct((t // tm, ne, 128), F32)],
        grid=(t // tm,),
        in_specs=[pl.BlockSpec((tm, d), row),
                  pl.BlockSpec((1, d), fixed),
                  pl.BlockSpec((1, 1, d), grp),
                  pl.BlockSpec((1, 1, d), grp),
                  pl.BlockSpec((ne, d), fixed),
                  pl.BlockSpec((ne, 1), fixed),
                  pl.BlockSpec((tm, tm), fixed),
                  pl.BlockSpec((ne, ne), fixed)],
        out_specs=[pl.BlockSpec((tm, d), row),
                   pl.BlockSpec((TOP_K, tm), col),
                   pl.BlockSpec((TOP_K, tm), col),
                   pl.BlockSpec((1, ne, 128), lambda i: (i, 0, 0))],
        compiler_params=pltpu.CompilerParams(
            dimension_semantics=("arbitrary",), vmem_limit_bytes=VMEM_LIMIT_BYTES),
        name="moe_route",
    )(xs, gain.reshape(1, d), shift3, scale3, w_router.T[perm], router_bias[perm].reshape(ne, 1), tri, lt)


def _chunk_row(j):
    return j * MOE_CHUNK if isinstance(j, int) else pl.multiple_of(j * MOE_CHUNK, MOE_CHUNK)


def _chunk_issue(tab_smem, tab_row, make_copy):
    def issue_pair(jj, n):
        for priority in range(2):
            j = jj * 2 + priority
            dst = tab_smem[tab_row, j]

            @pl.when(dst >= 0)
            def _():
                make_copy(j, pl.multiple_of(dst, MOE_CHUNK)).start(priority=priority)

            n = n + (dst >= 0).astype(jnp.int32)
        return n

    return lax.fori_loop(0, N_CHUNKS // 2, issue_pair, jnp.int32(0), unroll=4)


def _chunk_drain(n, make_copy):
    def drain(j, carry):
        make_copy(0, 0).wait()
        return carry

    lax.fori_loop(0, n, drain, 0)


def _moe_dispatch_kernel(tab_ref, pos_ref, f_ref, xs_in_ref, xs_ref, tab_smem, cnt_smem, loc_ref, sem,
                         idx_sem, *, n_steps):
    del xs_in_ref
    tm, d = f_ref.shape
    i = pl.program_id(0)
    slot = i % 2
    idx_copy = pltpu.make_async_copy(tab_ref.at[0], tab_smem, idx_sem)
    idx_copy.start()
    r = lax.broadcasted_iota(jnp.int32, (tm, MOE_LB), 1)
    p = pos_ref[...]
    onehot = ((p[:, 0:1] == r) | (p[:, 1:2] == r)).astype(BF16)
    loc = _dotb_tn(onehot, f_ref[...])
    bits = pltpu.bitcast(loc, jnp.uint32)
    half = d // PACK
    loc_ref[slot] = (bits[:, :half] >> 16) | (bits[:, half:] & jnp.uint32(0xFFFF0000))
    idx_copy.wait()

    def copy_from(buf):
        def make_copy(j, dst):
            return pltpu.make_async_copy(loc_ref.at[buf, pl.ds(_chunk_row(j), MOE_CHUNK)],
                                         xs_ref.at[pl.ds(dst, MOE_CHUNK)], sem.at[buf])
        return make_copy

    n = _chunk_issue(tab_smem, 0, copy_from(slot))
    cnt_smem[slot] = n

    @pl.when(i > 0)
    def _():
        _chunk_drain(cnt_smem[1 - slot], copy_from(1 - slot))

    @pl.when(i == n_steps - 1)
    def _():
        _chunk_drain(n, copy_from(slot))


def _moe_dispatch(f, pos_cols, table, slots):
    t, d = f.shape
    tm = MOE_TILE
    n_slots, wd = slots.shape
    return pl.pallas_call(
        functools.partial(_moe_dispatch_kernel, n_steps=t // tm),
        out_shape=jax.ShapeDtypeStruct((n_slots, wd), jnp.uint32),
        grid=(t // tm,),
        in_specs=[pl.BlockSpec((1, 1, TAB_W), lambda i: (i, 0, 0)),
                  pl.BlockSpec((tm, TOP_K), lambda i: (i, 0)),
                  pl.BlockSpec((tm, d), lambda i: (i, 0)),
                  pl.BlockSpec(memory_space=pl.ANY)],
        out_specs=pl.BlockSpec(memory_space=pl.ANY),
        scratch_shapes=[pltpu.SMEM((1, TAB_W), jnp.int32),
                        pltpu.SMEM((2,), jnp.int32),
                        pltpu.VMEM((2, MOE_LB, wd), jnp.uint32),
                        pltpu.SemaphoreType.DMA((2,)), pltpu.SemaphoreType.DMA],
        input_output_aliases={3: 0},
        compiler_params=pltpu.CompilerParams(
            dimension_semantics=("arbitrary",), vmem_limit_bytes=VMEM_LIMIT_BYTES),
        name="moe_dispatch",
    )(table, pos_cols, f, slots)


def _moe_combine_kernel(tab_ref, nxt_ref, pos_ref, w_ref, y_ref, x_ref, gate_ref, fin_ref, o_ref, tab_smem,
                        cnt_smem, yloc_ref, sem, idx_sem, *, final_norm, n_steps):
    tm = x_ref.shape[0]
    i = pl.program_id(0)
    slot = i % 2

    def fetch(table_block, buf):
        idx_copy = pltpu.make_async_copy(table_block.at[0], tab_smem.at[pl.ds(buf, 1)], idx_sem)
        idx_copy.start()
        idx_copy.wait()
        cnt_smem[buf] = _chunk_issue(tab_smem, buf, copy_into(buf))

    def copy_into(buf):
        def make_copy(j, src):
            return pltpu.make_async_copy(y_ref.at[pl.ds(src, MOE_CHUNK)],
                                         yloc_ref.at[buf, pl.ds(_chunk_row(j), MOE_CHUNK)], sem.at[buf])
        return make_copy

    @pl.when(i == 0)
    def _():
        yloc_ref[...] = jnp.zeros_like(yloc_ref)
        fetch(tab_ref, 0)

    @pl.when(i + 1 < n_steps)
    def _():
        fetch(nxt_ref, 1 - slot)

    _chunk_drain(cnt_smem[slot], copy_into(slot))
    r = lax.broadcasted_iota(jnp.int32, (tm, MOE_LB), 1)
    p = pos_ref[...]
    w = w_ref[...]
    wmat = jnp.where(p[:, 0:1] == r, w[:, 0:1], 0.0) + jnp.where(p[:, 1:2] == r, w[:, 1:2], 0.0)
    packed = yloc_ref[slot]
    y_lo = pltpu.bitcast(packed << 16, F32)
    y_hi = pltpu.bitcast(packed & jnp.uint32(0xFFFF0000), F32)
    moe = jnp.concatenate([_dotb(wmat, y_lo), _dotb(wmat, y_hi)], axis=-1)
    out = x_ref[...] + gate_ref[0] * moe
    if final_norm:
        out = out * lax.rsqrt(jnp.mean(out * out, axis=-1, keepdims=True) + EPS) * fin_ref[...]
    o_ref[...] = out


def _moe_combine(ys, pos_cols, weight_cols, table, xs, gate3, n_lat_tiles, blocks_per_batch, final_gain=None):
    t, d = xs.shape
    tm = MOE_TILE
    per_tile = tm // DN_BLOCK
    n_tiles = t // tm if final_gain is None else n_lat_tiles
    fin = jnp.ones((1, d), F32) if final_gain is None else final_gain.reshape(1, d)
    row = lambda i: (i, 0)
    grp = lambda i: (_group_of_block(i * per_tile, n_lat_tiles * per_tile, blocks_per_batch), 0, 0)
    last = table.shape[0] - 1
    return pl.pallas_call(
        functools.partial(_moe_combine_kernel, final_norm=final_gain is not None, n_steps=n_tiles),
        out_shape=jax.ShapeDtypeStruct((n_tiles * tm, d), F32),
        grid=(n_tiles,),
        in_specs=[pl.BlockSpec((1, 1, TAB_W), lambda i: (i, 0, 0)),
                  pl.BlockSpec((1, 1, TAB_W), lambda i: (jnp.minimum(i + 1, last), 0, 0)),
                  pl.BlockSpec((tm, TOP_K), row),
                  pl.BlockSpec((tm, TOP_K), row),
                  pl.BlockSpec(memory_space=pl.ANY),
                  pl.BlockSpec((tm, d), row),
                  pl.BlockSpec((1, 1, d), grp),
                  pl.BlockSpec((1, d), lambda i: (0, 0))],
        out_specs=pl.BlockSpec((tm, d), row),
        scratch_shapes=[pltpu.SMEM((2, TAB_W), jnp.int32),
                        pltpu.SMEM((2,), jnp.int32),
                        pltpu.VMEM((2, MOE_LB, d // PACK), jnp.uint32),
                        pltpu.SemaphoreType.DMA((2,)), pltpu.SemaphoreType.DMA],
        compiler_params=pltpu.CompilerParams(
            dimension_semantics=("arbitrary",), vmem_limit_bytes=VMEM_LIMIT_BYTES),
        name="moe_combine",
    )(table, table, pos_cols, weight_cols, ys, xs, gate3, fin)


def _moe_layer(xs, gain, shift3, scale3, gate3, w_router, router_bias, w_gate, w_up, w_down, layer,
               n_lat_tiles, blocks_per_batch, final_gain=None, slots=None):
    t = xs.shape[0]
    n_tiles = t // MOE_TILE
    f, pos, weight, cnt = _moe_route(xs, gain, shift3, scale3, w_router, router_bias,
                                     n_lat_tiles, blocks_per_batch)
    seg = (cnt[:, :, 0].astype(jnp.int32) + MOE_CHUNK - 1) // MOE_CHUNK * MOE_CHUNK
    loc_end = jnp.cumsum(seg, axis=1)
    loc_off = loc_end - seg
    padded = (jnp.sum(seg, axis=0) + MOE_BLOCK - 1) // MOE_BLOCK * MOE_BLOCK
    pend = jnp.cumsum(padded)
    seg_start = (pend - padded)[None, :] + jnp.cumsum(seg, axis=0) - seg
    n_blocks = -(-(t * TOP_K + n_tiles * N_EXPERTS * (MOE_CHUNK - 1)) // MOE_BLOCK) + N_EXPERTS
    block_start = jnp.arange(n_blocks, dtype=jnp.int32) * MOE_BLOCK
    block_expert = jnp.minimum(jnp.sum(pend[None, :] <= block_start[:, None], axis=1),
                               N_EXPERTS - 1).astype(jnp.int32)
    n_active = (pend[-1:] // MOE_BLOCK).astype(jnp.int32)
    row0 = jnp.arange(N_CHUNKS, dtype=jnp.int32) * MOE_CHUNK
    e_of = jnp.sum(loc_end[:, None, :] <= row0[None, :, None], axis=-1)
    is_e = e_of[..., None] == jnp.arange(N_EXPERTS, dtype=jnp.int32)
    shift = jnp.sum(jnp.where(is_e, (seg_start - loc_off)[:, None, :], 0), axis=-1)
    table = jnp.where(e_of < N_EXPERTS, row0[None, :] + shift, -1)
    table = jnp.pad(table, ((0, 0), (0, TAB_W - N_CHUNKS)), constant_values=-1).reshape(n_tiles, 1, TAB_W)

    pos_cols = pos.T
    if slots is None:
        slots = jnp.zeros((n_blocks * MOE_BLOCK, f.shape[1] // PACK), jnp.uint32)
    xs_sorted = _moe_dispatch(f, pos_cols, table, slots)
    ys = _experts(xs_sorted, block_expert, n_active, w_gate, w_up, w_down, layer)
    out = _moe_combine(ys, pos_cols, weight.T, table, xs, gate3, n_lat_tiles, blocks_per_batch, final_gain)
    return out, xs_sorted


def _rmsnorm(x, gain):
    y = x * lax.rsqrt(jnp.mean(x * x, axis=-1, keepdims=True) + EPS)
    return y * gain


def _modulate(x, gain, shift, scale):
    return _rmsnorm(x, gain) * (1 + scale) + shift


def _l2norm(t):
    return t * lax.rsqrt(jnp.sum(t * t, axis=-1, keepdims=True) + EPS)


def _short_conv(x, w, on_grid):
    b, l, ch = x.shape
    xs = x.reshape(b, l // GRID_W, GRID_W, ch) if on_grid else x.reshape(b, 1, l, ch)
    n = xs.shape[2]
    xp = jnp.pad(xs, ((0, 0), (0, 0), (1, 1), (0, 0)))
    y = w[0] * xp[:, :, 0:n] + w[1] * xp[:, :, 1:n + 1] + w[2] * xp[:, :, 2:n + 2]
    return y.reshape(b, l, ch)


def _gated_delta_chunked(q, k, v, g, beta, s0):
    b, h, l, dk = q.shape
    dv = v.shape[-1]
    c = DN_CHUNK
    n = l // c
    q = q.reshape(b, h, n, c, dk)
    k = k.reshape(b, h, n, c, dk)
    v = v.reshape(b, h, n, c, dv)
    g = jnp.cumsum(g.reshape(b, h, n, c), axis=-1)
    beta = beta.reshape(b, h, n, c, 1)
    pos = jnp.arange(c)
    incl = pos[:, None] >= pos[None, :]
    strict = pos[:, None] > pos[None, :]
    decay = jnp.exp(jnp.where(incl, g[..., :, None] - g[..., None, :], -jnp.inf))
    kb = k * beta
    a_mat = jnp.einsum('bhnid,bhnjd->bhnij', kb, k) * jnp.where(strict, decay, 0.0)
    rhs = jnp.concatenate([v * beta, kb * jnp.exp(g)[..., None]], axis=-1)
    sol = lax.linalg.triangular_solve(a_mat + jnp.eye(c, dtype=a_mat.dtype), rhs,
                                      left_side=True, lower=True, unit_diagonal=True)
    u, w = sol[..., :dv], sol[..., dv:]
    attn = jnp.einsum('bhnid,bhnjd->bhnij', q, k) * decay
    g_last = g[..., -1:]
    q_dec = q * jnp.exp(g)[..., None]
    k_dec = k * jnp.exp(g_last - g)[..., None]

    def step(s, inp):
        qd, kd, uu, ww, at, gl = inp
        v_new = uu - jnp.einsum('bhck,bhkv->bhcv', ww, s)
        o = jnp.einsum('bhck,bhkv->bhcv', qd, s) + jnp.einsum('bhcs,bhsv->bhcv', at, v_new)
        s = s * jnp.exp(gl)[..., None] + jnp.einsum('bhck,bhcv->bhkv', kd, v_new)
        return s, o

    xs = tuple(jnp.moveaxis(t, 2, 0) for t in (q_dec, k_dec, u, w, attn, g_last))
    s_final, o = lax.scan(step, s0, xs)
    o = jnp.moveaxis(o, 0, 2).reshape(b, h, l, dv)
    return o, s_final


def _deltanet_mixer(p_ctx, p_lat, conv_w, a_log, dt_bias, out_norm):
    d = D_MODEL
    nh = DN_HEADS

    def project(p, on_grid):
        b, l, _ = p.shape
        qkv = jax.nn.silu(_short_conv(p[..., :3 * d], conv_w, on_grid))
        z = p[..., 3 * d:4 * d]
        a = p[..., 4 * d:4 * d + 2 * nh].reshape(b, l, 2, nh)
        bb = p[..., 4 * d + 2 * nh:].reshape(b, l, 2, nh)

        def heads(t):
            return jnp.transpose(t.reshape(b, l, nh, -1), (0, 2, 1, 3))

        q, k, v = (heads(t) for t in jnp.split(qkv, 3, axis=-1))
        q = _l2norm(q) * DN_DK ** -0.5
        k = _l2norm(k)
        g = -jnp.exp(a_log) * jax.nn.softplus(a + dt_bias)
        g = jnp.transpose(g, (2, 0, 3, 1))
        beta = jnp.transpose(jax.nn.sigmoid(bb), (2, 0, 3, 1))
        return q, k, v, g, beta, z

    def scan_both(q, k, v, g, beta, s_f, s_b):
        o_f, s_f = _gated_delta_chunked(q, k, v, g[0], beta[0], s_f)
        rev = lambda t: jnp.flip(t, axis=2)
        o_b, s_b = _gated_delta_chunked(rev(q), rev(k), rev(v), rev(g[1]), rev(beta[1]), s_b)
        return o_f + rev(o_b), s_f, s_b

    def finish(o, z):
        b, _, l, _ = o.shape
        o = jnp.transpose(o, (0, 2, 1, 3))
        o = o * lax.rsqrt(jnp.mean(o * o, axis=-1, keepdims=True) + EPS) * out_norm
        o = o * jax.nn.silu(z.reshape(b, l, nh, DN_DV))
        return o.reshape(b, l, d)

    qc, kc, vc, gc, bc, zc = project(p_ctx, False)
    s0 = jnp.zeros((p_ctx.shape[0], nh, DN_DK, DN_DV), F32)
    o_c, s_f, s_b = scan_both(qc, kc, vc, gc, bc, s0, s0)
    ql, kl, vl, gla, bl, zl = project(p_lat, True)
    o_l, _, _ = scan_both(ql, kl, vl, gla, bl, s_f, s_b)
    return finish(o_c, zc), finish(o_l, zl)


def _hyena_filters(l, w1, b1, freq, w2, b2, w3):
    pos = jnp.arange(l, dtype=F32)[:, None]
    t = pos / max(l - 1, 1)
    bands = jnp.linspace(1e-4, HY_BANDS - 1, HY_BANDS, dtype=F32)[None, :]
    ang = (2 * math.pi / l) * pos * bands
    feat = jnp.concatenate([t, jnp.cos(ang), -jnp.sin(ang)], axis=-1)
    hp = lax.Precision.HIGHEST
    hdn = jnp.sin(freq * (jnp.dot(feat, w1, precision=hp) + b1))
    hdn = jnp.sin(freq * (jnp.dot(hdn, w2, precision=hp) + b2))
    filt = jnp.dot(hdn, w3, precision=hp).reshape(l, HY_ORDER, 2, D_MODEL)
    deltas = jnp.abs(jnp.linspace(math.log(HY_TARGET) / HY_SLOW, math.log(HY_TARGET) / HY_FAST,
                                  D_MODEL, dtype=F32))
    window = jnp.exp(-t * deltas[None, :])
    return filt * window[:, None, None, :]


def _two_sided_fftconv(u, h_fwd, h_bwd):
    l = u.shape[1]
    k = jnp.concatenate([h_fwd, jnp.zeros_like(h_fwd[:1]), jnp.flip(h_bwd[1:], axis=0)], axis=0)
    kf = jnp.fft.rfft(k, axis=0)
    uf = jnp.fft.rfft(u, n=2 * l, axis=1)
    return jnp.fft.irfft(uf * kf[None], n=2 * l, axis=1)[:, :l]


def _hyena_stream(p, on_grid, conv_w, f_w1, f_b1, f_freq, f_w2, f_b2, f_w3, bias):
    l = p.shape[1]
    p = _short_conv(p, conv_w, on_grid)
    v, x1, x2 = jnp.split(p, 3, axis=-1)
    filt = _hyena_filters(l, f_w1, f_b1, f_freq, f_w2, f_b2, f_w3)
    z = v
    for n, gate in enumerate((x1, x2)):
        conv = _two_sided_fftconv(z, filt[:, n, 0], filt[:, n, 1])
        z = gate * (conv + bias[n] * z)
    return z


def _shortconv_stream(p, on_grid, conv_w):
    bg, cg, xin = jnp.split(p, 3, axis=-1)
    return bg * _short_conv(cg * xin, conv_w, on_grid)


def _route(h, w_router, router_bias):
    t = h.shape[0]
    scores = jax.nn.sigmoid(jnp.dot(h, w_router, precision=lax.Precision.HIGHEST))
    choice = (scores + router_bias).reshape(t, N_GROUPS, EXPERTS_PER_GROUP)
    group_score = lax.top_k(choice, GROUP_SCORE_K)[0].sum(-1)
    group = jnp.argmax(group_score, axis=-1)
    in_group = jnp.take_along_axis(choice, group[:, None, None], axis=1)[:, 0]
    local = lax.top_k(in_group, TOP_K)[1]
    expert = group[:, None] * EXPERTS_PER_GROUP + local
    weight = jnp.take_along_axis(scores, expert, axis=1)
    weight = weight / jnp.sum(weight, axis=-1, keepdims=True)
    return expert.astype(jnp.int32), weight


def _moe_ffn(x, w_router, router_bias, w_gate, w_up, w_down):
    t, d = x.shape
    expert, weight = _route(x, w_router, router_bias)
    a = t * TOP_K
    e_flat = expert.reshape(-1)
    order = jnp.argsort(e_flat)
    e_sorted = e_flat[order]
    tok_sorted = (order // TOP_K).astype(jnp.int32)
    counts = jnp.zeros((N_EXPERTS,), jnp.int32).at[e_flat].add(1)
    start = jnp.cumsum(counts) - counts
    padded = (counts + MOE_BLOCK - 1) // MOE_BLOCK * MOE_BLOCK
    pend = jnp.cumsum(padded)
    pstart = pend - padded
    dest = pstart[e_sorted] + (jnp.arange(a, dtype=jnp.int32) - start[e_sorted])
    n_blocks = -(-a // MOE_BLOCK) + N_EXPERTS
    n_slots = n_blocks * MOE_BLOCK
    slot_tok = jnp.full((n_slots,), t, jnp.int32).at[dest].set(tok_sorted)
    block_start = jnp.arange(n_blocks, dtype=jnp.int32) * MOE_BLOCK
    block_expert = jnp.minimum(jnp.searchsorted(pend, block_start, side='right'),
                               N_EXPERTS - 1).astype(jnp.int32)
    x_pad = jnp.concatenate([x.astype(BF16), jnp.zeros((1, d), BF16)], axis=0)
    xs = x_pad[slot_tok]
    ys = _expert_ffn(xs, block_expert, jnp.ones((n_slots,), F32), w_gate, w_up, w_down)
    slot_of = jnp.zeros((a,), jnp.int32).at[order].set(dest).reshape(t, TOP_K)
    out = weight[:, 0:1] * ys[slot_of[:, 0]] + weight[:, 1:2] * ys[slot_of[:, 1]]
    return out


def _sc_layer_kernel(x_ref, gain_ref, shift_ref, scale_ref, win_ref, cw_ref, wout_ref, gate_ref, o_ref, *,
                     n_lat_tiles):
    d = D_MODEL
    tm = x_ref.shape[0]
    seg = jnp.where(pl.program_id(0) >= n_lat_tiles, CTX_LEN, GRID_W)
    pos = lax.broadcasted_iota(jnp.int32, (tm, 1), 0) & (seg - 1)
    hb = _modulated(x_ref[...], gain_ref[...], shift_ref[0], scale_ref[0]).astype(BF16)
    u = (jnp.dot(hb, win_ref[:, d:2 * d], preferred_element_type=F32)
         * jnp.dot(hb, win_ref[:, 2 * d:], preferred_element_type=F32))
    prev = jnp.where(pos != 0, pltpu.roll(u, 1, axis=0), 0.0)
    nxt = jnp.where(pos != seg - 1, pltpu.roll(u, tm - 1, axis=0), 0.0)
    cw = cw_ref[...]
    y = jnp.dot(hb, win_ref[:, :d], preferred_element_type=F32) * (
        cw[0:1] * prev + cw[1:2] * u + cw[2:3] * nxt)
    o_ref[...] = x_ref[...] + gate_ref[0] * jnp.dot(y.astype(BF16), wout_ref[...],
                                                    preferred_element_type=F32)


def _shortconv_layer(xs, gain, shift3, scale3, w_in, conv_w, w_out, gate3, n_lat_tiles, blocks_per_batch):
    t, d = xs.shape
    tm = HY_TOK_TILE
    per_tile = tm // DN_BLOCK
    row = lambda i: (i, 0)
    fixed = lambda i: (0, 0)
    grp = lambda i: (_group_of_block(i * per_tile, n_lat_tiles * per_tile, blocks_per_batch), 0, 0)
    return pl.pallas_call(
        functools.partial(_sc_layer_kernel, n_lat_tiles=n_lat_tiles),
        out_shape=jax.ShapeDtypeStruct((t, d), F32),
        grid=(t // tm,),
        in_specs=[pl.BlockSpec((tm, d), row),
                  pl.BlockSpec((1, d), fixed),
                  pl.BlockSpec((1, 1, d), grp),
                  pl.BlockSpec((1, 1, d), grp),
                  pl.BlockSpec((d, 3 * d), fixed),
                  pl.BlockSpec((3, d), fixed),
                  pl.BlockSpec((d, d), fixed),
                  pl.BlockSpec((1, 1, d), grp)],
        out_specs=pl.BlockSpec((tm, d), row),
        compiler_params=pltpu.CompilerParams(
            dimension_semantics=("arbitrary",), vmem_limit_bytes=VMEM_LIMIT_BYTES),
        name="shortconv_layer",
    )(xs, gain.reshape(1, d), shift3, scale3, w_in.astype(BF16), conv_w, w_out.astype(BF16), gate3)


def kernel(x, c, ctx, c_ctx, ada_w, ada_b, norm_mix, norm_ffn, norm_final, dn_w_in, dn_conv, dn_a_log,
           dn_dt_bias, dn_out_norm, dn_w_out, hy_w_in, hy_conv, hy_f_w1, hy_f_b1, hy_f_freq, hy_f_w2,
           hy_f_b2, hy_f_w3, hy_bias, hy_w_out, sc_w_in, sc_conv, sc_w_out, w_router, router_bias,
           moe_w_gate, moe_w_up, moe_w_down):
    d = D_MODEL
    bsz, seq, _ = x.shape
    n_ctx = bsz * CTX_LEN
    n_lat = bsz * seq
    silu_c = jax.nn.silu(c)
    silu_cc = jax.nn.silu(c_ctx)
    hp = lax.Precision.HIGHEST

    xs = jnp.concatenate([x.reshape(n_lat, d), ctx.reshape(n_ctx, d)], axis=0)
    n_lat_blocks, blocks_per_batch = n_lat // DN_BLOCK, seq // DN_BLOCK
    slots = None

    for i in range(DEPTH):
        kind, j = i % N_MIXERS, i // N_MIXERS
        ml = jnp.split(jnp.dot(silu_c, ada_w[i], precision=hp) + ada_b[i], N_MOD, axis=-1)
        mc = jnp.split(jnp.dot(silu_cc, ada_w[i], precision=hp) + ada_b[i], N_MOD, axis=-1)
        mod = [jnp.concatenate([mc[m][None], ml[m]], axis=0)[:, None, :] for m in range(N_MOD)]
        if kind == 0:
            xs = _deltanet_layer(xs, norm_mix[i], mod[0], mod[1], dn_w_in[j], dn_conv[j], dn_a_log[j],
                                 dn_dt_bias[j], dn_out_norm[j], dn_w_out[j], mod[2], n_lat_blocks,
                                 blocks_per_batch)
        elif kind == 1:
            xs = _hyena_layer(xs, norm_mix[i], mod[0], mod[1], hy_w_in[j], hy_conv[j], hy_f_w1[j], hy_f_b1[j],
                              hy_f_freq[j], hy_f_w2[j], hy_f_b2[j], hy_f_w3[j], hy_bias[j], hy_w_out[j],
                              mod[2], bsz, seq)
        else:
            xs = _shortconv_layer(xs, norm_mix[i], mod[0], mod[1], sc_w_in[j], sc_conv[j], sc_w_out[j],
                                  mod[2], n_lat // HY_TOK_TILE, blocks_per_batch)
        xs, slots = _moe_layer(xs, norm_ffn[i], mod[3], mod[4], mod[5], w_router, router_bias,
                               moe_w_gate, moe_w_up, moe_w_down, i, n_lat // MOE_TILE, blocks_per_batch,
                               norm_final if i == DEPTH - 1 else None, slots)
    return xs.reshape(bsz, seq, d)
```

```python
import functools
import math

import numpy as np
import jax
import jax.numpy as jnp
from jax import lax
from jax.experimental import pallas as pl
from jax.experimental.pallas import tpu as pltpu

D_MODEL = 1024
DEPTH = 4
CTX_LEN = 256
GRID_W = 64
N_MIXERS = 3
EPS = 1e-6
N_MOD = 6

DN_HEADS = 8
DN_DK = D_MODEL // DN_HEADS
DN_DV = D_MODEL // DN_HEADS
DN_CHUNK = 64

HY_ORDER = 2
HY_BANDS = 16
HY_TARGET = 1e-2
HY_FAST = 0.3
HY_SLOW = 1.5

N_EXPERTS = 32
N_GROUPS = 8
EXPERTS_PER_GROUP = N_EXPERTS // N_GROUPS
GROUP_SCORE_K = 2
TOP_K = 2
D_EXPERT = 512
MOE_BLOCK = 512

F32 = jnp.float32
BF16 = jnp.bfloat16

ROW_TILE = 512
VMEM_LIMIT_BYTES = 48 * 1024 * 1024


def _mm_kernel(x_ref, w_ref, o_ref):
    o_ref[...] = jnp.dot(x_ref[...].astype(BF16), w_ref[...], preferred_element_type=F32)


def _mm(x, w, tn=None):
    m, k = x.shape
    n = w.shape[1]
    tm = min(ROW_TILE, m)
    tn = n if tn is None else tn
    assert m % tm == 0 and n % tn == 0
    return pl.pallas_call(
        _mm_kernel,
        out_shape=jax.ShapeDtypeStruct((m, n), F32),
        grid=(m // tm, n // tn),
        in_specs=[pl.BlockSpec((tm, k), lambda i, j: (i, 0)),
                  pl.BlockSpec((k, tn), lambda i, j: (0, j))],
        out_specs=pl.BlockSpec((tm, tn), lambda i, j: (i, j)),
        compiler_params=pltpu.CompilerParams(
            dimension_semantics=("arbitrary", "arbitrary"), vmem_limit_bytes=VMEM_LIMIT_BYTES),
        name="dense_mm",
    )(x, w.astype(BF16))


def _expert_kernel(be_ref, x_ref, wg_ref, wu_ref, wd_ref, sw_ref, o_ref):
    del be_ref
    xb = x_ref[...]
    g = jnp.dot(xb, wg_ref[0], preferred_element_type=F32)
    u = jnp.dot(xb, wu_ref[0], preferred_element_type=F32)
    hid = (g * jax.nn.sigmoid(g)) * u
    y = jnp.dot(hid.astype(BF16), wd_ref[0], preferred_element_type=F32)
    o_ref[...] = y * sw_ref[...]


def _expert_ffn(xs, block_expert, slot_w, w_gate, w_up, w_down):
    n_slots, d = xs.shape
    n_blocks = n_slots // MOE_BLOCK
    grid_spec = pltpu.PrefetchScalarGridSpec(
        num_scalar_prefetch=1,
        grid=(n_blocks,),
        in_specs=[
            pl.BlockSpec((MOE_BLOCK, d), lambda i, be: (i, 0)),
            pl.BlockSpec((1, d, D_EXPERT), lambda i, be: (be[i], 0, 0)),
            pl.BlockSpec((1, d, D_EXPERT), lambda i, be: (be[i], 0, 0)),
            pl.BlockSpec((1, D_EXPERT, d), lambda i, be: (be[i], 0, 0)),
            pl.BlockSpec((MOE_BLOCK, 1), lambda i, be: (i, 0)),
        ],
        out_specs=pl.BlockSpec((MOE_BLOCK, d), lambda i, be: (i, 0)),
    )
    return pl.pallas_call(
        _expert_kernel,
        out_shape=jax.ShapeDtypeStruct((n_slots, d), F32),
        grid_spec=grid_spec,
        compiler_params=pltpu.CompilerParams(
            dimension_semantics=("arbitrary",), vmem_limit_bytes=VMEM_LIMIT_BYTES),
        name="expert_ffn",
    )(block_expert, xs, w_gate.astype(BF16), w_up.astype(BF16), w_down.astype(BF16),
      slot_w.reshape(n_slots, 1))


DN_BLOCK = CTX_LEN
DN_HB = DN_HEADS
N_CHUNKS_PER_BLOCK = DN_BLOCK // DN_CHUNK


def _group_of_block(i, n_lat_blocks, blocks_per_batch):
    return jnp.where(i >= n_lat_blocks, 0, 1 + i // blocks_per_batch)


def _modulated(x, gain, shift, scale):
    y = x * lax.rsqrt(jnp.mean(x * x, axis=-1, keepdims=True) + EPS) * gain
    return y * (1 + scale) + shift


def _dn_inproj_kernel(x_ref, gain_ref, shift_ref, scale_ref, w_ref, wab_ref, cw_ref, alog_ref, dtb_ref,
                      q_ref, k_ref, v_ref, z_ref, gate_ref, gate_t_ref, *, n_lat_blocks):
    i = pl.program_id(0)
    nrow = DN_BLOCK
    d = D_MODEL
    pair = 2 * DN_DK
    seg = jnp.where(i >= n_lat_blocks, CTX_LEN, GRID_W)
    r = lax.broadcasted_iota(jnp.int32, (nrow, 1), 0)
    pos = r & (seg - 1)
    not_first = pos != 0
    not_last = pos != seg - 1
    h = _modulated(x_ref[...], gain_ref[...], shift_ref[0], scale_ref[0])
    hb = h.astype(BF16)
    outs = (q_ref, k_ref, v_ref)
    for part in range(3):
        for hp in range(d // pair):
            col = part * d + hp * pair
            x = jnp.dot(hb, w_ref[:, col:col + pair], preferred_element_type=F32)
            cw = cw_ref[:, col:col + pair]
            xp = jnp.where(not_first, pltpu.roll(x, 1, axis=0), 0.0)
            xn = jnp.where(not_last, pltpu.roll(x, nrow - 1, axis=0), 0.0)
            y = cw[0:1] * xp + cw[1:2] * x + cw[2:3] * xn
            y = y * jax.nn.sigmoid(y)
            for hh in range(2):
                yh = y[:, hh * DN_DK:(hh + 1) * DN_DK]
                if part < 2:
                    yh = yh * lax.rsqrt(jnp.sum(yh * yh, axis=-1, keepdims=True) + EPS)
                if part == 0:
                    yh = yh * DN_DK ** -0.5
                outs[part][:, hp * pair + hh * DN_DK:hp * pair + (hh + 1) * DN_DK] = yh
    for j in range(d // pair):
        z_ref[:, j * pair:(j + 1) * pair] = jnp.dot(hb, w_ref[:, 3 * d + j * pair:3 * d + (j + 1) * pair],
                                                    preferred_element_type=F32).astype(BF16)

    ab = jnp.dot(hb, wab_ref[...], preferred_element_type=F32)
    nd = 2 * DN_HEADS
    a = ab[:, :nd] + dtb_ref[...]
    softplus = jnp.maximum(a, 0.0) + jnp.log(1.0 + jnp.exp(-jnp.abs(a)))
    g = -jnp.exp(alog_ref[...]) * softplus
    beta = jax.nn.sigmoid(ab[:, nd:])
    cpos = r & (DN_CHUNK - 1)
    gp, gs = g, g
    sh = 1
    while sh < DN_CHUNK:
        gp = gp + jnp.where(cpos >= sh, pltpu.roll(gp, sh, axis=0), 0.0)
        gs = gs + jnp.where(cpos < DN_CHUNK - sh, pltpu.roll(gs, nrow - sh, axis=0), 0.0)
        sh *= 2
    colid = lax.broadcasted_iota(jnp.int32, (1, nd), 1)
    gates = jnp.concatenate([jnp.where(colid < DN_HEADS, gp, gs), beta], axis=1)
    gate_ref[...] = gates
    gate_t_ref[...] = gates.T


def _dn_inproj(xs, gain, shift3, scale3, w_in, conv_w, a_log, dt_bias, n_lat_blocks, blocks_per_batch):
    t, d = xs.shape
    nd = 2 * DN_HEADS
    row = lambda i: (i, 0)
    fixed = lambda i: (0, 0)
    grp = lambda i: (_group_of_block(i, n_lat_blocks, blocks_per_batch), 0, 0)
    return pl.pallas_call(
        functools.partial(_dn_inproj_kernel, n_lat_blocks=n_lat_blocks),
        out_shape=[jax.ShapeDtypeStruct((t, d), F32)] * 3 + [jax.ShapeDtypeStruct((t, d), BF16),
                                                              jax.ShapeDtypeStruct((t, 2 * nd), F32),
                                                              jax.ShapeDtypeStruct((2 * nd, t), F32)],
        grid=(t // DN_BLOCK,),
        in_specs=[pl.BlockSpec((DN_BLOCK, d), row),
                  pl.BlockSpec((1, d), fixed),
                  pl.BlockSpec((1, 1, d), grp),
                  pl.BlockSpec((1, 1, d), grp),
                  pl.BlockSpec((d, 4 * d), fixed),
                  pl.BlockSpec((d, 2 * nd), fixed),
                  pl.BlockSpec((3, 3 * d), fixed),
                  pl.BlockSpec((1, nd), fixed),
                  pl.BlockSpec((1, nd), fixed)],
        out_specs=[pl.BlockSpec((DN_BLOCK, d), row)] * 4 + [pl.BlockSpec((DN_BLOCK, 2 * nd), row),
                                                            pl.BlockSpec((2 * nd, DN_BLOCK), lambda i: (0, i))],
        compiler_params=pltpu.CompilerParams(
            dimension_semantics=("arbitrary",), vmem_limit_bytes=VMEM_LIMIT_BYTES),
        name="dn_inproj",
    )(xs, gain.reshape(1, d), shift3, scale3, w_in[:, :4 * d].astype(BF16), w_in[:, 4 * d:].astype(BF16), conv_w,
      a_log.reshape(1, nd), dt_bias.reshape(1, nd))


def _dotb(a, b):
    return jnp.dot(a.astype(BF16), b.astype(BF16), preferred_element_type=F32)


def _dotb_nt(a, b):
    return lax.dot_general(a.astype(BF16), b.astype(BF16), (((1,), (1,)), ((), ())),
                           preferred_element_type=F32)


def _dotb_tn(a, b):
    return lax.dot_general(a.astype(BF16), b.astype(BF16), (((0,), (0,)), ((), ())),
                           preferred_element_type=F32)


def _unit_tri_inverses(mats, ii, jj):
    del ii, jj
    c = DN_CHUNK
    assert len(mats) % 2 == 0
    ii = lax.broadcasted_iota(jnp.int32, (c, 2 * c), 0)
    lane = lax.broadcasted_iota(jnp.int32, (c, 2 * c), 1)
    jj = lane & (c - 1)
    left = lane < c

    def blockdiag(p):
        pb = p.astype(BF16)
        zero = jnp.zeros_like(pb)
        return jnp.concatenate([jnp.where(left, pb, zero), jnp.where(left, zero, pb)], axis=0)

    def mul(p, q):
        return jnp.dot(p.astype(BF16), blockdiag(q), preferred_element_type=F32)

    pairs = [jnp.concatenate([mats[i], mats[i + 1]], axis=1) for i in range(0, len(mats), 2)]
    eye = (ii == jj).astype(F32)
    diag8 = (ii >> 3) == (jj >> 3)
    n = [-jnp.where(diag8, a, 0.0) for a in pairs]
    n2 = [mul(x, x) for x in n]
    m = [eye + x for x in n]
    m = [x + mul(x, y) for x, y in zip(m, n2)]
    n4 = [mul(x, x) for x in n2]
    m = [x + mul(x, y) for x, y in zip(m, n4)]
    sh = 3
    while (1 << sh) < c:
        off = ((ii >> (sh + 1)) == (jj >> (sh + 1))) & ((ii >> sh) != (jj >> sh))
        cm = [mul(jnp.where(off, a, 0.0), x) for a, x in zip(pairs, m)]
        m = [x - mul(x, y) for x, y in zip(m, cm)]
        sh += 1
    return [half for x in m for half in (x[:, :c], x[:, c:])]


def _dn_scan_kernel(qf_ref, kf_ref, vf_ref, gcf_ref, grf_ref, qb_ref, kb_ref, vb_ref, gcb_ref, grb_ref,
                    of_ref, ob_ref, s_ref):
    @pl.when(pl.program_id(2) == 0)
    def _():
        s_ref[...] = jnp.zeros_like(s_ref)

    c = DN_CHUNK
    ncb = N_CHUNKS_PER_BLOCK
    ii = lax.broadcasted_iota(jnp.int32, (c, c), 0)
    jj = lax.broadcasted_iota(jnp.int32, (c, c), 1)
    incl = (ii >= jj, ii <= jj)
    strict = (ii > jj, ii < jj)
    dirs = ((qf_ref, kf_ref, vf_ref, gcf_ref, grf_ref, of_ref),
            (qb_ref, kb_ref, vb_ref, gcb_ref, grb_ref, ob_ref))
    items = [(d, hh, ci) for d in range(2) for hh in range(DN_HB) for ci in range(ncb)]

    def rows(ci):
        return slice(ci * c, (ci + 1) * c)

    def cols(hh):
        return slice(hh * DN_DK, (hh + 1) * DN_DK)

    q = [dirs[d][0][rows(ci), cols(hh)] for d, hh, ci in items]
    k = [dirs[d][1][rows(ci), cols(hh)] for d, hh, ci in items]
    v = [dirs[d][2][rows(ci), cols(hh)] for d, hh, ci in items]
    nh = DN_HEADS
    gc = [dirs[d][3][rows(ci), d * nh + hh:d * nh + hh + 1] for d, hh, ci in items]
    gr = [dirs[d][4][d * nh + hh:d * nh + hh + 1, rows(ci)] for d, hh, ci in items]
    beta = [dirs[d][3][rows(ci), (2 + d) * nh + hh:(2 + d) * nh + hh + 1] for d, hh, ci in items]

    decay = [jnp.where(incl[it[0]], jnp.exp(jnp.where(incl[it[0]], x - y, 0.0)), 0.0)
             for it, x, y in zip(items, gc, gr)]
    kb = [x * y for x, y in zip(k, beta)]
    a = [_dotb_nt(x, y) * jnp.where(strict[it[0]], z, 0.0) for it, x, y, z in zip(items, kb, k, decay)]
    attn = [_dotb_nt(x, y) * z for x, y, z in zip(q, k, decay)]
    tinv = _unit_tri_inverses(a, ii, jj)
    eg = [jnp.exp(x) for x in gc]
    uw = [_dotb(t, jnp.concatenate([x * y, z * e], axis=-1))
          for t, x, y, z, e in zip(tinv, v, beta, kb, eg)]
    g_last = [x[0:1] if it[0] else x[c - 1:c] for it, x in zip(items, gc)]
    wq = [jnp.concatenate([x[:, DN_DV:], y * e], axis=0) for x, y, e in zip(uw, q, eg)]
    k_dec = [x * jnp.exp(y - z) for x, y, z in zip(k, g_last, gc)]
    s_dec = [jnp.exp(x) for x in g_last]

    chains = [(d, hh) for d in range(2) for hh in range(DN_HB)]
    state = [s_ref[d, hh] for d, hh in chains]
    for step in range(ncb):
        cur = [items.index((d, hh, ncb - 1 - step if d else step)) for d, hh in chains]
        ws = [_dotb(wq[n], s) for n, s in zip(cur, state)]
        v_new = [uw[n][:, :DN_DV] - x[:c] for n, x in zip(cur, ws)]
        o = [x[c:] + _dotb(attn[n], y) for n, x, y in zip(cur, ws, v_new)]
        state = [s * s_dec[n] + _dotb_tn(k_dec[n], y) for n, s, y in zip(cur, state, v_new)]
        for n, x in zip(cur, o):
            d, hh, ci = items[n]
            dirs[d][5][rows(ci), cols(hh)] = x.astype(BF16)
    for (d, hh), s in zip(chains, state):
        s_ref[d, hh] = s


def _dn_scan(q, k, v, gates, gates_t, n_lat_blocks, blocks_per_batch):
    t, d = q.shape
    bsz = n_lat_blocks // blocks_per_batch
    assert DN_HB == DN_HEADS
    ng = gates.shape[1]

    def blk_f(b, s):
        return jnp.where(s == 0, n_lat_blocks + b, b * blocks_per_batch + s - 1)

    def blk_b(b, s):
        return jnp.where(s == 0, n_lat_blocks + b, b * blocks_per_batch + blocks_per_batch - s)

    hw = DN_HB * DN_DK

    def specs(blk):
        return [pl.BlockSpec((DN_BLOCK, hw), lambda b, hg, s: (blk(b, s), hg))] * 3 + [
            pl.BlockSpec((DN_BLOCK, ng), lambda b, hg, s: (blk(b, s), 0)),
            pl.BlockSpec((ng, DN_BLOCK), lambda b, hg, s: (0, blk(b, s)))]

    return pl.pallas_call(
        _dn_scan_kernel,
        out_shape=[jax.ShapeDtypeStruct((t, d), BF16)] * 2,
        grid=(bsz, DN_HEADS // DN_HB, 1 + blocks_per_batch),
        in_specs=specs(blk_f) + specs(blk_b),
        out_specs=[pl.BlockSpec((DN_BLOCK, hw), lambda b, hg, s: (blk_f(b, s), hg)),
                   pl.BlockSpec((DN_BLOCK, hw), lambda b, hg, s: (blk_b(b, s), hg))],
        scratch_shapes=[pltpu.VMEM((2, DN_HB, DN_DK, DN_DV), F32)],
        compiler_params=pltpu.CompilerParams(
            dimension_semantics=("arbitrary", "arbitrary", "arbitrary"),
            vmem_limit_bytes=VMEM_LIMIT_BYTES),
        name="dn_scan",
    )(q, k, v, gates, gates_t, q, k, v, gates, gates_t)


def _dn_out_kernel(of_ref, ob_ref, z_ref, on_ref, w_ref, x_ref, gate_ref, o_ref):
    z = z_ref[...].astype(F32)
    zs = z * jax.nn.sigmoid(z)
    parts = []
    for h in range(DN_HEADS):
        cols = slice(h * DN_DV, (h + 1) * DN_DV)
        o = of_ref[:, cols].astype(F32) + ob_ref[:, cols].astype(F32)
        o = o * lax.rsqrt(jnp.mean(o * o, axis=-1, keepdims=True) + EPS)
        parts.append(o)
    y = jnp.concatenate(parts, axis=-1) * on_ref[...] * zs
    o_ref[...] = x_ref[...] + gate_ref[0] * jnp.dot(y.astype(BF16), w_ref[...],
                                                    preferred_element_type=F32)


def _dn_out(o_f, o_b, z, out_norm, w_out, xs, gate3, n_lat_blocks, blocks_per_batch):
    t, d = xs.shape
    row = lambda i: (i, 0)
    fixed = lambda i: (0, 0)
    grp = lambda i: (_group_of_block(i, n_lat_blocks, blocks_per_batch), 0, 0)
    return pl.pallas_call(
        _dn_out_kernel,
        out_shape=jax.ShapeDtypeStruct((t, d), F32),
        grid=(t // DN_BLOCK,),
        in_specs=[pl.BlockSpec((DN_BLOCK, d), row),
                  pl.BlockSpec((DN_BLOCK, d), row),
                  pl.BlockSpec((DN_BLOCK, d), row),
                  pl.BlockSpec((1, d), fixed),
                  pl.BlockSpec((d, d), fixed),
                  pl.BlockSpec((DN_BLOCK, d), row),
                  pl.BlockSpec((1, 1, d), grp)],
        out_specs=pl.BlockSpec((DN_BLOCK, d), row),
        compiler_params=pltpu.CompilerParams(
            dimension_semantics=("arbitrary",), vmem_limit_bytes=VMEM_LIMIT_BYTES),
        name="dn_out",
    )(o_f, o_b, z, jnp.tile(out_norm, DN_HEADS).reshape(1, d), w_out.astype(BF16), xs, gate3)


def _deltanet_layer(xs, gain, shift3, scale3, w_in, conv_w, a_log, dt_bias, out_norm, w_out, gate3,
                    n_lat_blocks, blocks_per_batch):
    q, k, v, z, gates, gates_t = _dn_inproj(xs, gain, shift3, scale3, w_in, conv_w, a_log, dt_bias,
                                            n_lat_blocks, blocks_per_batch)
    o_f, o_b = _dn_scan(q, k, v, gates, gates_t, n_lat_blocks, blocks_per_batch)
    return _dn_out(o_f, o_b, z, out_norm, w_out, xs, gate3, n_lat_blocks, blocks_per_batch)


HY_N2 = 256
HY_CB = 16
HY_TOK_TILE = 512
HY_FILT_TILE = 512


def _dft_constants(nr):
    n1, n2 = 2 * nr, HY_N2
    n = n1 * n2
    nk = -(-(nr + 1) // 8) * 8
    keep = np.arange(nk) <= nr
    k1 = np.where(keep, np.arange(nk), 0)
    f1 = np.exp(-2j * np.pi * np.outer(k1, np.arange(nr)) / n1) * keep[:, None]
    twice = np.where((k1 > 0) & (k1 < nr), 2.0, 1.0) * keep
    lhs_fwd = np.concatenate([f1.real, f1.imag], axis=0)
    lhs_inv = np.concatenate([f1.real.T * twice, f1.imag.T * twice], axis=1) / n
    tw = np.exp(-2j * np.pi * np.outer(k1, np.arange(n2)) / n)
    a2 = np.arange(n2)
    f2 = np.exp(-2j * np.pi * np.outer(a2, a2) / n2)
    w_fwd = np.block([[f2.real, f2.imag], [-f2.imag, f2.real]])
    w_inv = np.block([[f2.real, -f2.imag], [f2.imag, f2.real]])
    return (jnp.asarray(lhs_fwd, BF16), jnp.asarray(lhs_inv, BF16), jnp.asarray(tw.real, F32),
            jnp.asarray(tw.imag, F32), jnp.asarray(w_fwd, BF16), jnp.asarray(w_inv, BF16))


def _hy_dft(x3, lhs_fwd, twr, twi, w_fwd):
    n1 = twr.shape[0]
    a = [jnp.dot(lhs_fwd, x3[c].astype(BF16), preferred_element_type=F32) for c in range(x3.shape[0])]
    br = jnp.concatenate([t[:n1] * twr - t[n1:] * twi for t in a], axis=0)
    bi = jnp.concatenate([t[:n1] * twi + t[n1:] * twr for t in a], axis=0)
    b = jnp.concatenate([br, bi], axis=1)
    return jnp.dot(b.astype(BF16), w_fwd, preferred_element_type=F32)


def _hy_idft(p, cb, lhs_inv, twr, twi, w_inv):
    n1, n2 = twr.shape
    c = jnp.dot(p.astype(BF16), w_inv, preferred_element_type=F32)
    out = []
    for ch in range(cb):
        cr = c[ch * n1:(ch + 1) * n1, :n2]
        ci = c[ch * n1:(ch + 1) * n1, n2:]
        d = jnp.concatenate([cr * twr + ci * twi, ci * twr - cr * twi], axis=0)
        out.append(jnp.dot(lhs_inv, d.astype(BF16), preferred_element_type=F32))
    return out


def _hy_spectrum_kernel(hf_ref, hb_ref, lf_ref, twr_ref, twi_ref, wf_ref, o_ref):
    cb, nr, n2 = hf_ref.shape
    first = ((lax.broadcasted_iota(jnp.int32, (nr, n2), 0) == 0)
             & (lax.broadcasted_iota(jnp.int32, (nr, n2), 1) == 0))
    hb = jnp.where(first, 0.0, hb_ref[...])
    consts = (lf_ref[...], twr_ref[...], twi_ref[...], wf_ref[...])
    xf = _hy_dft(hf_ref[...], *consts)
    xb = _hy_dft(hb, *consts)
    o_ref[...] = jnp.concatenate([xf[:, :n2] + xb[:, :n2], xf[:, n2:] - xb[:, n2:]],
                                 axis=1).reshape(o_ref.shape)


def _hy_spectrum(filt, consts):
    d = D_MODEL
    l = filt.shape[1]
    nr = l // HY_N2
    lhs_fwd, _, twr, twi, w_fwd, _ = consts
    n1 = twr.shape[0]
    cpo = d // HY_CB
    fixed2 = lambda o, c: (0, 0)
    return pl.pallas_call(
        _hy_spectrum_kernel,
        out_shape=jax.ShapeDtypeStruct((HY_ORDER * d, n1, 2 * HY_N2), F32),
        grid=(HY_ORDER, cpo),
        in_specs=[pl.BlockSpec((HY_CB, nr, HY_N2), lambda o, c: (2 * o * cpo + c, 0, 0)),
                  pl.BlockSpec((HY_CB, nr, HY_N2), lambda o, c: ((2 * o + 1) * cpo + c, 0, 0)),
                  pl.BlockSpec(lhs_fwd.shape, fixed2),
                  pl.BlockSpec(twr.shape, fixed2),
                  pl.BlockSpec(twi.shape, fixed2),
                  pl.BlockSpec(w_fwd.shape, fixed2)],
        out_specs=pl.BlockSpec((HY_CB, n1, 2 * HY_N2), lambda o, c: (o * cpo + c, 0, 0)),
        compiler_params=pltpu.CompilerParams(
            dimension_semantics=("arbitrary", "arbitrary"), vmem_limit_bytes=VMEM_LIMIT_BYTES),
        name="hy_spectrum",
    )(filt.reshape(-1, nr, HY_N2), filt.reshape(-1, nr, HY_N2), lhs_fwd, twr, twi, w_fwd)


def _hy_conv_kernel(z_ref, g_ref, k_ref, bias_ref, lf_ref, li_ref, twr_ref, twi_ref, wf_ref, wi_ref,
                    o_ref):
    cb, nr, n2 = z_ref.shape
    twr, twi = twr_ref[...], twi_ref[...]
    z = z_ref[...]
    x = _hy_dft(z, lf_ref[...], twr, twi, wf_ref[...])
    kk = k_ref[...].reshape(x.shape)
    xr, xi, kr, ki = x[:, :n2], x[:, n2:], kk[:, :n2], kk[:, n2:]
    p = jnp.concatenate([xr * kr - xi * ki, xr * ki + xi * kr], axis=1)
    conv = _hy_idft(p, cb, li_ref[...], twr, twi, wi_ref[...])
    for c in range(cb):
        o_ref[c] = g_ref[c] * (conv[c] + bias_ref[c] * z[c])


def _hy_conv(z, z_part, gate, gate_part, khat, order, bias, consts, bsz):
    d = D_MODEL
    l = z.shape[1] // bsz
    nr = l // HY_N2
    cpo = d // HY_CB
    lhs_fwd, lhs_inv, twr, twi, w_fwd, w_inv = consts
    n1 = twr.shape[0]
    fixed2 = lambda c, b: (0, 0)
    out = pl.pallas_call(
        _hy_conv_kernel,
        out_shape=jax.ShapeDtypeStruct((d, bsz * nr, HY_N2), F32),
        grid=(cpo, bsz),
        in_specs=[pl.BlockSpec((HY_CB, nr, HY_N2), lambda c, b: (z_part * cpo + c, b, 0)),
                  pl.BlockSpec((HY_CB, nr, HY_N2), lambda c, b: (gate_part * cpo + c, b, 0)),
                  pl.BlockSpec((HY_CB, n1, 2 * HY_N2), lambda c, b: (order * cpo + c, 0, 0)),
                  pl.BlockSpec((HY_CB, 1, 1), lambda c, b: (c, 0, 0)),
                  pl.BlockSpec(lhs_fwd.shape, fixed2),
                  pl.BlockSpec(lhs_inv.shape, fixed2),
                  pl.BlockSpec(twr.shape, fixed2),
                  pl.BlockSpec(twi.shape, fixed2),
                  pl.BlockSpec(w_fwd.shape, fixed2),
                  pl.BlockSpec(w_inv.shape, fixed2)],
        out_specs=pl.BlockSpec((HY_CB, nr, HY_N2), lambda c, b: (c, b, 0)),
        compiler_params=pltpu.CompilerParams(
            dimension_semantics=("arbitrary", "arbitrary"), vmem_limit_bytes=VMEM_LIMIT_BYTES),
        name="hy_conv",
    )(z.reshape(-1, bsz * nr, HY_N2), gate.reshape(-1, bsz * nr, HY_N2), khat,
      bias.reshape(d, 1, 1), lhs_fwd, lhs_inv, twr, twi, w_fwd, w_inv)
    return out.reshape(d, bsz * l)


def _hy_ctx_kernel(z_ref, g_ref, hf_ref, hb_ref, bias_ref, wf_ref, wi_ref, o_ref, *, bsz):
    l = hf_ref.shape[1]
    wf, wi = wf_ref[...], wi_ref[...]
    hb = jnp.where(lax.broadcasted_iota(jnp.int32, (1, l), 1) == 0, 0.0, hb_ref[...])
    kf = jnp.dot(hf_ref[...].astype(BF16), wf, preferred_element_type=F32)
    kb = jnp.dot(hb.astype(BF16), wf, preferred_element_type=F32)
    n = 2 * l
    kr, ki = kf[:, :n] + kb[:, :n], kf[:, n:] - kb[:, n:]
    bias = bias_ref[...]
    for b in range(bsz):
        z = z_ref[:, b * l:(b + 1) * l]
        x = jnp.dot(z.astype(BF16), wf, preferred_element_type=F32)
        xr, xi = x[:, :n], x[:, n:]
        p = jnp.concatenate([xr * kr - xi * ki, xr * ki + xi * kr], axis=1)
        conv = jnp.dot(p.astype(BF16), wi, preferred_element_type=F32)
        o_ref[:, b * l:(b + 1) * l] = g_ref[:, b * l:(b + 1) * l] * (conv + bias * z)


def _hy_ctx(z, z_part, gate, gate_part, filt, order, bias, bsz):
    d = D_MODEL
    l = filt.shape[1]
    n = 2 * l
    ang = 2 * np.pi * np.outer(np.arange(l), np.arange(n)) / n
    w_fwd = jnp.asarray(np.concatenate([np.cos(ang), -np.sin(ang)], axis=1), BF16)
    w_inv = jnp.asarray(np.concatenate([np.cos(ang.T), -np.sin(ang.T)], axis=0) / n, BF16)
    cb = 256
    nblk = d // cb
    fixed = lambda c: (0, 0)
    return pl.pallas_call(
        functools.partial(_hy_ctx_kernel, bsz=bsz),
        out_shape=jax.ShapeDtypeStruct((d, bsz * l), F32),
        grid=(nblk,),
        in_specs=[pl.BlockSpec((cb, bsz * l), lambda c: (z_part * nblk + c, 0)),
                  pl.BlockSpec((cb, bsz * l), lambda c: (gate_part * nblk + c, 0)),
                  pl.BlockSpec((cb, l), lambda c: (2 * order * nblk + c, 0)),
                  pl.BlockSpec((cb, l), lambda c: ((2 * order + 1) * nblk + c, 0)),
                  pl.BlockSpec((cb, 1), lambda c: (c, 0)),
                  pl.BlockSpec(w_fwd.shape, fixed),
                  pl.BlockSpec(w_inv.shape, fixed)],
        out_specs=pl.BlockSpec((cb, bsz * l), lambda c: (c, 0)),
        compiler_params=pltpu.CompilerParams(
            dimension_semantics=("arbitrary",), vmem_limit_bytes=VMEM_LIMIT_BYTES),
        name="hy_ctx_conv",
    )(z, gate, filt, filt, bias.reshape(d, 1), w_fwd, w_inv)


def _hy_filter_kernel(band_ref, w1t_ref, w1c_ref, w1s_ref, b1_ref, fr_ref, w2_ref, b2_ref, w3_ref,
                      delta_ref, o_ref, *, l):
    tl = o_ref.shape[1]
    d = D_MODEL
    hp = lax.Precision.HIGHEST
    pos = (lax.broadcasted_iota(jnp.int32, (1, tl), 1) + pl.program_id(0) * tl).astype(F32)
    t = pos / max(l - 1, 1)
    ang = ((2 * math.pi / l) * pos) * band_ref[...]
    fr = fr_ref[...]
    pre = (w1t_ref[...] * t + jnp.dot(w1c_ref[...], jnp.cos(ang), precision=hp)
           + jnp.dot(w1s_ref[...], -jnp.sin(ang), precision=hp) + b1_ref[...])
    hdn = jnp.sin(fr * pre)
    hdn = jnp.sin(fr * (jnp.dot(w2_ref[...], hdn, precision=hp) + b2_ref[...]))
    window = jnp.exp(-t * delta_ref[...])
    for part in range(2 * HY_ORDER):
        rows = slice(part * d, (part + 1) * d)
        o_ref[rows, :] = jnp.dot(w3_ref[rows, :], hdn.astype(BF16), preferred_element_type=F32) * window


def _hy_filter(l, w1, b1, freq, w2, b2, w3):
    d = D_MODEL
    nb = HY_BANDS
    tl = min(HY_FILT_TILE, l)
    col = lambda v: v.reshape(-1, 1)
    bands = jnp.linspace(1e-4, nb - 1, nb, dtype=F32)
    deltas = jnp.abs(jnp.linspace(math.log(HY_TARGET) / HY_SLOW, math.log(HY_TARGET) / HY_FAST, d, dtype=F32))
    w1t = w1.T
    args = (col(bands), w1t[:, 0:1], w1t[:, 1:1 + nb], w1t[:, 1 + nb:], col(b1), col(freq), w2.T, col(b2),
            w3.T.astype(BF16), col(deltas))
    return pl.pallas_call(
        functools.partial(_hy_filter_kernel, l=l),
        out_shape=jax.ShapeDtypeStruct((2 * HY_ORDER * d, l), F32),
        grid=(l // tl,),
        in_specs=[pl.BlockSpec(a.shape, lambda j: (0, 0)) for a in args],
        out_specs=pl.BlockSpec((2 * HY_ORDER * d, tl), lambda j: (0, j)),
        compiler_params=pltpu.CompilerParams(
            dimension_semantics=("arbitrary",), vmem_limit_bytes=VMEM_LIMIT_BYTES),
        name="hy_filter",
    )(*args)


def _hy_inproj_kernel(x_ref, gain_ref, shift_ref, scale_ref, wt_ref, cw_ref, o_ref, *, seg):
    nch = wt_ref.shape[0]
    tm = x_ref.shape[0]
    hb = _modulated(x_ref[...], gain_ref[...], shift_ref[0], scale_ref[0]).astype(BF16)
    pos = lax.broadcasted_iota(jnp.int32, (1, tm), 1) & (seg - 1)
    not_first = pos != 0
    not_last = pos != seg - 1
    sub = 512
    for j in range(nch // sub):
        rows = slice(j * sub, (j + 1) * sub)
        p = _dotb_nt(wt_ref[rows, :], hb)
        cw = cw_ref[rows, :]
        prev = jnp.where(not_first, pltpu.roll(p, 1, axis=1), 0.0)
        nxt = jnp.where(not_last, pltpu.roll(p, tm - 1, axis=1), 0.0)
        o_ref[rows, :] = cw[:, 0:1] * prev + cw[:, 1:2] * p + cw[:, 2:3] * nxt


def _hy_inproj(xs, gain, shift3, scale3, w_in, conv_w, first_tile, n_tiles, seg, n_lat_blocks,
               blocks_per_batch):
    k = xs.shape[1]
    nch = w_in.shape[1]
    tm = HY_TOK_TILE
    per_tile = tm // DN_BLOCK
    grp = lambda i: (_group_of_block((first_tile + i) * per_tile, n_lat_blocks, blocks_per_batch), 0, 0)
    return pl.pallas_call(
        functools.partial(_hy_inproj_kernel, seg=seg),
        out_shape=jax.ShapeDtypeStruct((nch, n_tiles * tm), F32),
        grid=(n_tiles,),
        in_specs=[pl.BlockSpec((tm, k), lambda i: (first_tile + i, 0)),
                  pl.BlockSpec((1, k), lambda i: (0, 0)),
                  pl.BlockSpec((1, 1, k), grp),
                  pl.BlockSpec((1, 1, k), grp),
                  pl.BlockSpec((nch, k), lambda i: (0, 0)),
                  pl.BlockSpec((nch, 3), lambda i: (0, 0))],
        out_specs=pl.BlockSpec((nch, tm), lambda i: (0, i)),
        compiler_params=pltpu.CompilerParams(
            dimension_semantics=("arbitrary",), vmem_limit_bytes=VMEM_LIMIT_BYTES),
        name="hy_inproj",
    )(xs, gain.reshape(1, k), shift3, scale3, w_in.T.astype(BF16), conv_w.T)


def _hy_out_kernel(zl_ref, zc_ref, w_ref, x_ref, gate_ref, o_ref, *, n_lat_tiles):
    z = jnp.where(pl.program_id(0) >= n_lat_tiles, zc_ref[...], zl_ref[...])
    o_ref[...] = x_ref[...] + gate_ref[0] * _dotb_tn(z, w_ref[...])


def _hy_out(z_lat, z_ctx, w_out, xs, gate3, blocks_per_batch):
    t, d = xs.shape
    tm = HY_TOK_TILE
    n_lat_tiles = z_lat.shape[1] // tm
    per_tile = tm // DN_BLOCK
    grp = lambda i: (_group_of_block(i * per_tile, n_lat_tiles * per_tile, blocks_per_batch), 0, 0)
    return pl.pallas_call(
        functools.partial(_hy_out_kernel, n_lat_tiles=n_lat_tiles),
        out_shape=jax.ShapeDtypeStruct((t, d), F32),
        grid=(t // tm,),
        in_specs=[pl.BlockSpec((d, tm), lambda i: (0, jnp.minimum(i, n_lat_tiles - 1))),
                  pl.BlockSpec((d, tm), lambda i: (0, 0)),
                  pl.BlockSpec((d, d), lambda i: (0, 0)),
                  pl.BlockSpec((tm, d), lambda i: (i, 0)),
                  pl.BlockSpec((1, 1, d), grp)],
        out_specs=pl.BlockSpec((tm, d), lambda i: (i, 0)),
        compiler_params=pltpu.CompilerParams(
            dimension_semantics=("arbitrary",), vmem_limit_bytes=VMEM_LIMIT_BYTES),
        name="hy_out",
    )(z_lat, z_ctx, w_out.astype(BF16), xs, gate3)


def _hyena_layer(xs, gain, shift3, scale3, w_in, conv_w, f_w1, f_b1, f_freq, f_w2, f_b2, f_w3, bias, w_out,
                 gate3, bsz, seq):
    n_lat_tiles = bsz * seq // HY_TOK_TILE
    assert bsz * CTX_LEN == HY_TOK_TILE
    margs = (xs, gain, shift3, scale3, w_in, conv_w)
    blocks = (bsz * seq // DN_BLOCK, seq // DN_BLOCK)
    p_lat = _hy_inproj(*margs, 0, n_lat_tiles, GRID_W, *blocks)
    p_ctx = _hy_inproj(*margs, n_lat_tiles, 1, CTX_LEN, *blocks)
    fargs = (f_w1, f_b1, f_freq, f_w2, f_b2, f_w3)
    consts = _dft_constants(seq // HY_N2)
    khat = _hy_spectrum(_hy_filter(seq, *fargs), consts)
    filt_ctx = _hy_filter(CTX_LEN, *fargs)
    z_lat, z_ctx = p_lat, p_ctx
    for n in range(HY_ORDER):
        z_lat = _hy_conv(z_lat, 0, p_lat, n + 1, khat, n, bias[n], consts, bsz)
        z_ctx = _hy_ctx(z_ctx, 0, p_ctx, n + 1, filt_ctx, n, bias[n], bsz)
    return _hy_out(z_lat, z_ctx, w_out, xs, gate3, seq // DN_BLOCK)


MOE_TILE = 512
PACK = 2


def _route_kernel(x_ref, gain_ref, shift_ref, scale_ref, wr_ref, rb_ref, tri_ref,
                  f_ref, e_ref, w_ref, r_ref, cnt_ref, carry_ref):
    tm, d = x_ref.shape
    ne, epg, ng = N_EXPERTS, EXPERTS_PER_GROUP, N_GROUPS

    @pl.when(pl.program_id(0) == 0)
    def _():
        carry_ref[...] = jnp.zeros_like(carry_ref)

    x = x_ref[...]
    y = x * lax.rsqrt(jnp.mean(x * x, axis=-1, keepdims=True) + EPS) * gain_ref[...]
    f = y * (1 + scale_ref[0]) + shift_ref[0]
    bits = pltpu.bitcast(f.astype(BF16).astype(F32), jnp.uint32)
    half = d // PACK
    f_ref[...] = (bits[:, :half] >> 16) | (bits[:, half:] & jnp.uint32(0xFFFF0000))

    logits = lax.dot_general(wr_ref[...], f, (((1,), (1,)), ((), ())),
                             precision=lax.Precision.HIGHEST, preferred_element_type=F32)
    scores = jax.nn.sigmoid(logits)
    biased = scores + rb_ref[...]
    s = [scores[j * ng:(j + 1) * ng] for j in range(epg)]
    c = [biased[j * ng:(j + 1) * ng] for j in range(epg)]
    hi01, lo01 = jnp.maximum(c[0], c[1]), jnp.minimum(c[0], c[1])
    hi23, lo23 = jnp.maximum(c[2], c[3]), jnp.minimum(c[2], c[3])
    gscore = jnp.maximum(hi01, hi23) + jnp.maximum(jnp.minimum(hi01, hi23), jnp.maximum(lo01, lo23))
    gi = lax.broadcasted_iota(jnp.int32, (ng, tm), 0)
    gmax = jnp.max(gscore, axis=0, keepdims=True)
    grp = jnp.min(jnp.where(gscore == gmax, gi, ng), axis=0, keepdims=True)
    sel = gi == grp
    cv = [jnp.sum(jnp.where(sel, t, 0.0), axis=0, keepdims=True) for t in c]
    sv = [jnp.sum(jnp.where(sel, t, 0.0), axis=0, keepdims=True) for t in s]

    def pick(excluded):
        best = jnp.full((1, tm), -jnp.inf, F32)
        idx = jnp.zeros((1, tm), jnp.int32)
        val = jnp.zeros((1, tm), F32)
        for j in range(epg):
            cand = cv[j] if excluded is None else jnp.where(excluded == j, -jnp.inf, cv[j])
            take = cand > best
            best = jnp.where(take, cand, best)
            idx = jnp.where(take, j, idx)
            val = jnp.where(take, sv[j], val)
        return idx, val

    i1, v1 = pick(None)
    i2, v2 = pick(i1)
    e1 = grp * epg + i1
    e2 = grp * epg + i2
    wsum = v1 + v2
    e_ref[0:1, :] = e1
    e_ref[1:2, :] = e2
    w_ref[0:1, :] = v1 / wsum
    w_ref[1:2, :] = v2 / wsum

    ei = lax.broadcasted_iota(jnp.int32, (ne, tm), 0)
    oh1 = ei == e1
    oh2 = ei == e2
    tri = tri_ref[...]
    pre1 = jnp.dot(oh1.astype(BF16), tri, preferred_element_type=F32)
    pre2 = jnp.dot(oh2.astype(BF16), tri, preferred_element_type=F32)
    tot1 = pre1[:, tm - 1:tm]
    tot2 = pre2[:, tm - 1:tm]
    carry = carry_ref[:, 0:1]
    r1 = jnp.sum(jnp.where(oh1, carry + pre1 - 1.0, 0.0), axis=0, keepdims=True)
    r2 = jnp.sum(jnp.where(oh2, carry + tot1 + pre2 - 1.0, 0.0), axis=0, keepdims=True)
    r_ref[0:1, :] = r1.astype(jnp.int32)
    r_ref[1:2, :] = r2.astype(jnp.int32)
    carry_ref[...] = carry_ref[...] + (tot1 + tot2)
    cnt_ref[...] = carry_ref[...]


def _moe_route(xs, gain, shift3, scale3, w_router, router_bias, n_lat_tiles, blocks_per_batch):
    t, d = xs.shape
    tm = MOE_TILE
    per_tile = tm // DN_BLOCK
    row = lambda i: (i, 0)
    col = lambda i: (0, i)
    fixed = lambda i: (0, 0)
    grp = lambda i: (_group_of_block(i * per_tile, n_lat_tiles * per_tile, blocks_per_batch), 0, 0)
    tri = jnp.asarray(np.triu(np.ones((tm, tm), np.float32)), BF16)
    perm = np.arange(N_EXPERTS).reshape(N_GROUPS, EXPERTS_PER_GROUP).T.reshape(-1)
    return pl.pallas_call(
        _route_kernel,
        out_shape=[jax.ShapeDtypeStruct((t, d // PACK), jnp.uint32),
                   jax.ShapeDtypeStruct((TOP_K, t), jnp.int32),
                   jax.ShapeDtypeStruct((TOP_K, t), F32),
                   jax.ShapeDtypeStruct((TOP_K, t), jnp.int32),
                   jax.ShapeDtypeStruct((N_EXPERTS, 128), F32)],
        grid=(t // tm,),
        in_specs=[pl.BlockSpec((tm, d), row),
                  pl.BlockSpec((1, d), fixed),
                  pl.BlockSpec((1, 1, d), grp),
                  pl.BlockSpec((1, 1, d), grp),
                  pl.BlockSpec((N_EXPERTS, d), fixed),
                  pl.BlockSpec((N_EXPERTS, 1), fixed),
                  pl.BlockSpec((tm, tm), fixed)],
        out_specs=[pl.BlockSpec((tm, d // PACK), row),
                   pl.BlockSpec((TOP_K, tm), col),
                   pl.BlockSpec((TOP_K, tm), col),
                   pl.BlockSpec((TOP_K, tm), col),
                   pl.BlockSpec((N_EXPERTS, 128), fixed)],
        scratch_shapes=[pltpu.VMEM((N_EXPERTS, 128), F32)],
        compiler_params=pltpu.CompilerParams(
            dimension_semantics=("arbitrary",), vmem_limit_bytes=VMEM_LIMIT_BYTES),
        name="moe_route",
    )(xs, gain.reshape(1, d), shift3, scale3, w_router.T[perm], router_bias[perm].reshape(N_EXPERTS, 1), tri)


def _row_copy(src, src_row, dst, dst_row, sem):
    return pltpu.make_async_copy(src.at[pl.ds(src_row, 1)], dst.at[pl.ds(dst_row, 1)], sem)


def _dispatch_kernel(dest_ref, f_ref, xs_in_ref, xs_ref, dest_smem, sem, idx_sem):
    del xs_in_ref
    tm = f_ref.shape[0]
    idx_copy = pltpu.make_async_copy(dest_ref, dest_smem, idx_sem)
    idx_copy.start()
    idx_copy.wait()

    def issue(tok, carry):
        for k in range(TOP_K):
            _row_copy(f_ref, tok, xs_ref, dest_smem[k, tok], sem).start(priority=k)
        return carry

    def drain(tok, carry):
        for k in range(TOP_K):
            _row_copy(f_ref, 0, xs_ref, 0, sem).wait()
        return carry

    lax.fori_loop(0, tm, issue, 0, unroll=8)
    lax.fori_loop(0, tm, drain, 0, unroll=8)


def _dispatch(f_packed, dest, n_slots):
    t, wd = f_packed.shape
    tm = MOE_TILE
    return pl.pallas_call(
        _dispatch_kernel,
        out_shape=jax.ShapeDtypeStruct((n_slots, wd), jnp.uint32),
        grid=(t // tm,),
        in_specs=[pl.BlockSpec((TOP_K, tm), lambda i: (0, i)),
                  pl.BlockSpec((tm, wd), lambda i: (i, 0)),
                  pl.BlockSpec(memory_space=pl.ANY)],
        out_specs=pl.BlockSpec(memory_space=pl.ANY),
        scratch_shapes=[pltpu.SMEM((TOP_K, tm), jnp.int32),
                        pltpu.SemaphoreType.DMA, pltpu.SemaphoreType.DMA],
        input_output_aliases={2: 0},
        compiler_params=pltpu.CompilerParams(
            dimension_semantics=("arbitrary",), vmem_limit_bytes=VMEM_LIMIT_BYTES),
        name="moe_dispatch",
    )(dest, f_packed, jnp.zeros((n_slots, wd), jnp.uint32))


def _experts_kernel(be_ref, na_ref, x_ref, wg_ref, wu_ref, wd_ref, o_ref, wgb_ref, wub_ref, wdb_ref):
    i = pl.program_id(0)
    prev = be_ref[jnp.maximum(i - 1, 0)]

    @pl.when((i == 0) | (be_ref[i] != prev))
    def _():
        wgb_ref[...] = wg_ref[0, 0].astype(BF16)
        wub_ref[...] = wu_ref[0, 0].astype(BF16)
        wdb_ref[...] = wd_ref[0, 0].astype(BF16)

    @pl.when(i < na_ref[0])
    def _():
        packed = x_ref[...]
        lo = pltpu.bitcast(packed << 16, F32)
        hi = pltpu.bitcast(packed & jnp.uint32(0xFFFF0000), F32)
        xb = jnp.concatenate([lo, hi], axis=-1).astype(BF16)
        g = jnp.dot(xb, wgb_ref[...], preferred_element_type=F32)
        u = jnp.dot(xb, wub_ref[...], preferred_element_type=F32)
        hid = (g * jax.nn.sigmoid(g)) * u
        y = jnp.dot(hid.astype(BF16), wdb_ref[...], preferred_element_type=F32)
        ybits = pltpu.bitcast(y.astype(BF16).astype(F32), jnp.uint32)
        half = y.shape[1] // PACK
        o_ref[...] = (ybits[:, :half] >> 16) | (ybits[:, half:] & jnp.uint32(0xFFFF0000))

    @pl.when(i >= na_ref[0])
    def _():
        o_ref[...] = jnp.zeros_like(o_ref)


def _experts(xs_sorted, block_expert, n_active, w_gate, w_up, w_down, layer):
    n_slots, wd = xs_sorted.shape
    d = wd * PACK
    n_blocks = n_slots // MOE_BLOCK
    blk = lambda i, be, na: (jnp.minimum(i, na[0] - 1), 0)
    wsel = lambda i, be, na: (layer, be[jnp.minimum(i, na[0] - 1)], 0, 0)
    grid_spec = pltpu.PrefetchScalarGridSpec(
        num_scalar_prefetch=2,
        grid=(n_blocks,),
        in_specs=[pl.BlockSpec((MOE_BLOCK, wd), blk),
                  pl.BlockSpec((1, 1, d, D_EXPERT), wsel),
                  pl.BlockSpec((1, 1, d, D_EXPERT), wsel),
                  pl.BlockSpec((1, 1, D_EXPERT, d), wsel)],
        out_specs=pl.BlockSpec((MOE_BLOCK, wd), lambda i, be, na: (i, 0)),
        scratch_shapes=[pltpu.VMEM((d, D_EXPERT), BF16), pltpu.VMEM((d, D_EXPERT), BF16),
                        pltpu.VMEM((D_EXPERT, d), BF16)],
    )
    return pl.pallas_call(
        _experts_kernel,
        out_shape=jax.ShapeDtypeStruct((n_slots, wd), jnp.uint32),
        grid_spec=grid_spec,
        compiler_params=pltpu.CompilerParams(
            dimension_semantics=("arbitrary",), vmem_limit_bytes=VMEM_LIMIT_BYTES),
        name="moe_experts",
    )(block_expert, n_active, xs_sorted, w_gate, w_up, w_down)


def _combine_kernel(dest_ref, y_ref, x_ref, w_ref, gate_ref, fin_ref, o_ref, dest_smem, ya_ref, yb_ref, sem,
                    idx_sem, *, final_norm):
    tm = x_ref.shape[0]
    idx_copy = pltpu.make_async_copy(dest_ref, dest_smem, idx_sem)
    idx_copy.start()
    idx_copy.wait()
    bufs = (ya_ref, yb_ref)

    def issue(tok, carry):
        for k in range(TOP_K):
            _row_copy(y_ref, dest_smem[k, tok], bufs[k], tok, sem).start(priority=k)
        return carry

    def drain(tok, carry):
        for k in range(TOP_K):
            _row_copy(y_ref, 0, bufs[k], 0, sem).wait()
        return carry

    lax.fori_loop(0, tm, issue, 0, unroll=8)
    lax.fori_loop(0, tm, drain, 0, unroll=8)
    w = w_ref[...]
    out = x_ref[...] + gate_ref[0] * (w[:, 0:1] * ya_ref[...] + w[:, 1:2] * yb_ref[...])
    if final_norm:
        out = out * lax.rsqrt(jnp.mean(out * out, axis=-1, keepdims=True) + EPS) * fin_ref[...]
    o_ref[...] = out


def _combine(ys, dest, weight_cols, xs, gate3, n_lat_tiles, blocks_per_batch, final_gain=None):
    t, d = xs.shape
    tm = MOE_TILE
    per_tile = tm // DN_BLOCK
    n_tiles = t // tm if final_gain is None else n_lat_tiles
    fin = jnp.ones((1, d), F32) if final_gain is None else final_gain.reshape(1, d)
    row = lambda i: (i, 0)
    grp = lambda i: (_group_of_block(i * per_tile, n_lat_tiles * per_tile, blocks_per_batch), 0, 0)
    return pl.pallas_call(
        functools.partial(_combine_kernel, final_norm=final_gain is not None),
        out_shape=jax.ShapeDtypeStruct((n_tiles * tm, d), F32),
        grid=(n_tiles,),
        in_specs=[pl.BlockSpec((TOP_K, tm), lambda i: (0, i)),
                  pl.BlockSpec(memory_space=pl.ANY),
                  pl.BlockSpec((tm, d), row),
                  pl.BlockSpec((tm, TOP_K), row),
                  pl.BlockSpec((1, 1, d), grp),
                  pl.BlockSpec((1, d), lambda i: (0, 0))],
        out_specs=pl.BlockSpec((tm, d), row),
        scratch_shapes=[pltpu.SMEM((TOP_K, tm), jnp.int32),
                        pltpu.VMEM((tm, d), F32), pltpu.VMEM((tm, d), F32),
                        pltpu.SemaphoreType.DMA, pltpu.SemaphoreType.DMA],
        compiler_params=pltpu.CompilerParams(
            dimension_semantics=("arbitrary",), vmem_limit_bytes=VMEM_LIMIT_BYTES),
        name="moe_combine",
    )(dest, ys, xs, weight_cols, gate3, fin)


def _moe_layer(xs, gain, shift3, scale3, gate3, w_router, router_bias, w_gate, w_up, w_down, layer,
               n_lat_tiles, blocks_per_batch, final_gain=None):
    t = xs.shape[0]
    f_packed, expert, weight, rank, counts = _moe_route(xs, gain, shift3, scale3, w_router, router_bias,
                                                        n_lat_tiles, blocks_per_batch)
    counts = counts[:, 0].astype(jnp.int32)
    padded = (counts + MOE_BLOCK - 1) // MOE_BLOCK * MOE_BLOCK
    pend = jnp.cumsum(padded)
    pstart = pend - padded
    n_blocks = -(-(t * TOP_K) // MOE_BLOCK) + N_EXPERTS
    block_start = jnp.arange(n_blocks, dtype=jnp.int32) * MOE_BLOCK
    block_expert = jnp.minimum(jnp.sum(pend[None, :] <= block_start[:, None], axis=1),
                               N_EXPERTS - 1).astype(jnp.int32)
    n_active = (pend[-1:] // MOE_BLOCK).astype(jnp.int32)
    is_e = expert[..., None] == jnp.arange(N_EXPERTS, dtype=jnp.int32)
    dest = rank + jnp.sum(jnp.where(is_e, pstart, 0), axis=-1)
    xs_sorted = _dispatch(f_packed, dest, n_blocks * MOE_BLOCK)
    ys = _experts(xs_sorted, block_expert, n_active, w_gate, w_up, w_down, layer)
    return _combine(ys, dest, weight.T, xs, gate3, n_lat_tiles, blocks_per_batch, final_gain)


MOE_CHUNK = 8
MOE_LB = 1280
assert MOE_LB >= MOE_TILE * TOP_K + N_EXPERTS * (MOE_CHUNK - 1) and MOE_LB % 128 == 0
N_CHUNKS = MOE_LB // MOE_CHUNK
TAB_W = 256
assert TAB_W >= N_CHUNKS


def _moe_route_kernel(x_ref, gain_ref, shift_ref, scale_ref, wr_ref, rb_ref, tri_ref, lt_ref,
                      f_ref, pos_ref, w_ref, cnt_ref):
    tm, d = x_ref.shape
    ne, epg, ng = N_EXPERTS, EXPERTS_PER_GROUP, N_GROUPS
    f = _modulated(x_ref[...], gain_ref[...], shift_ref[0], scale_ref[0])
    f_ref[...] = f.astype(BF16)

    logits = lax.dot_general(wr_ref[...], f, (((1,), (1,)), ((), ())),
                             precision=lax.Precision.HIGHEST, preferred_element_type=F32)
    scores = jax.nn.sigmoid(logits)
    biased = scores + rb_ref[...]
    s = [scores[j * ng:(j + 1) * ng] for j in range(epg)]
    c = [biased[j * ng:(j + 1) * ng] for j in range(epg)]
    hi01, lo01 = jnp.maximum(c[0], c[1]), jnp.minimum(c[0], c[1])
    hi23, lo23 = jnp.maximum(c[2], c[3]), jnp.minimum(c[2], c[3])
    gscore = jnp.maximum(hi01, hi23) + jnp.maximum(jnp.minimum(hi01, hi23), jnp.maximum(lo01, lo23))
    gi = lax.broadcasted_iota(jnp.int32, (ng, tm), 0)
    gmax = jnp.max(gscore, axis=0, keepdims=True)
    grp = jnp.min(jnp.where(gscore == gmax, gi, ng), axis=0, keepdims=True)
    sel = gi == grp
    cv = [jnp.sum(jnp.where(sel, t, 0.0), axis=0, keepdims=True) for t in c]
    sv = [jnp.sum(jnp.where(sel, t, 0.0), axis=0, keepdims=True) for t in s]

    def pick(excluded):
        best = jnp.full((1, tm), -jnp.inf, F32)
        idx = jnp.zeros((1, tm), jnp.int32)
        val = jnp.zeros((1, tm), F32)
        for j in range(epg):
            cand = cv[j] if excluded is None else jnp.where(excluded == j, -jnp.inf, cv[j])
            take = cand > best
            best = jnp.where(take, cand, best)
            idx = jnp.where(take, j, idx)
            val = jnp.where(take, sv[j], val)
        return idx, val

    i1, v1 = pick(None)
    i2, v2 = pick(i1)
    wsum = v1 + v2
    w_ref[0:1, :] = v1 / wsum
    w_ref[1:2, :] = v2 / wsum

    ei = lax.broadcasted_iota(jnp.int32, (ne, tm), 0)
    oh1 = ei == grp * epg + i1
    oh2 = ei == grp * epg + i2
    tri = tri_ref[...]
    pre1 = jnp.dot(oh1.astype(BF16), tri, preferred_element_type=F32)
    pre2 = jnp.dot(oh2.astype(BF16), tri, preferred_element_type=F32)
    tot1 = pre1[:, tm - 1:tm]
    tot = tot1 + pre2[:, tm - 1:tm]
    seg = jnp.floor((tot + (MOE_CHUNK - 1)) * (1.0 / MOE_CHUNK)) * MOE_CHUNK
    off = jnp.dot(lt_ref[...], jnp.broadcast_to(seg, (ne, 128)).astype(BF16),
                  preferred_element_type=F32)[:, 0:1]
    p1 = jnp.sum(jnp.where(oh1, off + pre1 - 1.0, 0.0), axis=0, keepdims=True)
    p2 = jnp.sum(jnp.where(oh2, off + tot1 + pre2 - 1.0, 0.0), axis=0, keepdims=True)
    pos_ref[0:1, :] = p1.astype(jnp.int32)
    pos_ref[1:2, :] = p2.astype(jnp.int32)
    cnt_ref[0] = jnp.broadcast_to(tot, (ne, 128))


def _moe_route(xs, gain, shift3, scale3, w_router, router_bias, n_lat_tiles, blocks_per_batch):
    t, d = xs.shape
    tm = MOE_TILE
    ne = N_EXPERTS
    per_tile = tm // DN_BLOCK
    row = lambda i: (i, 0)
    col = lambda i: (0, i)
    fixed = lambda i: (0, 0)
    grp = lambda i: (_group_of_block(i * per_tile, n_lat_tiles * per_tile, blocks_per_batch), 0, 0)
    tri = jnp.asarray(np.triu(np.ones((tm, tm), np.float32)), BF16)
    lt = jnp.asarray(np.tril(np.ones((ne, ne), np.float32), -1), BF16)
    perm = np.arange(ne).reshape(N_GROUPS, EXPERTS_PER_GROUP).T.reshape(-1)
    return pl.pallas_call(
        _moe_route_kernel,
        out_shape=[jax.ShapeDtypeStruct((t, d), BF16),
                   jax.ShapeDtypeStruct((TOP_K, t), jnp.int32),
                   jax.ShapeDtypeStruct((TOP_K, t), F32),
                   jax.ShapeDtypeStruct((t // tm, ne, 128), F32)],
        grid=(t // tm,),
        in_specs=[pl.BlockSpec((tm, d), row),
                  pl.BlockSpec((1, d), fixed),
                  pl.BlockSpec((1, 1, d), grp),
                  pl.BlockSpec((1, 1, d), grp),
                  pl.BlockSpec((ne, d), fixed),
                  pl.BlockSpec((ne, 1), fixed),
                  pl.BlockSpec((tm, tm), fixed),
                  pl.BlockSpec((ne, ne), fixed)],
        out_specs=[pl.BlockSpec((tm, d), row),
                   pl.BlockSpec((TOP_K, tm), col),
                   pl.BlockSpec((TOP_K, tm), col),
                   pl.BlockSpec((1, ne, 128), lambda i: (i, 0, 0))],
        compiler_params=pltpu.CompilerParams(
            dimension_semantics=("arbitrary",), vmem_limit_bytes=VMEM_LIMIT_BYTES),
        name="moe_route",
    )(xs, gain.reshape(1, d), shift3, scale3, w_router.T[perm], router_bias[perm].reshape(ne, 1), tri, lt)


def _chunk_row(j):
    return j * MOE_CHUNK if isinstance(j, int) else pl.multiple_of(j * MOE_CHUNK, MOE_CHUNK)


def _chunk_issue(tab_smem, tab_row, make_copy):
    def issue_pair(jj, n):
        for priority in range(2):
            j = jj * 2 + priority
            dst = tab_smem[tab_row, j]

            @pl.when(dst >= 0)
            def _():
                make_copy(j, pl.multiple_of(dst, MOE_CHUNK)).start(priority=priority)

            n = n + (dst >= 0).astype(jnp.int32)
        return n

    return lax.fori_loop(0, N_CHUNKS // 2, issue_pair, jnp.int32(0), unroll=4)


def _chunk_drain(n, make_copy):
    def drain(j, carry):
        make_copy(0, 0).wait()
        return carry

    lax.fori_loop(0, n, drain, 0)


def _moe_dispatch_kernel(tab_ref, pos_ref, f_ref, xs_in_ref, xs_ref, tab_smem, cnt_smem, loc_ref, sem,
                         idx_sem, *, n_steps):
    del xs_in_ref
    tm, d = f_ref.shape
    i = pl.program_id(0)
    slot = i % 2
    idx_copy = pltpu.make_async_copy(tab_ref.at[0], tab_smem, idx_sem)
    idx_copy.start()
    r = lax.broadcasted_iota(jnp.int32, (tm, MOE_LB), 1)
    p = pos_ref[...]
    onehot = ((p[:, 0:1] == r) | (p[:, 1:2] == r)).astype(BF16)
    loc = _dotb_tn(onehot, f_ref[...])
    bits = pltpu.bitcast(loc, jnp.uint32)
    half = d // PACK
    loc_ref[slot] = (bits[:, :half] >> 16) | (bits[:, half:] & jnp.uint32(0xFFFF0000))
    idx_copy.wait()

    def copy_from(buf):
        def make_copy(j, dst):
            return pltpu.make_async_copy(loc_ref.at[buf, pl.ds(_chunk_row(j), MOE_CHUNK)],
                                         xs_ref.at[pl.ds(dst, MOE_CHUNK)], sem.at[buf])
        return make_copy

    n = _chunk_issue(tab_smem, 0, copy_from(slot))
    cnt_smem[slot] = n

    @pl.when(i > 0)
    def _():
        _chunk_drain(cnt_smem[1 - slot], copy_from(1 - slot))

    @pl.when(i == n_steps - 1)
    def _():
        _chunk_drain(n, copy_from(slot))


def _moe_dispatch(f, pos_cols, table, slots):
    t, d = f.shape
    tm = MOE_TILE
    n_slots, wd = slots.shape
    return pl.pallas_call(
        functools.partial(_moe_dispatch_kernel, n_steps=t // tm),
        out_shape=jax.ShapeDtypeStruct((n_slots, wd), jnp.uint32),
        grid=(t // tm,),
        in_specs=[pl.BlockSpec((1, 1, TAB_W), lambda i: (i, 0, 0)),
                  pl.BlockSpec((tm, TOP_K), lambda i: (i, 0)),
                  pl.BlockSpec((tm, d), lambda i: (i, 0)),
                  pl.BlockSpec(memory_space=pl.ANY)],
        out_specs=pl.BlockSpec(memory_space=pl.ANY),
        scratch_shapes=[pltpu.SMEM((1, TAB_W), jnp.int32),
                        pltpu.SMEM((2,), jnp.int32),
                        pltpu.VMEM((2, MOE_LB, wd), jnp.uint32),
                        pltpu.SemaphoreType.DMA((2,)), pltpu.SemaphoreType.DMA],
        input_output_aliases={3: 0},
        compiler_params=pltpu.CompilerParams(
            dimension_semantics=("arbitrary",), vmem_limit_bytes=VMEM_LIMIT_BYTES),
        name="moe_dispatch",
    )(table, pos_cols, f, slots)


def _moe_combine_kernel(tab_ref, nxt_ref, pos_ref, w_ref, y_ref, x_ref, gate_ref, fin_ref, o_ref, tab_smem,
                        cnt_smem, yloc_ref, sem, idx_sem, *, final_norm, n_steps):
    tm = x_ref.shape[0]
    i = pl.program_id(0)
    slot = i % 2

    def fetch(table_block, buf):
        idx_copy = pltpu.make_async_copy(table_block.at[0], tab_smem.at[pl.ds(buf, 1)], idx_sem)
        idx_copy.start()
        idx_copy.wait()
        cnt_smem[buf] = _chunk_issue(tab_smem, buf, copy_into(buf))

    def copy_into(buf):
        def make_copy(j, src):
            return pltpu.make_async_copy(y_ref.at[pl.ds(src, MOE_CHUNK)],
                                         yloc_ref.at[buf, pl.ds(_chunk_row(j), MOE_CHUNK)], sem.at[buf])
        return make_copy

    @pl.when(i == 0)
    def _():
        yloc_ref[...] = jnp.zeros_like(yloc_ref)
        fetch(tab_ref, 0)

    @pl.when(i + 1 < n_steps)
    def _():
        fetch(nxt_ref, 1 - slot)

    _chunk_drain(cnt_smem[slot], copy_into(slot))
    r = lax.broadcasted_iota(jnp.int32, (tm, MOE_LB), 1)
    p = pos_ref[...]
    w = w_ref[...]
    wmat = jnp.where(p[:, 0:1] == r, w[:, 0:1], 0.0) + jnp.where(p[:, 1:2] == r, w[:, 1:2], 0.0)
    packed = yloc_ref[slot]
    y_lo = pltpu.bitcast(packed << 16, F32)
    y_hi = pltpu.bitcast(packed & jnp.uint32(0xFFFF0000), F32)
    moe = jnp.concatenate([_dotb(wmat, y_lo), _dotb(wmat, y_hi)], axis=-1)
    out = x_ref[...] + gate_ref[0] * moe
    if final_norm:
        out = out * lax.rsqrt(jnp.mean(out * out, axis=-1, keepdims=True) + EPS) * fin_ref[...]
    o_ref[...] = out


def _moe_combine(ys, pos_cols, weight_cols, table, xs, gate3, n_lat_tiles, blocks_per_batch, final_gain=None):
    t, d = xs.shape
    tm = MOE_TILE
    per_tile = tm // DN_BLOCK
    n_tiles = t // tm if final_gain is None else n_lat_tiles
    fin = jnp.ones((1, d), F32) if final_gain is None else final_gain.reshape(1, d)
    row = lambda i: (i, 0)
    grp = lambda i: (_group_of_block(i * per_tile, n_lat_tiles * per_tile, blocks_per_batch), 0, 0)
    last = table.shape[0] - 1
    return pl.pallas_call(
        functools.partial(_moe_combine_kernel, final_norm=final_gain is not None, n_steps=n_tiles),
        out_shape=jax.ShapeDtypeStruct((n_tiles * tm, d), F32),
        grid=(n_tiles,),
        in_specs=[pl.BlockSpec((1, 1, TAB_W), lambda i: (i, 0, 0)),
                  pl.BlockSpec((1, 1, TAB_W), lambda i: (jnp.minimum(i + 1, last), 0, 0)),
                  pl.BlockSpec((tm, TOP_K), row),
                  pl.BlockSpec((tm, TOP_K), row),
                  pl.BlockSpec(memory_space=pl.ANY),
                  pl.BlockSpec((tm, d), row),
                  pl.BlockSpec((1, 1, d), grp),
                  pl.BlockSpec((1, d), lambda i: (0, 0))],
        out_specs=pl.BlockSpec((tm, d), row),
        scratch_shapes=[pltpu.SMEM((2, TAB_W), jnp.int32),
                        pltpu.SMEM((2,), jnp.int32),
                        pltpu.VMEM((2, MOE_LB, d // PACK), jnp.uint32),
                        pltpu.SemaphoreType.DMA((2,)), pltpu.SemaphoreType.DMA],
        compiler_params=pltpu.CompilerParams(
            dimension_semantics=("arbitrary",), vmem_limit_bytes=VMEM_LIMIT_BYTES),
        name="moe_combine",
    )(table, table, pos_cols, weight_cols, ys, xs, gate3, fin)


def _moe_layer(xs, gain, shift3, scale3, gate3, w_router, router_bias, w_gate, w_up, w_down, layer,
               n_lat_tiles, blocks_per_batch, final_gain=None, slots=None):
    t = xs.shape[0]
    n_tiles = t // MOE_TILE
    f, pos, weight, cnt = _moe_route(xs, gain, shift3, scale3, w_router, router_bias,
                                     n_lat_tiles, blocks_per_batch)
    seg = (cnt[:, :, 0].astype(jnp.int32) + MOE_CHUNK - 1) // MOE_CHUNK * MOE_CHUNK
    loc_end = jnp.cumsum(seg, axis=1)
    loc_off = loc_end - seg
    padded = (jnp.sum(seg, axis=0) + MOE_BLOCK - 1) // MOE_BLOCK * MOE_BLOCK
    pend = jnp.cumsum(padded)
    seg_start = (pend - padded)[None, :] + jnp.cumsum(seg, axis=0) - seg
    n_blocks = -(-(t * TOP_K + n_tiles * N_EXPERTS * (MOE_CHUNK - 1)) // MOE_BLOCK) + N_EXPERTS
    block_start = jnp.arange(n_blocks, dtype=jnp.int32) * MOE_BLOCK
    block_expert = jnp.minimum(jnp.sum(pend[None, :] <= block_start[:, None], axis=1),
                               N_EXPERTS - 1).astype(jnp.int32)
    n_active = (pend[-1:] // MOE_BLOCK).astype(jnp.int32)
    row0 = jnp.arange(N_CHUNKS, dtype=jnp.int32) * MOE_CHUNK
    e_of = jnp.sum(loc_end[:, None, :] <= row0[None, :, None], axis=-1)
    is_e = e_of[..., None] == jnp.arange(N_EXPERTS, dtype=jnp.int32)
    shift = jnp.sum(jnp.where(is_e, (seg_start - loc_off)[:, None, :], 0), axis=-1)
    table = jnp.where(e_of < N_EXPERTS, row0[None, :] + shift, -1)
    table = jnp.pad(table, ((0, 0), (0, TAB_W - N_CHUNKS)), constant_values=-1).reshape(n_tiles, 1, TAB_W)

    pos_cols = pos.T
    if slots is None:
        slots = jnp.zeros((n_blocks * MOE_BLOCK, f.shape[1] // PACK), jnp.uint32)
    xs_sorted = _moe_dispatch(f, pos_cols, table, slots)
    ys = _experts(xs_sorted, block_expert, n_active, w_gate, w_up, w_down, layer)
    out = _moe_combine(ys, pos_cols, weight.T, table, xs, gate3, n_lat_tiles, blocks_per_batch, final_gain)
    return out, xs_sorted


def _rmsnorm(x, gain):
    y = x * lax.rsqrt(jnp.mean(x * x, axis=-1, keepdims=True) + EPS)
    return y * gain


def _modulate(x, gain, shift, scale):
    return _rmsnorm(x, gain) * (1 + scale) + shift


def _l2norm(t):
    return t * lax.rsqrt(jnp.sum(t * t, axis=-1, keepdims=True) + EPS)


def _short_conv(x, w, on_grid):
    b, l, ch = x.shape
    xs = x.reshape(b, l // GRID_W, GRID_W, ch) if on_grid else x.reshape(b, 1, l, ch)
    n = xs.shape[2]
    xp = jnp.pad(xs, ((0, 0), (0, 0), (1, 1), (0, 0)))
    y = w[0] * xp[:, :, 0:n] + w[1] * xp[:, :, 1:n + 1] + w[2] * xp[:, :, 2:n + 2]
    return y.reshape(b, l, ch)


def _gated_delta_chunked(q, k, v, g, beta, s0):
    b, h, l, dk = q.shape
    dv = v.shape[-1]
    c = DN_CHUNK
    n = l // c
    q = q.reshape(b, h, n, c, dk)
    k = k.reshape(b, h, n, c, dk)
    v = v.reshape(b, h, n, c, dv)
    g = jnp.cumsum(g.reshape(b, h, n, c), axis=-1)
    beta = beta.reshape(b, h, n, c, 1)
    pos = jnp.arange(c)
    incl = pos[:, None] >= pos[None, :]
    strict = pos[:, None] > pos[None, :]
    decay = jnp.exp(jnp.where(incl, g[..., :, None] - g[..., None, :], -jnp.inf))
    kb = k * beta
    a_mat = jnp.einsum('bhnid,bhnjd->bhnij', kb, k) * jnp.where(strict, decay, 0.0)
    rhs = jnp.concatenate([v * beta, kb * jnp.exp(g)[..., None]], axis=-1)
    sol = lax.linalg.triangular_solve(a_mat + jnp.eye(c, dtype=a_mat.dtype), rhs,
                                      left_side=True, lower=True, unit_diagonal=True)
    u, w = sol[..., :dv], sol[..., dv:]
    attn = jnp.einsum('bhnid,bhnjd->bhnij', q, k) * decay
    g_last = g[..., -1:]
    q_dec = q * jnp.exp(g)[..., None]
    k_dec = k * jnp.exp(g_last - g)[..., None]

    def step(s, inp):
        qd, kd, uu, ww, at, gl = inp
        v_new = uu - jnp.einsum('bhck,bhkv->bhcv', ww, s)
        o = jnp.einsum('bhck,bhkv->bhcv', qd, s) + jnp.einsum('bhcs,bhsv->bhcv', at, v_new)
        s = s * jnp.exp(gl)[..., None] + jnp.einsum('bhck,bhcv->bhkv', kd, v_new)
        return s, o

    xs = tuple(jnp.moveaxis(t, 2, 0) for t in (q_dec, k_dec, u, w, attn, g_last))
    s_final, o = lax.scan(step, s0, xs)
    o = jnp.moveaxis(o, 0, 2).reshape(b, h, l, dv)
    return o, s_final


def _deltanet_mixer(p_ctx, p_lat, conv_w, a_log, dt_bias, out_norm):
    d = D_MODEL
    nh = DN_HEADS

    def project(p, on_grid):
        b, l, _ = p.shape
        qkv = jax.nn.silu(_short_conv(p[..., :3 * d], conv_w, on_grid))
        z = p[..., 3 * d:4 * d]
        a = p[..., 4 * d:4 * d + 2 * nh].reshape(b, l, 2, nh)
        bb = p[..., 4 * d + 2 * nh:].reshape(b, l, 2, nh)

        def heads(t):
            return jnp.transpose(t.reshape(b, l, nh, -1), (0, 2, 1, 3))

        q, k, v = (heads(t) for t in jnp.split(qkv, 3, axis=-1))
        q = _l2norm(q) * DN_DK ** -0.5
        k = _l2norm(k)
        g = -jnp.exp(a_log) * jax.nn.softplus(a + dt_bias)
        g = jnp.transpose(g, (2, 0, 3, 1))
        beta = jnp.transpose(jax.nn.sigmoid(bb), (2, 0, 3, 1))
        return q, k, v, g, beta, z

    def scan_both(q, k, v, g, beta, s_f, s_b):
        o_f, s_f = _gated_delta_chunked(q, k, v, g[0], beta[0], s_f)
        rev = lambda t: jnp.flip(t, axis=2)
        o_b, s_b = _gated_delta_chunked(rev(q), rev(k), rev(v), rev(g[1]), rev(beta[1]), s_b)
        return o_f + rev(o_b), s_f, s_b

    def finish(o, z):
        b, _, l, _ = o.shape
        o = jnp.transpose(o, (0, 2, 1, 3))
        o = o * lax.rsqrt(jnp.mean(o * o, axis=-1, keepdims=True) + EPS) * out_norm
        o = o * jax.nn.silu(z.reshape(b, l, nh, DN_DV))
        return o.reshape(b, l, d)

    qc, kc, vc, gc, bc, zc = project(p_ctx, False)
    s0 = jnp.zeros((p_ctx.shape[0], nh, DN_DK, DN_DV), F32)
    o_c, s_f, s_b = scan_both(qc, kc, vc, gc, bc, s0, s0)
    ql, kl, vl, gla, bl, zl = project(p_lat, True)
    o_l, _, _ = scan_both(ql, kl, vl, gla, bl, s_f, s_b)
    return finish(o_c, zc), finish(o_l, zl)


def _hyena_filters(l, w1, b1, freq, w2, b2, w3):
    pos = jnp.arange(l, dtype=F32)[:, None]
    t = pos / max(l - 1, 1)
    bands = jnp.linspace(1e-4, HY_BANDS - 1, HY_BANDS, dtype=F32)[None, :]
    ang = (2 * math.pi / l) * pos * bands
    feat = jnp.concatenate([t, jnp.cos(ang), -jnp.sin(ang)], axis=-1)
    hp = lax.Precision.HIGHEST
    hdn = jnp.sin(freq * (jnp.dot(feat, w1, precision=hp) + b1))
    hdn = jnp.sin(freq * (jnp.dot(hdn, w2, precision=hp) + b2))
    filt = jnp.dot(hdn, w3, precision=hp).reshape(l, HY_ORDER, 2, D_MODEL)
    deltas = jnp.abs(jnp.linspace(math.log(HY_TARGET) / HY_SLOW, math.log(HY_TARGET) / HY_FAST,
                                  D_MODEL, dtype=F32))
    window = jnp.exp(-t * deltas[None, :])
    return filt * window[:, None, None, :]


def _two_sided_fftconv(u, h_fwd, h_bwd):
    l = u.shape[1]
    k = jnp.concatenate([h_fwd, jnp.zeros_like(h_fwd[:1]), jnp.flip(h_bwd[1:], axis=0)], axis=0)
    kf = jnp.fft.rfft(k, axis=0)
    uf = jnp.fft.rfft(u, n=2 * l, axis=1)
    return jnp.fft.irfft(uf * kf[None], n=2 * l, axis=1)[:, :l]


def _hyena_stream(p, on_grid, conv_w, f_w1, f_b1, f_freq, f_w2, f_b2, f_w3, bias):
    l = p.shape[1]
    p = _short_conv(p, conv_w, on_grid)
    v, x1, x2 = jnp.split(p, 3, axis=-1)
    filt = _hyena_filters(l, f_w1, f_b1, f_freq, f_w2, f_b2, f_w3)
    z = v
    for n, gate in enumerate((x1, x2)):
        conv = _two_sided_fftconv(z, filt[:, n, 0], filt[:, n, 1])
        z = gate * (conv + bias[n] * z)
    return z


def _shortconv_stream(p, on_grid, conv_w):
    bg, cg, xin = jnp.split(p, 3, axis=-1)
    return bg * _short_conv(cg * xin, conv_w, on_grid)


def _route(h, w_router, router_bias):
    t = h.shape[0]
    scores = jax.nn.sigmoid(jnp.dot(h, w_router, precision=lax.Precision.HIGHEST))
    choice = (scores + router_bias).reshape(t, N_GROUPS, EXPERTS_PER_GROUP)
    group_score = lax.top_k(choice, GROUP_SCORE_K)[0].sum(-1)
    group = jnp.argmax(group_score, axis=-1)
    in_group = jnp.take_along_axis(choice, group[:, None, None], axis=1)[:, 0]
    local = lax.top_k(in_group, TOP_K)[1]
    expert = group[:, None] * EXPERTS_PER_GROUP + local
    weight = jnp.take_along_axis(scores, expert, axis=1)
    weight = weight / jnp.sum(weight, axis=-1, keepdims=True)
    return expert.astype(jnp.int32), weight


def _moe_ffn(x, w_router, router_bias, w_gate, w_up, w_down):
    t, d = x.shape
    expert, weight = _route(x, w_router, router_bias)
    a = t * TOP_K
    e_flat = expert.reshape(-1)
    order = jnp.argsort(e_flat)
    e_sorted = e_flat[order]
    tok_sorted = (order // TOP_K).astype(jnp.int32)
    counts = jnp.zeros((N_EXPERTS,), jnp.int32).at[e_flat].add(1)
    start = jnp.cumsum(counts) - counts
    padded = (counts + MOE_BLOCK - 1) // MOE_BLOCK * MOE_BLOCK
    pend = jnp.cumsum(padded)
    pstart = pend - padded
    dest = pstart[e_sorted] + (jnp.arange(a, dtype=jnp.int32) - start[e_sorted])
    n_blocks = -(-a // MOE_BLOCK) + N_EXPERTS
    n_slots = n_blocks * MOE_BLOCK
    slot_tok = jnp.full((n_slots,), t, jnp.int32).at[dest].set(tok_sorted)
    block_start = jnp.arange(n_blocks, dtype=jnp.int32) * MOE_BLOCK
    block_expert = jnp.minimum(jnp.searchsorted(pend, block_start, side='right'),
                               N_EXPERTS - 1).astype(jnp.int32)
    x_pad = jnp.concatenate([x.astype(BF16), jnp.zeros((1, d), BF16)], axis=0)
    xs = x_pad[slot_tok]
    ys = _expert_ffn(xs, block_expert, jnp.ones((n_slots,), F32), w_gate, w_up, w_down)
    slot_of = jnp.zeros((a,), jnp.int32).at[order].set(dest).reshape(t, TOP_K)
    out = weight[:, 0:1] * ys[slot_of[:, 0]] + weight[:, 1:2] * ys[slot_of[:, 1]]
    return out


def _sc_layer_kernel(x_ref, gain_ref, shift_ref, scale_ref, win_ref, cw_ref, wout_ref, gate_ref, o_ref, *,
                     n_lat_tiles):
    d = D_MODEL
    tm = x_ref.shape[0]
    seg = jnp.where(pl.program_id(0) >= n_lat_tiles, CTX_LEN, GRID_W)
    pos = lax.broadcasted_iota(jnp.int32, (tm, 1), 0) & (seg - 1)
    hb = _modulated(x_ref[...], gain_ref[...], shift_ref[0], scale_ref[0]).astype(BF16)
    u = (jnp.dot(hb, win_ref[:, d:2 * d], preferred_element_type=F32)
         * jnp.dot(hb, win_ref[:, 2 * d:], preferred_element_type=F32))
    prev = jnp.where(pos != 0, pltpu.roll(u, 1, axis=0), 0.0)
    nxt = jnp.where(pos != seg - 1, pltpu.roll(u, tm - 1, axis=0), 0.0)
    cw = cw_ref[...]
    y = jnp.dot(hb, win_ref[:, :d], preferred_element_type=F32) * (
        cw[0:1] * prev + cw[1:2] * u + cw[2:3] * nxt)
    o_ref[...] = x_ref[...] + gate_ref[0] * jnp.dot(y.astype(BF16), wout_ref[...],
                                                    preferred_element_type=F32)


def _shortconv_layer(xs, gain, shift3, scale3, w_in, conv_w, w_out, gate3, n_lat_tiles, blocks_per_batch):
    t, d = xs.shape
    tm = HY_TOK_TILE
    per_tile = tm // DN_BLOCK
    row = lambda i: (i, 0)
    fixed = lambda i: (0, 0)
    grp = lambda i: (_group_of_block(i * per_tile, n_lat_tiles * per_tile, blocks_per_batch), 0, 0)
    return pl.pallas_call(
        functools.partial(_sc_layer_kernel, n_lat_tiles=n_lat_tiles),
        out_shape=jax.ShapeDtypeStruct((t, d), F32),
        grid=(t // tm,),
        in_specs=[pl.BlockSpec((tm, d), row),
                  pl.BlockSpec((1, d), fixed),
                  pl.BlockSpec((1, 1, d), grp),
                  pl.BlockSpec((1, 1, d), grp),
                  pl.BlockSpec((d, 3 * d), fixed),
                  pl.BlockSpec((3, d), fixed),
                  pl.BlockSpec((d, d), fixed),
                  pl.BlockSpec((1, 1, d), grp)],
        out_specs=pl.BlockSpec((tm, d), row),
        compiler_params=pltpu.CompilerParams(
            dimension_semantics=("arbitrary",), vmem_limit_bytes=VMEM_LIMIT_BYTES),
        name="shortconv_layer",
    )(xs, gain.reshape(1, d), shift3, scale3, w_in.astype(BF16), conv_w, w_out.astype(BF16), gate3)


def kernel(x, c, ctx, c_ctx, ada_w, ada_b, norm_mix, norm_ffn, norm_final, dn_w_in, dn_conv, dn_a_log,
           dn_dt_bias, dn_out_norm, dn_w_out, hy_w_in, hy_conv, hy_f_w1, hy_f_b1, hy_f_freq, hy_f_w2,
           hy_f_b2, hy_f_w3, hy_bias, hy_w_out, sc_w_in, sc_conv, sc_w_out, w_router, router_bias,
           moe_w_gate, moe_w_up, moe_w_down):
    d = D_MODEL
    bsz, seq, _ = x.shape
    n_ctx = bsz * CTX_LEN
    n_lat = bsz * seq
    silu_c = jax.nn.silu(c)
    silu_cc = jax.nn.silu(c_ctx)
    hp = lax.Precision.HIGHEST

    xs = jnp.concatenate([x.reshape(n_lat, d), ctx.reshape(n_ctx, d)], axis=0)
    n_lat_blocks, blocks_per_batch = n_lat // DN_BLOCK, seq // DN_BLOCK
    slots = None

    for i in range(DEPTH):
        kind, j = i % N_MIXERS, i // N_MIXERS
        ml = jnp.split(jnp.dot(silu_c, ada_w[i], precision=hp) + ada_b[i], N_MOD, axis=-1)
        mc = jnp.split(jnp.dot(silu_cc, ada_w[i], precision=hp) + ada_b[i], N_MOD, axis=-1)
        mod = [jnp.concatenate([mc[m][None], ml[m]], axis=0)[:, None, :] for m in range(N_MOD)]
        if kind == 0:
            xs = _deltanet_layer(xs, norm_mix[i], mod[0], mod[1], dn_w_in[j], dn_conv[j], dn_a_log[j],
                                 dn_dt_bias[j], dn_out_norm[j], dn_w_out[j], mod[2], n_lat_blocks,
                                 blocks_per_batch)
        elif kind == 1:
            xs = _hyena_layer(xs, norm_mix[i], mod[0], mod[1], hy_w_in[j], hy_conv[j], hy_f_w1[j], hy_f_b1[j],
                              hy_f_freq[j], hy_f_w2[j], hy_f_b2[j], hy_f_w3[j], hy_bias[j], hy_w_out[j],
                              mod[2], bsz, seq)
        else:
            xs = _shortconv_layer(xs, norm_mix[i], mod[0], mod[1], sc_w_in[j], sc_conv[j], sc_w_out[j],
                                  mod[2], n_lat // HY_TOK_TILE, blocks_per_batch)
        xs, slots = _moe_layer(xs, norm_ffn[i], mod[3], mod[4], mod[5], w_router, router_bias,
                               moe_w_gate, moe_w_up, moe_w_down, i, n_lat // MOE_TILE, blocks_per_batch,
                               norm_final if i == DEPTH - 1 else None, slots)
    return xs.reshape(bsz, seq, d)
```

```python
import functools
import math

import numpy as np
import jax
import jax.numpy as jnp
from jax import lax
from jax.experimental import pallas as pl
from jax.experimental.pallas import tpu as pltpu

D_MODEL = 1024
DEPTH = 4
CTX_LEN = 256
GRID_W = 64
N_MIXERS = 3
EPS = 1e-6
N_MOD = 6

DN_HEADS = 8
DN_DK = D_MODEL // DN_HEADS
DN_DV = D_MODEL // DN_HEADS
DN_CHUNK = 64

HY_ORDER = 2
HY_BANDS = 16
HY_TARGET = 1e-2
HY_FAST = 0.3
HY_SLOW = 1.5

N_EXPERTS = 32
N_GROUPS = 8
EXPERTS_PER_GROUP = N_EXPERTS // N_GROUPS
GROUP_SCORE_K = 2
TOP_K = 2
D_EXPERT = 512
MOE_BLOCK = 512

F32 = jnp.float32
BF16 = jnp.bfloat16

VMEM_LIMIT_BYTES = 48 * 1024 * 1024


DN_BLOCK = CTX_LEN
DN_HB = DN_HEADS
N_CHUNKS_PER_BLOCK = DN_BLOCK // DN_CHUNK


def _group_of_block(i, n_lat_blocks, blocks_per_batch):
    return jnp.where(i >= n_lat_blocks, 0, 1 + i // blocks_per_batch)


def _modulated(x, gain, shift, scale):
    y = x * lax.rsqrt(jnp.mean(x * x, axis=-1, keepdims=True) + EPS) * gain
    return y * (1 + scale) + shift


def _dn_inproj_kernel(x_ref, gain_ref, shift_ref, scale_ref, w_ref, wab_ref, cw_ref, alog_ref, dtb_ref,
                      q_ref, k_ref, v_ref, z_ref, gate_ref, gate_t_ref, *, n_lat_blocks):
    i = pl.program_id(0)
    nrow = DN_BLOCK
    d = D_MODEL
    pair = 2 * DN_DK
    seg = jnp.where(i >= n_lat_blocks, CTX_LEN, GRID_W)
    r = lax.broadcasted_iota(jnp.int32, (nrow, 1), 0)
    pos = r & (seg - 1)
    not_first = pos != 0
    not_last = pos != seg - 1
    h = _modulated(x_ref[...], gain_ref[...], shift_ref[0], scale_ref[0])
    hb = h.astype(BF16)
    outs = (q_ref, k_ref, v_ref)
    for part in range(3):
        for hp in range(d // pair):
            col = part * d + hp * pair
            x = jnp.dot(hb, w_ref[:, col:col + pair], preferred_element_type=F32)
            cw = cw_ref[:, col:col + pair]
            xp = jnp.where(not_first, pltpu.roll(x, 1, axis=0), 0.0)
            xn = jnp.where(not_last, pltpu.roll(x, nrow - 1, axis=0), 0.0)
            y = cw[0:1] * xp + cw[1:2] * x + cw[2:3] * xn
            y = y * jax.nn.sigmoid(y)
            for hh in range(2):
                yh = y[:, hh * DN_DK:(hh + 1) * DN_DK]
                if part < 2:
                    yh = yh * lax.rsqrt(jnp.sum(yh * yh, axis=-1, keepdims=True) + EPS)
                if part == 0:
                    yh = yh * DN_DK ** -0.5
                outs[part][:, hp * pair + hh * DN_DK:hp * pair + (hh + 1) * DN_DK] = yh
    for j in range(d // pair):
        z_ref[:, j * pair:(j + 1) * pair] = jnp.dot(hb, w_ref[:, 3 * d + j * pair:3 * d + (j + 1) * pair],
                                                    preferred_element_type=F32).astype(BF16)

    ab = jnp.dot(hb, wab_ref[...], preferred_element_type=F32)
    nd = 2 * DN_HEADS
    a = ab[:, :nd] + dtb_ref[...]
    softplus = jnp.maximum(a, 0.0) + jnp.log(1.0 + jnp.exp(-jnp.abs(a)))
    g = -jnp.exp(alog_ref[...]) * softplus
    beta = jax.nn.sigmoid(ab[:, nd:])
    cpos = r & (DN_CHUNK - 1)
    gp, gs = g, g
    sh = 1
    while sh < DN_CHUNK:
        gp = gp + jnp.where(cpos >= sh, pltpu.roll(gp, sh, axis=0), 0.0)
        gs = gs + jnp.where(cpos < DN_CHUNK - sh, pltpu.roll(gs, nrow - sh, axis=0), 0.0)
        sh *= 2
    colid = lax.broadcasted_iota(jnp.int32, (1, nd), 1)
    gates = jnp.concatenate([jnp.where(colid < DN_HEADS, gp, gs), beta], axis=1)
    gate_ref[...] = gates
    gate_t_ref[...] = gates.T


def _dn_inproj(xs, gain, shift3, scale3, w_in, conv_w, a_log, dt_bias, n_lat_blocks, blocks_per_batch):
    t, d = xs.shape
    nd = 2 * DN_HEADS
    row = lambda i: (i, 0)
    fixed = lambda i: (0, 0)
    grp = lambda i: (_group_of_block(i, n_lat_blocks, blocks_per_batch), 0, 0)
    return pl.pallas_call(
        functools.partial(_dn_inproj_kernel, n_lat_blocks=n_lat_blocks),
        out_shape=[jax.ShapeDtypeStruct((t, d), F32)] * 3 + [jax.ShapeDtypeStruct((t, d), BF16),
                                                              jax.ShapeDtypeStruct((t, 2 * nd), F32),
                                                              jax.ShapeDtypeStruct((2 * nd, t), F32)],
        grid=(t // DN_BLOCK,),
        in_specs=[pl.BlockSpec((DN_BLOCK, d), row),
                  pl.BlockSpec((1, d), fixed),
                  pl.BlockSpec((1, 1, d), grp),
                  pl.BlockSpec((1, 1, d), grp),
                  pl.BlockSpec((d, 4 * d), fixed),
                  pl.BlockSpec((d, 2 * nd), fixed),
                  pl.BlockSpec((3, 3 * d), fixed),
                  pl.BlockSpec((1, nd), fixed),
                  pl.BlockSpec((1, nd), fixed)],
        out_specs=[pl.BlockSpec((DN_BLOCK, d), row)] * 4 + [pl.BlockSpec((DN_BLOCK, 2 * nd), row),
                                                            pl.BlockSpec((2 * nd, DN_BLOCK), lambda i: (0, i))],
        compiler_params=pltpu.CompilerParams(
            dimension_semantics=("arbitrary",), vmem_limit_bytes=VMEM_LIMIT_BYTES),
        name="dn_inproj",
    )(xs, gain.reshape(1, d), shift3, scale3, w_in[:, :4 * d].astype(BF16), w_in[:, 4 * d:].astype(BF16), conv_w,
      a_log.reshape(1, nd), dt_bias.reshape(1, nd))


def _dotb(a, b):
    return jnp.dot(a.astype(BF16), b.astype(BF16), preferred_element_type=F32)


def _dotb_nt(a, b):
    return lax.dot_general(a.astype(BF16), b.astype(BF16), (((1,), (1,)), ((), ())),
                           preferred_element_type=F32)


def _dotb_tn(a, b):
    return lax.dot_general(a.astype(BF16), b.astype(BF16), (((0,), (0,)), ((), ())),
                           preferred_element_type=F32)


def _unit_tri_inverses(mats):
    c = DN_CHUNK
    assert len(mats) % 2 == 0
    ii = lax.broadcasted_iota(jnp.int32, (c, 2 * c), 0)
    lane = lax.broadcasted_iota(jnp.int32, (c, 2 * c), 1)
    jj = lane & (c - 1)
    left = lane < c

    def blockdiag(p):
        pb = p.astype(BF16)
        zero = jnp.zeros_like(pb)
        return jnp.concatenate([jnp.where(left, pb, zero), jnp.where(left, zero, pb)], axis=0)

    def mul(p, q):
        return jnp.dot(p.astype(BF16), blockdiag(q), preferred_element_type=F32)

    pairs = [jnp.concatenate([mats[i], mats[i + 1]], axis=1) for i in range(0, len(mats), 2)]
    eye = (ii == jj).astype(F32)
    diag8 = (ii >> 3) == (jj >> 3)
    n = [-jnp.where(diag8, a, 0.0) for a in pairs]
    n2 = [mul(x, x) for x in n]
    m = [eye + x for x in n]
    m = [x + mul(x, y) for x, y in zip(m, n2)]
    n4 = [mul(x, x) for x in n2]
    m = [x + mul(x, y) for x, y in zip(m, n4)]
    sh = 3
    while (1 << sh) < c:
        off = ((ii >> (sh + 1)) == (jj >> (sh + 1))) & ((ii >> sh) != (jj >> sh))
        cm = [mul(jnp.where(off, a, 0.0), x) for a, x in zip(pairs, m)]
        m = [x - mul(x, y) for x, y in zip(m, cm)]
        sh += 1
    return [half for x in m for half in (x[:, :c], x[:, c:])]


def _dn_scan_kernel(qf_ref, kf_ref, vf_ref, gcf_ref, grf_ref, qb_ref, kb_ref, vb_ref, gcb_ref, grb_ref,
                    of_ref, ob_ref, s_ref):
    @pl.when(pl.program_id(2) == 0)
    def _():
        s_ref[...] = jnp.zeros_like(s_ref)

    c = DN_CHUNK
    ncb = N_CHUNKS_PER_BLOCK
    ii = lax.broadcasted_iota(jnp.int32, (c, c), 0)
    jj = lax.broadcasted_iota(jnp.int32, (c, c), 1)
    incl = (ii >= jj, ii <= jj)
    strict = (ii > jj, ii < jj)
    dirs = ((qf_ref, kf_ref, vf_ref, gcf_ref, grf_ref, of_ref),
            (qb_ref, kb_ref, vb_ref, gcb_ref, grb_ref, ob_ref))
    items = [(d, hh, ci) for d in range(2) for hh in range(DN_HB) for ci in range(ncb)]

    def rows(ci):
        return slice(ci * c, (ci + 1) * c)

    def cols(hh):
        return slice(hh * DN_DK, (hh + 1) * DN_DK)

    q = [dirs[d][0][rows(ci), cols(hh)] for d, hh, ci in items]
    k = [dirs[d][1][rows(ci), cols(hh)] for d, hh, ci in items]
    v = [dirs[d][2][rows(ci), cols(hh)] for d, hh, ci in items]
    nh = DN_HEADS
    gc = [dirs[d][3][rows(ci), d * nh + hh:d * nh + hh + 1] for d, hh, ci in items]
    gr = [dirs[d][4][d * nh + hh:d * nh + hh + 1, rows(ci)] for d, hh, ci in items]
    beta = [dirs[d][3][rows(ci), (2 + d) * nh + hh:(2 + d) * nh + hh + 1] for d, hh, ci in items]

    decay = [jnp.where(incl[it[0]], jnp.exp(jnp.where(incl[it[0]], x - y, 0.0)), 0.0)
             for it, x, y in zip(items, gc, gr)]
    kb = [x * y for x, y in zip(k, beta)]
    a = [_dotb_nt(x, y) * jnp.where(strict[it[0]], z, 0.0) for it, x, y, z in zip(items, kb, k, decay)]
    attn = [_dotb_nt(x, y) * z for x, y, z in zip(q, k, decay)]
    tinv = _unit_tri_inverses(a)
    eg = [jnp.exp(x) for x in gc]
    uw = [_dotb(t, jnp.concatenate([x * y, z * e], axis=-1))
          for t, x, y, z, e in zip(tinv, v, beta, kb, eg)]
    g_last = [x[0:1] if it[0] else x[c - 1:c] for it, x in zip(items, gc)]
    wq = [jnp.concatenate([x[:, DN_DV:], y * e], axis=0) for x, y, e in zip(uw, q, eg)]
    k_dec = [x * jnp.exp(y - z) for x, y, z in zip(k, g_last, gc)]
    s_dec = [jnp.exp(x) for x in g_last]

    chains = [(d, hh) for d in range(2) for hh in range(DN_HB)]
    state = [s_ref[d, hh] for d, hh in chains]
    for step in range(ncb):
        cur = [items.index((d, hh, ncb - 1 - step if d else step)) for d, hh in chains]
        ws = [_dotb(wq[n], s) for n, s in zip(cur, state)]
        v_new = [uw[n][:, :DN_DV] - x[:c] for n, x in zip(cur, ws)]
        o = [x[c:] + _dotb(attn[n], y) for n, x, y in zip(cur, ws, v_new)]
        state = [s * s_dec[n] + _dotb_tn(k_dec[n], y) for n, s, y in zip(cur, state, v_new)]
        for n, x in zip(cur, o):
            d, hh, ci = items[n]
            dirs[d][5][rows(ci), cols(hh)] = x.astype(BF16)
    for (d, hh), s in zip(chains, state):
        s_ref[d, hh] = s


def _dn_scan(q, k, v, gates, gates_t, n_lat_blocks, blocks_per_batch):
    t, d = q.shape
    bsz = n_lat_blocks // blocks_per_batch
    assert DN_HB == DN_HEADS
    ng = gates.shape[1]

    def blk_f(b, s):
        return jnp.where(s == 0, n_lat_blocks + b, b * blocks_per_batch + s - 1)

    def blk_b(b, s):
        return jnp.where(s == 0, n_lat_blocks + b, b * blocks_per_batch + blocks_per_batch - s)

    hw = DN_HB * DN_DK

    def specs(blk):
        return [pl.BlockSpec((DN_BLOCK, hw), lambda b, hg, s: (blk(b, s), hg))] * 3 + [
            pl.BlockSpec((DN_BLOCK, ng), lambda b, hg, s: (blk(b, s), 0)),
            pl.BlockSpec((ng, DN_BLOCK), lambda b, hg, s: (0, blk(b, s)))]

    return pl.pallas_call(
        _dn_scan_kernel,
        out_shape=[jax.ShapeDtypeStruct((t, d), BF16)] * 2,
        grid=(bsz, DN_HEADS // DN_HB, 1 + blocks_per_batch),
        in_specs=specs(blk_f) + specs(blk_b),
        out_specs=[pl.BlockSpec((DN_BLOCK, hw), lambda b, hg, s: (blk_f(b, s), hg)),
                   pl.BlockSpec((DN_BLOCK, hw), lambda b, hg, s: (blk_b(b, s), hg))],
        scratch_shapes=[pltpu.VMEM((2, DN_HB, DN_DK, DN_DV), F32)],
        compiler_params=pltpu.CompilerParams(
            dimension_semantics=("arbitrary", "arbitrary", "arbitrary"),
            vmem_limit_bytes=VMEM_LIMIT_BYTES),
        name="dn_scan",
    )(q, k, v, gates, gates_t, q, k, v, gates, gates_t)


def _dn_out_kernel(of_ref, ob_ref, z_ref, on_ref, w_ref, x_ref, gate_ref, o_ref):
    z = z_ref[...].astype(F32)
    zs = z * jax.nn.sigmoid(z)
    parts = []
    for h in range(DN_HEADS):
        cols = slice(h * DN_DV, (h + 1) * DN_DV)
        o = of_ref[:, cols].astype(F32) + ob_ref[:, cols].astype(F32)
        o = o * lax.rsqrt(jnp.mean(o * o, axis=-1, keepdims=True) + EPS)
        parts.append(o)
    y = jnp.concatenate(parts, axis=-1) * on_ref[...] * zs
    o_ref[...] = x_ref[...] + gate_ref[0] * jnp.dot(y.astype(BF16), w_ref[...],
                                                    preferred_element_type=F32)


def _dn_out(o_f, o_b, z, out_norm, w_out, xs, gate3, n_lat_blocks, blocks_per_batch):
    t, d = xs.shape
    row = lambda i: (i, 0)
    fixed = lambda i: (0, 0)
    grp = lambda i: (_group_of_block(i, n_lat_blocks, blocks_per_batch), 0, 0)
    return pl.pallas_call(
        _dn_out_kernel,
        out_shape=jax.ShapeDtypeStruct((t, d), F32),
        grid=(t // DN_BLOCK,),
        in_specs=[pl.BlockSpec((DN_BLOCK, d), row),
                  pl.BlockSpec((DN_BLOCK, d), row),
                  pl.BlockSpec((DN_BLOCK, d), row),
                  pl.BlockSpec((1, d), fixed),
                  pl.BlockSpec((d, d), fixed),
                  pl.BlockSpec((DN_BLOCK, d), row),
                  pl.BlockSpec((1, 1, d), grp)],
        out_specs=pl.BlockSpec((DN_BLOCK, d), row),
        compiler_params=pltpu.CompilerParams(
            dimension_semantics=("arbitrary",), vmem_limit_bytes=VMEM_LIMIT_BYTES),
        name="dn_out",
    )(o_f, o_b, z, jnp.tile(out_norm, DN_HEADS).reshape(1, d), w_out.astype(BF16), xs, gate3)


def _deltanet_layer(xs, gain, shift3, scale3, w_in, conv_w, a_log, dt_bias, out_norm, w_out, gate3,
                    n_lat_blocks, blocks_per_batch):
    q, k, v, z, gates, gates_t = _dn_inproj(xs, gain, shift3, scale3, w_in, conv_w, a_log, dt_bias,
                                            n_lat_blocks, blocks_per_batch)
    o_f, o_b = _dn_scan(q, k, v, gates, gates_t, n_lat_blocks, blocks_per_batch)
    return _dn_out(o_f, o_b, z, out_norm, w_out, xs, gate3, n_lat_blocks, blocks_per_batch)


HY_N2 = 256
HY_CB = 16
HY_TOK_TILE = 512
HY_FILT_TILE = 512


def _dft_constants(nr):
    n1, n2 = 2 * nr, HY_N2
    n = n1 * n2
    nk = -(-(nr + 1) // 8) * 8
    keep = np.arange(nk) <= nr
    k1 = np.where(keep, np.arange(nk), 0)
    f1 = np.exp(-2j * np.pi * np.outer(k1, np.arange(nr)) / n1) * keep[:, None]
    twice = np.where((k1 > 0) & (k1 < nr), 2.0, 1.0) * keep
    lhs_fwd = np.concatenate([f1.real, f1.imag], axis=0)
    lhs_inv = np.concatenate([f1.real.T * twice, f1.imag.T * twice], axis=1) / n
    tw = np.exp(-2j * np.pi * np.outer(k1, np.arange(n2)) / n)
    a2 = np.arange(n2)
    f2 = np.exp(-2j * np.pi * np.outer(a2, a2) / n2)
    w_fwd = np.block([[f2.real, f2.imag], [-f2.imag, f2.real]])
    w_inv = np.block([[f2.real, -f2.imag], [f2.imag, f2.real]])
    return (jnp.asarray(lhs_fwd, BF16), jnp.asarray(lhs_inv, BF16), jnp.asarray(tw.real, F32),
            jnp.asarray(tw.imag, F32), jnp.asarray(w_fwd, BF16), jnp.asarray(w_inv, BF16))


def _hy_dft(x3, lhs_fwd, twr, twi, w_fwd):
    n1 = twr.shape[0]
    a = [jnp.dot(lhs_fwd, x3[c].astype(BF16), preferred_element_type=F32) for c in range(x3.shape[0])]
    br = jnp.concatenate([t[:n1] * twr - t[n1:] * twi for t in a], axis=0)
    bi = jnp.concatenate([t[:n1] * twi + t[n1:] * twr for t in a], axis=0)
    b = jnp.concatenate([br, bi], axis=1)
    return jnp.dot(b.astype(BF16), w_fwd, preferred_element_type=F32)


def _hy_idft(p, cb, lhs_inv, twr, twi, w_inv):
    n1, n2 = twr.shape
    c = jnp.dot(p.astype(BF16), w_inv, preferred_element_type=F32)
    out = []
    for ch in range(cb):
        cr = c[ch * n1:(ch + 1) * n1, :n2]
        ci = c[ch * n1:(ch + 1) * n1, n2:]
        d = jnp.concatenate([cr * twr + ci * twi, ci * twr - cr * twi], axis=0)
        out.append(jnp.dot(lhs_inv, d.astype(BF16), preferred_element_type=F32))
    return out


def _hy_spectrum_kernel(hf_ref, hb_ref, lf_ref, twr_ref, twi_ref, wf_ref, o_ref):
    cb, nr, n2 = hf_ref.shape
    first = ((lax.broadcasted_iota(jnp.int32, (nr, n2), 0) == 0)
             & (lax.broadcasted_iota(jnp.int32, (nr, n2), 1) == 0))
    hb = jnp.where(first, 0.0, hb_ref[...])
    consts = (lf_ref[...], twr_ref[...], twi_ref[...], wf_ref[...])
    xf = _hy_dft(hf_ref[...], *consts)
    xb = _hy_dft(hb, *consts)
    o_ref[...] = jnp.concatenate([xf[:, :n2] + xb[:, :n2], xf[:, n2:] - xb[:, n2:]],
                                 axis=1).reshape(o_ref.shape)


def _hy_spectrum(filt, consts):
    d = D_MODEL
    l = filt.shape[1]
    nr = l // HY_N2
    lhs_fwd, _, twr, twi, w_fwd, _ = consts
    n1 = twr.shape[0]
    cpo = d // HY_CB
    fixed2 = lambda o, c: (0, 0)
    return pl.pallas_call(
        _hy_spectrum_kernel,
        out_shape=jax.ShapeDtypeStruct((HY_ORDER * d, n1, 2 * HY_N2), F32),
        grid=(HY_ORDER, cpo),
        in_specs=[pl.BlockSpec((HY_CB, nr, HY_N2), lambda o, c: (2 * o * cpo + c, 0, 0)),
                  pl.BlockSpec((HY_CB, nr, HY_N2), lambda o, c: ((2 * o + 1) * cpo + c, 0, 0)),
                  pl.BlockSpec(lhs_fwd.shape, fixed2),
                  pl.BlockSpec(twr.shape, fixed2),
                  pl.BlockSpec(twi.shape, fixed2),
                  pl.BlockSpec(w_fwd.shape, fixed2)],
        out_specs=pl.BlockSpec((HY_CB, n1, 2 * HY_N2), lambda o, c: (o * cpo + c, 0, 0)),
        compiler_params=pltpu.CompilerParams(
            dimension_semantics=("arbitrary", "arbitrary"), vmem_limit_bytes=VMEM_LIMIT_BYTES),
        name="hy_spectrum",
    )(filt.reshape(-1, nr, HY_N2), filt.reshape(-1, nr, HY_N2), lhs_fwd, twr, twi, w_fwd)


def _hy_conv_kernel(z_ref, g_ref, k_ref, bias_ref, lf_ref, li_ref, twr_ref, twi_ref, wf_ref, wi_ref,
                    o_ref):
    cb, nr, n2 = z_ref.shape
    twr, twi = twr_ref[...], twi_ref[...]
    z = z_ref[...]
    x = _hy_dft(z, lf_ref[...], twr, twi, wf_ref[...])
    kk = k_ref[...].reshape(x.shape)
    xr, xi, kr, ki = x[:, :n2], x[:, n2:], kk[:, :n2], kk[:, n2:]
    p = jnp.concatenate([xr * kr - xi * ki, xr * ki + xi * kr], axis=1)
    conv = _hy_idft(p, cb, li_ref[...], twr, twi, wi_ref[...])
    for c in range(cb):
        o_ref[c] = g_ref[c] * (conv[c] + bias_ref[c] * z[c])


def _hy_conv(z, z_part, gate, gate_part, khat, order, bias, consts, bsz):
    d = D_MODEL
    l = z.shape[1] // bsz
    nr = l // HY_N2
    cpo = d // HY_CB
    lhs_fwd, lhs_inv, twr, twi, w_fwd, w_inv = consts
    n1 = twr.shape[0]
    fixed2 = lambda c, b: (0, 0)
    out = pl.pallas_call(
        _hy_conv_kernel,
        out_shape=jax.ShapeDtypeStruct((d, bsz * nr, HY_N2), F32),
        grid=(cpo, bsz),
        in_specs=[pl.BlockSpec((HY_CB, nr, HY_N2), lambda c, b: (z_part * cpo + c, b, 0)),
                  pl.BlockSpec((HY_CB, nr, HY_N2), lambda c, b: (gate_part * cpo + c, b, 0)),
                  pl.BlockSpec((HY_CB, n1, 2 * HY_N2), lambda c, b: (order * cpo + c, 0, 0)),
                  pl.BlockSpec((HY_CB, 1, 1), lambda c, b: (c, 0, 0)),
                  pl.BlockSpec(lhs_fwd.shape, fixed2),
                  pl.BlockSpec(lhs_inv.shape, fixed2),
                  pl.BlockSpec(twr.shape, fixed2),
                  pl.BlockSpec(twi.shape, fixed2),
                  pl.BlockSpec(w_fwd.shape, fixed2),
                  pl.BlockSpec(w_inv.shape, fixed2)],
        out_specs=pl.BlockSpec((HY_CB, nr, HY_N2), lambda c, b: (c, b, 0)),
        compiler_params=pltpu.CompilerParams(
            dimension_semantics=("arbitrary", "arbitrary"), vmem_limit_bytes=VMEM_LIMIT_BYTES),
        name="hy_conv",
    )(z.reshape(-1, bsz * nr, HY_N2), gate.reshape(-1, bsz * nr, HY_N2), khat,
      bias.reshape(d, 1, 1), lhs_fwd, lhs_inv, twr, twi, w_fwd, w_inv)
    return out.reshape(d, bsz * l)


def _hy_ctx_kernel(z_ref, g_ref, hf_ref, hb_ref, bias_ref, wf_ref, wi_ref, o_ref, *, bsz):
    l = hf_ref.shape[1]
    wf, wi = wf_ref[...], wi_ref[...]
    hb = jnp.where(lax.broadcasted_iota(jnp.int32, (1, l), 1) == 0, 0.0, hb_ref[...])
    kf = jnp.dot(hf_ref[...].astype(BF16), wf, preferred_element_type=F32)
    kb = jnp.dot(hb.astype(BF16), wf, preferred_element_type=F32)
    n = 2 * l
    kr, ki = kf[:, :n] + kb[:, :n], kf[:, n:] - kb[:, n:]
    bias = bias_ref[...]
    for b in range(bsz):
        z = z_ref[:, b * l:(b + 1) * l]
        x = jnp.dot(z.astype(BF16), wf, preferred_element_type=F32)
        xr, xi = x[:, :n], x[:, n:]
        p = jnp.concatenate([xr * kr - xi * ki, xr * ki + xi * kr], axis=1)
        conv = jnp.dot(p.astype(BF16), wi, preferred_element_type=F32)
        o_ref[:, b * l:(b + 1) * l] = g_ref[:, b * l:(b + 1) * l] * (conv + bias * z)


def _hy_ctx(z, z_part, gate, gate_part, filt, order, bias, bsz):
    d = D_MODEL
    l = filt.shape[1]
    n = 2 * l
    ang = 2 * np.pi * np.outer(np.arange(l), np.arange(n)) / n
    w_fwd = jnp.asarray(np.concatenate([np.cos(ang), -np.sin(ang)], axis=1), BF16)
    w_inv = jnp.asarray(np.concatenate([np.cos(ang.T), -np.sin(ang.T)], axis=0) / n, BF16)
    cb = 256
    nblk = d // cb
    fixed = lambda c: (0, 0)
    return pl.pallas_call(
        functools.partial(_hy_ctx_kernel, bsz=bsz),
        out_shape=jax.ShapeDtypeStruct((d, bsz * l), F32),
        grid=(nblk,),
        in_specs=[pl.BlockSpec((cb, bsz * l), lambda c: (z_part * nblk + c, 0)),
                  pl.BlockSpec((cb, bsz * l), lambda c: (gate_part * nblk + c, 0)),
                  pl.BlockSpec((cb, l), lambda c: (2 * order * nblk + c, 0)),
                  pl.BlockSpec((cb, l), lambda c: ((2 * order + 1) * nblk + c, 0)),
                  pl.BlockSpec((cb, 1), lambda c: (c, 0)),
                  pl.BlockSpec(w_fwd.shape, fixed),
                  pl.BlockSpec(w_inv.shape, fixed)],
        out_specs=pl.BlockSpec((cb, bsz * l), lambda c: (c, 0)),
        compiler_params=pltpu.CompilerParams(
            dimension_semantics=("arbitrary",), vmem_limit_bytes=VMEM_LIMIT_BYTES),
        name="hy_ctx_conv",
    )(z, gate, filt, filt, bias.reshape(d, 1), w_fwd, w_inv)


def _hy_filter_kernel(band_ref, w1t_ref, w1c_ref, w1s_ref, b1_ref, fr_ref, w2_ref, b2_ref, w3_ref,
                      delta_ref, o_ref, *, l):
    tl = o_ref.shape[1]
    d = D_MODEL
    hp = lax.Precision.HIGHEST
    pos = (lax.broadcasted_iota(jnp.int32, (1, tl), 1) + pl.program_id(0) * tl).astype(F32)
    t = pos / max(l - 1, 1)
    ang = ((2 * math.pi / l) * pos) * band_ref[...]
    fr = fr_ref[...]
    pre = (w1t_ref[...] * t + jnp.dot(w1c_ref[...], jnp.cos(ang), precision=hp)
           + jnp.dot(w1s_ref[...], -jnp.sin(ang), precision=hp) + b1_ref[...])
    hdn = jnp.sin(fr * pre)
    hdn = jnp.sin(fr * (jnp.dot(w2_ref[...], hdn, precision=hp) + b2_ref[...]))
    window = jnp.exp(-t * delta_ref[...])
    for part in range(2 * HY_ORDER):
        rows = slice(part * d, (part + 1) * d)
        o_ref[rows, :] = jnp.dot(w3_ref[rows, :], hdn.astype(BF16), preferred_element_type=F32) * window


def _hy_filter(l, w1, b1, freq, w2, b2, w3):
    d = D_MODEL
    nb = HY_BANDS
    tl = min(HY_FILT_TILE, l)
    col = lambda v: v.reshape(-1, 1)
    bands = jnp.linspace(1e-4, nb - 1, nb, dtype=F32)
    deltas = jnp.abs(jnp.linspace(math.log(HY_TARGET) / HY_SLOW, math.log(HY_TARGET) / HY_FAST, d, dtype=F32))
    w1t = w1.T
    args = (col(bands), w1t[:, 0:1], w1t[:, 1:1 + nb], w1t[:, 1 + nb:], col(b1), col(freq), w2.T, col(b2),
            w3.T.astype(BF16), col(deltas))
    return pl.pallas_call(
        functools.partial(_hy_filter_kernel, l=l),
        out_shape=jax.ShapeDtypeStruct((2 * HY_ORDER * d, l), F32),
        grid=(l // tl,),
        in_specs=[pl.BlockSpec(a.shape, lambda j: (0, 0)) for a in args],
        out_specs=pl.BlockSpec((2 * HY_ORDER * d, tl), lambda j: (0, j)),
        compiler_params=pltpu.CompilerParams(
            dimension_semantics=("arbitrary",), vmem_limit_bytes=VMEM_LIMIT_BYTES),
        name="hy_filter",
    )(*args)


def _hy_inproj_kernel(x_ref, gain_ref, shift_ref, scale_ref, wt_ref, cw_ref, o_ref, *, seg):
    nch = wt_ref.shape[0]
    tm = x_ref.shape[0]
    hb = _modulated(x_ref[...], gain_ref[...], shift_ref[0], scale_ref[0]).astype(BF16)
    pos = lax.broadcasted_iota(jnp.int32, (1, tm), 1) & (seg - 1)
    not_first = pos != 0
    not_last = pos != seg - 1
    sub = 512
    for j in range(nch // sub):
        rows = slice(j * sub, (j + 1) * sub)
        p = _dotb_nt(wt_ref[rows, :], hb)
        cw = cw_ref[rows, :]
        prev = jnp.where(not_first, pltpu.roll(p, 1, axis=1), 0.0)
        nxt = jnp.where(not_last, pltpu.roll(p, tm - 1, axis=1), 0.0)
        o_ref[rows, :] = cw[:, 0:1] * prev + cw[:, 1:2] * p + cw[:, 2:3] * nxt


def _hy_inproj(xs, gain, shift3, scale3, w_in, conv_w, first_tile, n_tiles, seg, n_lat_blocks,
               blocks_per_batch):
    k = xs.shape[1]
    nch = w_in.shape[1]
    tm = HY_TOK_TILE
    per_tile = tm // DN_BLOCK
    grp = lambda i: (_group_of_block((first_tile + i) * per_tile, n_lat_blocks, blocks_per_batch), 0, 0)
    return pl.pallas_call(
        functools.partial(_hy_inproj_kernel, seg=seg),
        out_shape=jax.ShapeDtypeStruct((nch, n_tiles * tm), F32),
        grid=(n_tiles,),
        in_specs=[pl.BlockSpec((tm, k), lambda i: (first_tile + i, 0)),
                  pl.BlockSpec((1, k), lambda i: (0, 0)),
                  pl.BlockSpec((1, 1, k), grp),
                  pl.BlockSpec((1, 1, k), grp),
                  pl.BlockSpec((nch, k), lambda i: (0, 0)),
                  pl.BlockSpec((nch, 3), lambda i: (0, 0))],
        out_specs=pl.BlockSpec((nch, tm), lambda i: (0, i)),
        compiler_params=pltpu.CompilerParams(
            dimension_semantics=("arbitrary",), vmem_limit_bytes=VMEM_LIMIT_BYTES),
        name="hy_inproj",
    )(xs, gain.reshape(1, k), shift3, scale3, w_in.T.astype(BF16), conv_w.T)


def _hy_out_kernel(zl_ref, zc_ref, w_ref, x_ref, gate_ref, o_ref, *, n_lat_tiles):
    z = jnp.where(pl.program_id(0) >= n_lat_tiles, zc_ref[...], zl_ref[...])
    o_ref[...] = x_ref[...] + gate_ref[0] * _dotb_tn(z, w_ref[...])


def _hy_out(z_lat, z_ctx, w_out, xs, gate3, blocks_per_batch):
    t, d = xs.shape
    tm = HY_TOK_TILE
    n_lat_tiles = z_lat.shape[1] // tm
    per_tile = tm // DN_BLOCK
    grp = lambda i: (_group_of_block(i * per_tile, n_lat_tiles * per_tile, blocks_per_batch), 0, 0)
    return pl.pallas_call(
        functools.partial(_hy_out_kernel, n_lat_tiles=n_lat_tiles),
        out_shape=jax.ShapeDtypeStruct((t, d), F32),
        grid=(t // tm,),
        in_specs=[pl.BlockSpec((d, tm), lambda i: (0, jnp.minimum(i, n_lat_tiles - 1))),
                  pl.BlockSpec((d, tm), lambda i: (0, 0)),
                  pl.BlockSpec((d, d), lambda i: (0, 0)),
                  pl.BlockSpec((tm, d), lambda i: (i, 0)),
                  pl.BlockSpec((1, 1, d), grp)],
        out_specs=pl.BlockSpec((tm, d), lambda i: (i, 0)),
        compiler_params=pltpu.CompilerParams(
            dimension_semantics=("arbitrary",), vmem_limit_bytes=VMEM_LIMIT_BYTES),
        name="hy_out",
    )(z_lat, z_ctx, w_out.astype(BF16), xs, gate3)


def _hyena_layer(xs, gain, shift3, scale3, w_in, conv_w, f_w1, f_b1, f_freq, f_w2, f_b2, f_w3, bias, w_out,
                 gate3, bsz, seq):
    n_lat_tiles = bsz * seq // HY_TOK_TILE
    assert bsz * CTX_LEN == HY_TOK_TILE
    margs = (xs, gain, shift3, scale3, w_in, conv_w)
    blocks = (bsz * seq // DN_BLOCK, seq // DN_BLOCK)
    p_lat = _hy_inproj(*margs, 0, n_lat_tiles, GRID_W, *blocks)
    p_ctx = _hy_inproj(*margs, n_lat_tiles, 1, CTX_LEN, *blocks)
    fargs = (f_w1, f_b1, f_freq, f_w2, f_b2, f_w3)
    consts = _dft_constants(seq // HY_N2)
    khat = _hy_spectrum(_hy_filter(seq, *fargs), consts)
    filt_ctx = _hy_filter(CTX_LEN, *fargs)
    z_lat, z_ctx = p_lat, p_ctx
    for n in range(HY_ORDER):
        z_lat = _hy_conv(z_lat, 0, p_lat, n + 1, khat, n, bias[n], consts, bsz)
        z_ctx = _hy_ctx(z_ctx, 0, p_ctx, n + 1, filt_ctx, n, bias[n], bsz)
    return _hy_out(z_lat, z_ctx, w_out, xs, gate3, seq // DN_BLOCK)


MOE_TILE = 512
PACK = 2


def _experts_kernel(be_ref, na_ref, x_ref, wg_ref, wu_ref, wd_ref, o_ref, wgb_ref, wub_ref, wdb_ref):
    i = pl.program_id(0)
    prev = be_ref[jnp.maximum(i - 1, 0)]

    @pl.when((i == 0) | (be_ref[i] != prev))
    def _():
        wgb_ref[...] = wg_ref[0, 0].astype(BF16)
        wub_ref[...] = wu_ref[0, 0].astype(BF16)
        wdb_ref[...] = wd_ref[0, 0].astype(BF16)

    @pl.when(i < na_ref[0])
    def _():
        packed = x_ref[...]
        lo = pltpu.bitcast(packed << 16, F32)
        hi = pltpu.bitcast(packed & jnp.uint32(0xFFFF0000), F32)
        xb = jnp.concatenate([lo, hi], axis=-1).astype(BF16)
        g = jnp.dot(xb, wgb_ref[...], preferred_element_type=F32)
        u = jnp.dot(xb, wub_ref[...], preferred_element_type=F32)
        hid = (g * jax.nn.sigmoid(g)) * u
        y = jnp.dot(hid.astype(BF16), wdb_ref[...], preferred_element_type=F32)
        ybits = pltpu.bitcast(y.astype(BF16).astype(F32), jnp.uint32)
        half = y.shape[1] // PACK
        o_ref[...] = (ybits[:, :half] >> 16) | (ybits[:, half:] & jnp.uint32(0xFFFF0000))

    @pl.when(i >= na_ref[0])
    def _():
        o_ref[...] = jnp.zeros_like(o_ref)


def _experts(xs_sorted, block_expert, n_active, w_gate, w_up, w_down, layer):
    n_slots, wd = xs_sorted.shape
    d = wd * PACK
    n_blocks = n_slots // MOE_BLOCK
    blk = lambda i, be, na: (jnp.minimum(i, na[0] - 1), 0)
    wsel = lambda i, be, na: (layer, be[jnp.minimum(i, na[0] - 1)], 0, 0)
    grid_spec = pltpu.PrefetchScalarGridSpec(
        num_scalar_prefetch=2,
        grid=(n_blocks,),
        in_specs=[pl.BlockSpec((MOE_BLOCK, wd), blk),
                  pl.BlockSpec((1, 1, d, D_EXPERT), wsel),
                  pl.BlockSpec((1, 1, d, D_EXPERT), wsel),
                  pl.BlockSpec((1, 1, D_EXPERT, d), wsel)],
        out_specs=pl.BlockSpec((MOE_BLOCK, wd), lambda i, be, na: (i, 0)),
        scratch_shapes=[pltpu.VMEM((d, D_EXPERT), BF16), pltpu.VMEM((d, D_EXPERT), BF16),
                        pltpu.VMEM((D_EXPERT, d), BF16)],
    )
    return pl.pallas_call(
        _experts_kernel,
        out_shape=jax.ShapeDtypeStruct((n_slots, wd), jnp.uint32),
        grid_spec=grid_spec,
        compiler_params=pltpu.CompilerParams(
            dimension_semantics=("arbitrary",), vmem_limit_bytes=VMEM_LIMIT_BYTES),
        name="moe_experts",
    )(block_expert, n_active, xs_sorted, w_gate, w_up, w_down)


MOE_CHUNK = 8
MOE_LB = 1280
assert MOE_LB >= MOE_TILE * TOP_K + N_EXPERTS * (MOE_CHUNK - 1) and MOE_LB % 128 == 0
N_CHUNKS = MOE_LB // MOE_CHUNK
TAB_W = 256
assert TAB_W >= N_CHUNKS


def _moe_route_kernel(x_ref, gain_ref, shift_ref, scale_ref, wr_ref, rb_ref, tri_ref, lt_ref,
                      f_ref, pos_ref, w_ref, cnt_ref):
    tm, d = x_ref.shape
    ne, epg, ng = N_EXPERTS, EXPERTS_PER_GROUP, N_GROUPS
    f = _modulated(x_ref[...], gain_ref[...], shift_ref[0], scale_ref[0])
    f_ref[...] = f.astype(BF16)

    logits = lax.dot_general(wr_ref[...], f, (((1,), (1,)), ((), ())),
                             precision=lax.Precision.HIGHEST, preferred_element_type=F32)
    scores = jax.nn.sigmoid(logits)
    biased = scores + rb_ref[...]
    s = [scores[j * ng:(j + 1) * ng] for j in range(epg)]
    c = [biased[j * ng:(j + 1) * ng] for j in range(epg)]
    hi01, lo01 = jnp.maximum(c[0], c[1]), jnp.minimum(c[0], c[1])
    hi23, lo23 = jnp.maximum(c[2], c[3]), jnp.minimum(c[2], c[3])
    gscore = jnp.maximum(hi01, hi23) + jnp.maximum(jnp.minimum(hi01, hi23), jnp.maximum(lo01, lo23))
    gi = lax.broadcasted_iota(jnp.int32, (ng, tm), 0)
    gmax = jnp.max(gscore, axis=0, keepdims=True)
    grp = jnp.min(jnp.where(gscore == gmax, gi, ng), axis=0, keepdims=True)
    sel = gi == grp
    cv = [jnp.sum(jnp.where(sel, t, 0.0), axis=0, keepdims=True) for t in c]
    sv = [jnp.sum(jnp.where(sel, t, 0.0), axis=0, keepdims=True) for t in s]

    def pick(excluded):
        best = jnp.full((1, tm), -jnp.inf, F32)
        idx = jnp.zeros((1, tm), jnp.int32)
        val = jnp.zeros((1, tm), F32)
        for j in range(epg):
            cand = cv[j] if excluded is None else jnp.where(excluded == j, -jnp.inf, cv[j])
            take = cand > best
            best = jnp.where(take, cand, best)
            idx = jnp.where(take, j, idx)
            val = jnp.where(take, sv[j], val)
        return idx, val

    i1, v1 = pick(None)
    i2, v2 = pick(i1)
    wsum = v1 + v2
    w_ref[0:1, :] = v1 / wsum
    w_ref[1:2, :] = v2 / wsum

    ei = lax.broadcasted_iota(jnp.int32, (ne, tm), 0)
    oh1 = ei == grp * epg + i1
    oh2 = ei == grp * epg + i2
    tri = tri_ref[...]
    pre1 = jnp.dot(oh1.astype(BF16), tri, preferred_element_type=F32)
    pre2 = jnp.dot(oh2.astype(BF16), tri, preferred_element_type=F32)
    tot1 = pre1[:, tm - 1:tm]
    tot = tot1 + pre2[:, tm - 1:tm]
    seg = jnp.floor((tot + (MOE_CHUNK - 1)) * (1.0 / MOE_CHUNK)) * MOE_CHUNK
    off = jnp.dot(lt_ref[...], jnp.broadcast_to(seg, (ne, 128)).astype(BF16),
                  preferred_element_type=F32)[:, 0:1]
    p1 = jnp.sum(jnp.where(oh1, off + pre1 - 1.0, 0.0), axis=0, keepdims=True)
    p2 = jnp.sum(jnp.where(oh2, off + tot1 + pre2 - 1.0, 0.0), axis=0, keepdims=True)
    pos_ref[0:1, :] = p1.astype(jnp.int32)
    pos_ref[1:2, :] = p2.astype(jnp.int32)
    cnt_ref[0] = jnp.broadcast_to(tot, (ne, 128))


def _moe_route(xs, gain, shift3, scale3, w_router, router_bias, n_lat_tiles, blocks_per_batch):
    t, d = xs.shape
    tm = MOE_TILE
    ne = N_EXPERTS
    per_tile = tm // DN_BLOCK
    row = lambda i: (i, 0)
    col = lambda i: (0, i)
    fixed = lambda i: (0, 0)
    grp = lambda i: (_group_of_block(i * per_tile, n_lat_tiles * per_tile, blocks_per_batch), 0, 0)
    tri = jnp.asarray(np.triu(np.ones((tm, tm), np.float32)), BF16)
    lt = jnp.asarray(np.tril(np.ones((ne, ne), np.float32), -1), BF16)
    perm = np.arange(ne).reshape(N_GROUPS, EXPERTS_PER_GROUP).T.reshape(-1)
    return pl.pallas_call(
        _moe_route_kernel,
        out_shape=[jax.ShapeDtypeStruct((t, d), BF16),
                   jax.ShapeDtypeStruct((TOP_K, t), jnp.int32),
                   jax.ShapeDtypeStruct((TOP_K, t), F32),
                   jax.ShapeDtypeStruct((t // tm, ne, 128), F32)],
        grid=(t // tm,),
        in_specs=[pl.BlockSpec((tm, d), row),
                  pl.BlockSpec((1, d), fixed),
                  pl.BlockSpec((1, 1, d), grp),
                  pl.BlockSpec((1, 1, d), grp),
                  pl.BlockSpec((ne, d), fixed),
                  pl.BlockSpec((ne, 1), fixed),
                  pl.BlockSpec((tm, tm), fixed),
                  pl.BlockSpec((ne, ne), fixed)],
        out_specs=[pl.BlockSpec((tm, d), row),
                   pl.BlockSpec((TOP_K, tm), col),
                   pl.BlockSpec((TOP_K, tm), col),
                   pl.BlockSpec((1, ne, 128), lambda i: (i, 0, 0))],
        compiler_params=pltpu.CompilerParams(
            dimension_semantics=("arbitrary",), vmem_limit_bytes=VMEM_LIMIT_BYTES),
        name="moe_route",
    )(xs, gain.reshape(1, d), shift3, scale3, w_router.T[perm], router_bias[perm].reshape(ne, 1), tri, lt)


def _chunk_row(j):
    return j * MOE_CHUNK if isinstance(j, int) else pl.multiple_of(j * MOE_CHUNK, MOE_CHUNK)


def _chunk_issue(tab_smem, tab_row, make_copy):
    def issue_pair(jj, n):
        for priority in range(2):
            j = jj * 2 + priority
            dst = tab_smem[tab_row, j]

            @pl.when(dst >= 0)
            def _():
                make_copy(j, pl.multiple_of(dst, MOE_CHUNK)).start(priority=priority)

            n = n + (dst >= 0).astype(jnp.int32)
        return n

    return lax.fori_loop(0, N_CHUNKS // 2, issue_pair, jnp.int32(0), unroll=4)


def _chunk_drain(n, make_copy):
    def drain(j, carry):
        make_copy(0, 0).wait()
        return carry

    lax.fori_loop(0, n, drain, 0)


def _moe_dispatch_kernel(tab_ref, pos_ref, f_ref, xs_in_ref, xs_ref, cnt_smem, loc_ref, sem, *, n_steps):
    del xs_in_ref
    tm, d = f_ref.shape
    i = pl.program_id(0)
    slot = i % 2
    r = lax.broadcasted_iota(jnp.int32, (tm, MOE_LB), 1)
    p = pos_ref[...]
    onehot = ((p[:, 0:1] == r) | (p[:, 1:2] == r)).astype(BF16)
    loc = _dotb_tn(onehot, f_ref[...])
    bits = pltpu.bitcast(loc, jnp.uint32)
    half = d // PACK
    loc_ref[slot] = (bits[:, :half] >> 16) | (bits[:, half:] & jnp.uint32(0xFFFF0000))

    def copy_from(buf):
        def make_copy(j, dst):
            return pltpu.make_async_copy(loc_ref.at[buf, pl.ds(_chunk_row(j), MOE_CHUNK)],
                                         xs_ref.at[pl.ds(dst, MOE_CHUNK)], sem.at[buf])
        return make_copy

    n = _chunk_issue(tab_ref, i, copy_from(slot))
    cnt_smem[slot] = n

    @pl.when(i > 0)
    def _():
        _chunk_drain(cnt_smem[1 - slot], copy_from(1 - slot))

    @pl.when(i == n_steps - 1)
    def _():
        _chunk_drain(n, copy_from(slot))


def _moe_dispatch(f, pos_cols, table, slots):
    t, d = f.shape
    tm = MOE_TILE
    n_slots, wd = slots.shape
    grid_spec = pltpu.PrefetchScalarGridSpec(
        num_scalar_prefetch=1,
        grid=(t // tm,),
        in_specs=[pl.BlockSpec((tm, TOP_K), lambda i, tab: (i, 0)),
                  pl.BlockSpec((tm, d), lambda i, tab: (i, 0)),
                  pl.BlockSpec(memory_space=pl.ANY)],
        out_specs=pl.BlockSpec(memory_space=pl.ANY),
        scratch_shapes=[pltpu.SMEM((2,), jnp.int32),
                        pltpu.VMEM((2, MOE_LB, wd), jnp.uint32),
                        pltpu.SemaphoreType.DMA((2,))],
    )
    return pl.pallas_call(
        functools.partial(_moe_dispatch_kernel, n_steps=t // tm),
        out_shape=jax.ShapeDtypeStruct((n_slots, wd), jnp.uint32),
        grid_spec=grid_spec,
        input_output_aliases={3: 0},
        compiler_params=pltpu.CompilerParams(
            dimension_semantics=("arbitrary",), vmem_limit_bytes=VMEM_LIMIT_BYTES),
        name="moe_dispatch",
    )(table, pos_cols, f, slots)


def _moe_combine_kernel(tab_ref, pos_ref, w_ref, y_ref, x_ref, gate_ref, fin_ref, o_ref, cnt_smem, yloc_ref,
                        sem, *, final_norm, n_steps):
    tm = x_ref.shape[0]
    i = pl.program_id(0)
    slot = i % 2

    def fetch(tile, buf):
        cnt_smem[buf] = _chunk_issue(tab_ref, tile, copy_into(buf))

    def copy_into(buf):
        def make_copy(j, src):
            return pltpu.make_async_copy(y_ref.at[pl.ds(src, MOE_CHUNK)],
                                         yloc_ref.at[buf, pl.ds(_chunk_row(j), MOE_CHUNK)], sem.at[buf])
        return make_copy

    @pl.when(i == 0)
    def _():
        yloc_ref[...] = jnp.zeros_like(yloc_ref)
        fetch(0, 0)

    @pl.when(i + 1 < n_steps)
    def _():
        fetch(i + 1, 1 - slot)

    _chunk_drain(cnt_smem[slot], copy_into(slot))
    r = lax.broadcasted_iota(jnp.int32, (tm, MOE_LB), 1)
    p = pos_ref[...]
    w = w_ref[...]
    wmat = jnp.where(p[:, 0:1] == r, w[:, 0:1], 0.0) + jnp.where(p[:, 1:2] == r, w[:, 1:2], 0.0)
    packed = yloc_ref[slot]
    y_lo = pltpu.bitcast(packed << 16, F32)
    y_hi = pltpu.bitcast(packed & jnp.uint32(0xFFFF0000), F32)
    moe = jnp.concatenate([_dotb(wmat, y_lo), _dotb(wmat, y_hi)], axis=-1)
    out = x_ref[...] + gate_ref[0] * moe
    if final_norm:
        out = out * lax.rsqrt(jnp.mean(out * out, axis=-1, keepdims=True) + EPS) * fin_ref[...]
    o_ref[...] = out


def _moe_combine(ys, pos_cols, weight_cols, table, xs, gate3, n_lat_tiles, blocks_per_batch, final_gain=None):
    t, d = xs.shape
    tm = MOE_TILE
    per_tile = tm // DN_BLOCK
    n_tiles = t // tm if final_gain is None else n_lat_tiles
    fin = jnp.ones((1, d), F32) if final_gain is None else final_gain.reshape(1, d)
    row = lambda i, tab: (i, 0)
    grp = lambda i, tab: (_group_of_block(i * per_tile, n_lat_tiles * per_tile, blocks_per_batch), 0, 0)
    grid_spec = pltpu.PrefetchScalarGridSpec(
        num_scalar_prefetch=1,
        grid=(n_tiles,),
        in_specs=[pl.BlockSpec((tm, TOP_K), row),
                  pl.BlockSpec((tm, TOP_K), row),
                  pl.BlockSpec(memory_space=pl.ANY),
                  pl.BlockSpec((tm, d), row),
                  pl.BlockSpec((1, 1, d), grp),
                  pl.BlockSpec((1, d), lambda i, tab: (0, 0))],
        out_specs=pl.BlockSpec((tm, d), row),
        scratch_shapes=[pltpu.SMEM((2,), jnp.int32),
                        pltpu.VMEM((2, MOE_LB, d // PACK), jnp.uint32),
                        pltpu.SemaphoreType.DMA((2,))],
    )
    return pl.pallas_call(
        functools.partial(_moe_combine_kernel, final_norm=final_gain is not None, n_steps=n_tiles),
        out_shape=jax.ShapeDtypeStruct((n_tiles * tm, d), F32),
        grid_spec=grid_spec,
        compiler_params=pltpu.CompilerParams(
            dimension_semantics=("arbitrary",), vmem_limit_bytes=VMEM_LIMIT_BYTES),
        name="moe_combine",
    )(table, pos_cols, weight_cols, ys, xs, gate3, fin)


def _moe_layer(xs, gain, shift3, scale3, gate3, w_router, router_bias, w_gate, w_up, w_down, layer,
               n_lat_tiles, blocks_per_batch, final_gain=None, slots=None):
    t = xs.shape[0]
    n_tiles = t // MOE_TILE
    f, pos, weight, cnt = _moe_route(xs, gain, shift3, scale3, w_router, router_bias,
                                     n_lat_tiles, blocks_per_batch)
    seg = (cnt[:, :, 0].astype(jnp.int32) + MOE_CHUNK - 1) // MOE_CHUNK * MOE_CHUNK
    loc_end = jnp.cumsum(seg, axis=1)
    loc_off = loc_end - seg
    padded = (jnp.sum(seg, axis=0) + MOE_BLOCK - 1) // MOE_BLOCK * MOE_BLOCK
    pend = jnp.cumsum(padded)
    seg_start = (pend - padded)[None, :] + jnp.cumsum(seg, axis=0) - seg
    n_blocks = -(-(t * TOP_K + n_tiles * N_EXPERTS * (MOE_CHUNK - 1)) // MOE_BLOCK) + N_EXPERTS
    block_start = jnp.arange(n_blocks, dtype=jnp.int32) * MOE_BLOCK
    block_expert = jnp.minimum(jnp.sum(pend[None, :] <= block_start[:, None], axis=1),
                               N_EXPERTS - 1).astype(jnp.int32)
    n_active = (pend[-1:] // MOE_BLOCK).astype(jnp.int32)
    row0 = jnp.arange(N_CHUNKS, dtype=jnp.int32) * MOE_CHUNK
    e_of = jnp.sum(loc_end[:, None, :] <= row0[None, :, None], axis=-1)
    is_e = e_of[..., None] == jnp.arange(N_EXPERTS, dtype=jnp.int32)
    shift = jnp.sum(jnp.where(is_e, (seg_start - loc_off)[:, None, :], 0), axis=-1)
    table = jnp.where(e_of < N_EXPERTS, row0[None, :] + shift, -1)
    table = jnp.pad(table, ((0, 0), (0, TAB_W - N_CHUNKS)), constant_values=-1)

    pos_cols = pos.T
    if slots is None:
        slots = jnp.zeros((n_blocks * MOE_BLOCK, f.shape[1] // PACK), jnp.uint32)
    xs_sorted = _moe_dispatch(f, pos_cols, table, slots)
    ys = _experts(xs_sorted, block_expert, n_active, w_gate, w_up, w_down, layer)
    out = _moe_combine(ys, pos_cols, weight.T, table, xs, gate3, n_lat_tiles, blocks_per_batch, final_gain)
    return out, xs_sorted


def _sc_layer_kernel(x_ref, gain_ref, shift_ref, scale_ref, win_ref, cw_ref, wout_ref, gate_ref, o_ref, *,
                     n_lat_tiles):
    d = D_MODEL
    tm = x_ref.shape[0]
    seg = jnp.where(pl.program_id(0) >= n_lat_tiles, CTX_LEN, GRID_W)
    pos = lax.broadcasted_iota(jnp.int32, (tm, 1), 0) & (seg - 1)
    hb = _modulated(x_ref[...], gain_ref[...], shift_ref[0], scale_ref[0]).astype(BF16)
    u = (jnp.dot(hb, win_ref[:, d:2 * d], preferred_element_type=F32)
         * jnp.dot(hb, win_ref[:, 2 * d:], preferred_element_type=F32))
    prev = jnp.where(pos != 0, pltpu.roll(u, 1, axis=0), 0.0)
    nxt = jnp.where(pos != seg - 1, pltpu.roll(u, tm - 1, axis=0), 0.0)
    cw = cw_ref[...]
    y = jnp.dot(hb, win_ref[:, :d], preferred_element_type=F32) * (
        cw[0:1] * prev + cw[1:2] * u + cw[2:3] * nxt)
    o_ref[...] = x_ref[...] + gate_ref[0] * jnp.dot(y.astype(BF16), wout_ref[...],
                                                    preferred_element_type=F32)


def _shortconv_layer(xs, gain, shift3, scale3, w_in, conv_w, w_out, gate3, n_lat_tiles, blocks_per_batch):
    t, d = xs.shape
    tm = HY_TOK_TILE
    per_tile = tm // DN_BLOCK
    row = lambda i: (i, 0)
    fixed = lambda i: (0, 0)
    grp = lambda i: (_group_of_block(i * per_tile, n_lat_tiles * per_tile, blocks_per_batch), 0, 0)
    return pl.pallas_call(
        functools.partial(_sc_layer_kernel, n_lat_tiles=n_lat_tiles),
        out_shape=jax.ShapeDtypeStruct((t, d), F32),
        grid=(t // tm,),
        in_specs=[pl.BlockSpec((tm, d), row),
                  pl.BlockSpec((1, d), fixed),
                  pl.BlockSpec((1, 1, d), grp),
                  pl.BlockSpec((1, 1, d), grp),
                  pl.BlockSpec((d, 3 * d), fixed),
                  pl.BlockSpec((3, d), fixed),
                  pl.BlockSpec((d, d), fixed),
                  pl.BlockSpec((1, 1, d), grp)],
        out_specs=pl.BlockSpec((tm, d), row),
        compiler_params=pltpu.CompilerParams(
            dimension_semantics=("arbitrary",), vmem_limit_bytes=VMEM_LIMIT_BYTES),
        name="shortconv_layer",
    )(xs, gain.reshape(1, d), shift3, scale3, w_in.astype(BF16), conv_w, w_out.astype(BF16), gate3)


def kernel(x, c, ctx, c_ctx, ada_w, ada_b, norm_mix, norm_ffn, norm_final, dn_w_in, dn_conv, dn_a_log,
           dn_dt_bias, dn_out_norm, dn_w_out, hy_w_in, hy_conv, hy_f_w1, hy_f_b1, hy_f_freq, hy_f_w2,
           hy_f_b2, hy_f_w3, hy_bias, hy_w_out, sc_w_in, sc_conv, sc_w_out, w_router, router_bias,
           moe_w_gate, moe_w_up, moe_w_down):
    d = D_MODEL
    bsz, seq, _ = x.shape
    n_ctx = bsz * CTX_LEN
    n_lat = bsz * seq
    silu_c = jax.nn.silu(c)
    silu_cc = jax.nn.silu(c_ctx)
    hp = lax.Precision.HIGHEST

    xs = jnp.concatenate([x.reshape(n_lat, d), ctx.reshape(n_ctx, d)], axis=0)
    n_lat_blocks, blocks_per_batch = n_lat // DN_BLOCK, seq // DN_BLOCK
    slots = None

    for i in range(DEPTH):
        kind, j = i % N_MIXERS, i // N_MIXERS
        ml = jnp.split(jnp.dot(silu_c, ada_w[i], precision=hp) + ada_b[i], N_MOD, axis=-1)
        mc = jnp.split(jnp.dot(silu_cc, ada_w[i], precision=hp) + ada_b[i], N_MOD, axis=-1)
        mod = [jnp.concatenate([mc[m][None], ml[m]], axis=0)[:, None, :] for m in range(N_MOD)]
        if kind == 0:
            xs = _deltanet_layer(xs, norm_mix[i], mod[0], mod[1], dn_w_in[j], dn_conv[j], dn_a_log[j],
                                 dn_dt_bias[j], dn_out_norm[j], dn_w_out[j], mod[2], n_lat_blocks,
                                 blocks_per_batch)
        elif kind == 1:
            xs = _hyena_layer(xs, norm_mix[i], mod[0], mod[1], hy_w_in[j], hy_conv[j], hy_f_w1[j], hy_f_b1[j],
                              hy_f_freq[j], hy_f_w2[j], hy_f_b2[j], hy_f_w3[j], hy_bias[j], hy_w_out[j],
                              mod[2], bsz, seq)
        else:
            xs = _shortconv_layer(xs, norm_mix[i], mod[0], mod[1], sc_w_in[j], sc_conv[j], sc_w_out[j],
                                  mod[2], n_lat // HY_TOK_TILE, blocks_per_batch)
        xs, slots = _moe_layer(xs, norm_ffn[i], mod[3], mod[4], mod[5], w_router, router_bias,
                               moe_w_gate, moe_w_up, moe_w_down, i, n_lat // MOE_TILE, blocks_per_batch,
                               norm_final if i == DEPTH - 1 else None, slots)
    return xs.reshape(bsz, seq, d)
```

```python
import functools
import math

import numpy as np
import jax
import jax.numpy as jnp
from jax import lax
from jax.experimental import pallas as pl
from jax.experimental.pallas import tpu as pltpu

D_MODEL = 1024
DEPTH = 4
CTX_LEN = 256
GRID_W = 64
N_MIXERS = 3
EPS = 1e-6
N_MOD = 6

DN_HEADS = 8
DN_DK = D_MODEL // DN_HEADS
DN_DV = D_MODEL // DN_HEADS
DN_CHUNK = 64

HY_ORDER = 2
HY_BANDS = 16
HY_TARGET = 1e-2
HY_FAST = 0.3
HY_SLOW = 1.5

N_EXPERTS = 32
N_GROUPS = 8
EXPERTS_PER_GROUP = N_EXPERTS // N_GROUPS
GROUP_SCORE_K = 2
TOP_K = 2
D_EXPERT = 512
MOE_BLOCK = 512

F32 = jnp.float32
BF16 = jnp.bfloat16

VMEM_LIMIT_BYTES = 48 * 1024 * 1024


DN_BLOCK = CTX_LEN
DN_HB = DN_HEADS
N_CHUNKS_PER_BLOCK = DN_BLOCK // DN_CHUNK


def _group_of_block(i, n_lat_blocks, blocks_per_batch):
    return jnp.where(i >= n_lat_blocks, 0, 1 + i // blocks_per_batch)


def _modulated(x, gain, shift, scale):
    y = x * lax.rsqrt(jnp.mean(x * x, axis=-1, keepdims=True) + EPS) * gain
    return y * (1 + scale) + shift


def _stream_rows(x_refs, is_ctx):
    if len(x_refs) == 1:
        return x_refs[0][...]
    return jnp.where(is_ctx, x_refs[1][...], x_refs[0][...])


def _dn_inproj_kernel(*refs, n_lat_blocks, n_src):
    x_refs = refs[:n_src]
    (gain_ref, shift_ref, scale_ref, w_ref, wab_ref, cw_ref, alog_ref, dtb_ref,
     q_ref, k_ref, v_ref, z_ref, gate_ref, gate_t_ref) = refs[n_src:]
    i = pl.program_id(0)
    nrow = DN_BLOCK
    d = D_MODEL
    pair = 2 * DN_DK
    seg = jnp.where(i >= n_lat_blocks, CTX_LEN, GRID_W)
    r = lax.broadcasted_iota(jnp.int32, (nrow, 1), 0)
    pos = r & (seg - 1)
    not_first = pos != 0
    not_last = pos != seg - 1
    h = _modulated(_stream_rows(x_refs, i >= n_lat_blocks), gain_ref[...], shift_ref[0], scale_ref[0])
    hb = h.astype(BF16)
    outs = (q_ref, k_ref, v_ref)
    for part in range(3):
        for hp in range(d // pair):
            col = part * d + hp * pair
            x = jnp.dot(hb, w_ref[:, col:col + pair], preferred_element_type=F32)
            cw = cw_ref[:, col:col + pair]
            xp = jnp.where(not_first, pltpu.roll(x, 1, axis=0), 0.0)
            xn = jnp.where(not_last, pltpu.roll(x, nrow - 1, axis=0), 0.0)
            y = cw[0:1] * xp + cw[1:2] * x + cw[2:3] * xn
            y = y * jax.nn.sigmoid(y)
            for hh in range(2):
                yh = y[:, hh * DN_DK:(hh + 1) * DN_DK]
                if part < 2:
                    yh = yh * lax.rsqrt(jnp.sum(yh * yh, axis=-1, keepdims=True) + EPS)
                if part == 0:
                    yh = yh * DN_DK ** -0.5
                outs[part][:, hp * pair + hh * DN_DK:hp * pair + (hh + 1) * DN_DK] = yh
    for j in range(d // pair):
        z_ref[:, j * pair:(j + 1) * pair] = jnp.dot(hb, w_ref[:, 3 * d + j * pair:3 * d + (j + 1) * pair],
                                                    preferred_element_type=F32).astype(BF16)

    ab = jnp.dot(hb, wab_ref[...], preferred_element_type=F32)
    nd = 2 * DN_HEADS
    a = ab[:, :nd] + dtb_ref[...]
    softplus = jnp.maximum(a, 0.0) + jnp.log(1.0 + jnp.exp(-jnp.abs(a)))
    g = -jnp.exp(alog_ref[...]) * softplus
    beta = jax.nn.sigmoid(ab[:, nd:])
    cpos = r & (DN_CHUNK - 1)
    gp, gs = g, g
    sh = 1
    while sh < DN_CHUNK:
        gp = gp + jnp.where(cpos >= sh, pltpu.roll(gp, sh, axis=0), 0.0)
        gs = gs + jnp.where(cpos < DN_CHUNK - sh, pltpu.roll(gs, nrow - sh, axis=0), 0.0)
        sh *= 2
    colid = lax.broadcasted_iota(jnp.int32, (1, nd), 1)
    gates = jnp.concatenate([jnp.where(colid < DN_HEADS, gp, gs), beta], axis=1)
    gate_ref[...] = gates
    gate_t_ref[...] = gates.T


def _stream_specs(xs, n_lat_blocks):
    if not isinstance(xs, tuple):
        return [pl.BlockSpec((DN_BLOCK, xs.shape[1]), lambda i: (i, 0))], xs.shape[0]
    d = xs[0].shape[1]
    return [pl.BlockSpec((DN_BLOCK, d), lambda i: (jnp.minimum(i, n_lat_blocks - 1), 0)),
            pl.BlockSpec((DN_BLOCK, d), lambda i: (jnp.maximum(i - n_lat_blocks, 0), 0))], sum(a.shape[0] for a in xs)


def _dn_inproj(xs, gain, shift3, scale3, w_in, conv_w, a_log, dt_bias, n_lat_blocks, blocks_per_batch):
    x_specs, t = _stream_specs(xs, n_lat_blocks)
    x_args = xs if isinstance(xs, tuple) else (xs,)
    d = D_MODEL
    nd = 2 * DN_HEADS
    row = lambda i: (i, 0)
    fixed = lambda i: (0, 0)
    grp = lambda i: (_group_of_block(i, n_lat_blocks, blocks_per_batch), 0, 0)
    return pl.pallas_call(
        functools.partial(_dn_inproj_kernel, n_lat_blocks=n_lat_blocks, n_src=len(x_args)),
        out_shape=[jax.ShapeDtypeStruct((t, d), F32)] * 3 + [jax.ShapeDtypeStruct((t, d), BF16),
                                                              jax.ShapeDtypeStruct((t, 2 * nd), F32),
                                                              jax.ShapeDtypeStruct((2 * nd, t), F32)],
        grid=(t // DN_BLOCK,),
        in_specs=x_specs + [
                  pl.BlockSpec((1, d), fixed),
                  pl.BlockSpec((1, 1, d), grp),
                  pl.BlockSpec((1, 1, d), grp),
                  pl.BlockSpec((d, 4 * d), fixed),
                  pl.BlockSpec((d, 2 * nd), fixed),
                  pl.BlockSpec((3, 3 * d), fixed),
                  pl.BlockSpec((1, nd), fixed),
                  pl.BlockSpec((1, nd), fixed)],
        out_specs=[pl.BlockSpec((DN_BLOCK, d), row)] * 4 + [pl.BlockSpec((DN_BLOCK, 2 * nd), row),
                                                            pl.BlockSpec((2 * nd, DN_BLOCK), lambda i: (0, i))],
        compiler_params=pltpu.CompilerParams(
            dimension_semantics=("arbitrary",), vmem_limit_bytes=VMEM_LIMIT_BYTES),
        name="dn_inproj",
    )(*x_args, gain.reshape(1, d), shift3, scale3, w_in[:, :4 * d].astype(BF16), w_in[:, 4 * d:].astype(BF16),
      conv_w, a_log.reshape(1, nd), dt_bias.reshape(1, nd))


def _dotb(a, b):
    return jnp.dot(a.astype(BF16), b.astype(BF16), preferred_element_type=F32)


def _dotb_nt(a, b):
    return lax.dot_general(a.astype(BF16), b.astype(BF16), (((1,), (1,)), ((), ())),
                           preferred_element_type=F32)


def _dotb_tn(a, b):
    return lax.dot_general(a.astype(BF16), b.astype(BF16), (((0,), (0,)), ((), ())),
                           preferred_element_type=F32)


def _unit_tri_inverses(mats):
    c = DN_CHUNK
    assert len(mats) % 2 == 0
    ii = lax.broadcasted_iota(jnp.int32, (c, 2 * c), 0)
    lane = lax.broadcasted_iota(jnp.int32, (c, 2 * c), 1)
    jj = lane & (c - 1)
    left = lane < c

    def blockdiag(p):
        pb = p.astype(BF16)
        zero = jnp.zeros_like(pb)
        return jnp.concatenate([jnp.where(left, pb, zero), jnp.where(left, zero, pb)], axis=0)

    def mul(p, q):
        return jnp.dot(p.astype(BF16), blockdiag(q), preferred_element_type=F32)

    pairs = [jnp.concatenate([mats[i], mats[i + 1]], axis=1) for i in range(0, len(mats), 2)]
    eye = (ii == jj).astype(F32)
    diag8 = (ii >> 3) == (jj >> 3)
    n = [-jnp.where(diag8, a, 0.0) for a in pairs]
    n2 = [mul(x, x) for x in n]
    m = [eye + x for x in n]
    m = [x + mul(x, y) for x, y in zip(m, n2)]
    n4 = [mul(x, x) for x in n2]
    m = [x + mul(x, y) for x, y in zip(m, n4)]
    sh = 3
    while (1 << sh) < c:
        off = ((ii >> (sh + 1)) == (jj >> (sh + 1))) & ((ii >> sh) != (jj >> sh))
        cm = [mul(jnp.where(off, a, 0.0), x) for a, x in zip(pairs, m)]
        m = [x - mul(x, y) for x, y in zip(m, cm)]
        sh += 1
    return [half for x in m for half in (x[:, :c], x[:, c:])]


def _dn_scan_kernel(qf_ref, kf_ref, vf_ref, gcf_ref, grf_ref, qb_ref, kb_ref, vb_ref, gcb_ref, grb_ref,
                    of_ref, ob_ref, s_ref):
    @pl.when(pl.program_id(2) == 0)
    def _():
        s_ref[...] = jnp.zeros_like(s_ref)

    c = DN_CHUNK
    ncb = N_CHUNKS_PER_BLOCK
    ii = lax.broadcasted_iota(jnp.int32, (c, c), 0)
    jj = lax.broadcasted_iota(jnp.int32, (c, c), 1)
    incl = (ii >= jj, ii <= jj)
    strict = (ii > jj, ii < jj)
    dirs = ((qf_ref, kf_ref, vf_ref, gcf_ref, grf_ref, of_ref),
            (qb_ref, kb_ref, vb_ref, gcb_ref, grb_ref, ob_ref))
    items = [(d, hh, ci) for d in range(2) for hh in range(DN_HB) for ci in range(ncb)]

    def rows(ci):
        return slice(ci * c, (ci + 1) * c)

    def cols(hh):
        return slice(hh * DN_DK, (hh + 1) * DN_DK)

    q = [dirs[d][0][rows(ci), cols(hh)] for d, hh, ci in items]
    k = [dirs[d][1][rows(ci), cols(hh)] for d, hh, ci in items]
    v = [dirs[d][2][rows(ci), cols(hh)] for d, hh, ci in items]
    nh = DN_HEADS
    gc = [dirs[d][3][rows(ci), d * nh + hh:d * nh + hh + 1] for d, hh, ci in items]
    gr = [dirs[d][4][d * nh + hh:d * nh + hh + 1, rows(ci)] for d, hh, ci in items]
    beta = [dirs[d][3][rows(ci), (2 + d) * nh + hh:(2 + d) * nh + hh + 1] for d, hh, ci in items]

    decay = [jnp.where(incl[it[0]], jnp.exp(jnp.where(incl[it[0]], x - y, 0.0)), 0.0)
             for it, x, y in zip(items, gc, gr)]
    kb = [x * y for x, y in zip(k, beta)]
    a = [_dotb_nt(x, y) * jnp.where(strict[it[0]], z, 0.0) for it, x, y, z in zip(items, kb, k, decay)]
    attn = [_dotb_nt(x, y) * z for x, y, z in zip(q, k, decay)]
    tinv = _unit_tri_inverses(a)
    eg = [jnp.exp(x) for x in gc]
    uw = [_dotb(t, jnp.concatenate([x * y, z * e], axis=-1))
          for t, x, y, z, e in zip(tinv, v, beta, kb, eg)]
    g_last = [x[0:1] if it[0] else x[c - 1:c] for it, x in zip(items, gc)]
    wq = [jnp.concatenate([x[:, DN_DV:], y * e], axis=0) for x, y, e in zip(uw, q, eg)]
    k_dec = [x * jnp.exp(y - z) for x, y, z in zip(k, g_last, gc)]
    s_dec = [jnp.exp(x) for x in g_last]

    chains = [(d, hh) for d in range(2) for hh in range(DN_HB)]
    state = [s_ref[d, hh] for d, hh in chains]
    for step in range(ncb):
        cur = [items.index((d, hh, ncb - 1 - step if d else step)) for d, hh in chains]
        ws = [_dotb(wq[n], s) for n, s in zip(cur, state)]
        v_new = [uw[n][:, :DN_DV] - x[:c] for n, x in zip(cur, ws)]
        o = [x[c:] + _dotb(attn[n], y) for n, x, y in zip(cur, ws, v_new)]
        state = [s * s_dec[n] + _dotb_tn(k_dec[n], y) for n, s, y in zip(cur, state, v_new)]
        for n, x in zip(cur, o):
            d, hh, ci = items[n]
            dirs[d][5][rows(ci), cols(hh)] = x.astype(BF16)
    for (d, hh), s in zip(chains, state):
        s_ref[d, hh] = s


def _dn_scan(q, k, v, gates, gates_t, n_lat_blocks, blocks_per_batch):
    t, d = q.shape
    bsz = n_lat_blocks // blocks_per_batch
    assert DN_HB == DN_HEADS
    ng = gates.shape[1]

    def blk_f(b, s):
        return jnp.where(s == 0, n_lat_blocks + b, b * blocks_per_batch + s - 1)

    def blk_b(b, s):
        return jnp.where(s == 0, n_lat_blocks + b, b * blocks_per_batch + blocks_per_batch - s)

    hw = DN_HB * DN_DK

    def specs(blk):
        return [pl.BlockSpec((DN_BLOCK, hw), lambda b, hg, s: (blk(b, s), hg))] * 3 + [
            pl.BlockSpec((DN_BLOCK, ng), lambda b, hg, s: (blk(b, s), 0)),
            pl.BlockSpec((ng, DN_BLOCK), lambda b, hg, s: (0, blk(b, s)))]

    return pl.pallas_call(
        _dn_scan_kernel,
        out_shape=[jax.ShapeDtypeStruct((t, d), BF16)] * 2,
        grid=(bsz, DN_HEADS // DN_HB, 1 + blocks_per_batch),
        in_specs=specs(blk_f) + specs(blk_b),
        out_specs=[pl.BlockSpec((DN_BLOCK, hw), lambda b, hg, s: (blk_f(b, s), hg)),
                   pl.BlockSpec((DN_BLOCK, hw), lambda b, hg, s: (blk_b(b, s), hg))],
        scratch_shapes=[pltpu.VMEM((2, DN_HB, DN_DK, DN_DV), F32)],
        compiler_params=pltpu.CompilerParams(
            dimension_semantics=("arbitrary", "arbitrary", "arbitrary"),
            vmem_limit_bytes=VMEM_LIMIT_BYTES),
        name="dn_scan",
    )(q, k, v, gates, gates_t, q, k, v, gates, gates_t)


def _dn_out_kernel(of_ref, ob_ref, z_ref, on_ref, w_ref, gate_ref, *refs, n_lat_blocks):
    x_refs, o_ref = refs[:-1], refs[-1]
    z = z_ref[...].astype(F32)
    zs = z * jax.nn.sigmoid(z)
    parts = []
    for h in range(DN_HEADS):
        cols = slice(h * DN_DV, (h + 1) * DN_DV)
        o = of_ref[:, cols].astype(F32) + ob_ref[:, cols].astype(F32)
        o = o * lax.rsqrt(jnp.mean(o * o, axis=-1, keepdims=True) + EPS)
        parts.append(o)
    y = jnp.concatenate(parts, axis=-1) * on_ref[...] * zs
    x = _stream_rows(x_refs, pl.program_id(0) >= n_lat_blocks)
    o_ref[...] = x + gate_ref[0] * jnp.dot(y.astype(BF16), w_ref[...], preferred_element_type=F32)


def _dn_out(o_f, o_b, z, out_norm, w_out, xs, gate3, n_lat_blocks, blocks_per_batch):
    x_specs, t = _stream_specs(xs, n_lat_blocks)
    x_args = xs if isinstance(xs, tuple) else (xs,)
    d = D_MODEL
    row = lambda i: (i, 0)
    fixed = lambda i: (0, 0)
    grp = lambda i: (_group_of_block(i, n_lat_blocks, blocks_per_batch), 0, 0)
    return pl.pallas_call(
        functools.partial(_dn_out_kernel, n_lat_blocks=n_lat_blocks),
        out_shape=jax.ShapeDtypeStruct((t, d), F32),
        grid=(t // DN_BLOCK,),
        in_specs=[pl.BlockSpec((DN_BLOCK, d), row),
                  pl.BlockSpec((DN_BLOCK, d), row),
                  pl.BlockSpec((DN_BLOCK, d), row),
                  pl.BlockSpec((1, d), fixed),
                  pl.BlockSpec((d, d), fixed),
                  pl.BlockSpec((1, 1, d), grp)] + x_specs,
        out_specs=pl.BlockSpec((DN_BLOCK, d), row),
        compiler_params=pltpu.CompilerParams(
            dimension_semantics=("arbitrary",), vmem_limit_bytes=VMEM_LIMIT_BYTES),
        name="dn_out",
    )(o_f, o_b, z, jnp.tile(out_norm, DN_HEADS).reshape(1, d), w_out.astype(BF16), gate3, *x_args)


def _deltanet_layer(xs, gain, shift3, scale3, w_in, conv_w, a_log, dt_bias, out_norm, w_out, gate3,
                    n_lat_blocks, blocks_per_batch):
    q, k, v, z, gates, gates_t = _dn_inproj(xs, gain, shift3, scale3, w_in, conv_w, a_log, dt_bias,
                                            n_lat_blocks, blocks_per_batch)
    o_f, o_b = _dn_scan(q, k, v, gates, gates_t, n_lat_blocks, blocks_per_batch)
    return _dn_out(o_f, o_b, z, out_norm, w_out, xs, gate3, n_lat_blocks, blocks_per_batch)


HY_N2 = 256
HY_CB = 16
HY_TOK_TILE = 512
HY_FILT_TILE = 512


def _dft_constants(nr):
    n1, n2 = 2 * nr, HY_N2
    n = n1 * n2
    nk = -(-(nr + 1) // 8) * 8
    keep = np.arange(nk) <= nr
    k1 = np.where(keep, np.arange(nk), 0)
    f1 = np.exp(-2j * np.pi * np.outer(k1, np.arange(nr)) / n1) * keep[:, None]
    twice = np.where((k1 > 0) & (k1 < nr), 2.0, 1.0) * keep
    lhs_fwd = np.concatenate([f1.real, f1.imag], axis=0)
    lhs_inv = np.concatenate([f1.real.T * twice, f1.imag.T * twice], axis=1) / n
    tw = np.exp(-2j * np.pi * np.outer(k1, np.arange(n2)) / n)
    a2 = np.arange(n2)
    f2 = np.exp(-2j * np.pi * np.outer(a2, a2) / n2)
    w_fwd = np.block([[f2.real, f2.imag], [-f2.imag, f2.real]])
    w_inv = np.block([[f2.real, -f2.imag], [f2.imag, f2.real]])
    return (jnp.asarray(lhs_fwd, BF16), jnp.asarray(lhs_inv, BF16), jnp.asarray(tw.real, F32),
            jnp.asarray(tw.imag, F32), jnp.asarray(w_fwd, BF16), jnp.asarray(w_inv, BF16))


def _hy_dft(x3, lhs_fwd, twr, twi, w_fwd):
    n1 = twr.shape[0]
    a = [jnp.dot(lhs_fwd, x3[c].astype(BF16), preferred_element_type=F32) for c in range(x3.shape[0])]
    br = jnp.concatenate([t[:n1] * twr - t[n1:] * twi for t in a], axis=0)
    bi = jnp.concatenate([t[:n1] * twi + t[n1:] * twr for t in a], axis=0)
    b = jnp.concatenate([br, bi], axis=1)
    return jnp.dot(b.astype(BF16), w_fwd, preferred_element_type=F32)


def _hy_idft(p, cb, lhs_inv, twr, twi, w_inv):
    n1, n2 = twr.shape
    c = jnp.dot(p.astype(BF16), w_inv, preferred_element_type=F32)
    out = []
    for ch in range(cb):
        cr = c[ch * n1:(ch + 1) * n1, :n2]
        ci = c[ch * n1:(ch + 1) * n1, n2:]
        d = jnp.concatenate([cr * twr + ci * twi, ci * twr - cr * twi], axis=0)
        out.append(jnp.dot(lhs_inv, d.astype(BF16), preferred_element_type=F32))
    return out


def _hy_spectrum_kernel(hf_ref, hb_ref, lf_ref, twr_ref, twi_ref, wf_ref, o_ref):
    cb, nr, n2 = hf_ref.shape
    first = ((lax.broadcasted_iota(jnp.int32, (nr, n2), 0) == 0)
             & (lax.broadcasted_iota(jnp.int32, (nr, n2), 1) == 0))
    hb = jnp.where(first, 0.0, hb_ref[...])
    consts = (lf_ref[...], twr_ref[...], twi_ref[...], wf_ref[...])
    xf = _hy_dft(hf_ref[...], *consts)
    xb = _hy_dft(hb, *consts)
    o_ref[...] = jnp.concatenate([xf[:, :n2] + xb[:, :n2], xf[:, n2:] - xb[:, n2:]],
                                 axis=1).reshape(o_ref.shape)


def _hy_spectrum(filt, consts):
    d = D_MODEL
    l = filt.shape[1]
    nr = l // HY_N2
    lhs_fwd, _, twr, twi, w_fwd, _ = consts
    n1 = twr.shape[0]
    cpo = d // HY_CB
    fixed2 = lambda o, c: (0, 0)
    return pl.pallas_call(
        _hy_spectrum_kernel,
        out_shape=jax.ShapeDtypeStruct((HY_ORDER * d, n1, 2 * HY_N2), F32),
        grid=(HY_ORDER, cpo),
        in_specs=[pl.BlockSpec((HY_CB, nr, HY_N2), lambda o, c: (2 * o * cpo + c, 0, 0)),
                  pl.BlockSpec((HY_CB, nr, HY_N2), lambda o, c: ((2 * o + 1) * cpo + c, 0, 0)),
                  pl.BlockSpec(lhs_fwd.shape, fixed2),
                  pl.BlockSpec(twr.shape, fixed2),
                  pl.BlockSpec(twi.shape, fixed2),
                  pl.BlockSpec(w_fwd.shape, fixed2)],
        out_specs=pl.BlockSpec((HY_CB, n1, 2 * HY_N2), lambda o, c: (o * cpo + c, 0, 0)),
        compiler_params=pltpu.CompilerParams(
            dimension_semantics=("arbitrary", "arbitrary"), vmem_limit_bytes=VMEM_LIMIT_BYTES),
        name="hy_spectrum",
    )(filt.reshape(-1, nr, HY_N2), filt.reshape(-1, nr, HY_N2), lhs_fwd, twr, twi, w_fwd)


def _hy_conv_kernel(z_ref, g_ref, k_ref, bias_ref, lf_ref, li_ref, twr_ref, twi_ref, wf_ref, wi_ref,
                    o_ref):
    cb, nr, n2 = z_ref.shape
    twr, twi = twr_ref[...], twi_ref[...]
    z = z_ref[...]
    x = _hy_dft(z, lf_ref[...], twr, twi, wf_ref[...])
    kk = k_ref[...].reshape(x.shape)
    xr, xi, kr, ki = x[:, :n2], x[:, n2:], kk[:, :n2], kk[:, n2:]
    p = jnp.concatenate([xr * kr - xi * ki, xr * ki + xi * kr], axis=1)
    conv = _hy_idft(p, cb, li_ref[...], twr, twi, wi_ref[...])
    for c in range(cb):
        o_ref[c] = g_ref[c] * (conv[c] + bias_ref[c] * z[c])


def _hy_conv(z, z_part, gate, gate_part, khat, order, bias, consts, bsz):
    d = D_MODEL
    l = z.shape[1] // bsz
    nr = l // HY_N2
    cpo = d // HY_CB
    lhs_fwd, lhs_inv, twr, twi, w_fwd, w_inv = consts
    n1 = twr.shape[0]
    fixed2 = lambda c, b: (0, 0)
    out = pl.pallas_call(
        _hy_conv_kernel,
        out_shape=jax.ShapeDtypeStruct((d, bsz * nr, HY_N2), F32),
        grid=(cpo, bsz),
        in_specs=[pl.BlockSpec((HY_CB, nr, HY_N2), lambda c, b: (z_part * cpo + c, b, 0)),
                  pl.BlockSpec((HY_CB, nr, HY_N2), lambda c, b: (gate_part * cpo + c, b, 0)),
                  pl.BlockSpec((HY_CB, n1, 2 * HY_N2), lambda c, b: (order * cpo + c, 0, 0)),
                  pl.BlockSpec((HY_CB, 1, 1), lambda c, b: (c, 0, 0)),
                  pl.BlockSpec(lhs_fwd.shape, fixed2),
                  pl.BlockSpec(lhs_inv.shape, fixed2),
                  pl.BlockSpec(twr.shape, fixed2),
                  pl.BlockSpec(twi.shape, fixed2),
                  pl.BlockSpec(w_fwd.shape, fixed2),
                  pl.BlockSpec(w_inv.shape, fixed2)],
        out_specs=pl.BlockSpec((HY_CB, nr, HY_N2), lambda c, b: (c, b, 0)),
        compiler_params=pltpu.CompilerParams(
            dimension_semantics=("arbitrary", "arbitrary"), vmem_limit_bytes=VMEM_LIMIT_BYTES),
        name="hy_conv",
    )(z.reshape(-1, bsz * nr, HY_N2), gate.reshape(-1, bsz * nr, HY_N2), khat,
      bias.reshape(d, 1, 1), lhs_fwd, lhs_inv, twr, twi, w_fwd, w_inv)
    return out.reshape(d, bsz * l)


def _hy_ctx_kernel(z_ref, g_ref, hf_ref, hb_ref, bias_ref, wf_ref, wi_ref, o_ref, *, bsz):
    l = hf_ref.shape[1]
    wf, wi = wf_ref[...], wi_ref[...]
    hb = jnp.where(lax.broadcasted_iota(jnp.int32, (1, l), 1) == 0, 0.0, hb_ref[...])
    kf = jnp.dot(hf_ref[...].astype(BF16), wf, preferred_element_type=F32)
    kb = jnp.dot(hb.astype(BF16), wf, preferred_element_type=F32)
    n = 2 * l
    kr, ki = kf[:, :n] + kb[:, :n], kf[:, n:] - kb[:, n:]
    bias = bias_ref[...]
    for b in range(bsz):
        z = z_ref[:, b * l:(b + 1) * l]
        x = jnp.dot(z.astype(BF16), wf, preferred_element_type=F32)
        xr, xi = x[:, :n], x[:, n:]
        p = jnp.concatenate([xr * kr - xi * ki, xr * ki + xi * kr], axis=1)
        conv = jnp.dot(p.astype(BF16), wi, preferred_element_type=F32)
        o_ref[:, b * l:(b + 1) * l] = g_ref[:, b * l:(b + 1) * l] * (conv + bias * z)


def _hy_ctx(z, z_part, gate, gate_part, filt, order, bias, bsz):
    d = D_MODEL
    l = filt.shape[1]
    n = 2 * l
    ang = 2 * np.pi * np.outer(np.arange(l), np.arange(n)) / n
    w_fwd = jnp.asarray(np.concatenate([np.cos(ang), -np.sin(ang)], axis=1), BF16)
    w_inv = jnp.asarray(np.concatenate([np.cos(ang.T), -np.sin(ang.T)], axis=0) / n, BF16)
    cb = 256
    nblk = d // cb
    fixed = lambda c: (0, 0)
    return pl.pallas_call(
        functools.partial(_hy_ctx_kernel, bsz=bsz),
        out_shape=jax.ShapeDtypeStruct((d, bsz * l), F32),
        grid=(nblk,),
        in_specs=[pl.BlockSpec((cb, bsz * l), lambda c: (z_part * nblk + c, 0)),
                  pl.BlockSpec((cb, bsz * l), lambda c: (gate_part * nblk + c, 0)),
                  pl.BlockSpec((cb, l), lambda c: (2 * order * nblk + c, 0)),
                  pl.BlockSpec((cb, l), lambda c: ((2 * order + 1) * nblk + c, 0)),
                  pl.BlockSpec((cb, 1), lambda c: (c, 0)),
                  pl.BlockSpec(w_fwd.shape, fixed),
                  pl.BlockSpec(w_inv.shape, fixed)],
        out_specs=pl.BlockSpec((cb, bsz * l), lambda c: (c, 0)),
        compiler_params=pltpu.CompilerParams(
            dimension_semantics=("arbitrary",), vmem_limit_bytes=VMEM_LIMIT_BYTES),
        name="hy_ctx_conv",
    )(z, gate, filt, filt, bias.reshape(d, 1), w_fwd, w_inv)


def _hy_filter_kernel(band_ref, w1t_ref, w1c_ref, w1s_ref, b1_ref, fr_ref, w2_ref, b2_ref, w3_ref,
                      delta_ref, o_ref, *, l):
    tl = o_ref.shape[1]
    d = D_MODEL
    hp = lax.Precision.HIGHEST
    pos = (lax.broadcasted_iota(jnp.int32, (1, tl), 1) + pl.program_id(0) * tl).astype(F32)
    t = pos / max(l - 1, 1)
    ang = ((2 * math.pi / l) * pos) * band_ref[...]
    fr = fr_ref[...]
    pre = (w1t_ref[...] * t + jnp.dot(w1c_ref[...], jnp.cos(ang), precision=hp)
           + jnp.dot(w1s_ref[...], -jnp.sin(ang), precision=hp) + b1_ref[...])
    hdn = jnp.sin(fr * pre)
    hdn = jnp.sin(fr * (jnp.dot(w2_ref[...], hdn, precision=hp) + b2_ref[...]))
    window = jnp.exp(-t * delta_ref[...])
    for part in range(2 * HY_ORDER):
        rows = slice(part * d, (part + 1) * d)
        o_ref[rows, :] = jnp.dot(w3_ref[rows, :], hdn.astype(BF16), preferred_element_type=F32) * window


def _hy_filter(l, w1, b1, freq, w2, b2, w3):
    d = D_MODEL
    nb = HY_BANDS
    tl = min(HY_FILT_TILE, l)
    col = lambda v: v.reshape(-1, 1)
    bands = jnp.linspace(1e-4, nb - 1, nb, dtype=F32)
    deltas = jnp.abs(jnp.linspace(math.log(HY_TARGET) / HY_SLOW, math.log(HY_TARGET) / HY_FAST, d, dtype=F32))
    w1t = w1.T
    args = (col(bands), w1t[:, 0:1], w1t[:, 1:1 + nb], w1t[:, 1 + nb:], col(b1), col(freq), w2.T, col(b2),
            w3.T.astype(BF16), col(deltas))
    return pl.pallas_call(
        functools.partial(_hy_filter_kernel, l=l),
        out_shape=jax.ShapeDtypeStruct((2 * HY_ORDER * d, l), F32),
        grid=(l // tl,),
        in_specs=[pl.BlockSpec(a.shape, lambda j: (0, 0)) for a in args],
        out_specs=pl.BlockSpec((2 * HY_ORDER * d, tl), lambda j: (0, j)),
        compiler_params=pltpu.CompilerParams(
            dimension_semantics=("arbitrary",), vmem_limit_bytes=VMEM_LIMIT_BYTES),
        name="hy_filter",
    )(*args)


def _hy_inproj_kernel(x_ref, gain_ref, shift_ref, scale_ref, wt_ref, cw_ref, o_ref, *, seg):
    nch = wt_ref.shape[0]
    tm = x_ref.shape[0]
    hb = _modulated(x_ref[...], gain_ref[...], shift_ref[0], scale_ref[0]).astype(BF16)
    pos = lax.broadcasted_iota(jnp.int32, (1, tm), 1) & (seg - 1)
    not_first = pos != 0
    not_last = pos != seg - 1
    sub = 512
    for j in range(nch // sub):
        rows = slice(j * sub, (j + 1) * sub)
        p = _dotb_nt(wt_ref[rows, :], hb)
        cw = cw_ref[rows, :]
        prev = jnp.where(not_first, pltpu.roll(p, 1, axis=1), 0.0)
        nxt = jnp.where(not_last, pltpu.roll(p, tm - 1, axis=1), 0.0)
        o_ref[rows, :] = cw[:, 0:1] * prev + cw[:, 1:2] * p + cw[:, 2:3] * nxt


def _hy_inproj(xs, gain, shift3, scale3, w_in, conv_w, first_tile, n_tiles, seg, n_lat_blocks,
               blocks_per_batch):
    k = xs.shape[1]
    nch = w_in.shape[1]
    tm = HY_TOK_TILE
    per_tile = tm // DN_BLOCK
    grp = lambda i: (_group_of_block((first_tile + i) * per_tile, n_lat_blocks, blocks_per_batch), 0, 0)
    return pl.pallas_call(
        functools.partial(_hy_inproj_kernel, seg=seg),
        out_shape=jax.ShapeDtypeStruct((nch, n_tiles * tm), F32),
        grid=(n_tiles,),
        in_specs=[pl.BlockSpec((tm, k), lambda i: (first_tile + i, 0)),
                  pl.BlockSpec((1, k), lambda i: (0, 0)),
                  pl.BlockSpec((1, 1, k), grp),
                  pl.BlockSpec((1, 1, k), grp),
                  pl.BlockSpec((nch, k), lambda i: (0, 0)),
                  pl.BlockSpec((nch, 3), lambda i: (0, 0))],
        out_specs=pl.BlockSpec((nch, tm), lambda i: (0, i)),
        compiler_params=pltpu.CompilerParams(
            dimension_semantics=("arbitrary",), vmem_limit_bytes=VMEM_LIMIT_BYTES),
        name="hy_inproj",
    )(xs, gain.reshape(1, k), shift3, scale3, w_in.T.astype(BF16), conv_w.T)


def _hy_out_kernel(zl_ref, zc_ref, w_ref, x_ref, gate_ref, o_ref, *, n_lat_tiles):
    z = jnp.where(pl.program_id(0) >= n_lat_tiles, zc_ref[...], zl_ref[...])
    o_ref[...] = x_ref[...] + gate_ref[0] * _dotb_tn(z, w_ref[...])


def _hy_out(z_lat, z_ctx, w_out, xs, gate3, blocks_per_batch):
    t, d = xs.shape
    tm = HY_TOK_TILE
    n_lat_tiles = z_lat.shape[1] // tm
    per_tile = tm // DN_BLOCK
    grp = lambda i: (_group_of_block(i * per_tile, n_lat_tiles * per_tile, blocks_per_batch), 0, 0)
    return pl.pallas_call(
        functools.partial(_hy_out_kernel, n_lat_tiles=n_lat_tiles),
        out_shape=jax.ShapeDtypeStruct((t, d), F32),
        grid=(t // tm,),
        in_specs=[pl.BlockSpec((d, tm), lambda i: (0, jnp.minimum(i, n_lat_tiles - 1))),
                  pl.BlockSpec((d, tm), lambda i: (0, 0)),
                  pl.BlockSpec((d, d), lambda i: (0, 0)),
                  pl.BlockSpec((tm, d), lambda i: (i, 0)),
                  pl.BlockSpec((1, 1, d), grp)],
        out_specs=pl.BlockSpec((tm, d), lambda i: (i, 0)),
        compiler_params=pltpu.CompilerParams(
            dimension_semantics=("arbitrary",), vmem_limit_bytes=VMEM_LIMIT_BYTES),
        name="hy_out",
    )(z_lat, z_ctx, w_out.astype(BF16), xs, gate3)


def _hyena_layer(xs, gain, shift3, scale3, w_in, conv_w, f_w1, f_b1, f_freq, f_w2, f_b2, f_w3, bias, w_out,
                 gate3, bsz, seq):
    n_lat_tiles = bsz * seq // HY_TOK_TILE
    assert bsz * CTX_LEN == HY_TOK_TILE
    margs = (xs, gain, shift3, scale3, w_in, conv_w)
    blocks = (bsz * seq // DN_BLOCK, seq // DN_BLOCK)
    p_lat = _hy_inproj(*margs, 0, n_lat_tiles, GRID_W, *blocks)
    p_ctx = _hy_inproj(*margs, n_lat_tiles, 1, CTX_LEN, *blocks)
    fargs = (f_w1, f_b1, f_freq, f_w2, f_b2, f_w3)
    consts = _dft_constants(seq // HY_N2)
    khat = _hy_spectrum(_hy_filter(seq, *fargs), consts)
    filt_ctx = _hy_filter(CTX_LEN, *fargs)
    z_lat, z_ctx = p_lat, p_ctx
    for n in range(HY_ORDER):
        z_lat = _hy_conv(z_lat, 0, p_lat, n + 1, khat, n, bias[n], consts, bsz)
        z_ctx = _hy_ctx(z_ctx, 0, p_ctx, n + 1, filt_ctx, n, bias[n], bsz)
    return _hy_out(z_lat, z_ctx, w_out, xs, gate3, seq // DN_BLOCK)


MOE_TILE = 512
PACK = 2


def _experts_kernel(be_ref, na_ref, x_ref, wg_ref, wu_ref, wd_ref, o_ref, wgb_ref, wub_ref, wdb_ref):
    i = pl.program_id(0)
    prev = be_ref[jnp.maximum(i - 1, 0)]

    @pl.when((i == 0) | (be_ref[i] != prev))
    def _():
        wgb_ref[...] = wg_ref[0, 0].astype(BF16)
        wub_ref[...] = wu_ref[0, 0].astype(BF16)
        wdb_ref[...] = wd_ref[0, 0].astype(BF16)

    @pl.when(i < na_ref[0])
    def _():
        packed = x_ref[...]
        lo = pltpu.bitcast(packed << 16, F32)
        hi = pltpu.bitcast(packed & jnp.uint32(0xFFFF0000), F32)
        xb = jnp.concatenate([lo, hi], axis=-1).astype(BF16)
        g = jnp.dot(xb, wgb_ref[...], preferred_element_type=F32)
        u = jnp.dot(xb, wub_ref[...], preferred_element_type=F32)
        hid = (g * jax.nn.sigmoid(g)) * u
        y = jnp.dot(hid.astype(BF16), wdb_ref[...], preferred_element_type=F32)
        ybits = pltpu.bitcast(y.astype(BF16).astype(F32), jnp.uint32)
        half = y.shape[1] // PACK
        o_ref[...] = (ybits[:, :half] >> 16) | (ybits[:, half:] & jnp.uint32(0xFFFF0000))

    @pl.when(i >= na_ref[0])
    def _():
        o_ref[...] = jnp.zeros_like(o_ref)


def _experts(xs_sorted, block_expert, n_active, w_gate, w_up, w_down, layer):
    n_slots, wd = xs_sorted.shape
    d = wd * PACK
    n_blocks = n_slots // MOE_BLOCK
    blk = lambda i, be, na: (jnp.minimum(i, na[0] - 1), 0)
    wsel = lambda i, be, na: (layer, be[jnp.minimum(i, na[0] - 1)], 0, 0)
    grid_spec = pltpu.PrefetchScalarGridSpec(
        num_scalar_prefetch=2,
        grid=(n_blocks,),
        in_specs=[pl.BlockSpec((MOE_BLOCK, wd), blk),
                  pl.BlockSpec((1, 1, d, D_EXPERT), wsel),
                  pl.BlockSpec((1, 1, d, D_EXPERT), wsel),
                  pl.BlockSpec((1, 1, D_EXPERT, d), wsel)],
        out_specs=pl.BlockSpec((MOE_BLOCK, wd), lambda i, be, na: (i, 0)),
        scratch_shapes=[pltpu.VMEM((d, D_EXPERT), BF16), pltpu.VMEM((d, D_EXPERT), BF16),
                        pltpu.VMEM((D_EXPERT, d), BF16)],
    )
    return pl.pallas_call(
        _experts_kernel,
        out_shape=jax.ShapeDtypeStruct((n_slots, wd), jnp.uint32),
        grid_spec=grid_spec,
        compiler_params=pltpu.CompilerParams(
            dimension_semantics=("arbitrary",), vmem_limit_bytes=VMEM_LIMIT_BYTES),
        name="moe_experts",
    )(block_expert, n_active, xs_sorted, w_gate, w_up, w_down)


MOE_CHUNK = 8
MOE_LB = 1280
assert MOE_LB >= MOE_TILE * TOP_K + N_EXPERTS * (MOE_CHUNK - 1) and MOE_LB % 128 == 0
N_CHUNKS = MOE_LB // MOE_CHUNK
TAB_W = 256
assert TAB_W >= N_CHUNKS


def _moe_route_kernel(x_ref, gain_ref, shift_ref, scale_ref, wr_ref, rb_ref, tri_ref, lt_ref,
                      f_ref, pos_ref, w_ref, cnt_ref):
    tm, d = x_ref.shape
    ne, epg, ng = N_EXPERTS, EXPERTS_PER_GROUP, N_GROUPS
    f = _modulated(x_ref[...], gain_ref[...], shift_ref[0], scale_ref[0])
    f_ref[...] = f.astype(BF16)

    logits = lax.dot_general(wr_ref[...], f, (((1,), (1,)), ((), ())),
                             precision=lax.Precision.HIGHEST, preferred_element_type=F32)
    scores = jax.nn.sigmoid(logits)
    biased = scores + rb_ref[...]
    s = [scores[j * ng:(j + 1) * ng] for j in range(epg)]
    c = [biased[j * ng:(j + 1) * ng] for j in range(epg)]
    hi01, lo01 = jnp.maximum(c[0], c[1]), jnp.minimum(c[0], c[1])
    hi23, lo23 = jnp.maximum(c[2], c[3]), jnp.minimum(c[2], c[3])
    gscore = jnp.maximum(hi01, hi23) + jnp.maximum(jnp.minimum(hi01, hi23), jnp.maximum(lo01, lo23))
    gi = lax.broadcasted_iota(jnp.int32, (ng, tm), 0)
    gmax = jnp.max(gscore, axis=0, keepdims=True)
    grp = jnp.min(jnp.where(gscore == gmax, gi, ng), axis=0, keepdims=True)
    sel = gi == grp
    cv = [jnp.sum(jnp.where(sel, t, 0.0), axis=0, keepdims=True) for t in c]
    sv = [jnp.sum(jnp.where(sel, t, 0.0), axis=0, keepdims=True) for t in s]

    def pick(excluded):
        best = jnp.full((1, tm), -jnp.inf, F32)
        idx = jnp.zeros((1, tm), jnp.int32)
        val = jnp.zeros((1, tm), F32)
        for j in range(epg):
            cand = cv[j] if excluded is None else jnp.where(excluded == j, -jnp.inf, cv[j])
            take = cand > best
            best = jnp.where(take, cand, best)
            idx = jnp.where(take, j, idx)
            val = jnp.where(take, sv[j], val)
        return idx, val

    i1, v1 = pick(None)
    i2, v2 = pick(i1)
    wsum = v1 + v2
    w_ref[0:1, :] = v1 / wsum
    w_ref[1:2, :] = v2 / wsum

    ei = lax.broadcasted_iota(jnp.int32, (ne, tm), 0)
    oh1 = ei == grp * epg + i1
    oh2 = ei == grp * epg + i2
    tri = tri_ref[...]
    pre1 = jnp.dot(oh1.astype(BF16), tri, preferred_element_type=F32)
    pre2 = jnp.dot(oh2.astype(BF16), tri, preferred_element_type=F32)
    tot1 = pre1[:, tm - 1:tm]
    tot = tot1 + pre2[:, tm - 1:tm]
    seg = jnp.floor((tot + (MOE_CHUNK - 1)) * (1.0 / MOE_CHUNK)) * MOE_CHUNK
    off = jnp.dot(lt_ref[...], jnp.broadcast_to(seg, (ne, 128)).astype(BF16),
                  preferred_element_type=F32)[:, 0:1]
    p1 = jnp.sum(jnp.where(oh1, off + pre1 - 1.0, 0.0), axis=0, keepdims=True)
    p2 = jnp.sum(jnp.where(oh2, off + tot1 + pre2 - 1.0, 0.0), axis=0, keepdims=True)
    pos_ref[0:1, :] = p1.astype(jnp.int32)
    pos_ref[1:2, :] = p2.astype(jnp.int32)
    cnt_ref[0] = jnp.broadcast_to(tot, (ne, 128))


def _moe_route(xs, gain, shift3, scale3, w_router, router_bias, n_lat_tiles, blocks_per_batch):
    t, d = xs.shape
    tm = MOE_TILE
    ne = N_EXPERTS
    per_tile = tm // DN_BLOCK
    row = lambda i: (i, 0)
    col = lambda i: (0, i)
    fixed = lambda i: (0, 0)
    grp = lambda i: (_group_of_block(i * per_tile, n_lat_tiles * per_tile, blocks_per_batch), 0, 0)
    tri = jnp.asarray(np.triu(np.ones((tm, tm), np.float32)), BF16)
    lt = jnp.asarray(np.tril(np.ones((ne, ne), np.float32), -1), BF16)
    perm = np.arange(ne).reshape(N_GROUPS, EXPERTS_PER_GROUP).T.reshape(-1)
    return pl.pallas_call(
        _moe_route_kernel,
        out_shape=[jax.ShapeDtypeStruct((t, d), BF16),
                   jax.ShapeDtypeStruct((TOP_K, t), jnp.int32),
                   jax.ShapeDtypeStruct((TOP_K, t), F32),
                   jax.ShapeDtypeStruct((t // tm, ne, 128), F32)],
        grid=(t // tm,),
        in_specs=[pl.BlockSpec((tm, d), row),
                  pl.BlockSpec((1, d), fixed),
                  pl.BlockSpec((1, 1, d), grp),
                  pl.BlockSpec((1, 1, d), grp),
                  pl.BlockSpec((ne, d), fixed),
                  pl.BlockSpec((ne, 1), fixed),
                  pl.BlockSpec((tm, tm), fixed),
                  pl.BlockSpec((ne, ne), fixed)],
        out_specs=[pl.BlockSpec((tm, d), row),
                   pl.BlockSpec((TOP_K, tm), col),
                   pl.BlockSpec((TOP_K, tm), col),
                   pl.BlockSpec((1, ne, 128), lambda i: (i, 0, 0))],
        compiler_params=pltpu.CompilerParams(
            dimension_semantics=("arbitrary",), vmem_limit_bytes=VMEM_LIMIT_BYTES),
        name="moe_route",
    )(xs, gain.reshape(1, d), shift3, scale3, w_router.T[perm], router_bias[perm].reshape(ne, 1), tri, lt)


def _chunk_row(j):
    return j * MOE_CHUNK if isinstance(j, int) else pl.multiple_of(j * MOE_CHUNK, MOE_CHUNK)


def _chunk_issue(tab_smem, tab_row, make_copy):
    def issue_pair(jj, n):
        for priority in range(2):
            j = jj * 2 + priority
            dst = tab_smem[tab_row, j]

            @pl.when(dst >= 0)
            def _():
                make_copy(j, pl.multiple_of(dst, MOE_CHUNK)).start(priority=priority)

            n = n + (dst >= 0).astype(jnp.int32)
        return n

    return lax.fori_loop(0, N_CHUNKS // 2, issue_pair, jnp.int32(0), unroll=4)


def _chunk_drain(n, make_copy):
    def drain(j, carry):
        make_copy(0, 0).wait()
        return carry

    lax.fori_loop(0, n, drain, 0)


def _moe_dispatch_kernel(tab_ref, pos_ref, f_ref, xs_in_ref, xs_ref, cnt_smem, loc_ref, sem, *, n_steps):
    del xs_in_ref
    tm, d = f_ref.shape
    i = pl.program_id(0)
    slot = i % 2
    r = lax.broadcasted_iota(jnp.int32, (tm, MOE_LB), 1)
    p = pos_ref[...]
    onehot = ((p[:, 0:1] == r) | (p[:, 1:2] == r)).astype(BF16)
    loc = _dotb_tn(onehot, f_ref[...])
    bits = pltpu.bitcast(loc, jnp.uint32)
    half = d // PACK
    loc_ref[slot] = (bits[:, :half] >> 16) | (bits[:, half:] & jnp.uint32(0xFFFF0000))

    def copy_from(buf):
        def make_copy(j, dst):
            return pltpu.make_async_copy(loc_ref.at[buf, pl.ds(_chunk_row(j), MOE_CHUNK)],
                                         xs_ref.at[pl.ds(dst, MOE_CHUNK)], sem.at[buf])
        return make_copy

    n = _chunk_issue(tab_ref, i, copy_from(slot))
    cnt_smem[slot] = n

    @pl.when(i > 0)
    def _():
        _chunk_drain(cnt_smem[1 - slot], copy_from(1 - slot))

    @pl.when(i == n_steps - 1)
    def _():
        _chunk_drain(n, copy_from(slot))


def _moe_dispatch(f, pos_cols, table, slots):
    t, d = f.shape
    tm = MOE_TILE
    n_slots, wd = slots.shape
    grid_spec = pltpu.PrefetchScalarGridSpec(
        num_scalar_prefetch=1,
        grid=(t // tm,),
        in_specs=[pl.BlockSpec((tm, TOP_K), lambda i, tab: (i, 0)),
                  pl.BlockSpec((tm, d), lambda i, tab: (i, 0)),
                  pl.BlockSpec(memory_space=pl.ANY)],
        out_specs=pl.BlockSpec(memory_space=pl.ANY),
        scratch_shapes=[pltpu.SMEM((2,), jnp.int32),
                        pltpu.VMEM((2, MOE_LB, wd), jnp.uint32),
                        pltpu.SemaphoreType.DMA((2,))],
    )
    return pl.pallas_call(
        functools.partial(_moe_dispatch_kernel, n_steps=t // tm),
        out_shape=jax.ShapeDtypeStruct((n_slots, wd), jnp.uint32),
        grid_spec=grid_spec,
        input_output_aliases={3: 0},
        compiler_params=pltpu.CompilerParams(
            dimension_semantics=("arbitrary",), vmem_limit_bytes=VMEM_LIMIT_BYTES),
        name="moe_dispatch",
    )(table, pos_cols, f, slots)


def _moe_combine_kernel(tab_ref, pos_ref, w_ref, y_ref, x_ref, gate_ref, fin_ref, o_ref, cnt_smem, yloc_ref,
                        sem, *, final_norm, n_steps):
    tm = x_ref.shape[0]
    i = pl.program_id(0)
    slot = i % 2

    def fetch(tile, buf):
        cnt_smem[buf] = _chunk_issue(tab_ref, tile, copy_into(buf))

    def copy_into(buf):
        def make_copy(j, src):
            return pltpu.make_async_copy(y_ref.at[pl.ds(src, MOE_CHUNK)],
                                         yloc_ref.at[buf, pl.ds(_chunk_row(j), MOE_CHUNK)], sem.at[buf])
        return make_copy

    @pl.when(i == 0)
    def _():
        yloc_ref[...] = jnp.zeros_like(yloc_ref)
        fetch(0, 0)

    @pl.when(i + 1 < n_steps)
    def _():
        fetch(i + 1, 1 - slot)

    _chunk_drain(cnt_smem[slot], copy_into(slot))
    r = lax.broadcasted_iota(jnp.int32, (tm, MOE_LB), 1)
    p = pos_ref[...]
    w = w_ref[...]
    wmat = jnp.where(p[:, 0:1] == r, w[:, 0:1], 0.0) + jnp.where(p[:, 1:2] == r, w[:, 1:2], 0.0)
    packed = yloc_ref[slot]
    y_lo = pltpu.bitcast(packed << 16, F32)
    y_hi = pltpu.bitcast(packed & jnp.uint32(0xFFFF0000), F32)
    moe = jnp.concatenate([_dotb(wmat, y_lo), _dotb(wmat, y_hi)], axis=-1)
    out = x_ref[...] + gate_ref[0] * moe
    if final_norm:
        out = out * lax.rsqrt(jnp.mean(out * out, axis=-1, keepdims=True) + EPS) * fin_ref[...]
    o_ref[...] = out


def _moe_combine(ys, pos_cols, weight_cols, table, xs, gate3, n_lat_tiles, blocks_per_batch, final_gain=None):
    t, d = xs.shape
    tm = MOE_TILE
    per_tile = tm // DN_BLOCK
    n_tiles = t // tm if final_gain is None else n_lat_tiles
    fin = jnp.ones((1, d), F32) if final_gain is None else final_gain.reshape(1, d)
    row = lambda i, tab: (i, 0)
    grp = lambda i, tab: (_group_of_block(i * per_tile, n_lat_tiles * per_tile, blocks_per_batch), 0, 0)
    grid_spec = pltpu.PrefetchScalarGridSpec(
        num_scalar_prefetch=1,
        grid=(n_tiles,),
        in_specs=[pl.BlockSpec((tm, TOP_K), row),
                  pl.BlockSpec((tm, TOP_K), row),
                  pl.BlockSpec(memory_space=pl.ANY),
                  pl.BlockSpec((tm, d), row),
                  pl.BlockSpec((1, 1, d), grp),
                  pl.BlockSpec((1, d), lambda i, tab: (0, 0))],
        out_specs=pl.BlockSpec((tm, d), row),
        scratch_shapes=[pltpu.SMEM((2,), jnp.int32),
                        pltpu.VMEM((2, MOE_LB, d // PACK), jnp.uint32),
                        pltpu.SemaphoreType.DMA((2,))],
    )
    return pl.pallas_call(
        functools.partial(_moe_combine_kernel, final_norm=final_gain is not None, n_steps=n_tiles),
        out_shape=jax.ShapeDtypeStruct((n_tiles * tm, d), F32),
        grid_spec=grid_spec,
        compiler_params=pltpu.CompilerParams(
            dimension_semantics=("arbitrary",), vmem_limit_bytes=VMEM_LIMIT_BYTES),
        name="moe_combine",
    )(table, pos_cols, weight_cols, ys, xs, gate3, fin)


def _moe_layer(xs, gain, shift3, scale3, gate3, w_router, router_bias, w_gate, w_up, w_down, layer,
               n_lat_tiles, blocks_per_batch, final_gain=None, slots=None):
    t = xs.shape[0]
    n_tiles = t // MOE_TILE
    f, pos, weight, cnt = _moe_route(xs, gain, shift3, scale3, w_router, router_bias,
                                     n_lat_tiles, blocks_per_batch)
    seg = (cnt[:, :, 0].astype(jnp.int32) + MOE_CHUNK - 1) // MOE_CHUNK * MOE_CHUNK
    loc_end = jnp.cumsum(seg, axis=1)
    loc_off = loc_end - seg
    padded = (jnp.sum(seg, axis=0) + MOE_BLOCK - 1) // MOE_BLOCK * MOE_BLOCK
    pend = jnp.cumsum(padded)
    seg_start = (pend - padded)[None, :] + jnp.cumsum(seg, axis=0) - seg
    n_blocks = -(-(t * TOP_K + n_tiles * N_EXPERTS * (MOE_CHUNK - 1)) // MOE_BLOCK) + N_EXPERTS
    block_start = jnp.arange(n_blocks, dtype=jnp.int32) * MOE_BLOCK
    block_expert = jnp.minimum(jnp.sum(pend[None, :] <= block_start[:, None], axis=1),
                               N_EXPERTS - 1).astype(jnp.int32)
    n_active = (pend[-1:] // MOE_BLOCK).astype(jnp.int32)
    row0 = jnp.arange(N_CHUNKS, dtype=jnp.int32) * MOE_CHUNK
    e_of = jnp.sum(loc_end[:, None, :] <= row0[None, :, None], axis=-1)
    is_e = e_of[..., None] == jnp.arange(N_EXPERTS, dtype=jnp.int32)
    shift = jnp.sum(jnp.where(is_e, (seg_start - loc_off)[:, None, :], 0), axis=-1)
    table = jnp.where(e_of < N_EXPERTS, row0[None, :] + shift, -1)
    table = jnp.pad(table, ((0, 0), (0, TAB_W - N_CHUNKS)), constant_values=-1)

    pos_cols = pos.T
    if slots is None:
        slots = jnp.zeros((n_blocks * MOE_BLOCK, f.shape[1] // PACK), jnp.uint32)
    xs_sorted = _moe_dispatch(f, pos_cols, table, slots)
    ys = _experts(xs_sorted, block_expert, n_active, w_gate, w_up, w_down, layer)
    out = _moe_combine(ys, pos_cols, weight.T, table, xs, gate3, n_lat_tiles, blocks_per_batch, final_gain)
    return out, xs_sorted


def _sc_layer_kernel(x_ref, gain_ref, shift_ref, scale_ref, win_ref, cw_ref, wout_ref, gate_ref, o_ref, *,
                     n_lat_tiles):
    d = D_MODEL
    tm = x_ref.shape[0]
    seg = jnp.where(pl.program_id(0) >= n_lat_tiles, CTX_LEN, GRID_W)
    pos = lax.broadcasted_iota(jnp.int32, (tm, 1), 0) & (seg - 1)
    hb = _modulated(x_ref[...], gain_ref[...], shift_ref[0], scale_ref[0]).astype(BF16)
    u = (jnp.dot(hb, win_ref[:, d:2 * d], preferred_element_type=F32)
         * jnp.dot(hb, win_ref[:, 2 * d:], preferred_element_type=F32))
    prev = jnp.where(pos != 0, pltpu.roll(u, 1, axis=0), 0.0)
    nxt = jnp.where(pos != seg - 1, pltpu.roll(u, tm - 1, axis=0), 0.0)
    cw = cw_ref[...]
    y = jnp.dot(hb, win_ref[:, :d], preferred_element_type=F32) * (
        cw[0:1] * prev + cw[1:2] * u + cw[2:3] * nxt)
    o_ref[...] = x_ref[...] + gate_ref[0] * jnp.dot(y.astype(BF16), wout_ref[...],
                                                    preferred_element_type=F32)


def _shortconv_layer(xs, gain, shift3, scale3, w_in, conv_w, w_out, gate3, n_lat_tiles, blocks_per_batch):
    t, d = xs.shape
    tm = HY_TOK_TILE
    per_tile = tm // DN_BLOCK
    row = lambda i: (i, 0)
    fixed = lambda i: (0, 0)
    grp = lambda i: (_group_of_block(i * per_tile, n_lat_tiles * per_tile, blocks_per_batch), 0, 0)
    return pl.pallas_call(
        functools.partial(_sc_layer_kernel, n_lat_tiles=n_lat_tiles),
        out_shape=jax.ShapeDtypeStruct((t, d), F32),
        grid=(t // tm,),
        in_specs=[pl.BlockSpec((tm, d), row),
                  pl.BlockSpec((1, d), fixed),
                  pl.BlockSpec((1, 1, d), grp),
                  pl.BlockSpec((1, 1, d), grp),
                  pl.BlockSpec((d, 3 * d), fixed),
                  pl.BlockSpec((3, d), fixed),
                  pl.BlockSpec((d, d), fixed),
                  pl.BlockSpec((1, 1, d), grp)],
        out_specs=pl.BlockSpec((tm, d), row),
        compiler_params=pltpu.CompilerParams(
            dimension_semantics=("arbitrary",), vmem_limit_bytes=VMEM_LIMIT_BYTES),
        name="shortconv_layer",
    )(xs, gain.reshape(1, d), shift3, scale3, w_in.astype(BF16), conv_w, w_out.astype(BF16), gate3)


def kernel(x, c, ctx, c_ctx, ada_w, ada_b, norm_mix, norm_ffn, norm_final, dn_w_in, dn_conv, dn_a_log,
           dn_dt_bias, dn_out_norm, dn_w_out, hy_w_in, hy_conv, hy_f_w1, hy_f_b1, hy_f_freq, hy_f_w2,
           hy_f_b2, hy_f_w3, hy_bias, hy_w_out, sc_w_in, sc_conv, sc_w_out, w_router, router_bias,
           moe_w_gate, moe_w_up, moe_w_down):
    d = D_MODEL
    bsz, seq, _ = x.shape
    n_ctx = bsz * CTX_LEN
    n_lat = bsz * seq
    silu_c = jax.nn.silu(c)
    silu_cc = jax.nn.silu(c_ctx)
    hp = lax.Precision.HIGHEST

    assert N_MIXERS > 0 and 0 % N_MIXERS == 0
    xs = (x.reshape(n_lat, d), ctx.reshape(n_ctx, d))
    n_lat_blocks, blocks_per_batch = n_lat // DN_BLOCK, seq // DN_BLOCK
    slots = None

    for i in range(DEPTH):
        kind, j = i % N_MIXERS, i // N_MIXERS
        ml = jnp.split(jnp.dot(silu_c, ada_w[i], precision=hp) + ada_b[i], N_MOD, axis=-1)
        mc = jnp.split(jnp.dot(silu_cc, ada_w[i], precision=hp) + ada_b[i], N_MOD, axis=-1)
        mod = [jnp.concatenate([mc[m][None], ml[m]], axis=0)[:, None, :] for m in range(N_MOD)]
        if kind == 0:
            xs = _deltanet_layer(xs, norm_mix[i], mod[0], mod[1], dn_w_in[j], dn_conv[j], dn_a_log[j],
                                 dn_dt_bias[j], dn_out_norm[j], dn_w_out[j], mod[2], n_lat_blocks,
                                 blocks_per_batch)
        elif kind == 1:
            xs = _hyena_layer(xs, norm_mix[i], mod[0], mod[1], hy_w_in[j], hy_conv[j], hy_f_w1[j], hy_f_b1[j],
                              hy_f_freq[j], hy_f_w2[j], hy_f_b2[j], hy_f_w3[j], hy_bias[j], hy_w_out[j],
                              mod[2], bsz, seq)
        else:
            xs = _shortconv_layer(xs, norm_mix[i], mod[0], mod[1], sc_w_in[j], sc_conv[j], sc_w_out[j],
                                  mod[2], n_lat // HY_TOK_TILE, blocks_per_batch)
        xs, slots = _moe_layer(xs, norm_ffn[i], mod[3], mod[4], mod[5], w_router, router_bias,
                               moe_w_gate, moe_w_up, moe_w_down, i, n_lat // MOE_TILE, blocks_per_batch,
                               norm_final if i == DEPTH - 1 else None, slots)
    return xs.reshape(bsz, seq, d)
```

```python
import functools
import math

import numpy as np
import jax
import jax.numpy as jnp
from jax import lax
from jax.experimental import pallas as pl
from jax.experimental.pallas import tpu as pltpu

D_MODEL = 1024
DEPTH = 4
CTX_LEN = 256
GRID_W = 64
N_MIXERS = 3
EPS = 1e-6
N_MOD = 6

DN_HEADS = 8
DN_DK = D_MODEL // DN_HEADS
DN_DV = D_MODEL // DN_HEADS
DN_CHUNK = 64

HY_ORDER = 2
HY_BANDS = 16
HY_TARGET = 1e-2
HY_FAST = 0.3
HY_SLOW = 1.5

N_EXPERTS = 32
N_GROUPS = 8
EXPERTS_PER_GROUP = N_EXPERTS // N_GROUPS
GROUP_SCORE_K = 2
TOP_K = 2
D_EXPERT = 512
MOE_BLOCK = 512

F32 = jnp.float32
BF16 = jnp.bfloat16

VMEM_LIMIT_BYTES = 48 * 1024 * 1024


DN_BLOCK = CTX_LEN
DN_HB = DN_HEADS
DN_OUT_TILE = 512
N_CHUNKS_PER_BLOCK = DN_BLOCK // DN_CHUNK


def _group_of_block(i, n_lat_blocks, blocks_per_batch):
    return jnp.where(i >= n_lat_blocks, 0, 1 + i // blocks_per_batch)


def _modulated(x, gain, shift, scale):
    y = x * lax.rsqrt(jnp.mean(x * x, axis=-1, keepdims=True) + EPS) * gain
    return y * (1 + scale) + shift


def _stream_rows(x_refs, is_ctx):
    if len(x_refs) == 1:
        return x_refs[0][...]
    return jnp.where(is_ctx, x_refs[1][...], x_refs[0][...])


def _dn_inproj_kernel(*refs, n_lat_blocks, n_src):
    x_refs = refs[:n_src]
    (gain_ref, shift_ref, scale_ref, w_ref, wab_ref, cw_ref, alog_ref, dtb_ref,
     q_ref, k_ref, v_ref, z_ref, gate_ref, gate_t_ref) = refs[n_src:]
    i = pl.program_id(0)
    nrow = DN_BLOCK
    d = D_MODEL
    pair = 2 * DN_DK
    seg = jnp.where(i >= n_lat_blocks, CTX_LEN, GRID_W)
    r = lax.broadcasted_iota(jnp.int32, (nrow, 1), 0)
    pos = r & (seg - 1)
    not_first = pos != 0
    not_last = pos != seg - 1
    h = _modulated(_stream_rows(x_refs, i >= n_lat_blocks), gain_ref[...], shift_ref[0], scale_ref[0])
    hb = h.astype(BF16)
    outs = (q_ref, k_ref, v_ref)
    for part in range(3):
        for hp in range(d // pair):
            col = part * d + hp * pair
            x = jnp.dot(hb, w_ref[:, col:col + pair], preferred_element_type=F32)
            cw = cw_ref[:, col:col + pair]
            xp = jnp.where(not_first, pltpu.roll(x, 1, axis=0), 0.0)
            xn = jnp.where(not_last, pltpu.roll(x, nrow - 1, axis=0), 0.0)
            y = cw[0:1] * xp + cw[1:2] * x + cw[2:3] * xn
            y = y * jax.nn.sigmoid(y)
            for hh in range(2):
                yh = y[:, hh * DN_DK:(hh + 1) * DN_DK]
                if part < 2:
                    yh = yh * lax.rsqrt(jnp.sum(yh * yh, axis=-1, keepdims=True) + EPS)
                if part == 0:
                    yh = yh * DN_DK ** -0.5
                outs[part][:, hp * pair + hh * DN_DK:hp * pair + (hh + 1) * DN_DK] = yh
    for j in range(d // pair):
        z_ref[:, j * pair:(j + 1) * pair] = jnp.dot(hb, w_ref[:, 3 * d + j * pair:3 * d + (j + 1) * pair],
                                                    preferred_element_type=F32).astype(BF16)

    ab = jnp.dot(hb, wab_ref[...], preferred_element_type=F32)
    nd = 2 * DN_HEADS
    a = ab[:, :nd] + dtb_ref[...]
    softplus = jnp.maximum(a, 0.0) + jnp.log(1.0 + jnp.exp(-jnp.abs(a)))
    g = -jnp.exp(alog_ref[...]) * softplus
    beta = jax.nn.sigmoid(ab[:, nd:])
    cpos = r & (DN_CHUNK - 1)
    gp, gs = g, g
    sh = 1
    while sh < DN_CHUNK:
        gp = gp + jnp.where(cpos >= sh, pltpu.roll(gp, sh, axis=0), 0.0)
        gs = gs + jnp.where(cpos < DN_CHUNK - sh, pltpu.roll(gs, nrow - sh, axis=0), 0.0)
        sh *= 2
    colid = lax.broadcasted_iota(jnp.int32, (1, nd), 1)
    gates = jnp.concatenate([jnp.where(colid < DN_HEADS, gp, gs), beta], axis=1)
    gate_ref[...] = gates
    gate_t_ref[...] = gates.T


def _stream_specs(xs, n_lat_tiles, rows=DN_BLOCK):
    if not isinstance(xs, tuple):
        return [pl.BlockSpec((rows, xs.shape[1]), lambda i: (i, 0))], xs.shape[0]
    d = xs[0].shape[1]
    return [pl.BlockSpec((rows, d), lambda i: (jnp.minimum(i, n_lat_tiles - 1), 0)),
            pl.BlockSpec((rows, d), lambda i: (jnp.maximum(i - n_lat_tiles, 0), 0))], sum(a.shape[0] for a in xs)


def _dn_inproj(xs, gain, shift3, scale3, w_in, conv_w, a_log, dt_bias, n_lat_blocks, blocks_per_batch):
    x_specs, t = _stream_specs(xs, n_lat_blocks)
    x_args = xs if isinstance(xs, tuple) else (xs,)
    d = D_MODEL
    nd = 2 * DN_HEADS
    row = lambda i: (i, 0)
    fixed = lambda i: (0, 0)
    grp = lambda i: (_group_of_block(i, n_lat_blocks, blocks_per_batch), 0, 0)
    return pl.pallas_call(
        functools.partial(_dn_inproj_kernel, n_lat_blocks=n_lat_blocks, n_src=len(x_args)),
        out_shape=[jax.ShapeDtypeStruct((t, d), F32)] * 3 + [jax.ShapeDtypeStruct((t, d), BF16),
                                                              jax.ShapeDtypeStruct((t, 2 * nd), F32),
                                                              jax.ShapeDtypeStruct((2 * nd, t), F32)],
        grid=(t // DN_BLOCK,),
        in_specs=x_specs + [
                  pl.BlockSpec((1, d), fixed),
                  pl.BlockSpec((1, 1, d), grp),
                  pl.BlockSpec((1, 1, d), grp),
                  pl.BlockSpec((d, 4 * d), fixed),
                  pl.BlockSpec((d, 2 * nd), fixed),
                  pl.BlockSpec((3, 3 * d), fixed),
                  pl.BlockSpec((1, nd), fixed),
                  pl.BlockSpec((1, nd), fixed)],
        out_specs=[pl.BlockSpec((DN_BLOCK, d), row)] * 4 + [pl.BlockSpec((DN_BLOCK, 2 * nd), row),
                                                            pl.BlockSpec((2 * nd, DN_BLOCK), lambda i: (0, i))],
        compiler_params=pltpu.CompilerParams(
            dimension_semantics=("arbitrary",), vmem_limit_bytes=VMEM_LIMIT_BYTES),
        name="dn_inproj",
    )(*x_args, gain.reshape(1, d), shift3, scale3, w_in[:, :4 * d].astype(BF16), w_in[:, 4 * d:].astype(BF16),
      conv_w, a_log.reshape(1, nd), dt_bias.reshape(1, nd))


def _dotb(a, b):
    return jnp.dot(a.astype(BF16), b.astype(BF16), preferred_element_type=F32)


def _dotb_nt(a, b):
    return lax.dot_general(a.astype(BF16), b.astype(BF16), (((1,), (1,)), ((), ())),
                           preferred_element_type=F32)


def _dotb_tn(a, b):
    return lax.dot_general(a.astype(BF16), b.astype(BF16), (((0,), (0,)), ((), ())),
                           preferred_element_type=F32)


def _unit_tri_inverses(mats):
    c = DN_CHUNK
    assert len(mats) % 2 == 0
    ii = lax.broadcasted_iota(jnp.int32, (c, 2 * c), 0)
    lane = lax.broadcasted_iota(jnp.int32, (c, 2 * c), 1)
    jj = lane & (c - 1)
    left = lane < c

    def blockdiag(p):
        pb = p.astype(BF16)
        zero = jnp.zeros_like(pb)
        return jnp.concatenate([jnp.where(left, pb, zero), jnp.where(left, zero, pb)], axis=0)

    def mul(p, q):
        return jnp.dot(p.astype(BF16), blockdiag(q), preferred_element_type=F32)

    pairs = [jnp.concatenate([mats[i], mats[i + 1]], axis=1) for i in range(0, len(mats), 2)]
    eye = (ii == jj).astype(F32)
    diag8 = (ii >> 3) == (jj >> 3)
    n = [-jnp.where(diag8, a, 0.0) for a in pairs]
    n2 = [mul(x, x) for x in n]
    m = [eye + x for x in n]
    m = [x + mul(x, y) for x, y in zip(m, n2)]
    n4 = [mul(x, x) for x in n2]
    m = [x + mul(x, y) for x, y in zip(m, n4)]
    sh = 3
    while (1 << sh) < c:
        off = ((ii >> (sh + 1)) == (jj >> (sh + 1))) & ((ii >> sh) != (jj >> sh))
        cm = [mul(jnp.where(off, a, 0.0), x) for a, x in zip(pairs, m)]
        m = [x - mul(x, y) for x, y in zip(m, cm)]
        sh += 1
    return [half for x in m for half in (x[:, :c], x[:, c:])]


def _dn_scan_kernel(qf_ref, kf_ref, vf_ref, gcf_ref, grf_ref, qb_ref, kb_ref, vb_ref, gcb_ref, grb_ref,
                    of_ref, ob_ref, s_ref):
    @pl.when(pl.program_id(2) == 0)
    def _():
        s_ref[...] = jnp.zeros_like(s_ref)

    c = DN_CHUNK
    ncb = N_CHUNKS_PER_BLOCK
    ii = lax.broadcasted_iota(jnp.int32, (c, c), 0)
    jj = lax.broadcasted_iota(jnp.int32, (c, c), 1)
    incl = (ii >= jj, ii <= jj)
    strict = (ii > jj, ii < jj)
    dirs = ((qf_ref, kf_ref, vf_ref, gcf_ref, grf_ref, of_ref),
            (qb_ref, kb_ref, vb_ref, gcb_ref, grb_ref, ob_ref))
    items = [(d, hh, ci) for d in range(2) for hh in range(DN_HB) for ci in range(ncb)]

    def rows(ci):
        return slice(ci * c, (ci + 1) * c)

    def cols(hh):
        return slice(hh * DN_DK, (hh + 1) * DN_DK)

    q = [dirs[d][0][rows(ci), cols(hh)] for d, hh, ci in items]
    k = [dirs[d][1][rows(ci), cols(hh)] for d, hh, ci in items]
    v = [dirs[d][2][rows(ci), cols(hh)] for d, hh, ci in items]
    nh = DN_HEADS
    gc = [dirs[d][3][rows(ci), d * nh + hh:d * nh + hh + 1] for d, hh, ci in items]
    gr = [dirs[d][4][d * nh + hh:d * nh + hh + 1, rows(ci)] for d, hh, ci in items]
    beta = [dirs[d][3][rows(ci), (2 + d) * nh + hh:(2 + d) * nh + hh + 1] for d, hh, ci in items]

    decay = [jnp.where(incl[it[0]], jnp.exp(jnp.where(incl[it[0]], x - y, 0.0)), 0.0)
             for it, x, y in zip(items, gc, gr)]
    kb = [x * y for x, y in zip(k, beta)]
    a = [_dotb_nt(x, y) * jnp.where(strict[it[0]], z, 0.0) for it, x, y, z in zip(items, kb, k, decay)]
    attn = [_dotb_nt(x, y) * z for x, y, z in zip(q, k, decay)]
    tinv = _unit_tri_inverses(a)
    eg = [jnp.exp(x) for x in gc]
    uw = [_dotb(t, jnp.concatenate([x * y, z * e], axis=-1))
          for t, x, y, z, e in zip(tinv, v, beta, kb, eg)]
    g_last = [x[0:1] if it[0] else x[c - 1:c] for it, x in zip(items, gc)]
    wq = [jnp.concatenate([x[:, DN_DV:], y * e], axis=0) for x, y, e in zip(uw, q, eg)]
    k_dec = [x * jnp.exp(y - z) for x, y, z in zip(k, g_last, gc)]
    s_dec = [jnp.exp(x) for x in g_last]

    chains = [(d, hh) for d in range(2) for hh in range(DN_HB)]
    state = [s_ref[d, hh] for d, hh in chains]
    for step in range(ncb):
        cur = [items.index((d, hh, ncb - 1 - step if d else step)) for d, hh in chains]
        ws = [_dotb(wq[n], s) for n, s in zip(cur, state)]
        v_new = [uw[n][:, :DN_DV] - x[:c] for n, x in zip(cur, ws)]
        o = [x[c:] + _dotb(attn[n], y) for n, x, y in zip(cur, ws, v_new)]
        state = [s * s_dec[n] + _dotb_tn(k_dec[n], y) for n, s, y in zip(cur, state, v_new)]
        for n, x in zip(cur, o):
            d, hh, ci = items[n]
            dirs[d][5][rows(ci), cols(hh)] = x.astype(BF16)
    for (d, hh), s in zip(chains, state):
        s_ref[d, hh] = s


def _dn_scan(q, k, v, gates, gates_t, n_lat_blocks, blocks_per_batch):
    t, d = q.shape
    bsz = n_lat_blocks // blocks_per_batch
    assert DN_HB == DN_HEADS
    ng = gates.shape[1]

    def blk_f(b, s):
        return jnp.where(s == 0, n_lat_blocks + b, b * blocks_per_batch + s - 1)

    def blk_b(b, s):
        return jnp.where(s == 0, n_lat_blocks + b, b * blocks_per_batch + blocks_per_batch - s)

    hw = DN_HB * DN_DK

    def specs(blk):
        return [pl.BlockSpec((DN_BLOCK, hw), lambda b, hg, s: (blk(b, s), hg))] * 3 + [
            pl.BlockSpec((DN_BLOCK, ng), lambda b, hg, s: (blk(b, s), 0)),
            pl.BlockSpec((ng, DN_BLOCK), lambda b, hg, s: (0, blk(b, s)))]

    return pl.pallas_call(
        _dn_scan_kernel,
        out_shape=[jax.ShapeDtypeStruct((t, d), BF16)] * 2,
        grid=(bsz, DN_HEADS // DN_HB, 1 + blocks_per_batch),
        in_specs=specs(blk_f) + specs(blk_b),
        out_specs=[pl.BlockSpec((DN_BLOCK, hw), lambda b, hg, s: (blk_f(b, s), hg)),
                   pl.BlockSpec((DN_BLOCK, hw), lambda b, hg, s: (blk_b(b, s), hg))],
        scratch_shapes=[pltpu.VMEM((2, DN_HB, DN_DK, DN_DV), F32)],
        compiler_params=pltpu.CompilerParams(
            dimension_semantics=("arbitrary", "arbitrary", "arbitrary"),
            vmem_limit_bytes=VMEM_LIMIT_BYTES),
        name="dn_scan",
    )(q, k, v, gates, gates_t, q, k, v, gates, gates_t)


def _dn_out_kernel(of_ref, ob_ref, z_ref, on_ref, w_ref, gate_ref, *refs, n_lat_tiles):
    x_refs, o_ref = refs[:-1], refs[-1]
    z = z_ref[...].astype(F32)
    zs = z * jax.nn.sigmoid(z)
    parts = []
    for h in range(DN_HEADS):
        cols = slice(h * DN_DV, (h + 1) * DN_DV)
        o = of_ref[:, cols].astype(F32) + ob_ref[:, cols].astype(F32)
        o = o * lax.rsqrt(jnp.mean(o * o, axis=-1, keepdims=True) + EPS)
        parts.append(o)
    y = jnp.concatenate(parts, axis=-1) * on_ref[...] * zs
    x = _stream_rows(x_refs, pl.program_id(0) >= n_lat_tiles)
    o_ref[...] = x + gate_ref[0] * jnp.dot(y.astype(BF16), w_ref[...], preferred_element_type=F32)


def _dn_out(o_f, o_b, z, out_norm, w_out, xs, gate3, n_lat_blocks, blocks_per_batch):
    tm = DN_OUT_TILE
    per_tile = tm // DN_BLOCK
    n_lat_tiles = n_lat_blocks // per_tile
    x_specs, t = _stream_specs(xs, n_lat_tiles, tm)
    x_args = xs if isinstance(xs, tuple) else (xs,)
    d = D_MODEL
    row = lambda i: (i, 0)
    fixed = lambda i: (0, 0)
    grp = lambda i: (_group_of_block(i * per_tile, n_lat_blocks, blocks_per_batch), 0, 0)
    return pl.pallas_call(
        functools.partial(_dn_out_kernel, n_lat_tiles=n_lat_tiles),
        out_shape=jax.ShapeDtypeStruct((t, d), F32),
        grid=(t // tm,),
        in_specs=[pl.BlockSpec((tm, d), row),
                  pl.BlockSpec((tm, d), row),
                  pl.BlockSpec((tm, d), row),
                  pl.BlockSpec((1, d), fixed),
                  pl.BlockSpec((d, d), fixed),
                  pl.BlockSpec((1, 1, d), grp)] + x_specs,
        out_specs=pl.BlockSpec((tm, d), row),
        compiler_params=pltpu.CompilerParams(
            dimension_semantics=("arbitrary",), vmem_limit_bytes=VMEM_LIMIT_BYTES),
        name="dn_out",
    )(o_f, o_b, z, jnp.tile(out_norm, DN_HEADS).reshape(1, d), w_out.astype(BF16), gate3, *x_args)


def _deltanet_layer(xs, gain, shift3, scale3, w_in, conv_w, a_log, dt_bias, out_norm, w_out, gate3,
                    n_lat_blocks, blocks_per_batch):
    q, k, v, z, gates, gates_t = _dn_inproj(xs, gain, shift3, scale3, w_in, conv_w, a_log, dt_bias,
                                            n_lat_blocks, blocks_per_batch)
    o_f, o_b = _dn_scan(q, k, v, gates, gates_t, n_lat_blocks, blocks_per_batch)
    return _dn_out(o_f, o_b, z, out_norm, w_out, xs, gate3, n_lat_blocks, blocks_per_batch)


HY_N2 = 256
HY_CB = 16
HY_TOK_TILE = 512
HY_FILT_TILE = 512


def _dft_constants(nr):
    n1, n2 = 2 * nr, HY_N2
    n = n1 * n2
    nk = -(-(nr + 1) // 8) * 8
    keep = np.arange(nk) <= nr
    k1 = np.where(keep, np.arange(nk), 0)
    f1 = np.exp(-2j * np.pi * np.outer(k1, np.arange(nr)) / n1) * keep[:, None]
    twice = np.where((k1 > 0) & (k1 < nr), 2.0, 1.0) * keep
    lhs_fwd = np.concatenate([f1.real, f1.imag], axis=0)
    lhs_inv = np.concatenate([f1.real.T * twice, f1.imag.T * twice], axis=1) / n
    tw = np.exp(-2j * np.pi * np.outer(k1, np.arange(n2)) / n)
    a2 = np.arange(n2)
    f2 = np.exp(-2j * np.pi * np.outer(a2, a2) / n2)
    w_fwd = np.block([[f2.real, f2.imag], [-f2.imag, f2.real]])
    w_inv = np.block([[f2.real, -f2.imag], [f2.imag, f2.real]])
    return (jnp.asarray(lhs_fwd, BF16), jnp.asarray(lhs_inv, BF16), jnp.asarray(tw.real, F32),
            jnp.asarray(tw.imag, F32), jnp.asarray(w_fwd, BF16), jnp.asarray(w_inv, BF16))


def _hy_dft(x3, lhs_fwd, twr, twi, w_fwd):
    n1 = twr.shape[0]
    a = [jnp.dot(lhs_fwd, x3[c].astype(BF16), preferred_element_type=F32) for c in range(x3.shape[0])]
    br = jnp.concatenate([t[:n1] * twr - t[n1:] * twi for t in a], axis=0)
    bi = jnp.concatenate([t[:n1] * twi + t[n1:] * twr for t in a], axis=0)
    b = jnp.concatenate([br, bi], axis=1)
    return jnp.dot(b.astype(BF16), w_fwd, preferred_element_type=F32)


def _hy_idft(p, cb, lhs_inv, twr, twi, w_inv):
    n1, n2 = twr.shape
    c = jnp.dot(p.astype(BF16), w_inv, preferred_element_type=F32)
    out = []
    for ch in range(cb):
        cr = c[ch * n1:(ch + 1) * n1, :n2]
        ci = c[ch * n1:(ch + 1) * n1, n2:]
        d = jnp.concatenate([cr * twr + ci * twi, ci * twr - cr * twi], axis=0)
        out.append(jnp.dot(lhs_inv, d.astype(BF16), preferred_element_type=F32))
    return out


def _hy_spectrum_kernel(hf_ref, hb_ref, lf_ref, twr_ref, twi_ref, wf_ref, o_ref):
    cb, nr, n2 = hf_ref.shape
    first = ((lax.broadcasted_iota(jnp.int32, (nr, n2), 0) == 0)
             & (lax.broadcasted_iota(jnp.int32, (nr, n2), 1) == 0))
    hb = jnp.where(first, 0.0, hb_ref[...])
    consts = (lf_ref[...], twr_ref[...], twi_ref[...], wf_ref[...])
    xf = _hy_dft(hf_ref[...], *consts)
    xb = _hy_dft(hb, *consts)
    o_ref[...] = jnp.concatenate([xf[:, :n2] + xb[:, :n2], xf[:, n2:] - xb[:, n2:]],
                                 axis=1).reshape(o_ref.shape)


def _hy_spectrum(filt, consts):
    d = D_MODEL
    l = filt.shape[1]
    nr = l // HY_N2
    lhs_fwd, _, twr, twi, w_fwd, _ = consts
    n1 = twr.shape[0]
    cpo = d // HY_CB
    fixed2 = lambda o, c: (0, 0)
    return pl.pallas_call(
        _hy_spectrum_kernel,
        out_shape=jax.ShapeDtypeStruct((HY_ORDER * d, n1, 2 * HY_N2), F32),
        grid=(HY_ORDER, cpo),
        in_specs=[pl.BlockSpec((HY_CB, nr, HY_N2), lambda o, c: (2 * o * cpo + c, 0, 0)),
                  pl.BlockSpec((HY_CB, nr, HY_N2), lambda o, c: ((2 * o + 1) * cpo + c, 0, 0)),
                  pl.BlockSpec(lhs_fwd.shape, fixed2),
                  pl.BlockSpec(twr.shape, fixed2),
                  pl.BlockSpec(twi.shape, fixed2),
                  pl.BlockSpec(w_fwd.shape, fixed2)],
        out_specs=pl.BlockSpec((HY_CB, n1, 2 * HY_N2), lambda o, c: (o * cpo + c, 0, 0)),
        compiler_params=pltpu.CompilerParams(
            dimension_semantics=("arbitrary", "arbitrary"), vmem_limit_bytes=VMEM_LIMIT_BYTES),
        name="hy_spectrum",
    )(filt.reshape(-1, nr, HY_N2), filt.reshape(-1, nr, HY_N2), lhs_fwd, twr, twi, w_fwd)


def _hy_conv_kernel(z_ref, g_ref, k_ref, bias_ref, lf_ref, li_ref, twr_ref, twi_ref, wf_ref, wi_ref,
                    o_ref):
    cb, nr, n2 = z_ref.shape
    twr, twi = twr_ref[...], twi_ref[...]
    z = z_ref[...]
    x = _hy_dft(z, lf_ref[...], twr, twi, wf_ref[...])
    kk = k_ref[...].reshape(x.shape)
    xr, xi, kr, ki = x[:, :n2], x[:, n2:], kk[:, :n2], kk[:, n2:]
    p = jnp.concatenate([xr * kr - xi * ki, xr * ki + xi * kr], axis=1)
    conv = _hy_idft(p, cb, li_ref[...], twr, twi, wi_ref[...])
    for c in range(cb):
        o_ref[c] = g_ref[c] * (conv[c] + bias_ref[c] * z[c])


def _hy_conv(z, z_part, gate, gate_part, khat, order, bias, consts, bsz):
    d = D_MODEL
    l = z.shape[1] // bsz
    nr = l // HY_N2
    cpo = d // HY_CB
    lhs_fwd, lhs_inv, twr, twi, w_fwd, w_inv = consts
    n1 = twr.shape[0]
    fixed2 = lambda c, b: (0, 0)
    out = pl.pallas_call(
        _hy_conv_kernel,
        out_shape=jax.ShapeDtypeStruct((d, bsz * nr, HY_N2), F32),
        grid=(cpo, bsz),
        in_specs=[pl.BlockSpec((HY_CB, nr, HY_N2), lambda c, b: (z_part * cpo + c, b, 0)),
                  pl.BlockSpec((HY_CB, nr, HY_N2), lambda c, b: (gate_part * cpo + c, b, 0)),
                  pl.BlockSpec((HY_CB, n1, 2 * HY_N2), lambda c, b: (order * cpo + c, 0, 0)),
                  pl.BlockSpec((HY_CB, 1, 1), lambda c, b: (c, 0, 0)),
                  pl.BlockSpec(lhs_fwd.shape, fixed2),
                  pl.BlockSpec(lhs_inv.shape, fixed2),
                  pl.BlockSpec(twr.shape, fixed2),
                  pl.BlockSpec(twi.shape, fixed2),
                  pl.BlockSpec(w_fwd.shape, fixed2),
                  pl.BlockSpec(w_inv.shape, fixed2)],
        out_specs=pl.BlockSpec((HY_CB, nr, HY_N2), lambda c, b: (c, b, 0)),
        compiler_params=pltpu.CompilerParams(
            dimension_semantics=("arbitrary", "arbitrary"), vmem_limit_bytes=VMEM_LIMIT_BYTES),
        name="hy_conv",
    )(z.reshape(-1, bsz * nr, HY_N2), gate.reshape(-1, bsz * nr, HY_N2), khat,
      bias.reshape(d, 1, 1), lhs_fwd, lhs_inv, twr, twi, w_fwd, w_inv)
    return out.reshape(d, bsz * l)


def _hy_ctx_kernel(z_ref, g_ref, hf_ref, hb_ref, bias_ref, wf_ref, wi_ref, o_ref, *, bsz):
    l = hf_ref.shape[1]
    wf, wi = wf_ref[...], wi_ref[...]
    hb = jnp.where(lax.broadcasted_iota(jnp.int32, (1, l), 1) == 0, 0.0, hb_ref[...])
    kf = jnp.dot(hf_ref[...].astype(BF16), wf, preferred_element_type=F32)
    kb = jnp.dot(hb.astype(BF16), wf, preferred_element_type=F32)
    n = 2 * l
    kr, ki = kf[:, :n] + kb[:, :n], kf[:, n:] - kb[:, n:]
    bias = bias_ref[...]
    for b in range(bsz):
        z = z_ref[:, b * l:(b + 1) * l]
        x = jnp.dot(z.astype(BF16), wf, preferred_element_type=F32)
        xr, xi = x[:, :n], x[:, n:]
        p = jnp.concatenate([xr * kr - xi * ki, xr * ki + xi * kr], axis=1)
        conv = jnp.dot(p.astype(BF16), wi, preferred_element_type=F32)
        o_ref[:, b * l:(b + 1) * l] = g_ref[:, b * l:(b + 1) * l] * (conv + bias * z)


def _hy_ctx(z, z_part, gate, gate_part, filt, order, bias, bsz):
    d = D_MODEL
    l = filt.shape[1]
    n = 2 * l
    ang = 2 * np.pi * np.outer(np.arange(l), np.arange(n)) / n
    w_fwd = jnp.asarray(np.concatenate([np.cos(ang), -np.sin(ang)], axis=1), BF16)
    w_inv = jnp.asarray(np.concatenate([np.cos(ang.T), -np.sin(ang.T)], axis=0) / n, BF16)
    cb = 256
    nblk = d // cb
    fixed = lambda c: (0, 0)
    return pl.pallas_call(
        functools.partial(_hy_ctx_kernel, bsz=bsz),
        out_shape=jax.ShapeDtypeStruct((d, bsz * l), F32),
        grid=(nblk,),
        in_specs=[pl.BlockSpec((cb, bsz * l), lambda c: (z_part * nblk + c, 0)),
                  pl.BlockSpec((cb, bsz * l), lambda c: (gate_part * nblk + c, 0)),
                  pl.BlockSpec((cb, l), lambda c: (2 * order * nblk + c, 0)),
                  pl.BlockSpec((cb, l), lambda c: ((2 * order + 1) * nblk + c, 0)),
                  pl.BlockSpec((cb, 1), lambda c: (c, 0)),
                  pl.BlockSpec(w_fwd.shape, fixed),
                  pl.BlockSpec(w_inv.shape, fixed)],
        out_specs=pl.BlockSpec((cb, bsz * l), lambda c: (c, 0)),
        compiler_params=pltpu.CompilerParams(
            dimension_semantics=("arbitrary",), vmem_limit_bytes=VMEM_LIMIT_BYTES),
        name="hy_ctx_conv",
    )(z, gate, filt, filt, bias.reshape(d, 1), w_fwd, w_inv)


def _hy_filter_kernel(band_ref, w1t_ref, w1c_ref, w1s_ref, b1_ref, fr_ref, w2_ref, b2_ref, w3_ref,
                      delta_ref, o_ref, *, l):
    tl = o_ref.shape[1]
    d = D_MODEL
    hp = lax.Precision.HIGHEST
    pos = (lax.broadcasted_iota(jnp.int32, (1, tl), 1) + pl.program_id(0) * tl).astype(F32)
    t = pos / max(l - 1, 1)
    ang = ((2 * math.pi / l) * pos) * band_ref[...]
    fr = fr_ref[...]
    pre = (w1t_ref[...] * t + jnp.dot(w1c_ref[...], jnp.cos(ang), precision=hp)
           + jnp.dot(w1s_ref[...], -jnp.sin(ang), precision=hp) + b1_ref[...])
    hdn = jnp.sin(fr * pre)
    hdn = jnp.sin(fr * (jnp.dot(w2_ref[...], hdn, precision=hp) + b2_ref[...]))
    window = jnp.exp(-t * delta_ref[...])
    for part in range(2 * HY_ORDER):
        rows = slice(part * d, (part + 1) * d)
        o_ref[rows, :] = jnp.dot(w3_ref[rows, :], hdn.astype(BF16), preferred_element_type=F32) * window


def _hy_filter(l, w1, b1, freq, w2, b2, w3):
    d = D_MODEL
    nb = HY_BANDS
    tl = min(HY_FILT_TILE, l)
    col = lambda v: v.reshape(-1, 1)
    bands = jnp.linspace(1e-4, nb - 1, nb, dtype=F32)
    deltas = jnp.abs(jnp.linspace(math.log(HY_TARGET) / HY_SLOW, math.log(HY_TARGET) / HY_FAST, d, dtype=F32))
    w1t = w1.T
    args = (col(bands), w1t[:, 0:1], w1t[:, 1:1 + nb], w1t[:, 1 + nb:], col(b1), col(freq), w2.T, col(b2),
            w3.T.astype(BF16), col(deltas))
    return pl.pallas_call(
        functools.partial(_hy_filter_kernel, l=l),
        out_shape=jax.ShapeDtypeStruct((2 * HY_ORDER * d, l), F32),
        grid=(l // tl,),
        in_specs=[pl.BlockSpec(a.shape, lambda j: (0, 0)) for a in args],
        out_specs=pl.BlockSpec((2 * HY_ORDER * d, tl), lambda j: (0, j)),
        compiler_params=pltpu.CompilerParams(
            dimension_semantics=("arbitrary",), vmem_limit_bytes=VMEM_LIMIT_BYTES),
        name="hy_filter",
    )(*args)


def _hy_inproj_kernel(x_ref, gain_ref, shift_ref, scale_ref, wt_ref, cw_ref, o_ref, *, seg):
    nch = wt_ref.shape[0]
    tm = x_ref.shape[0]
    hb = _modulated(x_ref[...], gain_ref[...], shift_ref[0], scale_ref[0]).astype(BF16)
    pos = lax.broadcasted_iota(jnp.int32, (1, tm), 1) & (seg - 1)
    not_first = pos != 0
    not_last = pos != seg - 1
    sub = 512
    for j in range(nch // sub):
        rows = slice(j * sub, (j + 1) * sub)
        p = _dotb_nt(wt_ref[rows, :], hb)
        cw = cw_ref[rows, :]
        prev = jnp.where(not_first, pltpu.roll(p, 1, axis=1), 0.0)
        nxt = jnp.where(not_last, pltpu.roll(p, tm - 1, axis=1), 0.0)
        o_ref[rows, :] = cw[:, 0:1] * prev + cw[:, 1:2] * p + cw[:, 2:3] * nxt


def _hy_inproj(xs, gain, shift3, scale3, w_in, conv_w, first_tile, n_tiles, seg, n_lat_blocks,
               blocks_per_batch):
    k = xs.shape[1]
    nch = w_in.shape[1]
    tm = HY_TOK_TILE
    per_tile = tm // DN_BLOCK
    grp = lambda i: (_group_of_block((first_tile + i) * per_tile, n_lat_blocks, blocks_per_batch), 0, 0)
    return pl.pallas_call(
        functools.partial(_hy_inproj_kernel, seg=seg),
        out_shape=jax.ShapeDtypeStruct((nch, n_tiles * tm), F32),
        grid=(n_tiles,),
        in_specs=[pl.BlockSpec((tm, k), lambda i: (first_tile + i, 0)),
                  pl.BlockSpec((1, k), lambda i: (0, 0)),
                  pl.BlockSpec((1, 1, k), grp),
                  pl.BlockSpec((1, 1, k), grp),
                  pl.BlockSpec((nch, k), lambda i: (0, 0)),
                  pl.BlockSpec((nch, 3), lambda i: (0, 0))],
        out_specs=pl.BlockSpec((nch, tm), lambda i: (0, i)),
        compiler_params=pltpu.CompilerParams(
            dimension_semantics=("arbitrary",), vmem_limit_bytes=VMEM_LIMIT_BYTES),
        name="hy_inproj",
    )(xs, gain.reshape(1, k), shift3, scale3, w_in.T.astype(BF16), conv_w.T)


def _hy_out_kernel(zl_ref, zc_ref, w_ref, x_ref, gate_ref, o_ref, *, n_lat_tiles):
    z = jnp.where(pl.program_id(0) >= n_lat_tiles, zc_ref[...], zl_ref[...])
    o_ref[...] = x_ref[...] + gate_ref[0] * _dotb_tn(z, w_ref[...])


def _hy_out(z_lat, z_ctx, w_out, xs, gate3, blocks_per_batch):
    t, d = xs.shape
    tm = HY_TOK_TILE
    n_lat_tiles = z_lat.shape[1] // tm
    per_tile = tm // DN_BLOCK
    grp = lambda i: (_group_of_block(i * per_tile, n_lat_tiles * per_tile, blocks_per_batch), 0, 0)
    return pl.pallas_call(
        functools.partial(_hy_out_kernel, n_lat_tiles=n_lat_tiles),
        out_shape=jax.ShapeDtypeStruct((t, d), F32),
        grid=(t // tm,),
        in_specs=[pl.BlockSpec((d, tm), lambda i: (0, jnp.minimum(i, n_lat_tiles - 1))),
                  pl.BlockSpec((d, tm), lambda i: (0, 0)),
                  pl.BlockSpec((d, d), lambda i: (0, 0)),
                  pl.BlockSpec((tm, d), lambda i: (i, 0)),
                  pl.BlockSpec((1, 1, d), grp)],
        out_specs=pl.BlockSpec((tm, d), lambda i: (i, 0)),
        compiler_params=pltpu.CompilerParams(
            dimension_semantics=("arbitrary",), vmem_limit_bytes=VMEM_LIMIT_BYTES),
        name="hy_out",
    )(z_lat, z_ctx, w_out.astype(BF16), xs, gate3)


def _hyena_layer(xs, gain, shift3, scale3, w_in, conv_w, f_w1, f_b1, f_freq, f_w2, f_b2, f_w3, bias, w_out,
                 gate3, bsz, seq):
    n_lat_tiles = bsz * seq // HY_TOK_TILE
    assert bsz * CTX_LEN == HY_TOK_TILE
    margs = (xs, gain, shift3, scale3, w_in, conv_w)
    blocks = (bsz * seq // DN_BLOCK, seq // DN_BLOCK)
    p_lat = _hy_inproj(*margs, 0, n_lat_tiles, GRID_W, *blocks)
    p_ctx = _hy_inproj(*margs, n_lat_tiles, 1, CTX_LEN, *blocks)
    fargs = (f_w1, f_b1, f_freq, f_w2, f_b2, f_w3)
    consts = _dft_constants(seq // HY_N2)
    khat = _hy_spectrum(_hy_filter(seq, *fargs), consts)
    filt_ctx = _hy_filter(CTX_LEN, *fargs)
    z_lat, z_ctx = p_lat, p_ctx
    for n in range(HY_ORDER):
        z_lat = _hy_conv(z_lat, 0, p_lat, n + 1, khat, n, bias[n], consts, bsz)
        z_ctx = _hy_ctx(z_ctx, 0, p_ctx, n + 1, filt_ctx, n, bias[n], bsz)
    return _hy_out(z_lat, z_ctx, w_out, xs, gate3, seq // DN_BLOCK)


MOE_TILE = 512
PACK = 2


def _experts_kernel(be_ref, na_ref, x_ref, wg_ref, wu_ref, wd_ref, o_ref, wgb_ref, wub_ref, wdb_ref):
    i = pl.program_id(0)
    prev = be_ref[jnp.maximum(i - 1, 0)]

    @pl.when((i == 0) | (be_ref[i] != prev))
    def _():
        wgb_ref[...] = wg_ref[0, 0].astype(BF16)
        wub_ref[...] = wu_ref[0, 0].astype(BF16)
        wdb_ref[...] = wd_ref[0, 0].astype(BF16)

    @pl.when(i < na_ref[0])
    def _():
        packed = x_ref[...]
        lo = pltpu.bitcast(packed << 16, F32)
        hi = pltpu.bitcast(packed & jnp.uint32(0xFFFF0000), F32)
        xb = jnp.concatenate([lo, hi], axis=-1).astype(BF16)
        g = jnp.dot(xb, wgb_ref[...], preferred_element_type=F32)
        u = jnp.dot(xb, wub_ref[...], preferred_element_type=F32)
        hid = (g * jax.nn.sigmoid(g)) * u
        y = jnp.dot(hid.astype(BF16), wdb_ref[...], preferred_element_type=F32)
        ybits = pltpu.bitcast(y.astype(BF16).astype(F32), jnp.uint32)
        half = y.shape[1] // PACK
        o_ref[...] = (ybits[:, :half] >> 16) | (ybits[:, half:] & jnp.uint32(0xFFFF0000))

    @pl.when(i >= na_ref[0])
    def _():
        o_ref[...] = jnp.zeros_like(o_ref)


def _experts(xs_sorted, block_expert, n_active, w_gate, w_up, w_down, layer):
    n_slots, wd = xs_sorted.shape
    d = wd * PACK
    n_blocks = n_slots // MOE_BLOCK
    blk = lambda i, be, na: (jnp.minimum(i, na[0] - 1), 0)
    wsel = lambda i, be, na: (layer, be[jnp.minimum(i, na[0] - 1)], 0, 0)
    grid_spec = pltpu.PrefetchScalarGridSpec(
        num_scalar_prefetch=2,
        grid=(n_blocks,),
        in_specs=[pl.BlockSpec((MOE_BLOCK, wd), blk),
                  pl.BlockSpec((1, 1, d, D_EXPERT), wsel),
                  pl.BlockSpec((1, 1, d, D_EXPERT), wsel),
                  pl.BlockSpec((1, 1, D_EXPERT, d), wsel)],
        out_specs=pl.BlockSpec((MOE_BLOCK, wd), lambda i, be, na: (i, 0)),
        scratch_shapes=[pltpu.VMEM((d, D_EXPERT), BF16), pltpu.VMEM((d, D_EXPERT), BF16),
                        pltpu.VMEM((D_EXPERT, d), BF16)],
    )
    return pl.pallas_call(
        _experts_kernel,
        out_shape=jax.ShapeDtypeStruct((n_slots, wd), jnp.uint32),
        grid_spec=grid_spec,
        compiler_params=pltpu.CompilerParams(
            dimension_semantics=("arbitrary",), vmem_limit_bytes=VMEM_LIMIT_BYTES),
        name="moe_experts",
    )(block_expert, n_active, xs_sorted, w_gate, w_up, w_down)


MOE_CHUNK = 8
MOE_LB = 1280
assert MOE_LB >= MOE_TILE * TOP_K + N_EXPERTS * (MOE_CHUNK - 1) and MOE_LB % 128 == 0
N_CHUNKS = MOE_LB // MOE_CHUNK
TAB_W = 256
assert TAB_W >= N_CHUNKS


def _moe_route_kernel(x_ref, gain_ref, shift_ref, scale_ref, wr_ref, rb_ref, tri_ref, lt_ref,
                      f_ref, pos_ref, w_ref, cnt_ref):
    tm, d = x_ref.shape
    ne, epg, ng = N_EXPERTS, EXPERTS_PER_GROUP, N_GROUPS
    f = _modulated(x_ref[...], gain_ref[...], shift_ref[0], scale_ref[0])
    f_ref[...] = f.astype(BF16)

    logits = lax.dot_general(wr_ref[...], f, (((1,), (1,)), ((), ())),
                             precision=lax.Precision.HIGHEST, preferred_element_type=F32)
    scores = jax.nn.sigmoid(logits)
    biased = scores + rb_ref[...]
    s = [scores[j * ng:(j + 1) * ng] for j in range(epg)]
    c = [biased[j * ng:(j + 1) * ng] for j in range(epg)]
    hi01, lo01 = jnp.maximum(c[0], c[1]), jnp.minimum(c[0], c[1])
    hi23, lo23 = jnp.maximum(c[2], c[3]), jnp.minimum(c[2], c[3])
    gscore = jnp.maximum(hi01, hi23) + jnp.maximum(jnp.minimum(hi01, hi23), jnp.maximum(lo01, lo23))
    gi = lax.broadcasted_iota(jnp.int32, (ng, tm), 0)
    gmax = jnp.max(gscore, axis=0, keepdims=True)
    grp = jnp.min(jnp.where(gscore == gmax, gi, ng), axis=0, keepdims=True)
    sel = gi == grp
    cv = [jnp.sum(jnp.where(sel, t, 0.0), axis=0, keepdims=True) for t in c]
    sv = [jnp.sum(jnp.where(sel, t, 0.0), axis=0, keepdims=True) for t in s]

    def pick(excluded):
        best = jnp.full((1, tm), -jnp.inf, F32)
        idx = jnp.zeros((1, tm), jnp.int32)
        val = jnp.zeros((1, tm), F32)
        for j in range(epg):
            cand = cv[j] if excluded is None else jnp.where(excluded == j, -jnp.inf, cv[j])
            take = cand > best
            best = jnp.where(take, cand, best)
            idx = jnp.where(take, j, idx)
            val = jnp.where(take, sv[j], val)
        return idx, val

    i1, v1 = pick(None)
    i2, v2 = pick(i1)
    wsum = v1 + v2
    w_ref[0:1, :] = v1 / wsum
    w_ref[1:2, :] = v2 / wsum

    ei = lax.broadcasted_iota(jnp.int32, (ne, tm), 0)
    oh1 = ei == grp * epg + i1
    oh2 = ei == grp * epg + i2
    tri = tri_ref[...]
    pre1 = jnp.dot(oh1.astype(BF16), tri, preferred_element_type=F32)
    pre2 = jnp.dot(oh2.astype(BF16), tri, preferred_element_type=F32)
    tot1 = pre1[:, tm - 1:tm]
    tot = tot1 + pre2[:, tm - 1:tm]
    seg = jnp.floor((tot + (MOE_CHUNK - 1)) * (1.0 / MOE_CHUNK)) * MOE_CHUNK
    off = jnp.dot(lt_ref[...], jnp.broadcast_to(seg, (ne, 128)).astype(BF16),
                  preferred_element_type=F32)[:, 0:1]
    p1 = jnp.sum(jnp.where(oh1, off + pre1 - 1.0, 0.0), axis=0, keepdims=True)
    p2 = jnp.sum(jnp.where(oh2, off + tot1 + pre2 - 1.0, 0.0), axis=0, keepdims=True)
    pos_ref[0:1, :] = p1.astype(jnp.int32)
    pos_ref[1:2, :] = p2.astype(jnp.int32)
    cnt_ref[0] = jnp.broadcast_to(tot, (ne, 128))


def _moe_route(xs, gain, shift3, scale3, w_router, router_bias, n_lat_tiles, blocks_per_batch):
    t, d = xs.shape
    tm = MOE_TILE
    ne = N_EXPERTS
    per_tile = tm // DN_BLOCK
    row = lambda i: (i, 0)
    col = lambda i: (0, i)
    fixed = lambda i: (0, 0)
    grp = lambda i: (_group_of_block(i * per_tile, n_lat_tiles * per_tile, blocks_per_batch), 0, 0)
    tri = jnp.asarray(np.triu(np.ones((tm, tm), np.float32)), BF16)
    lt = jnp.asarray(np.tril(np.ones((ne, ne), np.float32), -1), BF16)
    perm = np.arange(ne).reshape(N_GROUPS, EXPERTS_PER_GROUP).T.reshape(-1)
    return pl.pallas_call(
        _moe_route_kernel,
        out_shape=[jax.ShapeDtypeStruct((t, d), BF16),
                   jax.ShapeDtypeStruct((TOP_K, t), jnp.int32),
                   jax.ShapeDtypeStruct((TOP_K, t), F32),
                   jax.ShapeDtypeStruct((t // tm, ne, 128), F32)],
        grid=(t // tm,),
        in_specs=[pl.BlockSpec((tm, d), row),
                  pl.BlockSpec((1, d), fixed),
                  pl.BlockSpec((1, 1, d), grp),
                  pl.BlockSpec((1, 1, d), grp),
                  pl.BlockSpec((ne, d), fixed),
                  pl.BlockSpec((ne, 1), fixed),
                  pl.BlockSpec((tm, tm), fixed),
                  pl.BlockSpec((ne, ne), fixed)],
        out_specs=[pl.BlockSpec((tm, d), row),
                   pl.BlockSpec((TOP_K, tm), col),
                   pl.BlockSpec((TOP_K, tm), col),
                   pl.BlockSpec((1, ne, 128), lambda i: (i, 0, 0))],
        compiler_params=pltpu.CompilerParams(
            dimension_semantics=("arbitrary",), vmem_limit_bytes=VMEM_LIMIT_BYTES),
        name="moe_route",
    )(xs, gain.reshape(1, d), shift3, scale3, w_router.T[perm], router_bias[perm].reshape(ne, 1), tri, lt)


def _chunk_row(j):
    return j * MOE_CHUNK if isinstance(j, int) else pl.multiple_of(j * MOE_CHUNK, MOE_CHUNK)


def _chunk_issue(tab_smem, tab_row, make_copy):
    def issue_pair(jj, n):
        for priority in range(2):
            j = jj * 2 + priority
            dst = tab_smem[tab_row, j]

            @pl.when(dst >= 0)
            def _():
                make_copy(j, pl.multiple_of(dst, MOE_CHUNK)).start(priority=priority)

            n = n + (dst >= 0).astype(jnp.int32)
        return n

    return lax.fori_loop(0, N_CHUNKS // 2, issue_pair, jnp.int32(0), unroll=4)


def _chunk_drain(n, make_copy):
    def drain(j, carry):
        make_copy(0, 0).wait()
        return carry

    lax.fori_loop(0, n, drain, 0)


def _moe_dispatch_kernel(tab_ref, pos_ref, f_ref, xs_in_ref, xs_ref, cnt_smem, loc_ref, sem, *, n_steps):
    del xs_in_ref
    tm, d = f_ref.shape
    i = pl.program_id(0)
    slot = i % 2
    r = lax.broadcasted_iota(jnp.int32, (tm, MOE_LB), 1)
    p = pos_ref[...]
    onehot = ((p[:, 0:1] == r) | (p[:, 1:2] == r)).astype(BF16)
    loc = _dotb_tn(onehot, f_ref[...])
    bits = pltpu.bitcast(loc, jnp.uint32)
    half = d // PACK
    loc_ref[slot] = (bits[:, :half] >> 16) | (bits[:, half:] & jnp.uint32(0xFFFF0000))

    def copy_from(buf):
        def make_copy(j, dst):
            return pltpu.make_async_copy(loc_ref.at[buf, pl.ds(_chunk_row(j), MOE_CHUNK)],
                                         xs_ref.at[pl.ds(dst, MOE_CHUNK)], sem.at[buf])
        return make_copy

    n = _chunk_issue(tab_ref, i, copy_from(slot))
    cnt_smem[slot] = n

    @pl.when(i > 0)
    def _():
        _chunk_drain(cnt_smem[1 - slot], copy_from(1 - slot))

    @pl.when(i == n_steps - 1)
    def _():
        _chunk_drain(n, copy_from(slot))


def _moe_dispatch(f, pos_cols, table, slots):
    t, d = f.shape
    tm = MOE_TILE
    n_slots, wd = slots.shape
    grid_spec = pltpu.PrefetchScalarGridSpec(
        num_scalar_prefetch=1,
        grid=(t // tm,),
        in_specs=[pl.BlockSpec((tm, TOP_K), lambda i, tab: (i, 0)),
                  pl.BlockSpec((tm, d), lambda i, tab: (i, 0)),
                  pl.BlockSpec(memory_space=pl.ANY)],
        out_specs=pl.BlockSpec(memory_space=pl.ANY),
        scratch_shapes=[pltpu.SMEM((2,), jnp.int32),
                        pltpu.VMEM((2, MOE_LB, wd), jnp.uint32),
                        pltpu.SemaphoreType.DMA((2,))],
    )
    return pl.pallas_call(
        functools.partial(_moe_dispatch_kernel, n_steps=t // tm),
        out_shape=jax.ShapeDtypeStruct((n_slots, wd), jnp.uint32),
        grid_spec=grid_spec,
        input_output_aliases={3: 0},
        compiler_params=pltpu.CompilerParams(
            dimension_semantics=("arbitrary",), vmem_limit_bytes=VMEM_LIMIT_BYTES),
        name="moe_dispatch",
    )(table, pos_cols, f, slots)


def _moe_combine_kernel(tab_ref, pos_ref, w_ref, y_ref, x_ref, gate_ref, fin_ref, o_ref, cnt_smem, yloc_ref,
                        sem, *, final_norm, n_steps):
    tm = x_ref.shape[0]
    i = pl.program_id(0)
    slot = i % 2

    def fetch(tile, buf):
        cnt_smem[buf] = _chunk_issue(tab_ref, tile, copy_into(buf))

    def copy_into(buf):
        def make_copy(j, src):
            return pltpu.make_async_copy(y_ref.at[pl.ds(src, MOE_CHUNK)],
                                         yloc_ref.at[buf, pl.ds(_chunk_row(j), MOE_CHUNK)], sem.at[buf])
        return make_copy

    @pl.when(i == 0)
    def _():
        yloc_ref[...] = jnp.zeros_like(yloc_ref)
        fetch(0, 0)

    @pl.when(i + 1 < n_steps)
    def _():
        fetch(i + 1, 1 - slot)

    _chunk_drain(cnt_smem[slot], copy_into(slot))
    r = lax.broadcasted_iota(jnp.int32, (tm, MOE_LB), 1)
    p = pos_ref[...]
    w = w_ref[...]
    wmat = jnp.where(p[:, 0:1] == r, w[:, 0:1], 0.0) + jnp.where(p[:, 1:2] == r, w[:, 1:2], 0.0)
    packed = yloc_ref[slot]
    y_lo = pltpu.bitcast(packed << 16, F32)
    y_hi = pltpu.bitcast(packed & jnp.uint32(0xFFFF0000), F32)
    moe = jnp.concatenate([_dotb(wmat, y_lo), _dotb(wmat, y_hi)], axis=-1)
    out = x_ref[...] + gate_ref[0] * moe
    if final_norm:
        out = out * lax.rsqrt(jnp.mean(out * out, axis=-1, keepdims=True) + EPS) * fin_ref[...]
    o_ref[...] = out


def _moe_combine(ys, pos_cols, weight_cols, table, xs, gate3, n_lat_tiles, blocks_per_batch, final_gain=None):
    t, d = xs.shape
    tm = MOE_TILE
    per_tile = tm // DN_BLOCK
    n_tiles = t // tm if final_gain is None else n_lat_tiles
    fin = jnp.ones((1, d), F32) if final_gain is None else final_gain.reshape(1, d)
    row = lambda i, tab: (i, 0)
    grp = lambda i, tab: (_group_of_block(i * per_tile, n_lat_tiles * per_tile, blocks_per_batch), 0, 0)
    grid_spec = pltpu.PrefetchScalarGridSpec(
        num_scalar_prefetch=1,
        grid=(n_tiles,),
        in_specs=[pl.BlockSpec((tm, TOP_K), row),
                  pl.BlockSpec((tm, TOP_K), row),
                  pl.BlockSpec(memory_space=pl.ANY),
                  pl.BlockSpec((tm, d), row),
                  pl.BlockSpec((1, 1, d), grp),
                  pl.BlockSpec((1, d), lambda i, tab: (0, 0))],
        out_specs=pl.BlockSpec((tm, d), row),
        scratch_shapes=[pltpu.SMEM((2,), jnp.int32),
                        pltpu.VMEM((2, MOE_LB, d // PACK), jnp.uint32),
                        pltpu.SemaphoreType.DMA((2,))],
    )
    return pl.pallas_call(
        functools.partial(_moe_combine_kernel, final_norm=final_gain is not None, n_steps=n_tiles),
        out_shape=jax.ShapeDtypeStruct((n_tiles * tm, d), F32),
        grid_spec=grid_spec,
        compiler_params=pltpu.CompilerParams(
            dimension_semantics=("arbitrary",), vmem_limit_bytes=VMEM_LIMIT_BYTES),
        name="moe_combine",
    )(table, pos_cols, weight_cols, ys, xs, gate3, fin)


def _moe_layer(xs, gain, shift3, scale3, gate3, w_router, router_bias, w_gate, w_up, w_down, layer,
               n_lat_tiles, blocks_per_batch, final_gain=None, slots=None):
    t = xs.shape[0]
    n_tiles = t // MOE_TILE
    f, pos, weight, cnt = _moe_route(xs, gain, shift3, scale3, w_router, router_bias,
                                     n_lat_tiles, blocks_per_batch)
    seg = (cnt[:, :, 0].astype(jnp.int32) + MOE_CHUNK - 1) // MOE_CHUNK * MOE_CHUNK
    loc_end = jnp.cumsum(seg, axis=1)
    loc_off = loc_end - seg
    padded = (jnp.sum(seg, axis=0) + MOE_BLOCK - 1) // MOE_BLOCK * MOE_BLOCK
    pend = jnp.cumsum(padded)
    seg_start = (pend - padded)[None, :] + jnp.cumsum(seg, axis=0) - seg
    n_blocks = -(-(t * TOP_K + n_tiles * N_EXPERTS * (MOE_CHUNK - 1)) // MOE_BLOCK) + N_EXPERTS
    block_start = jnp.arange(n_blocks, dtype=jnp.int32) * MOE_BLOCK
    block_expert = jnp.minimum(jnp.sum(pend[None, :] <= block_start[:, None], axis=1),
                               N_EXPERTS - 1).astype(jnp.int32)
    n_active = (pend[-1:] // MOE_BLOCK).astype(jnp.int32)
    row0 = jnp.arange(N_CHUNKS, dtype=jnp.int32) * MOE_CHUNK
    e_of = jnp.sum(loc_end[:, None, :] <= row0[None, :, None], axis=-1)
    is_e = e_of[..., None] == jnp.arange(N_EXPERTS, dtype=jnp.int32)
    shift = jnp.sum(jnp.where(is_e, (seg_start - loc_off)[:, None, :], 0), axis=-1)
    table = jnp.where(e_of < N_EXPERTS, row0[None, :] + shift, -1)
    table = jnp.pad(table, ((0, 0), (0, TAB_W - N_CHUNKS)), constant_values=-1)

    pos_cols = pos.T
    if slots is None:
        slots = jnp.zeros((n_blocks * MOE_BLOCK, f.shape[1] // PACK), jnp.uint32)
    xs_sorted = _moe_dispatch(f, pos_cols, table, slots)
    ys = _experts(xs_sorted, block_expert, n_active, w_gate, w_up, w_down, layer)
    out = _moe_combine(ys, pos_cols, weight.T, table, xs, gate3, n_lat_tiles, blocks_per_batch, final_gain)
    return out, xs_sorted


def _sc_layer_kernel(x_ref, gain_ref, shift_ref, scale_ref, win_ref, cw_ref, wout_ref, gate_ref, o_ref, *,
                     n_lat_tiles):
    d = D_MODEL
    tm = x_ref.shape[0]
    seg = jnp.where(pl.program_id(0) >= n_lat_tiles, CTX_LEN, GRID_W)
    pos = lax.broadcasted_iota(jnp.int32, (tm, 1), 0) & (seg - 1)
    hb = _modulated(x_ref[...], gain_ref[...], shift_ref[0], scale_ref[0]).astype(BF16)
    u = (jnp.dot(hb, win_ref[:, d:2 * d], preferred_element_type=F32)
         * jnp.dot(hb, win_ref[:, 2 * d:], preferred_element_type=F32))
    prev = jnp.where(pos != 0, pltpu.roll(u, 1, axis=0), 0.0)
    nxt = jnp.where(pos != seg - 1, pltpu.roll(u, tm - 1, axis=0), 0.0)
    cw = cw_ref[...]
    y = jnp.dot(hb, win_ref[:, :d], preferred_element_type=F32) * (
        cw[0:1] * prev + cw[1:2] * u + cw[2:3] * nxt)
    o_ref[...] = x_ref[...] + gate_ref[0] * jnp.dot(y.astype(BF16), wout_ref[...],
                                                    preferred_element_type=F32)


def _shortconv_layer(xs, gain, shift3, scale3, w_in, conv_w, w_out, gate3, n_lat_tiles, blocks_per_batch):
    t, d = xs.shape
    tm = HY_TOK_TILE
    per_tile = tm // DN_BLOCK
    row = lambda i: (i, 0)
    fixed = lambda i: (0, 0)
    grp = lambda i: (_group_of_block(i * per_tile, n_lat_tiles * per_tile, blocks_per_batch), 0, 0)
    return pl.pallas_call(
        functools.partial(_sc_layer_kernel, n_lat_tiles=n_lat_tiles),
        out_shape=jax.ShapeDtypeStruct((t, d), F32),
        grid=(t // tm,),
        in_specs=[pl.BlockSpec((tm, d), row),
                  pl.BlockSpec((1, d), fixed),
                  pl.BlockSpec((1, 1, d), grp),
                  pl.BlockSpec((1, 1, d), grp),
                  pl.BlockSpec((d, 3 * d), fixed),
                  pl.BlockSpec((3, d), fixed),
                  pl.BlockSpec((d, d), fixed),
                  pl.BlockSpec((1, 1, d), grp)],
        out_specs=pl.BlockSpec((tm, d), row),
        compiler_params=pltpu.CompilerParams(
            dimension_semantics=("arbitrary",), vmem_limit_bytes=VMEM_LIMIT_BYTES),
        name="shortconv_layer",
    )(xs, gain.reshape(1, d), shift3, scale3, w_in.astype(BF16), conv_w, w_out.astype(BF16), gate3)


def kernel(x, c, ctx, c_ctx, ada_w, ada_b, norm_mix, norm_ffn, norm_final, dn_w_in, dn_conv, dn_a_log,
           dn_dt_bias, dn_out_norm, dn_w_out, hy_w_in, hy_conv, hy_f_w1, hy_f_b1, hy_f_freq, hy_f_w2,
           hy_f_b2, hy_f_w3, hy_bias, hy_w_out, sc_w_in, sc_conv, sc_w_out, w_router, router_bias,
           moe_w_gate, moe_w_up, moe_w_down):
    d = D_MODEL
    bsz, seq, _ = x.shape
    n_ctx = bsz * CTX_LEN
    n_lat = bsz * seq
    silu_c = jax.nn.silu(c)
    silu_cc = jax.nn.silu(c_ctx)
    hp = lax.Precision.HIGHEST

    assert N_MIXERS > 0 and 0 % N_MIXERS == 0
    xs = (x.reshape(n_lat, d), ctx.reshape(n_ctx, d))
    n_lat_blocks, blocks_per_batch = n_lat // DN_BLOCK, seq // DN_BLOCK
    slots = None

    for i in range(DEPTH):
        kind, j = i % N_MIXERS, i // N_MIXERS
        ml = jnp.split(jnp.dot(silu_c, ada_w[i], precision=hp) + ada_b[i], N_MOD, axis=-1)
        mc = jnp.split(jnp.dot(silu_cc, ada_w[i], precision=hp) + ada_b[i], N_MOD, axis=-1)
        mod = [jnp.concatenate([mc[m][None], ml[m]], axis=0)[:, None, :] for m in range(N_MOD)]
        if kind == 0:
            xs = _deltanet_layer(xs, norm_mix[i], mod[0], mod[1], dn_w_in[j], dn_conv[j], dn_a_log[j],
                                 dn_dt_bias[j], dn_out_norm[j], dn_w_out[j], mod[2], n_lat_blocks,
                                 blocks_per_batch)
        elif kind == 1:
            xs = _hyena_layer(xs, norm_mix[i], mod[0], mod[1], hy_w_in[j], hy_conv[j], hy_f_w1[j], hy_f_b1[j],
                              hy_f_freq[j], hy_f_w2[j], hy_f_b2[j], hy_f_w3[j], hy_bias[j], hy_w_out[j],
                              mod[2], bsz, seq)
        else:
            xs = _shortconv_layer(xs, norm_mix[i], mod[0], mod[1], sc_w_in[j], sc_conv[j], sc_w_out[j],
                                  mod[2], n_lat // HY_TOK_TILE, blocks_per_batch)
        xs, slots = _moe_layer(xs, norm_ffn[i], mod[3], mod[4], mod[5], w_router, router_bias,
                               moe_w_gate, moe_w_up, moe_w_down, i, n_lat // MOE_TILE, blocks_per_batch,
                               norm_final if i == DEPTH - 1 else None, slots)
    return xs.reshape(bsz, seq, d)
```

```python
import functools
import math

import numpy as np
import jax
import jax.numpy as jnp
from jax import lax
from jax.experimental import pallas as pl
from jax.experimental.pallas import tpu as pltpu

D_MODEL = 1024
DEPTH = 4
CTX_LEN = 256
GRID_W = 64
N_MIXERS = 3
EPS = 1e-6
N_MOD = 6

DN_HEADS = 8
DN_DK = D_MODEL // DN_HEADS
DN_DV = D_MODEL // DN_HEADS
DN_CHUNK = 64

HY_ORDER = 2
HY_BANDS = 16
HY_TARGET = 1e-2
HY_FAST = 0.3
HY_SLOW = 1.5

N_EXPERTS = 32
N_GROUPS = 8
EXPERTS_PER_GROUP = N_EXPERTS // N_GROUPS
GROUP_SCORE_K = 2
TOP_K = 2
D_EXPERT = 512
MOE_BLOCK = 1024

F32 = jnp.float32
BF16 = jnp.bfloat16

VMEM_LIMIT_BYTES = 48 * 1024 * 1024


DN_BLOCK = CTX_LEN
DN_HB = DN_HEADS
DN_OUT_TILE = 512
N_CHUNKS_PER_BLOCK = DN_BLOCK // DN_CHUNK


def _group_of_block(i, n_lat_blocks, blocks_per_batch):
    return jnp.where(i >= n_lat_blocks, 0, 1 + i // blocks_per_batch)


def _modulated(x, gain, shift, scale):
    y = x * lax.rsqrt(jnp.mean(x * x, axis=-1, keepdims=True) + EPS) * gain
    return y * (1 + scale) + shift


def _stream_rows(x_refs, is_ctx):
    if len(x_refs) == 1:
        return x_refs[0][...]
    return jnp.where(is_ctx, x_refs[1][...], x_refs[0][...])


def _dn_inproj_kernel(*refs, n_lat_blocks, n_src):
    x_refs = refs[:n_src]
    (gain_ref, shift_ref, scale_ref, w_ref, wab_ref, cw_ref, alog_ref, dtb_ref,
     q_ref, k_ref, v_ref, z_ref, gate_ref, gate_t_ref) = refs[n_src:]
    i = pl.program_id(0)
    nrow = DN_BLOCK
    d = D_MODEL
    pair = 2 * DN_DK
    seg = jnp.where(i >= n_lat_blocks, CTX_LEN, GRID_W)
    r = lax.broadcasted_iota(jnp.int32, (nrow, 1), 0)
    pos = r & (seg - 1)
    not_first = pos != 0
    not_last = pos != seg - 1
    h = _modulated(_stream_rows(x_refs, i >= n_lat_blocks), gain_ref[...], shift_ref[0], scale_ref[0])
    hb = h.astype(BF16)
    outs = (q_ref, k_ref, v_ref)
    for part in range(3):
        for hp in range(d // pair):
            col = part * d + hp * pair
            x = jnp.dot(hb, w_ref[:, col:col + pair], preferred_element_type=F32)
            cw = cw_ref[:, col:col + pair]
            xp = jnp.where(not_first, pltpu.roll(x, 1, axis=0), 0.0)
            xn = jnp.where(not_last, pltpu.roll(x, nrow - 1, axis=0), 0.0)
            y = cw[0:1] * xp + cw[1:2] * x + cw[2:3] * xn
            y = y * jax.nn.sigmoid(y)
            for hh in range(2):
                yh = y[:, hh * DN_DK:(hh + 1) * DN_DK]
                if part < 2:
                    yh = yh * lax.rsqrt(jnp.sum(yh * yh, axis=-1, keepdims=True) + EPS)
                if part == 0:
                    yh = yh * DN_DK ** -0.5
                outs[part][:, hp * pair + hh * DN_DK:hp * pair + (hh + 1) * DN_DK] = yh
    for j in range(d // pair):
        z_ref[:, j * pair:(j + 1) * pair] = jnp.dot(hb, w_ref[:, 3 * d + j * pair:3 * d + (j + 1) * pair],
                                                    preferred_element_type=F32).astype(BF16)

    ab = jnp.dot(hb, wab_ref[...], preferred_element_type=F32)
    nd = 2 * DN_HEADS
    a = ab[:, :nd] + dtb_ref[...]
    softplus = jnp.maximum(a, 0.0) + jnp.log(1.0 + jnp.exp(-jnp.abs(a)))
    g = -jnp.exp(alog_ref[...]) * softplus
    beta = jax.nn.sigmoid(ab[:, nd:])
    cpos = r & (DN_CHUNK - 1)
    gp, gs = g, g
    sh = 1
    while sh < DN_CHUNK:
        gp = gp + jnp.where(cpos >= sh, pltpu.roll(gp, sh, axis=0), 0.0)
        gs = gs + jnp.where(cpos < DN_CHUNK - sh, pltpu.roll(gs, nrow - sh, axis=0), 0.0)
        sh *= 2
    colid = lax.broadcasted_iota(jnp.int32, (1, nd), 1)
    gates = jnp.concatenate([jnp.where(colid < DN_HEADS, gp, gs), beta], axis=1)
    gate_ref[...] = gates
    gate_t_ref[...] = gates.T


def _stream_specs(xs, n_lat_tiles, rows=DN_BLOCK):
    if not isinstance(xs, tuple):
        return [pl.BlockSpec((rows, xs.shape[1]), lambda i: (i, 0))], xs.shape[0]
    d = xs[0].shape[1]
    return [pl.BlockSpec((rows, d), lambda i: (jnp.minimum(i, n_lat_tiles - 1), 0)),
            pl.BlockSpec((rows, d), lambda i: (jnp.maximum(i - n_lat_tiles, 0), 0))], sum(a.shape[0] for a in xs)


def _dn_inproj(xs, gain, shift3, scale3, w_in, conv_w, a_log, dt_bias, n_lat_blocks, blocks_per_batch):
    x_specs, t = _stream_specs(xs, n_lat_blocks)
    x_args = xs if isinstance(xs, tuple) else (xs,)
    d = D_MODEL
    nd = 2 * DN_HEADS
    row = lambda i: (i, 0)
    fixed = lambda i: (0, 0)
    grp = lambda i: (_group_of_block(i, n_lat_blocks, blocks_per_batch), 0, 0)
    return pl.pallas_call(
        functools.partial(_dn_inproj_kernel, n_lat_blocks=n_lat_blocks, n_src=len(x_args)),
        out_shape=[jax.ShapeDtypeStruct((t, d), F32)] * 3 + [jax.ShapeDtypeStruct((t, d), BF16),
                                                              jax.ShapeDtypeStruct((t, 2 * nd), F32),
                                                              jax.ShapeDtypeStruct((2 * nd, t), F32)],
        grid=(t // DN_BLOCK,),
        in_specs=x_specs + [
                  pl.BlockSpec((1, d), fixed),
                  pl.BlockSpec((1, 1, d), grp),
                  pl.BlockSpec((1, 1, d), grp),
                  pl.BlockSpec((d, 4 * d), fixed),
                  pl.BlockSpec((d, 2 * nd), fixed),
                  pl.BlockSpec((3, 3 * d), fixed),
                  pl.BlockSpec((1, nd), fixed),
                  pl.BlockSpec((1, nd), fixed)],
        out_specs=[pl.BlockSpec((DN_BLOCK, d), row)] * 4 + [pl.BlockSpec((DN_BLOCK, 2 * nd), row),
                                                            pl.BlockSpec((2 * nd, DN_BLOCK), lambda i: (0, i))],
        compiler_params=pltpu.CompilerParams(
            dimension_semantics=("arbitrary",), vmem_limit_bytes=VMEM_LIMIT_BYTES),
        name="dn_inproj",
    )(*x_args, gain.reshape(1, d), shift3, scale3, w_in[:, :4 * d].astype(BF16), w_in[:, 4 * d:].astype(BF16),
      conv_w, a_log.reshape(1, nd), dt_bias.reshape(1, nd))


def _dotb(a, b):
    return jnp.dot(a.astype(BF16), b.astype(BF16), preferred_element_type=F32)


def _dotb_nt(a, b):
    return lax.dot_general(a.astype(BF16), b.astype(BF16), (((1,), (1,)), ((), ())),
                           preferred_element_type=F32)


def _dotb_tn(a, b):
    return lax.dot_general(a.astype(BF16), b.astype(BF16), (((0,), (0,)), ((), ())),
                           preferred_element_type=F32)


def _unit_tri_inverses(mats):
    c = DN_CHUNK
    assert len(mats) % 2 == 0
    ii = lax.broadcasted_iota(jnp.int32, (c, 2 * c), 0)
    lane = lax.broadcasted_iota(jnp.int32, (c, 2 * c), 1)
    jj = lane & (c - 1)
    left = lane < c

    def blockdiag(p):
        pb = p.astype(BF16)
        zero = jnp.zeros_like(pb)
        return jnp.concatenate([jnp.where(left, pb, zero), jnp.where(left, zero, pb)], axis=0)

    def mul(p, q):
        return jnp.dot(p.astype(BF16), blockdiag(q), preferred_element_type=F32)

    pairs = [jnp.concatenate([mats[i], mats[i + 1]], axis=1) for i in range(0, len(mats), 2)]
    eye = (ii == jj).astype(F32)
    diag8 = (ii >> 3) == (jj >> 3)
    n = [-jnp.where(diag8, a, 0.0) for a in pairs]
    n2 = [mul(x, x) for x in n]
    m = [eye + x for x in n]
    m = [x + mul(x, y) for x, y in zip(m, n2)]
    n4 = [mul(x, x) for x in n2]
    m = [x + mul(x, y) for x, y in zip(m, n4)]
    sh = 3
    while (1 << sh) < c:
        off = ((ii >> (sh + 1)) == (jj >> (sh + 1))) & ((ii >> sh) != (jj >> sh))
        cm = [mul(jnp.where(off, a, 0.0), x) for a, x in zip(pairs, m)]
        m = [x - mul(x, y) for x, y in zip(m, cm)]
        sh += 1
    return [half for x in m for half in (x[:, :c], x[:, c:])]


def _dn_scan_kernel(qf_ref, kf_ref, vf_ref, gcf_ref, grf_ref, qb_ref, kb_ref, vb_ref, gcb_ref, grb_ref,
                    of_ref, ob_ref, s_ref):
    @pl.when(pl.program_id(2) == 0)
    def _():
        s_ref[...] = jnp.zeros_like(s_ref)

    c = DN_CHUNK
    ncb = N_CHUNKS_PER_BLOCK
    ii = lax.broadcasted_iota(jnp.int32, (c, c), 0)
    jj = lax.broadcasted_iota(jnp.int32, (c, c), 1)
    incl = (ii >= jj, ii <= jj)
    strict = (ii > jj, ii < jj)
    dirs = ((qf_ref, kf_ref, vf_ref, gcf_ref, grf_ref, of_ref),
            (qb_ref, kb_ref, vb_ref, gcb_ref, grb_ref, ob_ref))
    items = [(d, hh, ci) for d in range(2) for hh in range(DN_HB) for ci in range(ncb)]

    def rows(ci):
        return slice(ci * c, (ci + 1) * c)

    def cols(hh):
        return slice(hh * DN_DK, (hh + 1) * DN_DK)

    q = [dirs[d][0][rows(ci), cols(hh)] for d, hh, ci in items]
    k = [dirs[d][1][rows(ci), cols(hh)] for d, hh, ci in items]
    v = [dirs[d][2][rows(ci), cols(hh)] for d, hh, ci in items]
    nh = DN_HEADS
    gc = [dirs[d][3][rows(ci), d * nh + hh:d * nh + hh + 1] for d, hh, ci in items]
    gr = [dirs[d][4][d * nh + hh:d * nh + hh + 1, rows(ci)] for d, hh, ci in items]
    beta = [dirs[d][3][rows(ci), (2 + d) * nh + hh:(2 + d) * nh + hh + 1] for d, hh, ci in items]

    decay = [jnp.where(incl[it[0]], jnp.exp(jnp.where(incl[it[0]], x - y, 0.0)), 0.0)
             for it, x, y in zip(items, gc, gr)]
    kb = [x * y for x, y in zip(k, beta)]
    a = [_dotb_nt(x, y) * jnp.where(strict[it[0]], z, 0.0) for it, x, y, z in zip(items, kb, k, decay)]
    attn = [_dotb_nt(x, y) * z for x, y, z in zip(q, k, decay)]
    tinv = _unit_tri_inverses(a)
    eg = [jnp.exp(x) for x in gc]
    uw = [_dotb(t, jnp.concatenate([x * y, z * e], axis=-1))
          for t, x, y, z, e in zip(tinv, v, beta, kb, eg)]
    g_last = [x[0:1] if it[0] else x[c - 1:c] for it, x in zip(items, gc)]
    wq = [jnp.concatenate([x[:, DN_DV:], y * e], axis=0) for x, y, e in zip(uw, q, eg)]
    k_dec = [x * jnp.exp(y - z) for x, y, z in zip(k, g_last, gc)]
    s_dec = [jnp.exp(x) for x in g_last]

    chains = [(d, hh) for d in range(2) for hh in range(DN_HB)]
    state = [s_ref[d, hh] for d, hh in chains]
    for step in range(ncb):
        cur = [items.index((d, hh, ncb - 1 - step if d else step)) for d, hh in chains]
        ws = [_dotb(wq[n], s) for n, s in zip(cur, state)]
        v_new = [uw[n][:, :DN_DV] - x[:c] for n, x in zip(cur, ws)]
        o = [x[c:] + _dotb(attn[n], y) for n, x, y in zip(cur, ws, v_new)]
        state = [s * s_dec[n] + _dotb_tn(k_dec[n], y) for n, s, y in zip(cur, state, v_new)]
        for n, x in zip(cur, o):
            d, hh, ci = items[n]
            dirs[d][5][rows(ci), cols(hh)] = x.astype(BF16)
    for (d, hh), s in zip(chains, state):
        s_ref[d, hh] = s


def _dn_scan(q, k, v, gates, gates_t, n_lat_blocks, blocks_per_batch):
    t, d = q.shape
    bsz = n_lat_blocks // blocks_per_batch
    assert DN_HB == DN_HEADS
    ng = gates.shape[1]

    def blk_f(b, s):
        return jnp.where(s == 0, n_lat_blocks + b, b * blocks_per_batch + s - 1)

    def blk_b(b, s):
        return jnp.where(s == 0, n_lat_blocks + b, b * blocks_per_batch + blocks_per_batch - s)

    hw = DN_HB * DN_DK

    def specs(blk):
        return [pl.BlockSpec((DN_BLOCK, hw), lambda b, hg, s: (blk(b, s), hg))] * 3 + [
            pl.BlockSpec((DN_BLOCK, ng), lambda b, hg, s: (blk(b, s), 0)),
            pl.BlockSpec((ng, DN_BLOCK), lambda b, hg, s: (0, blk(b, s)))]

    return pl.pallas_call(
        _dn_scan_kernel,
        out_shape=[jax.ShapeDtypeStruct((t, d), BF16)] * 2,
        grid=(bsz, DN_HEADS // DN_HB, 1 + blocks_per_batch),
        in_specs=specs(blk_f) + specs(blk_b),
        out_specs=[pl.BlockSpec((DN_BLOCK, hw), lambda b, hg, s: (blk_f(b, s), hg)),
                   pl.BlockSpec((DN_BLOCK, hw), lambda b, hg, s: (blk_b(b, s), hg))],
        scratch_shapes=[pltpu.VMEM((2, DN_HB, DN_DK, DN_DV), F32)],
        compiler_params=pltpu.CompilerParams(
            dimension_semantics=("arbitrary", "arbitrary", "arbitrary"),
            vmem_limit_bytes=VMEM_LIMIT_BYTES),
        name="dn_scan",
    )(q, k, v, gates, gates_t, q, k, v, gates, gates_t)


def _dn_out_kernel(of_ref, ob_ref, z_ref, on_ref, w_ref, gate_ref, *refs, n_lat_tiles):
    x_refs, o_ref = refs[:-1], refs[-1]
    z = z_ref[...].astype(F32)
    zs = z * jax.nn.sigmoid(z)
    parts = []
    for h in range(DN_HEADS):
        cols = slice(h * DN_DV, (h + 1) * DN_DV)
        o = of_ref[:, cols].astype(F32) + ob_ref[:, cols].astype(F32)
        o = o * lax.rsqrt(jnp.mean(o * o, axis=-1, keepdims=True) + EPS)
        parts.append(o)
    y = jnp.concatenate(parts, axis=-1) * on_ref[...] * zs
    x = _stream_rows(x_refs, pl.program_id(0) >= n_lat_tiles)
    o_ref[...] = x + gate_ref[0] * jnp.dot(y.astype(BF16), w_ref[...], preferred_element_type=F32)


def _dn_out(o_f, o_b, z, out_norm, w_out, xs, gate3, n_lat_blocks, blocks_per_batch):
    tm = DN_OUT_TILE
    per_tile = tm // DN_BLOCK
    n_lat_tiles = n_lat_blocks // per_tile
    x_specs, t = _stream_specs(xs, n_lat_tiles, tm)
    x_args = xs if isinstance(xs, tuple) else (xs,)
    d = D_MODEL
    row = lambda i: (i, 0)
    fixed = lambda i: (0, 0)
    grp = lambda i: (_group_of_block(i * per_tile, n_lat_blocks, blocks_per_batch), 0, 0)
    return pl.pallas_call(
        functools.partial(_dn_out_kernel, n_lat_tiles=n_lat_tiles),
        out_shape=jax.ShapeDtypeStruct((t, d), F32),
        grid=(t // tm,),
        in_specs=[pl.BlockSpec((tm, d), row),
                  pl.BlockSpec((tm, d), row),
                  pl.BlockSpec((tm, d), row),
                  pl.BlockSpec((1, d), fixed),
                  pl.BlockSpec((d, d), fixed),
                  pl.BlockSpec((1, 1, d), grp)] + x_specs,
        out_specs=pl.BlockSpec((tm, d), row),
        compiler_params=pltpu.CompilerParams(
            dimension_semantics=("arbitrary",), vmem_limit_bytes=VMEM_LIMIT_BYTES),
        name="dn_out",
    )(o_f, o_b, z, jnp.tile(out_norm, DN_HEADS).reshape(1, d), w_out.astype(BF16), gate3, *x_args)


def _deltanet_layer(xs, gain, shift3, scale3, w_in, conv_w, a_log, dt_bias, out_norm, w_out, gate3,
                    n_lat_blocks, blocks_per_batch):
    q, k, v, z, gates, gates_t = _dn_inproj(xs, gain, shift3, scale3, w_in, conv_w, a_log, dt_bias,
                                            n_lat_blocks, blocks_per_batch)
    o_f, o_b = _dn_scan(q, k, v, gates, gates_t, n_lat_blocks, blocks_per_batch)
    return _dn_out(o_f, o_b, z, out_norm, w_out, xs, gate3, n_lat_blocks, blocks_per_batch)


HY_N2 = 256
HY_CB = 16
HY_TOK_TILE = 512
HY_FILT_TILE = 512


def _dft_constants(nr):
    n1, n2 = 2 * nr, HY_N2
    n = n1 * n2
    nk = -(-(nr + 1) // 8) * 8
    keep = np.arange(nk) <= nr
    k1 = np.where(keep, np.arange(nk), 0)
    f1 = np.exp(-2j * np.pi * np.outer(k1, np.arange(nr)) / n1) * keep[:, None]
    twice = np.where((k1 > 0) & (k1 < nr), 2.0, 1.0) * keep
    lhs_fwd = np.concatenate([f1.real, f1.imag], axis=0)
    lhs_inv = np.concatenate([f1.real.T * twice, f1.imag.T * twice], axis=1) / n
    tw = np.exp(-2j * np.pi * np.outer(k1, np.arange(n2)) / n)
    a2 = np.arange(n2)
    f2 = np.exp(-2j * np.pi * np.outer(a2, a2) / n2)
    w_fwd = np.block([[f2.real, f2.imag], [-f2.imag, f2.real]])
    w_inv = np.block([[f2.real, -f2.imag], [f2.imag, f2.real]])
    return (jnp.asarray(lhs_fwd, BF16), jnp.asarray(lhs_inv, BF16), jnp.asarray(tw.real, F32),
            jnp.asarray(tw.imag, F32), jnp.asarray(w_fwd, BF16), jnp.asarray(w_inv, BF16))


def _hy_dft(x3, lhs_fwd, twr, twi, w_fwd):
    n1 = twr.shape[0]
    a = [jnp.dot(lhs_fwd, x3[c].astype(BF16), preferred_element_type=F32) for c in range(x3.shape[0])]
    br = jnp.concatenate([t[:n1] * twr - t[n1:] * twi for t in a], axis=0)
    bi = jnp.concatenate([t[:n1] * twi + t[n1:] * twr for t in a], axis=0)
    b = jnp.concatenate([br, bi], axis=1)
    return jnp.dot(b.astype(BF16), w_fwd, preferred_element_type=F32)


def _hy_idft(p, cb, lhs_inv, twr, twi, w_inv):
    n1, n2 = twr.shape
    c = jnp.dot(p.astype(BF16), w_inv, preferred_element_type=F32)
    out = []
    for ch in range(cb):
        cr = c[ch * n1:(ch + 1) * n1, :n2]
        ci = c[ch * n1:(ch + 1) * n1, n2:]
        d = jnp.concatenate([cr * twr + ci * twi, ci * twr - cr * twi], axis=0)
        out.append(jnp.dot(lhs_inv, d.astype(BF16), preferred_element_type=F32))
    return out


def _hy_spectrum_kernel(hf_ref, hb_ref, lf_ref, twr_ref, twi_ref, wf_ref, o_ref):
    cb, nr, n2 = hf_ref.shape
    first = ((lax.broadcasted_iota(jnp.int32, (nr, n2), 0) == 0)
             & (lax.broadcasted_iota(jnp.int32, (nr, n2), 1) == 0))
    hb = jnp.where(first, 0.0, hb_ref[...])
    consts = (lf_ref[...], twr_ref[...], twi_ref[...], wf_ref[...])
    xf = _hy_dft(hf_ref[...], *consts)
    xb = _hy_dft(hb, *consts)
    o_ref[...] = jnp.concatenate([xf[:, :n2] + xb[:, :n2], xf[:, n2:] - xb[:, n2:]],
                                 axis=1).reshape(o_ref.shape)


def _hy_spectrum(filt, consts):
    d = D_MODEL
    l = filt.shape[1]
    nr = l // HY_N2
    lhs_fwd, _, twr, twi, w_fwd, _ = consts
    n1 = twr.shape[0]
    cpo = d // HY_CB
    fixed2 = lambda o, c: (0, 0)
    return pl.pallas_call(
        _hy_spectrum_kernel,
        out_shape=jax.ShapeDtypeStruct((HY_ORDER * d, n1, 2 * HY_N2), F32),
        grid=(HY_ORDER, cpo),
        in_specs=[pl.BlockSpec((HY_CB, nr, HY_N2), lambda o, c: (2 * o * cpo + c, 0, 0)),
                  pl.BlockSpec((HY_CB, nr, HY_N2), lambda o, c: ((2 * o + 1) * cpo + c, 0, 0)),
                  pl.BlockSpec(lhs_fwd.shape, fixed2),
                  pl.BlockSpec(twr.shape, fixed2),
                  pl.BlockSpec(twi.shape, fixed2),
                  pl.BlockSpec(w_fwd.shape, fixed2)],
        out_specs=pl.BlockSpec((HY_CB, n1, 2 * HY_N2), lambda o, c: (o * cpo + c, 0, 0)),
        compiler_params=pltpu.CompilerParams(
            dimension_semantics=("arbitrary", "arbitrary"), vmem_limit_bytes=VMEM_LIMIT_BYTES),
        name="hy_spectrum",
    )(filt.reshape(-1, nr, HY_N2), filt.reshape(-1, nr, HY_N2), lhs_fwd, twr, twi, w_fwd)


def _hy_conv_kernel(z_ref, g_ref, k_ref, bias_ref, lf_ref, li_ref, twr_ref, twi_ref, wf_ref, wi_ref,
                    o_ref):
    cb, nr, n2 = z_ref.shape
    twr, twi = twr_ref[...], twi_ref[...]
    z = z_ref[...]
    x = _hy_dft(z, lf_ref[...], twr, twi, wf_ref[...])
    kk = k_ref[...].reshape(x.shape)
    xr, xi, kr, ki = x[:, :n2], x[:, n2:], kk[:, :n2], kk[:, n2:]
    p = jnp.concatenate([xr * kr - xi * ki, xr * ki + xi * kr], axis=1)
    conv = _hy_idft(p, cb, li_ref[...], twr, twi, wi_ref[...])
    for c in range(cb):
        o_ref[c] = g_ref[c] * (conv[c] + bias_ref[c] * z[c])


def _hy_conv(z, z_part, gate, gate_part, khat, order, bias, consts, bsz):
    d = D_MODEL
    l = z.shape[1] // bsz
    nr = l // HY_N2
    cpo = d // HY_CB
    lhs_fwd, lhs_inv, twr, twi, w_fwd, w_inv = consts
    n1 = twr.shape[0]
    fixed2 = lambda c, b: (0, 0)
    out = pl.pallas_call(
        _hy_conv_kernel,
        out_shape=jax.ShapeDtypeStruct((d, bsz * nr, HY_N2), F32),
        grid=(cpo, bsz),
        in_specs=[pl.BlockSpec((HY_CB, nr, HY_N2), lambda c, b: (z_part * cpo + c, b, 0)),
                  pl.BlockSpec((HY_CB, nr, HY_N2), lambda c, b: (gate_part * cpo + c, b, 0)),
                  pl.BlockSpec((HY_CB, n1, 2 * HY_N2), lambda c, b: (order * cpo + c, 0, 0)),
                  pl.BlockSpec((HY_CB, 1, 1), lambda c, b: (c, 0, 0)),
                  pl.BlockSpec(lhs_fwd.shape, fixed2),
                  pl.BlockSpec(lhs_inv.shape, fixed2),
                  pl.BlockSpec(twr.shape, fixed2),
                  pl.BlockSpec(twi.shape, fixed2),
                  pl.BlockSpec(w_fwd.shape, fixed2),
                  pl.BlockSpec(w_inv.shape, fixed2)],
        out_specs=pl.BlockSpec((HY_CB, nr, HY_N2), lambda c, b: (c, b, 0)),
        compiler_params=pltpu.CompilerParams(
            dimension_semantics=("arbitrary", "arbitrary"), vmem_limit_bytes=VMEM_LIMIT_BYTES),
        name="hy_conv",
    )(z.reshape(-1, bsz * nr, HY_N2), gate.reshape(-1, bsz * nr, HY_N2), khat,
      bias.reshape(d, 1, 1), lhs_fwd, lhs_inv, twr, twi, w_fwd, w_inv)
    return out.reshape(d, bsz * l)


def _hy_ctx_kernel(z_ref, g_ref, hf_ref, hb_ref, bias_ref, wf_ref, wi_ref, o_ref, *, bsz):
    l = hf_ref.shape[1]
    wf, wi = wf_ref[...], wi_ref[...]
    hb = jnp.where(lax.broadcasted_iota(jnp.int32, (1, l), 1) == 0, 0.0, hb_ref[...])
    kf = jnp.dot(hf_ref[...].astype(BF16), wf, preferred_element_type=F32)
    kb = jnp.dot(hb.astype(BF16), wf, preferred_element_type=F32)
    n = 2 * l
    kr, ki = kf[:, :n] + kb[:, :n], kf[:, n:] - kb[:, n:]
    bias = bias_ref[...]
    for b in range(bsz):
        z = z_ref[:, b * l:(b + 1) * l]
        x = jnp.dot(z.astype(BF16), wf, preferred_element_type=F32)
        xr, xi = x[:, :n], x[:, n:]
        p = jnp.concatenate([xr * kr - xi * ki, xr * ki + xi * kr], axis=1)
        conv = jnp.dot(p.astype(BF16), wi, preferred_element_type=F32)
        o_ref[:, b * l:(b + 1) * l] = g_ref[:, b * l:(b + 1) * l] * (conv + bias * z)


def _hy_ctx(z, z_part, gate, gate_part, filt, order, bias, bsz):
    d = D_MODEL
    l = filt.shape[1]
    n = 2 * l
    ang = 2 * np.pi * np.outer(np.arange(l), np.arange(n)) / n
    w_fwd = jnp.asarray(np.concatenate([np.cos(ang), -np.sin(ang)], axis=1), BF16)
    w_inv = jnp.asarray(np.concatenate([np.cos(ang.T), -np.sin(ang.T)], axis=0) / n, BF16)
    cb = 256
    nblk = d // cb
    fixed = lambda c: (0, 0)
    return pl.pallas_call(
        functools.partial(_hy_ctx_kernel, bsz=bsz),
        out_shape=jax.ShapeDtypeStruct((d, bsz * l), F32),
        grid=(nblk,),
        in_specs=[pl.BlockSpec((cb, bsz * l), lambda c: (z_part * nblk + c, 0)),
                  pl.BlockSpec((cb, bsz * l), lambda c: (gate_part * nblk + c, 0)),
                  pl.BlockSpec((cb, l), lambda c: (2 * order * nblk + c, 0)),
                  pl.BlockSpec((cb, l), lambda c: ((2 * order + 1) * nblk + c, 0)),
                  pl.BlockSpec((cb, 1), lambda c: (c, 0)),
                  pl.BlockSpec(w_fwd.shape, fixed),
                  pl.BlockSpec(w_inv.shape, fixed)],
        out_specs=pl.BlockSpec((cb, bsz * l), lambda c: (c, 0)),
        compiler_params=pltpu.CompilerParams(
            dimension_semantics=("arbitrary",), vmem_limit_bytes=VMEM_LIMIT_BYTES),
        name="hy_ctx_conv",
    )(z, gate, filt, filt, bias.reshape(d, 1), w_fwd, w_inv)


def _hy_filter_kernel(band_ref, w1t_ref, w1c_ref, w1s_ref, b1_ref, fr_ref, w2_ref, b2_ref, w3_ref,
                      delta_ref, o_ref, *, l):
    tl = o_ref.shape[1]
    d = D_MODEL
    hp = lax.Precision.HIGHEST
    pos = (lax.broadcasted_iota(jnp.int32, (1, tl), 1) + pl.program_id(0) * tl).astype(F32)
    t = pos / max(l - 1, 1)
    ang = ((2 * math.pi / l) * pos) * band_ref[...]
    fr = fr_ref[...]
    pre = (w1t_ref[...] * t + jnp.dot(w1c_ref[...], jnp.cos(ang), precision=hp)
           + jnp.dot(w1s_ref[...], -jnp.sin(ang), precision=hp) + b1_ref[...])
    hdn = jnp.sin(fr * pre)
    hdn = jnp.sin(fr * (jnp.dot(w2_ref[...], hdn, precision=hp) + b2_ref[...]))
    window = jnp.exp(-t * delta_ref[...])
    for part in range(2 * HY_ORDER):
        rows = slice(part * d, (part + 1) * d)
        o_ref[rows, :] = jnp.dot(w3_ref[rows, :], hdn.astype(BF16), preferred_element_type=F32) * window


def _hy_filter(l, w1, b1, freq, w2, b2, w3):
    d = D_MODEL
    nb = HY_BANDS
    tl = min(HY_FILT_TILE, l)
    col = lambda v: v.reshape(-1, 1)
    bands = jnp.linspace(1e-4, nb - 1, nb, dtype=F32)
    deltas = jnp.abs(jnp.linspace(math.log(HY_TARGET) / HY_SLOW, math.log(HY_TARGET) / HY_FAST, d, dtype=F32))
    w1t = w1.T
    args = (col(bands), w1t[:, 0:1], w1t[:, 1:1 + nb], w1t[:, 1 + nb:], col(b1), col(freq), w2.T, col(b2),
            w3.T.astype(BF16), col(deltas))
    return pl.pallas_call(
        functools.partial(_hy_filter_kernel, l=l),
        out_shape=jax.ShapeDtypeStruct((2 * HY_ORDER * d, l), F32),
        grid=(l // tl,),
        in_specs=[pl.BlockSpec(a.shape, lambda j: (0, 0)) for a in args],
        out_specs=pl.BlockSpec((2 * HY_ORDER * d, tl), lambda j: (0, j)),
        compiler_params=pltpu.CompilerParams(
            dimension_semantics=("arbitrary",), vmem_limit_bytes=VMEM_LIMIT_BYTES),
        name="hy_filter",
    )(*args)


def _hy_inproj_kernel(x_ref, gain_ref, shift_ref, scale_ref, wt_ref, cw_ref, o_ref, *, seg):
    nch = wt_ref.shape[0]
    tm = x_ref.shape[0]
    hb = _modulated(x_ref[...], gain_ref[...], shift_ref[0], scale_ref[0]).astype(BF16)
    pos = lax.broadcasted_iota(jnp.int32, (1, tm), 1) & (seg - 1)
    not_first = pos != 0
    not_last = pos != seg - 1
    sub = 512
    for j in range(nch // sub):
        rows = slice(j * sub, (j + 1) * sub)
        p = _dotb_nt(wt_ref[rows, :], hb)
        cw = cw_ref[rows, :]
        prev = jnp.where(not_first, pltpu.roll(p, 1, axis=1), 0.0)
        nxt = jnp.where(not_last, pltpu.roll(p, tm - 1, axis=1), 0.0)
        o_ref[rows, :] = cw[:, 0:1] * prev + cw[:, 1:2] * p + cw[:, 2:3] * nxt


def _hy_inproj(xs, gain, shift3, scale3, w_in, conv_w, first_tile, n_tiles, seg, n_lat_blocks,
               blocks_per_batch):
    k = xs.shape[1]
    nch = w_in.shape[1]
    tm = HY_TOK_TILE
    per_tile = tm // DN_BLOCK
    grp = lambda i: (_group_of_block((first_tile + i) * per_tile, n_lat_blocks, blocks_per_batch), 0, 0)
    return pl.pallas_call(
        functools.partial(_hy_inproj_kernel, seg=seg),
        out_shape=jax.ShapeDtypeStruct((nch, n_tiles * tm), F32),
        grid=(n_tiles,),
        in_specs=[pl.BlockSpec((tm, k), lambda i: (first_tile + i, 0)),
                  pl.BlockSpec((1, k), lambda i: (0, 0)),
                  pl.BlockSpec((1, 1, k), grp),
                  pl.BlockSpec((1, 1, k), grp),
                  pl.BlockSpec((nch, k), lambda i: (0, 0)),
                  pl.BlockSpec((nch, 3), lambda i: (0, 0))],
        out_specs=pl.BlockSpec((nch, tm), lambda i: (0, i)),
        compiler_params=pltpu.CompilerParams(
            dimension_semantics=("arbitrary",), vmem_limit_bytes=VMEM_LIMIT_BYTES),
        name="hy_inproj",
    )(xs, gain.reshape(1, k), shift3, scale3, w_in.T.astype(BF16), conv_w.T)


def _hy_out_kernel(zl_ref, zc_ref, w_ref, x_ref, gate_ref, o_ref, *, n_lat_tiles):
    z = jnp.where(pl.program_id(0) >= n_lat_tiles, zc_ref[...], zl_ref[...])
    o_ref[...] = x_ref[...] + gate_ref[0] * _dotb_tn(z, w_ref[...])


def _hy_out(z_lat, z_ctx, w_out, xs, gate3, blocks_per_batch):
    t, d = xs.shape
    tm = HY_TOK_TILE
    n_lat_tiles = z_lat.shape[1] // tm
    per_tile = tm // DN_BLOCK
    grp = lambda i: (_group_of_block(i * per_tile, n_lat_tiles * per_tile, blocks_per_batch), 0, 0)
    return pl.pallas_call(
        functools.partial(_hy_out_kernel, n_lat_tiles=n_lat_tiles),
        out_shape=jax.ShapeDtypeStruct((t, d), F32),
        grid=(t // tm,),
        in_specs=[pl.BlockSpec((d, tm), lambda i: (0, jnp.minimum(i, n_lat_tiles - 1))),
                  pl.BlockSpec((d, tm), lambda i: (0, 0)),
                  pl.BlockSpec((d, d), lambda i: (0, 0)),
                  pl.BlockSpec((tm, d), lambda i: (i, 0)),
                  pl.BlockSpec((1, 1, d), grp)],
        out_specs=pl.BlockSpec((tm, d), lambda i: (i, 0)),
        compiler_params=pltpu.CompilerParams(
            dimension_semantics=("arbitrary",), vmem_limit_bytes=VMEM_LIMIT_BYTES),
        name="hy_out",
    )(z_lat, z_ctx, w_out.astype(BF16), xs, gate3)


def _hyena_layer(xs, gain, shift3, scale3, w_in, conv_w, f_w1, f_b1, f_freq, f_w2, f_b2, f_w3, bias, w_out,
                 gate3, bsz, seq):
    n_lat_tiles = bsz * seq // HY_TOK_TILE
    assert bsz * CTX_LEN == HY_TOK_TILE
    margs = (xs, gain, shift3, scale3, w_in, conv_w)
    blocks = (bsz * seq // DN_BLOCK, seq // DN_BLOCK)
    p_lat = _hy_inproj(*margs, 0, n_lat_tiles, GRID_W, *blocks)
    p_ctx = _hy_inproj(*margs, n_lat_tiles, 1, CTX_LEN, *blocks)
    fargs = (f_w1, f_b1, f_freq, f_w2, f_b2, f_w3)
    consts = _dft_constants(seq // HY_N2)
    khat = _hy_spectrum(_hy_filter(seq, *fargs), consts)
    filt_ctx = _hy_filter(CTX_LEN, *fargs)
    z_lat, z_ctx = p_lat, p_ctx
    for n in range(HY_ORDER):
        z_lat = _hy_conv(z_lat, 0, p_lat, n + 1, khat, n, bias[n], consts, bsz)
        z_ctx = _hy_ctx(z_ctx, 0, p_ctx, n + 1, filt_ctx, n, bias[n], bsz)
    return _hy_out(z_lat, z_ctx, w_out, xs, gate3, seq // DN_BLOCK)


MOE_TILE = 512
PACK = 2


def _experts_kernel(be_ref, na_ref, x_ref, wg_ref, wu_ref, wd_ref, o_ref, wgb_ref, wub_ref, wdb_ref):
    i = pl.program_id(0)
    prev = be_ref[jnp.maximum(i - 1, 0)]

    @pl.when((i == 0) | (be_ref[i] != prev))
    def _():
        wgb_ref[...] = wg_ref[0, 0].astype(BF16)
        wub_ref[...] = wu_ref[0, 0].astype(BF16)
        wdb_ref[...] = wd_ref[0, 0].astype(BF16)

    @pl.when(i < na_ref[0])
    def _():
        packed = x_ref[...]
        lo = pltpu.bitcast(packed << 16, F32)
        hi = pltpu.bitcast(packed & jnp.uint32(0xFFFF0000), F32)
        xb = jnp.concatenate([lo, hi], axis=-1).astype(BF16)
        g = jnp.dot(xb, wgb_ref[...], preferred_element_type=F32)
        u = jnp.dot(xb, wub_ref[...], preferred_element_type=F32)
        hid = (g * jax.nn.sigmoid(g)) * u
        y = jnp.dot(hid.astype(BF16), wdb_ref[...], preferred_element_type=F32)
        ybits = pltpu.bitcast(y.astype(BF16).astype(F32), jnp.uint32)
        half = y.shape[1] // PACK
        o_ref[...] = (ybits[:, :half] >> 16) | (ybits[:, half:] & jnp.uint32(0xFFFF0000))

    @pl.when(i >= na_ref[0])
    def _():
        o_ref[...] = jnp.zeros_like(o_ref)


def _experts(xs_sorted, block_expert, n_active, w_gate, w_up, w_down, layer):
    n_slots, wd = xs_sorted.shape
    d = wd * PACK
    n_blocks = n_slots // MOE_BLOCK
    blk = lambda i, be, na: (jnp.minimum(i, na[0] - 1), 0)
    wsel = lambda i, be, na: (layer, be[jnp.minimum(i, na[0] - 1)], 0, 0)
    grid_spec = pltpu.PrefetchScalarGridSpec(
        num_scalar_prefetch=2,
        grid=(n_blocks,),
        in_specs=[pl.BlockSpec((MOE_BLOCK, wd), blk),
                  pl.BlockSpec((1, 1, d, D_EXPERT), wsel),
                  pl.BlockSpec((1, 1, d, D_EXPERT), wsel),
                  pl.BlockSpec((1, 1, D_EXPERT, d), wsel)],
        out_specs=pl.BlockSpec((MOE_BLOCK, wd), lambda i, be, na: (i, 0)),
        scratch_shapes=[pltpu.VMEM((d, D_EXPERT), BF16), pltpu.VMEM((d, D_EXPERT), BF16),
                        pltpu.VMEM((D_EXPERT, d), BF16)],
    )
    return pl.pallas_call(
        _experts_kernel,
        out_shape=jax.ShapeDtypeStruct((n_slots, wd), jnp.uint32),
        grid_spec=grid_spec,
        compiler_params=pltpu.CompilerParams(
            dimension_semantics=("arbitrary",), vmem_limit_bytes=VMEM_LIMIT_BYTES),
        name="moe_experts",
    )(block_expert, n_active, xs_sorted, w_gate, w_up, w_down)


MOE_CHUNK = 8
MOE_LB = 1280
assert MOE_LB >= MOE_TILE * TOP_K + N_EXPERTS * (MOE_CHUNK - 1) and MOE_LB % 128 == 0
N_CHUNKS = MOE_LB // MOE_CHUNK
TAB_W = 256
assert TAB_W >= N_CHUNKS


def _moe_route_kernel(x_ref, gain_ref, shift_ref, scale_ref, wr_ref, rb_ref, tri_ref, lt_ref,
                      f_ref, pos_ref, w_ref, cnt_ref):
    tm, d = x_ref.shape
    ne, epg, ng = N_EXPERTS, EXPERTS_PER_GROUP, N_GROUPS
    f = _modulated(x_ref[...], gain_ref[...], shift_ref[0], scale_ref[0])
    f_ref[...] = f.astype(BF16)

    logits = lax.dot_general(wr_ref[...], f, (((1,), (1,)), ((), ())),
                             precision=lax.Precision.HIGHEST, preferred_element_type=F32)
    scores = jax.nn.sigmoid(logits)
    biased = scores + rb_ref[...]
    s = [scores[j * ng:(j + 1) * ng] for j in range(epg)]
    c = [biased[j * ng:(j + 1) * ng] for j in range(epg)]
    hi01, lo01 = jnp.maximum(c[0], c[1]), jnp.minimum(c[0], c[1])
    hi23, lo23 = jnp.maximum(c[2], c[3]), jnp.minimum(c[2], c[3])
    gscore = jnp.maximum(hi01, hi23) + jnp.maximum(jnp.minimum(hi01, hi23), jnp.maximum(lo01, lo23))
    gi = lax.broadcasted_iota(jnp.int32, (ng, tm), 0)
    gmax = jnp.max(gscore, axis=0, keepdims=True)
    grp = jnp.min(jnp.where(gscore == gmax, gi, ng), axis=0, keepdims=True)
    sel = gi == grp
    cv = [jnp.sum(jnp.where(sel, t, 0.0), axis=0, keepdims=True) for t in c]
    sv = [jnp.sum(jnp.where(sel, t, 0.0), axis=0, keepdims=True) for t in s]

    def pick(excluded):
        best = jnp.full((1, tm), -jnp.inf, F32)
        idx = jnp.zeros((1, tm), jnp.int32)
        val = jnp.zeros((1, tm), F32)
        for j in range(epg):
            cand = cv[j] if excluded is None else jnp.where(excluded == j, -jnp.inf, cv[j])
            take = cand > best
            best = jnp.where(take, cand, best)
            idx = jnp.where(take, j, idx)
            val = jnp.where(take, sv[j], val)
        return idx, val

    i1, v1 = pick(None)
    i2, v2 = pick(i1)
    wsum = v1 + v2
    w_ref[0:1, :] = v1 / wsum
    w_ref[1:2, :] = v2 / wsum

    ei = lax.broadcasted_iota(jnp.int32, (ne, tm), 0)
    oh1 = ei == grp * epg + i1
    oh2 = ei == grp * epg + i2
    tri = tri_ref[...]
    pre1 = jnp.dot(oh1.astype(BF16), tri, preferred_element_type=F32)
    pre2 = jnp.dot(oh2.astype(BF16), tri, preferred_element_type=F32)
    tot1 = pre1[:, tm - 1:tm]
    tot = tot1 + pre2[:, tm - 1:tm]
    seg = jnp.floor((tot + (MOE_CHUNK - 1)) * (1.0 / MOE_CHUNK)) * MOE_CHUNK
    off = jnp.dot(lt_ref[...], jnp.broadcast_to(seg, (ne, 128)).astype(BF16),
                  preferred_element_type=F32)[:, 0:1]
    p1 = jnp.sum(jnp.where(oh1, off + pre1 - 1.0, 0.0), axis=0, keepdims=True)
    p2 = jnp.sum(jnp.where(oh2, off + tot1 + pre2 - 1.0, 0.0), axis=0, keepdims=True)
    pos_ref[0:1, :] = p1.astype(jnp.int32)
    pos_ref[1:2, :] = p2.astype(jnp.int32)
    cnt_ref[0] = jnp.broadcast_to(tot, (ne, 128))


def _moe_route(xs, gain, shift3, scale3, w_router, router_bias, n_lat_tiles, blocks_per_batch):
    t, d = xs.shape
    tm = MOE_TILE
    ne = N_EXPERTS
    per_tile = tm // DN_BLOCK
    row = lambda i: (i, 0)
    col = lambda i: (0, i)
    fixed = lambda i: (0, 0)
    grp = lambda i: (_group_of_block(i * per_tile, n_lat_tiles * per_tile, blocks_per_batch), 0, 0)
    tri = jnp.asarray(np.triu(np.ones((tm, tm), np.float32)), BF16)
    lt = jnp.asarray(np.tril(np.ones((ne, ne), np.float32), -1), BF16)
    perm = np.arange(ne).reshape(N_GROUPS, EXPERTS_PER_GROUP).T.reshape(-1)
    return pl.pallas_call(
        _moe_route_kernel,
        out_shape=[jax.ShapeDtypeStruct((t, d), BF16),
                   jax.ShapeDtypeStruct((TOP_K, t), jnp.int32),
                   jax.ShapeDtypeStruct((TOP_K, t), F32),
                   jax.ShapeDtypeStruct((t // tm, ne, 128), F32)],
        grid=(t // tm,),
        in_specs=[pl.BlockSpec((tm, d), row),
                  pl.BlockSpec((1, d), fixed),
                  pl.BlockSpec((1, 1, d), grp),
                  pl.BlockSpec((1, 1, d), grp),
                  pl.BlockSpec((ne, d), fixed),
                  pl.BlockSpec((ne, 1), fixed),
                  pl.BlockSpec((tm, tm), fixed),
                  pl.BlockSpec((ne, ne), fixed)],
        out_specs=[pl.BlockSpec((tm, d), row),
                   pl.BlockSpec((TOP_K, tm), col),
                   pl.BlockSpec((TOP_K, tm), col),
                   pl.BlockSpec((1, ne, 128), lambda i: (i, 0, 0))],
        compiler_params=pltpu.CompilerParams(
            dimension_semantics=("arbitrary",), vmem_limit_bytes=VMEM_LIMIT_BYTES),
        name="moe_route",
    )(xs, gain.reshape(1, d), shift3, scale3, w_router.T[perm], router_bias[perm].reshape(ne, 1), tri, lt)


def _chunk_row(j):
    return j * MOE_CHUNK if isinstance(j, int) else pl.multiple_of(j * MOE_CHUNK, MOE_CHUNK)


def _chunk_issue(tab_smem, tab_row, make_copy):
    def issue_pair(jj, n):
        for priority in range(2):
            j = jj * 2 + priority
            dst = tab_smem[tab_row, j]

            @pl.when(dst >= 0)
            def _():
                make_copy(j, pl.multiple_of(dst, MOE_CHUNK)).start(priority=priority)

            n = n + (dst >= 0).astype(jnp.int32)
        return n

    return lax.fori_loop(0, N_CHUNKS // 2, issue_pair, jnp.int32(0), unroll=4)


def _chunk_drain(n, make_copy):
    def drain(j, carry):
        make_copy(0, 0).wait()
        return carry

    lax.fori_loop(0, n, drain, 0)


def _moe_dispatch_kernel(tab_ref, pos_ref, f_ref, xs_in_ref, xs_ref, cnt_smem, loc_ref, sem, *, n_steps):
    del xs_in_ref
    tm, d = f_ref.shape
    i = pl.program_id(0)
    slot = i % 2
    r = lax.broadcasted_iota(jnp.int32, (tm, MOE_LB), 1)
    p = pos_ref[...]
    onehot = ((p[:, 0:1] == r) | (p[:, 1:2] == r)).astype(BF16)
    loc = _dotb_tn(onehot, f_ref[...])
    bits = pltpu.bitcast(loc, jnp.uint32)
    half = d // PACK
    loc_ref[slot] = (bits[:, :half] >> 16) | (bits[:, half:] & jnp.uint32(0xFFFF0000))

    def copy_from(buf):
        def make_copy(j, dst):
            return pltpu.make_async_copy(loc_ref.at[buf, pl.ds(_chunk_row(j), MOE_CHUNK)],
                                         xs_ref.at[pl.ds(dst, MOE_CHUNK)], sem.at[buf])
        return make_copy

    n = _chunk_issue(tab_ref, i, copy_from(slot))
    cnt_smem[slot] = n

    @pl.when(i > 0)
    def _():
        _chunk_drain(cnt_smem[1 - slot], copy_from(1 - slot))

    @pl.when(i == n_steps - 1)
    def _():
        _chunk_drain(n, copy_from(slot))


def _moe_dispatch(f, pos_cols, table, slots):
    t, d = f.shape
    tm = MOE_TILE
    n_slots, wd = slots.shape
    grid_spec = pltpu.PrefetchScalarGridSpec(
        num_scalar_prefetch=1,
        grid=(t // tm,),
        in_specs=[pl.BlockSpec((tm, TOP_K), lambda i, tab: (i, 0)),
                  pl.BlockSpec((tm, d), lambda i, tab: (i, 0)),
                  pl.BlockSpec(memory_space=pl.ANY)],
        out_specs=pl.BlockSpec(memory_space=pl.ANY),
        scratch_shapes=[pltpu.SMEM((2,), jnp.int32),
                        pltpu.VMEM((2, MOE_LB, wd), jnp.uint32),
                        pltpu.SemaphoreType.DMA((2,))],
    )
    return pl.pallas_call(
        functools.partial(_moe_dispatch_kernel, n_steps=t // tm),
        out_shape=jax.ShapeDtypeStruct((n_slots, wd), jnp.uint32),
        grid_spec=grid_spec,
        input_output_aliases={3: 0},
        compiler_params=pltpu.CompilerParams(
            dimension_semantics=("arbitrary",), vmem_limit_bytes=VMEM_LIMIT_BYTES),
        name="moe_dispatch",
    )(table, pos_cols, f, slots)


def _moe_combine_kernel(tab_ref, pos_ref, w_ref, y_ref, x_ref, gate_ref, fin_ref, o_ref, cnt_smem, yloc_ref,
                        sem, *, final_norm, n_steps):
    tm = x_ref.shape[0]
    i = pl.program_id(0)
    slot = i % 2

    def fetch(tile, buf):
        cnt_smem[buf] = _chunk_issue(tab_ref, tile, copy_into(buf))

    def copy_into(buf):
        def make_copy(j, src):
            return pltpu.make_async_copy(y_ref.at[pl.ds(src, MOE_CHUNK)],
                                         yloc_ref.at[buf, pl.ds(_chunk_row(j), MOE_CHUNK)], sem.at[buf])
        return make_copy

    @pl.when(i == 0)
    def _():
        yloc_ref[...] = jnp.zeros_like(yloc_ref)
        fetch(0, 0)

    @pl.when(i + 1 < n_steps)
    def _():
        fetch(i + 1, 1 - slot)

    _chunk_drain(cnt_smem[slot], copy_into(slot))
    r = lax.broadcasted_iota(jnp.int32, (tm, MOE_LB), 1)
    p = pos_ref[...]
    w = w_ref[...]
    wmat = jnp.where(p[:, 0:1] == r, w[:, 0:1], 0.0) + jnp.where(p[:, 1:2] == r, w[:, 1:2], 0.0)
    packed = yloc_ref[slot]
    y_lo = pltpu.bitcast(packed << 16, F32)
    y_hi = pltpu.bitcast(packed & jnp.uint32(0xFFFF0000), F32)
    moe = jnp.concatenate([_dotb(wmat, y_lo), _dotb(wmat, y_hi)], axis=-1)
    out = x_ref[...] + gate_ref[0] * moe
    if final_norm:
        out = out * lax.rsqrt(jnp.mean(out * out, axis=-1, keepdims=True) + EPS) * fin_ref[...]
    o_ref[...] = out


def _moe_combine(ys, pos_cols, weight_cols, table, xs, gate3, n_lat_tiles, blocks_per_batch, final_gain=None):
    t, d = xs.shape
    tm = MOE_TILE
    per_tile = tm // DN_BLOCK
    n_tiles = t // tm if final_gain is None else n_lat_tiles
    fin = jnp.ones((1, d), F32) if final_gain is None else final_gain.reshape(1, d)
    row = lambda i, tab: (i, 0)
    grp = lambda i, tab: (_group_of_block(i * per_tile, n_lat_tiles * per_tile, blocks_per_batch), 0, 0)
    grid_spec = pltpu.PrefetchScalarGridSpec(
        num_scalar_prefetch=1,
        grid=(n_tiles,),
        in_specs=[pl.BlockSpec((tm, TOP_K), row),
                  pl.BlockSpec((tm, TOP_K), row),
                  pl.BlockSpec(memory_space=pl.ANY),
                  pl.BlockSpec((tm, d), row),
                  pl.BlockSpec((1, 1, d), grp),
                  pl.BlockSpec((1, d), lambda i, tab: (0, 0))],
        out_specs=pl.BlockSpec((tm, d), row),
        scratch_shapes=[pltpu.SMEM((2,), jnp.int32),
                        pltpu.VMEM((2, MOE_LB, d // PACK), jnp.uint32),
                        pltpu.SemaphoreType.DMA((2,))],
    )
    return pl.pallas_call(
        functools.partial(_moe_combine_kernel, final_norm=final_gain is not None, n_steps=n_tiles),
        out_shape=jax.ShapeDtypeStruct((n_tiles * tm, d), F32),
        grid_spec=grid_spec,
        compiler_params=pltpu.CompilerParams(
            dimension_semantics=("arbitrary",), vmem_limit_bytes=VMEM_LIMIT_BYTES),
        name="moe_combine",
    )(table, pos_cols, weight_cols, ys, xs, gate3, fin)


def _moe_layer(xs, gain, shift3, scale3, gate3, w_router, router_bias, w_gate, w_up, w_down, layer,
               n_lat_tiles, blocks_per_batch, final_gain=None, slots=None):
    t = xs.shape[0]
    n_tiles = t // MOE_TILE
    f, pos, weight, cnt = _moe_route(xs, gain, shift3, scale3, w_router, router_bias,
                                     n_lat_tiles, blocks_per_batch)
    seg = (cnt[:, :, 0].astype(jnp.int32) + MOE_CHUNK - 1) // MOE_CHUNK * MOE_CHUNK
    loc_end = jnp.cumsum(seg, axis=1)
    loc_off = loc_end - seg
    padded = (jnp.sum(seg, axis=0) + MOE_BLOCK - 1) // MOE_BLOCK * MOE_BLOCK
    pend = jnp.cumsum(padded)
    seg_start = (pend - padded)[None, :] + jnp.cumsum(seg, axis=0) - seg
    n_blocks = -(-(t * TOP_K + n_tiles * N_EXPERTS * (MOE_CHUNK - 1)) // MOE_BLOCK) + N_EXPERTS
    block_start = jnp.arange(n_blocks, dtype=jnp.int32) * MOE_BLOCK
    block_expert = jnp.minimum(jnp.sum(pend[None, :] <= block_start[:, None], axis=1),
                               N_EXPERTS - 1).astype(jnp.int32)
    n_active = (pend[-1:] // MOE_BLOCK).astype(jnp.int32)
    row0 = jnp.arange(N_CHUNKS, dtype=jnp.int32) * MOE_CHUNK
    e_of = jnp.sum(loc_end[:, None, :] <= row0[None, :, None], axis=-1)
    is_e = e_of[..., None] == jnp.arange(N_EXPERTS, dtype=jnp.int32)
    shift = jnp.sum(jnp.where(is_e, (seg_start - loc_off)[:, None, :], 0), axis=-1)
    table = jnp.where(e_of < N_EXPERTS, row0[None, :] + shift, -1)
    table = jnp.pad(table, ((0, 0), (0, TAB_W - N_CHUNKS)), constant_values=-1)

    pos_cols = pos.T
    if slots is None:
        slots = jnp.zeros((n_blocks * MOE_BLOCK, f.shape[1] // PACK), jnp.uint32)
    xs_sorted = _moe_dispatch(f, pos_cols, table, slots)
    ys = _experts(xs_sorted, block_expert, n_active, w_gate, w_up, w_down, layer)
    out = _moe_combine(ys, pos_cols, weight.T, table, xs, gate3, n_lat_tiles, blocks_per_batch, final_gain)
    return out, xs_sorted


def _sc_layer_kernel(x_ref, gain_ref, shift_ref, scale_ref, win_ref, cw_ref, wout_ref, gate_ref, o_ref, *,
                     n_lat_tiles):
    d = D_MODEL
    tm = x_ref.shape[0]
    seg = jnp.where(pl.program_id(0) >= n_lat_tiles, CTX_LEN, GRID_W)
    pos = lax.broadcasted_iota(jnp.int32, (tm, 1), 0) & (seg - 1)
    hb = _modulated(x_ref[...], gain_ref[...], shift_ref[0], scale_ref[0]).astype(BF16)
    u = (jnp.dot(hb, win_ref[:, d:2 * d], preferred_element_type=F32)
         * jnp.dot(hb, win_ref[:, 2 * d:], preferred_element_type=F32))
    prev = jnp.where(pos != 0, pltpu.roll(u, 1, axis=0), 0.0)
    nxt = jnp.where(pos != seg - 1, pltpu.roll(u, tm - 1, axis=0), 0.0)
    cw = cw_ref[...]
    y = jnp.dot(hb, win_ref[:, :d], preferred_element_type=F32) * (
        cw[0:1] * prev + cw[1:2] * u + cw[2:3] * nxt)
    o_ref[...] = x_ref[...] + gate_ref[0] * jnp.dot(y.astype(BF16), wout_ref[...],
                                                    preferred_element_type=F32)


def _shortconv_layer(xs, gain, shift3, scale3, w_in, conv_w, w_out, gate3, n_lat_tiles, blocks_per_batch):
    t, d = xs.shape
    tm = HY_TOK_TILE
    per_tile = tm // DN_BLOCK
    row = lambda i: (i, 0)
    fixed = lambda i: (0, 0)
    grp = lambda i: (_group_of_block(i * per_tile, n_lat_tiles * per_tile, blocks_per_batch), 0, 0)
    return pl.pallas_call(
        functools.partial(_sc_layer_kernel, n_lat_tiles=n_lat_tiles),
        out_shape=jax.ShapeDtypeStruct((t, d), F32),
        grid=(t // tm,),
        in_specs=[pl.BlockSpec((tm, d), row),
                  pl.BlockSpec((1, d), fixed),
                  pl.BlockSpec((1, 1, d), grp),
                  pl.BlockSpec((1, 1, d), grp),
                  pl.BlockSpec((d, 3 * d), fixed),
                  pl.BlockSpec((3, d), fixed),
                  pl.BlockSpec((d, d), fixed),
                  pl.BlockSpec((1, 1, d), grp)],
        out_specs=pl.BlockSpec((tm, d), row),
        compiler_params=pltpu.CompilerParams(
            dimension_semantics=("arbitrary",), vmem_limit_bytes=VMEM_LIMIT_BYTES),
        name="shortconv_layer",
    )(xs, gain.reshape(1, d), shift3, scale3, w_in.astype(BF16), conv_w, w_out.astype(BF16), gate3)


def kernel(x, c, ctx, c_ctx, ada_w, ada_b, norm_mix, norm_ffn, norm_final, dn_w_in, dn_conv, dn_a_log,
           dn_dt_bias, dn_out_norm, dn_w_out, hy_w_in, hy_conv, hy_f_w1, hy_f_b1, hy_f_freq, hy_f_w2,
           hy_f_b2, hy_f_w3, hy_bias, hy_w_out, sc_w_in, sc_conv, sc_w_out, w_router, router_bias,
           moe_w_gate, moe_w_up, moe_w_down):
    d = D_MODEL
    bsz, seq, _ = x.shape
    n_ctx = bsz * CTX_LEN
    n_lat = bsz * seq
    silu_c = jax.nn.silu(c)
    silu_cc = jax.nn.silu(c_ctx)
    hp = lax.Precision.HIGHEST

    assert N_MIXERS > 0 and 0 % N_MIXERS == 0
    xs = (x.reshape(n_lat, d), ctx.reshape(n_ctx, d))
    n_lat_blocks, blocks_per_batch = n_lat // DN_BLOCK, seq // DN_BLOCK
    slots = None

    for i in range(DEPTH):
        kind, j = i % N_MIXERS, i // N_MIXERS
        ml = jnp.split(jnp.dot(silu_c, ada_w[i], precision=hp) + ada_b[i], N_MOD, axis=-1)
        mc = jnp.split(jnp.dot(silu_cc, ada_w[i], precision=hp) + ada_b[i], N_MOD, axis=-1)
        mod = [jnp.concatenate([mc[m][None], ml[m]], axis=0)[:, None, :] for m in range(N_MOD)]
        if kind == 0:
            xs = _deltanet_layer(xs, norm_mix[i], mod[0], mod[1], dn_w_in[j], dn_conv[j], dn_a_log[j],
                                 dn_dt_bias[j], dn_out_norm[j], dn_w_out[j], mod[2], n_lat_blocks,
                                 blocks_per_batch)
        elif kind == 1:
            xs = _hyena_layer(xs, norm_mix[i], mod[0], mod[1], hy_w_in[j], hy_conv[j], hy_f_w1[j], hy_f_b1[j],
                              hy_f_freq[j], hy_f_w2[j], hy_f_b2[j], hy_f_w3[j], hy_bias[j], hy_w_out[j],
                              mod[2], bsz, seq)
        else:
            xs = _shortconv_layer(xs, norm_mix[i], mod[0], mod[1], sc_w_in[j], sc_conv[j], sc_w_out[j],
                                  mod[2], n_lat // HY_TOK_TILE, blocks_per_batch)
        xs, slots = _moe_layer(xs, norm_ffn[i], mod[3], mod[4], mod[5], w_router, router_bias,
                               moe_w_gate, moe_w_up, moe_w_down, i, n_lat // MOE_TILE, blocks_per_batch,
                               norm_final if i == DEPTH - 1 else None, slots)
    return xs.reshape(bsz, seq, d)
```
